```python
import math
import jax, jax.numpy as jnp
from jax import lax
import numpy as np

D_MODEL = 1024
BATCH = 8
SEQ = 4096
DEPTH = 4

CHUNK = 64
MIX_W = 3 * D_MODEL // 4
N_BRANCH = 3
CONV_K = 4
HG_DK = 128
HG_DV = 128
HG_HEADS = MIX_W // HG_DV
F_MIN = 1e-30
SSM_HEADDIM = 64
SSM_HEADS = MIX_W // SSM_HEADDIM
SSM_STATE = 128
SSM_GROUPS = 2
SSM_CONV_CH = MIX_W + 2 * SSM_GROUPS * SSM_STATE
GDN_DK = 128
GDN_DV = 128
GDN_HEADS = MIX_W // GDN_DV
GDN_QKV = GDN_HEADS * (2 * GDN_DK + GDN_DV)
FFN_HIDDEN = (((8 * D_MODEL + 2) // 3 + 255) // 256) * 256
NORM_EPS = 1e-6

IN_SIZES = (
    HG_HEADS * HG_DK,
    HG_HEADS * HG_DK,
    HG_HEADS * HG_DV,
    HG_HEADS * HG_DV,
    MIX_W,
    SSM_CONV_CH,
    SSM_HEADS,
    GDN_QKV,
    GDN_HEADS * GDN_DV,
    GDN_HEADS,
    GDN_HEADS,
    N_BRANCH * D_MODEL,
)
IN_WIDTH = sum(IN_SIZES)
IN_SPLITS = tuple(int(v) for v in np.cumsum(IN_SIZES)[:-1])

kernel_name = 'hybrid_hgrn2_ssd_gdn_streaming_trunk'


def rms_norm(x, w):
    xf = x.astype(jnp.float32)
    y = xf * lax.rsqrt(jnp.mean(xf * xf, axis=-1, keepdims=True) + NORM_EPS)
    return y.astype(x.dtype) * w


def l2_normalize(x):
    xf = x.astype(jnp.float32)
    return (xf * lax.rsqrt(jnp.sum(xf * xf, axis=-1, keepdims=True) + NORM_EPS)).astype(x.dtype)


def causal_depthwise_conv(x, w, b=None):
    k, ch = w.shape
    y = lax.conv_general_dilated(x, w[:, None, :], window_strides=(1,), padding=((k - 1, 0),),
                                 dimension_numbers=('NWC', 'WIO', 'NWC'), feature_group_count=ch)
    return y if b is None else y + b


def to_chunks(t):
    bsz, s, h, d = t.shape
    return t.reshape(bsz, s // CHUNK, CHUNK, h, d).transpose(1, 0, 3, 2, 4)


def from_chunks(t):
    nc, bsz, h, c, d = t.shape
    return t.transpose(1, 0, 3, 2, 4).reshape(bsz, nc * c, h, d)


def causal_mask(strict=False):
    return jnp.tril(jnp.ones((CHUNK, CHUNK), dtype=bool), -1 if strict else 0)


def masked_exp(diff, mask):
    return jnp.where(mask, jnp.exp(jnp.where(mask, diff, 0.0)), 0.0)


def gla_chunked(q, k, v, logf):
    bsz, _, heads, dk = q.shape
    dv = v.shape[-1]
    mask = causal_mask()[:, :, None]

    def step(state, inp):
        q_c, k_c, v_c, b_c = inp
        o_inter = jnp.einsum('bhtk,bhkv->bhtv', q_c * jnp.exp(b_c), state)
        pair = masked_exp(b_c[:, :, :, None, :] - b_c[:, :, None, :, :], mask)
        scores = jnp.einsum('bhtk,bhsk,bhtsk->bhts', q_c, k_c, pair)
        o = o_inter + jnp.einsum('bhts,bhsv->bhtv', scores, v_c)
        b_end = b_c[:, :, -1:, :]
        new_state = (jnp.exp(b_end[:, :, 0, :, None]) * state
                     + jnp.einsum('bhsk,bhsv->bhkv', k_c * jnp.exp(b_end - b_c), v_c))
        return new_state, o

    b = jnp.cumsum(to_chunks(logf), axis=3)
    init = jnp.zeros((bsz, heads, dk, dv), v.dtype)
    _, o = lax.scan(step, init, (to_chunks(q), to_chunks(k), to_chunks(v), b))
    return from_chunks(o)


def ssd_chunked(cm, bm, xdt, loga):
    qc, kc, vc = to_chunks(cm), to_chunks(bm), to_chunks(xdt)
    cum = jnp.cumsum(to_chunks(loga[..., None])[..., 0], axis=-1)
    seg = masked_exp(cum[..., :, None] - cum[..., None, :], causal_mask())
    y_intra = jnp.einsum('nbhts,nbhsp->nbhtp', jnp.einsum('nbhtk,nbhsk->nbhts', qc, kc) * seg, vc)
    chunk_states = jnp.einsum('nbhsk,nbhsp->nbhkp', kc * jnp.exp(cum[..., -1:] - cum)[..., None], vc)
    chunk_decay = jnp.exp(cum[..., -1])

    def carry(state, inp):
        local, dec = inp
        return dec[..., None, None] * state + local, state

    init = jnp.zeros(chunk_states.shape[1:], chunk_states.dtype)
    _, starts = lax.scan(carry, init, (chunk_states, chunk_decay))
    y_inter = jnp.einsum('nbhtk,nbhkp->nbhtp', qc * jnp.exp(cum)[..., None], starts)
    return from_chunks(y_intra + y_inter)


def gated_delta_chunked(q, k, v, logg, beta):
    bsz, _, heads, dk = q.shape
    dv = v.shape[-1]
    qc, kc, vc = to_chunks(q), to_chunks(k), to_chunks(v)
    bc = to_chunks(beta[..., None])[..., 0]
    cum = jnp.cumsum(to_chunks(logg[..., None])[..., 0], axis=-1)
    decay = masked_exp(cum[..., :, None] - cum[..., None, :], causal_mask())
    kk = jnp.einsum('nbhtk,nbhsk->nbhts', kc, kc)
    a_low = jnp.where(causal_mask(strict=True), bc[..., :, None] * kk * decay, 0.0)
    rhs = jnp.concatenate([vc * bc[..., None], kc * (bc * jnp.exp(cum))[..., None]], axis=-1)
    eye = jnp.eye(CHUNK, dtype=jnp.float32)
    sol = lax.linalg.triangular_solve(eye + a_low.astype(jnp.float32), rhs.astype(jnp.float32),
                                      left_side=True, lower=True, unit_diagonal=True).astype(v.dtype)
    u_base, w_corr = sol[..., :dv], sol[..., dv:]
    qk = jnp.einsum('nbhtk,nbhsk->nbhts', qc, kc) * decay

    def step(state, inp):
        q_c, k_c, u_b, w_c, qk_c, cum_c = inp
        u = u_b - jnp.einsum('bhtk,bhkv->bhtv', w_c, state)
        o = (jnp.einsum('bhtk,bhkv->bhtv', q_c * jnp.exp(cum_c)[..., None], state)
             + jnp.einsum('bhts,bhsv->bhtv', qk_c, u))
        new_state = (jnp.exp(cum_c[..., -1])[..., None, None] * state
                     + jnp.einsum('bhsk,bhsv->bhkv', k_c * jnp.exp(cum_c[..., -1:] - cum_c)[..., None], u))
        return new_state, o

    init = jnp.zeros((bsz, heads, dk, dv), v.dtype)
    _, o = lax.scan(step, init, (qc, kc, u_base, w_corr, qk, cum))
    return from_chunks(o)


def hgrn2_branch(q_raw, f_raw, i_raw, g_raw, lower_bound, norm_w):
    bsz, s, _ = q_raw.shape
    q = jax.nn.silu(q_raw)
    f = lower_bound + (1 - lower_bound) * jax.nn.sigmoid(f_raw)
    logf = jnp.log(jnp.maximum(f, F_MIN))
    k = (1 - lower_bound) * jax.nn.sigmoid(-f_raw)
    o = gla_chunked(q.reshape(bsz, s, HG_HEADS, HG_DK), k.reshape(bsz, s, HG_HEADS, HG_DK),
                    i_raw.reshape(bsz, s, HG_HEADS, HG_DV), logf.reshape(bsz, s, HG_HEADS, HG_DK))
    o = rms_norm(o, norm_w) * jax.nn.silu(g_raw.reshape(bsz, s, HG_HEADS, HG_DV))
    return o.reshape(bsz, s, MIX_W)


def mamba2_branch(z, xbc, dt_raw, conv_w, conv_b, dt_bias, a_log, d_skip, norm_w):
    bsz, s, _ = z.shape
    xbc = jax.nn.silu(causal_depthwise_conv(xbc, conv_w, conv_b))
    xs, bm, cm = jnp.split(xbc, [MIX_W, MIX_W + SSM_GROUPS * SSM_STATE], axis=-1)
    xs = xs.reshape(bsz, s, SSM_HEADS, SSM_HEADDIM)
    rep = SSM_HEADS // SSM_GROUPS
    bm = jnp.repeat(bm.reshape(bsz, s, SSM_GROUPS, SSM_STATE), rep, axis=2)
    cm = jnp.repeat(cm.reshape(bsz, s, SSM_GROUPS, SSM_STATE), rep, axis=2)
    dt = jax.nn.softplus(dt_raw + dt_bias)
    loga = -jnp.exp(a_log) * dt
    y = ssd_chunked(cm, bm, xs * dt[..., None], loga) + d_skip[:, None] * xs
    y = y.reshape(bsz, s, MIX_W) * jax.nn.silu(z)
    gw = MIX_W // SSM_GROUPS
    y = rms_norm(y.reshape(bsz, s, SSM_GROUPS, gw), norm_w.reshape(SSM_GROUPS, gw))
    return y.reshape(bsz, s, MIX_W)


def gdn_branch(qkv, z, b_raw, a_raw, conv_w, dt_bias, a_log, norm_w):
    bsz, s, _ = z.shape
    qkv = jax.nn.silu(causal_depthwise_conv(qkv, conv_w))
    q, k, v = jnp.split(qkv, [GDN_HEADS * GDN_DK, 2 * GDN_HEADS * GDN_DK], axis=-1)
    q = l2_normalize(q.reshape(bsz, s, GDN_HEADS, GDN_DK)) * (GDN_DK ** -0.5)
    k = l2_normalize(k.reshape(bsz, s, GDN_HEADS, GDN_DK))
    v = v.reshape(bsz, s, GDN_HEADS, GDN_DV)
    beta = jax.nn.sigmoid(b_raw)
    logg = -jnp.exp(a_log) * jax.nn.softplus(a_raw + dt_bias)
    o = gated_delta_chunked(q, k, v, logg, beta)
    o = rms_norm(o, norm_w) * jax.nn.silu(z.reshape(bsz, s, GDN_HEADS, GDN_DV))
    return o.reshape(bsz, s, MIX_W)


def _fwd_setup_inputs(seed: int = 0) -> dict:
    key = jax.random.key(seed)
    ks = iter(jax.random.split(key, 40))

    def normal(shape, scale):
        return scale * jax.random.normal(next(ks), shape, jnp.float32)

    def uniform(shape, lo, hi):
        return jax.random.uniform(next(ks), shape, jnp.float32, lo, hi)

    def dt_bias_init(shape):
        dt = jnp.exp(uniform(shape, math.log(1e-3), math.log(1e-1)))
        return dt + jnp.log(-jnp.expm1(-dt))

    L = DEPTH
    return {
        'x': normal((BATCH, SEQ, D_MODEL), 1.0),
        'c': normal((BATCH, D_MODEL), 1.0),
        'w_ada': normal((L, D_MODEL, 6 * D_MODEL), 0.1 * D_MODEL ** -0.5),
        'b_ada': normal((L, 6 * D_MODEL), 0.01),
        'norm_mix': 1.0 + normal((L, D_MODEL), 0.02),
        'norm_ffn': 1.0 + normal((L, D_MODEL), 0.02),
        'w_in': normal((L, D_MODEL, IN_WIDTH), D_MODEL ** -0.5),
        'b_merge': normal((L, N_BRANCH * D_MODEL), 0.01),
        'hgrn_lb_logits': normal((L, HG_HEADS * HG_DK), 0.1),
        'hgrn_norm': 1.0 + normal((L, HG_DV), 0.02),
        'ssm_conv_w': normal((L, CONV_K, SSM_CONV_CH), CONV_K ** -0.5),
        'ssm_conv_b': normal((L, SSM_CONV_CH), 0.01),
        'ssm_dt_bias': dt_bias_init((L, SSM_HEADS)),
        'ssm_a_log': jnp.log(uniform((L, SSM_HEADS), 1.0, 16.0)),
        'ssm_d': 1.0 + normal((L, SSM_HEADS), 0.02),
        'ssm_norm': 1.0 + normal((L, MIX_W), 0.02),
        'gdn_conv_w': normal((L, CONV_K, GDN_QKV), CONV_K ** -0.5),
        'gdn_dt_bias': dt_bias_init((L, GDN_HEADS)),
        'gdn_a_log': jnp.log(uniform((L, GDN_HEADS), 1.0, 16.0)),
        'gdn_norm': 1.0 + normal((L, GDN_DV), 0.02),
        'w_branch': normal((L, N_BRANCH, MIX_W, D_MODEL), MIX_W ** -0.5),
        'w_out': normal((L, D_MODEL, D_MODEL), D_MODEL ** -0.5),
        'w_ffn_in': normal((L, D_MODEL, 2 * FFN_HIDDEN), D_MODEL ** -0.5),
        'w_ffn_out': normal((L, FFN_HIDDEN, D_MODEL), FFN_HIDDEN ** -0.5),
        'norm_final': 1.0 + normal((D_MODEL,), 0.02),
    }


def _fwd_reference(x, c, w_ada, b_ada, norm_mix, norm_ffn, w_in, b_merge, hgrn_lb_logits, hgrn_norm,
              ssm_conv_w, ssm_conv_b, ssm_dt_bias, ssm_a_log, ssm_d, ssm_norm,
              gdn_conv_w, gdn_dt_bias, gdn_a_log, gdn_norm,
              w_branch, w_out, w_ffn_in, w_ffn_out, norm_final):
    bsz, s, _ = x.shape
    p = jax.nn.softmax(hgrn_lb_logits.astype(jnp.float32), axis=0)
    lower_bounds = (jnp.cumsum(p, axis=0) - p[0]).astype(x.dtype)
    c_act = jax.nn.silu(c)
    for layer in range(DEPTH):
        mod = c_act @ w_ada[layer] + b_ada[layer]
        sh1, sc1, g1, sh2, sc2, g2 = [m[:, None, :] for m in jnp.split(mod, 6, axis=-1)]
        h = rms_norm(x, norm_mix[layer]) * (1 + sc1) + sh1
        (hq, hf, hi, hg, sz, sxbc, sdt, gqkv, gz, gb, ga, gate_logits) = jnp.split(
            h @ w_in[layer], IN_SPLITS, axis=-1)
        y_hgrn = hgrn2_branch(hq, hf, hi, hg, lower_bounds[layer], hgrn_norm[layer])
        y_ssm = mamba2_branch(sz, sxbc, sdt, ssm_conv_w[layer], ssm_conv_b[layer], ssm_dt_bias[layer],
                              ssm_a_log[layer], ssm_d[layer], ssm_norm[layer])
        y_gdn = gdn_branch(gqkv, gz, gb, ga, gdn_conv_w[layer], gdn_dt_bias[layer], gdn_a_log[layer],
                           gdn_norm[layer])
        branch_out = jnp.einsum('nbsw,nwd->nbsd', jnp.stack([y_hgrn, y_ssm, y_gdn], axis=0), w_branch[layer])
        gates = jax.nn.sigmoid(gate_logits + b_merge[layer]).reshape(bsz, s, N_BRANCH, D_MODEL)
        merged = jnp.einsum('bsnd,nbsd->bsd', gates, branch_out)
        x = x + (1 + g1) * (merged @ w_out[layer])
        h = rms_norm(x, norm_ffn[layer]) * (1 + sc2) + sh2
        gate, up = jnp.split(h @ w_ffn_in[layer], 2, axis=-1)
        x = x + (1 + g2) * ((jax.nn.silu(gate) * up) @ w_ffn_out[layer])
    return rms_norm(x, norm_final)


import jax as _jax
import jax.numpy as _jnp

TWIN_FORMAT = 'train_step'
FWD_PARAMS = ['x', 'c', 'w_ada', 'b_ada', 'norm_mix', 'norm_ffn', 'w_in', 'b_merge', 'hgrn_lb_logits', 'hgrn_norm', 'ssm_conv_w', 'ssm_conv_b', 'ssm_dt_bias', 'ssm_a_log', 'ssm_d', 'ssm_norm', 'gdn_conv_w', 'gdn_dt_bias', 'gdn_a_log', 'gdn_norm', 'w_branch', 'w_out', 'w_ffn_in', 'w_ffn_out', 'norm_final']
TWIN_WEIGHTS = ['w_ada', 'b_ada', 'norm_mix', 'norm_ffn', 'w_in', 'b_merge', 'hgrn_lb_logits', 'hgrn_norm', 'ssm_conv_w', 'ssm_conv_b', 'ssm_dt_bias', 'ssm_a_log', 'ssm_d', 'ssm_norm', 'gdn_conv_w', 'gdn_dt_bias', 'gdn_a_log', 'gdn_norm', 'w_branch', 'w_out', 'w_ffn_in', 'w_ffn_out', 'norm_final']
TWIN_DIFF_INPUT = 'x'
TWIN_INPUTS = ['x', 'c', 'w_ada', 'b_ada', 'norm_mix', 'norm_ffn', 'w_in', 'b_merge', 'hgrn_lb_logits', 'hgrn_norm', 'ssm_conv_w', 'ssm_conv_b', 'ssm_dt_bias', 'ssm_a_log', 'ssm_d', 'ssm_norm', 'gdn_conv_w', 'gdn_dt_bias', 'gdn_a_log', 'gdn_norm', 'w_branch', 'w_out', 'w_ffn_in', 'w_ffn_out', 'norm_final', 'loss_target', 'm_w_ada', 'm_b_ada', 'm_norm_mix', 'm_norm_ffn', 'm_w_in', 'm_b_merge', 'm_hgrn_lb_logits', 'm_hgrn_norm', 'm_ssm_conv_w', 'm_ssm_conv_b', 'm_ssm_dt_bias', 'm_ssm_a_log', 'm_ssm_d', 'm_ssm_norm', 'm_gdn_conv_w', 'm_gdn_dt_bias', 'm_gdn_a_log', 'm_gdn_norm', 'm_w_branch', 'm_w_out', 'm_w_ffn_in', 'm_w_ffn_out', 'm_norm_final', 'v_w_ada', 'v_b_ada', 'v_norm_mix', 'v_norm_ffn', 'v_w_in', 'v_b_merge', 'v_hgrn_lb_logits', 'v_hgrn_norm', 'v_ssm_conv_w', 'v_ssm_conv_b', 'v_ssm_dt_bias', 'v_ssm_a_log', 'v_ssm_d', 'v_ssm_norm', 'v_gdn_conv_w', 'v_gdn_dt_bias', 'v_gdn_a_log', 'v_gdn_norm', 'v_w_branch', 'v_w_out', 'v_w_ffn_in', 'v_w_ffn_out', 'v_norm_final']
TWIN_OUTPUTS = ['loss', 'grad_x', 'grad_w_ada', 'grad_b_ada', 'grad_norm_mix', 'grad_norm_ffn', 'grad_w_in', 'grad_b_merge', 'grad_hgrn_lb_logits', 'grad_hgrn_norm', 'grad_ssm_conv_w', 'grad_ssm_conv_b', 'grad_ssm_dt_bias', 'grad_ssm_a_log', 'grad_ssm_d', 'grad_ssm_norm', 'grad_gdn_conv_w', 'grad_gdn_dt_bias', 'grad_gdn_a_log', 'grad_gdn_norm', 'grad_w_branch', 'grad_w_out', 'grad_w_ffn_in', 'grad_w_ffn_out', 'grad_norm_final', 'delta_w_ada', 'delta_b_ada', 'delta_norm_mix', 'delta_norm_ffn', 'delta_w_in', 'delta_b_merge', 'delta_hgrn_lb_logits', 'delta_hgrn_norm', 'delta_ssm_conv_w', 'delta_ssm_conv_b', 'delta_ssm_dt_bias', 'delta_ssm_a_log', 'delta_ssm_d', 'delta_ssm_norm', 'delta_gdn_conv_w', 'delta_gdn_dt_bias', 'delta_gdn_a_log', 'delta_gdn_norm', 'delta_w_branch', 'delta_w_out', 'delta_w_ffn_in', 'delta_w_ffn_out', 'delta_norm_final', 'new_m_w_ada', 'new_m_b_ada', 'new_m_norm_mix', 'new_m_norm_ffn', 'new_m_w_in', 'new_m_b_merge', 'new_m_hgrn_lb_logits', 'new_m_hgrn_norm', 'new_m_ssm_conv_w', 'new_m_ssm_conv_b', 'new_m_ssm_dt_bias', 'new_m_ssm_a_log', 'new_m_ssm_d', 'new_m_ssm_norm', 'new_m_gdn_conv_w', 'new_m_gdn_dt_bias', 'new_m_gdn_a_log', 'new_m_gdn_norm', 'new_m_w_branch', 'new_m_w_out', 'new_m_w_ffn_in', 'new_m_w_ffn_out', 'new_m_norm_final', 'new_v_w_ada', 'new_v_b_ada', 'new_v_norm_mix', 'new_v_norm_ffn', 'new_v_w_in', 'new_v_b_merge', 'new_v_hgrn_lb_logits', 'new_v_hgrn_norm', 'new_v_ssm_conv_w', 'new_v_ssm_conv_b', 'new_v_ssm_dt_bias', 'new_v_ssm_a_log', 'new_v_ssm_d', 'new_v_ssm_norm', 'new_v_gdn_conv_w', 'new_v_gdn_dt_bias', 'new_v_gdn_a_log', 'new_v_gdn_norm', 'new_v_w_branch', 'new_v_w_out', 'new_v_w_ffn_in', 'new_v_w_ffn_out', 'new_v_norm_final']
TWIN_LEAF_KINDS = {'loss': 'loss', 'grad_x': 'grad_x', 'grad_w_ada': 'grad_w', 'grad_b_ada': 'grad_w', 'grad_norm_mix': 'grad_w', 'grad_norm_ffn': 'grad_w', 'grad_w_in': 'grad_w', 'grad_b_merge': 'grad_w', 'grad_hgrn_lb_logits': 'grad_w', 'grad_hgrn_norm': 'grad_w', 'grad_ssm_conv_w': 'grad_w', 'grad_ssm_conv_b': 'grad_w', 'grad_ssm_dt_bias': 'grad_w', 'grad_ssm_a_log': 'grad_w', 'grad_ssm_d': 'grad_w', 'grad_ssm_norm': 'grad_w', 'grad_gdn_conv_w': 'grad_w', 'grad_gdn_dt_bias': 'grad_w', 'grad_gdn_a_log': 'grad_w', 'grad_gdn_norm': 'grad_w', 'grad_w_branch': 'grad_w', 'grad_w_out': 'grad_w', 'grad_w_ffn_in': 'grad_w', 'grad_w_ffn_out': 'grad_w', 'grad_norm_final': 'grad_w', 'delta_w_ada': 'delta_w', 'delta_b_ada': 'delta_w', 'delta_norm_mix': 'delta_w', 'delta_norm_ffn': 'delta_w', 'delta_w_in': 'delta_w', 'delta_b_merge': 'delta_w', 'delta_hgrn_lb_logits': 'delta_w', 'delta_hgrn_norm': 'delta_w', 'delta_ssm_conv_w': 'delta_w', 'delta_ssm_conv_b': 'delta_w', 'delta_ssm_dt_bias': 'delta_w', 'delta_ssm_a_log': 'delta_w', 'delta_ssm_d': 'delta_w', 'delta_ssm_norm': 'delta_w', 'delta_gdn_conv_w': 'delta_w', 'delta_gdn_dt_bias': 'delta_w', 'delta_gdn_a_log': 'delta_w', 'delta_gdn_norm': 'delta_w', 'delta_w_branch': 'delta_w', 'delta_w_out': 'delta_w', 'delta_w_ffn_in': 'delta_w', 'delta_w_ffn_out': 'delta_w', 'delta_norm_final': 'delta_w', 'new_m_w_ada': 'new_m', 'new_m_b_ada': 'new_m', 'new_m_norm_mix': 'new_m', 'new_m_norm_ffn': 'new_m', 'new_m_w_in': 'new_m', 'new_m_b_merge': 'new_m', 'new_m_hgrn_lb_logits': 'new_m', 'new_m_hgrn_norm': 'new_m', 'new_m_ssm_conv_w': 'new_m', 'new_m_ssm_conv_b': 'new_m', 'new_m_ssm_dt_bias': 'new_m', 'new_m_ssm_a_log': 'new_m', 'new_m_ssm_d': 'new_m', 'new_m_ssm_norm': 'new_m', 'new_m_gdn_conv_w': 'new_m', 'new_m_gdn_dt_bias': 'new_m', 'new_m_gdn_a_log': 'new_m', 'new_m_gdn_norm': 'new_m', 'new_m_w_branch': 'new_m', 'new_m_w_out': 'new_m', 'new_m_w_ffn_in': 'new_m', 'new_m_w_ffn_out': 'new_m', 'new_m_norm_final': 'new_m', 'new_v_w_ada': 'new_v', 'new_v_b_ada': 'new_v', 'new_v_norm_mix': 'new_v', 'new_v_norm_ffn': 'new_v', 'new_v_w_in': 'new_v', 'new_v_b_merge': 'new_v', 'new_v_hgrn_lb_logits': 'new_v', 'new_v_hgrn_norm': 'new_v', 'new_v_ssm_conv_w': 'new_v', 'new_v_ssm_conv_b': 'new_v', 'new_v_ssm_dt_bias': 'new_v', 'new_v_ssm_a_log': 'new_v', 'new_v_ssm_d': 'new_v', 'new_v_ssm_norm': 'new_v', 'new_v_gdn_conv_w': 'new_v', 'new_v_gdn_dt_bias': 'new_v', 'new_v_gdn_a_log': 'new_v', 'new_v_gdn_norm': 'new_v', 'new_v_w_branch': 'new_v', 'new_v_w_out': 'new_v', 'new_v_w_ffn_in': 'new_v', 'new_v_w_ffn_out': 'new_v', 'new_v_norm_final': 'new_v'}


def _forward(args):
    return _fwd_reference(*[args[k] for k in FWD_PARAMS])


def _output_shape():
    out = _jax.eval_shape(lambda: _forward(_fwd_setup_inputs(0)))
    return out.shape, out.dtype

N_MICROBATCH = 1
ADAM_LR = 0.001
ADAM_B1 = 0.9
ADAM_B2 = 0.999
ADAM_EPS = 1e-08
ADAM_WD = 0.01
ADAM_STEP = 10
PER_EXAMPLE_BATCH_AXIS = {'x': 0, 'c': 0, 'loss_target': 0}
SHARED_INPUTS = []
_WEIGHT_DTYPES = {'w_ada': _jnp.float32, 'b_ada': _jnp.float32, 'norm_mix': _jnp.float32, 'norm_ffn': _jnp.float32, 'w_in': _jnp.float32, 'b_merge': _jnp.float32, 'hgrn_lb_logits': _jnp.float32, 'hgrn_norm': _jnp.float32, 'ssm_conv_w': _jnp.float32, 'ssm_conv_b': _jnp.float32, 'ssm_dt_bias': _jnp.float32, 'ssm_a_log': _jnp.float32, 'ssm_d': _jnp.float32, 'ssm_norm': _jnp.float32, 'gdn_conv_w': _jnp.float32, 'gdn_dt_bias': _jnp.float32, 'gdn_a_log': _jnp.float32, 'gdn_norm': _jnp.float32, 'w_branch': _jnp.float32, 'w_out': _jnp.float32, 'w_ffn_in': _jnp.float32, 'w_ffn_out': _jnp.float32, 'norm_final': _jnp.float32}
MOMENT_SCALE = {'w_ada': 9.874010e-02, 'b_ada': 1.622246e-01, 'norm_mix': 1.930271e-01, 'norm_ffn': 1.226247e-01, 'w_in': 5.733375e-02, 'b_merge': 2.755267e-02, 'hgrn_lb_logits': 3.635036e-03, 'hgrn_norm': 1.563921e-01, 'ssm_conv_w': 8.888824e-02, 'ssm_conv_b': 1.337139e-01, 'ssm_dt_bias': 2.110786e-01, 'ssm_a_log': 4.936626e-01, 'ssm_d': 7.218538e-01, 'ssm_norm': 1.113652e-01, 'gdn_conv_w': 4.819280e-02, 'gdn_dt_bias': 6.186268e-01, 'gdn_a_log': 6.502706e-01, 'gdn_norm': 1.669809e-01, 'w_branch': 6.943514e-02, 'w_out': 1.201461e-01, 'w_ffn_in': 5.308060e-02, 'w_ffn_out': 8.665950e-02, 'norm_final': 3.201502e+01}


def _to_microbatches(a, axis):
    t = _jnp.moveaxis(a, axis, 0)
    t = t.reshape((N_MICROBATCH, t.shape[0] // N_MICROBATCH) + t.shape[1:])
    return _jnp.moveaxis(t, 1, axis + 1)


def setup_inputs(seed: int = 0) -> dict:
    inp = _fwd_setup_inputs(seed)
    key = _jax.random.fold_in(_jax.random.key(seed), 7919)
    shape, _ = _output_shape()
    out = dict(inp)
    out["loss_target"] = _jax.random.normal(_jax.random.fold_in(key, 0), shape, _jnp.float32)
    for i, name in enumerate(TWIN_WEIGHTS):
        w = inp[name].astype(_jnp.float32)
        if MOMENT_SCALE is None:
            s = _jnp.sqrt(_jnp.mean(_jnp.square(w)) + 1e-30)
        else:
            s = MOMENT_SCALE[name]
        km, kv = _jax.random.split(_jax.random.fold_in(key, i + 1))
        out[name] = w
        out["m_" + name] = s * _jax.random.normal(km, w.shape, _jnp.float32)
        out["v_" + name] = (s * s) * _jax.random.uniform(kv, w.shape, _jnp.float32, 0.5, 1.5)
    if N_MICROBATCH > 1:
        for name, axis in PER_EXAMPLE_BATCH_AXIS.items():
            out[name] = _to_microbatches(out[name], axis)
    return {'x': out['x'], 'c': out['c'], 'w_ada': out['w_ada'], 'b_ada': out['b_ada'], 'norm_mix': out['norm_mix'], 'norm_ffn': out['norm_ffn'], 'w_in': out['w_in'], 'b_merge': out['b_merge'], 'hgrn_lb_logits': out['hgrn_lb_logits'], 'hgrn_norm': out['hgrn_norm'], 'ssm_conv_w': out['ssm_conv_w'], 'ssm_conv_b': out['ssm_conv_b'], 'ssm_dt_bias': out['ssm_dt_bias'], 'ssm_a_log': out['ssm_a_log'], 'ssm_d': out['ssm_d'], 'ssm_norm': out['ssm_norm'], 'gdn_conv_w': out['gdn_conv_w'], 'gdn_dt_bias': out['gdn_dt_bias'], 'gdn_a_log': out['gdn_a_log'], 'gdn_norm': out['gdn_norm'], 'w_branch': out['w_branch'], 'w_out': out['w_out'], 'w_ffn_in': out['w_ffn_in'], 'w_ffn_out': out['w_ffn_out'], 'norm_final': out['norm_final'], 'loss_target': out['loss_target'], 'm_w_ada': out['m_w_ada'], 'm_b_ada': out['m_b_ada'], 'm_norm_mix': out['m_norm_mix'], 'm_norm_ffn': out['m_norm_ffn'], 'm_w_in': out['m_w_in'], 'm_b_merge': out['m_b_merge'], 'm_hgrn_lb_logits': out['m_hgrn_lb_logits'], 'm_hgrn_norm': out['m_hgrn_norm'], 'm_ssm_conv_w': out['m_ssm_conv_w'], 'm_ssm_conv_b': out['m_ssm_conv_b'], 'm_ssm_dt_bias': out['m_ssm_dt_bias'], 'm_ssm_a_log': out['m_ssm_a_log'], 'm_ssm_d': out['m_ssm_d'], 'm_ssm_norm': out['m_ssm_norm'], 'm_gdn_conv_w': out['m_gdn_conv_w'], 'm_gdn_dt_bias': out['m_gdn_dt_bias'], 'm_gdn_a_log': out['m_gdn_a_log'], 'm_gdn_norm': out['m_gdn_norm'], 'm_w_branch': out['m_w_branch'], 'm_w_out': out['m_w_out'], 'm_w_ffn_in': out['m_w_ffn_in'], 'm_w_ffn_out': out['m_w_ffn_out'], 'm_norm_final': out['m_norm_final'], 'v_w_ada': out['v_w_ada'], 'v_b_ada': out['v_b_ada'], 'v_norm_mix': out['v_norm_mix'], 'v_norm_ffn': out['v_norm_ffn'], 'v_w_in': out['v_w_in'], 'v_b_merge': out['v_b_merge'], 'v_hgrn_lb_logits': out['v_hgrn_lb_logits'], 'v_hgrn_norm': out['v_hgrn_norm'], 'v_ssm_conv_w': out['v_ssm_conv_w'], 'v_ssm_conv_b': out['v_ssm_conv_b'], 'v_ssm_dt_bias': out['v_ssm_dt_bias'], 'v_ssm_a_log': out['v_ssm_a_log'], 'v_ssm_d': out['v_ssm_d'], 'v_ssm_norm': out['v_ssm_norm'], 'v_gdn_conv_w': out['v_gdn_conv_w'], 'v_gdn_dt_bias': out['v_gdn_dt_bias'], 'v_gdn_a_log': out['v_gdn_a_log'], 'v_gdn_norm': out['v_gdn_norm'], 'v_w_branch': out['v_w_branch'], 'v_w_out': out['v_w_out'], 'v_w_ffn_in': out['v_w_ffn_in'], 'v_w_ffn_out': out['v_w_ffn_out'], 'v_norm_final': out['v_norm_final']}


def _loss(weights, diff, rest, loss_target):
    with _jax.named_scope("forward"):
        args = {**rest, TWIN_DIFF_INPUT: diff, **{k: w.astype(_WEIGHT_DTYPES[k]) for k, w in weights.items()}}
        y = _forward(args)
    with _jax.named_scope("loss_head"):
        err = _jnp.square(y.astype(_jnp.float32) - loss_target)
        return 0.5 * _jnp.sum(_jnp.mean(err, axis=-1)) if err.ndim else 0.5 * err


def _adamw(w, g, m, v):
    m = ADAM_B1 * m + (1.0 - ADAM_B1) * g
    v = ADAM_B2 * v + (1.0 - ADAM_B2) * _jnp.square(g)
    m_hat = m / (1.0 - ADAM_B1 ** ADAM_STEP)
    v_hat = v / (1.0 - ADAM_B2 ** ADAM_STEP)
    delta = -ADAM_LR * (m_hat / (_jnp.sqrt(v_hat) + ADAM_EPS) + ADAM_WD * w)
    return delta, m, v


def reference(x, c, w_ada, b_ada, norm_mix, norm_ffn, w_in, b_merge, hgrn_lb_logits, hgrn_norm, ssm_conv_w, ssm_conv_b, ssm_dt_bias, ssm_a_log, ssm_d, ssm_norm, gdn_conv_w, gdn_dt_bias, gdn_a_log, gdn_norm, w_branch, w_out, w_ffn_in, w_ffn_out, norm_final, loss_target, m_w_ada, m_b_ada, m_norm_mix, m_norm_ffn, m_w_in, m_b_merge, m_hgrn_lb_logits, m_hgrn_norm, m_ssm_conv_w, m_ssm_conv_b, m_ssm_dt_bias, m_ssm_a_log, m_ssm_d, m_ssm_norm, m_gdn_conv_w, m_gdn_dt_bias, m_gdn_a_log, m_gdn_norm, m_w_branch, m_w_out, m_w_ffn_in, m_w_ffn_out, m_norm_final, v_w_ada, v_b_ada, v_norm_mix, v_norm_ffn, v_w_in, v_b_merge, v_hgrn_lb_logits, v_hgrn_norm, v_ssm_conv_w, v_ssm_conv_b, v_ssm_dt_bias, v_ssm_a_log, v_ssm_d, v_ssm_norm, v_gdn_conv_w, v_gdn_dt_bias, v_gdn_a_log, v_gdn_norm, v_w_branch, v_w_out, v_w_ffn_in, v_w_ffn_out, v_norm_final):
    given = dict(x=x, c=c, w_ada=w_ada, b_ada=b_ada, norm_mix=norm_mix, norm_ffn=norm_ffn, w_in=w_in, b_merge=b_merge, hgrn_lb_logits=hgrn_lb_logits, hgrn_norm=hgrn_norm, ssm_conv_w=ssm_conv_w, ssm_conv_b=ssm_conv_b, ssm_dt_bias=ssm_dt_bias, ssm_a_log=ssm_a_log, ssm_d=ssm_d, ssm_norm=ssm_norm, gdn_conv_w=gdn_conv_w, gdn_dt_bias=gdn_dt_bias, gdn_a_log=gdn_a_log, gdn_norm=gdn_norm, w_branch=w_branch, w_out=w_out, w_ffn_in=w_ffn_in, w_ffn_out=w_ffn_out, norm_final=norm_final, loss_target=loss_target, m_w_ada=m_w_ada, m_b_ada=m_b_ada, m_norm_mix=m_norm_mix, m_norm_ffn=m_norm_ffn, m_w_in=m_w_in, m_b_merge=m_b_merge, m_hgrn_lb_logits=m_hgrn_lb_logits, m_hgrn_norm=m_hgrn_norm, m_ssm_conv_w=m_ssm_conv_w, m_ssm_conv_b=m_ssm_conv_b, m_ssm_dt_bias=m_ssm_dt_bias, m_ssm_a_log=m_ssm_a_log, m_ssm_d=m_ssm_d, m_ssm_norm=m_ssm_norm, m_gdn_conv_w=m_gdn_conv_w, m_gdn_dt_bias=m_gdn_dt_bias, m_gdn_a_log=m_gdn_a_log, m_gdn_norm=m_gdn_norm, m_w_branch=m_w_branch, m_w_out=m_w_out, m_w_ffn_in=m_w_ffn_in, m_w_ffn_out=m_w_ffn_out, m_norm_final=m_norm_final, v_w_ada=v_w_ada, v_b_ada=v_b_ada, v_norm_mix=v_norm_mix, v_norm_ffn=v_norm_ffn, v_w_in=v_w_in, v_b_merge=v_b_merge, v_hgrn_lb_logits=v_hgrn_lb_logits, v_hgrn_norm=v_hgrn_norm, v_ssm_conv_w=v_ssm_conv_w, v_ssm_conv_b=v_ssm_conv_b, v_ssm_dt_bias=v_ssm_dt_bias, v_ssm_a_log=v_ssm_a_log, v_ssm_d=v_ssm_d, v_ssm_norm=v_ssm_norm, v_gdn_conv_w=v_gdn_conv_w, v_gdn_dt_bias=v_gdn_dt_bias, v_gdn_a_log=v_gdn_a_log, v_gdn_norm=v_gdn_norm, v_w_branch=v_w_branch, v_w_out=v_w_out, v_w_ffn_in=v_w_ffn_in, v_w_ffn_out=v_w_ffn_out, v_norm_final=v_norm_final)
    weights = {n: given[n] for n in TWIN_WEIGHTS}
    shared = {n: given[n] for n in SHARED_INPUTS}
    per_example = {n: given[n] for n in ['x', 'c']}
    grad_fn = _jax.value_and_grad(_loss, argnums=(0, 1))

    def one_microbatch(ex, loss_target):
        ex = dict(ex)
        diff = ex.pop(TWIN_DIFF_INPUT)
        return grad_fn(weights, diff, {**shared, **ex}, loss_target)

    if N_MICROBATCH == 1:
        loss, (grad_w, grad_x) = one_microbatch(per_example, given["loss_target"])
    else:
        def body(carry, xs):
            loss_sum, grad_sum = carry
            l_k, (gw_k, gx_k) = one_microbatch(xs[0], xs[1])
            with _jax.named_scope("update"):
                return (loss_sum + l_k, _jax.tree.map(_jnp.add, grad_sum, gw_k)), gx_k

        init = (_jnp.zeros((), _jnp.float32), _jax.tree.map(_jnp.zeros_like, weights))
        (loss, grad_w), grad_x = _jax.lax.scan(body, init, (per_example, given["loss_target"]))
    with _jax.named_scope("update"):
        delta_w, new_m, new_v = {}, {}, {}
        for n in TWIN_WEIGHTS:
            delta_w[n], new_m[n], new_v[n] = _adamw(weights[n], grad_w[n], given["m_" + n], given["v_" + n])
    return (loss, grad_x, *[grad_w[n] for n in TWIN_WEIGHTS], *[delta_w[n] for n in TWIN_WEIGHTS],
            *[new_m[n] for n in TWIN_WEIGHTS], *[new_v[n] for n in TWIN_WEIGHTS])
```

```python
import functools
import math

import jax
import jax.numpy as jnp
from jax import lax
from jax.experimental import pallas as pl
from jax.experimental.pallas import tpu as pltpu

F32 = jnp.float32
BF = jnp.bfloat16
HI = lax.Precision.HIGHEST

D_MODEL = 1024
DEPTH = 4
CHUNK = 64
MIX_W = 768
HEAD = 128
N_HEAD6 = 6
SSM_P = 64
SSM_N = 128
CONV_CH = 1280
QKV_W = 2304
FFN_H = 2816
IN_WIDTH = 11288
NORM_EPS = 1e-6
F_MIN = 1e-30
HALO = 8
SMALL_W = 512
LANES = 128
DT_OFF, GB_OFF, GA_OFF = 0, 12, 18

ADAM_LR, ADAM_B1, ADAM_B2, ADAM_EPS, ADAM_WD, ADAM_STEP = 0.001, 0.9, 0.999, 1e-08, 0.01, 10

VMEM_LIMIT = 56 * 1024 * 1024
TOKEN_TILE = 256


def _pc(body, **kw):
    return pl.pallas_call(body, **kw)


def _cparams(sem):
    return pltpu.CompilerParams(dimension_semantics=sem, vmem_limit_bytes=VMEM_LIMIT)


def _bdot(a, b):
    return jnp.dot(a.astype(BF), b.astype(BF), preferred_element_type=F32)


def _bdot_nt(a, b):
    return lax.dot_general(a.astype(BF), b.astype(BF), (((1,), (1,)), ((), ())), preferred_element_type=F32)


def _bdot_tn(a, b):
    return lax.dot_general(a.astype(BF), b.astype(BF), (((0,), (0,)), ((), ())), preferred_element_type=F32)


def _silu(x):
    return x * jax.nn.sigmoid(x)


def _tri_mask(n, strict=False):
    t = lax.broadcasted_iota(jnp.int32, (n, n), 0)
    s = lax.broadcasted_iota(jnp.int32, (n, n), 1)
    return (s < t) if strict else (s <= t)


def _masked_exp(diff, mask):
    return jnp.where(mask, jnp.exp(jnp.where(mask, diff, 0.0)), 0.0)


def _cumsum_rows(x):
    tri = jnp.where(_tri_mask(x.shape[0]), 1.0, 0.0).astype(F32)
    return jnp.dot(tri, x, preferred_element_type=F32, precision=HI)


def _rms(x, w):
    return x * lax.rsqrt(jnp.mean(x * x, axis=-1, keepdims=True) + NORM_EPS) * w


def _causal_conv(halo, x, w):
    ext = jnp.concatenate([halo, x], axis=0)
    n = x.shape[0]
    acc = w[0:1, :] * ext[HALO - 3:HALO - 3 + n, :]
    for i in range(1, 4):
        acc = acc + w[i:i + 1, :] * ext[HALO - 3 + i:HALO - 3 + i + n, :]
    return acc


@jax.custom_vjp
def _unit_lower_solve(a, r):
    x = r
    for s in range(CHUNK - 1):
        x = x - a[:, s:s + 1] * x[s:s + 1, :]
    return x


def _uls_fwd(a, r):
    x = _unit_lower_solve(a, r)
    return x, (a, x)


def _uls_bwd(res, g):
    a, x = res
    at = a.T
    y = g
    for s in range(CHUNK - 1, 0, -1):
        y = y - at[:, s:s + 1] * y[s:s + 1, :]
    da = jnp.where(_tri_mask(CHUNK, strict=True), -_bdot_nt(y, x), 0.0)
    return da, y


_unit_lower_solve.defvjp(_uls_fwd, _uls_bwd)


def _hgrn_chunk(tiles, halos, state, consts):
    q_raw, f_raw, v_all, g_raw = tiles
    lb, norm_w = consts
    q_all = _silu(q_raw)
    f = lb + (1.0 - lb) * jax.nn.sigmoid(f_raw)
    logf = jnp.log(jnp.maximum(f, F_MIN))
    k_all = (1.0 - lb) * jax.nn.sigmoid(-f_raw)
    b_all = _cumsum_rows(logf)
    sub = 16
    row = lax.broadcasted_iota(jnp.int32, (sub, 1), 0)
    outs, new_states = [], []
    for h in range(N_HEAD6):
        sl = slice(h * HEAD, (h + 1) * HEAD)
        q, k, v, b = q_all[:, sl], k_all[:, sl], v_all[:, sl], b_all[:, sl]
        st = state[h]
        o_inter = _bdot_nt(q * jnp.exp(b), st)
        blocks = []
        for i in range(CHUNK // sub):
            r0 = i * sub
            qi, ki, vi, bi = q[r0:r0 + sub], k[r0:r0 + sub], v[r0:r0 + sub], b[r0:r0 + sub]
            if i > 0:
                ref = b[r0 - 1:r0, :]
                a_off = _bdot_nt(qi * jnp.exp(bi - ref), k[0:r0] * jnp.exp(ref - b[0:r0]))
                oi = _bdot(a_off, v[0:r0])
            else:
                oi = jnp.zeros((sub, HEAD), F32)
            for s in range(sub):
                m = row >= s
                e = _masked_exp(bi - bi[s:s + 1, :], m)
                col = jnp.sum(qi * ki[s:s + 1, :] * e, axis=1, keepdims=True)
                oi = oi + col * vi[s:s + 1, :]
            blocks.append(oi)
        o = jnp.concatenate(blocks, axis=0) + o_inter
        b_end = b[CHUNK - 1:CHUNK, :]
        new_states.append(st * jnp.exp(b_end) + _bdot_tn(v, k * jnp.exp(b_end - b)))
        outs.append(_rms(o, norm_w) * _silu(g_raw[:, sl]))
    return (jnp.concatenate(outs, axis=1),), jnp.stack(new_states)


def _ssd_chunk(tiles, halos, state, consts):
    z, xbc_raw, small = tiles
    (halo,) = halos
    conv_w, conv_b, dt_bias, a_log, d_skip, norm_w = consts
    xbc = _silu(_causal_conv(halo, xbc_raw, conv_w) + conv_b)
    xs, bm, cm = xbc[:, :MIX_W], xbc[:, MIX_W:MIX_W + 2 * SSM_N], xbc[:, MIX_W + 2 * SSM_N:]
    dt = jax.nn.softplus(small + dt_bias)
    cum = _cumsum_rows(-jnp.exp(a_log) * dt)
    cum_t = cum.T
    lane = lax.broadcasted_iota(jnp.int32, (1, LANES), 1)
    tri = _tri_mask(CHUNK)
    ys, new_states = [], []
    gmats = []
    for g in range(2):
        gmats.append(_bdot_nt(cm[:, g * SSM_N:(g + 1) * SSM_N], bm[:, g * SSM_N:(g + 1) * SSM_N]))
    for p in range(6):
        g = p // 3
        bg, cg = bm[:, g * SSM_N:(g + 1) * SSM_N], cm[:, g * SSM_N:(g + 1) * SSM_N]
        xp = xs[:, p * LANES:(p + 1) * LANES]
        sp = state[p]
        yp = jnp.zeros((CHUNK, LANES), F32)
        sn = jnp.zeros((SSM_N, LANES), F32)
        decay = jnp.zeros((1, LANES), F32)
        for j in range(2):
            h = 2 * p + j
            hm = jnp.where((lane >= j * SSM_P) & (lane < (j + 1) * SSM_P), 1.0, 0.0).astype(F32)
            ch, ch_t, ce = cum[:, h:h + 1], cum_t[h:h + 1, :], cum[CHUNK - 1:CHUNK, h:h + 1]
            seg = _masked_exp(ch - ch_t, tri)
            vm = xp * dt[:, h:h + 1] * hm
            yp = yp + _bdot(gmats[g] * seg, vm) + _bdot(cg * jnp.exp(ch), sp * hm) + d_skip[:, h:h + 1] * xp * hm
            sn = sn + _bdot_tn(bg * jnp.exp(ce - ch), vm)
            decay = decay + jnp.exp(ce) * hm
        ys.append(yp)
        new_states.append(sn + sp * decay)
    y = jnp.concatenate(ys, axis=1) * _silu(z)
    gw = MIX_W // 2
    y = jnp.concatenate([_rms(y[:, g * gw:(g + 1) * gw], norm_w[:, g * gw:(g + 1) * gw]) for g in range(2)], axis=1)
    return (y,), jnp.stack(new_states)


def _gdn_chunk(tiles, halos, state, consts):
    qkv_raw, z, small = tiles
    (halo,) = halos
    conv_w, dt_bias, a_log, norm_w = consts
    qkv = _silu(_causal_conv(halo, qkv_raw, conv_w))
    beta_all = jax.nn.sigmoid(small)
    cum = _cumsum_rows(-jnp.exp(a_log) * jax.nn.softplus(small + dt_bias))
    cum_t = cum.T
    tri, tri_strict = _tri_mask(CHUNK), _tri_mask(CHUNK, strict=True)
    outs, new_states = [], []
    for h in range(N_HEAD6):
        q = qkv[:, h * HEAD:(h + 1) * HEAD]
        k = qkv[:, MIX_W + h * HEAD:MIX_W + (h + 1) * HEAD]
        v = qkv[:, 2 * MIX_W + h * HEAD:2 * MIX_W + (h + 1) * HEAD]
        q = q * lax.rsqrt(jnp.sum(q * q, axis=-1, keepdims=True) + NORM_EPS) * (HEAD ** -0.5)
        k = k * lax.rsqrt(jnp.sum(k * k, axis=-1, keepdims=True) + NORM_EPS)
        beta = beta_all[:, GB_OFF + h:GB_OFF + h + 1]
        c, c_t = cum[:, GA_OFF + h:GA_OFF + h + 1], cum_t[GA_OFF + h:GA_OFF + h + 1, :]
        ce = cum[CHUNK - 1:CHUNK, GA_OFF + h:GA_OFF + h + 1]
        decay = _masked_exp(c - c_t, tri)
        a_low = jnp.where(tri_strict, beta * _bdot_nt(k, k) * decay, 0.0)
        rhs = jnp.concatenate([v * beta, k * (beta * jnp.exp(c))], axis=1)
        sol = _unit_lower_solve(a_low, rhs)
        u_base, w_corr = sol[:, :HEAD], sol[:, HEAD:]
        qk = _bdot_nt(q, k) * decay
        st = state[h]
        u = u_base - _bdot(w_corr, st)
        o = _bdot(q * jnp.exp(c), st) + _bdot(qk, u)
        new_states.append(jnp.exp(ce) * st + _bdot_tn(k * jnp.exp(ce - c), u))
        outs.append(_rms(o, norm_w) * _silu(z[:, h * HEAD:(h + 1) * HEAD]))
    return (jnp.concatenate(outs, axis=1),), jnp.stack(new_states)


def _scan_fwd(name, fn, tiled, halo_idx, consts, out_width, state_shape):
    seq = tiled[0][0].shape[0]
    nc = seq // CHUNK
    n_t, n_h, n_c = len(tiled), len(halo_idx), len(consts)

    def body(*refs):
        t_refs, h_refs, c_refs = refs[:n_t], refs[n_t:n_t + n_h], refs[n_t + n_h:n_t + n_h + n_c]
        y_ref, save_ref, st_ref = refs[n_t + n_h + n_c:]
        i = pl.program_id(0)

        @pl.when(i == 0)
        def _():
            st_ref[...] = jnp.zeros_like(st_ref)

        flag = jnp.where(i > 0, 1.0, 0.0).astype(F32)
        st = st_ref[...]
        (y,), new = fn([r[...] for r in t_refs], [r[...] * flag for r in h_refs], st, [r[...] for r in c_refs])
        save_ref[0] = st
        y_ref[...] = y.astype(y_ref.dtype)
        st_ref[...] = new

    in_specs = [pl.BlockSpec((CHUNK, w), functools.partial(lambda i, cb: (i, cb), cb=cb)) for _, w, cb in tiled]
    in_specs += [pl.BlockSpec((HALO, tiled[j][1]),
                              functools.partial(lambda i, cb: (jnp.maximum(i * (CHUNK // HALO) - 1, 0), cb), cb=tiled[j][2]))
                 for j in halo_idx]
    in_specs += [pl.BlockSpec(c.shape, functools.partial(lambda i, nd: (0,) * nd, nd=c.ndim)) for c in consts]
    zeros = (0,) * len(state_shape)
    return _pc(
        body, name=name, grid=(nc,), in_specs=in_specs,
        out_specs=(pl.BlockSpec((CHUNK, out_width), lambda i: (i, 0)),
                   pl.BlockSpec((1,) + state_shape, lambda i: (i,) + zeros)),
        out_shape=(jax.ShapeDtypeStruct((seq, out_width), BF), jax.ShapeDtypeStruct((nc,) + state_shape, F32)),
        scratch_shapes=[pltpu.VMEM(state_shape, F32)],
        compiler_params=_cparams(("arbitrary",)),
    )(*[t[0] for t in tiled], *[tiled[j][0] for j in halo_idx], *consts)


def _scan_bwd(name, fn, tiled, halo_idx, consts, saved, dy, dtile_dtypes, extra=None):
    seq = tiled[0][0].shape[0]
    nc = seq // CHUNK
    n_t, n_h, n_c = len(tiled), len(halo_idx), len(consts)
    state_shape = saved.shape[1:]
    n_x = 0 if extra is None else 1

    def body(*refs):
        t_refs, h_refs, c_refs = refs[:n_t], refs[n_t:n_t + n_h], refs[n_t + n_h:n_t + n_h + n_c]
        pos = n_t + n_h + n_c
        save_ref, dy_ref = refs[pos], refs[pos + 1]
        x_refs = refs[pos + 2:pos + 2 + n_x]
        pos += 2 + n_x
        dt_refs, dc_refs = refs[pos:pos + n_t], refs[pos + n_t:pos + n_t + n_c]
        dst_ref = refs[pos + n_t + n_c]
        carry_refs = refs[pos + n_t + n_c + 1:]
        i = pl.program_id(0)

        @pl.when(i == 0)
        def _():
            dst_ref[...] = jnp.zeros_like(dst_ref)
            for r in carry_refs:
                r[...] = jnp.zeros_like(r)
            for r in dc_refs:
                r[...] = jnp.zeros_like(r)

        flag = jnp.where(i < nc - 1, 1.0, 0.0).astype(F32)
        tiles = [r[...] for r in t_refs]
        halos = [r[...] * flag for r in h_refs]
        cvals = [r[...] for r in c_refs]
        _, vjp = jax.vjp(fn, tiles, halos, save_ref[0], cvals)
        d_tiles, d_halos, d_state, d_consts = vjp(((dy_ref[...].astype(F32),), dst_ref[...]))
        dst_ref[...] = d_state
        for r, g in zip(dc_refs, d_consts):
            r[...] += g
        for j, (r, g) in enumerate(zip(dt_refs, d_tiles)):
            if extra is not None and extra[0] == j:
                g = g + x_refs[0][...].astype(F32)
            r[...] = g.astype(r.dtype)
            if j in halo_idx:
                cr = carry_refs[halo_idx.index(j)]
                r[CHUNK - HALO:CHUNK, :] = (g[CHUNK - HALO:CHUNK, :] + cr[...]).astype(r.dtype)
                cr[...] = d_halos[halo_idx.index(j)] * flag

    rev = lambda i: nc - 1 - i
    in_specs = [pl.BlockSpec((CHUNK, w), functools.partial(lambda i, cb: (rev(i), cb), cb=cb)) for _, w, cb in tiled]
    in_specs += [pl.BlockSpec((HALO, tiled[j][1]),
                              functools.partial(lambda i, cb: (jnp.maximum(rev(i) * (CHUNK // HALO) - 1, 0), cb), cb=tiled[j][2]))
                 for j in halo_idx]
    in_specs += [pl.BlockSpec(c.shape, functools.partial(lambda i, nd: (0,) * nd, nd=c.ndim)) for c in consts]
    zeros = (0,) * len(state_shape)
    in_specs += [pl.BlockSpec((1,) + state_shape, lambda i: (rev(i),) + zeros),
                 pl.BlockSpec((CHUNK, dy.shape[1]), lambda i: (rev(i), 0))]
    args = [t[0] for t in tiled] + [tiled[j][0] for j in halo_idx] + list(consts) + [saved, dy]
    if extra is not None:
        in_specs.append(pl.BlockSpec((CHUNK, extra[1].shape[1]), lambda i: (rev(i), 0)))
        args.append(extra[1])
    out_specs = [pl.BlockSpec((CHUNK, w), lambda i: (rev(i), 0)) for _, w, _ in tiled]
    out_specs += [pl.BlockSpec(c.shape, functools.partial(lambda i, nd: (0,) * nd, nd=c.ndim)) for c in consts]
    out_shape = [jax.ShapeDtypeStruct((seq, w), dtd) for (_, w, _), dtd in zip(tiled, dtile_dtypes)]
    out_shape += [jax.ShapeDtypeStruct(c.shape, F32) for c in consts]
    scratch = [pltpu.VMEM(state_shape, F32)] + [pltpu.VMEM((HALO, tiled[j][1]), F32) for j in halo_idx]
    return _pc(body, name=name, grid=(nc,), in_specs=in_specs, out_specs=tuple(out_specs), out_shape=tuple(out_shape),
               scratch_shapes=scratch, compiler_params=_cparams(("arbitrary",)))(*args)


def _tile_fwd(name, fn, tiled, consts, outs, tm=TOKEN_TILE):
    seq = tiled[0][0].shape[0]
    n_t, n_c = len(tiled), len(consts)

    def body(*refs):
        res = fn(*[r[...] for r in refs[:n_t + n_c]])
        for r, y in zip(refs[n_t + n_c:], res):
            r[...] = y.astype(r.dtype)

    in_specs = [pl.BlockSpec((tm, w), functools.partial(lambda i, cb: (i, cb), cb=cb)) for _, w, cb in tiled]
    in_specs += [pl.BlockSpec(c.shape, functools.partial(lambda i, nd: (0,) * nd, nd=c.ndim)) for c in consts]
    return _pc(body, name=name, grid=(seq // tm,), in_specs=in_specs,
               out_specs=tuple(pl.BlockSpec((tm, w), lambda i: (i, 0)) for w, _ in outs),
               out_shape=tuple(jax.ShapeDtypeStruct((seq, w), dtp) for w, dtp in outs),
               compiler_params=_cparams(("arbitrary",)))(*[t[0] for t in tiled], *consts)


def _tile_bwd(name, fn, tiled, consts, douts, dtile_dtypes, add_to=None, tm=TOKEN_TILE):
    seq = tiled[0][0].shape[0]
    n_t, n_c, n_o = len(tiled), len(consts), len(douts)
    n_x = 0 if add_to is None else 1
    keep = [j for j, dtp in enumerate(dtile_dtypes) if dtp is not None]

    def body(*refs):
        vals = [r[...].astype(F32) for r in refs[:n_t + n_c]]
        pos = n_t + n_c
        g_refs, x_refs = refs[pos:pos + n_o], refs[pos + n_o:pos + n_o + n_x]
        pos += n_o + n_x
        dt_refs, dc_refs = refs[pos:pos + len(keep)], refs[pos + len(keep):]
        i = pl.program_id(0)

        @pl.when(i == 0)
        def _():
            for r in dc_refs:
                r[...] = jnp.zeros_like(r)

        _, vjp = jax.vjp(fn, *vals)
        cts = vjp(tuple(g[...].astype(F32) for g in g_refs))
        for r, j in zip(dt_refs, keep):
            g = cts[j]
            if add_to is not None and add_to[0] == j:
                g = g + x_refs[0][...].astype(F32)
            r[...] = g.astype(r.dtype)
        for r, g in zip(dc_refs, cts[n_t:]):
            r[...] += g

    in_specs = [pl.BlockSpec((tm, w), functools.partial(lambda i, cb: (i, cb), cb=cb)) for _, w, cb in tiled]
    in_specs += [pl.BlockSpec(c.shape, functools.partial(lambda i, nd: (0,) * nd, nd=c.ndim)) for c in consts]
    in_specs += [pl.BlockSpec((tm, g.shape[1]), lambda i: (i, 0)) for g in douts]
    args = [t[0] for t in tiled] + list(consts) + list(douts)
    if add_to is not None:
        in_specs.append(pl.BlockSpec((tm, add_to[1].shape[1]), lambda i: (i, 0)))
        args.append(add_to[1])
    out_specs = [pl.BlockSpec((tm, tiled[j][1]), lambda i: (i, 0)) for j in keep]
    out_specs += [pl.BlockSpec(c.shape, functools.partial(lambda i, nd: (0,) * nd, nd=c.ndim)) for c in consts]
    out_shape = [jax.ShapeDtypeStruct((seq, tiled[j][1]), dtile_dtypes[j]) for j in keep]
    out_shape += [jax.ShapeDtypeStruct(c.shape, F32) for c in consts]
    return _pc(body, name=name, grid=(seq // tm,), in_specs=in_specs, out_specs=tuple(out_specs),
               out_shape=tuple(out_shape), compiler_params=_cparams(("arbitrary",)))(*args)


def _lnmod_fn(x, nw, sc, sh):
    return (_rms(x, nw) * (1.0 + sc) + sh,)


def _resid_fn(x, o, g):
    return (x + (1.0 + g) * o,)


def _swiglu_fn(gu):
    return (_silu(gu[:, :FFN_H]) * gu[:, FFN_H:],)


def _merge_fn(yh, ys, yg, logits, wb, b_merge):
    gates = jax.nn.sigmoid(logits + b_merge)
    acc = None
    for n, y in enumerate((yh, ys, yg)):
        t = gates[:, n * D_MODEL:(n + 1) * D_MODEL] * _bdot(y, wb[n])
        acc = t if acc is None else acc + t
    return (acc,)


def _pick(n, cands=(512, 256, 128)):
    for c in cands:
        if n % c == 0:
            return c
    return n


def _mm(a, b, mode, out_dtype, name):
    if mode == "nn":
        (m, k), n = a.shape, b.shape[1]
    elif mode == "nt":
        (m, k), n = a.shape, b.shape[0]
    else:
        (k, m), n = a.shape, b.shape[1]
    tm, tn, tk = _pick(m), _pick(n), _pick(k, (1024, 512, 256, 128))
    nk = k // tk
    dims = {"nn": ((1,), (0,)), "nt": ((1,), (1,)), "tn": ((0,), (0,))}[mode]

    def body(a_ref, b_ref, o_ref, acc_ref):
        kk = pl.program_id(2)

        @pl.when(kk == 0)
        def _():
            acc_ref[...] = jnp.zeros_like(acc_ref)

        acc_ref[...] += lax.dot_general(a_ref[...], b_ref[...], (dims, ((), ())), preferred_element_type=F32)

        @pl.when(kk == nk - 1)
        def _():
            o_ref[...] = acc_ref[...].astype(o_ref.dtype)

    a_spec = pl.BlockSpec((tk, tm), lambda i, j, kk: (kk, i)) if mode == "tn" else pl.BlockSpec((tm, tk), lambda i, j, kk: (i, kk))
    b_spec = pl.BlockSpec((tn, tk), lambda i, j, kk: (j, kk)) if mode == "nt" else pl.BlockSpec((tk, tn), lambda i, j, kk: (kk, j))
    return _pc(body, name=name, grid=(m // tm, n // tn, nk), in_specs=[a_spec, b_spec],
               out_specs=pl.BlockSpec((tm, tn), lambda i, j, kk: (i, j)),
               out_shape=jax.ShapeDtypeStruct((m, n), out_dtype),
               scratch_shapes=[pltpu.VMEM((tm, tn), F32)],
               compiler_params=_cparams(("parallel", "parallel", "arbitrary")))(a.astype(BF), b.astype(BF))


def _final_loss(x, tgt, norm_final, tm=TOKEN_TILE):
    seq = x.shape[0]

    def fn(xv, nf, tv):
        err = jnp.square(_rms(xv, nf) - tv)
        return 0.5 * jnp.sum(jnp.mean(err, axis=-1))

    def body(x_ref, t_ref, nf_ref, loss_ref, dx_ref, dnf_ref):
        i = pl.program_id(0)

        @pl.when(i == 0)
        def _():
            loss_ref[...] = jnp.zeros_like(loss_ref)
            dnf_ref[...] = jnp.zeros_like(dnf_ref)

        val, vjp = jax.vjp(functools.partial(fn, tv=t_ref[...]), x_ref[...], nf_ref[...])
        dx, dnf = vjp(jnp.ones((), F32))
        dx_ref[...] = dx
        dnf_ref[...] += dnf
        loss_ref[...] += jnp.broadcast_to(val, loss_ref.shape)

    return _pc(body, name="final_loss", grid=(seq // tm,),
               in_specs=[pl.BlockSpec((tm, D_MODEL), lambda i: (i, 0)), pl.BlockSpec((tm, D_MODEL), lambda i: (i, 0)),
                         pl.BlockSpec((1, D_MODEL), lambda i: (0, 0))],
               out_specs=(pl.BlockSpec((8, LANES), lambda i: (0, 0)), pl.BlockSpec((tm, D_MODEL), lambda i: (i, 0)),
                          pl.BlockSpec((1, D_MODEL), lambda i: (0, 0))),
               out_shape=(jax.ShapeDtypeStruct((8, LANES), F32), jax.ShapeDtypeStruct((seq, D_MODEL), F32),
                          jax.ShapeDtypeStruct((1, D_MODEL), F32)),
               compiler_params=_cparams(("arbitrary",)))(x, tgt, norm_final)


def _ada_fwd(c_all, w_ada, b_ada_cols):
    n_l, _, cols = w_ada.shape

    def body(c_ref, w_ref, b_ref, o_ref):
        o_ref[0] = jnp.dot(_silu(c_ref[...]), w_ref[0], preferred_element_type=F32, precision=HI) + b_ref[0]

    return _pc(body, name="ada_fwd", grid=(n_l,),
               in_specs=[pl.BlockSpec((8, D_MODEL), lambda l: (0, 0)), pl.BlockSpec((1, D_MODEL, cols), lambda l: (l, 0, 0)),
                         pl.BlockSpec((1, 1, cols), lambda l: (l, 0, 0))],
               out_specs=pl.BlockSpec((1, 8, cols), lambda l: (l, 0, 0)),
               out_shape=jax.ShapeDtypeStruct((n_l, 8, cols), F32),
               compiler_params=_cparams(("arbitrary",)))(c_all, w_ada, b_ada_cols.reshape(n_l, 1, cols))


def _ada_bwd(c_all, dmod_cols):
    n_l, _, cols = dmod_cols.shape

    def body(c_ref, g_ref, o_ref):
        o_ref[0] = lax.dot_general(_silu(c_ref[...]), g_ref[0], (((0,), (0,)), ((), ())), preferred_element_type=F32,
                                   precision=HI)

    return _pc(body, name="ada_bwd", grid=(n_l,),
               in_specs=[pl.BlockSpec((8, D_MODEL), lambda l: (0, 0)), pl.BlockSpec((1, 8, cols), lambda l: (l, 0, 0))],
               out_specs=pl.BlockSpec((1, D_MODEL, cols), lambda l: (l, 0, 0)),
               out_shape=jax.ShapeDtypeStruct((n_l, D_MODEL, cols), F32),
               compiler_params=_cparams(("arbitrary",)))(c_all, dmod_cols)


def _lb_fn(logits):
    e = jnp.exp(logits - jnp.max(logits, axis=0, keepdims=True))
    p = e / jnp.sum(e, axis=0, keepdims=True)
    r = lax.broadcasted_iota(jnp.int32, (DEPTH, 1), 0)
    lb = jnp.zeros_like(p)
    for j in range(1, DEPTH):
        lb = lb + jnp.where(r >= j, p[j:j + 1, :], 0.0)
    return lb


def _lb_fwd(logits):
    def body(l_ref, o_ref):
        o_ref[...] = _lb_fn(l_ref[...])

    return _pc(body, name="lb_fwd", out_shape=jax.ShapeDtypeStruct(logits.shape, F32))(logits)


def _lb_bwd(logits, dlb):
    def body(l_ref, g_ref, o_ref):
        _, vjp = jax.vjp(_lb_fn, l_ref[...])
        o_ref[...] = vjp(g_ref[...])[0]

    return _pc(body, name="lb_bwd", out_shape=jax.ShapeDtypeStruct(logits.shape, F32))(logits, dlb)


def _rows_for(n_rows, n_cols):
    r = 8
    while r * 2 <= n_rows and n_rows % (r * 2) == 0 and r * 2 * n_cols <= 256 * 1024:
        r *= 2
    return r if n_rows % r == 0 else n_rows


def _ew(name, fn, ins, out_dtypes):
    n_rows, n_cols = ins[0].shape
    tr = _rows_for(n_rows, n_cols)
    n_in = len(ins)

    def body(*refs):
        res = fn(*[r[...] for r in refs[:n_in]])
        for r, y in zip(refs[n_in:], res):
            r[...] = y.astype(r.dtype)

    spec = pl.BlockSpec((tr, n_cols), lambda i: (i, 0))
    return _pc(body, name=name, grid=(n_rows // tr,), in_specs=[spec] * n_in, out_specs=tuple([spec] * len(out_dtypes)),
               out_shape=tuple(jax.ShapeDtypeStruct((n_rows, n_cols), d) for d in out_dtypes),
               compiler_params=_cparams(("arbitrary",)))(*ins)


def _adamw_fn(w, g, m, v):
    m = ADAM_B1 * m + (1.0 - ADAM_B1) * g
    v = ADAM_B2 * v + (1.0 - ADAM_B2) * jnp.square(g)
    m_hat = m / (1.0 - ADAM_B1 ** ADAM_STEP)
    v_hat = v / (1.0 - ADAM_B2 ** ADAM_STEP)
    return -ADAM_LR * (m_hat / (jnp.sqrt(v_hat) + ADAM_EPS) + ADAM_WD * w), m, v


def _adamw(name, w, g, m, v):
    shape = w.shape
    two = (-1, shape[-1])
    d, nm, nv = _ew(name, _adamw_fn, [a.reshape(two) for a in (w, g, m, v)], [F32, F32, F32])
    return d.reshape(shape), nm.reshape(shape), nv.reshape(shape)


def _sum_leading(name, a, out_dtype):
    n, n_rows, n_cols = a.shape
    tr = _rows_for(n_rows, n_cols)

    def body(a_ref, o_ref):
        acc = a_ref[0].astype(F32)
        for j in range(1, n):
            acc = acc + a_ref[j].astype(F32)
        o_ref[...] = acc.astype(o_ref.dtype)

    return _pc(body, name=name, grid=(n_rows // tr,), in_specs=[pl.BlockSpec((n, tr, n_cols), lambda i: (0, i, 0))],
               out_specs=pl.BlockSpec((tr, n_cols), lambda i: (i, 0)),
               out_shape=jax.ShapeDtypeStruct((n_rows, n_cols), out_dtype),
               compiler_params=_cparams(("arbitrary",)))(a)


MESH = pl.DeviceIdType.MESH
ANY = pl.BlockSpec(memory_space=pl.ANY)


def _place():
    return lax.axis_index("x"), lax.axis_index("y"), lax.axis_index("c")


def _all_gather_small(name, a):
    m_per, n = a.shape

    def body(x_ref, out_ref, send_sems, recv_sems, local_sem):
        x, y, c = _place()
        me, sibling = (x, y, c), (x, y, 1 - c)
        chips = [(1 - x, y), (x, 1 - y), (1 - x, 1 - y)]

        def rows(px, py, pc):
            return out_ref.at[pl.ds((4 * px + 2 * py + pc) * m_per, m_per), :]

        def copy(k, block, to, src=None):
            return pltpu.make_async_remote_copy(src_ref=rows(*block) if src is None else src, dst_ref=rows(*block),
                                                send_sem=send_sems.at[k], recv_sem=recv_sems.at[k], device_id=to,
                                                device_id_type=MESH)

        mine = pltpu.make_async_copy(x_ref, rows(*me), local_sem)
        mine.start()
        first = [copy(0, me, sibling, src=x_ref)]
        first += [copy(1 + j, me, (*chip, c), src=x_ref) for j, chip in enumerate(chips)]
        for cp in first:
            cp.start()
        passed = [copy(4 + j, (*chip, c), sibling) for j, chip in enumerate(chips)]
        for j, chip in enumerate(chips):
            copy(1 + j, (*chip, c), me).wait_recv()
            passed[j].start()
        copy(0, sibling, me).wait_recv()
        for j, chip in enumerate(chips):
            copy(4 + j, (*chip, 1 - c), me).wait_recv()
        for cp in first + passed:
            cp.wait_send()
        mine.wait()

    out = _pc(body, name=name, out_shape=jax.ShapeDtypeStruct((8 * m_per, n), a.dtype),
              in_specs=[pl.BlockSpec(memory_space=pltpu.VMEM)], out_specs=pl.BlockSpec(memory_space=pltpu.VMEM),
              scratch_shapes=[pltpu.SemaphoreType.DMA((7,)), pltpu.SemaphoreType.DMA((7,)), pltpu.SemaphoreType.DMA],
              compiler_params=pltpu.CompilerParams(vmem_limit_bytes=VMEM_LIMIT))(a)
    return out.reshape(8, m_per, n)


def _chip_gather(name, pack):
    n_l, n_r, n_c = pack.shape
    half = n_r // 2

    def body(p_ref, o_ref, send_sems, recv_sems, local_sem):
        x, y, c = _place()
        sibling = (x, y, 1 - c)
        chips = [(1 - x, y), (x, 1 - y), (1 - x, 1 - y)]

        def slab(px, py, pc):
            return o_ref.at[2 * px + py, :, pl.ds(pc * half, half), :]

        def copy(k, src, dst, to):
            return pltpu.make_async_remote_copy(src_ref=src, dst_ref=dst, send_sem=send_sems.at[k], recv_sem=recv_sems.at[k],
                                                device_id=to, device_id_type=MESH)

        mine = pltpu.make_async_copy(p_ref, o_ref.at[2 * x + y], local_sem)
        mine.start()
        first = [copy(j, p_ref.at[:, pl.ds(c * half, half), :], slab(x, y, c), (*chip, c)) for j, chip in enumerate(chips)]
        for cp in first:
            cp.start()
        passed = [copy(3 + j, slab(*chip, c), slab(*chip, c), sibling) for j, chip in enumerate(chips)]
        for j, chip in enumerate(chips):
            copy(j, slab(*chip, c), slab(*chip, c), (*chip, c)).wait_recv()
            passed[j].start()
        for j, chip in enumerate(chips):
            copy(3 + j, slab(*chip, 1 - c), slab(*chip, 1 - c), sibling).wait_recv()
        for cp in first + passed:
            cp.wait_send()
        mine.wait()

    return _pc(body, name=name, out_shape=jax.ShapeDtypeStruct((4, n_l, n_r, n_c), pack.dtype), in_specs=[ANY], out_specs=ANY,
               scratch_shapes=[pltpu.SemaphoreType.DMA((6,)), pltpu.SemaphoreType.DMA((6,)), pltpu.SemaphoreType.DMA])(pack)


def _pair_swap(name, give):
    def body(g_ref, o_ref, send_sem, recv_sem):
        x, y, c = _place()
        cp = pltpu.make_async_remote_copy(src_ref=g_ref, dst_ref=o_ref, send_sem=send_sem, recv_sem=recv_sem,
                                          device_id=(x, y, 1 - c), device_id_type=MESH)
        cp.start()
        cp.wait()

    return _pc(body, name=name, out_shape=jax.ShapeDtypeStruct(give.shape, give.dtype), in_specs=[ANY], out_specs=ANY,
               scratch_shapes=[pltpu.SemaphoreType.DMA, pltpu.SemaphoreType.DMA])(give)


def _chip_exchange(name, parts):
    def body(p_ref, o_ref, send_sems, recv_sems, local_sem):
        x, y, c = _place()
        me = 2 * x + y
        chips = [(1 - x, y), (x, 1 - y), (1 - x, 1 - y)]

        def copy(k, src, dst, to):
            return pltpu.make_async_remote_copy(src_ref=src, dst_ref=dst, send_sem=send_sems.at[k], recv_sem=recv_sems.at[k],
                                                device_id=to, device_id_type=MESH)

        mine = pltpu.make_async_copy(p_ref.at[me], o_ref.at[me], local_sem)
        mine.start()
        sends = [copy(j, p_ref.at[2 * px + py], o_ref.at[me], (px, py, c)) for j, (px, py) in enumerate(chips)]
        for cp in sends:
            cp.start()
        for j, (px, py) in enumerate(chips):
            copy(j, p_ref.at[2 * px + py], o_ref.at[2 * px + py], (px, py, c)).wait_recv()
        for cp in sends:
            cp.wait_send()
        mine.wait()

    return _pc(body, name=name, out_shape=jax.ShapeDtypeStruct(parts.shape, parts.dtype), in_specs=[ANY], out_specs=ANY,
               scratch_shapes=[pltpu.SemaphoreType.DMA((3,)), pltpu.SemaphoreType.DMA((3,)), pltpu.SemaphoreType.DMA])(parts)


def _pair_join(name, half_rows):
    n_l, half, n_c = half_rows.shape

    def body(h_ref, o_ref, send_sem, recv_sem, local_sem):
        x, y, c = _place()
        mine = pltpu.make_async_copy(h_ref, o_ref.at[:, pl.ds(c * half, half), :], local_sem)
        mine.start()
        cp = pltpu.make_async_remote_copy(src_ref=h_ref, dst_ref=o_ref.at[:, pl.ds(c * half, half), :], send_sem=send_sem,
                                          recv_sem=recv_sem, device_id=(x, y, 1 - c), device_id_type=MESH)
        cp.start()
        pltpu.make_async_remote_copy(src_ref=h_ref, dst_ref=o_ref.at[:, pl.ds((1 - c) * half, half), :], send_sem=send_sem,
                                     recv_sem=recv_sem, device_id=(x, y, 1 - c), device_id_type=MESH).wait_recv()
        cp.wait_send()
        mine.wait()

    return _pc(body, name=name, out_shape=jax.ShapeDtypeStruct((n_l, 2 * half, n_c), half_rows.dtype), in_specs=[ANY],
               out_specs=ANY,
               scratch_shapes=[pltpu.SemaphoreType.DMA, pltpu.SemaphoreType.DMA, pltpu.SemaphoreType.DMA])(half_rows)


N_CHIP = 4
PACK_C = 1024
BIG = (("w_in", (1024, 2822), 1), ("w_branch", (3, 768, 256), 2), ("w_out", (256, 1024), 0),
       ("w_ffn_in", (1024, 1408), 1), ("w_ffn_out", (704, 1024), 0))
BIG_ROWS = tuple(math.prod(s) // PACK_C for _, s, _ in BIG)
PACK_R = 5792
G768 = ((0, 3072), (3072, 3840), (7436, 8204))
GXBC, GQKV, GGATE = (3840, 5120), (5132, 7436), (8216, 11288)
GSMALL = ((5120, 5132), (8204, 8210), (8210, 8216))
W768, WXBC, WGATE = 4608, CONV_CH, 3 * D_MODEL
IN_PAD = W768 + WXBC + QKV_W + WGATE + SMALL_W


def _pack_layer(parts, dtype):
    rows = [p.astype(dtype).reshape(-1, PACK_C) for p in parts]
    rows.append(jnp.zeros((PACK_R - sum(BIG_ROWS), PACK_C), dtype))
    return jnp.concatenate(rows, axis=0)


def _unpack_layer(rows):
    out, r0 = [], 0
    for (_, shape, _), n in zip(BIG, BIG_ROWS):
        out.append(rows[r0:r0 + n].reshape(shape))
        r0 += n
    return out


def _regroup_w_in(w):
    cat = lambda spans: jnp.concatenate([w[:, a:b] for a, b in spans], axis=1)
    small = jnp.concatenate([cat(GSMALL), jnp.zeros((w.shape[0], SMALL_W - 24), w.dtype)], axis=1)
    return cat(G768), cat((GXBC,)), cat((GQKV,)), cat((GGATE,)), small


def _ungroup_w_in(d):
    o_xbc, o_qkv, o_gate, o_small = W768, W768 + WXBC, W768 + WXBC + QKV_W, W768 + WXBC + QKV_W + WGATE
    spans = ((0, 3072), (3072, 3840), (o_xbc, o_xbc + WXBC), (o_small, o_small + 12), (o_qkv, o_qkv + QKV_W),
             (3840, 4608), (o_small + 12, o_small + 18), (o_small + 18, o_small + 24), (o_gate, o_gate + WGATE))
    return jnp.concatenate([d[:, a:b] for a, b in spans], axis=1)


def _lane_pad(v, off):
    return jnp.zeros((1, LANES), F32).at[0, off:off + v.shape[0]].set(v)


STATE6 = (N_HEAD6, HEAD, HEAD)


def _mixer_inputs(sv, lp):
    p768, pxbc, pqkv, psmall = sv["p768"], sv["pxbc"], sv["pqkv"], sv["psmall"]
    hgrn = ([(p768, MIX_W, j) for j in range(4)], [], [lp["lb"], lp["hgrn_norm"]])
    ssd = ([(p768, MIX_W, 4), (pxbc, CONV_CH, 0), (psmall, LANES, 0)], [1],
           [lp["ssm_conv_w"], lp["ssm_conv_b"], lp["ssm_dt_bias"], lp["ssm_a_log"], lp["ssm_d"], lp["ssm_norm"]])
    gdn = ([(pqkv, QKV_W, 0), (p768, MIX_W, 5), (psmall, LANES, 0)], [0],
           [lp["gdn_conv_w"], lp["gdn_dt_bias"], lp["gdn_a_log"], lp["gdn_norm"]])
    return hgrn, ssd, gdn


def _layer_fwd(x, md, lw, lp):
    sv = {"x": x}
    (sv["h1"],) = _tile_fwd("lnmod1", _lnmod_fn, [(x, D_MODEL, 0)], [lp["norm_mix"], md["sc1"], md["sh1"]], [(D_MODEL, BF)])
    for nm in ("768", "xbc", "qkv", "gate", "small"):
        sv["p" + nm] = _mm(sv["h1"], lw["win_" + nm], "nn", F32, "proj_" + nm)
    hgrn, ssd, gdn = _mixer_inputs(sv, lp)
    sv["y_h"], sv["st_h"] = _scan_fwd("hgrn_fwd", _hgrn_chunk, *hgrn, MIX_W, STATE6)
    sv["y_s"], sv["st_s"] = _scan_fwd("ssd_fwd", _ssd_chunk, *ssd, MIX_W, STATE6)
    sv["y_g"], sv["st_g"] = _scan_fwd("gdn_fwd", _gdn_chunk, *gdn, MIX_W, STATE6)
    (sv["merged"],) = _tile_fwd("merge", _merge_fn, _merge_tiles(sv), [lw["w_branch"], lp["b_merge"]], [(D_MODEL, BF)])
    sv["out"] = _mm(sv["merged"], lw["w_out"], "nn", F32, "out_proj")
    (sv["x_mid"],) = _tile_fwd("resid1", _resid_fn, [(x, D_MODEL, 0), (sv["out"], D_MODEL, 0)], [md["g1"]], [(D_MODEL, F32)])
    (sv["h2"],) = _tile_fwd("lnmod2", _lnmod_fn, [(sv["x_mid"], D_MODEL, 0)], [lp["norm_ffn"], md["sc2"], md["sh2"]],
                            [(D_MODEL, BF)])
    sv["gu"] = _mm(sv["h2"], lw["w_ffn_in"], "nn", F32, "ffn_in")
    (sv["act"],) = _tile_fwd("swiglu", _swiglu_fn, [(sv["gu"], 2 * FFN_H, 0)], [], [(FFN_H, BF)])
    sv["o2"] = _mm(sv["act"], lw["w_ffn_out"], "nn", F32, "ffn_out")
    (x_out,) = _tile_fwd("resid2", _resid_fn, [(sv["x_mid"], D_MODEL, 0), (sv["o2"], D_MODEL, 0)], [md["g2"]], [(D_MODEL, F32)])
    return x_out, sv


def _merge_tiles(sv):
    return [(sv["y_h"], MIX_W, 0), (sv["y_s"], MIX_W, 0), (sv["y_g"], MIX_W, 0), (sv["pgate"], WGATE, 0)]


def _layer_bwd(dx_out, sv, md, lw, lp):
    g = {}
    x, x_mid = sv["x"], sv["x_mid"]
    d_xmid, d_o2, g["g2"] = _tile_bwd("resid2_b", _resid_fn, [(x_mid, D_MODEL, 0), (sv["o2"], D_MODEL, 0)], [md["g2"]], [dx_out],
                                      [F32, BF])
    d_act = _mm(d_o2, lw["w_ffn_out"], "nt", F32, "ffn_out_dx")
    g["w_ffn_out"] = _mm(sv["act"], d_o2, "tn", F32, "ffn_out_dw")
    (d_gu,) = _tile_bwd("swiglu_b", _swiglu_fn, [(sv["gu"], 2 * FFN_H, 0)], [], [d_act], [BF])
    d_h2 = _mm(d_gu, lw["w_ffn_in"], "nt", F32, "ffn_in_dx")
    g["w_ffn_in"] = _mm(sv["h2"], d_gu, "tn", F32, "ffn_in_dw")
    d_xmid, g["norm_ffn"], g["sc2"], g["sh2"] = _tile_bwd(
        "lnmod2_b", _lnmod_fn, [(x_mid, D_MODEL, 0)], [lp["norm_ffn"], md["sc2"], md["sh2"]], [d_h2], [F32], add_to=(0, d_xmid))
    d_x, d_out, g["g1"] = _tile_bwd("resid1_b", _resid_fn, [(x, D_MODEL, 0), (sv["out"], D_MODEL, 0)], [md["g1"]], [d_xmid],
                                    [F32, BF])
    d_merged = _mm(d_out, lw["w_out"], "nt", F32, "out_proj_dx")
    g["w_out"] = _mm(sv["merged"], d_out, "tn", F32, "out_proj_dw")
    d_yh, d_ys, d_yg, d_gate, g["w_branch"], g["b_merge"] = _tile_bwd(
        "merge_b", _merge_fn, _merge_tiles(sv), [lw["w_branch"], lp["b_merge"]], [d_merged], [F32, F32, F32, BF])
    hgrn, ssd, gdn = _mixer_inputs(sv, lp)
    d_q, d_f, d_v, d_g, g["lb"], g["hgrn_norm"] = _scan_bwd("hgrn_bwd", _hgrn_chunk, *hgrn, sv["st_h"], d_yh, [BF] * 4)
    (d_sz, d_xbc, d_small, g["ssm_conv_w"], g["ssm_conv_b"], g["ssm_dt_bias"], g["ssm_a_log"], g["ssm_d"],
     g["ssm_norm"]) = _scan_bwd("ssd_bwd", _ssd_chunk, *ssd, sv["st_s"], d_ys, [BF, BF, F32])
    d_qkv, d_gz, d_small, g["gdn_conv_w"], g["gdn_dt_bias"], g["gdn_a_log"], g["gdn_norm"] = _scan_bwd(
        "gdn_bwd", _gdn_chunk, *gdn, sv["st_g"], d_yg, [BF, BF, BF], extra=(2, d_small))
    d_proj = jnp.concatenate([d_q, d_f, d_v, d_g, d_sz, d_gz, d_xbc, d_qkv, d_gate, d_small,
                              jnp.zeros((x.shape[0], SMALL_W - LANES), BF)], axis=1)
    d_h1 = _mm(d_proj, lw["win_all"], "nt", F32, "proj_dx")
    g["w_in"] = _ungroup_w_in(_mm(sv["h1"], d_proj, "tn", F32, "proj_dw"))
    d_x, g["norm_mix"], g["sc1"], g["sh1"] = _tile_bwd(
        "lnmod1_b", _lnmod_fn, [(x, D_MODEL, 0)], [lp["norm_mix"], md["sc1"], md["sh1"]], [d_h1], [F32], add_to=(0, d_x))
    return d_x, g


SMALL_REPL = ("norm_mix", "norm_ffn", "b_merge", "hgrn_lb_logits", "hgrn_norm", "ssm_conv_w", "ssm_conv_b", "ssm_dt_bias",
              "ssm_a_log", "ssm_d", "ssm_norm", "gdn_conv_w", "gdn_dt_bias", "gdn_a_log", "gdn_norm", "norm_final")
WEIGHTS = ("w_ada", "b_ada", "norm_mix", "norm_ffn", "w_in", "b_merge", "hgrn_lb_logits", "hgrn_norm", "ssm_conv_w",
           "ssm_conv_b", "ssm_dt_bias", "ssm_a_log", "ssm_d", "ssm_norm", "gdn_conv_w", "gdn_dt_bias", "gdn_a_log",
           "gdn_norm", "w_branch", "w_out", "w_ffn_in", "w_ffn_out", "norm_final")
SMALL_ROWS = 120


def _pad_rows(flat, n_rows, n_cols):
    return jnp.concatenate([flat, jnp.zeros((n_rows * n_cols - flat.shape[0],), flat.dtype)]).reshape(n_rows, n_cols)


def _device_step(x, tgt, mod, lb, wfull, sp):
    mds, lps, svs = [], [], []
    h = x
    for l in range(DEPTH):
        md = {n: mod[l, i * D_MODEL:(i + 1) * D_MODEL][None, :] for i, n in enumerate(("sh1", "sc1", "g1", "sh2", "sc2", "g2"))}
        lp = {n: sp[n][l][None, :] for n in ("norm_mix", "norm_ffn", "b_merge", "hgrn_norm", "ssm_conv_b", "ssm_norm", "gdn_norm")}
        lp["lb"] = lb[l][None, :]
        lp["ssm_conv_w"], lp["gdn_conv_w"] = sp["ssm_conv_w"][l], sp["gdn_conv_w"][l]
        for n in ("ssm_dt_bias", "ssm_a_log", "ssm_d"):
            lp[n] = _lane_pad(sp[n][l], DT_OFF)
        for n in ("gdn_dt_bias", "gdn_a_log"):
            lp[n] = _lane_pad(sp[n][l], GA_OFF)
        h, sv = _layer_fwd(h, md, wfull[l], lp)
        mds.append(md), lps.append(lp), svs.append(sv)
    loss, dh, d_nf = _final_loss(h, tgt, sp["norm_final"][None, :])
    grads = [None] * DEPTH
    for l in reversed(range(DEPTH)):
        dh, grads[l] = _layer_bwd(dh, svs[l], mds[l], wfull[l], lps[l])
    return loss, dh, d_nf, grads


def kernel(x, c, w_ada, b_ada, norm_mix, norm_ffn, w_in, b_merge, hgrn_lb_logits, hgrn_norm, ssm_conv_w, ssm_conv_b, ssm_dt_bias, ssm_a_log, ssm_d, ssm_norm, gdn_conv_w, gdn_dt_bias, gdn_a_log, gdn_norm, w_branch, w_out, w_ffn_in, w_ffn_out, norm_final, loss_target, m_w_ada, m_b_ada, m_norm_mix, m_norm_ffn, m_w_in, m_b_merge, m_hgrn_lb_logits, m_hgrn_norm, m_ssm_conv_w, m_ssm_conv_b, m_ssm_dt_bias, m_ssm_a_log, m_ssm_d, m_ssm_norm, m_gdn_conv_w, m_gdn_dt_bias, m_gdn_a_log, m_gdn_norm, m_w_branch, m_w_out, m_w_ffn_in, m_w_ffn_out, m_norm_final, v_w_ada, v_b_ada, v_norm_mix, v_norm_ffn, v_w_in, v_b_merge, v_hgrn_lb_logits, v_hgrn_norm, v_ssm_conv_w, v_ssm_conv_b, v_ssm_dt_bias, v_ssm_a_log, v_ssm_d, v_ssm_norm, v_gdn_conv_w, v_gdn_dt_bias, v_gdn_a_log, v_gdn_norm, v_w_branch, v_w_out, v_w_ffn_in, v_w_ffn_out, v_norm_final):
    w = dict(w_ada=w_ada, b_ada=b_ada, norm_mix=norm_mix, norm_ffn=norm_ffn, w_in=w_in, b_merge=b_merge,
             hgrn_lb_logits=hgrn_lb_logits, hgrn_norm=hgrn_norm, ssm_conv_w=ssm_conv_w, ssm_conv_b=ssm_conv_b,
             ssm_dt_bias=ssm_dt_bias, ssm_a_log=ssm_a_log, ssm_d=ssm_d, ssm_norm=ssm_norm, gdn_conv_w=gdn_conv_w,
             gdn_dt_bias=gdn_dt_bias, gdn_a_log=gdn_a_log, gdn_norm=gdn_norm, w_branch=w_branch, w_out=w_out,
             w_ffn_in=w_ffn_in, w_ffn_out=w_ffn_out, norm_final=norm_final)
    m = dict(w_ada=m_w_ada, b_ada=m_b_ada, norm_mix=m_norm_mix, norm_ffn=m_norm_ffn, w_in=m_w_in, b_merge=m_b_merge,
             hgrn_lb_logits=m_hgrn_lb_logits, hgrn_norm=m_hgrn_norm, ssm_conv_w=m_ssm_conv_w, ssm_conv_b=m_ssm_conv_b,
             ssm_dt_bias=m_ssm_dt_bias, ssm_a_log=m_ssm_a_log, ssm_d=m_ssm_d, ssm_norm=m_ssm_norm, gdn_conv_w=m_gdn_conv_w,
             gdn_dt_bias=m_gdn_dt_bias, gdn_a_log=m_gdn_a_log, gdn_norm=m_gdn_norm, w_branch=m_w_branch, w_out=m_w_out,
             w_ffn_in=m_w_ffn_in, w_ffn_out=m_w_ffn_out, norm_final=m_norm_final)
    v = dict(w_ada=v_w_ada, b_ada=v_b_ada, norm_mix=v_norm_mix, norm_ffn=v_norm_ffn, w_in=v_w_in, b_merge=v_b_merge,
             hgrn_lb_logits=v_hgrn_lb_logits, hgrn_norm=v_hgrn_norm, ssm_conv_w=v_ssm_conv_w, ssm_conv_b=v_ssm_conv_b,
             ssm_dt_bias=v_ssm_dt_bias, ssm_a_log=v_ssm_a_log, ssm_d=v_ssm_d, ssm_norm=v_ssm_norm, gdn_conv_w=v_gdn_conv_w,
             gdn_dt_bias=v_gdn_dt_bias, gdn_a_log=v_gdn_a_log, gdn_norm=v_gdn_norm, w_branch=v_w_branch, w_out=v_w_out,
             w_ffn_in=v_w_ffn_in, w_ffn_out=v_w_ffn_out, norm_final=v_norm_final)
    xi, yi, ci = _place()
    chip, me = 2 * xi + yi, 4 * xi + 2 * yi + ci
    seq = x.shape[1]

    conv_flat = jnp.concatenate([ssm_conv_w.reshape(-1), gdn_conv_w.reshape(-1)])
    n_conv = conv_flat.shape[0]
    first = _all_gather_small("gather_c_conv", _pad_rows(jnp.concatenate([c[0], conv_flat]), 16, D_MODEL))
    c_all = first[:, 0, :]
    conv_all = first[0::2].reshape(N_CHIP, -1)[:, D_MODEL:D_MODEL + n_conv]
    n_ssm = ssm_conv_w.size
    sp = dict(w)
    sp["ssm_conv_w"] = jnp.concatenate([conv_all[j, :n_ssm].reshape(ssm_conv_w.shape) for j in range(N_CHIP)], axis=2)
    sp["gdn_conv_w"] = jnp.concatenate([conv_all[j, n_ssm:].reshape(gdn_conv_w.shape) for j in range(N_CHIP)], axis=2)

    ada_cols = w_ada.shape[2]
    mod_part = _ada_fwd(c_all, w_ada, lax.dynamic_slice_in_dim(b_ada, chip * ada_cols, ada_cols, axis=1))
    mod_all = _all_gather_small("gather_mod", mod_part.reshape(DEPTH * 8, ada_cols))[0::2].reshape(N_CHIP, DEPTH, 8, ada_cols)
    mod = lax.dynamic_index_in_dim(mod_all, me, axis=2, keepdims=False).transpose(1, 0, 2).reshape(DEPTH, N_CHIP * ada_cols)
    lb = _lb_fwd(hgrn_lb_logits)

    pack = jnp.stack([_pack_layer([w[n][l] for n, _, _ in BIG], BF) for l in range(DEPTH)])
    gathered = _chip_gather("gather_weights", pack)
    wfull = []
    for l in range(DEPTH):
        shards = [_unpack_layer(gathered[j, l]) for j in range(N_CHIP)]
        full = {n: jnp.concatenate([shards[j][i] for j in range(N_CHIP)], axis=ax) for i, (n, _, ax) in enumerate(BIG)}
        for nm, part in zip(("768", "xbc", "qkv", "gate", "small"), _regroup_w_in(full["w_in"])):
            full["win_" + nm] = part
        full["win_all"] = jnp.concatenate([full["win_" + nm] for nm in ("768", "xbc", "qkv", "gate", "small")], axis=1)
        wfull.append(full)

    loss8, d_x, d_nf, lg = _device_step(x[0], loss_target[0], mod, lb, wfull, sp)

    gpack = jnp.stack([jnp.stack([
        _pack_layer([lax.slice_in_dim(lg[l][n], j * s[ax], (j + 1) * s[ax], axis=ax) for n, s, ax in BIG], BF)
        for l in range(DEPTH)]) for j in range(N_CHIP)])
    half = PACK_R // 2
    keep = lax.dynamic_slice_in_dim(gpack, ci * half, half, axis=2).reshape(-1, PACK_C)
    give = lax.dynamic_slice_in_dim(gpack, (1 - ci) * half, half, axis=2).reshape(-1, PACK_C)
    got = _pair_swap("grad_pair_swap", give)
    (pair_sum,) = _ew("grad_pair_sum", lambda a, b: (a.astype(F32) + b.astype(F32),), [keep, got], [BF])
    parts = _chip_exchange("grad_chip_exchange", pair_sum.reshape(N_CHIP, DEPTH * half, PACK_C))
    mine = _sum_leading("grad_chip_sum", parts, F32)
    gfull = _pair_join("grad_pair_join", mine.reshape(DEPTH, half, PACK_C))
    grad = {}
    per_layer = [_unpack_layer(gfull[l]) for l in range(DEPTH)]
    for i, (n, _, _) in enumerate(BIG):
        grad[n] = jnp.stack([per_layer[l][i] for l in range(DEPTH)])

    dmod = jnp.stack([jnp.concatenate([lg[l][n] for n in ("sh1", "sc1", "g1", "sh2", "sc2", "g2")], axis=1)[0] for l in range(DEPTH)])
    d_lb = jnp.stack([lg[l]["lb"][0] for l in range(DEPTH)])
    contrib = {
        "norm_mix": jnp.stack([lg[l]["norm_mix"][0] for l in range(DEPTH)]),
        "norm_ffn": jnp.stack([lg[l]["norm_ffn"][0] for l in range(DEPTH)]),
        "b_merge": jnp.stack([lg[l]["b_merge"][0] for l in range(DEPTH)]),
        "hgrn_lb_logits": _lb_bwd(hgrn_lb_logits, d_lb),
        "hgrn_norm": jnp.stack([lg[l]["hgrn_norm"][0] for l in range(DEPTH)]),
        "ssm_conv_w": jnp.stack([lg[l]["ssm_conv_w"] for l in range(DEPTH)]),
        "ssm_conv_b": jnp.stack([lg[l]["ssm_conv_b"][0] for l in range(DEPTH)]),
        "ssm_dt_bias": jnp.stack([lg[l]["ssm_dt_bias"][0, DT_OFF:DT_OFF + 12] for l in range(DEPTH)]),
        "ssm_a_log": jnp.stack([lg[l]["ssm_a_log"][0, DT_OFF:DT_OFF + 12] for l in range(DEPTH)]),
        "ssm_d": jnp.stack([lg[l]["ssm_d"][0, DT_OFF:DT_OFF + 12] for l in range(DEPTH)]),
        "ssm_norm": jnp.stack([lg[l]["ssm_norm"][0] for l in range(DEPTH)]),
        "gdn_conv_w": jnp.stack([lg[l]["gdn_conv_w"] for l in range(DEPTH)]),
        "gdn_dt_bias": jnp.stack([lg[l]["gdn_dt_bias"][0, GA_OFF:GA_OFF + 6] for l in range(DEPTH)]),
        "gdn_a_log": jnp.stack([lg[l]["gdn_a_log"][0, GA_OFF:GA_OFF + 6] for l in range(DEPTH)]),
        "gdn_norm": jnp.stack([lg[l]["gdn_norm"][0] for l in range(DEPTH)]),
        "norm_final": d_nf[0],
    }
    flat = jnp.concatenate([dmod.reshape(-1)] + [contrib[n].reshape(-1) for n in SMALL_REPL] + [loss8[0, 0:1]])
    small_all = _all_gather_small("gather_small_grads", _pad_rows(flat, SMALL_ROWS, D_MODEL))
    total = _sum_leading("small_grad_sum", small_all, F32).reshape(-1)
    n_mod = dmod.size
    grad["b_ada"] = total[:n_mod].reshape(b_ada.shape)
    off = n_mod
    full_small = {}
    for n in SMALL_REPL:
        full_small[n] = total[off:off + contrib[n].size].reshape(contrib[n].shape)
        off += contrib[n].size
    loss = total[off]
    for n in SMALL_REPL:
        if n in ("ssm_conv_w", "gdn_conv_w"):
            cols = w[n].shape[2]
            grad[n] = lax.dynamic_slice_in_dim(full_small[n], chip * cols, cols, axis=2)
        else:
            grad[n] = full_small[n]
    dmod_cols = lax.dynamic_slice_in_dim(small_all[:, :n_mod // D_MODEL, :].reshape(8, DEPTH, -1), chip * ada_cols, ada_cols, axis=2)
    grad["w_ada"] = _ada_bwd(c_all, dmod_cols.transpose(1, 0, 2))

    delta, new_m, new_v = {}, {}, {}
    big_names = ("w_ada",) + tuple(n for n, _, _ in BIG)
    for n in big_names:
        delta[n], new_m[n], new_v[n] = _adamw("adamw_" + n, w[n], grad[n], m[n], v[n])
    small_names = [n for n in WEIGHTS if n not in big_names]
    packs = [_pad_rows(jnp.concatenate([d[n].reshape(-1) for n in small_names]), 584, LANES) for d in (w, grad, m, v)]
    outs = _ew("adamw_small", _adamw_fn, packs, [F32, F32, F32])
    off = 0
    for n in small_names:
        for dst, o in zip((delta, new_m, new_v), outs):
            dst[n] = o.reshape(-1)[off:off + w[n].size].reshape(w[n].shape)
        off += w[n].size
    return (loss, d_x[None], *[grad[n] for n in WEIGHTS], *[delta[n] for n in WEIGHTS], *[new_m[n] for n in WEIGHTS],
            *[new_v[n] for n in WEIGHTS])
```

```python
import functools
import math

import jax
import jax.numpy as jnp
from jax import lax
from jax.experimental import pallas as pl
from jax.experimental.pallas import tpu as pltpu

F32 = jnp.float32
BF = jnp.bfloat16
HI = lax.Precision.HIGHEST

D_MODEL = 1024
DEPTH = 4
CHUNK = 64
MIX_W = 768
HEAD = 128
N_HEAD6 = 6
SSM_P = 64
SSM_N = 128
CONV_CH = 1280
QKV_W = 2304
FFN_H = 2816
IN_WIDTH = 11288
NORM_EPS = 1e-6
F_MIN = 1e-30
HALO = 8
HGRN_SUB = 16
SMALL_W = 512
LANES = 128
DT_OFF, GB_OFF, GA_OFF = 0, 12, 18

ADAM_LR, ADAM_B1, ADAM_B2, ADAM_EPS, ADAM_WD, ADAM_STEP = 0.001, 0.9, 0.999, 1e-08, 0.01, 10

VMEM_LIMIT = 56 * 1024 * 1024
TOKEN_TILE = 256


def _pc(body, **kw):
    return pl.pallas_call(body, **kw)


def _cparams(sem):
    return pltpu.CompilerParams(dimension_semantics=sem, vmem_limit_bytes=VMEM_LIMIT)


def _bdot(a, b):
    return jnp.dot(a.astype(BF), b.astype(BF), preferred_element_type=F32)


def _bdot_nt(a, b):
    return lax.dot_general(a.astype(BF), b.astype(BF), (((1,), (1,)), ((), ())), preferred_element_type=F32)


def _bdot_tn(a, b):
    return lax.dot_general(a.astype(BF), b.astype(BF), (((0,), (0,)), ((), ())), preferred_element_type=F32)


def _silu(x):
    return x * jax.nn.sigmoid(x)


def _tri_mask(n, strict=False):
    t = lax.broadcasted_iota(jnp.int32, (n, n), 0)
    s = lax.broadcasted_iota(jnp.int32, (n, n), 1)
    return (s < t) if strict else (s <= t)


def _masked_exp(diff, mask):
    return jnp.where(mask, jnp.exp(jnp.where(mask, diff, 0.0)), 0.0)


def _cumsum_rows(x):
    tri = jnp.where(_tri_mask(x.shape[0]), 1.0, 0.0).astype(F32)
    return jnp.dot(tri, x, preferred_element_type=F32, precision=HI)


def _rms(x, w):
    return x * lax.rsqrt(jnp.mean(x * x, axis=-1, keepdims=True) + NORM_EPS) * w


def _causal_conv(halo, x, w):
    ext = jnp.concatenate([halo, x], axis=0)
    n = x.shape[0]
    acc = w[0:1, :] * ext[HALO - 3:HALO - 3 + n, :]
    for i in range(1, 4):
        acc = acc + w[i:i + 1, :] * ext[HALO - 3 + i:HALO - 3 + i + n, :]
    return acc


def _unit_lower_inverse(a):
    n = a.shape[0]
    t = lax.broadcasted_iota(jnp.int32, (n, n), 0)
    s_ = lax.broadcasted_iota(jnp.int32, (n, n), 1)
    x = jnp.where(t == s_, 1.0, 0.0).astype(F32)
    for s in range(n - 1):
        r0 = 8 * ((s + 1) // 8)
        low = x[r0:] - a[r0:, s:s + 1] * x[s:s + 1, :]
        x = low if r0 == 0 else jnp.concatenate([x[:r0], low], axis=0)
    return x


@jax.custom_vjp
def _unit_lower_solve(a, r):
    return jnp.dot(_unit_lower_inverse(a), r, preferred_element_type=F32, precision=HI)


def _uls_fwd(a, r):
    inv = _unit_lower_inverse(a)
    x = jnp.dot(inv, r, preferred_element_type=F32, precision=HI)
    return x, (inv, x)


def _uls_bwd(res, g):
    inv, x = res
    y = lax.dot_general(inv, g, (((0,), (0,)), ((), ())), preferred_element_type=F32, precision=HI)
    da = jnp.where(_tri_mask(CHUNK, strict=True), -_bdot_nt(y, x), 0.0)
    return da, y


_unit_lower_solve.defvjp(_uls_fwd, _uls_bwd)


def _hgrn_chunk(tiles, halos, state, consts):
    q_raw, f_raw, v_all, g_raw = tiles
    lb, norm_w = consts
    q_all = _silu(q_raw)
    f = lb + (1.0 - lb) * jax.nn.sigmoid(f_raw)
    logf = jnp.log(jnp.maximum(f, F_MIN))
    k_all = (1.0 - lb) * jax.nn.sigmoid(-f_raw)
    b_all = _cumsum_rows(logf)
    sub = HGRN_SUB
    row = lax.broadcasted_iota(jnp.int32, (sub, 1), 0)
    outs, new_states = [], []
    for h in range(N_HEAD6):
        sl = slice(h * HEAD, (h + 1) * HEAD)
        q, k, v, b = q_all[:, sl], k_all[:, sl], v_all[:, sl], b_all[:, sl]
        st = state[h]
        o_inter = _bdot_nt(q * jnp.exp(b), st)
        blocks = []
        for i in range(CHUNK // sub):
            r0 = i * sub
            qi, ki, vi, bi = q[r0:r0 + sub], k[r0:r0 + sub], v[r0:r0 + sub], b[r0:r0 + sub]
            if i > 0:
                ref = b[r0 - 1:r0, :]
                a_off = _bdot_nt(qi * jnp.exp(bi - ref), k[0:r0] * jnp.exp(ref - b[0:r0]))
                oi = _bdot(a_off, v[0:r0])
            else:
                oi = jnp.zeros((sub, HEAD), F32)
            for s in range(sub):
                m = row >= s
                e = _masked_exp(bi - bi[s:s + 1, :], m)
                col = jnp.sum(qi * ki[s:s + 1, :] * e, axis=1, keepdims=True)
                oi = oi + col * vi[s:s + 1, :]
            blocks.append(oi)
        o = jnp.concatenate(blocks, axis=0) + o_inter
        b_end = b[CHUNK - 1:CHUNK, :]
        new_states.append(st * jnp.exp(b_end) + _bdot_tn(v, k * jnp.exp(b_end - b)))
        outs.append(_rms(o, norm_w) * _silu(g_raw[:, sl]))
    return (jnp.concatenate(outs, axis=1),), jnp.stack(new_states)


def _ssd_chunk(tiles, halos, state, consts):
    z, xbc_raw, small = tiles
    (halo,) = halos
    conv_w, conv_b, dt_bias, a_log, d_skip, norm_w = consts
    xbc = _silu(_causal_conv(halo, xbc_raw, conv_w) + conv_b)
    xs, bm, cm = xbc[:, :MIX_W], xbc[:, MIX_W:MIX_W + 2 * SSM_N], xbc[:, MIX_W + 2 * SSM_N:]
    dt = jax.nn.softplus(small + dt_bias)
    cum = _cumsum_rows(-jnp.exp(a_log) * dt)
    cum_t = cum.T
    lane = lax.broadcasted_iota(jnp.int32, (1, LANES), 1)
    tri = _tri_mask(CHUNK)
    ys, new_states = [], []
    gmats = []
    for g in range(2):
        gmats.append(_bdot_nt(cm[:, g * SSM_N:(g + 1) * SSM_N], bm[:, g * SSM_N:(g + 1) * SSM_N]))
    for p in range(6):
        g = p // 3
        bg, cg = bm[:, g * SSM_N:(g + 1) * SSM_N], cm[:, g * SSM_N:(g + 1) * SSM_N]
        xp = xs[:, p * LANES:(p + 1) * LANES]
        sp = state[p]
        yp = jnp.zeros((CHUNK, LANES), F32)
        sn = jnp.zeros((SSM_N, LANES), F32)
        decay = jnp.zeros((1, LANES), F32)
        for j in range(2):
            h = 2 * p + j
            hm = jnp.where((lane >= j * SSM_P) & (lane < (j + 1) * SSM_P), 1.0, 0.0).astype(F32)
            ch, ch_t, ce = cum[:, h:h + 1], cum_t[h:h + 1, :], cum[CHUNK - 1:CHUNK, h:h + 1]
            seg = _masked_exp(ch - ch_t, tri)
            vm = xp * dt[:, h:h + 1] * hm
            yp = yp + _bdot(gmats[g] * seg, vm) + _bdot(cg * jnp.exp(ch), sp * hm) + d_skip[:, h:h + 1] * xp * hm
            sn = sn + _bdot_tn(bg * jnp.exp(ce - ch), vm)
            decay = decay + jnp.exp(ce) * hm
        ys.append(yp)
        new_states.append(sn + sp * decay)
    y = jnp.concatenate(ys, axis=1) * _silu(z)
    gw = MIX_W // 2
    y = jnp.concatenate([_rms(y[:, g * gw:(g + 1) * gw], norm_w[:, g * gw:(g + 1) * gw]) for g in range(2)], axis=1)
    return (y,), jnp.stack(new_states)


def _gdn_chunk(tiles, halos, state, consts):
    qkv_raw, z, small = tiles
    (halo,) = halos
    conv_w, dt_bias, a_log, norm_w = consts
    qkv = _silu(_causal_conv(halo, qkv_raw, conv_w))
    beta_all = jax.nn.sigmoid(small)
    cum = _cumsum_rows(-jnp.exp(a_log) * jax.nn.softplus(small + dt_bias))
    cum_t = cum.T
    tri, tri_strict = _tri_mask(CHUNK), _tri_mask(CHUNK, strict=True)
    outs, new_states = [], []
    for h in range(N_HEAD6):
        q = qkv[:, h * HEAD:(h + 1) * HEAD]
        k = qkv[:, MIX_W + h * HEAD:MIX_W + (h + 1) * HEAD]
        v = qkv[:, 2 * MIX_W + h * HEAD:2 * MIX_W + (h + 1) * HEAD]
        q = q * lax.rsqrt(jnp.sum(q * q, axis=-1, keepdims=True) + NORM_EPS) * (HEAD ** -0.5)
        k = k * lax.rsqrt(jnp.sum(k * k, axis=-1, keepdims=True) + NORM_EPS)
        beta = beta_all[:, GB_OFF + h:GB_OFF + h + 1]
        c, c_t = cum[:, GA_OFF + h:GA_OFF + h + 1], cum_t[GA_OFF + h:GA_OFF + h + 1, :]
        ce = cum[CHUNK - 1:CHUNK, GA_OFF + h:GA_OFF + h + 1]
        decay = _masked_exp(c - c_t, tri)
        a_low = jnp.where(tri_strict, beta * _bdot_nt(k, k) * decay, 0.0)
        rhs = jnp.concatenate([v * beta, k * (beta * jnp.exp(c))], axis=1)
        sol = _unit_lower_solve(a_low, rhs)
        u_base, w_corr = sol[:, :HEAD], sol[:, HEAD:]
        qk = _bdot_nt(q, k) * decay
        st = state[h]
        u = u_base - _bdot(w_corr, st)
        o = _bdot(q * jnp.exp(c), st) + _bdot(qk, u)
        new_states.append(jnp.exp(ce) * st + _bdot_tn(k * jnp.exp(ce - c), u))
        outs.append(_rms(o, norm_w) * _silu(z[:, h * HEAD:(h + 1) * HEAD]))
    return (jnp.concatenate(outs, axis=1),), jnp.stack(new_states)


def _scan_fwd(name, fn, tiled, halo_idx, consts, out_width, state_shape):
    seq = tiled[0][0].shape[0]
    nc = seq // CHUNK
    n_t, n_h, n_c = len(tiled), len(halo_idx), len(consts)

    def body(*refs):
        t_refs, h_refs, c_refs = refs[:n_t], refs[n_t:n_t + n_h], refs[n_t + n_h:n_t + n_h + n_c]
        y_ref, save_ref, st_ref = refs[n_t + n_h + n_c:]
        i = pl.program_id(0)

        @pl.when(i == 0)
        def _():
            st_ref[...] = jnp.zeros_like(st_ref)

        flag = jnp.where(i > 0, 1.0, 0.0).astype(F32)
        st = st_ref[...]
        (y,), new = fn([r[...] for r in t_refs], [r[...] * flag for r in h_refs], st, [r[...] for r in c_refs])
        save_ref[0] = st
        y_ref[...] = y.astype(y_ref.dtype)
        st_ref[...] = new

    in_specs = [pl.BlockSpec((CHUNK, w), functools.partial(lambda i, cb: (i, cb), cb=cb)) for _, w, cb in tiled]
    in_specs += [pl.BlockSpec((HALO, tiled[j][1]),
                              functools.partial(lambda i, cb: (jnp.maximum(i * (CHUNK // HALO) - 1, 0), cb), cb=tiled[j][2]))
                 for j in halo_idx]
    in_specs += [pl.BlockSpec(c.shape, functools.partial(lambda i, nd: (0,) * nd, nd=c.ndim)) for c in consts]
    zeros = (0,) * len(state_shape)
    return _pc(
        body, name=name, grid=(nc,), in_specs=in_specs,
        out_specs=(pl.BlockSpec((CHUNK, out_width), lambda i: (i, 0)),
                   pl.BlockSpec((1,) + state_shape, lambda i: (i,) + zeros)),
        out_shape=(jax.ShapeDtypeStruct((seq, out_width), BF), jax.ShapeDtypeStruct((nc,) + state_shape, F32)),
        scratch_shapes=[pltpu.VMEM(state_shape, F32)],
        compiler_params=_cparams(("arbitrary",)),
    )(*[t[0] for t in tiled], *[tiled[j][0] for j in halo_idx], *consts)


def _scan_bwd(name, fn, tiled, halo_idx, consts, saved, dy, dtile_dtypes, extra=None):
    seq = tiled[0][0].shape[0]
    nc = seq // CHUNK
    n_t, n_h, n_c = len(tiled), len(halo_idx), len(consts)
    state_shape = saved.shape[1:]
    n_x = 0 if extra is None else 1

    def body(*refs):
        t_refs, h_refs, c_refs = refs[:n_t], refs[n_t:n_t + n_h], refs[n_t + n_h:n_t + n_h + n_c]
        pos = n_t + n_h + n_c
        save_ref, dy_ref = refs[pos], refs[pos + 1]
        x_refs = refs[pos + 2:pos + 2 + n_x]
        pos += 2 + n_x
        dt_refs, dc_refs = refs[pos:pos + n_t], refs[pos + n_t:pos + n_t + n_c]
        dst_ref = refs[pos + n_t + n_c]
        carry_refs = refs[pos + n_t + n_c + 1:]
        i = pl.program_id(0)

        @pl.when(i == 0)
        def _():
            dst_ref[...] = jnp.zeros_like(dst_ref)
            for r in carry_refs:
                r[...] = jnp.zeros_like(r)
            for r in dc_refs:
                r[...] = jnp.zeros_like(r)

        flag = jnp.where(i < nc - 1, 1.0, 0.0).astype(F32)
        tiles = [r[...] for r in t_refs]
        halos = [r[...] * flag for r in h_refs]
        cvals = [r[...] for r in c_refs]
        _, vjp = jax.vjp(fn, tiles, halos, save_ref[0], cvals)
        d_tiles, d_halos, d_state, d_consts = vjp(((dy_ref[...].astype(F32),), dst_ref[...]))
        dst_ref[...] = d_state
        for r, g in zip(dc_refs, d_consts):
            r[...] += g
        for j, (r, g) in enumerate(zip(dt_refs, d_tiles)):
            if extra is not None and extra[0] == j:
                g = g + x_refs[0][...].astype(F32)
            r[...] = g.astype(r.dtype)
            if j in halo_idx:
                cr = carry_refs[halo_idx.index(j)]
                r[CHUNK - HALO:CHUNK, :] = (g[CHUNK - HALO:CHUNK, :] + cr[...]).astype(r.dtype)
                cr[...] = d_halos[halo_idx.index(j)] * flag

    rev = lambda i: nc - 1 - i
    in_specs = [pl.BlockSpec((CHUNK, w), functools.partial(lambda i, cb: (rev(i), cb), cb=cb)) for _, w, cb in tiled]
    in_specs += [pl.BlockSpec((HALO, tiled[j][1]),
                              functools.partial(lambda i, cb: (jnp.maximum(rev(i) * (CHUNK // HALO) - 1, 0), cb), cb=tiled[j][2]))
                 for j in halo_idx]
    in_specs += [pl.BlockSpec(c.shape, functools.partial(lambda i, nd: (0,) * nd, nd=c.ndim)) for c in consts]
    zeros = (0,) * len(state_shape)
    in_specs += [pl.BlockSpec((1,) + state_shape, lambda i: (rev(i),) + zeros),
                 pl.BlockSpec((CHUNK, dy.shape[1]), lambda i: (rev(i), 0))]
    args = [t[0] for t in tiled] + [tiled[j][0] for j in halo_idx] + list(consts) + [saved, dy]
    if extra is not None:
        in_specs.append(pl.BlockSpec((CHUNK, extra[1].shape[1]), lambda i: (rev(i), 0)))
        args.append(extra[1])
    out_specs = [pl.BlockSpec((CHUNK, w), lambda i: (rev(i), 0)) for _, w, _ in tiled]
    out_specs += [pl.BlockSpec(c.shape, functools.partial(lambda i, nd: (0,) * nd, nd=c.ndim)) for c in consts]
    out_shape = [jax.ShapeDtypeStruct((seq, w), dtd) for (_, w, _), dtd in zip(tiled, dtile_dtypes)]
    out_shape += [jax.ShapeDtypeStruct(c.shape, F32) for c in consts]
    scratch = [pltpu.VMEM(state_shape, F32)] + [pltpu.VMEM((HALO, tiled[j][1]), F32) for j in halo_idx]
    return _pc(body, name=name, grid=(nc,), in_specs=in_specs, out_specs=tuple(out_specs), out_shape=tuple(out_shape),
               scratch_shapes=scratch, compiler_params=_cparams(("arbitrary",)))(*args)


def _tile_fwd(name, fn, tiled, consts, outs, tm=TOKEN_TILE):
    seq = tiled[0][0].shape[0]
    n_t, n_c = len(tiled), len(consts)

    def body(*refs):
        res = fn(*[r[...] for r in refs[:n_t + n_c]])
        for r, y in zip(refs[n_t + n_c:], res):
            r[...] = y.astype(r.dtype)

    in_specs = [pl.BlockSpec((tm, w), functools.partial(lambda i, cb: (i, cb), cb=cb)) for _, w, cb in tiled]
    in_specs += [pl.BlockSpec(c.shape, functools.partial(lambda i, nd: (0,) * nd, nd=c.ndim)) for c in consts]
    return _pc(body, name=name, grid=(seq // tm,), in_specs=in_specs,
               out_specs=tuple(pl.BlockSpec((tm, w), lambda i: (i, 0)) for w, _ in outs),
               out_shape=tuple(jax.ShapeDtypeStruct((seq, w), dtp) for w, dtp in outs),
               compiler_params=_cparams(("arbitrary",)))(*[t[0] for t in tiled], *consts)


def _tile_bwd(name, fn, tiled, consts, douts, dtile_dtypes, add_to=None, tm=TOKEN_TILE):
    seq = tiled[0][0].shape[0]
    n_t, n_c, n_o = len(tiled), len(consts), len(douts)
    n_x = 0 if add_to is None else 1
    keep = [j for j, dtp in enumerate(dtile_dtypes) if dtp is not None]

    def body(*refs):
        vals = [r[...].astype(F32) for r in refs[:n_t + n_c]]
        pos = n_t + n_c
        g_refs, x_refs = refs[pos:pos + n_o], refs[pos + n_o:pos + n_o + n_x]
        pos += n_o + n_x
        dt_refs, dc_refs = refs[pos:pos + len(keep)], refs[pos + len(keep):]
        i = pl.program_id(0)

        @pl.when(i == 0)
        def _():
            for r in dc_refs:
                r[...] = jnp.zeros_like(r)

        _, vjp = jax.vjp(fn, *vals)
        cts = vjp(tuple(g[...].astype(F32) for g in g_refs))
        for r, j in zip(dt_refs, keep):
            g = cts[j]
            if add_to is not None and add_to[0] == j:
                g = g + x_refs[0][...].astype(F32)
            r[...] = g.astype(r.dtype)
        for r, g in zip(dc_refs, cts[n_t:]):
            r[...] += g

    in_specs = [pl.BlockSpec((tm, w), functools.partial(lambda i, cb: (i, cb), cb=cb)) for _, w, cb in tiled]
    in_specs += [pl.BlockSpec(c.shape, functools.partial(lambda i, nd: (0,) * nd, nd=c.ndim)) for c in consts]
    in_specs += [pl.BlockSpec((tm, g.shape[1]), lambda i: (i, 0)) for g in douts]
    args = [t[0] for t in tiled] + list(consts) + list(douts)
    if add_to is not None:
        in_specs.append(pl.BlockSpec((tm, add_to[1].shape[1]), lambda i: (i, 0)))
        args.append(add_to[1])
    out_specs = [pl.BlockSpec((tm, tiled[j][1]), lambda i: (i, 0)) for j in keep]
    out_specs += [pl.BlockSpec(c.shape, functools.partial(lambda i, nd: (0,) * nd, nd=c.ndim)) for c in consts]
    out_shape = [jax.ShapeDtypeStruct((seq, tiled[j][1]), dtile_dtypes[j]) for j in keep]
    out_shape += [jax.ShapeDtypeStruct(c.shape, F32) for c in consts]
    return _pc(body, name=name, grid=(seq // tm,), in_specs=in_specs, out_specs=tuple(out_specs),
               out_shape=tuple(out_shape), compiler_params=_cparams(("arbitrary",)))(*args)


def _lnmod_fn(x, nw, sc, sh):
    return (_rms(x, nw) * (1.0 + sc) + sh,)


def _resid_fn(x, o, g):
    return (x + (1.0 + g) * o,)


def _swiglu_fn(gu):
    return (_silu(gu[:, :FFN_H]) * gu[:, FFN_H:],)


def _merge_fn(yh, ys, yg, logits, wb, b_merge):
    gates = jax.nn.sigmoid(logits + b_merge)
    acc = None
    for n, y in enumerate((yh, ys, yg)):
        t = gates[:, n * D_MODEL:(n + 1) * D_MODEL] * _bdot(y, wb[n])
        acc = t if acc is None else acc + t
    return (acc,)


MM_VMEM_BUDGET = 40 * 1024 * 1024
MM_TILE_CAP = 1024
MM_K_CAP = 4096


def _divisor(n, cap, unit=LANES):
    best = None
    for d in range(unit, min(n, cap) + 1, unit):
        if n % d == 0:
            best = d
    return n if best is None else best


def _mm_tiles(m, n, k, out_bytes):
    tk = k if k <= MM_K_CAP else _divisor(k, 3072)
    tm, tn = _divisor(m, MM_TILE_CAP), _divisor(n, MM_TILE_CAP + MM_TILE_CAP // 2)

    def need(tm_, tn_):
        acc = tm_ * tn_ * 4 if tk < k else 0
        return 2 * 2 * tk * (tm_ + tn_) + acc + 2 * tm_ * tn_ * out_bytes

    while need(tm, tn) > MM_VMEM_BUDGET:
        if tn >= tm and _divisor(n, tn - LANES) < tn:
            tn = _divisor(n, tn - LANES)
        elif _divisor(m, tm - LANES) < tm:
            tm = _divisor(m, tm - LANES)
        else:
            break
    return tm, tn, tk


def _mm(a, b, mode, out_dtype, name):
    if mode == "nn":
        (m, k), n = a.shape, b.shape[1]
    elif mode == "nt":
        (m, k), n = a.shape, b.shape[0]
    else:
        (k, m), n = a.shape, b.shape[1]
    tm, tn, tk = _mm_tiles(m, n, k, jnp.dtype(out_dtype).itemsize)
    nk = k // tk
    dims = {"nn": ((1,), (0,)), "nt": ((1,), (1,)), "tn": ((0,), (0,))}[mode]

    def body_one(a_ref, b_ref, o_ref):
        o_ref[...] = lax.dot_general(a_ref[...], b_ref[...], (dims, ((), ())), preferred_element_type=F32).astype(o_ref.dtype)

    def body_acc(a_ref, b_ref, o_ref, acc_ref):
        kk = pl.program_id(2)

        @pl.when(kk == 0)
        def _():
            acc_ref[...] = jnp.zeros_like(acc_ref)

        acc_ref[...] += lax.dot_general(a_ref[...], b_ref[...], (dims, ((), ())), preferred_element_type=F32)

        @pl.when(kk == nk - 1)
        def _():
            o_ref[...] = acc_ref[...].astype(o_ref.dtype)

    a_spec = pl.BlockSpec((tk, tm), lambda i, j, kk: (kk, i)) if mode == "tn" else pl.BlockSpec((tm, tk), lambda i, j, kk: (i, kk))
    b_spec = pl.BlockSpec((tn, tk), lambda i, j, kk: (j, kk)) if mode == "nt" else pl.BlockSpec((tk, tn), lambda i, j, kk: (kk, j))
    return _pc(body_one if nk == 1 else body_acc, name=name, grid=(m // tm, n // tn, nk), in_specs=[a_spec, b_spec],
               out_specs=pl.BlockSpec((tm, tn), lambda i, j, kk: (i, j)),
               out_shape=jax.ShapeDtypeStruct((m, n), out_dtype),
               scratch_shapes=[] if nk == 1 else [pltpu.VMEM((tm, tn), F32)],
               compiler_params=_cparams(("parallel", "parallel", "arbitrary")))(a.astype(BF), b.astype(BF))


def _final_loss(x, tgt, norm_final, tm=TOKEN_TILE):
    seq = x.shape[0]

    def fn(xv, nf, tv):
        err = jnp.square(_rms(xv, nf) - tv)
        return 0.5 * jnp.sum(jnp.mean(err, axis=-1))

    def body(x_ref, t_ref, nf_ref, loss_ref, dx_ref, dnf_ref):
        i = pl.program_id(0)

        @pl.when(i == 0)
        def _():
            loss_ref[...] = jnp.zeros_like(loss_ref)
            dnf_ref[...] = jnp.zeros_like(dnf_ref)

        val, vjp = jax.vjp(functools.partial(fn, tv=t_ref[...]), x_ref[...], nf_ref[...])
        dx, dnf = vjp(jnp.ones((), F32))
        dx_ref[...] = dx
        dnf_ref[...] += dnf
        loss_ref[...] += jnp.broadcast_to(val, loss_ref.shape)

    return _pc(body, name="final_loss", grid=(seq // tm,),
               in_specs=[pl.BlockSpec((tm, D_MODEL), lambda i: (i, 0)), pl.BlockSpec((tm, D_MODEL), lambda i: (i, 0)),
                         pl.BlockSpec((1, D_MODEL), lambda i: (0, 0))],
               out_specs=(pl.BlockSpec((8, LANES), lambda i: (0, 0)), pl.BlockSpec((tm, D_MODEL), lambda i: (i, 0)),
                          pl.BlockSpec((1, D_MODEL), lambda i: (0, 0))),
               out_shape=(jax.ShapeDtypeStruct((8, LANES), F32), jax.ShapeDtypeStruct((seq, D_MODEL), F32),
                          jax.ShapeDtypeStruct((1, D_MODEL), F32)),
               compiler_params=_cparams(("arbitrary",)))(x, tgt, norm_final)


def _ada_fwd(c_all, w_ada, b_ada_cols):
    n_l, _, cols = w_ada.shape

    def body(c_ref, w_ref, b_ref, o_ref):
        o_ref[0] = jnp.dot(_silu(c_ref[...]), w_ref[0], preferred_element_type=F32, precision=HI) + b_ref[0]

    return _pc(body, name="ada_fwd", grid=(n_l,),
               in_specs=[pl.BlockSpec((8, D_MODEL), lambda l: (0, 0)), pl.BlockSpec((1, D_MODEL, cols), lambda l: (l, 0, 0)),
                         pl.BlockSpec((1, 1, cols), lambda l: (l, 0, 0))],
               out_specs=pl.BlockSpec((1, 8, cols), lambda l: (l, 0, 0)),
               out_shape=jax.ShapeDtypeStruct((n_l, 8, cols), F32),
               compiler_params=_cparams(("arbitrary",)))(c_all, w_ada, b_ada_cols.reshape(n_l, 1, cols))


def _ada_bwd(c_all, dmod_cols):
    n_l, _, cols = dmod_cols.shape

    def body(c_ref, g_ref, o_ref):
        o_ref[0] = lax.dot_general(_silu(c_ref[...]), g_ref[0], (((0,), (0,)), ((), ())), preferred_element_type=F32,
                                   precision=HI)

    return _pc(body, name="ada_bwd", grid=(n_l,),
               in_specs=[pl.BlockSpec((8, D_MODEL), lambda l: (0, 0)), pl.BlockSpec((1, 8, cols), lambda l: (l, 0, 0))],
               out_specs=pl.BlockSpec((1, D_MODEL, cols), lambda l: (l, 0, 0)),
               out_shape=jax.ShapeDtypeStruct((n_l, D_MODEL, cols), F32),
               compiler_params=_cparams(("arbitrary",)))(c_all, dmod_cols)


def _lb_fn(logits):
    e = jnp.exp(logits - jnp.max(logits, axis=0, keepdims=True))
    p = e / jnp.sum(e, axis=0, keepdims=True)
    r = lax.broadcasted_iota(jnp.int32, (DEPTH, 1), 0)
    lb = jnp.zeros_like(p)
    for j in range(1, DEPTH):
        lb = lb + jnp.where(r >= j, p[j:j + 1, :], 0.0)
    return lb


def _lb_fwd(logits):
    def body(l_ref, o_ref):
        o_ref[...] = _lb_fn(l_ref[...])

    return _pc(body, name="lb_fwd", out_shape=jax.ShapeDtypeStruct(logits.shape, F32))(logits)


def _lb_bwd(logits, dlb):
    def body(l_ref, g_ref, o_ref):
        _, vjp = jax.vjp(_lb_fn, l_ref[...])
        o_ref[...] = vjp(g_ref[...])[0]

    return _pc(body, name="lb_bwd", out_shape=jax.ShapeDtypeStruct(logits.shape, F32))(logits, dlb)


def _rows_for(n_rows, n_cols):
    r = 8
    while r * 2 <= n_rows and n_rows % (r * 2) == 0 and r * 2 * n_cols <= 256 * 1024:
        r *= 2
    return r if n_rows % r == 0 else n_rows


def _ew(name, fn, ins, out_dtypes):
    n_rows, n_cols = ins[0].shape
    tr = _rows_for(n_rows, n_cols)
    n_in = len(ins)

    def body(*refs):
        res = fn(*[r[...] for r in refs[:n_in]])
        for r, y in zip(refs[n_in:], res):
            r[...] = y.astype(r.dtype)

    spec = pl.BlockSpec((tr, n_cols), lambda i: (i, 0))
    return _pc(body, name=name, grid=(n_rows // tr,), in_specs=[spec] * n_in, out_specs=tuple([spec] * len(out_dtypes)),
               out_shape=tuple(jax.ShapeDtypeStruct((n_rows, n_cols), d) for d in out_dtypes),
               compiler_params=_cparams(("arbitrary",)))(*ins)


def _adamw_fn(w, g, m, v):
    m = ADAM_B1 * m + (1.0 - ADAM_B1) * g
    v = ADAM_B2 * v + (1.0 - ADAM_B2) * jnp.square(g)
    m_hat = m / (1.0 - ADAM_B1 ** ADAM_STEP)
    v_hat = v / (1.0 - ADAM_B2 ** ADAM_STEP)
    return -ADAM_LR * (m_hat / (jnp.sqrt(v_hat) + ADAM_EPS) + ADAM_WD * w), m, v


def _adamw(name, w, g, m, v):
    shape = w.shape
    two = (-1, shape[-1])
    d, nm, nv = _ew(name, _adamw_fn, [a.reshape(two) for a in (w, g, m, v)], [F32, F32, F32])
    return d.reshape(shape), nm.reshape(shape), nv.reshape(shape)


def _sum_leading(name, a, out_dtype):
    n, n_rows, n_cols = a.shape
    tr = _rows_for(n_rows, n_cols)

    def body(a_ref, o_ref):
        acc = a_ref[0].astype(F32)
        for j in range(1, n):
            acc = acc + a_ref[j].astype(F32)
        o_ref[...] = acc.astype(o_ref.dtype)

    return _pc(body, name=name, grid=(n_rows // tr,), in_specs=[pl.BlockSpec((n, tr, n_cols), lambda i: (0, i, 0))],
               out_specs=pl.BlockSpec((tr, n_cols), lambda i: (i, 0)),
               out_shape=jax.ShapeDtypeStruct((n_rows, n_cols), out_dtype),
               compiler_params=_cparams(("arbitrary",)))(a)


MESH = pl.DeviceIdType.MESH
ANY = pl.BlockSpec(memory_space=pl.ANY)


def _place():
    return lax.axis_index("x"), lax.axis_index("y"), lax.axis_index("c")


def _all_gather_small(name, a):
    m_per, n = a.shape

    def body(x_ref, out_ref, send_sems, recv_sems, local_sem):
        x, y, c = _place()
        me, sibling = (x, y, c), (x, y, 1 - c)
        chips = [(1 - x, y), (x, 1 - y), (1 - x, 1 - y)]

        def rows(px, py, pc):
            return out_ref.at[pl.ds((4 * px + 2 * py + pc) * m_per, m_per), :]

        def copy(k, block, to, src=None):
            return pltpu.make_async_remote_copy(src_ref=rows(*block) if src is None else src, dst_ref=rows(*block),
                                                send_sem=send_sems.at[k], recv_sem=recv_sems.at[k], device_id=to,
                                                device_id_type=MESH)

        mine = pltpu.make_async_copy(x_ref, rows(*me), local_sem)
        mine.start()
        first = [copy(0, me, sibling, src=x_ref)]
        first += [copy(1 + j, me, (*chip, c), src=x_ref) for j, chip in enumerate(chips)]
        for cp in first:
            cp.start()
        passed = [copy(4 + j, (*chip, c), sibling) for j, chip in enumerate(chips)]
        for j, chip in enumerate(chips):
            copy(1 + j, (*chip, c), me).wait_recv()
            passed[j].start()
        copy(0, sibling, me).wait_recv()
        for j, chip in enumerate(chips):
            copy(4 + j, (*chip, 1 - c), me).wait_recv()
        for cp in first + passed:
            cp.wait_send()
        mine.wait()

    out = _pc(body, name=name, out_shape=jax.ShapeDtypeStruct((8 * m_per, n), a.dtype),
              in_specs=[pl.BlockSpec(memory_space=pltpu.VMEM)], out_specs=pl.BlockSpec(memory_space=pltpu.VMEM),
              scratch_shapes=[pltpu.SemaphoreType.DMA((7,)), pltpu.SemaphoreType.DMA((7,)), pltpu.SemaphoreType.DMA],
              compiler_params=pltpu.CompilerParams(vmem_limit_bytes=VMEM_LIMIT))(a)
    return out.reshape(8, m_per, n)


def _chip_gather(name, pack):
    n_l, n_r, n_c = pack.shape
    half = n_r // 2

    def body(p_ref, o_ref, send_sems, recv_sems):
        x, y, c = _place()
        sibling = (x, y, 1 - c)
        chips = [(1 - x, y), (x, 1 - y), (1 - x, 1 - y)]

        def slab(px, py, pc):
            return o_ref.at[2 * px + py, :, pl.ds(pc * half, half), :]

        def copy(k, src, dst, to):
            return pltpu.make_async_remote_copy(src_ref=src, dst_ref=dst, send_sem=send_sems.at[k], recv_sem=recv_sems.at[k],
                                                device_id=to, device_id_type=MESH)

        first = [copy(j, p_ref.at[:, pl.ds(c * half, half), :], slab(x, y, c), (*chip, c)) for j, chip in enumerate(chips)]
        for cp in first:
            cp.start()
        passed = [copy(3 + j, slab(*chip, c), slab(*chip, c), sibling) for j, chip in enumerate(chips)]
        for j, chip in enumerate(chips):
            copy(j, slab(*chip, c), slab(*chip, c), (*chip, c)).wait_recv()
            passed[j].start()
        for j, chip in enumerate(chips):
            copy(3 + j, slab(*chip, 1 - c), slab(*chip, 1 - c), sibling).wait_recv()
        for cp in first + passed:
            cp.wait_send()

    return _pc(body, name=name, out_shape=jax.ShapeDtypeStruct((4, n_l, n_r, n_c), pack.dtype), in_specs=[ANY], out_specs=ANY,
               scratch_shapes=[pltpu.SemaphoreType.DMA((6,)), pltpu.SemaphoreType.DMA((6,))])(pack)


def _pair_swap(name, give):
    def body(g_ref, o_ref, send_sem, recv_sem):
        x, y, c = _place()
        cp = pltpu.make_async_remote_copy(src_ref=g_ref, dst_ref=o_ref, send_sem=send_sem, recv_sem=recv_sem,
                                          device_id=(x, y, 1 - c), device_id_type=MESH)
        cp.start()
        cp.wait()

    return _pc(body, name=name, out_shape=jax.ShapeDtypeStruct(give.shape, give.dtype), in_specs=[ANY], out_specs=ANY,
               scratch_shapes=[pltpu.SemaphoreType.DMA, pltpu.SemaphoreType.DMA])(give)


def _chip_exchange(name, parts):
    def body(p_ref, o_ref, send_sems, recv_sems):
        x, y, c = _place()
        me = 2 * x + y
        chips = [(1 - x, y), (x, 1 - y), (1 - x, 1 - y)]

        def copy(k, src, dst, to):
            return pltpu.make_async_remote_copy(src_ref=src, dst_ref=dst, send_sem=send_sems.at[k], recv_sem=recv_sems.at[k],
                                                device_id=to, device_id_type=MESH)

        sends = [copy(j, p_ref.at[2 * px + py], o_ref.at[me], (px, py, c)) for j, (px, py) in enumerate(chips)]
        for cp in sends:
            cp.start()
        for j, (px, py) in enumerate(chips):
            copy(j, p_ref.at[2 * px + py], o_ref.at[2 * px + py], (px, py, c)).wait_recv()
        for cp in sends:
            cp.wait_send()

    return _pc(body, name=name, out_shape=jax.ShapeDtypeStruct(parts.shape, parts.dtype), in_specs=[ANY], out_specs=ANY,
               scratch_shapes=[pltpu.SemaphoreType.DMA((3,)), pltpu.SemaphoreType.DMA((3,))])(parts)


N_CHIP = 4
PACK_C = 1024
BIG = (("w_in", (1024, 2822), 1), ("w_branch", (3, 768, 256), 2), ("w_out", (256, 1024), 0),
       ("w_ffn_in", (1024, 1408), 1), ("w_ffn_out", (704, 1024), 0))
BIG_ROWS = tuple(math.prod(s) // PACK_C for _, s, _ in BIG)
PACK_R = 5792
G768 = ((0, 3072), (3072, 3840), (7436, 8204))
GXBC, GQKV, GGATE = (3840, 5120), (5132, 7436), (8216, 11288)
GSMALL = ((5120, 5132), (8204, 8210), (8210, 8216))
W768, WXBC, WGATE = 4608, CONV_CH, 3 * D_MODEL
IN_PAD = W768 + WXBC + QKV_W + WGATE + SMALL_W


BIG_ROWS_PAD = tuple(-(-r // 16) * 16 for r in BIG_ROWS)


def _pack_layer(parts, dtype):
    rows = []
    for p, n, n_pad in zip(parts, BIG_ROWS, BIG_ROWS_PAD):
        rows.append(jnp.pad(p.astype(dtype).reshape(n, PACK_C), ((0, n_pad - n), (0, 0))))
    rows.append(jnp.zeros((PACK_R - sum(BIG_ROWS_PAD), PACK_C), dtype))
    return jnp.concatenate(rows, axis=0)


def _unpack_layer(rows):
    out, r0 = [], 0
    for (_, shape, _), n, n_pad in zip(BIG, BIG_ROWS, BIG_ROWS_PAD):
        out.append(rows[r0:r0 + n].reshape(shape))
        r0 += n_pad
    return out


def _regroup_w_in(w):
    cat = lambda spans: jnp.concatenate([w[:, a:b] for a, b in spans], axis=1)
    small = jnp.concatenate([cat(GSMALL), jnp.zeros((w.shape[0], SMALL_W - 24), w.dtype)], axis=1)
    return cat(G768), cat((GXBC,)), cat((GQKV,)), cat((GGATE,)), small


def _ungroup_w_in(d):
    o_xbc, o_qkv, o_gate, o_small = W768, W768 + WXBC, W768 + WXBC + QKV_W, W768 + WXBC + QKV_W + WGATE
    spans = ((0, 3072), (3072, 3840), (o_xbc, o_xbc + WXBC), (o_small, o_small + 12), (o_qkv, o_qkv + QKV_W),
             (3840, 4608), (o_small + 12, o_small + 18), (o_small + 18, o_small + 24), (o_gate, o_gate + WGATE))
    return jnp.concatenate([d[:, a:b] for a, b in spans], axis=1)


def _lane_pad(v, off):
    return jnp.pad(v, (off, LANES - off - v.shape[0]))[None, :]


STATE6 = (N_HEAD6, HEAD, HEAD)


def _mixer_inputs(sv, lp):
    p768, pxbc, pqkv, psmall = sv["p768"], sv["pxbc"], sv["pqkv"], sv["psmall"]
    hgrn = ([(p768, MIX_W, j) for j in range(4)], [], [lp["lb"], lp["hgrn_norm"]])
    ssd = ([(p768, MIX_W, 4), (pxbc, CONV_CH, 0), (psmall, LANES, 0)], [1],
           [lp["ssm_conv_w"], lp["ssm_conv_b"], lp["ssm_dt_bias"], lp["ssm_a_log"], lp["ssm_d"], lp["ssm_norm"]])
    gdn = ([(pqkv, QKV_W, 0), (p768, MIX_W, 5), (psmall, LANES, 0)], [0],
           [lp["gdn_conv_w"], lp["gdn_dt_bias"], lp["gdn_a_log"], lp["gdn_norm"]])
    return hgrn, ssd, gdn


def _layer_fwd(x, md, lw, lp):
    sv = {"x": x}
    (sv["h1"],) = _tile_fwd("lnmod1", _lnmod_fn, [(x, D_MODEL, 0)], [lp["norm_mix"], md["sc1"], md["sh1"]], [(D_MODEL, BF)])
    for nm in ("768", "xbc", "qkv", "gate", "small"):
        sv["p" + nm] = _mm(sv["h1"], lw["win_" + nm], "nn", F32, "proj_" + nm)
    hgrn, ssd, gdn = _mixer_inputs(sv, lp)
    sv["y_h"], sv["st_h"] = _scan_fwd("hgrn_fwd", _hgrn_chunk, *hgrn, MIX_W, STATE6)
    sv["y_s"], sv["st_s"] = _scan_fwd("ssd_fwd", _ssd_chunk, *ssd, MIX_W, STATE6)
    sv["y_g"], sv["st_g"] = _scan_fwd("gdn_fwd", _gdn_chunk, *gdn, MIX_W, STATE6)
    (sv["merged"],) = _tile_fwd("merge", _merge_fn, _merge_tiles(sv), [lw["w_branch"], lp["b_merge"]], [(D_MODEL, BF)])
    sv["out"] = _mm(sv["merged"], lw["w_out"], "nn", F32, "out_proj")
    (sv["x_mid"],) = _tile_fwd("resid1", _resid_fn, [(x, D_MODEL, 0), (sv["out"], D_MODEL, 0)], [md["g1"]], [(D_MODEL, F32)])
    (sv["h2"],) = _tile_fwd("lnmod2", _lnmod_fn, [(sv["x_mid"], D_MODEL, 0)], [lp["norm_ffn"], md["sc2"], md["sh2"]],
                            [(D_MODEL, BF)])
    sv["gu"] = _mm(sv["h2"], lw["w_ffn_in"], "nn", F32, "ffn_in")
    (sv["act"],) = _tile_fwd("swiglu", _swiglu_fn, [(sv["gu"], 2 * FFN_H, 0)], [], [(FFN_H, BF)])
    sv["o2"] = _mm(sv["act"], lw["w_ffn_out"], "nn", F32, "ffn_out")
    (x_out,) = _tile_fwd("resid2", _resid_fn, [(sv["x_mid"], D_MODEL, 0), (sv["o2"], D_MODEL, 0)], [md["g2"]], [(D_MODEL, F32)])
    return x_out, sv


def _merge_tiles(sv):
    return [(sv["y_h"], MIX_W, 0), (sv["y_s"], MIX_W, 0), (sv["y_g"], MIX_W, 0), (sv["pgate"], WGATE, 0)]


def _layer_bwd(dx_out, sv, md, lw, lp):
    g = {}
    x, x_mid = sv["x"], sv["x_mid"]
    d_xmid, d_o2, g["g2"] = _tile_bwd("resid2_b", _resid_fn, [(x_mid, D_MODEL, 0), (sv["o2"], D_MODEL, 0)], [md["g2"]], [dx_out],
                                      [F32, BF])
    d_act = _mm(d_o2, lw["w_ffn_out"], "nt", F32, "ffn_out_dx")
    g["w_ffn_out"] = _mm(sv["act"], d_o2, "tn", F32, "ffn_out_dw")
    (d_gu,) = _tile_bwd("swiglu_b", _swiglu_fn, [(sv["gu"], 2 * FFN_H, 0)], [], [d_act], [BF])
    d_h2 = _mm(d_gu, lw["w_ffn_in"], "nt", F32, "ffn_in_dx")
    g["w_ffn_in"] = _mm(sv["h2"], d_gu, "tn", F32, "ffn_in_dw")
    d_xmid, g["norm_ffn"], g["sc2"], g["sh2"] = _tile_bwd(
        "lnmod2_b", _lnmod_fn, [(x_mid, D_MODEL, 0)], [lp["norm_ffn"], md["sc2"], md["sh2"]], [d_h2], [F32], add_to=(0, d_xmid))
    d_x, d_out, g["g1"] = _tile_bwd("resid1_b", _resid_fn, [(x, D_MODEL, 0), (sv["out"], D_MODEL, 0)], [md["g1"]], [d_xmid],
                                    [F32, BF])
    d_merged = _mm(d_out, lw["w_out"], "nt", F32, "out_proj_dx")
    g["w_out"] = _mm(sv["merged"], d_out, "tn", F32, "out_proj_dw")
    d_yh, d_ys, d_yg, d_gate, g["w_branch"], g["b_merge"] = _tile_bwd(
        "merge_b", _merge_fn, _merge_tiles(sv), [lw["w_branch"], lp["b_merge"]], [d_merged], [F32, F32, F32, BF])
    hgrn, ssd, gdn = _mixer_inputs(sv, lp)
    d_q, d_f, d_v, d_g, g["lb"], g["hgrn_norm"] = _scan_bwd("hgrn_bwd", _hgrn_chunk, *hgrn, sv["st_h"], d_yh, [BF] * 4)
    (d_sz, d_xbc, d_small, g["ssm_conv_w"], g["ssm_conv_b"], g["ssm_dt_bias"], g["ssm_a_log"], g["ssm_d"],
     g["ssm_norm"]) = _scan_bwd("ssd_bwd", _ssd_chunk, *ssd, sv["st_s"], d_ys, [BF, BF, F32])
    d_qkv, d_gz, d_small, g["gdn_conv_w"], g["gdn_dt_bias"], g["gdn_a_log"], g["gdn_norm"] = _scan_bwd(
        "gdn_bwd", _gdn_chunk, *gdn, sv["st_g"], d_yg, [BF, BF, BF], extra=(2, d_small))
    d_proj = jnp.concatenate([d_q, d_f, d_v, d_g, d_sz, d_gz, d_xbc, d_qkv, d_gate, d_small,
                              jnp.zeros((x.shape[0], SMALL_W - LANES), BF)], axis=1)
    d_h1 = _mm(d_proj, lw["win_all"], "nt", F32, "proj_dx")
    g["w_in"] = _ungroup_w_in(_mm(sv["h1"], d_proj, "tn", F32, "proj_dw"))
    d_x, g["norm_mix"], g["sc1"], g["sh1"] = _tile_bwd(
        "lnmod1_b", _lnmod_fn, [(x, D_MODEL, 0)], [lp["norm_mix"], md["sc1"], md["sh1"]], [d_h1], [F32], add_to=(0, d_x))
    return d_x, g


SMALL_REPL = ("norm_mix", "norm_ffn", "b_merge", "hgrn_lb_logits", "hgrn_norm", "ssm_conv_w", "ssm_conv_b", "ssm_dt_bias",
              "ssm_a_log", "ssm_d", "ssm_norm", "gdn_conv_w", "gdn_dt_bias", "gdn_a_log", "gdn_norm", "norm_final")
WEIGHTS = ("w_ada", "b_ada", "norm_mix", "norm_ffn", "w_in", "b_merge", "hgrn_lb_logits", "hgrn_norm", "ssm_conv_w",
           "ssm_conv_b", "ssm_dt_bias", "ssm_a_log", "ssm_d", "ssm_norm", "gdn_conv_w", "gdn_dt_bias", "gdn_a_log",
           "gdn_norm", "w_branch", "w_out", "w_ffn_in", "w_ffn_out", "norm_final")
SMALL_ROWS = 120


def _pad_rows(flat, n_rows, n_cols):
    return jnp.concatenate([flat, jnp.zeros((n_rows * n_cols - flat.shape[0],), flat.dtype)]).reshape(n_rows, n_cols)


def _device_step(x, tgt, mod, lb, wfull, sp):
    mds, lps, svs = [], [], []
    h = x
    for l in range(DEPTH):
        md = {n: mod[l, i * D_MODEL:(i + 1) * D_MODEL][None, :] for i, n in enumerate(("sh1", "sc1", "g1", "sh2", "sc2", "g2"))}
        lp = {n: sp[n][l][None, :] for n in ("norm_mix", "norm_ffn", "b_merge", "hgrn_norm", "ssm_conv_b", "ssm_norm", "gdn_norm")}
        lp["lb"] = lb[l][None, :]
        lp["ssm_conv_w"], lp["gdn_conv_w"] = sp["ssm_conv_w"][l], sp["gdn_conv_w"][l]
        for n in ("ssm_dt_bias", "ssm_a_log", "ssm_d"):
            lp[n] = _lane_pad(sp[n][l], DT_OFF)
        for n in ("gdn_dt_bias", "gdn_a_log"):
            lp[n] = _lane_pad(sp[n][l], GA_OFF)
        h, sv = _layer_fwd(h, md, wfull[l], lp)
        mds.append(md), lps.append(lp), svs.append(sv)
    loss, dh, d_nf = _final_loss(h, tgt, sp["norm_final"][None, :])
    grads = [None] * DEPTH
    for l in reversed(range(DEPTH)):
        dh, grads[l] = _layer_bwd(dh, svs[l], mds[l], wfull[l], lps[l])
    return loss, dh, d_nf, grads


def kernel(x, c, w_ada, b_ada, norm_mix, norm_ffn, w_in, b_merge, hgrn_lb_logits, hgrn_norm, ssm_conv_w, ssm_conv_b, ssm_dt_bias, ssm_a_log, ssm_d, ssm_norm, gdn_conv_w, gdn_dt_bias, gdn_a_log, gdn_norm, w_branch, w_out, w_ffn_in, w_ffn_out, norm_final, loss_target, m_w_ada, m_b_ada, m_norm_mix, m_norm_ffn, m_w_in, m_b_merge, m_hgrn_lb_logits, m_hgrn_norm, m_ssm_conv_w, m_ssm_conv_b, m_ssm_dt_bias, m_ssm_a_log, m_ssm_d, m_ssm_norm, m_gdn_conv_w, m_gdn_dt_bias, m_gdn_a_log, m_gdn_norm, m_w_branch, m_w_out, m_w_ffn_in, m_w_ffn_out, m_norm_final, v_w_ada, v_b_ada, v_norm_mix, v_norm_ffn, v_w_in, v_b_merge, v_hgrn_lb_logits, v_hgrn_norm, v_ssm_conv_w, v_ssm_conv_b, v_ssm_dt_bias, v_ssm_a_log, v_ssm_d, v_ssm_norm, v_gdn_conv_w, v_gdn_dt_bias, v_gdn_a_log, v_gdn_norm, v_w_branch, v_w_out, v_w_ffn_in, v_w_ffn_out, v_norm_final):
    w = dict(w_ada=w_ada, b_ada=b_ada, norm_mix=norm_mix, norm_ffn=norm_ffn, w_in=w_in, b_merge=b_merge,
             hgrn_lb_logits=hgrn_lb_logits, hgrn_norm=hgrn_norm, ssm_conv_w=ssm_conv_w, ssm_conv_b=ssm_conv_b,
             ssm_dt_bias=ssm_dt_bias, ssm_a_log=ssm_a_log, ssm_d=ssm_d, ssm_norm=ssm_norm, gdn_conv_w=gdn_conv_w,
             gdn_dt_bias=gdn_dt_bias, gdn_a_log=gdn_a_log, gdn_norm=gdn_norm, w_branch=w_branch, w_out=w_out,
             w_ffn_in=w_ffn_in, w_ffn_out=w_ffn_out, norm_final=norm_final)
    m = dict(w_ada=m_w_ada, b_ada=m_b_ada, norm_mix=m_norm_mix, norm_ffn=m_norm_ffn, w_in=m_w_in, b_merge=m_b_merge,
             hgrn_lb_logits=m_hgrn_lb_logits, hgrn_norm=m_hgrn_norm, ssm_conv_w=m_ssm_conv_w, ssm_conv_b=m_ssm_conv_b,
             ssm_dt_bias=m_ssm_dt_bias, ssm_a_log=m_ssm_a_log, ssm_d=m_ssm_d, ssm_norm=m_ssm_norm, gdn_conv_w=m_gdn_conv_w,
             gdn_dt_bias=m_gdn_dt_bias, gdn_a_log=m_gdn_a_log, gdn_norm=m_gdn_norm, w_branch=m_w_branch, w_out=m_w_out,
             w_ffn_in=m_w_ffn_in, w_ffn_out=m_w_ffn_out, norm_final=m_norm_final)
    v = dict(w_ada=v_w_ada, b_ada=v_b_ada, norm_mix=v_norm_mix, norm_ffn=v_norm_ffn, w_in=v_w_in, b_merge=v_b_merge,
             hgrn_lb_logits=v_hgrn_lb_logits, hgrn_norm=v_hgrn_norm, ssm_conv_w=v_ssm_conv_w, ssm_conv_b=v_ssm_conv_b,
             ssm_dt_bias=v_ssm_dt_bias, ssm_a_log=v_ssm_a_log, ssm_d=v_ssm_d, ssm_norm=v_ssm_norm, gdn_conv_w=v_gdn_conv_w,
             gdn_dt_bias=v_gdn_dt_bias, gdn_a_log=v_gdn_a_log, gdn_norm=v_gdn_norm, w_branch=v_w_branch, w_out=v_w_out,
             w_ffn_in=v_w_ffn_in, w_ffn_out=v_w_ffn_out, norm_final=v_norm_final)
    xi, yi, ci = _place()
    chip, me = 2 * xi + yi, 4 * xi + 2 * yi + ci
    seq = x.shape[1]

    conv_flat = jnp.concatenate([ssm_conv_w.reshape(-1), gdn_conv_w.reshape(-1)])
    n_conv = conv_flat.shape[0]
    first = _all_gather_small("gather_c_conv", _pad_rows(jnp.concatenate([c[0], conv_flat]), 16, D_MODEL))
    c_all = first[:, 0, :]
    conv_all = first[0::2].reshape(N_CHIP, -1)[:, D_MODEL:D_MODEL + n_conv]
    n_ssm = ssm_conv_w.size
    sp = dict(w)
    sp["ssm_conv_w"] = jnp.concatenate([conv_all[j, :n_ssm].reshape(ssm_conv_w.shape) for j in range(N_CHIP)], axis=2)
    sp["gdn_conv_w"] = jnp.concatenate([conv_all[j, n_ssm:].reshape(gdn_conv_w.shape) for j in range(N_CHIP)], axis=2)

    ada_cols = w_ada.shape[2]
    mod_part = _ada_fwd(c_all, w_ada, lax.dynamic_slice_in_dim(b_ada, chip * ada_cols, ada_cols, axis=1))
    mod_all = _all_gather_small("gather_mod", mod_part.reshape(DEPTH * 8, ada_cols))[0::2].reshape(N_CHIP, DEPTH, 8, ada_cols)
    mod = lax.dynamic_index_in_dim(mod_all, me, axis=2, keepdims=False).transpose(1, 0, 2).reshape(DEPTH, N_CHIP * ada_cols)
    lb = _lb_fwd(hgrn_lb_logits)

    pack = jnp.stack([_pack_layer([w[n][l] for n, _, _ in BIG], BF) for l in range(DEPTH)])
    gathered = lax.dynamic_update_slice(_chip_gather("gather_weights", pack), pack[None], (chip, 0, 0, 0))
    wfull = []
    for l in range(DEPTH):
        shards = [_unpack_layer(gathered[j, l]) for j in range(N_CHIP)]
        full = {n: jnp.concatenate([shards[j][i] for j in range(N_CHIP)], axis=ax) for i, (n, _, ax) in enumerate(BIG)}
        for nm, part in zip(("768", "xbc", "qkv", "gate", "small"), _regroup_w_in(full["w_in"])):
            full["win_" + nm] = part
        full["win_all"] = jnp.concatenate([full["win_" + nm] for nm in ("768", "xbc", "qkv", "gate", "small")], axis=1)
        wfull.append(full)

    loss8, d_x, d_nf, lg = _device_step(x[0], loss_target[0], mod, lb, wfull, sp)

    gpack = jnp.stack([jnp.stack([
        _pack_layer([lax.slice_in_dim(lg[l][n], j * s[ax], (j + 1) * s[ax], axis=ax) for n, s, ax in BIG], BF)
        for l in range(DEPTH)]) for j in range(N_CHIP)])
    half = PACK_R // 2
    keep = lax.dynamic_slice_in_dim(gpack, ci * half, half, axis=2).reshape(-1, PACK_C)
    give = lax.dynamic_slice_in_dim(gpack, (1 - ci) * half, half, axis=2).reshape(-1, PACK_C)
    got = _pair_swap("grad_pair_swap", give)
    (pair_sum,) = _ew("grad_pair_sum", lambda a, b: (a.astype(F32) + b.astype(F32),), [keep, got], [BF])
    pair_sum = pair_sum.reshape(N_CHIP, DEPTH * half, PACK_C)
    parts = lax.dynamic_update_slice(_chip_exchange("grad_chip_exchange", pair_sum),
                                     lax.dynamic_slice_in_dim(pair_sum, chip, 1, axis=0), (chip, 0, 0))
    mine = _sum_leading("grad_chip_sum", parts, F32).reshape(DEPTH, half, PACK_C)
    theirs = _pair_swap("grad_pair_share", mine)
    gfull = jnp.concatenate([jnp.where(ci == 0, mine, theirs), jnp.where(ci == 0, theirs, mine)], axis=1)
    grad = {}
    per_layer = [_unpack_layer(gfull[l]) for l in range(DEPTH)]
    for i, (n, _, _) in enumerate(BIG):
        grad[n] = jnp.stack([per_layer[l][i] for l in range(DEPTH)])

    dmod = jnp.stack([jnp.concatenate([lg[l][n] for n in ("sh1", "sc1", "g1", "sh2", "sc2", "g2")], axis=1)[0] for l in range(DEPTH)])
    d_lb = jnp.stack([lg[l]["lb"][0] for l in range(DEPTH)])
    contrib = {
        "norm_mix": jnp.stack([lg[l]["norm_mix"][0] for l in range(DEPTH)]),
        "norm_ffn": jnp.stack([lg[l]["norm_ffn"][0] for l in range(DEPTH)]),
        "b_merge": jnp.stack([lg[l]["b_merge"][0] for l in range(DEPTH)]),
        "hgrn_lb_logits": _lb_bwd(hgrn_lb_logits, d_lb),
        "hgrn_norm": jnp.stack([lg[l]["hgrn_norm"][0] for l in range(DEPTH)]),
        "ssm_conv_w": jnp.stack([lg[l]["ssm_conv_w"] for l in range(DEPTH)]),
        "ssm_conv_b": jnp.stack([lg[l]["ssm_conv_b"][0] for l in range(DEPTH)]),
        "ssm_dt_bias": jnp.stack([lg[l]["ssm_dt_bias"][0, DT_OFF:DT_OFF + 12] for l in range(DEPTH)]),
        "ssm_a_log": jnp.stack([lg[l]["ssm_a_log"][0, DT_OFF:DT_OFF + 12] for l in range(DEPTH)]),
        "ssm_d": jnp.stack([lg[l]["ssm_d"][0, DT_OFF:DT_OFF + 12] for l in range(DEPTH)]),
        "ssm_norm": jnp.stack([lg[l]["ssm_norm"][0] for l in range(DEPTH)]),
        "gdn_conv_w": jnp.stack([lg[l]["gdn_conv_w"] for l in range(DEPTH)]),
        "gdn_dt_bias": jnp.stack([lg[l]["gdn_dt_bias"][0, GA_OFF:GA_OFF + 6] for l in range(DEPTH)]),
        "gdn_a_log": jnp.stack([lg[l]["gdn_a_log"][0, GA_OFF:GA_OFF + 6] for l in range(DEPTH)]),
        "gdn_norm": jnp.stack([lg[l]["gdn_norm"][0] for l in range(DEPTH)]),
        "norm_final": d_nf[0],
    }
    flat = jnp.concatenate([dmod.reshape(-1)] + [contrib[n].reshape(-1) for n in SMALL_REPL] + [loss8[0, 0:1]])
    small_all = _all_gather_small("gather_small_grads", _pad_rows(flat, SMALL_ROWS, D_MODEL))
    total = _sum_leading("small_grad_sum", small_all, F32).reshape(-1)
    n_mod = dmod.size
    grad["b_ada"] = total[:n_mod].reshape(b_ada.shape)
    off = n_mod
    full_small = {}
    for n in SMALL_REPL:
        full_small[n] = total[off:off + contrib[n].size].reshape(contrib[n].shape)
        off += contrib[n].size
    loss = total[off]
    for n in SMALL_REPL:
        if n in ("ssm_conv_w", "gdn_conv_w"):
            cols = w[n].shape[2]
            grad[n] = lax.dynamic_slice_in_dim(full_small[n], chip * cols, cols, axis=2)
        else:
            grad[n] = full_small[n]
    dmod_cols = lax.dynamic_slice_in_dim(small_all[:, :n_mod // D_MODEL, :].reshape(8, DEPTH, -1), chip * ada_cols, ada_cols, axis=2)
    grad["w_ada"] = _ada_bwd(c_all, dmod_cols.transpose(1, 0, 2))

    delta, new_m, new_v = {}, {}, {}
    big_names = ("w_ada",) + tuple(n for n, _, _ in BIG)
    for n in big_names:
        delta[n], new_m[n], new_v[n] = _adamw("adamw_" + n, w[n], grad[n], m[n], v[n])
    small_names = [n for n in WEIGHTS if n not in big_names]
    packs = [_pad_rows(jnp.concatenate([d[n].reshape(-1) for n in small_names]), 584, LANES) for d in (w, grad, m, v)]
    outs = _ew("adamw_small", _adamw_fn, packs, [F32, F32, F32])
    off = 0
    for n in small_names:
        for dst, o in zip((delta, new_m, new_v), outs):
            dst[n] = o.reshape(-1)[off:off + w[n].size].reshape(w[n].shape)
        off += w[n].size
    return (loss, d_x[None], *[grad[n] for n in WEIGHTS], *[delta[n] for n in WEIGHTS], *[new_m[n] for n in WEIGHTS],
            *[new_v[n] for n in WEIGHTS])
```

```python
import functools

import jax
import jax.numpy as jnp
from jax import lax
from jax.experimental import pallas as pl
from jax.experimental.pallas import tpu as pltpu

F32 = jnp.float32
BF = jnp.bfloat16
HI = lax.Precision.HIGHEST

D_MODEL = 1024
DEPTH = 4
CHUNK = 64
MIX_W = 768
HEAD = 128
N_HEAD6 = 6
SSM_P = 64
SSM_N = 128
CONV_CH = 1280
QKV_W = 2304
FFN_H = 2816
IN_WIDTH = 11288
NORM_EPS = 1e-6
F_MIN = 1e-30
HALO = 8
HGRN_SUB = 16
SMALL_W = 512
LANES = 128
DT_OFF, GB_OFF, GA_OFF = 0, 12, 18

ADAM_LR, ADAM_B1, ADAM_B2, ADAM_EPS, ADAM_WD, ADAM_STEP = 0.001, 0.9, 0.999, 1e-08, 0.01, 10

VMEM_LIMIT = 56 * 1024 * 1024
TOKEN_TILE = 256


def _pc(body, **kw):
    return pl.pallas_call(body, **kw)


def _cparams(sem):
    return pltpu.CompilerParams(dimension_semantics=sem, vmem_limit_bytes=VMEM_LIMIT)


def _bdot(a, b):
    return jnp.dot(a.astype(BF), b.astype(BF), preferred_element_type=F32)


def _bdot_nt(a, b):
    return lax.dot_general(a.astype(BF), b.astype(BF), (((1,), (1,)), ((), ())), preferred_element_type=F32)


def _bdot_tn(a, b):
    return lax.dot_general(a.astype(BF), b.astype(BF), (((0,), (0,)), ((), ())), preferred_element_type=F32)


def _silu(x):
    return x * jax.nn.sigmoid(x)


def _tri_mask(n, strict=False):
    t = lax.broadcasted_iota(jnp.int32, (n, n), 0)
    s = lax.broadcasted_iota(jnp.int32, (n, n), 1)
    return (s < t) if strict else (s <= t)


def _masked_exp(diff, mask):
    return jnp.where(mask, jnp.exp(jnp.where(mask, diff, 0.0)), 0.0)


def _cumsum_rows(x):
    tri = jnp.where(_tri_mask(x.shape[0]), 1.0, 0.0).astype(F32)
    return jnp.dot(tri, x, preferred_element_type=F32, precision=HI)


def _rms(x, w):
    return x * lax.rsqrt(jnp.mean(x * x, axis=-1, keepdims=True) + NORM_EPS) * w


def _causal_conv(halo, x, w):
    ext = jnp.concatenate([halo, x], axis=0)
    n = x.shape[0]
    acc = w[0:1, :] * ext[HALO - 3:HALO - 3 + n, :]
    for i in range(1, 4):
        acc = acc + w[i:i + 1, :] * ext[HALO - 3 + i:HALO - 3 + i + n, :]
    return acc


def _unit_lower_inverse(a):
    n = a.shape[0]
    t = lax.broadcasted_iota(jnp.int32, (n, n), 0)
    s_ = lax.broadcasted_iota(jnp.int32, (n, n), 1)
    x = jnp.where(t == s_, 1.0, 0.0).astype(F32)
    for s in range(n - 1):
        r0 = 8 * ((s + 1) // 8)
        low = x[r0:] - a[r0:, s:s + 1] * x[s:s + 1, :]
        x = low if r0 == 0 else jnp.concatenate([x[:r0], low], axis=0)
    return x


@jax.custom_vjp
def _unit_lower_solve(a, r):
    return jnp.dot(_unit_lower_inverse(a), r, preferred_element_type=F32, precision=HI)


def _uls_fwd(a, r):
    inv = _unit_lower_inverse(a)
    x = jnp.dot(inv, r, preferred_element_type=F32, precision=HI)
    return x, (inv, x)


def _uls_bwd(res, g):
    inv, x = res
    y = lax.dot_general(inv, g, (((0,), (0,)), ((), ())), preferred_element_type=F32, precision=HI)
    da = jnp.where(_tri_mask(CHUNK, strict=True), -_bdot_nt(y, x), 0.0)
    return da, y


_unit_lower_solve.defvjp(_uls_fwd, _uls_bwd)


def _hgrn_chunk(tiles, halos, state, consts):
    q_raw, f_raw, v_all, g_raw = tiles
    lb, norm_w = consts
    q_all = _silu(q_raw)
    f = lb + (1.0 - lb) * jax.nn.sigmoid(f_raw)
    logf = jnp.log(jnp.maximum(f, F_MIN))
    k_all = (1.0 - lb) * jax.nn.sigmoid(-f_raw)
    b_all = _cumsum_rows(logf)
    sub = HGRN_SUB
    row = lax.broadcasted_iota(jnp.int32, (sub, 1), 0)
    outs, new_states = [], []
    for h in range(N_HEAD6):
        sl = slice(h * HEAD, (h + 1) * HEAD)
        q, k, v, b = q_all[:, sl], k_all[:, sl], v_all[:, sl], b_all[:, sl]
        st = state[h]
        o_inter = _bdot_nt(q * jnp.exp(b), st)
        blocks = []
        for i in range(CHUNK // sub):
            r0 = i * sub
            qi, ki, vi, bi = q[r0:r0 + sub], k[r0:r0 + sub], v[r0:r0 + sub], b[r0:r0 + sub]
            if i > 0:
                ref = b[r0 - 1:r0, :]
                a_off = _bdot_nt(qi * jnp.exp(bi - ref), k[0:r0] * jnp.exp(ref - b[0:r0]))
                oi = _bdot(a_off, v[0:r0])
            else:
                oi = jnp.zeros((sub, HEAD), F32)
            for s in range(sub):
                m = row >= s
                e = _masked_exp(bi - bi[s:s + 1, :], m)
                col = jnp.sum(qi * ki[s:s + 1, :] * e, axis=1, keepdims=True)
                oi = oi + col * vi[s:s + 1, :]
            blocks.append(oi)
        o = jnp.concatenate(blocks, axis=0) + o_inter
        b_end = b[CHUNK - 1:CHUNK, :]
        new_states.append(st * jnp.exp(b_end) + _bdot_tn(v, k * jnp.exp(b_end - b)))
        outs.append(_rms(o, norm_w) * _silu(g_raw[:, sl]))
    return (jnp.concatenate(outs, axis=1),), jnp.stack(new_states)


def _ssd_chunk(tiles, halos, state, consts):
    z, xbc_raw, small = tiles
    (halo,) = halos
    conv_w, conv_b, dt_bias, a_log, d_skip, norm_w = consts
    xbc = _silu(_causal_conv(halo, xbc_raw, conv_w) + conv_b)
    xs, bm, cm = xbc[:, :MIX_W], xbc[:, MIX_W:MIX_W + 2 * SSM_N], xbc[:, MIX_W + 2 * SSM_N:]
    dt = jax.nn.softplus(small + dt_bias)
    cum = _cumsum_rows(-jnp.exp(a_log) * dt)
    cum_t = cum.T
    lane = lax.broadcasted_iota(jnp.int32, (1, LANES), 1)
    tri = _tri_mask(CHUNK)
    ys, new_states = [], []
    gmats = []
    for g in range(2):
        gmats.append(_bdot_nt(cm[:, g * SSM_N:(g + 1) * SSM_N], bm[:, g * SSM_N:(g + 1) * SSM_N]))
    for p in range(6):
        g = p // 3
        bg, cg = bm[:, g * SSM_N:(g + 1) * SSM_N], cm[:, g * SSM_N:(g + 1) * SSM_N]
        xp = xs[:, p * LANES:(p + 1) * LANES]
        sp = state[p]
        yp = jnp.zeros((CHUNK, LANES), F32)
        sn = jnp.zeros((SSM_N, LANES), F32)
        decay = jnp.zeros((1, LANES), F32)
        for j in range(2):
            h = 2 * p + j
            hm = jnp.where((lane >= j * SSM_P) & (lane < (j + 1) * SSM_P), 1.0, 0.0).astype(F32)
            ch, ch_t, ce = cum[:, h:h + 1], cum_t[h:h + 1, :], cum[CHUNK - 1:CHUNK, h:h + 1]
            seg = _masked_exp(ch - ch_t, tri)
            vm = xp * dt[:, h:h + 1] * hm
            yp = yp + _bdot(gmats[g] * seg, vm) + _bdot(cg * jnp.exp(ch), sp * hm) + d_skip[:, h:h + 1] * xp * hm
            sn = sn + _bdot_tn(bg * jnp.exp(ce - ch), vm)
            decay = decay + jnp.exp(ce) * hm
        ys.append(yp)
        new_states.append(sn + sp * decay)
    y = jnp.concatenate(ys, axis=1) * _silu(z)
    gw = MIX_W // 2
    y = jnp.concatenate([_rms(y[:, g * gw:(g + 1) * gw], norm_w[:, g * gw:(g + 1) * gw]) for g in range(2)], axis=1)
    return (y,), jnp.stack(new_states)


def _gdn_chunk(tiles, halos, state, consts):
    qkv_raw, z, small = tiles
    (halo,) = halos
    conv_w, dt_bias, a_log, norm_w = consts
    qkv = _silu(_causal_conv(halo, qkv_raw, conv_w))
    beta_all = jax.nn.sigmoid(small)
    cum = _cumsum_rows(-jnp.exp(a_log) * jax.nn.softplus(small + dt_bias))
    cum_t = cum.T
    tri, tri_strict = _tri_mask(CHUNK), _tri_mask(CHUNK, strict=True)
    outs, new_states = [], []
    for h in range(N_HEAD6):
        q = qkv[:, h * HEAD:(h + 1) * HEAD]
        k = qkv[:, MIX_W + h * HEAD:MIX_W + (h + 1) * HEAD]
        v = qkv[:, 2 * MIX_W + h * HEAD:2 * MIX_W + (h + 1) * HEAD]
        q = q * lax.rsqrt(jnp.sum(q * q, axis=-1, keepdims=True) + NORM_EPS) * (HEAD ** -0.5)
        k = k * lax.rsqrt(jnp.sum(k * k, axis=-1, keepdims=True) + NORM_EPS)
        beta = beta_all[:, GB_OFF + h:GB_OFF + h + 1]
        c, c_t = cum[:, GA_OFF + h:GA_OFF + h + 1], cum_t[GA_OFF + h:GA_OFF + h + 1, :]
        ce = cum[CHUNK - 1:CHUNK, GA_OFF + h:GA_OFF + h + 1]
        decay = _masked_exp(c - c_t, tri)
        a_low = jnp.where(tri_strict, beta * _bdot_nt(k, k) * decay, 0.0)
        rhs = jnp.concatenate([v * beta, k * (beta * jnp.exp(c))], axis=1)
        sol = _unit_lower_solve(a_low, rhs)
        u_base, w_corr = sol[:, :HEAD], sol[:, HEAD:]
        qk = _bdot_nt(q, k) * decay
        st = state[h]
        u = u_base - _bdot(w_corr, st)
        o = _bdot(q * jnp.exp(c), st) + _bdot(qk, u)
        new_states.append(jnp.exp(ce) * st + _bdot_tn(k * jnp.exp(ce - c), u))
        outs.append(_rms(o, norm_w) * _silu(z[:, h * HEAD:(h + 1) * HEAD]))
    return (jnp.concatenate(outs, axis=1),), jnp.stack(new_states)


def _scan_fwd(name, fn, tiled, halo_idx, consts, out_width, state_shape):
    seq = tiled[0][0].shape[0]
    nc = seq // CHUNK
    n_t, n_h, n_c = len(tiled), len(halo_idx), len(consts)

    def body(*refs):
        t_refs, h_refs, c_refs = refs[:n_t], refs[n_t:n_t + n_h], refs[n_t + n_h:n_t + n_h + n_c]
        y_ref, save_ref, st_ref = refs[n_t + n_h + n_c:]
        i = pl.program_id(0)

        @pl.when(i == 0)
        def _():
            st_ref[...] = jnp.zeros_like(st_ref)

        flag = jnp.where(i > 0, 1.0, 0.0).astype(F32)
        st = st_ref[...]
        (y,), new = fn([r[...] for r in t_refs], [r[...] * flag for r in h_refs], st, [r[...] for r in c_refs])
        save_ref[0] = st
        y_ref[...] = y.astype(y_ref.dtype)
        st_ref[...] = new

    in_specs = [pl.BlockSpec((CHUNK, w), functools.partial(lambda i, cb: (i, cb), cb=cb)) for _, w, cb in tiled]
    in_specs += [pl.BlockSpec((HALO, tiled[j][1]),
                              functools.partial(lambda i, cb: (jnp.maximum(i * (CHUNK // HALO) - 1, 0), cb), cb=tiled[j][2]))
                 for j in halo_idx]
    in_specs += [pl.BlockSpec(c.shape, functools.partial(lambda i, nd: (0,) * nd, nd=c.ndim)) for c in consts]
    zeros = (0,) * len(state_shape)
    return _pc(
        body, name=name, grid=(nc,), in_specs=in_specs,
        out_specs=(pl.BlockSpec((CHUNK, out_width), lambda i: (i, 0)),
                   pl.BlockSpec((1,) + state_shape, lambda i: (i,) + zeros)),
        out_shape=(jax.ShapeDtypeStruct((seq, out_width), BF), jax.ShapeDtypeStruct((nc,) + state_shape, F32)),
        scratch_shapes=[pltpu.VMEM(state_shape, F32)],
        compiler_params=_cparams(("arbitrary",)),
    )(*[t[0] for t in tiled], *[tiled[j][0] for j in halo_idx], *consts)


def _scan_bwd(name, fn, tiled, halo_idx, consts, saved, dy, dtile_dtypes, extra=None):
    seq = tiled[0][0].shape[0]
    nc = seq // CHUNK
    n_t, n_h, n_c = len(tiled), len(halo_idx), len(consts)
    state_shape = saved.shape[1:]
    n_x = 0 if extra is None else 1

    def body(*refs):
        t_refs, h_refs, c_refs = refs[:n_t], refs[n_t:n_t + n_h], refs[n_t + n_h:n_t + n_h + n_c]
        pos = n_t + n_h + n_c
        save_ref, dy_ref = refs[pos], refs[pos + 1]
        x_refs = refs[pos + 2:pos + 2 + n_x]
        pos += 2 + n_x
        dt_refs, dc_refs = refs[pos:pos + n_t], refs[pos + n_t:pos + n_t + n_c]
        dst_ref = refs[pos + n_t + n_c]
        carry_refs = refs[pos + n_t + n_c + 1:]
        i = pl.program_id(0)

        @pl.when(i == 0)
        def _():
            dst_ref[...] = jnp.zeros_like(dst_ref)
            for r in carry_refs:
                r[...] = jnp.zeros_like(r)
            for r in dc_refs:
                r[...] = jnp.zeros_like(r)

        flag = jnp.where(i < nc - 1, 1.0, 0.0).astype(F32)
        tiles = [r[...] for r in t_refs]
        halos = [r[...] * flag for r in h_refs]
        cvals = [r[...] for r in c_refs]
        _, vjp = jax.vjp(fn, tiles, halos, save_ref[0], cvals)
        d_tiles, d_halos, d_state, d_consts = vjp(((dy_ref[...].astype(F32),), dst_ref[...]))
        dst_ref[...] = d_state
        for r, g in zip(dc_refs, d_consts):
            r[...] += g
        for j, (r, g) in enumerate(zip(dt_refs, d_tiles)):
            if extra is not None and extra[0] == j:
                g = g + x_refs[0][...].astype(F32)
            r[...] = g.astype(r.dtype)
            if j in halo_idx:
                cr = carry_refs[halo_idx.index(j)]
                r[CHUNK - HALO:CHUNK, :] = (g[CHUNK - HALO:CHUNK, :] + cr[...]).astype(r.dtype)
                cr[...] = d_halos[halo_idx.index(j)] * flag

    rev = lambda i: nc - 1 - i
    in_specs = [pl.BlockSpec((CHUNK, w), functools.partial(lambda i, cb: (rev(i), cb), cb=cb)) for _, w, cb in tiled]
    in_specs += [pl.BlockSpec((HALO, tiled[j][1]),
                              functools.partial(lambda i, cb: (jnp.maximum(rev(i) * (CHUNK // HALO) - 1, 0), cb), cb=tiled[j][2]))
                 for j in halo_idx]
    in_specs += [pl.BlockSpec(c.shape, functools.partial(lambda i, nd: (0,) * nd, nd=c.ndim)) for c in consts]
    zeros = (0,) * len(state_shape)
    in_specs += [pl.BlockSpec((1,) + state_shape, lambda i: (rev(i),) + zeros),
                 pl.BlockSpec((CHUNK, dy.shape[1]), lambda i: (rev(i), 0))]
    args = [t[0] for t in tiled] + [tiled[j][0] for j in halo_idx] + list(consts) + [saved, dy]
    if extra is not None:
        in_specs.append(pl.BlockSpec((CHUNK, extra[1].shape[1]), lambda i: (rev(i), 0)))
        args.append(extra[1])
    out_specs = [pl.BlockSpec((CHUNK, w), lambda i: (rev(i), 0)) for _, w, _ in tiled]
    out_specs += [pl.BlockSpec(c.shape, functools.partial(lambda i, nd: (0,) * nd, nd=c.ndim)) for c in consts]
    out_shape = [jax.ShapeDtypeStruct((seq, w), dtd) for (_, w, _), dtd in zip(tiled, dtile_dtypes)]
    out_shape += [jax.ShapeDtypeStruct(c.shape, F32) for c in consts]
    scratch = [pltpu.VMEM(state_shape, F32)] + [pltpu.VMEM((HALO, tiled[j][1]), F32) for j in halo_idx]
    return _pc(body, name=name, grid=(nc,), in_specs=in_specs, out_specs=tuple(out_specs), out_shape=tuple(out_shape),
               scratch_shapes=scratch, compiler_params=_cparams(("arbitrary",)))(*args)


def _tile_fwd(name, fn, tiled, consts, outs, tm=TOKEN_TILE):
    seq = tiled[0][0].shape[0]
    n_t, n_c = len(tiled), len(consts)

    def body(*refs):
        res = fn(*[r[...] for r in refs[:n_t + n_c]])
        for r, y in zip(refs[n_t + n_c:], res):
            r[...] = y.astype(r.dtype)

    in_specs = [pl.BlockSpec((tm, w), functools.partial(lambda i, cb: (i, cb), cb=cb)) for _, w, cb in tiled]
    in_specs += [pl.BlockSpec(c.shape, functools.partial(lambda i, nd: (0,) * nd, nd=c.ndim)) for c in consts]
    return _pc(body, name=name, grid=(seq // tm,), in_specs=in_specs,
               out_specs=tuple(pl.BlockSpec((tm, w), lambda i: (i, 0)) for w, _ in outs),
               out_shape=tuple(jax.ShapeDtypeStruct((seq, w), dtp) for w, dtp in outs),
               compiler_params=_cparams(("arbitrary",)))(*[t[0] for t in tiled], *consts)


def _tile_bwd(name, fn, tiled, consts, douts, dtile_dtypes, add_to=None, tm=TOKEN_TILE):
    seq = tiled[0][0].shape[0]
    n_t, n_c, n_o = len(tiled), len(consts), len(douts)
    n_x = 0 if add_to is None else 1
    keep = [j for j, dtp in enumerate(dtile_dtypes) if dtp is not None]

    def body(*refs):
        vals = [r[...].astype(F32) for r in refs[:n_t + n_c]]
        pos = n_t + n_c
        g_refs, x_refs = refs[pos:pos + n_o], refs[pos + n_o:pos + n_o + n_x]
        pos += n_o + n_x
        dt_refs, dc_refs = refs[pos:pos + len(keep)], refs[pos + len(keep):]
        i = pl.program_id(0)

        @pl.when(i == 0)
        def _():
            for r in dc_refs:
                r[...] = jnp.zeros_like(r)

        _, vjp = jax.vjp(fn, *vals)
        cts = vjp(tuple(g[...].astype(F32) for g in g_refs))
        for r, j in zip(dt_refs, keep):
            g = cts[j]
            if add_to is not None and add_to[0] == j:
                g = g + x_refs[0][...].astype(F32)
            r[...] = g.astype(r.dtype)
        for r, g in zip(dc_refs, cts[n_t:]):
            r[...] += g

    in_specs = [pl.BlockSpec((tm, w), functools.partial(lambda i, cb: (i, cb), cb=cb)) for _, w, cb in tiled]
    in_specs += [pl.BlockSpec(c.shape, functools.partial(lambda i, nd: (0,) * nd, nd=c.ndim)) for c in consts]
    in_specs += [pl.BlockSpec((tm, g.shape[1]), lambda i: (i, 0)) for g in douts]
    args = [t[0] for t in tiled] + list(consts) + list(douts)
    if add_to is not None:
        in_specs.append(pl.BlockSpec((tm, add_to[1].shape[1]), lambda i: (i, 0)))
        args.append(add_to[1])
    out_specs = [pl.BlockSpec((tm, tiled[j][1]), lambda i: (i, 0)) for j in keep]
    out_specs += [pl.BlockSpec(c.shape, functools.partial(lambda i, nd: (0,) * nd, nd=c.ndim)) for c in consts]
    out_shape = [jax.ShapeDtypeStruct((seq, tiled[j][1]), dtile_dtypes[j]) for j in keep]
    out_shape += [jax.ShapeDtypeStruct(c.shape, F32) for c in consts]
    return _pc(body, name=name, grid=(seq // tm,), in_specs=in_specs, out_specs=tuple(out_specs),
               out_shape=tuple(out_shape), compiler_params=_cparams(("arbitrary",)))(*args)


def _lnmod_fn(x, nw, sc, sh):
    return (_rms(x, nw) * (1.0 + sc) + sh,)


def _resid_fn(x, o, g):
    return (x + (1.0 + g) * o,)


def _swiglu_fn(gu):
    return (_silu(gu[:, :FFN_H]) * gu[:, FFN_H:],)


def _merge_fn(yh, ys, yg, logits, wb, b_merge):
    gates = jax.nn.sigmoid(logits + b_merge)
    acc = None
    for n, y in enumerate((yh, ys, yg)):
        t = gates[:, n * D_MODEL:(n + 1) * D_MODEL] * _bdot(y, wb[n])
        acc = t if acc is None else acc + t
    return (acc,)


MM_VMEM_BUDGET = 40 * 1024 * 1024
MM_TILE_CAP = 1024
MM_K_CAP = 4096


def _divisor(n, cap, unit=LANES):
    best = None
    for d in range(unit, min(n, cap) + 1, unit):
        if n % d == 0:
            best = d
    return n if best is None else best


def _mm_tiles(m, n, k, out_bytes):
    tk = k if k <= MM_K_CAP else _divisor(k, 3072)
    tm, tn = _divisor(m, MM_TILE_CAP), _divisor(n, MM_TILE_CAP + MM_TILE_CAP // 2)

    def need(tm_, tn_):
        acc = tm_ * tn_ * 4 if tk < k else 0
        return 2 * 2 * tk * (tm_ + tn_) + acc + 2 * tm_ * tn_ * out_bytes

    while need(tm, tn) > MM_VMEM_BUDGET:
        if tn >= tm and _divisor(n, tn - LANES) < tn:
            tn = _divisor(n, tn - LANES)
        elif _divisor(m, tm - LANES) < tm:
            tm = _divisor(m, tm - LANES)
        else:
            break
    return tm, tn, tk


def _mm(a, b, mode, out_dtype, name):
    if mode == "nn":
        (m, k), n = a.shape, b.shape[1]
    elif mode == "nt":
        (m, k), n = a.shape, b.shape[0]
    else:
        (k, m), n = a.shape, b.shape[1]
    tm, tn, tk = _mm_tiles(m, n, k, jnp.dtype(out_dtype).itemsize)
    nk = k // tk
    dims = {"nn": ((1,), (0,)), "nt": ((1,), (1,)), "tn": ((0,), (0,))}[mode]

    def body_one(a_ref, b_ref, o_ref):
        o_ref[...] = lax.dot_general(a_ref[...], b_ref[...], (dims, ((), ())), preferred_element_type=F32).astype(o_ref.dtype)

    def body_acc(a_ref, b_ref, o_ref, acc_ref):
        kk = pl.program_id(2)

        @pl.when(kk == 0)
        def _():
            acc_ref[...] = jnp.zeros_like(acc_ref)

        acc_ref[...] += lax.dot_general(a_ref[...], b_ref[...], (dims, ((), ())), preferred_element_type=F32)

        @pl.when(kk == nk - 1)
        def _():
            o_ref[...] = acc_ref[...].astype(o_ref.dtype)

    a_spec = pl.BlockSpec((tk, tm), lambda i, j, kk: (kk, i)) if mode == "tn" else pl.BlockSpec((tm, tk), lambda i, j, kk: (i, kk))
    b_spec = pl.BlockSpec((tn, tk), lambda i, j, kk: (j, kk)) if mode == "nt" else pl.BlockSpec((tk, tn), lambda i, j, kk: (kk, j))
    return _pc(body_one if nk == 1 else body_acc, name=name, grid=(m // tm, n // tn, nk), in_specs=[a_spec, b_spec],
               out_specs=pl.BlockSpec((tm, tn), lambda i, j, kk: (i, j)),
               out_shape=jax.ShapeDtypeStruct((m, n), out_dtype),
               scratch_shapes=[] if nk == 1 else [pltpu.VMEM((tm, tn), F32)],
               compiler_params=_cparams(("parallel", "parallel", "arbitrary")))(a.astype(BF), b.astype(BF))


def _final_loss(x, tgt, norm_final, tm=TOKEN_TILE):
    seq = x.shape[0]

    def fn(xv, nf, tv):
        err = jnp.square(_rms(xv, nf) - tv)
        return 0.5 * jnp.sum(jnp.mean(err, axis=-1))

    def body(x_ref, t_ref, nf_ref, loss_ref, dx_ref, dnf_ref):
        i = pl.program_id(0)

        @pl.when(i == 0)
        def _():
            loss_ref[...] = jnp.zeros_like(loss_ref)
            dnf_ref[...] = jnp.zeros_like(dnf_ref)

        val, vjp = jax.vjp(functools.partial(fn, tv=t_ref[...]), x_ref[...], nf_ref[...])
        dx, dnf = vjp(jnp.ones((), F32))
        dx_ref[...] = dx
        dnf_ref[...] += dnf
        loss_ref[...] += jnp.broadcast_to(val, loss_ref.shape)

    return _pc(body, name="final_loss", grid=(seq // tm,),
               in_specs=[pl.BlockSpec((tm, D_MODEL), lambda i: (i, 0)), pl.BlockSpec((tm, D_MODEL), lambda i: (i, 0)),
                         pl.BlockSpec((1, D_MODEL), lambda i: (0, 0))],
               out_specs=(pl.BlockSpec((8, LANES), lambda i: (0, 0)), pl.BlockSpec((tm, D_MODEL), lambda i: (i, 0)),
                          pl.BlockSpec((1, D_MODEL), lambda i: (0, 0))),
               out_shape=(jax.ShapeDtypeStruct((8, LANES), F32), jax.ShapeDtypeStruct((seq, D_MODEL), F32),
                          jax.ShapeDtypeStruct((1, D_MODEL), F32)),
               compiler_params=_cparams(("arbitrary",)))(x, tgt, norm_final)


def _ada_fwd(c_all, w_ada, b_ada_cols):
    n_l, _, cols = w_ada.shape

    def body(c_ref, w_ref, b_ref, o_ref):
        o_ref[0] = jnp.dot(_silu(c_ref[...]), w_ref[0], preferred_element_type=F32, precision=HI) + b_ref[0]

    return _pc(body, name="ada_fwd", grid=(n_l,),
               in_specs=[pl.BlockSpec((8, D_MODEL), lambda l: (0, 0)), pl.BlockSpec((1, D_MODEL, cols), lambda l: (l, 0, 0)),
                         pl.BlockSpec((1, 1, cols), lambda l: (l, 0, 0))],
               out_specs=pl.BlockSpec((1, 8, cols), lambda l: (l, 0, 0)),
               out_shape=jax.ShapeDtypeStruct((n_l, 8, cols), F32),
               compiler_params=_cparams(("arbitrary",)))(c_all, w_ada, b_ada_cols.reshape(n_l, 1, cols))


def _ada_bwd(c_all, dmod_cols):
    n_l, _, cols = dmod_cols.shape

    def body(c_ref, g_ref, o_ref):
        o_ref[0] = lax.dot_general(_silu(c_ref[...]), g_ref[0], (((0,), (0,)), ((), ())), preferred_element_type=F32,
                                   precision=HI)

    return _pc(body, name="ada_bwd", grid=(n_l,),
               in_specs=[pl.BlockSpec((8, D_MODEL), lambda l: (0, 0)), pl.BlockSpec((1, 8, cols), lambda l: (l, 0, 0))],
               out_specs=pl.BlockSpec((1, D_MODEL, cols), lambda l: (l, 0, 0)),
               out_shape=jax.ShapeDtypeStruct((n_l, D_MODEL, cols), F32),
               compiler_params=_cparams(("arbitrary",)))(c_all, dmod_cols)


def _lb_fn(logits):
    e = jnp.exp(logits - jnp.max(logits, axis=0, keepdims=True))
    p = e / jnp.sum(e, axis=0, keepdims=True)
    r = lax.broadcasted_iota(jnp.int32, (DEPTH, 1), 0)
    lb = jnp.zeros_like(p)
    for j in range(1, DEPTH):
        lb = lb + jnp.where(r >= j, p[j:j + 1, :], 0.0)
    return lb


def _lb_fwd(logits):
    def body(l_ref, o_ref):
        o_ref[...] = _lb_fn(l_ref[...])

    return _pc(body, name="lb_fwd", out_shape=jax.ShapeDtypeStruct(logits.shape, F32))(logits)


def _lb_bwd(logits, dlb):
    def body(l_ref, g_ref, o_ref):
        _, vjp = jax.vjp(_lb_fn, l_ref[...])
        o_ref[...] = vjp(g_ref[...])[0]

    return _pc(body, name="lb_bwd", out_shape=jax.ShapeDtypeStruct(logits.shape, F32))(logits, dlb)


def _rows_for(n_rows, n_cols):
    r = 8
    while r * 2 <= n_rows and n_rows % (r * 2) == 0 and r * 2 * n_cols <= 256 * 1024:
        r *= 2
    return r if n_rows % r == 0 else n_rows


def _ew(name, fn, ins, out_dtypes):
    n_rows, n_cols = ins[0].shape
    tr = _rows_for(n_rows, n_cols)
    n_in = len(ins)

    def body(*refs):
        res = fn(*[r[...] for r in refs[:n_in]])
        for r, y in zip(refs[n_in:], res):
            r[...] = y.astype(r.dtype)

    spec = pl.BlockSpec((tr, n_cols), lambda i: (i, 0))
    return _pc(body, name=name, grid=(n_rows // tr,), in_specs=[spec] * n_in, out_specs=tuple([spec] * len(out_dtypes)),
               out_shape=tuple(jax.ShapeDtypeStruct((n_rows, n_cols), d) for d in out_dtypes),
               compiler_params=_cparams(("arbitrary",)))(*ins)


def _adamw_fn(w, g, m, v):
    m = ADAM_B1 * m + (1.0 - ADAM_B1) * g
    v = ADAM_B2 * v + (1.0 - ADAM_B2) * jnp.square(g)
    m_hat = m / (1.0 - ADAM_B1 ** ADAM_STEP)
    v_hat = v / (1.0 - ADAM_B2 ** ADAM_STEP)
    return -ADAM_LR * (m_hat / (jnp.sqrt(v_hat) + ADAM_EPS) + ADAM_WD * w), m, v


def _adamw(name, w, g, m, v):
    shape = w.shape
    two = (-1, shape[-1])
    d, nm, nv = _ew(name, _adamw_fn, [a.reshape(two) for a in (w, g, m, v)], [F32, F32, F32])
    return d.reshape(shape), nm.reshape(shape), nv.reshape(shape)


def _sum_leading(name, a, out_dtype):
    n, n_rows, n_cols = a.shape
    tr = _rows_for(n_rows, n_cols)

    def body(a_ref, o_ref):
        acc = a_ref[0].astype(F32)
        for j in range(1, n):
            acc = acc + a_ref[j].astype(F32)
        o_ref[...] = acc.astype(o_ref.dtype)

    return _pc(body, name=name, grid=(n_rows // tr,), in_specs=[pl.BlockSpec((n, tr, n_cols), lambda i: (0, i, 0))],
               out_specs=pl.BlockSpec((tr, n_cols), lambda i: (i, 0)),
               out_shape=jax.ShapeDtypeStruct((n_rows, n_cols), out_dtype),
               compiler_params=_cparams(("arbitrary",)))(a)


MESH = pl.DeviceIdType.MESH
ANY = pl.BlockSpec(memory_space=pl.ANY)


def _place():
    return lax.axis_index("x"), lax.axis_index("y"), lax.axis_index("c")


def _all_gather_small(name, a):
    m_per, n = a.shape

    def body(x_ref, out_ref, send_sems, recv_sems, local_sem):
        x, y, c = _place()
        me, sibling = (x, y, c), (x, y, 1 - c)
        chips = [(1 - x, y), (x, 1 - y), (1 - x, 1 - y)]

        def rows(px, py, pc):
            return out_ref.at[pl.ds((4 * px + 2 * py + pc) * m_per, m_per), :]

        def copy(k, block, to, src=None):
            return pltpu.make_async_remote_copy(src_ref=rows(*block) if src is None else src, dst_ref=rows(*block),
                                                send_sem=send_sems.at[k], recv_sem=recv_sems.at[k], device_id=to,
                                                device_id_type=MESH)

        mine = pltpu.make_async_copy(x_ref, rows(*me), local_sem)
        mine.start()
        first = [copy(0, me, sibling, src=x_ref)]
        first += [copy(1 + j, me, (*chip, c), src=x_ref) for j, chip in enumerate(chips)]
        for cp in first:
            cp.start()
        passed = [copy(4 + j, (*chip, c), sibling) for j, chip in enumerate(chips)]
        for j, chip in enumerate(chips):
            copy(1 + j, (*chip, c), me).wait_recv()
            passed[j].start()
        copy(0, sibling, me).wait_recv()
        for j, chip in enumerate(chips):
            copy(4 + j, (*chip, 1 - c), me).wait_recv()
        for cp in first + passed:
            cp.wait_send()
        mine.wait()

    out = _pc(body, name=name, out_shape=jax.ShapeDtypeStruct((8 * m_per, n), a.dtype),
              in_specs=[pl.BlockSpec(memory_space=pltpu.VMEM)], out_specs=pl.BlockSpec(memory_space=pltpu.VMEM),
              scratch_shapes=[pltpu.SemaphoreType.DMA((7,)), pltpu.SemaphoreType.DMA((7,)), pltpu.SemaphoreType.DMA],
              compiler_params=pltpu.CompilerParams(vmem_limit_bytes=VMEM_LIMIT))(a)
    return out.reshape(8, m_per, n)


def _chip_gather(name, pack):
    n_l, n_r, n_c = pack.shape
    half = n_r // 2

    def body(p_ref, o_ref, send_sems, recv_sems):
        x, y, c = _place()
        sibling = (x, y, 1 - c)
        chips = [(1 - x, y), (x, 1 - y), (1 - x, 1 - y)]

        def slab(px, py, pc):
            return o_ref.at[2 * px + py, :, pl.ds(pc * half, half), :]

        def copy(k, src, dst, to):
            return pltpu.make_async_remote_copy(src_ref=src, dst_ref=dst, send_sem=send_sems.at[k], recv_sem=recv_sems.at[k],
                                                device_id=to, device_id_type=MESH)

        first = [copy(j, p_ref.at[:, pl.ds(c * half, half), :], slab(x, y, c), (*chip, c)) for j, chip in enumerate(chips)]
        for cp in first:
            cp.start()
        passed = [copy(3 + j, slab(*chip, c), slab(*chip, c), sibling) for j, chip in enumerate(chips)]
        for j, chip in enumerate(chips):
            copy(j, slab(*chip, c), slab(*chip, c), (*chip, c)).wait_recv()
            passed[j].start()
        for j, chip in enumerate(chips):
            copy(3 + j, slab(*chip, 1 - c), slab(*chip, 1 - c), sibling).wait_recv()
        for cp in first + passed:
            cp.wait_send()

    return _pc(body, name=name, out_shape=jax.ShapeDtypeStruct((4, n_l, n_r, n_c), pack.dtype), in_specs=[ANY], out_specs=ANY,
               scratch_shapes=[pltpu.SemaphoreType.DMA((6,)), pltpu.SemaphoreType.DMA((6,))])(pack)


def _pair_swap(name, give):
    def body(g_ref, o_ref, send_sem, recv_sem):
        x, y, c = _place()
        cp = pltpu.make_async_remote_copy(src_ref=g_ref, dst_ref=o_ref, send_sem=send_sem, recv_sem=recv_sem,
                                          device_id=(x, y, 1 - c), device_id_type=MESH)
        cp.start()
        cp.wait()

    return _pc(body, name=name, out_shape=jax.ShapeDtypeStruct(give.shape, give.dtype), in_specs=[ANY], out_specs=ANY,
               scratch_shapes=[pltpu.SemaphoreType.DMA, pltpu.SemaphoreType.DMA])(give)


def _chip_exchange(name, parts):
    def body(p_ref, o_ref, send_sems, recv_sems):
        x, y, c = _place()
        me = 2 * x + y
        chips = [(1 - x, y), (x, 1 - y), (1 - x, 1 - y)]

        def copy(k, src, dst, to):
            return pltpu.make_async_remote_copy(src_ref=src, dst_ref=dst, send_sem=send_sems.at[k], recv_sem=recv_sems.at[k],
                                                device_id=to, device_id_type=MESH)

        sends = [copy(j, p_ref.at[2 * px + py], o_ref.at[me], (px, py, c)) for j, (px, py) in enumerate(chips)]
        for cp in sends:
            cp.start()
        for j, (px, py) in enumerate(chips):
            copy(j, p_ref.at[2 * px + py], o_ref.at[2 * px + py], (px, py, c)).wait_recv()
        for cp in sends:
            cp.wait_send()

    return _pc(body, name=name, out_shape=jax.ShapeDtypeStruct(parts.shape, parts.dtype), in_specs=[ANY], out_specs=ANY,
               scratch_shapes=[pltpu.SemaphoreType.DMA((3,)), pltpu.SemaphoreType.DMA((3,))])(parts)


N_CHIP = 4
BIG = (("w_in", (1024, 2822), 1, (1024, 2822)), ("w_branch", (3, 768, 256), 2, (2304, 256)),
       ("w_out", (256, 1024), 0, (256, 1024)), ("w_ffn_in", (1024, 1408), 1, (1024, 1408)),
       ("w_ffn_out", (704, 1024), 0, (704, 1024)))
G768 = ((0, 3072), (3072, 3840), (7436, 8204))
GXBC, GQKV, GGATE = (3840, 5120), (5132, 7436), (8216, 11288)
GSMALL = ((5120, 5132), (8204, 8210), (8210, 8216))
W768, WXBC, WGATE = 4608, CONV_CH, 3 * D_MODEL
IN_PAD = W768 + WXBC + QKV_W + WGATE + SMALL_W


def _join_shards(slabs, axis, shard_shape):
    n_l = slabs.shape[1]
    parts = [slabs[j].reshape((n_l,) + shard_shape) for j in range(N_CHIP)]
    return jnp.concatenate(parts, axis=axis + 1)


def _split_shards(full, axis, rows_cols):
    n_l = full.shape[0]
    size = full.shape[axis + 1] // N_CHIP
    return jnp.stack([lax.slice_in_dim(full, j * size, (j + 1) * size, axis=axis + 1).reshape((n_l,) + rows_cols)
                      for j in range(N_CHIP)])


def _gather_weights(w, chip, big=BIG):
    out = {}
    for n, shape, ax, rc in big:
        n_l = w[n].shape[0]
        mine = w[n].astype(BF).reshape((n_l,) + rc)
        slabs = lax.dynamic_update_slice(_chip_gather("gather_" + n, mine), mine[None], (chip, 0, 0, 0))
        out[n] = _join_shards(slabs, ax, shape)
    return out


def _reduce_grads(full_grads, chip, core, big=BIG):
    out = {}
    for n, shape, ax, (rows, cols) in big:
        n_l = full_grads[n].shape[0]
        slabs = _split_shards(full_grads[n], ax, (rows, cols))
        half = rows // 2
        keep = lax.dynamic_slice_in_dim(slabs, core * half, half, axis=2).reshape(-1, cols)
        give = lax.dynamic_slice_in_dim(slabs, (1 - core) * half, half, axis=2).reshape(-1, cols)
        got = _pair_swap("pair_swap_" + n, give)
        (pair_sum,) = _ew("pair_sum_" + n, lambda a, b: (a.astype(F32) + b.astype(F32),), [keep, got], [BF])
        pair_sum = pair_sum.reshape(N_CHIP, n_l * half, cols)
        parts = lax.dynamic_update_slice(_chip_exchange("chip_exchange_" + n, pair_sum),
                                         lax.dynamic_slice_in_dim(pair_sum, chip, 1, axis=0), (chip, 0, 0))
        mine = _sum_leading("chip_sum_" + n, parts, F32).reshape(n_l, half, cols)
        theirs = _pair_swap("pair_share_" + n, mine)
        full = jnp.concatenate([jnp.where(core == 0, mine, theirs), jnp.where(core == 0, theirs, mine)], axis=1)
        out[n] = full.reshape((n_l,) + shape)
    return out


def _regroup_w_in(w):
    cat = lambda spans: jnp.concatenate([w[:, a:b] for a, b in spans], axis=1)
    small = jnp.concatenate([cat(GSMALL), jnp.zeros((w.shape[0], SMALL_W - 24), w.dtype)], axis=1)
    return cat(G768), cat((GXBC,)), cat((GQKV,)), cat((GGATE,)), small


def _ungroup_w_in(d):
    o_xbc, o_qkv, o_gate, o_small = W768, W768 + WXBC, W768 + WXBC + QKV_W, W768 + WXBC + QKV_W + WGATE
    spans = ((0, 3072), (3072, 3840), (o_xbc, o_xbc + WXBC), (o_small, o_small + 12), (o_qkv, o_qkv + QKV_W),
             (3840, 4608), (o_small + 12, o_small + 18), (o_small + 18, o_small + 24), (o_gate, o_gate + WGATE))
    return jnp.concatenate([d[:, a:b] for a, b in spans], axis=1)


def _lane_pad(v, off):
    return jnp.pad(v, (off, LANES - off - v.shape[0]))[None, :]


STATE6 = (N_HEAD6, HEAD, HEAD)


def _mixer_inputs(sv, lp):
    p768, pxbc, pqkv, psmall = sv["p768"], sv["pxbc"], sv["pqkv"], sv["psmall"]
    hgrn = ([(p768, MIX_W, j) for j in range(4)], [], [lp["lb"], lp["hgrn_norm"]])
    ssd = ([(p768, MIX_W, 4), (pxbc, CONV_CH, 0), (psmall, LANES, 0)], [1],
           [lp["ssm_conv_w"], lp["ssm_conv_b"], lp["ssm_dt_bias"], lp["ssm_a_log"], lp["ssm_d"], lp["ssm_norm"]])
    gdn = ([(pqkv, QKV_W, 0), (p768, MIX_W, 5), (psmall, LANES, 0)], [0],
           [lp["gdn_conv_w"], lp["gdn_dt_bias"], lp["gdn_a_log"], lp["gdn_norm"]])
    return hgrn, ssd, gdn


def _layer_fwd(x, md, lw, lp):
    sv = {"x": x}
    (sv["h1"],) = _tile_fwd("lnmod1", _lnmod_fn, [(x, D_MODEL, 0)], [lp["norm_mix"], md["sc1"], md["sh1"]], [(D_MODEL, BF)])
    for nm in ("768", "xbc", "qkv", "gate", "small"):
        sv["p" + nm] = _mm(sv["h1"], lw["win_" + nm], "nn", F32, "proj_" + nm)
    hgrn, ssd, gdn = _mixer_inputs(sv, lp)
    sv["y_h"], sv["st_h"] = _scan_fwd("hgrn_fwd", _hgrn_chunk, *hgrn, MIX_W, STATE6)
    sv["y_s"], sv["st_s"] = _scan_fwd("ssd_fwd", _ssd_chunk, *ssd, MIX_W, STATE6)
    sv["y_g"], sv["st_g"] = _scan_fwd("gdn_fwd", _gdn_chunk, *gdn, MIX_W, STATE6)
    (sv["merged"],) = _tile_fwd("merge", _merge_fn, _merge_tiles(sv), [lw["w_branch"], lp["b_merge"]], [(D_MODEL, BF)])
    sv["out"] = _mm(sv["merged"], lw["w_out"], "nn", F32, "out_proj")
    (sv["x_mid"],) = _tile_fwd("resid1", _resid_fn, [(x, D_MODEL, 0), (sv["out"], D_MODEL, 0)], [md["g1"]], [(D_MODEL, F32)])
    (sv["h2"],) = _tile_fwd("lnmod2", _lnmod_fn, [(sv["x_mid"], D_MODEL, 0)], [lp["norm_ffn"], md["sc2"], md["sh2"]],
                            [(D_MODEL, BF)])
    sv["gu"] = _mm(sv["h2"], lw["w_ffn_in"], "nn", F32, "ffn_in")
    (sv["act"],) = _tile_fwd("swiglu", _swiglu_fn, [(sv["gu"], 2 * FFN_H, 0)], [], [(FFN_H, BF)])
    sv["o2"] = _mm(sv["act"], lw["w_ffn_out"], "nn", F32, "ffn_out")
    (x_out,) = _tile_fwd("resid2", _resid_fn, [(sv["x_mid"], D_MODEL, 0), (sv["o2"], D_MODEL, 0)], [md["g2"]], [(D_MODEL, F32)])
    return x_out, sv


def _merge_tiles(sv):
    return [(sv["y_h"], MIX_W, 0), (sv["y_s"], MIX_W, 0), (sv["y_g"], MIX_W, 0), (sv["pgate"], WGATE, 0)]


def _layer_bwd(dx_out, sv, md, lw, lp):
    g = {}
    x, x_mid = sv["x"], sv["x_mid"]
    d_xmid, d_o2, g["g2"] = _tile_bwd("resid2_b", _resid_fn, [(x_mid, D_MODEL, 0), (sv["o2"], D_MODEL, 0)], [md["g2"]], [dx_out],
                                      [F32, BF])
    d_act = _mm(d_o2, lw["w_ffn_out"], "nt", F32, "ffn_out_dx")
    g["w_ffn_out"] = _mm(sv["act"], d_o2, "tn", BF, "ffn_out_dw")
    (d_gu,) = _tile_bwd("swiglu_b", _swiglu_fn, [(sv["gu"], 2 * FFN_H, 0)], [], [d_act], [BF])
    d_h2 = _mm(d_gu, lw["w_ffn_in"], "nt", F32, "ffn_in_dx")
    g["w_ffn_in"] = _mm(sv["h2"], d_gu, "tn", BF, "ffn_in_dw")
    d_xmid, g["norm_ffn"], g["sc2"], g["sh2"] = _tile_bwd(
        "lnmod2_b", _lnmod_fn, [(x_mid, D_MODEL, 0)], [lp["norm_ffn"], md["sc2"], md["sh2"]], [d_h2], [F32], add_to=(0, d_xmid))
    d_x, d_out, g["g1"] = _tile_bwd("resid1_b", _resid_fn, [(x, D_MODEL, 0), (sv["out"], D_MODEL, 0)], [md["g1"]], [d_xmid],
                                    [F32, BF])
    d_merged = _mm(d_out, lw["w_out"], "nt", F32, "out_proj_dx")
    g["w_out"] = _mm(sv["merged"], d_out, "tn", BF, "out_proj_dw")
    d_yh, d_ys, d_yg, d_gate, g["w_branch"], g["b_merge"] = _tile_bwd(
        "merge_b", _merge_fn, _merge_tiles(sv), [lw["w_branch"], lp["b_merge"]], [d_merged], [F32, F32, F32, BF])
    hgrn, ssd, gdn = _mixer_inputs(sv, lp)
    d_q, d_f, d_v, d_g, g["lb"], g["hgrn_norm"] = _scan_bwd("hgrn_bwd", _hgrn_chunk, *hgrn, sv["st_h"], d_yh, [BF] * 4)
    (d_sz, d_xbc, d_small, g["ssm_conv_w"], g["ssm_conv_b"], g["ssm_dt_bias"], g["ssm_a_log"], g["ssm_d"],
     g["ssm_norm"]) = _scan_bwd("ssd_bwd", _ssd_chunk, *ssd, sv["st_s"], d_ys, [BF, BF, F32])
    d_qkv, d_gz, d_small, g["gdn_conv_w"], g["gdn_dt_bias"], g["gdn_a_log"], g["gdn_norm"] = _scan_bwd(
        "gdn_bwd", _gdn_chunk, *gdn, sv["st_g"], d_yg, [BF, BF, BF], extra=(2, d_small))
    d_proj = jnp.concatenate([d_q, d_f, d_v, d_g, d_sz, d_gz, d_xbc, d_qkv, d_gate, d_small,
                              jnp.zeros((x.shape[0], SMALL_W - LANES), BF)], axis=1)
    d_h1 = _mm(d_proj, lw["win_all"], "nt", F32, "proj_dx")
    g["w_in"] = _ungroup_w_in(_mm(sv["h1"], d_proj, "tn", BF, "proj_dw"))
    d_x, g["norm_mix"], g["sc1"], g["sh1"] = _tile_bwd(
        "lnmod1_b", _lnmod_fn, [(x, D_MODEL, 0)], [lp["norm_mix"], md["sc1"], md["sh1"]], [d_h1], [F32], add_to=(0, d_x))
    return d_x, g


SMALL_REPL = ("norm_mix", "norm_ffn", "b_merge", "hgrn_lb_logits", "hgrn_norm", "ssm_conv_w", "ssm_conv_b", "ssm_dt_bias",
              "ssm_a_log", "ssm_d", "ssm_norm", "gdn_conv_w", "gdn_dt_bias", "gdn_a_log", "gdn_norm", "norm_final")
WEIGHTS = ("w_ada", "b_ada", "norm_mix", "norm_ffn", "w_in", "b_merge", "hgrn_lb_logits", "hgrn_norm", "ssm_conv_w",
           "ssm_conv_b", "ssm_dt_bias", "ssm_a_log", "ssm_d", "ssm_norm", "gdn_conv_w", "gdn_dt_bias", "gdn_a_log",
           "gdn_norm", "w_branch", "w_out", "w_ffn_in", "w_ffn_out", "norm_final")
SMALL_ROWS = 120


def _pad_rows(flat, n_rows, n_cols):
    return jnp.concatenate([flat, jnp.zeros((n_rows * n_cols - flat.shape[0],), flat.dtype)]).reshape(n_rows, n_cols)


def _device_step(x, tgt, mod, lb, wfull, sp):
    mds, lps, svs = [], [], []
    h = x
    for l in range(DEPTH):
        md = {n: mod[l, i * D_MODEL:(i + 1) * D_MODEL][None, :] for i, n in enumerate(("sh1", "sc1", "g1", "sh2", "sc2", "g2"))}
        lp = {n: sp[n][l][None, :] for n in ("norm_mix", "norm_ffn", "b_merge", "hgrn_norm", "ssm_conv_b", "ssm_norm", "gdn_norm")}
        lp["lb"] = lb[l][None, :]
        lp["ssm_conv_w"], lp["gdn_conv_w"] = sp["ssm_conv_w"][l], sp["gdn_conv_w"][l]
        for n in ("ssm_dt_bias", "ssm_a_log", "ssm_d"):
            lp[n] = _lane_pad(sp[n][l], DT_OFF)
        for n in ("gdn_dt_bias", "gdn_a_log"):
            lp[n] = _lane_pad(sp[n][l], GA_OFF)
        h, sv = _layer_fwd(h, md, wfull[l], lp)
        mds.append(md), lps.append(lp), svs.append(sv)
    loss, dh, d_nf = _final_loss(h, tgt, sp["norm_final"][None, :])
    grads = [None] * DEPTH
    for l in reversed(range(DEPTH)):
        dh, grads[l] = _layer_bwd(dh, svs[l], mds[l], wfull[l], lps[l])
    return loss, dh, d_nf, grads


def kernel(x, c, w_ada, b_ada, norm_mix, norm_ffn, w_in, b_merge, hgrn_lb_logits, hgrn_norm, ssm_conv_w, ssm_conv_b, ssm_dt_bias, ssm_a_log, ssm_d, ssm_norm, gdn_conv_w, gdn_dt_bias, gdn_a_log, gdn_norm, w_branch, w_out, w_ffn_in, w_ffn_out, norm_final, loss_target, m_w_ada, m_b_ada, m_norm_mix, m_norm_ffn, m_w_in, m_b_merge, m_hgrn_lb_logits, m_hgrn_norm, m_ssm_conv_w, m_ssm_conv_b, m_ssm_dt_bias, m_ssm_a_log, m_ssm_d, m_ssm_norm, m_gdn_conv_w, m_gdn_dt_bias, m_gdn_a_log, m_gdn_norm, m_w_branch, m_w_out, m_w_ffn_in, m_w_ffn_out, m_norm_final, v_w_ada, v_b_ada, v_norm_mix, v_norm_ffn, v_w_in, v_b_merge, v_hgrn_lb_logits, v_hgrn_norm, v_ssm_conv_w, v_ssm_conv_b, v_ssm_dt_bias, v_ssm_a_log, v_ssm_d, v_ssm_norm, v_gdn_conv_w, v_gdn_dt_bias, v_gdn_a_log, v_gdn_norm, v_w_branch, v_w_out, v_w_ffn_in, v_w_ffn_out, v_norm_final):
    w = dict(w_ada=w_ada, b_ada=b_ada, norm_mix=norm_mix, norm_ffn=norm_ffn, w_in=w_in, b_merge=b_merge,
             hgrn_lb_logits=hgrn_lb_logits, hgrn_norm=hgrn_norm, ssm_conv_w=ssm_conv_w, ssm_conv_b=ssm_conv_b,
             ssm_dt_bias=ssm_dt_bias, ssm_a_log=ssm_a_log, ssm_d=ssm_d, ssm_norm=ssm_norm, gdn_conv_w=gdn_conv_w,
             gdn_dt_bias=gdn_dt_bias, gdn_a_log=gdn_a_log, gdn_norm=gdn_norm, w_branch=w_branch, w_out=w_out,
             w_ffn_in=w_ffn_in, w_ffn_out=w_ffn_out, norm_final=norm_final)
    m = dict(w_ada=m_w_ada, b_ada=m_b_ada, norm_mix=m_norm_mix, norm_ffn=m_norm_ffn, w_in=m_w_in, b_merge=m_b_merge,
             hgrn_lb_logits=m_hgrn_lb_logits, hgrn_norm=m_hgrn_norm, ssm_conv_w=m_ssm_conv_w, ssm_conv_b=m_ssm_conv_b,
             ssm_dt_bias=m_ssm_dt_bias, ssm_a_log=m_ssm_a_log, ssm_d=m_ssm_d, ssm_norm=m_ssm_norm, gdn_conv_w=m_gdn_conv_w,
             gdn_dt_bias=m_gdn_dt_bias, gdn_a_log=m_gdn_a_log, gdn_norm=m_gdn_norm, w_branch=m_w_branch, w_out=m_w_out,
             w_ffn_in=m_w_ffn_in, w_ffn_out=m_w_ffn_out, norm_final=m_norm_final)
    v = dict(w_ada=v_w_ada, b_ada=v_b_ada, norm_mix=v_norm_mix, norm_ffn=v_norm_ffn, w_in=v_w_in, b_merge=v_b_merge,
             hgrn_lb_logits=v_hgrn_lb_logits, hgrn_norm=v_hgrn_norm, ssm_conv_w=v_ssm_conv_w, ssm_conv_b=v_ssm_conv_b,
             ssm_dt_bias=v_ssm_dt_bias, ssm_a_log=v_ssm_a_log, ssm_d=v_ssm_d, ssm_norm=v_ssm_norm, gdn_conv_w=v_gdn_conv_w,
             gdn_dt_bias=v_gdn_dt_bias, gdn_a_log=v_gdn_a_log, gdn_norm=v_gdn_norm, w_branch=v_w_branch, w_out=v_w_out,
             w_ffn_in=v_w_ffn_in, w_ffn_out=v_w_ffn_out, norm_final=v_norm_final)
    xi, yi, ci = _place()
    chip, me = 2 * xi + yi, 4 * xi + 2 * yi + ci
    seq = x.shape[1]

    conv_flat = jnp.concatenate([ssm_conv_w.reshape(-1), gdn_conv_w.reshape(-1)])
    n_conv = conv_flat.shape[0]
    first = _all_gather_small("gather_c_conv", _pad_rows(jnp.concatenate([c[0], conv_flat]), 16, D_MODEL))
    c_all = first[:, 0, :]
    conv_all = first[0::2].reshape(N_CHIP, -1)[:, D_MODEL:D_MODEL + n_conv]
    n_ssm = ssm_conv_w.size
    sp = dict(w)
    sp["ssm_conv_w"] = jnp.concatenate([conv_all[j, :n_ssm].reshape(ssm_conv_w.shape) for j in range(N_CHIP)], axis=2)
    sp["gdn_conv_w"] = jnp.concatenate([conv_all[j, n_ssm:].reshape(gdn_conv_w.shape) for j in range(N_CHIP)], axis=2)

    ada_cols = w_ada.shape[2]
    mod_part = _ada_fwd(c_all, w_ada, lax.dynamic_slice_in_dim(b_ada, chip * ada_cols, ada_cols, axis=1))
    mod_all = _all_gather_small("gather_mod", mod_part.reshape(DEPTH * 8, ada_cols))[0::2].reshape(N_CHIP, DEPTH, 8, ada_cols)
    mod = lax.dynamic_index_in_dim(mod_all, me, axis=2, keepdims=False).transpose(1, 0, 2).reshape(DEPTH, N_CHIP * ada_cols)
    lb = _lb_fwd(hgrn_lb_logits)

    gathered = _gather_weights(w, chip)
    wfull = []
    for l in range(DEPTH):
        full = {n: gathered[n][l] for n, _, _, _ in BIG}
        for nm, part in zip(("768", "xbc", "qkv", "gate", "small"), _regroup_w_in(full["w_in"])):
            full["win_" + nm] = part
        full["win_all"] = jnp.concatenate([full["win_" + nm] for nm in ("768", "xbc", "qkv", "gate", "small")], axis=1)
        wfull.append(full)

    loss8, d_x, d_nf, lg = _device_step(x[0], loss_target[0], mod, lb, wfull, sp)

    grad = _reduce_grads({n: jnp.stack([lg[l][n].astype(BF) for l in range(DEPTH)]) for n, _, _, _ in BIG}, chip, ci)

    dmod = jnp.stack([jnp.concatenate([lg[l][n] for n in ("sh1", "sc1", "g1", "sh2", "sc2", "g2")], axis=1)[0] for l in range(DEPTH)])
    d_lb = jnp.stack([lg[l]["lb"][0] for l in range(DEPTH)])
    contrib = {
        "norm_mix": jnp.stack([lg[l]["norm_mix"][0] for l in range(DEPTH)]),
        "norm_ffn": jnp.stack([lg[l]["norm_ffn"][0] for l in range(DEPTH)]),
        "b_merge": jnp.stack([lg[l]["b_merge"][0] for l in range(DEPTH)]),
        "hgrn_lb_logits": _lb_bwd(hgrn_lb_logits, d_lb),
        "hgrn_norm": jnp.stack([lg[l]["hgrn_norm"][0] for l in range(DEPTH)]),
        "ssm_conv_w": jnp.stack([lg[l]["ssm_conv_w"] for l in range(DEPTH)]),
        "ssm_conv_b": jnp.stack([lg[l]["ssm_conv_b"][0] for l in range(DEPTH)]),
        "ssm_dt_bias": jnp.stack([lg[l]["ssm_dt_bias"][0, DT_OFF:DT_OFF + 12] for l in range(DEPTH)]),
        "ssm_a_log": jnp.stack([lg[l]["ssm_a_log"][0, DT_OFF:DT_OFF + 12] for l in range(DEPTH)]),
        "ssm_d": jnp.stack([lg[l]["ssm_d"][0, DT_OFF:DT_OFF + 12] for l in range(DEPTH)]),
        "ssm_norm": jnp.stack([lg[l]["ssm_norm"][0] for l in range(DEPTH)]),
        "gdn_conv_w": jnp.stack([lg[l]["gdn_conv_w"] for l in range(DEPTH)]),
        "gdn_dt_bias": jnp.stack([lg[l]["gdn_dt_bias"][0, GA_OFF:GA_OFF + 6] for l in range(DEPTH)]),
        "gdn_a_log": jnp.stack([lg[l]["gdn_a_log"][0, GA_OFF:GA_OFF + 6] for l in range(DEPTH)]),
        "gdn_norm": jnp.stack([lg[l]["gdn_norm"][0] for l in range(DEPTH)]),
        "norm_final": d_nf[0],
    }
    flat = jnp.concatenate([dmod.reshape(-1)] + [contrib[n].reshape(-1) for n in SMALL_REPL] + [loss8[0, 0:1]])
    small_all = _all_gather_small("gather_small_grads", _pad_rows(flat, SMALL_ROWS, D_MODEL))
    total = _sum_leading("small_grad_sum", small_all, F32).reshape(-1)
    n_mod = dmod.size
    grad["b_ada"] = total[:n_mod].reshape(b_ada.shape)
    off = n_mod
    full_small = {}
    for n in SMALL_REPL:
        full_small[n] = total[off:off + contrib[n].size].reshape(contrib[n].shape)
        off += contrib[n].size
    loss = total[off]
    for n in SMALL_REPL:
        if n in ("ssm_conv_w", "gdn_conv_w"):
            cols = w[n].shape[2]
            grad[n] = lax.dynamic_slice_in_dim(full_small[n], chip * cols, cols, axis=2)
        else:
            grad[n] = full_small[n]
    dmod_cols = lax.dynamic_slice_in_dim(small_all[:, :n_mod // D_MODEL, :].reshape(8, DEPTH, -1), chip * ada_cols, ada_cols, axis=2)
    grad["w_ada"] = _ada_bwd(c_all, dmod_cols.transpose(1, 0, 2))

    delta, new_m, new_v = {}, {}, {}
    big_names = ("w_ada",) + tuple(n for n, _, _, _ in BIG)
    for n in big_names:
        delta[n], new_m[n], new_v[n] = _adamw("adamw_" + n, w[n], grad[n], m[n], v[n])
    small_names = [n for n in WEIGHTS if n not in big_names]
    packs = [_pad_rows(jnp.concatenate([d[n].reshape(-1) for n in small_names]), 584, LANES) for d in (w, grad, m, v)]
    outs = _ew("adamw_small", _adamw_fn, packs, [F32, F32, F32])
    off = 0
    for n in small_names:
        for dst, o in zip((delta, new_m, new_v), outs):
            dst[n] = o.reshape(-1)[off:off + w[n].size].reshape(w[n].shape)
        off += w[n].size
    return (loss, d_x[None], *[grad[n] for n in WEIGHTS], *[delta[n] for n in WEIGHTS], *[new_m[n] for n in WEIGHTS],
            *[new_v[n] for n in WEIGHTS])
```

```python
import functools

import jax
import jax.numpy as jnp
from jax import lax
from jax.experimental import pallas as pl
from jax.experimental.pallas import tpu as pltpu

F32 = jnp.float32
BF = jnp.bfloat16
HI = lax.Precision.HIGHEST

D_MODEL = 1024
DEPTH = 4
CHUNK = 64
MIX_W = 768
HEAD = 128
N_HEAD6 = 6
SSM_P = 64
SSM_N = 128
CONV_CH = 1280
QKV_W = 2304
FFN_H = 2816
IN_WIDTH = 11288
NORM_EPS = 1e-6
F_MIN = 1e-30
HALO = 8
HGRN_SUB = 16
SMALL_W = 512
LANES = 128
DT_OFF, GB_OFF, GA_OFF = 0, 12, 18

ADAM_LR, ADAM_B1, ADAM_B2, ADAM_EPS, ADAM_WD, ADAM_STEP = 0.001, 0.9, 0.999, 1e-08, 0.01, 10

VMEM_LIMIT = 56 * 1024 * 1024
TOKEN_TILE = 256


def _pc(body, **kw):
    return pl.pallas_call(body, **kw)


def _cparams(sem):
    return pltpu.CompilerParams(dimension_semantics=sem, vmem_limit_bytes=VMEM_LIMIT)


def _bdot(a, b):
    return jnp.dot(a.astype(BF), b.astype(BF), preferred_element_type=F32)


def _bdot_nt(a, b):
    return lax.dot_general(a.astype(BF), b.astype(BF), (((1,), (1,)), ((), ())), preferred_element_type=F32)


def _bdot_tn(a, b):
    return lax.dot_general(a.astype(BF), b.astype(BF), (((0,), (0,)), ((), ())), preferred_element_type=F32)


def _silu(x):
    return x * jax.nn.sigmoid(x)


def _tri_mask(n, strict=False):
    t = lax.broadcasted_iota(jnp.int32, (n, n), 0)
    s = lax.broadcasted_iota(jnp.int32, (n, n), 1)
    return (s < t) if strict else (s <= t)


def _masked_exp(diff, mask):
    return jnp.where(mask, jnp.exp(jnp.where(mask, diff, 0.0)), 0.0)


def _split_bf16(x, n):
    parts, rest = [], x
    for _ in range(n):
        p = rest.astype(BF)
        parts.append(p)
        rest = rest - p.astype(F32)
    return parts


def _tri_sum(x, reverse):
    n, w = x.shape
    t = lax.broadcasted_iota(jnp.int32, (n, n), 0)
    s = lax.broadcasted_iota(jnp.int32, (n, n), 1)
    tri = jnp.where((s >= t) if reverse else (s <= t), 1.0, 0.0).astype(BF)
    y = jnp.dot(tri, jnp.concatenate(_split_bf16(x, 3), axis=1), preferred_element_type=F32)
    return y[:, :w] + y[:, w:2 * w] + y[:, 2 * w:]


@jax.custom_vjp
def _cumsum_rows(x):
    return _tri_sum(x, False)


_cumsum_rows.defvjp(lambda x: (_tri_sum(x, False), None), lambda _, g: (_tri_sum(g, True),))


def _dot_split(a, b, transpose_a=False):
    dims = (((0,), (0,)) if transpose_a else ((1,), (0,)), ((), ()))
    a_hi, a_lo = _split_bf16(a, 2)
    b_hi, b_lo = _split_bf16(b, 2)
    w = b.shape[1]
    y = lax.dot_general(a_hi, jnp.concatenate([b_hi, b_lo], axis=1), dims, preferred_element_type=F32)
    return y[:, :w] + y[:, w:] + lax.dot_general(a_lo, b_hi, dims, preferred_element_type=F32)


def _rms(x, w):
    return x * lax.rsqrt(jnp.mean(x * x, axis=-1, keepdims=True) + NORM_EPS) * w


def _causal_conv(halo, x, w):
    ext = jnp.concatenate([halo, x], axis=0)
    n = x.shape[0]
    acc = w[0:1, :] * ext[HALO - 3:HALO - 3 + n, :]
    for i in range(1, 4):
        acc = acc + w[i:i + 1, :] * ext[HALO - 3 + i:HALO - 3 + i + n, :]
    return acc


def _unit_lower_inverses(mats):
    n = mats[0].shape[0]
    t = lax.broadcasted_iota(jnp.int32, (n, n), 0)
    s_ = lax.broadcasted_iota(jnp.int32, (n, n), 1)
    xs = [jnp.where(t == s_, 1.0, 0.0).astype(F32) for _ in mats]
    for s in range(n - 1):
        r0 = 8 * ((s + 1) // 8)
        for i, a in enumerate(mats):
            x = xs[i]
            low = x[r0:] - a[r0:, s:s + 1] * x[s:s + 1, :]
            xs[i] = low if r0 == 0 else jnp.concatenate([x[:r0], low], axis=0)
    return xs


@jax.custom_vjp
def _unit_lower_solves(mats, rhss):
    return [_dot_split(inv, r) for inv, r in zip(_unit_lower_inverses(mats), rhss)]


def _uls_fwd(mats, rhss):
    invs = _unit_lower_inverses(mats)
    xs = [_dot_split(inv, r) for inv, r in zip(invs, rhss)]
    return xs, (invs, xs)


def _uls_bwd(res, gs):
    invs, xs = res
    ys = [_dot_split(inv, g, transpose_a=True) for inv, g in zip(invs, gs)]
    das = [jnp.where(_tri_mask(CHUNK, strict=True), -_bdot_nt(y, x), 0.0) for y, x in zip(ys, xs)]
    return das, ys


_unit_lower_solves.defvjp(_uls_fwd, _uls_bwd)


def _hgrn_chunk(tiles, halos, state, consts):
    q_raw, f_raw, v_all, g_raw = tiles
    lb, norm_w = consts
    q_all = _silu(q_raw)
    f = lb + (1.0 - lb) * jax.nn.sigmoid(f_raw)
    logf = jnp.log(jnp.maximum(f, F_MIN))
    k_all = (1.0 - lb) * jax.nn.sigmoid(-f_raw)
    b_all = _cumsum_rows(logf)
    sub = HGRN_SUB
    row = lax.broadcasted_iota(jnp.int32, (sub, 1), 0)
    src_row = lax.broadcasted_iota(jnp.int32, (CHUNK, 1), 0)
    src_lane = lax.broadcasted_iota(jnp.int32, (1, CHUNK), 1)
    heads = range(N_HEAD6)
    n_sub = CHUNK // sub
    cols = [slice(h * HEAD, (h + 1) * HEAD) for h in heads]
    qs, ks, vs, bs = ([a[:, sl] for sl in cols] for a in (q_all, k_all, v_all, b_all))
    o_inter = [_bdot_nt(qs[h] * jnp.exp(bs[h]), state[h]) for h in heads]
    blocks = [[None] * n_sub for _ in heads]
    for i in range(n_sub):
        r0 = i * sub
        for h in heads:
            if i > 0:
                ref = bs[h][r0 - 1:r0, :]
                blocks[h][i] = _bdot_nt(qs[h][r0:r0 + sub] * jnp.exp(bs[h][r0:r0 + sub] - ref),
                                        ks[h] * _masked_exp(ref - bs[h], src_row < r0))
            else:
                blocks[h][i] = jnp.zeros((sub, CHUNK), F32)
    for h in heads:
        for i in range(n_sub):
            r0 = i * sub
            qi, ki, bi = qs[h][r0:r0 + sub], ks[h][r0:r0 + sub], bs[h][r0:r0 + sub]
            for s in range(sub):
                e = _masked_exp(bi - bi[s:s + 1, :], row >= s)
                col = jnp.sum(qi * ki[s:s + 1, :] * e, axis=1, keepdims=True)
                blocks[h][i] = jnp.where(src_lane == r0 + s, col, blocks[h][i])
    os_ = [_bdot(jnp.concatenate(blocks[h], axis=0), vs[h]) + o_inter[h] for h in heads]
    ends = [bs[h][CHUNK - 1:CHUNK, :] for h in heads]
    new_states = [state[h] * jnp.exp(ends[h]) + _bdot_tn(vs[h], ks[h] * jnp.exp(ends[h] - bs[h])) for h in heads]
    outs = [_rms(os_[h], norm_w) * _silu(g_raw[:, cols[h]]) for h in heads]
    return (jnp.concatenate(outs, axis=1),), jnp.stack(new_states)


def _ssd_chunk(tiles, halos, state, consts):
    z, xbc_raw, small = tiles
    (halo,) = halos
    conv_w, conv_b, dt_bias, a_log, d_skip, norm_w = consts
    xbc = _silu(_causal_conv(halo, xbc_raw, conv_w) + conv_b)
    xs, bm, cm = xbc[:, :MIX_W], xbc[:, MIX_W:MIX_W + 2 * SSM_N], xbc[:, MIX_W + 2 * SSM_N:]
    dt = jax.nn.softplus(small + dt_bias)
    cum = _cumsum_rows(-jnp.exp(a_log) * dt)
    cum_t2 = jnp.concatenate([cum, cum], axis=0).T
    lane = lax.broadcasted_iota(jnp.int32, (1, LANES), 1)
    first = lane < SSM_P
    hm0 = jnp.where(first, 1.0, 0.0).astype(F32)
    hm1 = 1.0 - hm0
    src = jnp.where(first, lane, lane - SSM_P)
    tri2 = src <= lax.broadcasted_iota(jnp.int32, (CHUNK, 1), 0)
    pick = lambda a, b: jnp.where(first, a, b)
    bgs = [bm[:, g * SSM_N:(g + 1) * SSM_N] for g in range(2)]
    cgs = [cm[:, g * SSM_N:(g + 1) * SSM_N] for g in range(2)]
    gmats = [_bdot_nt(cgs[g], jnp.concatenate([bgs[g], bgs[g]], axis=0)) for g in range(2)]
    pairs = range(6)
    xps = [xs[:, p * LANES:(p + 1) * LANES] for p in pairs]
    c0s = [cum[:, 2 * p:2 * p + 1] for p in pairs]
    c1s = [cum[:, 2 * p + 1:2 * p + 2] for p in pairs]
    e0s = [cum[CHUNK - 1:CHUNK, 2 * p:2 * p + 1] for p in pairs]
    e1s = [cum[CHUNK - 1:CHUNK, 2 * p + 1:2 * p + 2] for p in pairs]
    segs = [_masked_exp(pick(c0s[p], c1s[p]) - pick(cum_t2[2 * p:2 * p + 1, :], cum_t2[2 * p + 1:2 * p + 2, :]), tri2)
            for p in pairs]
    vms = []
    for p in pairs:
        v = xps[p] * pick(dt[:, 2 * p:2 * p + 1], dt[:, 2 * p + 1:2 * p + 2])
        vms.append(jnp.concatenate([v * hm0, v * hm1], axis=0))
    y_intra = [_bdot(gmats[p // 3] * segs[p], vms[p]) for p in pairs]
    y_inter = [_bdot(jnp.concatenate([cgs[p // 3] * jnp.exp(c0s[p]), cgs[p // 3] * jnp.exp(c1s[p])], axis=1),
                     jnp.concatenate([state[p] * hm0, state[p] * hm1], axis=0)) for p in pairs]
    new_states = [_bdot_tn(jnp.concatenate([bgs[p // 3] * jnp.exp(e0s[p] - c0s[p]),
                                            bgs[p // 3] * jnp.exp(e1s[p] - c1s[p])], axis=0), vms[p])
                  + state[p] * pick(jnp.exp(e0s[p]), jnp.exp(e1s[p])) for p in pairs]
    ys = [y_intra[p] + y_inter[p] + pick(d_skip[:, 2 * p:2 * p + 1], d_skip[:, 2 * p + 1:2 * p + 2]) * xps[p] for p in pairs]
    y = jnp.concatenate(ys, axis=1) * _silu(z)
    gw = MIX_W // 2
    y = jnp.concatenate([_rms(y[:, g * gw:(g + 1) * gw], norm_w[:, g * gw:(g + 1) * gw]) for g in range(2)], axis=1)
    return (y,), jnp.stack(new_states)


def _gdn_chunk(tiles, halos, state, consts):
    qkv_raw, z, small = tiles
    (halo,) = halos
    conv_w, dt_bias, a_log, norm_w = consts
    qkv = _silu(_causal_conv(halo, qkv_raw, conv_w))
    beta_all = jax.nn.sigmoid(small)
    cum = _cumsum_rows(-jnp.exp(a_log) * jax.nn.softplus(small + dt_bias))
    cum_t = cum.T
    tri, tri_strict = _tri_mask(CHUNK), _tri_mask(CHUNK, strict=True)
    heads = range(N_HEAD6)
    qs, ks, betas, cs, ces, decays, rhss = [], [], [], [], [], [], []
    for h in heads:
        q = qkv[:, h * HEAD:(h + 1) * HEAD]
        k = qkv[:, MIX_W + h * HEAD:MIX_W + (h + 1) * HEAD]
        v = qkv[:, 2 * MIX_W + h * HEAD:2 * MIX_W + (h + 1) * HEAD]
        q = q * lax.rsqrt(jnp.sum(q * q, axis=-1, keepdims=True) + NORM_EPS) * (HEAD ** -0.5)
        k = k * lax.rsqrt(jnp.sum(k * k, axis=-1, keepdims=True) + NORM_EPS)
        beta = beta_all[:, GB_OFF + h:GB_OFF + h + 1]
        c, c_t = cum[:, GA_OFF + h:GA_OFF + h + 1], cum_t[GA_OFF + h:GA_OFF + h + 1, :]
        qs.append(q), ks.append(k), betas.append(beta), cs.append(c)
        ces.append(cum[CHUNK - 1:CHUNK, GA_OFF + h:GA_OFF + h + 1])
        decays.append(_masked_exp(c - c_t, tri))
        rhss.append(jnp.concatenate([v * beta, k * (beta * jnp.exp(c))], axis=1))
    qk_kks = [_bdot_nt(jnp.concatenate([qs[h], ks[h]], axis=0), ks[h]) for h in heads]
    sols = _unit_lower_solves([jnp.where(tri_strict, betas[h] * qk_kks[h][CHUNK:] * decays[h], 0.0) for h in heads], rhss)
    on_states = [_bdot(jnp.concatenate([sols[h][:, HEAD:], qs[h] * jnp.exp(cs[h])], axis=0), state[h]) for h in heads]
    us = [sols[h][:, :HEAD] - on_states[h][:CHUNK] for h in heads]
    os_ = [on_states[h][CHUNK:] + _bdot(qk_kks[h][:CHUNK] * decays[h], us[h]) for h in heads]
    new_states = [jnp.exp(ces[h]) * state[h] + _bdot_tn(ks[h] * jnp.exp(ces[h] - cs[h]), us[h]) for h in heads]
    outs = [_rms(os_[h], norm_w) * _silu(z[:, h * HEAD:(h + 1) * HEAD]) for h in heads]
    return (jnp.concatenate(outs, axis=1),), jnp.stack(new_states)


def _scan_fwd(name, fn, tiled, halo_idx, consts, out_width, state_shape):
    seq = tiled[0][0].shape[0]
    nc = seq // CHUNK
    n_t, n_h, n_c = len(tiled), len(halo_idx), len(consts)

    def body(*refs):
        t_refs, h_refs, c_refs = refs[:n_t], refs[n_t:n_t + n_h], refs[n_t + n_h:n_t + n_h + n_c]
        y_ref, save_ref, st_ref = refs[n_t + n_h + n_c:]
        i = pl.program_id(0)

        @pl.when(i == 0)
        def _():
            st_ref[...] = jnp.zeros_like(st_ref)

        flag = jnp.where(i > 0, 1.0, 0.0).astype(F32)
        st = st_ref[...]
        (y,), new = fn([r[...] for r in t_refs], [r[...] * flag for r in h_refs], st, [r[...] for r in c_refs])
        save_ref[0] = st
        y_ref[...] = y.astype(y_ref.dtype)
        st_ref[...] = new

    in_specs = [pl.BlockSpec((CHUNK, w), functools.partial(lambda i, cb: (i, cb), cb=cb)) for _, w, cb in tiled]
    in_specs += [pl.BlockSpec((HALO, tiled[j][1]),
                              functools.partial(lambda i, cb: (jnp.maximum(i * (CHUNK // HALO) - 1, 0), cb), cb=tiled[j][2]))
                 for j in halo_idx]
    in_specs += [pl.BlockSpec(c.shape, functools.partial(lambda i, nd: (0,) * nd, nd=c.ndim)) for c in consts]
    zeros = (0,) * len(state_shape)
    return _pc(
        body, name=name, grid=(nc,), in_specs=in_specs,
        out_specs=(pl.BlockSpec((CHUNK, out_width), lambda i: (i, 0)),
                   pl.BlockSpec((1,) + state_shape, lambda i: (i,) + zeros)),
        out_shape=(jax.ShapeDtypeStruct((seq, out_width), BF), jax.ShapeDtypeStruct((nc,) + state_shape, F32)),
        scratch_shapes=[pltpu.VMEM(state_shape, F32)],
        compiler_params=_cparams(("arbitrary",)),
    )(*[t[0] for t in tiled], *[tiled[j][0] for j in halo_idx], *consts)


def _scan_bwd(name, fn, tiled, halo_idx, consts, saved, dy, dtile_dtypes, extra=None):
    seq = tiled[0][0].shape[0]
    nc = seq // CHUNK
    n_t, n_h, n_c = len(tiled), len(halo_idx), len(consts)
    state_shape = saved.shape[1:]
    n_x = 0 if extra is None else 1

    def body(*refs):
        t_refs, h_refs, c_refs = refs[:n_t], refs[n_t:n_t + n_h], refs[n_t + n_h:n_t + n_h + n_c]
        pos = n_t + n_h + n_c
        save_ref, dy_ref = refs[pos], refs[pos + 1]
        x_refs = refs[pos + 2:pos + 2 + n_x]
        pos += 2 + n_x
        dt_refs, dc_refs = refs[pos:pos + n_t], refs[pos + n_t:pos + n_t + n_c]
        dst_ref = refs[pos + n_t + n_c]
        carry_refs = refs[pos + n_t + n_c + 1:]
        i = pl.program_id(0)

        @pl.when(i == 0)
        def _():
            dst_ref[...] = jnp.zeros_like(dst_ref)
            for r in carry_refs:
                r[...] = jnp.zeros_like(r)
            for r in dc_refs:
                r[...] = jnp.zeros_like(r)

        flag = jnp.where(i < nc - 1, 1.0, 0.0).astype(F32)
        tiles = [r[...] for r in t_refs]
        halos = [r[...] * flag for r in h_refs]
        cvals = [r[...] for r in c_refs]
        _, vjp = jax.vjp(fn, tiles, halos, save_ref[0], cvals)
        d_tiles, d_halos, d_state, d_consts = vjp(((dy_ref[...].astype(F32),), dst_ref[...]))
        dst_ref[...] = d_state
        for r, g in zip(dc_refs, d_consts):
            r[...] += g
        for j, (r, g) in enumerate(zip(dt_refs, d_tiles)):
            if extra is not None and extra[0] == j:
                g = g + x_refs[0][...].astype(F32)
            r[...] = g.astype(r.dtype)
            if j in halo_idx:
                cr = carry_refs[halo_idx.index(j)]
                r[CHUNK - HALO:CHUNK, :] = (g[CHUNK - HALO:CHUNK, :] + cr[...]).astype(r.dtype)
                cr[...] = d_halos[halo_idx.index(j)] * flag

    rev = lambda i: nc - 1 - i
    in_specs = [pl.BlockSpec((CHUNK, w), functools.partial(lambda i, cb: (rev(i), cb), cb=cb)) for _, w, cb in tiled]
    in_specs += [pl.BlockSpec((HALO, tiled[j][1]),
                              functools.partial(lambda i, cb: (jnp.maximum(rev(i) * (CHUNK // HALO) - 1, 0), cb), cb=tiled[j][2]))
                 for j in halo_idx]
    in_specs += [pl.BlockSpec(c.shape, functools.partial(lambda i, nd: (0,) * nd, nd=c.ndim)) for c in consts]
    zeros = (0,) * len(state_shape)
    in_specs += [pl.BlockSpec((1,) + state_shape, lambda i: (rev(i),) + zeros),
                 pl.BlockSpec((CHUNK, dy.shape[1]), lambda i: (rev(i), 0))]
    args = [t[0] for t in tiled] + [tiled[j][0] for j in halo_idx] + list(consts) + [saved, dy]
    if extra is not None:
        in_specs.append(pl.BlockSpec((CHUNK, extra[1].shape[1]), lambda i: (rev(i), 0)))
        args.append(extra[1])
    out_specs = [pl.BlockSpec((CHUNK, w), lambda i: (rev(i), 0)) for _, w, _ in tiled]
    out_specs += [pl.BlockSpec(c.shape, functools.partial(lambda i, nd: (0,) * nd, nd=c.ndim)) for c in consts]
    out_shape = [jax.ShapeDtypeStruct((seq, w), dtd) for (_, w, _), dtd in zip(tiled, dtile_dtypes)]
    out_shape += [jax.ShapeDtypeStruct(c.shape, F32) for c in consts]
    scratch = [pltpu.VMEM(state_shape, F32)] + [pltpu.VMEM((HALO, tiled[j][1]), F32) for j in halo_idx]
    return _pc(body, name=name, grid=(nc,), in_specs=in_specs, out_specs=tuple(out_specs), out_shape=tuple(out_shape),
               scratch_shapes=scratch, compiler_params=_cparams(("arbitrary",)))(*args)


def _tile_fwd(name, fn, tiled, consts, outs, tm=TOKEN_TILE):
    seq = tiled[0][0].shape[0]
    n_t, n_c = len(tiled), len(consts)

    def body(*refs):
        res = fn(*[r[...] for r in refs[:n_t + n_c]])
        for r, y in zip(refs[n_t + n_c:], res):
            r[...] = y.astype(r.dtype)

    in_specs = [pl.BlockSpec((tm, w), functools.partial(lambda i, cb: (i, cb), cb=cb)) for _, w, cb in tiled]
    in_specs += [pl.BlockSpec(c.shape, functools.partial(lambda i, nd: (0,) * nd, nd=c.ndim)) for c in consts]
    return _pc(body, name=name, grid=(seq // tm,), in_specs=in_specs,
               out_specs=tuple(pl.BlockSpec((tm, w), lambda i: (i, 0)) for w, _ in outs),
               out_shape=tuple(jax.ShapeDtypeStruct((seq, w), dtp) for w, dtp in outs),
               compiler_params=_cparams(("arbitrary",)))(*[t[0] for t in tiled], *consts)


def _tile_bwd(name, fn, tiled, consts, douts, dtile_dtypes, add_to=None, tm=TOKEN_TILE):
    seq = tiled[0][0].shape[0]
    n_t, n_c, n_o = len(tiled), len(consts), len(douts)
    n_x = 0 if add_to is None else 1
    keep = [j for j, dtp in enumerate(dtile_dtypes) if dtp is not None]

    def body(*refs):
        vals = [r[...].astype(F32) for r in refs[:n_t + n_c]]
        pos = n_t + n_c
        g_refs, x_refs = refs[pos:pos + n_o], refs[pos + n_o:pos + n_o + n_x]
        pos += n_o + n_x
        dt_refs, dc_refs = refs[pos:pos + len(keep)], refs[pos + len(keep):]
        i = pl.program_id(0)

        @pl.when(i == 0)
        def _():
            for r in dc_refs:
                r[...] = jnp.zeros_like(r)

        _, vjp = jax.vjp(fn, *vals)
        cts = vjp(tuple(g[...].astype(F32) for g in g_refs))
        for r, j in zip(dt_refs, keep):
            g = cts[j]
            if add_to is not None and add_to[0] == j:
                g = g + x_refs[0][...].astype(F32)
            r[...] = g.astype(r.dtype)
        for r, g in zip(dc_refs, cts[n_t:]):
            r[...] += g

    in_specs = [pl.BlockSpec((tm, w), functools.partial(lambda i, cb: (i, cb), cb=cb)) for _, w, cb in tiled]
    in_specs += [pl.BlockSpec(c.shape, functools.partial(lambda i, nd: (0,) * nd, nd=c.ndim)) for c in consts]
    in_specs += [pl.BlockSpec((tm, g.shape[1]), lambda i: (i, 0)) for g in douts]
    args = [t[0] for t in tiled] + list(consts) + list(douts)
    if add_to is not None:
        in_specs.append(pl.BlockSpec((tm, add_to[1].shape[1]), lambda i: (i, 0)))
        args.append(add_to[1])
    out_specs = [pl.BlockSpec((tm, tiled[j][1]), lambda i: (i, 0)) for j in keep]
    out_specs += [pl.BlockSpec(c.shape, functools.partial(lambda i, nd: (0,) * nd, nd=c.ndim)) for c in consts]
    out_shape = [jax.ShapeDtypeStruct((seq, tiled[j][1]), dtile_dtypes[j]) for j in keep]
    out_shape += [jax.ShapeDtypeStruct(c.shape, F32) for c in consts]
    return _pc(body, name=name, grid=(seq // tm,), in_specs=in_specs, out_specs=tuple(out_specs),
               out_shape=tuple(out_shape), compiler_params=_cparams(("arbitrary",)))(*args)


def _lnmod_fn(x, nw, sc, sh):
    return (_rms(x, nw) * (1.0 + sc) + sh,)


def _resid_fn(x, o, g):
    return (x + (1.0 + g) * o,)


def _swiglu_fn(gu):
    return (_silu(gu[:, :FFN_H]) * gu[:, FFN_H:],)


def _merge_fn(yh, ys, yg, logits, wb, b_merge):
    gates = jax.nn.sigmoid(logits + b_merge)
    acc = None
    for n, y in enumerate((yh, ys, yg)):
        t = gates[:, n * D_MODEL:(n + 1) * D_MODEL] * _bdot(y, wb[n])
        acc = t if acc is None else acc + t
    return (acc,)


MM_VMEM_BUDGET = 40 * 1024 * 1024
MM_TILE_CAP = 1024
MM_K_CAP = 4096


def _divisor(n, cap, unit=LANES):
    best = None
    for d in range(unit, min(n, cap) + 1, unit):
        if n % d == 0:
            best = d
    return n if best is None else best


def _mm_tiles(m, n, k, out_bytes):
    tk = k if k <= MM_K_CAP else _divisor(k, 3072)
    tm, tn = _divisor(m, MM_TILE_CAP), _divisor(n, MM_TILE_CAP + MM_TILE_CAP // 2)

    def need(tm_, tn_):
        acc = tm_ * tn_ * 4 if tk < k else 0
        return 2 * 2 * tk * (tm_ + tn_) + acc + 2 * tm_ * tn_ * out_bytes

    while need(tm, tn) > MM_VMEM_BUDGET:
        if tn >= tm and _divisor(n, tn - LANES) < tn:
            tn = _divisor(n, tn - LANES)
        elif _divisor(m, tm - LANES) < tm:
            tm = _divisor(m, tm - LANES)
        else:
            break
    return tm, tn, tk


def _mm(a, b, mode, out_dtype, name):
    if mode == "nn":
        (m, k), n = a.shape, b.shape[1]
    elif mode == "nt":
        (m, k), n = a.shape, b.shape[0]
    else:
        (k, m), n = a.shape, b.shape[1]
    tm, tn, tk = _mm_tiles(m, n, k, jnp.dtype(out_dtype).itemsize)
    nk = k // tk
    dims = {"nn": ((1,), (0,)), "nt": ((1,), (1,)), "tn": ((0,), (0,))}[mode]

    def body_one(a_ref, b_ref, o_ref):
        o_ref[...] = lax.dot_general(a_ref[...], b_ref[...], (dims, ((), ())), preferred_element_type=F32).astype(o_ref.dtype)

    def body_acc(a_ref, b_ref, o_ref, acc_ref):
        kk = pl.program_id(2)

        @pl.when(kk == 0)
        def _():
            acc_ref[...] = jnp.zeros_like(acc_ref)

        acc_ref[...] += lax.dot_general(a_ref[...], b_ref[...], (dims, ((), ())), preferred_element_type=F32)

        @pl.when(kk == nk - 1)
        def _():
            o_ref[...] = acc_ref[...].astype(o_ref.dtype)

    a_spec = pl.BlockSpec((tk, tm), lambda i, j, kk: (kk, i)) if mode == "tn" else pl.BlockSpec((tm, tk), lambda i, j, kk: (i, kk))
    b_spec = pl.BlockSpec((tn, tk), lambda i, j, kk: (j, kk)) if mode == "nt" else pl.BlockSpec((tk, tn), lambda i, j, kk: (kk, j))
    return _pc(body_one if nk == 1 else body_acc, name=name, grid=(m // tm, n // tn, nk), in_specs=[a_spec, b_spec],
               out_specs=pl.BlockSpec((tm, tn), lambda i, j, kk: (i, j)),
               out_shape=jax.ShapeDtypeStruct((m, n), out_dtype),
               scratch_shapes=[] if nk == 1 else [pltpu.VMEM((tm, tn), F32)],
               compiler_params=_cparams(("parallel", "parallel", "arbitrary")))(a.astype(BF), b.astype(BF))


def _final_loss(x, tgt, norm_final, tm=TOKEN_TILE):
    seq = x.shape[0]

    def fn(xv, nf, tv):
        err = jnp.square(_rms(xv, nf) - tv)
        return 0.5 * jnp.sum(jnp.mean(err, axis=-1))

    def body(x_ref, t_ref, nf_ref, loss_ref, dx_ref, dnf_ref):
        i = pl.program_id(0)

        @pl.when(i == 0)
        def _():
            loss_ref[...] = jnp.zeros_like(loss_ref)
            dnf_ref[...] = jnp.zeros_like(dnf_ref)

        val, vjp = jax.vjp(functools.partial(fn, tv=t_ref[...]), x_ref[...], nf_ref[...])
        dx, dnf = vjp(jnp.ones((), F32))
        dx_ref[...] = dx
        dnf_ref[...] += dnf
        loss_ref[...] += jnp.broadcast_to(val, loss_ref.shape)

    return _pc(body, name="final_loss", grid=(seq // tm,),
               in_specs=[pl.BlockSpec((tm, D_MODEL), lambda i: (i, 0)), pl.BlockSpec((tm, D_MODEL), lambda i: (i, 0)),
                         pl.BlockSpec((1, D_MODEL), lambda i: (0, 0))],
               out_specs=(pl.BlockSpec((8, LANES), lambda i: (0, 0)), pl.BlockSpec((tm, D_MODEL), lambda i: (i, 0)),
                          pl.BlockSpec((1, D_MODEL), lambda i: (0, 0))),
               out_shape=(jax.ShapeDtypeStruct((8, LANES), F32), jax.ShapeDtypeStruct((seq, D_MODEL), F32),
                          jax.ShapeDtypeStruct((1, D_MODEL), F32)),
               compiler_params=_cparams(("arbitrary",)))(x, tgt, norm_final)


def _ada_fwd(c_all, w_ada, b_ada_cols):
    n_l, _, cols = w_ada.shape

    def body(c_ref, w_ref, b_ref, o_ref):
        o_ref[0] = jnp.dot(_silu(c_ref[...]), w_ref[0], preferred_element_type=F32, precision=HI) + b_ref[0]

    return _pc(body, name="ada_fwd", grid=(n_l,),
               in_specs=[pl.BlockSpec((8, D_MODEL), lambda l: (0, 0)), pl.BlockSpec((1, D_MODEL, cols), lambda l: (l, 0, 0)),
                         pl.BlockSpec((1, 1, cols), lambda l: (l, 0, 0))],
               out_specs=pl.BlockSpec((1, 8, cols), lambda l: (l, 0, 0)),
               out_shape=jax.ShapeDtypeStruct((n_l, 8, cols), F32),
               compiler_params=_cparams(("arbitrary",)))(c_all, w_ada, b_ada_cols.reshape(n_l, 1, cols))


def _ada_bwd(c_all, dmod_cols):
    n_l, _, cols = dmod_cols.shape

    def body(c_ref, g_ref, o_ref):
        o_ref[0] = lax.dot_general(_silu(c_ref[...]), g_ref[0], (((0,), (0,)), ((), ())), preferred_element_type=F32,
                                   precision=HI)

    return _pc(body, name="ada_bwd", grid=(n_l,),
               in_specs=[pl.BlockSpec((8, D_MODEL), lambda l: (0, 0)), pl.BlockSpec((1, 8, cols), lambda l: (l, 0, 0))],
               out_specs=pl.BlockSpec((1, D_MODEL, cols), lambda l: (l, 0, 0)),
               out_shape=jax.ShapeDtypeStruct((n_l, D_MODEL, cols), F32),
               compiler_params=_cparams(("arbitrary",)))(c_all, dmod_cols)


def _lb_fn(logits):
    e = jnp.exp(logits - jnp.max(logits, axis=0, keepdims=True))
    p = e / jnp.sum(e, axis=0, keepdims=True)
    r = lax.broadcasted_iota(jnp.int32, (DEPTH, 1), 0)
    lb = jnp.zeros_like(p)
    for j in range(1, DEPTH):
        lb = lb + jnp.where(r >= j, p[j:j + 1, :], 0.0)
    return lb


def _lb_fwd(logits):
    def body(l_ref, o_ref):
        o_ref[...] = _lb_fn(l_ref[...])

    return _pc(body, name="lb_fwd", out_shape=jax.ShapeDtypeStruct(logits.shape, F32))(logits)


def _lb_bwd(logits, dlb):
    def body(l_ref, g_ref, o_ref):
        _, vjp = jax.vjp(_lb_fn, l_ref[...])
        o_ref[...] = vjp(g_ref[...])[0]

    return _pc(body, name="lb_bwd", out_shape=jax.ShapeDtypeStruct(logits.shape, F32))(logits, dlb)


def _rows_for(n_rows, n_cols):
    r = 8
    while r * 2 <= n_rows and n_rows % (r * 2) == 0 and r * 2 * n_cols <= 256 * 1024:
        r *= 2
    return r if n_rows % r == 0 else n_rows


def _ew(name, fn, ins, out_dtypes):
    n_rows, n_cols = ins[0].shape
    tr = _rows_for(n_rows, n_cols)
    n_in = len(ins)

    def body(*refs):
        res = fn(*[r[...] for r in refs[:n_in]])
        for r, y in zip(refs[n_in:], res):
            r[...] = y.astype(r.dtype)

    spec = pl.BlockSpec((tr, n_cols), lambda i: (i, 0))
    return _pc(body, name=name, grid=(n_rows // tr,), in_specs=[spec] * n_in, out_specs=tuple([spec] * len(out_dtypes)),
               out_shape=tuple(jax.ShapeDtypeStruct((n_rows, n_cols), d) for d in out_dtypes),
               compiler_params=_cparams(("arbitrary",)))(*ins)


def _adamw_fn(w, g, m, v):
    m = ADAM_B1 * m + (1.0 - ADAM_B1) * g
    v = ADAM_B2 * v + (1.0 - ADAM_B2) * jnp.square(g)
    m_hat = m / (1.0 - ADAM_B1 ** ADAM_STEP)
    v_hat = v / (1.0 - ADAM_B2 ** ADAM_STEP)
    return -ADAM_LR * (m_hat / (jnp.sqrt(v_hat) + ADAM_EPS) + ADAM_WD * w), m, v


def _adamw(name, w, g, m, v):
    shape = w.shape
    two = (-1, shape[-1])
    d, nm, nv = _ew(name, _adamw_fn, [a.reshape(two) for a in (w, g, m, v)], [F32, F32, F32])
    return d.reshape(shape), nm.reshape(shape), nv.reshape(shape)


def _sum_leading(name, a, out_dtype):
    n, n_rows, n_cols = a.shape
    tr = _rows_for(n_rows, n_cols)

    def body(a_ref, o_ref):
        acc = a_ref[0].astype(F32)
        for j in range(1, n):
            acc = acc + a_ref[j].astype(F32)
        o_ref[...] = acc.astype(o_ref.dtype)

    return _pc(body, name=name, grid=(n_rows // tr,), in_specs=[pl.BlockSpec((n, tr, n_cols), lambda i: (0, i, 0))],
               out_specs=pl.BlockSpec((tr, n_cols), lambda i: (i, 0)),
               out_shape=jax.ShapeDtypeStruct((n_rows, n_cols), out_dtype),
               compiler_params=_cparams(("arbitrary",)))(a)


MESH = pl.DeviceIdType.MESH
ANY = pl.BlockSpec(memory_space=pl.ANY)


def _place():
    return lax.axis_index("x"), lax.axis_index("y"), lax.axis_index("c")


def _all_gather_small(name, a):
    m_per, n = a.shape

    def body(x_ref, out_ref, send_sems, recv_sems, local_sem):
        x, y, c = _place()
        me, sibling = (x, y, c), (x, y, 1 - c)
        chips = [(1 - x, y), (x, 1 - y), (1 - x, 1 - y)]

        def rows(px, py, pc):
            return out_ref.at[pl.ds((4 * px + 2 * py + pc) * m_per, m_per), :]

        def copy(k, block, to, src=None):
            return pltpu.make_async_remote_copy(src_ref=rows(*block) if src is None else src, dst_ref=rows(*block),
                                                send_sem=send_sems.at[k], recv_sem=recv_sems.at[k], device_id=to,
                                                device_id_type=MESH)

        mine = pltpu.make_async_copy(x_ref, rows(*me), local_sem)
        mine.start()
        first = [copy(0, me, sibling, src=x_ref)]
        first += [copy(1 + j, me, (*chip, c), src=x_ref) for j, chip in enumerate(chips)]
        for cp in first:
            cp.start()
        passed = [copy(4 + j, (*chip, c), sibling) for j, chip in enumerate(chips)]
        for j, chip in enumerate(chips):
            copy(1 + j, (*chip, c), me).wait_recv()
            passed[j].start()
        copy(0, sibling, me).wait_recv()
        for j, chip in enumerate(chips):
            copy(4 + j, (*chip, 1 - c), me).wait_recv()
        for cp in first + passed:
            cp.wait_send()
        mine.wait()

    out = _pc(body, name=name, out_shape=jax.ShapeDtypeStruct((8 * m_per, n), a.dtype),
              in_specs=[pl.BlockSpec(memory_space=pltpu.VMEM)], out_specs=pl.BlockSpec(memory_space=pltpu.VMEM),
              scratch_shapes=[pltpu.SemaphoreType.DMA((7,)), pltpu.SemaphoreType.DMA((7,)), pltpu.SemaphoreType.DMA],
              compiler_params=pltpu.CompilerParams(vmem_limit_bytes=VMEM_LIMIT))(a)
    return out.reshape(8, m_per, n)


def _chip_gather(name, pack):
    n_l, n_r, n_c = pack.shape
    half = n_r // 2

    def body(p_ref, o_ref, send_sems, recv_sems):
        x, y, c = _place()
        sibling = (x, y, 1 - c)
        chips = [(1 - x, y), (x, 1 - y), (1 - x, 1 - y)]

        def slab(px, py, pc):
            return o_ref.at[2 * px + py, :, pl.ds(pc * half, half), :]

        def copy(k, src, dst, to):
            return pltpu.make_async_remote_copy(src_ref=src, dst_ref=dst, send_sem=send_sems.at[k], recv_sem=recv_sems.at[k],
                                                device_id=to, device_id_type=MESH)

        first = [copy(j, p_ref.at[:, pl.ds(c * half, half), :], slab(x, y, c), (*chip, c)) for j, chip in enumerate(chips)]
        for cp in first:
            cp.start()
        passed = [copy(3 + j, slab(*chip, c), slab(*chip, c), sibling) for j, chip in enumerate(chips)]
        for j, chip in enumerate(chips):
            copy(j, slab(*chip, c), slab(*chip, c), (*chip, c)).wait_recv()
            passed[j].start()
        for j, chip in enumerate(chips):
            copy(3 + j, slab(*chip, 1 - c), slab(*chip, 1 - c), sibling).wait_recv()
        for cp in first + passed:
            cp.wait_send()

    return _pc(body, name=name, out_shape=jax.ShapeDtypeStruct((4, n_l, n_r, n_c), pack.dtype), in_specs=[ANY], out_specs=ANY,
               scratch_shapes=[pltpu.SemaphoreType.DMA((6,)), pltpu.SemaphoreType.DMA((6,))])(pack)


def _pair_swap(name, give):
    def body(g_ref, o_ref, send_sem, recv_sem):
        x, y, c = _place()
        cp = pltpu.make_async_remote_copy(src_ref=g_ref, dst_ref=o_ref, send_sem=send_sem, recv_sem=recv_sem,
                                          device_id=(x, y, 1 - c), device_id_type=MESH)
        cp.start()
        cp.wait()

    return _pc(body, name=name, out_shape=jax.ShapeDtypeStruct(give.shape, give.dtype), in_specs=[ANY], out_specs=ANY,
               scratch_shapes=[pltpu.SemaphoreType.DMA, pltpu.SemaphoreType.DMA])(give)


def _chip_exchange(name, parts):
    def body(p_ref, o_ref, send_sems, recv_sems):
        x, y, c = _place()
        me = 2 * x + y
        chips = [(1 - x, y), (x, 1 - y), (1 - x, 1 - y)]

        def copy(k, src, dst, to):
            return pltpu.make_async_remote_copy(src_ref=src, dst_ref=dst, send_sem=send_sems.at[k], recv_sem=recv_sems.at[k],
                                                device_id=to, device_id_type=MESH)

        sends = [copy(j, p_ref.at[2 * px + py], o_ref.at[me], (px, py, c)) for j, (px, py) in enumerate(chips)]
        for cp in sends:
            cp.start()
        for j, (px, py) in enumerate(chips):
            copy(j, p_ref.at[2 * px + py], o_ref.at[2 * px + py], (px, py, c)).wait_recv()
        for cp in sends:
            cp.wait_send()

    return _pc(body, name=name, out_shape=jax.ShapeDtypeStruct(parts.shape, parts.dtype), in_specs=[ANY], out_specs=ANY,
               scratch_shapes=[pltpu.SemaphoreType.DMA((3,)), pltpu.SemaphoreType.DMA((3,))])(parts)


N_CHIP = 4
BIG = (("w_in", (1024, 2822), 1, (1024, 2822)), ("w_branch", (3, 768, 256), 2, (2304, 256)),
       ("w_out", (256, 1024), 0, (256, 1024)), ("w_ffn_in", (1024, 1408), 1, (1024, 1408)),
       ("w_ffn_out", (704, 1024), 0, (704, 1024)))
G768 = ((0, 3072), (3072, 3840), (7436, 8204))
GXBC, GQKV, GGATE = (3840, 5120), (5132, 7436), (8216, 11288)
GSMALL = ((5120, 5132), (8204, 8210), (8210, 8216))
W768, WXBC, WGATE = 4608, CONV_CH, 3 * D_MODEL
IN_PAD = W768 + WXBC + QKV_W + WGATE + SMALL_W


def _join_shards(slabs, axis, shard_shape):
    n_l = slabs.shape[1]
    parts = [slabs[j].reshape((n_l,) + shard_shape) for j in range(N_CHIP)]
    return jnp.concatenate(parts, axis=axis + 1)


def _split_shards(full, axis, rows_cols):
    n_l = full.shape[0]
    size = full.shape[axis + 1] // N_CHIP
    return jnp.stack([lax.slice_in_dim(full, j * size, (j + 1) * size, axis=axis + 1).reshape((n_l,) + rows_cols)
                      for j in range(N_CHIP)])


def _gather_weights(w, chip, big=BIG):
    out = {}
    for n, shape, ax, rc in big:
        n_l = w[n].shape[0]
        mine = w[n].astype(BF).reshape((n_l,) + rc)
        slabs = lax.dynamic_update_slice(_chip_gather("gather_" + n, mine), mine[None], (chip, 0, 0, 0))
        out[n] = _join_shards(slabs, ax, shape)
    return out


def _reduce_grads(full_grads, chip, core, big=BIG):
    out = {}
    for n, shape, ax, (rows, cols) in big:
        n_l = full_grads[n].shape[0]
        slabs = _split_shards(full_grads[n], ax, (rows, cols))
        half = rows // 2
        keep = lax.dynamic_slice_in_dim(slabs, core * half, half, axis=2).reshape(-1, cols)
        give = lax.dynamic_slice_in_dim(slabs, (1 - core) * half, half, axis=2).reshape(-1, cols)
        got = _pair_swap("pair_swap_" + n, give)
        (pair_sum,) = _ew("pair_sum_" + n, lambda a, b: (a.astype(F32) + b.astype(F32),), [keep, got], [BF])
        pair_sum = pair_sum.reshape(N_CHIP, n_l * half, cols)
        parts = lax.dynamic_update_slice(_chip_exchange("chip_exchange_" + n, pair_sum),
                                         lax.dynamic_slice_in_dim(pair_sum, chip, 1, axis=0), (chip, 0, 0))
        mine = _sum_leading("chip_sum_" + n, parts, F32).reshape(n_l, half, cols)
        theirs = _pair_swap("pair_share_" + n, mine)
        full = jnp.concatenate([jnp.where(core == 0, mine, theirs), jnp.where(core == 0, theirs, mine)], axis=1)
        out[n] = full.reshape((n_l,) + shape)
    return out


def _regroup_w_in(w):
    cat = lambda spans: jnp.concatenate([w[:, a:b] for a, b in spans], axis=1)
    small = jnp.concatenate([cat(GSMALL), jnp.zeros((w.shape[0], SMALL_W - 24), w.dtype)], axis=1)
    return cat(G768), cat((GXBC,)), cat((GQKV,)), cat((GGATE,)), small


def _ungroup_w_in(d):
    o_xbc, o_qkv, o_gate, o_small = W768, W768 + WXBC, W768 + WXBC + QKV_W, W768 + WXBC + QKV_W + WGATE
    spans = ((0, 3072), (3072, 3840), (o_xbc, o_xbc + WXBC), (o_small, o_small + 12), (o_qkv, o_qkv + QKV_W),
             (3840, 4608), (o_small + 12, o_small + 18), (o_small + 18, o_small + 24), (o_gate, o_gate + WGATE))
    return jnp.concatenate([d[:, a:b] for a, b in spans], axis=1)


def _lane_pad(v, off):
    return jnp.pad(v, (off, LANES - off - v.shape[0]))[None, :]


STATE6 = (N_HEAD6, HEAD, HEAD)


def _mixer_inputs(sv, lp):
    p768, pxbc, pqkv, psmall = sv["p768"], sv["pxbc"], sv["pqkv"], sv["psmall"]
    hgrn = ([(p768, MIX_W, j) for j in range(4)], [], [lp["lb"], lp["hgrn_norm"]])
    ssd = ([(p768, MIX_W, 4), (pxbc, CONV_CH, 0), (psmall, LANES, 0)], [1],
           [lp["ssm_conv_w"], lp["ssm_conv_b"], lp["ssm_dt_bias"], lp["ssm_a_log"], lp["ssm_d"], lp["ssm_norm"]])
    gdn = ([(pqkv, QKV_W, 0), (p768, MIX_W, 5), (psmall, LANES, 0)], [0],
           [lp["gdn_conv_w"], lp["gdn_dt_bias"], lp["gdn_a_log"], lp["gdn_norm"]])
    return hgrn, ssd, gdn


def _layer_fwd(x, md, lw, lp):
    sv = {"x": x}
    (sv["h1"],) = _tile_fwd("lnmod1", _lnmod_fn, [(x, D_MODEL, 0)], [lp["norm_mix"], md["sc1"], md["sh1"]], [(D_MODEL, BF)])
    for nm in ("768", "xbc", "qkv", "gate", "small"):
        sv["p" + nm] = _mm(sv["h1"], lw["win_" + nm], "nn", F32, "proj_" + nm)
    hgrn, ssd, gdn = _mixer_inputs(sv, lp)
    sv["y_h"], sv["st_h"] = _scan_fwd("hgrn_fwd", _hgrn_chunk, *hgrn, MIX_W, STATE6)
    sv["y_s"], sv["st_s"] = _scan_fwd("ssd_fwd", _ssd_chunk, *ssd, MIX_W, STATE6)
    sv["y_g"], sv["st_g"] = _scan_fwd("gdn_fwd", _gdn_chunk, *gdn, MIX_W, STATE6)
    (sv["merged"],) = _tile_fwd("merge", _merge_fn, _merge_tiles(sv), [lw["w_branch"], lp["b_merge"]], [(D_MODEL, BF)])
    sv["out"] = _mm(sv["merged"], lw["w_out"], "nn", F32, "out_proj")
    (sv["x_mid"],) = _tile_fwd("resid1", _resid_fn, [(x, D_MODEL, 0), (sv["out"], D_MODEL, 0)], [md["g1"]], [(D_MODEL, F32)])
    (sv["h2"],) = _tile_fwd("lnmod2", _lnmod_fn, [(sv["x_mid"], D_MODEL, 0)], [lp["norm_ffn"], md["sc2"], md["sh2"]],
                            [(D_MODEL, BF)])
    sv["gu"] = _mm(sv["h2"], lw["w_ffn_in"], "nn", F32, "ffn_in")
    (sv["act"],) = _tile_fwd("swiglu", _swiglu_fn, [(sv["gu"], 2 * FFN_H, 0)], [], [(FFN_H, BF)])
    sv["o2"] = _mm(sv["act"], lw["w_ffn_out"], "nn", F32, "ffn_out")
    (x_out,) = _tile_fwd("resid2", _resid_fn, [(sv["x_mid"], D_MODEL, 0), (sv["o2"], D_MODEL, 0)], [md["g2"]], [(D_MODEL, F32)])
    return x_out, sv


def _merge_tiles(sv):
    return [(sv["y_h"], MIX_W, 0), (sv["y_s"], MIX_W, 0), (sv["y_g"], MIX_W, 0), (sv["pgate"], WGATE, 0)]


def _layer_bwd(dx_out, sv, md, lw, lp):
    g = {}
    x, x_mid = sv["x"], sv["x_mid"]
    d_xmid, d_o2, g["g2"] = _tile_bwd("resid2_b", _resid_fn, [(x_mid, D_MODEL, 0), (sv["o2"], D_MODEL, 0)], [md["g2"]], [dx_out],
                                      [F32, BF])
    d_act = _mm(d_o2, lw["w_ffn_out"], "nt", F32, "ffn_out_dx")
    g["w_ffn_out"] = _mm(sv["act"], d_o2, "tn", BF, "ffn_out_dw")
    (d_gu,) = _tile_bwd("swiglu_b", _swiglu_fn, [(sv["gu"], 2 * FFN_H, 0)], [], [d_act], [BF])
    d_h2 = _mm(d_gu, lw["w_ffn_in"], "nt", F32, "ffn_in_dx")
    g["w_ffn_in"] = _mm(sv["h2"], d_gu, "tn", BF, "ffn_in_dw")
    d_xmid, g["norm_ffn"], g["sc2"], g["sh2"] = _tile_bwd(
        "lnmod2_b", _lnmod_fn, [(x_mid, D_MODEL, 0)], [lp["norm_ffn"], md["sc2"], md["sh2"]], [d_h2], [F32], add_to=(0, d_xmid))
    d_x, d_out, g["g1"] = _tile_bwd("resid1_b", _resid_fn, [(x, D_MODEL, 0), (sv["out"], D_MODEL, 0)], [md["g1"]], [d_xmid],
                                    [F32, BF])
    d_merged = _mm(d_out, lw["w_out"], "nt", F32, "out_proj_dx")
    g["w_out"] = _mm(sv["merged"], d_out, "tn", BF, "out_proj_dw")
    d_yh, d_ys, d_yg, d_gate, g["w_branch"], g["b_merge"] = _tile_bwd(
        "merge_b", _merge_fn, _merge_tiles(sv), [lw["w_branch"], lp["b_merge"]], [d_merged], [F32, F32, F32, BF])
    hgrn, ssd, gdn = _mixer_inputs(sv, lp)
    d_q, d_f, d_v, d_g, g["lb"], g["hgrn_norm"] = _scan_bwd("hgrn_bwd", _hgrn_chunk, *hgrn, sv["st_h"], d_yh, [BF] * 4)
    (d_sz, d_xbc, d_small, g["ssm_conv_w"], g["ssm_conv_b"], g["ssm_dt_bias"], g["ssm_a_log"], g["ssm_d"],
     g["ssm_norm"]) = _scan_bwd("ssd_bwd", _ssd_chunk, *ssd, sv["st_s"], d_ys, [BF, BF, F32])
    d_qkv, d_gz, d_small, g["gdn_conv_w"], g["gdn_dt_bias"], g["gdn_a_log"], g["gdn_norm"] = _scan_bwd(
        "gdn_bwd", _gdn_chunk, *gdn, sv["st_g"], d_yg, [BF, BF, BF], extra=(2, d_small))
    d_proj = jnp.concatenate([d_q, d_f, d_v, d_g, d_sz, d_gz, d_xbc, d_qkv, d_gate, d_small,
                              jnp.zeros((x.shape[0], SMALL_W - LANES), BF)], axis=1)
    d_h1 = _mm(d_proj, lw["win_all"], "nt", F32, "proj_dx")
    g["w_in"] = _ungroup_w_in(_mm(sv["h1"], d_proj, "tn", BF, "proj_dw"))
    d_x, g["norm_mix"], g["sc1"], g["sh1"] = _tile_bwd(
        "lnmod1_b", _lnmod_fn, [(x, D_MODEL, 0)], [lp["norm_mix"], md["sc1"], md["sh1"]], [d_h1], [F32], add_to=(0, d_x))
    return d_x, g


SMALL_REPL = ("norm_mix", "norm_ffn", "b_merge", "hgrn_lb_logits", "hgrn_norm", "ssm_conv_w", "ssm_conv_b", "ssm_dt_bias",
              "ssm_a_log", "ssm_d", "ssm_norm", "gdn_conv_w", "gdn_dt_bias", "gdn_a_log", "gdn_norm", "norm_final")
WEIGHTS = ("w_ada", "b_ada", "norm_mix", "norm_ffn", "w_in", "b_merge", "hgrn_lb_logits", "hgrn_norm", "ssm_conv_w",
           "ssm_conv_b", "ssm_dt_bias", "ssm_a_log", "ssm_d", "ssm_norm", "gdn_conv_w", "gdn_dt_bias", "gdn_a_log",
           "gdn_norm", "w_branch", "w_out", "w_ffn_in", "w_ffn_out", "norm_final")
SMALL_ROWS = 120


def _pad_rows(flat, n_rows, n_cols):
    return jnp.concatenate([flat, jnp.zeros((n_rows * n_cols - flat.shape[0],), flat.dtype)]).reshape(n_rows, n_cols)


def _device_step(x, tgt, mod, lb, wfull, sp):
    mds, lps, svs = [], [], []
    h = x
    for l in range(DEPTH):
        md = {n: mod[l, i * D_MODEL:(i + 1) * D_MODEL][None, :] for i, n in enumerate(("sh1", "sc1", "g1", "sh2", "sc2", "g2"))}
        lp = {n: sp[n][l][None, :] for n in ("norm_mix", "norm_ffn", "b_merge", "hgrn_norm", "ssm_conv_b", "ssm_norm", "gdn_norm")}
        lp["lb"] = lb[l][None, :]
        lp["ssm_conv_w"], lp["gdn_conv_w"] = sp["ssm_conv_w"][l], sp["gdn_conv_w"][l]
        for n in ("ssm_dt_bias", "ssm_a_log", "ssm_d"):
            lp[n] = _lane_pad(sp[n][l], DT_OFF)
        for n in ("gdn_dt_bias", "gdn_a_log"):
            lp[n] = _lane_pad(sp[n][l], GA_OFF)
        h, sv = _layer_fwd(h, md, wfull[l], lp)
        mds.append(md), lps.append(lp), svs.append(sv)
    loss, dh, d_nf = _final_loss(h, tgt, sp["norm_final"][None, :])
    grads = [None] * DEPTH
    for l in reversed(range(DEPTH)):
        dh, grads[l] = _layer_bwd(dh, svs[l], mds[l], wfull[l], lps[l])
    return loss, dh, d_nf, grads


def kernel(x, c, w_ada, b_ada, norm_mix, norm_ffn, w_in, b_merge, hgrn_lb_logits, hgrn_norm, ssm_conv_w, ssm_conv_b, ssm_dt_bias, ssm_a_log, ssm_d, ssm_norm, gdn_conv_w, gdn_dt_bias, gdn_a_log, gdn_norm, w_branch, w_out, w_ffn_in, w_ffn_out, norm_final, loss_target, m_w_ada, m_b_ada, m_norm_mix, m_norm_ffn, m_w_in, m_b_merge, m_hgrn_lb_logits, m_hgrn_norm, m_ssm_conv_w, m_ssm_conv_b, m_ssm_dt_bias, m_ssm_a_log, m_ssm_d, m_ssm_norm, m_gdn_conv_w, m_gdn_dt_bias, m_gdn_a_log, m_gdn_norm, m_w_branch, m_w_out, m_w_ffn_in, m_w_ffn_out, m_norm_final, v_w_ada, v_b_ada, v_norm_mix, v_norm_ffn, v_w_in, v_b_merge, v_hgrn_lb_logits, v_hgrn_norm, v_ssm_conv_w, v_ssm_conv_b, v_ssm_dt_bias, v_ssm_a_log, v_ssm_d, v_ssm_norm, v_gdn_conv_w, v_gdn_dt_bias, v_gdn_a_log, v_gdn_norm, v_w_branch, v_w_out, v_w_ffn_in, v_w_ffn_out, v_norm_final):
    w = dict(w_ada=w_ada, b_ada=b_ada, norm_mix=norm_mix, norm_ffn=norm_ffn, w_in=w_in, b_merge=b_merge,
             hgrn_lb_logits=hgrn_lb_logits, hgrn_norm=hgrn_norm, ssm_conv_w=ssm_conv_w, ssm_conv_b=ssm_conv_b,
             ssm_dt_bias=ssm_dt_bias, ssm_a_log=ssm_a_log, ssm_d=ssm_d, ssm_norm=ssm_norm, gdn_conv_w=gdn_conv_w,
             gdn_dt_bias=gdn_dt_bias, gdn_a_log=gdn_a_log, gdn_norm=gdn_norm, w_branch=w_branch, w_out=w_out,
             w_ffn_in=w_ffn_in, w_ffn_out=w_ffn_out, norm_final=norm_final)
    m = dict(w_ada=m_w_ada, b_ada=m_b_ada, norm_mix=m_norm_mix, norm_ffn=m_norm_ffn, w_in=m_w_in, b_merge=m_b_merge,
             hgrn_lb_logits=m_hgrn_lb_logits, hgrn_norm=m_hgrn_norm, ssm_conv_w=m_ssm_conv_w, ssm_conv_b=m_ssm_conv_b,
             ssm_dt_bias=m_ssm_dt_bias, ssm_a_log=m_ssm_a_log, ssm_d=m_ssm_d, ssm_norm=m_ssm_norm, gdn_conv_w=m_gdn_conv_w,
             gdn_dt_bias=m_gdn_dt_bias, gdn_a_log=m_gdn_a_log, gdn_norm=m_gdn_norm, w_branch=m_w_branch, w_out=m_w_out,
             w_ffn_in=m_w_ffn_in, w_ffn_out=m_w_ffn_out, norm_final=m_norm_final)
    v = dict(w_ada=v_w_ada, b_ada=v_b_ada, norm_mix=v_norm_mix, norm_ffn=v_norm_ffn, w_in=v_w_in, b_merge=v_b_merge,
             hgrn_lb_logits=v_hgrn_lb_logits, hgrn_norm=v_hgrn_norm, ssm_conv_w=v_ssm_conv_w, ssm_conv_b=v_ssm_conv_b,
             ssm_dt_bias=v_ssm_dt_bias, ssm_a_log=v_ssm_a_log, ssm_d=v_ssm_d, ssm_norm=v_ssm_norm, gdn_conv_w=v_gdn_conv_w,
             gdn_dt_bias=v_gdn_dt_bias, gdn_a_log=v_gdn_a_log, gdn_norm=v_gdn_norm, w_branch=v_w_branch, w_out=v_w_out,
             w_ffn_in=v_w_ffn_in, w_ffn_out=v_w_ffn_out, norm_final=v_norm_final)
    xi, yi, ci = _place()
    chip, me = 2 * xi + yi, 4 * xi + 2 * yi + ci
    seq = x.shape[1]

    conv_flat = jnp.concatenate([ssm_conv_w.reshape(-1), gdn_conv_w.reshape(-1)])
    n_conv = conv_flat.shape[0]
    first = _all_gather_small("gather_c_conv", _pad_rows(jnp.concatenate([c[0], conv_flat]), 16, D_MODEL))
    c_all = first[:, 0, :]
    conv_all = first[0::2].reshape(N_CHIP, -1)[:, D_MODEL:D_MODEL + n_conv]
    n_ssm = ssm_conv_w.size
    sp = dict(w)
    sp["ssm_conv_w"] = jnp.concatenate([conv_all[j, :n_ssm].reshape(ssm_conv_w.shape) for j in range(N_CHIP)], axis=2)
    sp["gdn_conv_w"] = jnp.concatenate([conv_all[j, n_ssm:].reshape(gdn_conv_w.shape) for j in range(N_CHIP)], axis=2)

    ada_cols = w_ada.shape[2]
    mod_part = _ada_fwd(c_all, w_ada, lax.dynamic_slice_in_dim(b_ada, chip * ada_cols, ada_cols, axis=1))
    mod_all = _all_gather_small("gather_mod", mod_part.reshape(DEPTH * 8, ada_cols))[0::2].reshape(N_CHIP, DEPTH, 8, ada_cols)
    mod = lax.dynamic_index_in_dim(mod_all, me, axis=2, keepdims=False).transpose(1, 0, 2).reshape(DEPTH, N_CHIP * ada_cols)
    lb = _lb_fwd(hgrn_lb_logits)

    gathered = _gather_weights(w, chip)
    wfull = []
    for l in range(DEPTH):
        full = {n: gathered[n][l] for n, _, _, _ in BIG}
        for nm, part in zip(("768", "xbc", "qkv", "gate", "small"), _regroup_w_in(full["w_in"])):
            full["win_" + nm] = part
        full["win_all"] = jnp.concatenate([full["win_" + nm] for nm in ("768", "xbc", "qkv", "gate", "small")], axis=1)
        wfull.append(full)

    loss8, d_x, d_nf, lg = _device_step(x[0], loss_target[0], mod, lb, wfull, sp)

    grad = _reduce_grads({n: jnp.stack([lg[l][n].astype(BF) for l in range(DEPTH)]) for n, _, _, _ in BIG}, chip, ci)

    dmod = jnp.stack([jnp.concatenate([lg[l][n] for n in ("sh1", "sc1", "g1", "sh2", "sc2", "g2")], axis=1)[0] for l in range(DEPTH)])
    d_lb = jnp.stack([lg[l]["lb"][0] for l in range(DEPTH)])
    contrib = {
        "norm_mix": jnp.stack([lg[l]["norm_mix"][0] for l in range(DEPTH)]),
        "norm_ffn": jnp.stack([lg[l]["norm_ffn"][0] for l in range(DEPTH)]),
        "b_merge": jnp.stack([lg[l]["b_merge"][0] for l in range(DEPTH)]),
        "hgrn_lb_logits": _lb_bwd(hgrn_lb_logits, d_lb),
        "hgrn_norm": jnp.stack([lg[l]["hgrn_norm"][0] for l in range(DEPTH)]),
        "ssm_conv_w": jnp.stack([lg[l]["ssm_conv_w"] for l in range(DEPTH)]),
        "ssm_conv_b": jnp.stack([lg[l]["ssm_conv_b"][0] for l in range(DEPTH)]),
        "ssm_dt_bias": jnp.stack([lg[l]["ssm_dt_bias"][0, DT_OFF:DT_OFF + 12] for l in range(DEPTH)]),
        "ssm_a_log": jnp.stack([lg[l]["ssm_a_log"][0, DT_OFF:DT_OFF + 12] for l in range(DEPTH)]),
        "ssm_d": jnp.stack([lg[l]["ssm_d"][0, DT_OFF:DT_OFF + 12] for l in range(DEPTH)]),
        "ssm_norm": jnp.stack([lg[l]["ssm_norm"][0] for l in range(DEPTH)]),
        "gdn_conv_w": jnp.stack([lg[l]["gdn_conv_w"] for l in range(DEPTH)]),
        "gdn_dt_bias": jnp.stack([lg[l]["gdn_dt_bias"][0, GA_OFF:GA_OFF + 6] for l in range(DEPTH)]),
        "gdn_a_log": jnp.stack([lg[l]["gdn_a_log"][0, GA_OFF:GA_OFF + 6] for l in range(DEPTH)]),
        "gdn_norm": jnp.stack([lg[l]["gdn_norm"][0] for l in range(DEPTH)]),
        "norm_final": d_nf[0],
    }
    flat = jnp.concatenate([dmod.reshape(-1)] + [contrib[n].reshape(-1) for n in SMALL_REPL] + [loss8[0, 0:1]])
    small_all = _all_gather_small("gather_small_grads", _pad_rows(flat, SMALL_ROWS, D_MODEL))
    total = _sum_leading("small_grad_sum", small_all, F32).reshape(-1)
    n_mod = dmod.size
    grad["b_ada"] = total[:n_mod].reshape(b_ada.shape)
    off = n_mod
    full_small = {}
    for n in SMALL_REPL:
        full_small[n] = total[off:off + contrib[n].size].reshape(contrib[n].shape)
        off += contrib[n].size
    loss = total[off]
    for n in SMALL_REPL:
        if n in ("ssm_conv_w", "gdn_conv_w"):
            cols = w[n].shape[2]
            grad[n] = lax.dynamic_slice_in_dim(full_small[n], chip * cols, cols, axis=2)
        else:
            grad[n] = full_small[n]
    dmod_cols = lax.dynamic_slice_in_dim(small_all[:, :n_mod // D_MODEL, :].reshape(8, DEPTH, -1), chip * ada_cols, ada_cols, axis=2)
    grad["w_ada"] = _ada_bwd(c_all, dmod_cols.transpose(1, 0, 2))

    delta, new_m, new_v = {}, {}, {}
    big_names = ("w_ada",) + tuple(n for n, _, _, _ in BIG)
    for n in big_names:
        delta[n], new_m[n], new_v[n] = _adamw("adamw_" + n, w[n], grad[n], m[n], v[n])
    small_names = [n for n in WEIGHTS if n not in big_names]
    packs = [_pad_rows(jnp.concatenate([d[n].reshape(-1) for n in small_names]), 584, LANES) for d in (w, grad, m, v)]
    outs = _ew("adamw_small", _adamw_fn, packs, [F32, F32, F32])
    off = 0
    for n in small_names:
        for dst, o in zip((delta, new_m, new_v), outs):
            dst[n] = o.reshape(-1)[off:off + w[n].size].reshape(w[n].shape)
        off += w[n].size
    return (loss, d_x[None], *[grad[n] for n in WEIGHTS], *[delta[n] for n in WEIGHTS], *[new_m[n] for n in WEIGHTS],
            *[new_v[n] for n in WEIGHTS])
```

```python
import functools

import jax
import jax.numpy as jnp
from jax import lax
from jax.experimental import pallas as pl
from jax.experimental.pallas import tpu as pltpu

F32 = jnp.float32
BF = jnp.bfloat16
HI = lax.Precision.HIGHEST

D_MODEL = 1024
DEPTH = 4
CHUNK = 64
MIX_W = 768
HEAD = 128
N_HEAD6 = 6
SSM_P = 64
SSM_N = 128
CONV_CH = 1280
QKV_W = 2304
FFN_H = 2816
IN_WIDTH = 11288
NORM_EPS = 1e-6
F_MIN = 1e-30
HALO = 8
HEAD_GROUP = 6
HGRN_SUB = 8
SMALL_W = 512
LANES = 128
DT_OFF, GB_OFF, GA_OFF = 0, 12, 18

ADAM_LR, ADAM_B1, ADAM_B2, ADAM_EPS, ADAM_WD, ADAM_STEP = 0.001, 0.9, 0.999, 1e-08, 0.01, 10

VMEM_LIMIT = 56 * 1024 * 1024
TOKEN_TILE = 256


def _pc(body, **kw):
    return pl.pallas_call(body, **kw)


def _cparams(sem):
    return pltpu.CompilerParams(dimension_semantics=sem, vmem_limit_bytes=VMEM_LIMIT)


def _bdot(a, b):
    return jnp.dot(a.astype(BF), b.astype(BF), preferred_element_type=F32)


def _bdot_nt(a, b):
    return lax.dot_general(a.astype(BF), b.astype(BF), (((1,), (1,)), ((), ())), preferred_element_type=F32)


def _bdot_tn(a, b):
    return lax.dot_general(a.astype(BF), b.astype(BF), (((0,), (0,)), ((), ())), preferred_element_type=F32)


def _silu(x):
    return x * jax.nn.sigmoid(x)


def _tri_mask(n, strict=False):
    t = lax.broadcasted_iota(jnp.int32, (n, n), 0)
    s = lax.broadcasted_iota(jnp.int32, (n, n), 1)
    return (s < t) if strict else (s <= t)


def _masked_exp(diff, mask):
    return jnp.where(mask, jnp.exp(jnp.where(mask, diff, 0.0)), 0.0)


def _split_bf16(x, n):
    parts, rest = [], x
    for _ in range(n):
        p = rest.astype(BF)
        parts.append(p)
        rest = rest - p.astype(F32)
    return parts


def _tri_sum(x, reverse):
    n, w = x.shape
    t = lax.broadcasted_iota(jnp.int32, (n, n), 0)
    s = lax.broadcasted_iota(jnp.int32, (n, n), 1)
    tri = jnp.where((s >= t) if reverse else (s <= t), 1.0, 0.0).astype(BF)
    y = jnp.dot(tri, jnp.concatenate(_split_bf16(x, 3), axis=1), preferred_element_type=F32)
    return y[:, :w] + y[:, w:2 * w] + y[:, 2 * w:]


@jax.custom_vjp
def _cumsum_rows(x):
    return _tri_sum(x, False)


_cumsum_rows.defvjp(lambda x: (_tri_sum(x, False), None), lambda _, g: (_tri_sum(g, True),))


def _dot_split(a, b, transpose_a=False):
    dims = (((0,), (0,)) if transpose_a else ((1,), (0,)), ((), ()))
    a_hi, a_lo = _split_bf16(a, 2)
    b_hi, b_lo = _split_bf16(b, 2)
    w = b.shape[1]
    y = lax.dot_general(a_hi, jnp.concatenate([b_hi, b_lo], axis=1), dims, preferred_element_type=F32)
    return y[:, :w] + y[:, w:] + lax.dot_general(a_lo, b_hi, dims, preferred_element_type=F32)


def _rms(x, w):
    return x * lax.rsqrt(jnp.mean(x * x, axis=-1, keepdims=True) + NORM_EPS) * w


def _causal_conv(halo, x, w):
    ext = jnp.concatenate([halo, x], axis=0)
    n = x.shape[0]
    acc = w[0:1, :] * ext[HALO - 3:HALO - 3 + n, :]
    for i in range(1, 4):
        acc = acc + w[i:i + 1, :] * ext[HALO - 3 + i:HALO - 3 + i + n, :]
    return acc


def _unit_lower_inverses(mats):
    n = mats[0].shape[0]
    t = lax.broadcasted_iota(jnp.int32, (n, n), 0)
    s_ = lax.broadcasted_iota(jnp.int32, (n, n), 1)
    xs = [jnp.where(t == s_, 1.0, 0.0).astype(F32) for _ in mats]
    for s in range(n - 1):
        r0 = 8 * ((s + 1) // 8)
        for i, a in enumerate(mats):
            x = xs[i]
            low = x[r0:] - a[r0:, s:s + 1] * x[s:s + 1, :]
            xs[i] = low if r0 == 0 else jnp.concatenate([x[:r0], low], axis=0)
    return xs


@jax.custom_vjp
def _unit_lower_solves(mats, rhss):
    return [_dot_split(inv, r) for inv, r in zip(_unit_lower_inverses(mats), rhss)]


def _uls_fwd(mats, rhss):
    invs = _unit_lower_inverses(mats)
    xs = [_dot_split(inv, r) for inv, r in zip(invs, rhss)]
    return xs, (invs, xs)


def _uls_bwd(res, gs):
    invs, xs = res
    ys = [_dot_split(inv, g, transpose_a=True) for inv, g in zip(invs, gs)]
    das = [jnp.where(_tri_mask(CHUNK, strict=True), -_bdot_nt(y, x), 0.0) for y, x in zip(ys, xs)]
    return das, ys


_unit_lower_solves.defvjp(_uls_fwd, _uls_bwd)


def _hgrn_chunk(tiles, halos, state, consts):
    q_raw, f_raw, v_all, g_raw = tiles
    lb, norm_w = consts
    q_all = _silu(q_raw)
    f = lb + (1.0 - lb) * jax.nn.sigmoid(f_raw)
    logf = jnp.log(jnp.maximum(f, F_MIN))
    k_all = (1.0 - lb) * jax.nn.sigmoid(-f_raw)
    b_all = _cumsum_rows(logf)
    sub = HGRN_SUB
    row = lax.broadcasted_iota(jnp.int32, (sub, 1), 0)
    src_row = lax.broadcasted_iota(jnp.int32, (CHUNK, 1), 0)
    src_lane = lax.broadcasted_iota(jnp.int32, (1, CHUNK), 1)
    heads = range(N_HEAD6)
    n_sub = CHUNK // sub
    cols = [slice(h * HEAD, (h + 1) * HEAD) for h in heads]
    qs, ks, vs, bs = ([a[:, sl] for sl in cols] for a in (q_all, k_all, v_all, b_all))
    o_inter = [_bdot_nt(qs[h] * jnp.exp(bs[h]), state[h]) for h in heads]
    blocks = [[None] * n_sub for _ in heads]
    for i in range(n_sub):
        r0 = i * sub
        for h in heads:
            if i > 0:
                ref = bs[h][r0 - 1:r0, :]
                blocks[h][i] = _bdot_nt(qs[h][r0:r0 + sub] * jnp.exp(bs[h][r0:r0 + sub] - ref),
                                        ks[h] * _masked_exp(ref - bs[h], src_row < r0))
            else:
                blocks[h][i] = jnp.zeros((sub, CHUNK), F32)
    for h in heads:
        for i in range(n_sub):
            r0 = i * sub
            qi, ki, bi = qs[h][r0:r0 + sub], ks[h][r0:r0 + sub], bs[h][r0:r0 + sub]
            for s in range(sub):
                e = _masked_exp(bi - bi[s:s + 1, :], row >= s)
                col = jnp.sum(qi * ki[s:s + 1, :] * e, axis=1, keepdims=True)
                blocks[h][i] = jnp.where(src_lane == r0 + s, col, blocks[h][i])
    os_ = [_bdot(jnp.concatenate(blocks[h], axis=0), vs[h]) + o_inter[h] for h in heads]
    ends = [bs[h][CHUNK - 1:CHUNK, :] for h in heads]
    new_states = [state[h] * jnp.exp(ends[h]) + _bdot_tn(vs[h], ks[h] * jnp.exp(ends[h] - bs[h])) for h in heads]
    outs = [_rms(os_[h], norm_w) * _silu(g_raw[:, cols[h]]) for h in heads]
    return (jnp.concatenate(outs, axis=1),), jnp.stack(new_states)


def _ssd_chunk(tiles, halos, state, consts):
    z, xbc_raw, small = tiles
    (halo,) = halos
    conv_w, conv_b, dt_bias, a_log, d_skip, norm_w = consts
    xbc = _silu(_causal_conv(halo, xbc_raw, conv_w) + conv_b)
    xs, bm, cm = xbc[:, :MIX_W], xbc[:, MIX_W:MIX_W + 2 * SSM_N], xbc[:, MIX_W + 2 * SSM_N:]
    dt = jax.nn.softplus(small + dt_bias)
    cum = _cumsum_rows(-jnp.exp(a_log) * dt)
    cum_t2 = jnp.concatenate([cum, cum], axis=0).T
    lane = lax.broadcasted_iota(jnp.int32, (1, LANES), 1)
    first = lane < SSM_P
    hm0 = jnp.where(first, 1.0, 0.0).astype(F32)
    hm1 = 1.0 - hm0
    src = jnp.where(first, lane, lane - SSM_P)
    tri2 = src <= lax.broadcasted_iota(jnp.int32, (CHUNK, 1), 0)
    pick = lambda a, b: jnp.where(first, a, b)
    bgs = [bm[:, g * SSM_N:(g + 1) * SSM_N] for g in range(2)]
    cgs = [cm[:, g * SSM_N:(g + 1) * SSM_N] for g in range(2)]
    gmats = [_bdot_nt(cgs[g], jnp.concatenate([bgs[g], bgs[g]], axis=0)) for g in range(2)]
    pairs = range(6)
    xps = [xs[:, p * LANES:(p + 1) * LANES] for p in pairs]
    c0s = [cum[:, 2 * p:2 * p + 1] for p in pairs]
    c1s = [cum[:, 2 * p + 1:2 * p + 2] for p in pairs]
    e0s = [cum[CHUNK - 1:CHUNK, 2 * p:2 * p + 1] for p in pairs]
    e1s = [cum[CHUNK - 1:CHUNK, 2 * p + 1:2 * p + 2] for p in pairs]
    segs = [_masked_exp(pick(c0s[p], c1s[p]) - pick(cum_t2[2 * p:2 * p + 1, :], cum_t2[2 * p + 1:2 * p + 2, :]), tri2)
            for p in pairs]
    vms = []
    for p in pairs:
        v = xps[p] * pick(dt[:, 2 * p:2 * p + 1], dt[:, 2 * p + 1:2 * p + 2])
        vms.append(jnp.concatenate([v * hm0, v * hm1], axis=0))
    y_intra = [_bdot(gmats[p // 3] * segs[p], vms[p]) for p in pairs]
    y_inter = [_bdot(jnp.concatenate([cgs[p // 3] * jnp.exp(c0s[p]), cgs[p // 3] * jnp.exp(c1s[p])], axis=1),
                     jnp.concatenate([state[p] * hm0, state[p] * hm1], axis=0)) for p in pairs]
    new_states = [_bdot_tn(jnp.concatenate([bgs[p // 3] * jnp.exp(e0s[p] - c0s[p]),
                                            bgs[p // 3] * jnp.exp(e1s[p] - c1s[p])], axis=0), vms[p])
                  + state[p] * pick(jnp.exp(e0s[p]), jnp.exp(e1s[p])) for p in pairs]
    ys = [y_intra[p] + y_inter[p] + pick(d_skip[:, 2 * p:2 * p + 1], d_skip[:, 2 * p + 1:2 * p + 2]) * xps[p] for p in pairs]
    y = jnp.concatenate(ys, axis=1) * _silu(z)
    gw = MIX_W // 2
    y = jnp.concatenate([_rms(y[:, g * gw:(g + 1) * gw], norm_w[:, g * gw:(g + 1) * gw]) for g in range(2)], axis=1)
    return (y,), jnp.stack(new_states)


def _gdn_chunk(tiles, halos, state, consts):
    qkv_raw, z, small = tiles
    (halo,) = halos
    conv_w, dt_bias, a_log, norm_w = consts
    qkv = _silu(_causal_conv(halo, qkv_raw, conv_w))
    beta_all = jax.nn.sigmoid(small)
    cum = _cumsum_rows(-jnp.exp(a_log) * jax.nn.softplus(small + dt_bias))
    cum_t = cum.T
    tri, tri_strict = _tri_mask(CHUNK), _tri_mask(CHUNK, strict=True)

    def group(hs):
        n = range(len(hs))
        qs, ks, betas, cs, ces, decays, rhss = [], [], [], [], [], [], []
        for h in hs:
            q = qkv[:, h * HEAD:(h + 1) * HEAD]
            k = qkv[:, MIX_W + h * HEAD:MIX_W + (h + 1) * HEAD]
            v = qkv[:, 2 * MIX_W + h * HEAD:2 * MIX_W + (h + 1) * HEAD]
            q = q * lax.rsqrt(jnp.sum(q * q, axis=-1, keepdims=True) + NORM_EPS) * (HEAD ** -0.5)
            k = k * lax.rsqrt(jnp.sum(k * k, axis=-1, keepdims=True) + NORM_EPS)
            beta = beta_all[:, GB_OFF + h:GB_OFF + h + 1]
            c, c_t = cum[:, GA_OFF + h:GA_OFF + h + 1], cum_t[GA_OFF + h:GA_OFF + h + 1, :]
            qs.append(q), ks.append(k), betas.append(beta), cs.append(c)
            ces.append(cum[CHUNK - 1:CHUNK, GA_OFF + h:GA_OFF + h + 1])
            decays.append(_masked_exp(c - c_t, tri))
            rhss.append(jnp.concatenate([v * beta, k * (beta * jnp.exp(c))], axis=1))
        sts = [state[h] for h in hs]
        qk_kks = [_bdot_nt(jnp.concatenate([qs[i], ks[i]], axis=0), ks[i]) for i in n]
        sols = _unit_lower_solves([jnp.where(tri_strict, betas[i] * qk_kks[i][CHUNK:] * decays[i], 0.0) for i in n], rhss)
        on_states = [_bdot(jnp.concatenate([sols[i][:, HEAD:], qs[i] * jnp.exp(cs[i])], axis=0), sts[i]) for i in n]
        us = [sols[i][:, :HEAD] - on_states[i][:CHUNK] for i in n]
        os_ = [on_states[i][CHUNK:] + _bdot(qk_kks[i][:CHUNK] * decays[i], us[i]) for i in n]
        new = [jnp.exp(ces[i]) * sts[i] + _bdot_tn(ks[i] * jnp.exp(ces[i] - cs[i]), us[i]) for i in n]
        outs = [_rms(os_[i], norm_w) * _silu(z[:, h * HEAD:(h + 1) * HEAD]) for i, h in enumerate(hs)]
        return outs, new

    outs, new_states = [], []
    for h0 in range(0, N_HEAD6, HEAD_GROUP):
        o, s = group(list(range(h0, h0 + HEAD_GROUP)))
        outs += o
        new_states += s
    return (jnp.concatenate(outs, axis=1),), jnp.stack(new_states)


def _scan_fwd(name, fn, tiled, halo_idx, consts, out_width, state_shape):
    seq = tiled[0][0].shape[0]
    nc = seq // CHUNK
    n_t, n_h, n_c = len(tiled), len(halo_idx), len(consts)

    def body(*refs):
        t_refs, h_refs, c_refs = refs[:n_t], refs[n_t:n_t + n_h], refs[n_t + n_h:n_t + n_h + n_c]
        y_ref, save_ref, st_ref = refs[n_t + n_h + n_c:]
        i = pl.program_id(0)

        @pl.when(i == 0)
        def _():
            st_ref[...] = jnp.zeros_like(st_ref)

        flag = jnp.where(i > 0, 1.0, 0.0).astype(F32)
        st = st_ref[...]
        (y,), new = fn([r[...] for r in t_refs], [r[...] * flag for r in h_refs], st, [r[...] for r in c_refs])
        save_ref[0] = st
        y_ref[...] = y.astype(y_ref.dtype)
        st_ref[...] = new

    in_specs = [pl.BlockSpec((CHUNK, w), functools.partial(lambda i, cb: (i, cb), cb=cb)) for _, w, cb in tiled]
    in_specs += [pl.BlockSpec((HALO, tiled[j][1]),
                              functools.partial(lambda i, cb: (jnp.maximum(i * (CHUNK // HALO) - 1, 0), cb), cb=tiled[j][2]))
                 for j in halo_idx]
    in_specs += [pl.BlockSpec(c.shape, functools.partial(lambda i, nd: (0,) * nd, nd=c.ndim)) for c in consts]
    zeros = (0,) * len(state_shape)
    return _pc(
        body, name=name, grid=(nc,), in_specs=in_specs,
        out_specs=(pl.BlockSpec((CHUNK, out_width), lambda i: (i, 0)),
                   pl.BlockSpec((1,) + state_shape, lambda i: (i,) + zeros)),
        out_shape=(jax.ShapeDtypeStruct((seq, out_width), BF), jax.ShapeDtypeStruct((nc,) + state_shape, F32)),
        scratch_shapes=[pltpu.VMEM(state_shape, F32)],
        compiler_params=_cparams(("arbitrary",)),
    )(*[t[0] for t in tiled], *[tiled[j][0] for j in halo_idx], *consts)


def _scan_bwd(name, fn, tiled, halo_idx, consts, saved, dy, dtile_dtypes, extra=None):
    seq = tiled[0][0].shape[0]
    nc = seq // CHUNK
    n_t, n_h, n_c = len(tiled), len(halo_idx), len(consts)
    state_shape = saved.shape[1:]
    n_x = 0 if extra is None else 1

    def body(*refs):
        t_refs, h_refs, c_refs = refs[:n_t], refs[n_t:n_t + n_h], refs[n_t + n_h:n_t + n_h + n_c]
        pos = n_t + n_h + n_c
        save_ref, dy_ref = refs[pos], refs[pos + 1]
        x_refs = refs[pos + 2:pos + 2 + n_x]
        pos += 2 + n_x
        dt_refs, dc_refs = refs[pos:pos + n_t], refs[pos + n_t:pos + n_t + n_c]
        dst_ref = refs[pos + n_t + n_c]
        carry_refs = refs[pos + n_t + n_c + 1:]
        i = pl.program_id(0)

        @pl.when(i == 0)
        def _():
            dst_ref[...] = jnp.zeros_like(dst_ref)
            for r in carry_refs:
                r[...] = jnp.zeros_like(r)
            for r in dc_refs:
                r[...] = jnp.zeros_like(r)

        flag = jnp.where(i < nc - 1, 1.0, 0.0).astype(F32)
        tiles = [r[...] for r in t_refs]
        halos = [r[...] * flag for r in h_refs]
        cvals = [r[...] for r in c_refs]
        _, vjp = jax.vjp(fn, tiles, halos, save_ref[0], cvals)
        d_tiles, d_halos, d_state, d_consts = vjp(((dy_ref[...].astype(F32),), dst_ref[...]))
        dst_ref[...] = d_state
        for r, g in zip(dc_refs, d_consts):
            r[...] += g
        for j, (r, g) in enumerate(zip(dt_refs, d_tiles)):
            if extra is not None and extra[0] == j:
                g = g + x_refs[0][...].astype(F32)
            r[...] = g.astype(r.dtype)
            if j in halo_idx:
                cr = carry_refs[halo_idx.index(j)]
                r[CHUNK - HALO:CHUNK, :] = (g[CHUNK - HALO:CHUNK, :] + cr[...]).astype(r.dtype)
                cr[...] = d_halos[halo_idx.index(j)] * flag

    rev = lambda i: nc - 1 - i
    in_specs = [pl.BlockSpec((CHUNK, w), functools.partial(lambda i, cb: (rev(i), cb), cb=cb)) for _, w, cb in tiled]
    in_specs += [pl.BlockSpec((HALO, tiled[j][1]),
                              functools.partial(lambda i, cb: (jnp.maximum(rev(i) * (CHUNK // HALO) - 1, 0), cb), cb=tiled[j][2]))
                 for j in halo_idx]
    in_specs += [pl.BlockSpec(c.shape, functools.partial(lambda i, nd: (0,) * nd, nd=c.ndim)) for c in consts]
    zeros = (0,) * len(state_shape)
    in_specs += [pl.BlockSpec((1,) + state_shape, lambda i: (rev(i),) + zeros),
                 pl.BlockSpec((CHUNK, dy.shape[1]), lambda i: (rev(i), 0))]
    args = [t[0] for t in tiled] + [tiled[j][0] for j in halo_idx] + list(consts) + [saved, dy]
    if extra is not None:
        in_specs.append(pl.BlockSpec((CHUNK, extra[1].shape[1]), lambda i: (rev(i), 0)))
        args.append(extra[1])
    out_specs = [pl.BlockSpec((CHUNK, w), lambda i: (rev(i), 0)) for _, w, _ in tiled]
    out_specs += [pl.BlockSpec(c.shape, functools.partial(lambda i, nd: (0,) * nd, nd=c.ndim)) for c in consts]
    out_shape = [jax.ShapeDtypeStruct((seq, w), dtd) for (_, w, _), dtd in zip(tiled, dtile_dtypes)]
    out_shape += [jax.ShapeDtypeStruct(c.shape, F32) for c in consts]
    scratch = [pltpu.VMEM(state_shape, F32)] + [pltpu.VMEM((HALO, tiled[j][1]), F32) for j in halo_idx]
    return _pc(body, name=name, grid=(nc,), in_specs=in_specs, out_specs=tuple(out_specs), out_shape=tuple(out_shape),
               scratch_shapes=scratch, compiler_params=_cparams(("arbitrary",)))(*args)


def _tile_fwd(name, fn, tiled, consts, outs, tm=TOKEN_TILE):
    seq = tiled[0][0].shape[0]
    n_t, n_c = len(tiled), len(consts)

    def body(*refs):
        res = fn(*[r[...] for r in refs[:n_t + n_c]])
        for r, y in zip(refs[n_t + n_c:], res):
            r[...] = y.astype(r.dtype)

    in_specs = [pl.BlockSpec((tm, w), functools.partial(lambda i, cb: (i, cb), cb=cb)) for _, w, cb in tiled]
    in_specs += [pl.BlockSpec(c.shape, functools.partial(lambda i, nd: (0,) * nd, nd=c.ndim)) for c in consts]
    return _pc(body, name=name, grid=(seq // tm,), in_specs=in_specs,
               out_specs=tuple(pl.BlockSpec((tm, w), lambda i: (i, 0)) for w, _ in outs),
               out_shape=tuple(jax.ShapeDtypeStruct((seq, w), dtp) for w, dtp in outs),
               compiler_params=_cparams(("arbitrary",)))(*[t[0] for t in tiled], *consts)


def _tile_bwd(name, fn, tiled, consts, douts, dtile_dtypes, add_to=None, tm=TOKEN_TILE):
    seq = tiled[0][0].shape[0]
    n_t, n_c, n_o = len(tiled), len(consts), len(douts)
    n_x = 0 if add_to is None else 1
    keep = [j for j, dtp in enumerate(dtile_dtypes) if dtp is not None]

    def body(*refs):
        vals = [r[...].astype(F32) for r in refs[:n_t + n_c]]
        pos = n_t + n_c
        g_refs, x_refs = refs[pos:pos + n_o], refs[pos + n_o:pos + n_o + n_x]
        pos += n_o + n_x
        dt_refs, dc_refs = refs[pos:pos + len(keep)], refs[pos + len(keep):]
        i = pl.program_id(0)

        @pl.when(i == 0)
        def _():
            for r in dc_refs:
                r[...] = jnp.zeros_like(r)

        _, vjp = jax.vjp(fn, *vals)
        cts = vjp(tuple(g[...].astype(F32) for g in g_refs))
        for r, j in zip(dt_refs, keep):
            g = cts[j]
            if add_to is not None and add_to[0] == j:
                g = g + x_refs[0][...].astype(F32)
            r[...] = g.astype(r.dtype)
        for r, g in zip(dc_refs, cts[n_t:]):
            r[...] += g

    in_specs = [pl.BlockSpec((tm, w), functools.partial(lambda i, cb: (i, cb), cb=cb)) for _, w, cb in tiled]
    in_specs += [pl.BlockSpec(c.shape, functools.partial(lambda i, nd: (0,) * nd, nd=c.ndim)) for c in consts]
    in_specs += [pl.BlockSpec((tm, g.shape[1]), lambda i: (i, 0)) for g in douts]
    args = [t[0] for t in tiled] + list(consts) + list(douts)
    if add_to is not None:
        in_specs.append(pl.BlockSpec((tm, add_to[1].shape[1]), lambda i: (i, 0)))
        args.append(add_to[1])
    out_specs = [pl.BlockSpec((tm, tiled[j][1]), lambda i: (i, 0)) for j in keep]
    out_specs += [pl.BlockSpec(c.shape, functools.partial(lambda i, nd: (0,) * nd, nd=c.ndim)) for c in consts]
    out_shape = [jax.ShapeDtypeStruct((seq, tiled[j][1]), dtile_dtypes[j]) for j in keep]
    out_shape += [jax.ShapeDtypeStruct(c.shape, F32) for c in consts]
    return _pc(body, name=name, grid=(seq // tm,), in_specs=in_specs, out_specs=tuple(out_specs),
               out_shape=tuple(out_shape), compiler_params=_cparams(("arbitrary",)))(*args)


def _lnmod_fn(x, nw, sc, sh):
    return (_rms(x, nw) * (1.0 + sc) + sh,)


def _resid_fn(x, o, g):
    return (x + (1.0 + g) * o,)


def _swiglu_fn(gu):
    return (_silu(gu[:, :FFN_H]) * gu[:, FFN_H:],)


def _merge_fn(yh, ys, yg, logits, wb, b_merge):
    gates = jax.nn.sigmoid(logits + b_merge)
    acc = None
    for n, y in enumerate((yh, ys, yg)):
        t = gates[:, n * D_MODEL:(n + 1) * D_MODEL] * _bdot(y, wb[n])
        acc = t if acc is None else acc + t
    return (acc,)


MM_VMEM_BUDGET = 40 * 1024 * 1024
MM_TILE_CAP = 1024
MM_K_CAP = 4096


def _divisor(n, cap, unit=LANES):
    best = None
    for d in range(unit, min(n, cap) + 1, unit):
        if n % d == 0:
            best = d
    return n if best is None else best


def _mm_tiles(m, n, k, out_bytes):
    tk = k if k <= MM_K_CAP else _divisor(k, 3072)
    tm, tn = _divisor(m, MM_TILE_CAP), _divisor(n, MM_TILE_CAP + MM_TILE_CAP // 2)

    def need(tm_, tn_):
        acc = tm_ * tn_ * 4 if tk < k else 0
        return 2 * 2 * tk * (tm_ + tn_) + acc + 2 * tm_ * tn_ * out_bytes

    while need(tm, tn) > MM_VMEM_BUDGET:
        if tn >= tm and _divisor(n, tn - LANES) < tn:
            tn = _divisor(n, tn - LANES)
        elif _divisor(m, tm - LANES) < tm:
            tm = _divisor(m, tm - LANES)
        else:
            break
    return tm, tn, tk


def _mm(a, b, mode, out_dtype, name):
    if mode == "nn":
        (m, k), n = a.shape, b.shape[1]
    elif mode == "nt":
        (m, k), n = a.shape, b.shape[0]
    else:
        (k, m), n = a.shape, b.shape[1]
    tm, tn, tk = _mm_tiles(m, n, k, jnp.dtype(out_dtype).itemsize)
    nk = k // tk
    dims = {"nn": ((1,), (0,)), "nt": ((1,), (1,)), "tn": ((0,), (0,))}[mode]

    def body_one(a_ref, b_ref, o_ref):
        o_ref[...] = lax.dot_general(a_ref[...], b_ref[...], (dims, ((), ())), preferred_element_type=F32).astype(o_ref.dtype)

    def body_acc(a_ref, b_ref, o_ref, acc_ref):
        kk = pl.program_id(2)

        @pl.when(kk == 0)
        def _():
            acc_ref[...] = jnp.zeros_like(acc_ref)

        acc_ref[...] += lax.dot_general(a_ref[...], b_ref[...], (dims, ((), ())), preferred_element_type=F32)

        @pl.when(kk == nk - 1)
        def _():
            o_ref[...] = acc_ref[...].astype(o_ref.dtype)

    a_spec = pl.BlockSpec((tk, tm), lambda i, j, kk: (kk, i)) if mode == "tn" else pl.BlockSpec((tm, tk), lambda i, j, kk: (i, kk))
    b_spec = pl.BlockSpec((tn, tk), lambda i, j, kk: (j, kk)) if mode == "nt" else pl.BlockSpec((tk, tn), lambda i, j, kk: (kk, j))
    return _pc(body_one if nk == 1 else body_acc, name=name, grid=(m // tm, n // tn, nk), in_specs=[a_spec, b_spec],
               out_specs=pl.BlockSpec((tm, tn), lambda i, j, kk: (i, j)),
               out_shape=jax.ShapeDtypeStruct((m, n), out_dtype),
               scratch_shapes=[] if nk == 1 else [pltpu.VMEM((tm, tn), F32)],
               compiler_params=_cparams(("parallel", "parallel", "arbitrary")))(a.astype(BF), b.astype(BF))


def _final_loss(x, tgt, norm_final, tm=TOKEN_TILE):
    seq = x.shape[0]

    def fn(xv, nf, tv):
        err = jnp.square(_rms(xv, nf) - tv)
        return 0.5 * jnp.sum(jnp.mean(err, axis=-1))

    def body(x_ref, t_ref, nf_ref, loss_ref, dx_ref, dnf_ref):
        i = pl.program_id(0)

        @pl.when(i == 0)
        def _():
            loss_ref[...] = jnp.zeros_like(loss_ref)
            dnf_ref[...] = jnp.zeros_like(dnf_ref)

        val, vjp = jax.vjp(functools.partial(fn, tv=t_ref[...]), x_ref[...], nf_ref[...])
        dx, dnf = vjp(jnp.ones((), F32))
        dx_ref[...] = dx
        dnf_ref[...] += dnf
        loss_ref[...] += jnp.broadcast_to(val, loss_ref.shape)

    return _pc(body, name="final_loss", grid=(seq // tm,),
               in_specs=[pl.BlockSpec((tm, D_MODEL), lambda i: (i, 0)), pl.BlockSpec((tm, D_MODEL), lambda i: (i, 0)),
                         pl.BlockSpec((1, D_MODEL), lambda i: (0, 0))],
               out_specs=(pl.BlockSpec((8, LANES), lambda i: (0, 0)), pl.BlockSpec((tm, D_MODEL), lambda i: (i, 0)),
                          pl.BlockSpec((1, D_MODEL), lambda i: (0, 0))),
               out_shape=(jax.ShapeDtypeStruct((8, LANES), F32), jax.ShapeDtypeStruct((seq, D_MODEL), F32),
                          jax.ShapeDtypeStruct((1, D_MODEL), F32)),
               compiler_params=_cparams(("arbitrary",)))(x, tgt, norm_final)


def _ada_fwd(c_all, w_ada, b_ada_cols):
    n_l, _, cols = w_ada.shape

    def body(c_ref, w_ref, b_ref, o_ref):
        o_ref[0] = jnp.dot(_silu(c_ref[...]), w_ref[0], preferred_element_type=F32, precision=HI) + b_ref[0]

    return _pc(body, name="ada_fwd", grid=(n_l,),
               in_specs=[pl.BlockSpec((8, D_MODEL), lambda l: (0, 0)), pl.BlockSpec((1, D_MODEL, cols), lambda l: (l, 0, 0)),
                         pl.BlockSpec((1, 1, cols), lambda l: (l, 0, 0))],
               out_specs=pl.BlockSpec((1, 8, cols), lambda l: (l, 0, 0)),
               out_shape=jax.ShapeDtypeStruct((n_l, 8, cols), F32),
               compiler_params=_cparams(("arbitrary",)))(c_all, w_ada, b_ada_cols.reshape(n_l, 1, cols))


def _ada_bwd(c_all, dmod_cols):
    n_l, _, cols = dmod_cols.shape

    def body(c_ref, g_ref, o_ref):
        o_ref[0] = lax.dot_general(_silu(c_ref[...]), g_ref[0], (((0,), (0,)), ((), ())), preferred_element_type=F32,
                                   precision=HI)

    return _pc(body, name="ada_bwd", grid=(n_l,),
               in_specs=[pl.BlockSpec((8, D_MODEL), lambda l: (0, 0)), pl.BlockSpec((1, 8, cols), lambda l: (l, 0, 0))],
               out_specs=pl.BlockSpec((1, D_MODEL, cols), lambda l: (l, 0, 0)),
               out_shape=jax.ShapeDtypeStruct((n_l, D_MODEL, cols), F32),
               compiler_params=_cparams(("arbitrary",)))(c_all, dmod_cols)


def _lb_fn(logits):
    e = jnp.exp(logits - jnp.max(logits, axis=0, keepdims=True))
    p = e / jnp.sum(e, axis=0, keepdims=True)
    r = lax.broadcasted_iota(jnp.int32, (DEPTH, 1), 0)
    lb = jnp.zeros_like(p)
    for j in range(1, DEPTH):
        lb = lb + jnp.where(r >= j, p[j:j + 1, :], 0.0)
    return lb


def _lb_fwd(logits):
    def body(l_ref, o_ref):
        o_ref[...] = _lb_fn(l_ref[...])

    return _pc(body, name="lb_fwd", out_shape=jax.ShapeDtypeStruct(logits.shape, F32))(logits)


def _lb_bwd(logits, dlb):
    def body(l_ref, g_ref, o_ref):
        _, vjp = jax.vjp(_lb_fn, l_ref[...])
        o_ref[...] = vjp(g_ref[...])[0]

    return _pc(body, name="lb_bwd", out_shape=jax.ShapeDtypeStruct(logits.shape, F32))(logits, dlb)


def _rows_for(n_rows, n_cols):
    r = 8
    while r * 2 <= n_rows and n_rows % (r * 2) == 0 and r * 2 * n_cols <= 256 * 1024:
        r *= 2
    return r if n_rows % r == 0 else n_rows


def _ew(name, fn, ins, out_dtypes):
    n_rows, n_cols = ins[0].shape
    tr = _rows_for(n_rows, n_cols)
    n_in = len(ins)

    def body(*refs):
        res = fn(*[r[...] for r in refs[:n_in]])
        for r, y in zip(refs[n_in:], res):
            r[...] = y.astype(r.dtype)

    spec = pl.BlockSpec((tr, n_cols), lambda i: (i, 0))
    return _pc(body, name=name, grid=(n_rows // tr,), in_specs=[spec] * n_in, out_specs=tuple([spec] * len(out_dtypes)),
               out_shape=tuple(jax.ShapeDtypeStruct((n_rows, n_cols), d) for d in out_dtypes),
               compiler_params=_cparams(("arbitrary",)))(*ins)


def _adamw_fn(w, g, m, v):
    m = ADAM_B1 * m + (1.0 - ADAM_B1) * g
    v = ADAM_B2 * v + (1.0 - ADAM_B2) * jnp.square(g)
    m_hat = m / (1.0 - ADAM_B1 ** ADAM_STEP)
    v_hat = v / (1.0 - ADAM_B2 ** ADAM_STEP)
    return -ADAM_LR * (m_hat / (jnp.sqrt(v_hat) + ADAM_EPS) + ADAM_WD * w), m, v


def _adamw(name, w, g, m, v):
    shape = w.shape
    two = (-1, shape[-1])
    d, nm, nv = _ew(name, _adamw_fn, [a.reshape(two) for a in (w, g, m, v)], [F32, F32, F32])
    return d.reshape(shape), nm.reshape(shape), nv.reshape(shape)


def _sum_leading(name, a, out_dtype):
    n, n_rows, n_cols = a.shape
    tr = _rows_for(n_rows, n_cols)

    def body(a_ref, o_ref):
        acc = a_ref[0].astype(F32)
        for j in range(1, n):
            acc = acc + a_ref[j].astype(F32)
        o_ref[...] = acc.astype(o_ref.dtype)

    return _pc(body, name=name, grid=(n_rows // tr,), in_specs=[pl.BlockSpec((n, tr, n_cols), lambda i: (0, i, 0))],
               out_specs=pl.BlockSpec((tr, n_cols), lambda i: (i, 0)),
               out_shape=jax.ShapeDtypeStruct((n_rows, n_cols), out_dtype),
               compiler_params=_cparams(("arbitrary",)))(a)


MESH = pl.DeviceIdType.MESH
ANY = pl.BlockSpec(memory_space=pl.ANY)


def _place():
    return lax.axis_index("x"), lax.axis_index("y"), lax.axis_index("c")


def _all_gather_small(name, a):
    m_per, n = a.shape

    def body(x_ref, out_ref, send_sems, recv_sems, local_sem):
        x, y, c = _place()
        me, sibling = (x, y, c), (x, y, 1 - c)
        chips = [(1 - x, y), (x, 1 - y), (1 - x, 1 - y)]

        def rows(px, py, pc):
            return out_ref.at[pl.ds((4 * px + 2 * py + pc) * m_per, m_per), :]

        def copy(k, block, to, src=None):
            return pltpu.make_async_remote_copy(src_ref=rows(*block) if src is None else src, dst_ref=rows(*block),
                                                send_sem=send_sems.at[k], recv_sem=recv_sems.at[k], device_id=to,
                                                device_id_type=MESH)

        mine = pltpu.make_async_copy(x_ref, rows(*me), local_sem)
        mine.start()
        first = [copy(0, me, sibling, src=x_ref)]
        first += [copy(1 + j, me, (*chip, c), src=x_ref) for j, chip in enumerate(chips)]
        for cp in first:
            cp.start()
        passed = [copy(4 + j, (*chip, c), sibling) for j, chip in enumerate(chips)]
        for j, chip in enumerate(chips):
            copy(1 + j, (*chip, c), me).wait_recv()
            passed[j].start()
        copy(0, sibling, me).wait_recv()
        for j, chip in enumerate(chips):
            copy(4 + j, (*chip, 1 - c), me).wait_recv()
        for cp in first + passed:
            cp.wait_send()
        mine.wait()

    out = _pc(body, name=name, out_shape=jax.ShapeDtypeStruct((8 * m_per, n), a.dtype),
              in_specs=[pl.BlockSpec(memory_space=pltpu.VMEM)], out_specs=pl.BlockSpec(memory_space=pltpu.VMEM),
              scratch_shapes=[pltpu.SemaphoreType.DMA((7,)), pltpu.SemaphoreType.DMA((7,)), pltpu.SemaphoreType.DMA],
              compiler_params=pltpu.CompilerParams(vmem_limit_bytes=VMEM_LIMIT))(a)
    return out.reshape(8, m_per, n)


def _chip_gather(name, pack):
    n_l, n_r, n_c = pack.shape
    half = n_r // 2

    def body(p_ref, o_ref, send_sems, recv_sems):
        x, y, c = _place()
        sibling = (x, y, 1 - c)
        chips = [(1 - x, y), (x, 1 - y), (1 - x, 1 - y)]

        def slab(px, py, pc):
            return o_ref.at[2 * px + py, :, pl.ds(pc * half, half), :]

        def copy(k, src, dst, to):
            return pltpu.make_async_remote_copy(src_ref=src, dst_ref=dst, send_sem=send_sems.at[k], recv_sem=recv_sems.at[k],
                                                device_id=to, device_id_type=MESH)

        first = [copy(j, p_ref.at[:, pl.ds(c * half, half), :], slab(x, y, c), (*chip, c)) for j, chip in enumerate(chips)]
        for cp in first:
            cp.start()
        passed = [copy(3 + j, slab(*chip, c), slab(*chip, c), sibling) for j, chip in enumerate(chips)]
        for j, chip in enumerate(chips):
            copy(j, slab(*chip, c), slab(*chip, c), (*chip, c)).wait_recv()
            passed[j].start()
        for j, chip in enumerate(chips):
            copy(3 + j, slab(*chip, 1 - c), slab(*chip, 1 - c), sibling).wait_recv()
        for cp in first + passed:
            cp.wait_send()

    return _pc(body, name=name, out_shape=jax.ShapeDtypeStruct((4, n_l, n_r, n_c), pack.dtype), in_specs=[ANY], out_specs=ANY,
               scratch_shapes=[pltpu.SemaphoreType.DMA((6,)), pltpu.SemaphoreType.DMA((6,))])(pack)


def _pair_swap(name, give):
    def body(g_ref, o_ref, send_sem, recv_sem):
        x, y, c = _place()
        cp = pltpu.make_async_remote_copy(src_ref=g_ref, dst_ref=o_ref, send_sem=send_sem, recv_sem=recv_sem,
                                          device_id=(x, y, 1 - c), device_id_type=MESH)
        cp.start()
        cp.wait()

    return _pc(body, name=name, out_shape=jax.ShapeDtypeStruct(give.shape, give.dtype), in_specs=[ANY], out_specs=ANY,
               scratch_shapes=[pltpu.SemaphoreType.DMA, pltpu.SemaphoreType.DMA])(give)


def _chip_exchange(name, parts):
    def body(p_ref, o_ref, send_sems, recv_sems):
        x, y, c = _place()
        me = 2 * x + y
        chips = [(1 - x, y), (x, 1 - y), (1 - x, 1 - y)]

        def copy(k, src, dst, to):
            return pltpu.make_async_remote_copy(src_ref=src, dst_ref=dst, send_sem=send_sems.at[k], recv_sem=recv_sems.at[k],
                                                device_id=to, device_id_type=MESH)

        sends = [copy(j, p_ref.at[2 * px + py], o_ref.at[me], (px, py, c)) for j, (px, py) in enumerate(chips)]
        for cp in sends:
            cp.start()
        for j, (px, py) in enumerate(chips):
            copy(j, p_ref.at[2 * px + py], o_ref.at[2 * px + py], (px, py, c)).wait_recv()
        for cp in sends:
            cp.wait_send()

    return _pc(body, name=name, out_shape=jax.ShapeDtypeStruct(parts.shape, parts.dtype), in_specs=[ANY], out_specs=ANY,
               scratch_shapes=[pltpu.SemaphoreType.DMA((3,)), pltpu.SemaphoreType.DMA((3,))])(parts)


N_CHIP = 4
BIG = (("w_in", (1024, 2822), 1, (1024, 2822)), ("w_branch", (3, 768, 256), 2, (2304, 256)),
       ("w_out", (256, 1024), 0, (256, 1024)), ("w_ffn_in", (1024, 1408), 1, (1024, 1408)),
       ("w_ffn_out", (704, 1024), 0, (704, 1024)))
G768 = ((0, 3072), (3072, 3840), (7436, 8204))
GXBC, GQKV, GGATE = (3840, 5120), (5132, 7436), (8216, 11288)
GSMALL = ((5120, 5132), (8204, 8210), (8210, 8216))
W768, WXBC, WGATE = 4608, CONV_CH, 3 * D_MODEL
IN_PAD = W768 + WXBC + QKV_W + WGATE + SMALL_W


def _join_shards(slabs, axis, shard_shape):
    n_l = slabs.shape[1]
    parts = [slabs[j].reshape((n_l,) + shard_shape) for j in range(N_CHIP)]
    return jnp.concatenate(parts, axis=axis + 1)


def _split_shards(full, axis, rows_cols):
    n_l = full.shape[0]
    size = full.shape[axis + 1] // N_CHIP
    return jnp.stack([lax.slice_in_dim(full, j * size, (j + 1) * size, axis=axis + 1).reshape((n_l,) + rows_cols)
                      for j in range(N_CHIP)])


def _gather_weights(w, chip, big=BIG):
    out = {}
    for n, shape, ax, rc in big:
        n_l = w[n].shape[0]
        mine = w[n].astype(BF).reshape((n_l,) + rc)
        slabs = lax.dynamic_update_slice(_chip_gather("gather_" + n, mine), mine[None], (chip, 0, 0, 0))
        out[n] = _join_shards(slabs, ax, shape)
    return out


def _reduce_grads(full_grads, chip, core, big=BIG):
    out = {}
    for n, shape, ax, (rows, cols) in big:
        n_l = full_grads[n].shape[0]
        slabs = _split_shards(full_grads[n], ax, (rows, cols))
        half = rows // 2
        keep = lax.dynamic_slice_in_dim(slabs, core * half, half, axis=2).reshape(-1, cols)
        give = lax.dynamic_slice_in_dim(slabs, (1 - core) * half, half, axis=2).reshape(-1, cols)
        got = _pair_swap("pair_swap_" + n, give)
        (pair_sum,) = _ew("pair_sum_" + n, lambda a, b: (a.astype(F32) + b.astype(F32),), [keep, got], [BF])
        pair_sum = pair_sum.reshape(N_CHIP, n_l * half, cols)
        parts = lax.dynamic_update_slice(_chip_exchange("chip_exchange_" + n, pair_sum),
                                         lax.dynamic_slice_in_dim(pair_sum, chip, 1, axis=0), (chip, 0, 0))
        mine = _sum_leading("chip_sum_" + n, parts, F32).reshape(n_l, half, cols)
        theirs = _pair_swap("pair_share_" + n, mine)
        full = jnp.concatenate([jnp.where(core == 0, mine, theirs), jnp.where(core == 0, theirs, mine)], axis=1)
        out[n] = full.reshape((n_l,) + shape)
    return out


def _regroup_w_in(w):
    cat = lambda spans: jnp.concatenate([w[:, a:b] for a, b in spans], axis=1)
    small = jnp.concatenate([cat(GSMALL), jnp.zeros((w.shape[0], SMALL_W - 24), w.dtype)], axis=1)
    return cat(G768), cat((GXBC,)), cat((GQKV,)), cat((GGATE,)), small


def _ungroup_w_in(d):
    o_xbc, o_qkv, o_gate, o_small = W768, W768 + WXBC, W768 + WXBC + QKV_W, W768 + WXBC + QKV_W + WGATE
    spans = ((0, 3072), (3072, 3840), (o_xbc, o_xbc + WXBC), (o_small, o_small + 12), (o_qkv, o_qkv + QKV_W),
             (3840, 4608), (o_small + 12, o_small + 18), (o_small + 18, o_small + 24), (o_gate, o_gate + WGATE))
    return jnp.concatenate([d[:, a:b] for a, b in spans], axis=1)


def _lane_pad(v, off):
    return jnp.pad(v, (off, LANES - off - v.shape[0]))[None, :]


STATE6 = (N_HEAD6, HEAD, HEAD)


def _mixer_inputs(sv, lp):
    p768, pxbc, pqkv, psmall = sv["p768"], sv["pxbc"], sv["pqkv"], sv["psmall"]
    hgrn = ([(p768, MIX_W, j) for j in range(4)], [], [lp["lb"], lp["hgrn_norm"]])
    ssd = ([(p768, MIX_W, 4), (pxbc, CONV_CH, 0), (psmall, LANES, 0)], [1],
           [lp["ssm_conv_w"], lp["ssm_conv_b"], lp["ssm_dt_bias"], lp["ssm_a_log"], lp["ssm_d"], lp["ssm_norm"]])
    gdn = ([(pqkv, QKV_W, 0), (p768, MIX_W, 5), (psmall, LANES, 0)], [0],
           [lp["gdn_conv_w"], lp["gdn_dt_bias"], lp["gdn_a_log"], lp["gdn_norm"]])
    return hgrn, ssd, gdn


def _layer_fwd(x, md, lw, lp):
    sv = {"x": x}
    (sv["h1"],) = _tile_fwd("lnmod1", _lnmod_fn, [(x, D_MODEL, 0)], [lp["norm_mix"], md["sc1"], md["sh1"]], [(D_MODEL, BF)])
    for nm in ("768", "xbc", "qkv", "gate", "small"):
        sv["p" + nm] = _mm(sv["h1"], lw["win_" + nm], "nn", F32, "proj_" + nm)
    hgrn, ssd, gdn = _mixer_inputs(sv, lp)
    sv["y_h"], sv["st_h"] = _scan_fwd("hgrn_fwd", _hgrn_chunk, *hgrn, MIX_W, STATE6)
    sv["y_s"], sv["st_s"] = _scan_fwd("ssd_fwd", _ssd_chunk, *ssd, MIX_W, STATE6)
    sv["y_g"], sv["st_g"] = _scan_fwd("gdn_fwd", _gdn_chunk, *gdn, MIX_W, STATE6)
    (sv["merged"],) = _tile_fwd("merge", _merge_fn, _merge_tiles(sv), [lw["w_branch"], lp["b_merge"]], [(D_MODEL, BF)])
    sv["out"] = _mm(sv["merged"], lw["w_out"], "nn", F32, "out_proj")
    (sv["x_mid"],) = _tile_fwd("resid1", _resid_fn, [(x, D_MODEL, 0), (sv["out"], D_MODEL, 0)], [md["g1"]], [(D_MODEL, F32)])
    (sv["h2"],) = _tile_fwd("lnmod2", _lnmod_fn, [(sv["x_mid"], D_MODEL, 0)], [lp["norm_ffn"], md["sc2"], md["sh2"]],
                            [(D_MODEL, BF)])
    sv["gu"] = _mm(sv["h2"], lw["w_ffn_in"], "nn", F32, "ffn_in")
    (sv["act"],) = _tile_fwd("swiglu", _swiglu_fn, [(sv["gu"], 2 * FFN_H, 0)], [], [(FFN_H, BF)])
    sv["o2"] = _mm(sv["act"], lw["w_ffn_out"], "nn", F32, "ffn_out")
    (x_out,) = _tile_fwd("resid2", _resid_fn, [(sv["x_mid"], D_MODEL, 0), (sv["o2"], D_MODEL, 0)], [md["g2"]], [(D_MODEL, F32)])
    return x_out, sv


def _merge_tiles(sv):
    return [(sv["y_h"], MIX_W, 0), (sv["y_s"], MIX_W, 0), (sv["y_g"], MIX_W, 0), (sv["pgate"], WGATE, 0)]


def _layer_bwd(dx_out, sv, md, lw, lp):
    g = {}
    x, x_mid = sv["x"], sv["x_mid"]
    d_xmid, d_o2, g["g2"] = _tile_bwd("resid2_b", _resid_fn, [(x_mid, D_MODEL, 0), (sv["o2"], D_MODEL, 0)], [md["g2"]], [dx_out],
                                      [F32, BF])
    d_act = _mm(d_o2, lw["w_ffn_out"], "nt", F32, "ffn_out_dx")
    g["w_ffn_out"] = _mm(sv["act"], d_o2, "tn", BF, "ffn_out_dw")
    (d_gu,) = _tile_bwd("swiglu_b", _swiglu_fn, [(sv["gu"], 2 * FFN_H, 0)], [], [d_act], [BF])
    d_h2 = _mm(d_gu, lw["w_ffn_in"], "nt", F32, "ffn_in_dx")
    g["w_ffn_in"] = _mm(sv["h2"], d_gu, "tn", BF, "ffn_in_dw")
    d_xmid, g["norm_ffn"], g["sc2"], g["sh2"] = _tile_bwd(
        "lnmod2_b", _lnmod_fn, [(x_mid, D_MODEL, 0)], [lp["norm_ffn"], md["sc2"], md["sh2"]], [d_h2], [F32], add_to=(0, d_xmid))
    d_x, d_out, g["g1"] = _tile_bwd("resid1_b", _resid_fn, [(x, D_MODEL, 0), (sv["out"], D_MODEL, 0)], [md["g1"]], [d_xmid],
                                    [F32, BF])
    d_merged = _mm(d_out, lw["w_out"], "nt", F32, "out_proj_dx")
    g["w_out"] = _mm(sv["merged"], d_out, "tn", BF, "out_proj_dw")
    d_yh, d_ys, d_yg, d_gate, g["w_branch"], g["b_merge"] = _tile_bwd(
        "merge_b", _merge_fn, _merge_tiles(sv), [lw["w_branch"], lp["b_merge"]], [d_merged], [F32, F32, F32, BF])
    hgrn, ssd, gdn = _mixer_inputs(sv, lp)
    d_q, d_f, d_v, d_g, g["lb"], g["hgrn_norm"] = _scan_bwd("hgrn_bwd", _hgrn_chunk, *hgrn, sv["st_h"], d_yh, [BF] * 4)
    (d_sz, d_xbc, d_small, g["ssm_conv_w"], g["ssm_conv_b"], g["ssm_dt_bias"], g["ssm_a_log"], g["ssm_d"],
     g["ssm_norm"]) = _scan_bwd("ssd_bwd", _ssd_chunk, *ssd, sv["st_s"], d_ys, [BF, BF, F32])
    d_qkv, d_gz, d_small, g["gdn_conv_w"], g["gdn_dt_bias"], g["gdn_a_log"], g["gdn_norm"] = _scan_bwd(
        "gdn_bwd", _gdn_chunk, *gdn, sv["st_g"], d_yg, [BF, BF, BF], extra=(2, d_small))
    d_proj = jnp.concatenate([d_q, d_f, d_v, d_g, d_sz, d_gz, d_xbc, d_qkv, d_gate, d_small,
                              jnp.zeros((x.shape[0], SMALL_W - LANES), BF)], axis=1)
    d_h1 = _mm(d_proj, lw["win_all"], "nt", F32, "proj_dx")
    g["w_in"] = _ungroup_w_in(_mm(sv["h1"], d_proj, "tn", BF, "proj_dw"))
    d_x, g["norm_mix"], g["sc1"], g["sh1"] = _tile_bwd(
        "lnmod1_b", _lnmod_fn, [(x, D_MODEL, 0)], [lp["norm_mix"], md["sc1"], md["sh1"]], [d_h1], [F32], add_to=(0, d_x))
    return d_x, g


SMALL_REPL = ("norm_mix", "norm_ffn", "b_merge", "hgrn_lb_logits", "hgrn_norm", "ssm_conv_w", "ssm_conv_b", "ssm_dt_bias",
              "ssm_a_log", "ssm_d", "ssm_norm", "gdn_conv_w", "gdn_dt_bias", "gdn_a_log", "gdn_norm", "norm_final")
WEIGHTS = ("w_ada", "b_ada", "norm_mix", "norm_ffn", "w_in", "b_merge", "hgrn_lb_logits", "hgrn_norm", "ssm_conv_w",
           "ssm_conv_b", "ssm_dt_bias", "ssm_a_log", "ssm_d", "ssm_norm", "gdn_conv_w", "gdn_dt_bias", "gdn_a_log",
           "gdn_norm", "w_branch", "w_out", "w_ffn_in", "w_ffn_out", "norm_final")
SMALL_ROWS = 120


def _pad_rows(flat, n_rows, n_cols):
    return jnp.concatenate([flat, jnp.zeros((n_rows * n_cols - flat.shape[0],), flat.dtype)]).reshape(n_rows, n_cols)


def _device_step(x, tgt, mod, lb, wfull, sp):
    mds, lps, svs = [], [], []
    h = x
    for l in range(DEPTH):
        md = {n: mod[l, i * D_MODEL:(i + 1) * D_MODEL][None, :] for i, n in enumerate(("sh1", "sc1", "g1", "sh2", "sc2", "g2"))}
        lp = {n: sp[n][l][None, :] for n in ("norm_mix", "norm_ffn", "b_merge", "hgrn_norm", "ssm_conv_b", "ssm_norm", "gdn_norm")}
        lp["lb"] = lb[l][None, :]
        lp["ssm_conv_w"], lp["gdn_conv_w"] = sp["ssm_conv_w"][l], sp["gdn_conv_w"][l]
        for n in ("ssm_dt_bias", "ssm_a_log", "ssm_d"):
            lp[n] = _lane_pad(sp[n][l], DT_OFF)
        for n in ("gdn_dt_bias", "gdn_a_log"):
            lp[n] = _lane_pad(sp[n][l], GA_OFF)
        h, sv = _layer_fwd(h, md, wfull[l], lp)
        mds.append(md), lps.append(lp), svs.append(sv)
    loss, dh, d_nf = _final_loss(h, tgt, sp["norm_final"][None, :])
    grads = [None] * DEPTH
    for l in reversed(range(DEPTH)):
        dh, grads[l] = _layer_bwd(dh, svs[l], mds[l], wfull[l], lps[l])
    return loss, dh, d_nf, grads


def kernel(x, c, w_ada, b_ada, norm_mix, norm_ffn, w_in, b_merge, hgrn_lb_logits, hgrn_norm, ssm_conv_w, ssm_conv_b, ssm_dt_bias, ssm_a_log, ssm_d, ssm_norm, gdn_conv_w, gdn_dt_bias, gdn_a_log, gdn_norm, w_branch, w_out, w_ffn_in, w_ffn_out, norm_final, loss_target, m_w_ada, m_b_ada, m_norm_mix, m_norm_ffn, m_w_in, m_b_merge, m_hgrn_lb_logits, m_hgrn_norm, m_ssm_conv_w, m_ssm_conv_b, m_ssm_dt_bias, m_ssm_a_log, m_ssm_d, m_ssm_norm, m_gdn_conv_w, m_gdn_dt_bias, m_gdn_a_log, m_gdn_norm, m_w_branch, m_w_out, m_w_ffn_in, m_w_ffn_out, m_norm_final, v_w_ada, v_b_ada, v_norm_mix, v_norm_ffn, v_w_in, v_b_merge, v_hgrn_lb_logits, v_hgrn_norm, v_ssm_conv_w, v_ssm_conv_b, v_ssm_dt_bias, v_ssm_a_log, v_ssm_d, v_ssm_norm, v_gdn_conv_w, v_gdn_dt_bias, v_gdn_a_log, v_gdn_norm, v_w_branch, v_w_out, v_w_ffn_in, v_w_ffn_out, v_norm_final):
    w = dict(w_ada=w_ada, b_ada=b_ada, norm_mix=norm_mix, norm_ffn=norm_ffn, w_in=w_in, b_merge=b_merge,
             hgrn_lb_logits=hgrn_lb_logits, hgrn_norm=hgrn_norm, ssm_conv_w=ssm_conv_w, ssm_conv_b=ssm_conv_b,
             ssm_dt_bias=ssm_dt_bias, ssm_a_log=ssm_a_log, ssm_d=ssm_d, ssm_norm=ssm_norm, gdn_conv_w=gdn_conv_w,
             gdn_dt_bias=gdn_dt_bias, gdn_a_log=gdn_a_log, gdn_norm=gdn_norm, w_branch=w_branch, w_out=w_out,
             w_ffn_in=w_ffn_in, w_ffn_out=w_ffn_out, norm_final=norm_final)
    m = dict(w_ada=m_w_ada, b_ada=m_b_ada, norm_mix=m_norm_mix, norm_ffn=m_norm_ffn, w_in=m_w_in, b_merge=m_b_merge,
             hgrn_lb_logits=m_hgrn_lb_logits, hgrn_norm=m_hgrn_norm, ssm_conv_w=m_ssm_conv_w, ssm_conv_b=m_ssm_conv_b,
             ssm_dt_bias=m_ssm_dt_bias, ssm_a_log=m_ssm_a_log, ssm_d=m_ssm_d, ssm_norm=m_ssm_norm, gdn_conv_w=m_gdn_conv_w,
             gdn_dt_bias=m_gdn_dt_bias, gdn_a_log=m_gdn_a_log, gdn_norm=m_gdn_norm, w_branch=m_w_branch, w_out=m_w_out,
             w_ffn_in=m_w_ffn_in, w_ffn_out=m_w_ffn_out, norm_final=m_norm_final)
    v = dict(w_ada=v_w_ada, b_ada=v_b_ada, norm_mix=v_norm_mix, norm_ffn=v_norm_ffn, w_in=v_w_in, b_merge=v_b_merge,
             hgrn_lb_logits=v_hgrn_lb_logits, hgrn_norm=v_hgrn_norm, ssm_conv_w=v_ssm_conv_w, ssm_conv_b=v_ssm_conv_b,
             ssm_dt_bias=v_ssm_dt_bias, ssm_a_log=v_ssm_a_log, ssm_d=v_ssm_d, ssm_norm=v_ssm_norm, gdn_conv_w=v_gdn_conv_w,
             gdn_dt_bias=v_gdn_dt_bias, gdn_a_log=v_gdn_a_log, gdn_norm=v_gdn_norm, w_branch=v_w_branch, w_out=v_w_out,
             w_ffn_in=v_w_ffn_in, w_ffn_out=v_w_ffn_out, norm_final=v_norm_final)
    xi, yi, ci = _place()
    chip, me = 2 * xi + yi, 4 * xi + 2 * yi + ci
    seq = x.shape[1]

    conv_flat = jnp.concatenate([ssm_conv_w.reshape(-1), gdn_conv_w.reshape(-1)])
    n_conv = conv_flat.shape[0]
    first = _all_gather_small("gather_c_conv", _pad_rows(jnp.concatenate([c[0], conv_flat]), 16, D_MODEL))
    c_all = first[:, 0, :]
    conv_all = first[0::2].reshape(N_CHIP, -1)[:, D_MODEL:D_MODEL + n_conv]
    n_ssm = ssm_conv_w.size
    sp = dict(w)
    sp["ssm_conv_w"] = jnp.concatenate([conv_all[j, :n_ssm].reshape(ssm_conv_w.shape) for j in range(N_CHIP)], axis=2)
    sp["gdn_conv_w"] = jnp.concatenate([conv_all[j, n_ssm:].reshape(gdn_conv_w.shape) for j in range(N_CHIP)], axis=2)

    ada_cols = w_ada.shape[2]
    mod_part = _ada_fwd(c_all, w_ada, lax.dynamic_slice_in_dim(b_ada, chip * ada_cols, ada_cols, axis=1))
    mod_all = _all_gather_small("gather_mod", mod_part.reshape(DEPTH * 8, ada_cols))[0::2].reshape(N_CHIP, DEPTH, 8, ada_cols)
    mod = lax.dynamic_index_in_dim(mod_all, me, axis=2, keepdims=False).transpose(1, 0, 2).reshape(DEPTH, N_CHIP * ada_cols)
    lb = _lb_fwd(hgrn_lb_logits)

    gathered = _gather_weights(w, chip)
    wfull = []
    for l in range(DEPTH):
        full = {n: gathered[n][l] for n, _, _, _ in BIG}
        for nm, part in zip(("768", "xbc", "qkv", "gate", "small"), _regroup_w_in(full["w_in"])):
            full["win_" + nm] = part
        full["win_all"] = jnp.concatenate([full["win_" + nm] for nm in ("768", "xbc", "qkv", "gate", "small")], axis=1)
        wfull.append(full)

    loss8, d_x, d_nf, lg = _device_step(x[0], loss_target[0], mod, lb, wfull, sp)

    grad = _reduce_grads({n: jnp.stack([lg[l][n].astype(BF) for l in range(DEPTH)]) for n, _, _, _ in BIG}, chip, ci)

    dmod = jnp.stack([jnp.concatenate([lg[l][n] for n in ("sh1", "sc1", "g1", "sh2", "sc2", "g2")], axis=1)[0] for l in range(DEPTH)])
    d_lb = jnp.stack([lg[l]["lb"][0] for l in range(DEPTH)])
    contrib = {
        "norm_mix": jnp.stack([lg[l]["norm_mix"][0] for l in range(DEPTH)]),
        "norm_ffn": jnp.stack([lg[l]["norm_ffn"][0] for l in range(DEPTH)]),
        "b_merge": jnp.stack([lg[l]["b_merge"][0] for l in range(DEPTH)]),
        "hgrn_lb_logits": _lb_bwd(hgrn_lb_logits, d_lb),
        "hgrn_norm": jnp.stack([lg[l]["hgrn_norm"][0] for l in range(DEPTH)]),
        "ssm_conv_w": jnp.stack([lg[l]["ssm_conv_w"] for l in range(DEPTH)]),
        "ssm_conv_b": jnp.stack([lg[l]["ssm_conv_b"][0] for l in range(DEPTH)]),
        "ssm_dt_bias": jnp.stack([lg[l]["ssm_dt_bias"][0, DT_OFF:DT_OFF + 12] for l in range(DEPTH)]),
        "ssm_a_log": jnp.stack([lg[l]["ssm_a_log"][0, DT_OFF:DT_OFF + 12] for l in range(DEPTH)]),
        "ssm_d": jnp.stack([lg[l]["ssm_d"][0, DT_OFF:DT_OFF + 12] for l in range(DEPTH)]),
        "ssm_norm": jnp.stack([lg[l]["ssm_norm"][0] for l in range(DEPTH)]),
        "gdn_conv_w": jnp.stack([lg[l]["gdn_conv_w"] for l in range(DEPTH)]),
        "gdn_dt_bias": jnp.stack([lg[l]["gdn_dt_bias"][0, GA_OFF:GA_OFF + 6] for l in range(DEPTH)]),
        "gdn_a_log": jnp.stack([lg[l]["gdn_a_log"][0, GA_OFF:GA_OFF + 6] for l in range(DEPTH)]),
        "gdn_norm": jnp.stack([lg[l]["gdn_norm"][0] for l in range(DEPTH)]),
        "norm_final": d_nf[0],
    }
    flat = jnp.concatenate([dmod.reshape(-1)] + [contrib[n].reshape(-1) for n in SMALL_REPL] + [loss8[0, 0:1]])
    small_all = _all_gather_small("gather_small_grads", _pad_rows(flat, SMALL_ROWS, D_MODEL))
    total = _sum_leading("small_grad_sum", small_all, F32).reshape(-1)
    n_mod = dmod.size
    grad["b_ada"] = total[:n_mod].reshape(b_ada.shape)
    off = n_mod
    full_small = {}
    for n in SMALL_REPL:
        full_small[n] = total[off:off + contrib[n].size].reshape(contrib[n].shape)
        off += contrib[n].size
    loss = total[off]
    for n in SMALL_REPL:
        if n in ("ssm_conv_w", "gdn_conv_w"):
            cols = w[n].shape[2]
            grad[n] = lax.dynamic_slice_in_dim(full_small[n], chip * cols, cols, axis=2)
        else:
            grad[n] = full_small[n]
    dmod_cols = lax.dynamic_slice_in_dim(small_all[:, :n_mod // D_MODEL, :].reshape(8, DEPTH, -1), chip * ada_cols, ada_cols, axis=2)
    grad["w_ada"] = _ada_bwd(c_all, dmod_cols.transpose(1, 0, 2))

    delta, new_m, new_v = {}, {}, {}
    big_names = ("w_ada",) + tuple(n for n, _, _, _ in BIG)
    for n in big_names:
        delta[n], new_m[n], new_v[n] = _adamw("adamw_" + n, w[n], grad[n], m[n], v[n])
    small_names = [n for n in WEIGHTS if n not in big_names]
    packs = [_pad_rows(jnp.concatenate([d[n].reshape(-1) for n in small_names]), 584, LANES) for d in (w, grad, m, v)]
    outs = _ew("adamw_small", _adamw_fn, packs, [F32, F32, F32])
    off = 0
    for n in small_names:
        for dst, o in zip((delta, new_m, new_v), outs):
            dst[n] = o.reshape(-1)[off:off + w[n].size].reshape(w[n].shape)
        off += w[n].size
    return (loss, d_x[None], *[grad[n] for n in WEIGHTS], *[delta[n] for n in WEIGHTS], *[new_m[n] for n in WEIGHTS],
            *[new_v[n] for n in WEIGHTS])
```

```python
import functools

import jax
import jax.numpy as jnp
from jax import lax
from jax.experimental import pallas as pl
from jax.experimental.pallas import tpu as pltpu

F32 = jnp.float32
BF = jnp.bfloat16
HI = lax.Precision.HIGHEST

D_MODEL = 1024
DEPTH = 4
CHUNK = 64
MIX_W = 768
HEAD = 128
N_HEAD6 = 6
SSM_P = 64
SSM_N = 128
CONV_CH = 1280
QKV_W = 2304
FFN_H = 2816
IN_WIDTH = 11288
NORM_EPS = 1e-6
F_MIN = 1e-30
HALO = 8
HEAD_GROUP = 6
HGRN_SUB = 8
SMALL_W = 512
LANES = 128
DT_OFF, GB_OFF, GA_OFF = 0, 12, 18

ADAM_LR, ADAM_B1, ADAM_B2, ADAM_EPS, ADAM_WD, ADAM_STEP = 0.001, 0.9, 0.999, 1e-08, 0.01, 10

VMEM_LIMIT = 56 * 1024 * 1024
TOKEN_TILE = 256


def _pc(body, **kw):
    return pl.pallas_call(body, **kw)


def _cparams(sem):
    return pltpu.CompilerParams(dimension_semantics=sem, vmem_limit_bytes=VMEM_LIMIT)


def _bdot(a, b):
    return jnp.dot(a.astype(BF), b.astype(BF), preferred_element_type=F32)


def _bdot_nt(a, b):
    return lax.dot_general(a.astype(BF), b.astype(BF), (((1,), (1,)), ((), ())), preferred_element_type=F32)


def _bdot_tn(a, b):
    return lax.dot_general(a.astype(BF), b.astype(BF), (((0,), (0,)), ((), ())), preferred_element_type=F32)


def _silu(x):
    return x * jax.nn.sigmoid(x)


def _tri_mask(n, strict=False):
    t = lax.broadcasted_iota(jnp.int32, (n, n), 0)
    s = lax.broadcasted_iota(jnp.int32, (n, n), 1)
    return (s < t) if strict else (s <= t)


def _masked_exp(diff, mask):
    return jnp.where(mask, jnp.exp(jnp.where(mask, diff, 0.0)), 0.0)


def _split_bf16(x, n):
    parts, rest = [], x
    for _ in range(n):
        p = rest.astype(BF)
        parts.append(p)
        rest = rest - p.astype(F32)
    return parts


def _tri_sum(x, reverse):
    n, w = x.shape
    t = lax.broadcasted_iota(jnp.int32, (n, n), 0)
    s = lax.broadcasted_iota(jnp.int32, (n, n), 1)
    tri = jnp.where((s >= t) if reverse else (s <= t), 1.0, 0.0).astype(BF)
    y = jnp.dot(tri, jnp.concatenate(_split_bf16(x, 3), axis=1), preferred_element_type=F32)
    return y[:, :w] + y[:, w:2 * w] + y[:, 2 * w:]


@jax.custom_vjp
def _cumsum_rows(x):
    return _tri_sum(x, False)


_cumsum_rows.defvjp(lambda x: (_tri_sum(x, False), None), lambda _, g: (_tri_sum(g, True),))


def _dot_split(a, b, transpose_a=False):
    dims = (((0,), (0,)) if transpose_a else ((1,), (0,)), ((), ()))
    a_hi, a_lo = _split_bf16(a, 2)
    b_hi, b_lo = _split_bf16(b, 2)
    w = b.shape[1]
    y = lax.dot_general(a_hi, jnp.concatenate([b_hi, b_lo], axis=1), dims, preferred_element_type=F32)
    return y[:, :w] + y[:, w:] + lax.dot_general(a_lo, b_hi, dims, preferred_element_type=F32)


def _rms(x, w):
    return x * lax.rsqrt(jnp.mean(x * x, axis=-1, keepdims=True) + NORM_EPS) * w


def _causal_conv(halo, x, w):
    ext = jnp.concatenate([halo, x], axis=0)
    n = x.shape[0]
    acc = w[0:1, :] * ext[HALO - 3:HALO - 3 + n, :]
    for i in range(1, 4):
        acc = acc + w[i:i + 1, :] * ext[HALO - 3 + i:HALO - 3 + i + n, :]
    return acc


def _unit_lower_inverses(mats):
    n = mats[0].shape[0]
    t = lax.broadcasted_iota(jnp.int32, (n, n), 0)
    s_ = lax.broadcasted_iota(jnp.int32, (n, n), 1)
    xs = [jnp.where(t == s_, 1.0, 0.0).astype(F32) for _ in mats]
    for s in range(n - 1):
        r0 = 8 * ((s + 1) // 8)
        for i, a in enumerate(mats):
            x = xs[i]
            low = x[r0:] - a[r0:, s:s + 1] * x[s:s + 1, :]
            xs[i] = low if r0 == 0 else jnp.concatenate([x[:r0], low], axis=0)
    return xs


@jax.custom_vjp
def _unit_lower_solves(mats, rhss):
    return [_dot_split(inv, r) for inv, r in zip(_unit_lower_inverses(mats), rhss)]


def _uls_fwd(mats, rhss):
    invs = _unit_lower_inverses(mats)
    xs = [_dot_split(inv, r) for inv, r in zip(invs, rhss)]
    return xs, (invs, xs)


def _uls_bwd(res, gs):
    invs, xs = res
    ys = [_dot_split(inv, g, transpose_a=True) for inv, g in zip(invs, gs)]
    das = [jnp.where(_tri_mask(CHUNK, strict=True), -_bdot_nt(y, x), 0.0) for y, x in zip(ys, xs)]
    return das, ys


_unit_lower_solves.defvjp(_uls_fwd, _uls_bwd)


def _hgrn_chunk(tiles, halos, state, consts):
    q_raw, f_raw, v_all, g_raw = tiles
    lb, norm_w = consts
    q_all = _silu(q_raw)
    f = lb + (1.0 - lb) * jax.nn.sigmoid(f_raw)
    logf = jnp.log(jnp.maximum(f, F_MIN))
    k_all = (1.0 - lb) * jax.nn.sigmoid(-f_raw)
    b_all = _cumsum_rows(logf)
    sub = HGRN_SUB
    row = lax.broadcasted_iota(jnp.int32, (sub, 1), 0)
    src_row = lax.broadcasted_iota(jnp.int32, (CHUNK, 1), 0)
    src_lane = lax.broadcasted_iota(jnp.int32, (1, CHUNK), 1)
    heads = range(N_HEAD6)
    n_sub = CHUNK // sub
    cols = [slice(h * HEAD, (h + 1) * HEAD) for h in heads]
    qs, ks, vs, bs = ([a[:, sl] for sl in cols] for a in (q_all, k_all, v_all, b_all))
    o_inter = [_bdot_nt(qs[h] * jnp.exp(bs[h]), state[h]) for h in heads]
    blocks = [[None] * n_sub for _ in heads]
    for i in range(n_sub):
        r0 = i * sub
        for h in heads:
            if i > 0:
                ref = bs[h][r0 - 1:r0, :]
                blocks[h][i] = _bdot_nt(qs[h][r0:r0 + sub] * jnp.exp(bs[h][r0:r0 + sub] - ref),
                                        ks[h] * _masked_exp(ref - bs[h], src_row < r0))
            else:
                blocks[h][i] = jnp.zeros((sub, CHUNK), F32)
    for h in heads:
        for i in range(n_sub):
            r0 = i * sub
            qi, ki, bi = qs[h][r0:r0 + sub], ks[h][r0:r0 + sub], bs[h][r0:r0 + sub]
            for s in range(sub):
                e = _masked_exp(bi - bi[s:s + 1, :], row >= s)
                col = jnp.sum(qi * ki[s:s + 1, :] * e, axis=1, keepdims=True)
                blocks[h][i] = jnp.where(src_lane == r0 + s, col, blocks[h][i])
    os_ = [_bdot(jnp.concatenate(blocks[h], axis=0), vs[h]) + o_inter[h] for h in heads]
    ends = [bs[h][CHUNK - 1:CHUNK, :] for h in heads]
    new_states = [state[h] * jnp.exp(ends[h]) + _bdot_tn(vs[h], ks[h] * jnp.exp(ends[h] - bs[h])) for h in heads]
    outs = [_rms(os_[h], norm_w) * _silu(g_raw[:, cols[h]]) for h in heads]
    return (jnp.concatenate(outs, axis=1),), jnp.stack(new_states)


def _ssd_chunk(tiles, halos, state, consts):
    z, xbc_raw, small = tiles
    (halo,) = halos
    conv_w, conv_b, dt_bias, a_log, d_skip, norm_w = consts
    xbc = _silu(_causal_conv(halo, xbc_raw, conv_w) + conv_b)
    xs, bm, cm = xbc[:, :MIX_W], xbc[:, MIX_W:MIX_W + 2 * SSM_N], xbc[:, MIX_W + 2 * SSM_N:]
    dt = jax.nn.softplus(small + dt_bias)
    cum = _cumsum_rows(-jnp.exp(a_log) * dt)
    cum_t2 = jnp.concatenate([cum, cum], axis=0).T
    lane = lax.broadcasted_iota(jnp.int32, (1, LANES), 1)
    first = lane < SSM_P
    hm0 = jnp.where(first, 1.0, 0.0).astype(F32)
    hm1 = 1.0 - hm0
    src = jnp.where(first, lane, lane - SSM_P)
    tri2 = src <= lax.broadcasted_iota(jnp.int32, (CHUNK, 1), 0)
    pick = lambda a, b: jnp.where(first, a, b)
    bgs = [bm[:, g * SSM_N:(g + 1) * SSM_N] for g in range(2)]
    cgs = [cm[:, g * SSM_N:(g + 1) * SSM_N] for g in range(2)]
    gmats = [_bdot_nt(cgs[g], jnp.concatenate([bgs[g], bgs[g]], axis=0)) for g in range(2)]
    pairs = range(6)
    xps = [xs[:, p * LANES:(p + 1) * LANES] for p in pairs]
    c0s = [cum[:, 2 * p:2 * p + 1] for p in pairs]
    c1s = [cum[:, 2 * p + 1:2 * p + 2] for p in pairs]
    e0s = [cum[CHUNK - 1:CHUNK, 2 * p:2 * p + 1] for p in pairs]
    e1s = [cum[CHUNK - 1:CHUNK, 2 * p + 1:2 * p + 2] for p in pairs]
    segs = [_masked_exp(pick(c0s[p], c1s[p]) - pick(cum_t2[2 * p:2 * p + 1, :], cum_t2[2 * p + 1:2 * p + 2, :]), tri2)
            for p in pairs]
    vms = []
    for p in pairs:
        v = xps[p] * pick(dt[:, 2 * p:2 * p + 1], dt[:, 2 * p + 1:2 * p + 2])
        vms.append(jnp.concatenate([v * hm0, v * hm1], axis=0))
    y_intra = [_bdot(gmats[p // 3] * segs[p], vms[p]) for p in pairs]
    y_inter = [_bdot(jnp.concatenate([cgs[p // 3] * jnp.exp(c0s[p]), cgs[p // 3] * jnp.exp(c1s[p])], axis=1),
                     jnp.concatenate([state[p] * hm0, state[p] * hm1], axis=0)) for p in pairs]
    new_states = [_bdot_tn(jnp.concatenate([bgs[p // 3] * jnp.exp(e0s[p] - c0s[p]),
                                            bgs[p // 3] * jnp.exp(e1s[p] - c1s[p])], axis=0), vms[p])
                  + state[p] * pick(jnp.exp(e0s[p]), jnp.exp(e1s[p])) for p in pairs]
    ys = [y_intra[p] + y_inter[p] + pick(d_skip[:, 2 * p:2 * p + 1], d_skip[:, 2 * p + 1:2 * p + 2]) * xps[p] for p in pairs]
    y = jnp.concatenate(ys, axis=1) * _silu(z)
    gw = MIX_W // 2
    y = jnp.concatenate([_rms(y[:, g * gw:(g + 1) * gw], norm_w[:, g * gw:(g + 1) * gw]) for g in range(2)], axis=1)
    return (y,), jnp.stack(new_states)


def _gdn_chunk(tiles, halos, state, consts):
    qkv_raw, z, small = tiles
    (halo,) = halos
    conv_w, dt_bias, a_log, norm_w = consts
    qkv = _silu(_causal_conv(halo, qkv_raw, conv_w))
    beta_all = jax.nn.sigmoid(small)
    cum = _cumsum_rows(-jnp.exp(a_log) * jax.nn.softplus(small + dt_bias))
    cum_t = cum.T
    tri, tri_strict = _tri_mask(CHUNK), _tri_mask(CHUNK, strict=True)

    def group(hs):
        n = range(len(hs))
        qs, ks, betas, cs, ces, decays, rhss = [], [], [], [], [], [], []
        for h in hs:
            q = qkv[:, h * HEAD:(h + 1) * HEAD]
            k = qkv[:, MIX_W + h * HEAD:MIX_W + (h + 1) * HEAD]
            v = qkv[:, 2 * MIX_W + h * HEAD:2 * MIX_W + (h + 1) * HEAD]
            q = q * lax.rsqrt(jnp.sum(q * q, axis=-1, keepdims=True) + NORM_EPS) * (HEAD ** -0.5)
            k = k * lax.rsqrt(jnp.sum(k * k, axis=-1, keepdims=True) + NORM_EPS)
            beta = beta_all[:, GB_OFF + h:GB_OFF + h + 1]
            c, c_t = cum[:, GA_OFF + h:GA_OFF + h + 1], cum_t[GA_OFF + h:GA_OFF + h + 1, :]
            qs.append(q), ks.append(k), betas.append(beta), cs.append(c)
            ces.append(cum[CHUNK - 1:CHUNK, GA_OFF + h:GA_OFF + h + 1])
            decays.append(_masked_exp(c - c_t, tri))
            rhss.append(jnp.concatenate([v * beta, k * (beta * jnp.exp(c))], axis=1))
        sts = [state[h] for h in hs]
        qk_kks = [_bdot_nt(jnp.concatenate([qs[i], ks[i]], axis=0), ks[i]) for i in n]
        sols = _unit_lower_solves([jnp.where(tri_strict, betas[i] * qk_kks[i][CHUNK:] * decays[i], 0.0) for i in n], rhss)
        on_states = [_bdot(jnp.concatenate([sols[i][:, HEAD:], qs[i] * jnp.exp(cs[i])], axis=0), sts[i]) for i in n]
        us = [sols[i][:, :HEAD] - on_states[i][:CHUNK] for i in n]
        os_ = [on_states[i][CHUNK:] + _bdot(qk_kks[i][:CHUNK] * decays[i], us[i]) for i in n]
        new = [jnp.exp(ces[i]) * sts[i] + _bdot_tn(ks[i] * jnp.exp(ces[i] - cs[i]), us[i]) for i in n]
        outs = [_rms(os_[i], norm_w) * _silu(z[:, h * HEAD:(h + 1) * HEAD]) for i, h in enumerate(hs)]
        return outs, new

    outs, new_states = [], []
    for h0 in range(0, N_HEAD6, HEAD_GROUP):
        o, s = group(list(range(h0, h0 + HEAD_GROUP)))
        outs += o
        new_states += s
    return (jnp.concatenate(outs, axis=1),), jnp.stack(new_states)


def _scan_fwd(name, fn, tiled, halo_idx, consts, out_width, state_shape):
    seq = tiled[0][0].shape[0]
    nc = seq // CHUNK
    n_t, n_h, n_c = len(tiled), len(halo_idx), len(consts)

    def body(*refs):
        t_refs, h_refs, c_refs = refs[:n_t], refs[n_t:n_t + n_h], refs[n_t + n_h:n_t + n_h + n_c]
        y_ref, save_ref, st_ref = refs[n_t + n_h + n_c:]
        i = pl.program_id(0)

        @pl.when(i == 0)
        def _():
            st_ref[...] = jnp.zeros_like(st_ref)

        flag = jnp.where(i > 0, 1.0, 0.0).astype(F32)
        st = st_ref[...]
        (y,), new = fn([r[...] for r in t_refs], [r[...] * flag for r in h_refs], st, [r[...] for r in c_refs])
        save_ref[0] = st
        y_ref[...] = y.astype(y_ref.dtype)
        st_ref[...] = new

    in_specs = [pl.BlockSpec((CHUNK, w), functools.partial(lambda i, cb: (i, cb), cb=cb)) for _, w, cb in tiled]
    in_specs += [pl.BlockSpec((HALO, tiled[j][1]),
                              functools.partial(lambda i, cb: (jnp.maximum(i * (CHUNK // HALO) - 1, 0), cb), cb=tiled[j][2]))
                 for j in halo_idx]
    in_specs += [pl.BlockSpec(c.shape, functools.partial(lambda i, nd: (0,) * nd, nd=c.ndim)) for c in consts]
    zeros = (0,) * len(state_shape)
    return _pc(
        body, name=name, grid=(nc,), in_specs=in_specs,
        out_specs=(pl.BlockSpec((CHUNK, out_width), lambda i: (i, 0)),
                   pl.BlockSpec((1,) + state_shape, lambda i: (i,) + zeros)),
        out_shape=(jax.ShapeDtypeStruct((seq, out_width), BF), jax.ShapeDtypeStruct((nc,) + state_shape, F32)),
        scratch_shapes=[pltpu.VMEM(state_shape, F32)],
        compiler_params=_cparams(("arbitrary",)),
    )(*[t[0] for t in tiled], *[tiled[j][0] for j in halo_idx], *consts)


def _scan_bwd(name, fn, tiled, halo_idx, consts, saved, dy, dtile_dtypes, extra=None):
    seq = tiled[0][0].shape[0]
    nc = seq // CHUNK
    n_t, n_h, n_c = len(tiled), len(halo_idx), len(consts)
    state_shape = saved.shape[1:]
    n_x = 0 if extra is None else 1

    def body(*refs):
        t_refs, h_refs, c_refs = refs[:n_t], refs[n_t:n_t + n_h], refs[n_t + n_h:n_t + n_h + n_c]
        pos = n_t + n_h + n_c
        save_ref, dy_ref = refs[pos], refs[pos + 1]
        x_refs = refs[pos + 2:pos + 2 + n_x]
        pos += 2 + n_x
        dt_refs, dc_refs = refs[pos:pos + n_t], refs[pos + n_t:pos + n_t + n_c]
        dst_ref = refs[pos + n_t + n_c]
        carry_refs = refs[pos + n_t + n_c + 1:]
        i = pl.program_id(0)

        @pl.when(i == 0)
        def _():
            dst_ref[...] = jnp.zeros_like(dst_ref)
            for r in carry_refs:
                r[...] = jnp.zeros_like(r)
            for r in dc_refs:
                r[...] = jnp.zeros_like(r)

        flag = jnp.where(i < nc - 1, 1.0, 0.0).astype(F32)
        tiles = [r[...] for r in t_refs]
        halos = [r[...] * flag for r in h_refs]
        cvals = [r[...] for r in c_refs]
        _, vjp = jax.vjp(fn, tiles, halos, save_ref[0], cvals)
        d_tiles, d_halos, d_state, d_consts = vjp(((dy_ref[...].astype(F32),), dst_ref[...]))
        dst_ref[...] = d_state
        for r, g in zip(dc_refs, d_consts):
            r[...] += g
        for j, (r, g) in enumerate(zip(dt_refs, d_tiles)):
            if extra is not None and extra[0] == j:
                g = g + x_refs[0][...].astype(F32)
            r[...] = g.astype(r.dtype)
            if j in halo_idx:
                cr = carry_refs[halo_idx.index(j)]
                r[CHUNK - HALO:CHUNK, :] = (g[CHUNK - HALO:CHUNK, :] + cr[...]).astype(r.dtype)
                cr[...] = d_halos[halo_idx.index(j)] * flag

    rev = lambda i: nc - 1 - i
    in_specs = [pl.BlockSpec((CHUNK, w), functools.partial(lambda i, cb: (rev(i), cb), cb=cb)) for _, w, cb in tiled]
    in_specs += [pl.BlockSpec((HALO, tiled[j][1]),
                              functools.partial(lambda i, cb: (jnp.maximum(rev(i) * (CHUNK // HALO) - 1, 0), cb), cb=tiled[j][2]))
                 for j in halo_idx]
    in_specs += [pl.BlockSpec(c.shape, functools.partial(lambda i, nd: (0,) * nd, nd=c.ndim)) for c in consts]
    zeros = (0,) * len(state_shape)
    in_specs += [pl.BlockSpec((1,) + state_shape, lambda i: (rev(i),) + zeros),
                 pl.BlockSpec((CHUNK, dy.shape[1]), lambda i: (rev(i), 0))]
    args = [t[0] for t in tiled] + [tiled[j][0] for j in halo_idx] + list(consts) + [saved, dy]
    if extra is not None:
        in_specs.append(pl.BlockSpec((CHUNK, extra[1].shape[1]), lambda i: (rev(i), 0)))
        args.append(extra[1])
    out_specs = [pl.BlockSpec((CHUNK, w), lambda i: (rev(i), 0)) for _, w, _ in tiled]
    out_specs += [pl.BlockSpec(c.shape, functools.partial(lambda i, nd: (0,) * nd, nd=c.ndim)) for c in consts]
    out_shape = [jax.ShapeDtypeStruct((seq, w), dtd) for (_, w, _), dtd in zip(tiled, dtile_dtypes)]
    out_shape += [jax.ShapeDtypeStruct(c.shape, F32) for c in consts]
    scratch = [pltpu.VMEM(state_shape, F32)] + [pltpu.VMEM((HALO, tiled[j][1]), F32) for j in halo_idx]
    return _pc(body, name=name, grid=(nc,), in_specs=in_specs, out_specs=tuple(out_specs), out_shape=tuple(out_shape),
               scratch_shapes=scratch, compiler_params=_cparams(("arbitrary",)))(*args)


def _tile_fwd(name, fn, tiled, consts, outs, tm=TOKEN_TILE):
    seq = tiled[0][0].shape[0]
    n_t, n_c = len(tiled), len(consts)

    def body(*refs):
        res = fn(*[r[...] for r in refs[:n_t + n_c]])
        for r, y in zip(refs[n_t + n_c:], res):
            r[...] = y.astype(r.dtype)

    in_specs = [pl.BlockSpec((tm, w), functools.partial(lambda i, cb: (i, cb), cb=cb)) for _, w, cb in tiled]
    in_specs += [pl.BlockSpec(c.shape, functools.partial(lambda i, nd: (0,) * nd, nd=c.ndim)) for c in consts]
    return _pc(body, name=name, grid=(seq // tm,), in_specs=in_specs,
               out_specs=tuple(pl.BlockSpec((tm, w), lambda i: (i, 0)) for w, _ in outs),
               out_shape=tuple(jax.ShapeDtypeStruct((seq, w), dtp) for w, dtp in outs),
               compiler_params=_cparams(("arbitrary",)))(*[t[0] for t in tiled], *consts)


def _tile_bwd(name, fn, tiled, consts, douts, dtile_dtypes, add_to=None, tm=TOKEN_TILE):
    seq = tiled[0][0].shape[0]
    n_t, n_c, n_o = len(tiled), len(consts), len(douts)
    n_x = 0 if add_to is None else 1
    keep = [j for j, dtp in enumerate(dtile_dtypes) if dtp is not None]

    def body(*refs):
        vals = [r[...].astype(F32) for r in refs[:n_t + n_c]]
        pos = n_t + n_c
        g_refs, x_refs = refs[pos:pos + n_o], refs[pos + n_o:pos + n_o + n_x]
        pos += n_o + n_x
        dt_refs, dc_refs = refs[pos:pos + len(keep)], refs[pos + len(keep):]
        i = pl.program_id(0)

        @pl.when(i == 0)
        def _():
            for r in dc_refs:
                r[...] = jnp.zeros_like(r)

        _, vjp = jax.vjp(fn, *vals)
        cts = vjp(tuple(g[...].astype(F32) for g in g_refs))
        for r, j in zip(dt_refs, keep):
            g = cts[j]
            if add_to is not None and add_to[0] == j:
                g = g + x_refs[0][...].astype(F32)
            r[...] = g.astype(r.dtype)
        for r, g in zip(dc_refs, cts[n_t:]):
            r[...] += g

    in_specs = [pl.BlockSpec((tm, w), functools.partial(lambda i, cb: (i, cb), cb=cb)) for _, w, cb in tiled]
    in_specs += [pl.BlockSpec(c.shape, functools.partial(lambda i, nd: (0,) * nd, nd=c.ndim)) for c in consts]
    in_specs += [pl.BlockSpec((tm, g.shape[1]), lambda i: (i, 0)) for g in douts]
    args = [t[0] for t in tiled] + list(consts) + list(douts)
    if add_to is not None:
        in_specs.append(pl.BlockSpec((tm, add_to[1].shape[1]), lambda i: (i, 0)))
        args.append(add_to[1])
    out_specs = [pl.BlockSpec((tm, tiled[j][1]), lambda i: (i, 0)) for j in keep]
    out_specs += [pl.BlockSpec(c.shape, functools.partial(lambda i, nd: (0,) * nd, nd=c.ndim)) for c in consts]
    out_shape = [jax.ShapeDtypeStruct((seq, tiled[j][1]), dtile_dtypes[j]) for j in keep]
    out_shape += [jax.ShapeDtypeStruct(c.shape, F32) for c in consts]
    return _pc(body, name=name, grid=(seq // tm,), in_specs=in_specs, out_specs=tuple(out_specs),
               out_shape=tuple(out_shape), compiler_params=_cparams(("arbitrary",)))(*args)


def _lnmod_fn(x, nw, sc, sh):
    return (_rms(x, nw) * (1.0 + sc) + sh,)


def _resid_fn(x, o, g):
    return (x + (1.0 + g) * o,)


def _swiglu_fn(gu):
    return (_silu(gu[:, :FFN_H]) * gu[:, FFN_H:],)


def _merge_fn(yh, ys, yg, logits, wb, b_merge):
    gates = jax.nn.sigmoid(logits + b_merge)
    acc = None
    for n, y in enumerate((yh, ys, yg)):
        t = gates[:, n * D_MODEL:(n + 1) * D_MODEL] * _bdot(y, wb[n])
        acc = t if acc is None else acc + t
    return (acc,)


MM_VMEM_BUDGET = 40 * 1024 * 1024
MM_TILE_CAP = 1024
MM_K_CAP = 4096


def _divisor(n, cap, unit=LANES):
    best = None
    for d in range(unit, min(n, cap) + 1, unit):
        if n % d == 0:
            best = d
    return n if best is None else best


def _mm_tiles(m, n, k, out_bytes):
    tk = k if k <= MM_K_CAP else _divisor(k, 3072)
    tm, tn = _divisor(m, MM_TILE_CAP), _divisor(n, MM_TILE_CAP + MM_TILE_CAP // 2)

    def need(tm_, tn_):
        acc = tm_ * tn_ * 4 if tk < k else 0
        return 2 * 2 * tk * (tm_ + tn_) + acc + 2 * tm_ * tn_ * out_bytes

    while need(tm, tn) > MM_VMEM_BUDGET:
        if tn >= tm and _divisor(n, tn - LANES) < tn:
            tn = _divisor(n, tn - LANES)
        elif _divisor(m, tm - LANES) < tm:
            tm = _divisor(m, tm - LANES)
        else:
            break
    return tm, tn, tk


def _mm(a, b, mode, out_dtype, name):
    if mode == "nn":
        (m, k), n = a.shape, b.shape[1]
    elif mode == "nt":
        (m, k), n = a.shape, b.shape[0]
    else:
        (k, m), n = a.shape, b.shape[1]
    tm, tn, tk = _mm_tiles(m, n, k, jnp.dtype(out_dtype).itemsize)
    nk = k // tk
    dims = {"nn": ((1,), (0,)), "nt": ((1,), (1,)), "tn": ((0,), (0,))}[mode]

    def body_one(a_ref, b_ref, o_ref):
        o_ref[...] = lax.dot_general(a_ref[...], b_ref[...], (dims, ((), ())), preferred_element_type=F32).astype(o_ref.dtype)

    def body_acc(a_ref, b_ref, o_ref, acc_ref):
        kk = pl.program_id(2)

        @pl.when(kk == 0)
        def _():
            acc_ref[...] = jnp.zeros_like(acc_ref)

        acc_ref[...] += lax.dot_general(a_ref[...], b_ref[...], (dims, ((), ())), preferred_element_type=F32)

        @pl.when(kk == nk - 1)
        def _():
            o_ref[...] = acc_ref[...].astype(o_ref.dtype)

    a_spec = pl.BlockSpec((tk, tm), lambda i, j, kk: (kk, i)) if mode == "tn" else pl.BlockSpec((tm, tk), lambda i, j, kk: (i, kk))
    b_spec = pl.BlockSpec((tn, tk), lambda i, j, kk: (j, kk)) if mode == "nt" else pl.BlockSpec((tk, tn), lambda i, j, kk: (kk, j))
    return _pc(body_one if nk == 1 else body_acc, name=name, grid=(m // tm, n // tn, nk), in_specs=[a_spec, b_spec],
               out_specs=pl.BlockSpec((tm, tn), lambda i, j, kk: (i, j)),
               out_shape=jax.ShapeDtypeStruct((m, n), out_dtype),
               scratch_shapes=[] if nk == 1 else [pltpu.VMEM((tm, tn), F32)],
               compiler_params=_cparams(("parallel", "parallel", "arbitrary")))(a.astype(BF), b.astype(BF))


def _final_loss(x, tgt, norm_final, tm=TOKEN_TILE):
    seq = x.shape[0]

    def fn(xv, nf, tv):
        err = jnp.square(_rms(xv, nf) - tv)
        return 0.5 * jnp.sum(jnp.mean(err, axis=-1))

    def body(x_ref, t_ref, nf_ref, loss_ref, dx_ref, dnf_ref):
        i = pl.program_id(0)

        @pl.when(i == 0)
        def _():
            loss_ref[...] = jnp.zeros_like(loss_ref)
            dnf_ref[...] = jnp.zeros_like(dnf_ref)

        val, vjp = jax.vjp(functools.partial(fn, tv=t_ref[...]), x_ref[...], nf_ref[...])
        dx, dnf = vjp(jnp.ones((), F32))
        dx_ref[...] = dx
        dnf_ref[...] += dnf
        loss_ref[...] += jnp.broadcast_to(val, loss_ref.shape)

    return _pc(body, name="final_loss", grid=(seq // tm,),
               in_specs=[pl.BlockSpec((tm, D_MODEL), lambda i: (i, 0)), pl.BlockSpec((tm, D_MODEL), lambda i: (i, 0)),
                         pl.BlockSpec((1, D_MODEL), lambda i: (0, 0))],
               out_specs=(pl.BlockSpec((8, LANES), lambda i: (0, 0)), pl.BlockSpec((tm, D_MODEL), lambda i: (i, 0)),
                          pl.BlockSpec((1, D_MODEL), lambda i: (0, 0))),
               out_shape=(jax.ShapeDtypeStruct((8, LANES), F32), jax.ShapeDtypeStruct((seq, D_MODEL), F32),
                          jax.ShapeDtypeStruct((1, D_MODEL), F32)),
               compiler_params=_cparams(("arbitrary",)))(x, tgt, norm_final)


def _ada_fwd(c_all, w_ada, b_ada_cols):
    n_l, _, cols = w_ada.shape

    def body(c_ref, w_ref, b_ref, o_ref):
        o_ref[0] = jnp.dot(_silu(c_ref[...]), w_ref[0], preferred_element_type=F32, precision=HI) + b_ref[0]

    return _pc(body, name="ada_fwd", grid=(n_l,),
               in_specs=[pl.BlockSpec((8, D_MODEL), lambda l: (0, 0)), pl.BlockSpec((1, D_MODEL, cols), lambda l: (l, 0, 0)),
                         pl.BlockSpec((1, 1, cols), lambda l: (l, 0, 0))],
               out_specs=pl.BlockSpec((1, 8, cols), lambda l: (l, 0, 0)),
               out_shape=jax.ShapeDtypeStruct((n_l, 8, cols), F32),
               compiler_params=_cparams(("arbitrary",)))(c_all, w_ada, b_ada_cols.reshape(n_l, 1, cols))


def _ada_bwd(c_all, dmod_cols):
    n_l, _, cols = dmod_cols.shape

    def body(c_ref, g_ref, o_ref):
        o_ref[0] = lax.dot_general(_silu(c_ref[...]), g_ref[0], (((0,), (0,)), ((), ())), preferred_element_type=F32,
                                   precision=HI)

    return _pc(body, name="ada_bwd", grid=(n_l,),
               in_specs=[pl.BlockSpec((8, D_MODEL), lambda l: (0, 0)), pl.BlockSpec((1, 8, cols), lambda l: (l, 0, 0))],
               out_specs=pl.BlockSpec((1, D_MODEL, cols), lambda l: (l, 0, 0)),
               out_shape=jax.ShapeDtypeStruct((n_l, D_MODEL, cols), F32),
               compiler_params=_cparams(("arbitrary",)))(c_all, dmod_cols)


def _lb_fn(logits):
    e = jnp.exp(logits - jnp.max(logits, axis=0, keepdims=True))
    p = e / jnp.sum(e, axis=0, keepdims=True)
    r = lax.broadcasted_iota(jnp.int32, (DEPTH, 1), 0)
    lb = jnp.zeros_like(p)
    for j in range(1, DEPTH):
        lb = lb + jnp.where(r >= j, p[j:j + 1, :], 0.0)
    return lb


def _lb_fwd(logits):
    def body(l_ref, o_ref):
        o_ref[...] = _lb_fn(l_ref[...])

    return _pc(body, name="lb_fwd", out_shape=jax.ShapeDtypeStruct(logits.shape, F32))(logits)


def _lb_bwd(logits, dlb):
    def body(l_ref, g_ref, o_ref):
        _, vjp = jax.vjp(_lb_fn, l_ref[...])
        o_ref[...] = vjp(g_ref[...])[0]

    return _pc(body, name="lb_bwd", out_shape=jax.ShapeDtypeStruct(logits.shape, F32))(logits, dlb)


def _rows_for(n_rows, n_cols):
    r = 8
    while r * 2 <= n_rows and n_rows % (r * 2) == 0 and r * 2 * n_cols <= 256 * 1024:
        r *= 2
    return r if n_rows % r == 0 else n_rows


def _ew(name, fn, ins, out_dtypes):
    n_rows, n_cols = ins[0].shape
    tr = _rows_for(n_rows, n_cols)
    n_in = len(ins)

    def body(*refs):
        res = fn(*[r[...] for r in refs[:n_in]])
        for r, y in zip(refs[n_in:], res):
            r[...] = y.astype(r.dtype)

    spec = pl.BlockSpec((tr, n_cols), lambda i: (i, 0))
    return _pc(body, name=name, grid=(n_rows // tr,), in_specs=[spec] * n_in, out_specs=tuple([spec] * len(out_dtypes)),
               out_shape=tuple(jax.ShapeDtypeStruct((n_rows, n_cols), d) for d in out_dtypes),
               compiler_params=_cparams(("arbitrary",)))(*ins)


def _adamw_fn(w, g, m, v):
    m = ADAM_B1 * m + (1.0 - ADAM_B1) * g
    v = ADAM_B2 * v + (1.0 - ADAM_B2) * jnp.square(g)
    m_hat = m / (1.0 - ADAM_B1 ** ADAM_STEP)
    v_hat = v / (1.0 - ADAM_B2 ** ADAM_STEP)
    return -ADAM_LR * (m_hat / (jnp.sqrt(v_hat) + ADAM_EPS) + ADAM_WD * w), m, v


def _adamw(name, w, g, m, v):
    shape = w.shape
    two = (-1, shape[-1])
    d, nm, nv = _ew(name, _adamw_fn, [a.reshape(two) for a in (w, g, m, v)], [F32, F32, F32])
    return d.reshape(shape), nm.reshape(shape), nv.reshape(shape)


def _sum_leading(name, a, out_dtype):
    n, n_rows, n_cols = a.shape
    tr = _rows_for(n_rows, n_cols)

    def body(a_ref, o_ref):
        acc = a_ref[0].astype(F32)
        for j in range(1, n):
            acc = acc + a_ref[j].astype(F32)
        o_ref[...] = acc.astype(o_ref.dtype)

    return _pc(body, name=name, grid=(n_rows // tr,), in_specs=[pl.BlockSpec((n, tr, n_cols), lambda i: (0, i, 0))],
               out_specs=pl.BlockSpec((tr, n_cols), lambda i: (i, 0)),
               out_shape=jax.ShapeDtypeStruct((n_rows, n_cols), out_dtype),
               compiler_params=_cparams(("arbitrary",)))(a)


MESH = pl.DeviceIdType.MESH
ANY = pl.BlockSpec(memory_space=pl.ANY)


def _place():
    return lax.axis_index("x"), lax.axis_index("y"), lax.axis_index("c")


def _all_gather_small(name, a):
    m_per, n = a.shape

    def body(x_ref, out_ref, send_sems, recv_sems, local_sem):
        x, y, c = _place()
        me, sibling = (x, y, c), (x, y, 1 - c)
        chips = [(1 - x, y), (x, 1 - y), (1 - x, 1 - y)]

        def rows(px, py, pc):
            return out_ref.at[pl.ds((4 * px + 2 * py + pc) * m_per, m_per), :]

        def copy(k, block, to, src=None):
            return pltpu.make_async_remote_copy(src_ref=rows(*block) if src is None else src, dst_ref=rows(*block),
                                                send_sem=send_sems.at[k], recv_sem=recv_sems.at[k], device_id=to,
                                                device_id_type=MESH)

        mine = pltpu.make_async_copy(x_ref, rows(*me), local_sem)
        mine.start()
        first = [copy(0, me, sibling, src=x_ref)]
        first += [copy(1 + j, me, (*chip, c), src=x_ref) for j, chip in enumerate(chips)]
        for cp in first:
            cp.start()
        passed = [copy(4 + j, (*chip, c), sibling) for j, chip in enumerate(chips)]
        for j, chip in enumerate(chips):
            copy(1 + j, (*chip, c), me).wait_recv()
            passed[j].start()
        copy(0, sibling, me).wait_recv()
        for j, chip in enumerate(chips):
            copy(4 + j, (*chip, 1 - c), me).wait_recv()
        for cp in first + passed:
            cp.wait_send()
        mine.wait()

    out = _pc(body, name=name, out_shape=jax.ShapeDtypeStruct((8 * m_per, n), a.dtype),
              in_specs=[pl.BlockSpec(memory_space=pltpu.VMEM)], out_specs=pl.BlockSpec(memory_space=pltpu.VMEM),
              scratch_shapes=[pltpu.SemaphoreType.DMA((7,)), pltpu.SemaphoreType.DMA((7,)), pltpu.SemaphoreType.DMA],
              compiler_params=pltpu.CompilerParams(vmem_limit_bytes=VMEM_LIMIT))(a)
    return out.reshape(8, m_per, n)


def _chip_gather(name, pack):
    n_l, n_r, n_c = pack.shape
    half = n_r // 2

    def body(p_ref, o_ref, send_sems, recv_sems):
        x, y, c = _place()
        sibling = (x, y, 1 - c)
        chips = [(1 - x, y), (x, 1 - y), (1 - x, 1 - y)]

        def slab(px, py, pc):
            return o_ref.at[2 * px + py, :, pl.ds(pc * half, half), :]

        def copy(k, src, dst, to):
            return pltpu.make_async_remote_copy(src_ref=src, dst_ref=dst, send_sem=send_sems.at[k], recv_sem=recv_sems.at[k],
                                                device_id=to, device_id_type=MESH)

        first = [copy(j, p_ref.at[:, pl.ds(c * half, half), :], slab(x, y, c), (*chip, c)) for j, chip in enumerate(chips)]
        for cp in first:
            cp.start()
        passed = [copy(3 + j, slab(*chip, c), slab(*chip, c), sibling) for j, chip in enumerate(chips)]
        for j, chip in enumerate(chips):
            copy(j, slab(*chip, c), slab(*chip, c), (*chip, c)).wait_recv()
            passed[j].start()
        for j, chip in enumerate(chips):
            copy(3 + j, slab(*chip, 1 - c), slab(*chip, 1 - c), sibling).wait_recv()
        for cp in first + passed:
            cp.wait_send()

    return _pc(body, name=name, out_shape=jax.ShapeDtypeStruct((4, n_l, n_r, n_c), pack.dtype), in_specs=[ANY], out_specs=ANY,
               scratch_shapes=[pltpu.SemaphoreType.DMA((6,)), pltpu.SemaphoreType.DMA((6,))])(pack)


def _pair_swap(name, give):
    def body(g_ref, o_ref, send_sem, recv_sem):
        x, y, c = _place()
        cp = pltpu.make_async_remote_copy(src_ref=g_ref, dst_ref=o_ref, send_sem=send_sem, recv_sem=recv_sem,
                                          device_id=(x, y, 1 - c), device_id_type=MESH)
        cp.start()
        cp.wait()

    return _pc(body, name=name, out_shape=jax.ShapeDtypeStruct(give.shape, give.dtype), in_specs=[ANY], out_specs=ANY,
               scratch_shapes=[pltpu.SemaphoreType.DMA, pltpu.SemaphoreType.DMA])(give)


def _chip_exchange(name, parts):
    def body(p_ref, o_ref, send_sems, recv_sems):
        x, y, c = _place()
        me = 2 * x + y
        chips = [(1 - x, y), (x, 1 - y), (1 - x, 1 - y)]

        def copy(k, src, dst, to):
            return pltpu.make_async_remote_copy(src_ref=src, dst_ref=dst, send_sem=send_sems.at[k], recv_sem=recv_sems.at[k],
                                                device_id=to, device_id_type=MESH)

        sends = [copy(j, p_ref.at[2 * px + py], o_ref.at[me], (px, py, c)) for j, (px, py) in enumerate(chips)]
        for cp in sends:
            cp.start()
        for j, (px, py) in enumerate(chips):
            copy(j, p_ref.at[2 * px + py], o_ref.at[2 * px + py], (px, py, c)).wait_recv()
        for cp in sends:
            cp.wait_send()

    return _pc(body, name=name, out_shape=jax.ShapeDtypeStruct(parts.shape, parts.dtype), in_specs=[ANY], out_specs=ANY,
               scratch_shapes=[pltpu.SemaphoreType.DMA((3,)), pltpu.SemaphoreType.DMA((3,))])(parts)


N_CHIP = 4
BIG = (("w_in", (1024, 2822), 1, (1024, 2822)), ("w_branch", (3, 768, 256), 2, (2304, 256)),
       ("w_out", (256, 1024), 0, (256, 1024)), ("w_ffn_in", (1024, 1408), 1, (1024, 1408)),
       ("w_ffn_out", (704, 1024), 0, (704, 1024)))
G768 = ((0, 3072), (3072, 3840), (7436, 8204))
GXBC, GQKV, GGATE = (3840, 5120), (5132, 7436), (8216, 11288)
GSMALL = ((5120, 5132), (8204, 8210), (8210, 8216))
W768, WXBC, WGATE = 4608, CONV_CH, 3 * D_MODEL
IN_PAD = W768 + WXBC + QKV_W + WGATE + SMALL_W


def _join_shards(slabs, axis, shard_shape):
    n_l = slabs.shape[1]
    parts = [slabs[j].reshape((n_l,) + shard_shape) for j in range(N_CHIP)]
    return jnp.concatenate(parts, axis=axis + 1)


def _split_shards(full, axis, rows_cols):
    n_l = full.shape[0]
    size = full.shape[axis + 1] // N_CHIP
    return jnp.stack([lax.slice_in_dim(full, j * size, (j + 1) * size, axis=axis + 1).reshape((n_l,) + rows_cols)
                      for j in range(N_CHIP)])


def _gather_weights(w, chip, big=BIG):
    out = {}
    for n, shape, ax, rc in big:
        n_l = w[n].shape[0]
        mine = w[n].astype(BF).reshape((n_l,) + rc)
        slabs = lax.dynamic_update_slice(_chip_gather("gather_" + n, mine), mine[None], (chip, 0, 0, 0))
        out[n] = _join_shards(slabs, ax, shape)
    return out


def _pair_stage(full_grads, core, big=BIG):
    out = {}
    for n, _, ax, (rows, cols) in big:
        n_l = full_grads[n].shape[0]
        slabs = _split_shards(full_grads[n], ax, (rows, cols))
        half = rows // 2
        keep = lax.dynamic_slice_in_dim(slabs, core * half, half, axis=2).reshape(-1, cols)
        give = lax.dynamic_slice_in_dim(slabs, (1 - core) * half, half, axis=2).reshape(-1, cols)
        got = _pair_swap("pair_swap_" + n, give)
        (pair_sum,) = _ew("pair_sum_" + n, lambda a, b: (a.astype(F32) + b.astype(F32),), [keep, got], [BF])
        out[n] = pair_sum.reshape(N_CHIP, n_l * half, cols)
    return out


def _own_slab(landed, pair_sum, chip):
    return lax.dynamic_update_slice(landed, lax.dynamic_slice_in_dim(pair_sum, chip, 1, axis=0), (chip, 0, 0))


def _finish_reduce(parts, n_l, core, big=BIG):
    out = {}
    for n, shape, _, (rows, cols) in big:
        half = rows // 2
        mine = _sum_leading("chip_sum_" + n, parts[n], F32).reshape(n_l, half, cols)
        theirs = _pair_swap("pair_share_" + n, mine)
        full = jnp.concatenate([jnp.where(core == 0, mine, theirs), jnp.where(core == 0, theirs, mine)], axis=1)
        out[n] = full.reshape((n_l,) + shape)
    return out


def _reduce_grads(full_grads, chip, core, big=BIG):
    pair_sums = _pair_stage(full_grads, core, big)
    parts = {n: _own_slab(_chip_exchange("chip_exchange_" + n, pair_sums[n]), pair_sums[n], chip) for n, _, _, _ in big}
    return _finish_reduce(parts, full_grads[big[0][0]].shape[0], core, big)


HBM_SPEC = pl.BlockSpec(memory_space=pltpu.HBM)
SEM_SPEC = pl.BlockSpec(memory_space=pltpu.SEMAPHORE)
DATAFLOW = pltpu.SideEffectType.DATAFLOW_SIDE_EFFECTING


def _exchange_copies(p_ref, land_ref, sems, waiting):
    x, y, c = _place()
    me = 2 * x + y
    out = []
    for j, (px, py) in enumerate([(1 - x, y), (x, 1 - y), (1 - x, 1 - y)]):
        out.append(pltpu.make_async_remote_copy(src_ref=p_ref.at[2 * px + py], dst_ref=land_ref.at[2 * px + py if waiting else me],
                                                send_sem=sems[j], recv_sem=sems[3 + j], device_id=(px, py, c),
                                                device_id_type=MESH))
    return out


def _exchange_start(name, parts):
    def body(p_ref, land_ref, s0, s1, s2, r0, r1, r2, p_thru, land_thru, token):
        for cp in _exchange_copies(p_ref, land_ref, (s0, s1, s2, r0, r1, r2), waiting=False):
            cp.start()
        token[...] = jnp.zeros_like(token)

    res = _pc(body, name=name,
              out_shape=(pltpu.SemaphoreType.DMA(()),) * 6 + (pltpu.HBM(parts.shape, parts.dtype), pltpu.HBM(parts.shape, parts.dtype),
                                                            jax.ShapeDtypeStruct((8, LANES), F32)),
              in_specs=(HBM_SPEC, HBM_SPEC), out_specs=(SEM_SPEC,) * 6 + (HBM_SPEC, HBM_SPEC, pl.BlockSpec(memory_space=pltpu.VMEM)),
              input_output_aliases={0: 6, 1: 7}, compiler_params=pltpu.CompilerParams(has_side_effects=DATAFLOW))(
        pltpu.with_memory_space_constraint(parts, pltpu.HBM),
        pltpu.with_memory_space_constraint(lax.empty(parts.shape, parts.dtype), pltpu.HBM))
    return res[:6], res[6], res[7], res[8]


def _exchange_wait(name, sems, p_thru, land_thru, after):
    def body(p_ref, land_ref, s0, s1, s2, r0, r1, r2, after_ref, p_dead, got_ref):
        for cp in _exchange_copies(p_ref, land_ref, (s0, s1, s2, r0, r1, r2), waiting=True):
            cp.wait_send()
            cp.wait_recv()

    return _pc(body, name=name, out_shape=(pltpu.HBM(p_thru.shape, p_thru.dtype), pltpu.HBM(p_thru.shape, p_thru.dtype)),
               in_specs=(HBM_SPEC, HBM_SPEC) + (SEM_SPEC,) * 6 + (ANY,), out_specs=(HBM_SPEC, HBM_SPEC),
               input_output_aliases={0: 0, 1: 1}, compiler_params=pltpu.CompilerParams(has_side_effects=DATAFLOW))(
        p_thru, land_thru, *sems, after)[1]


def _regroup_w_in(w):
    cat = lambda spans: jnp.concatenate([w[:, a:b] for a, b in spans], axis=1)
    small = jnp.concatenate([cat(GSMALL), jnp.zeros((w.shape[0], SMALL_W - 24), w.dtype)], axis=1)
    return cat(G768), cat((GXBC,)), cat((GQKV,)), cat((GGATE,)), small


def _ungroup_w_in(d):
    o_xbc, o_qkv, o_gate, o_small = W768, W768 + WXBC, W768 + WXBC + QKV_W, W768 + WXBC + QKV_W + WGATE
    spans = ((0, 3072), (3072, 3840), (o_xbc, o_xbc + WXBC), (o_small, o_small + 12), (o_qkv, o_qkv + QKV_W),
             (3840, 4608), (o_small + 12, o_small + 18), (o_small + 18, o_small + 24), (o_gate, o_gate + WGATE))
    return jnp.concatenate([d[:, a:b] for a, b in spans], axis=1)


def _lane_pad(v, off):
    return jnp.pad(v, (off, LANES - off - v.shape[0]))[None, :]


STATE6 = (N_HEAD6, HEAD, HEAD)


def _mixer_inputs(sv, lp):
    p768, pxbc, pqkv, psmall = sv["p768"], sv["pxbc"], sv["pqkv"], sv["psmall"]
    hgrn = ([(p768, MIX_W, j) for j in range(4)], [], [lp["lb"], lp["hgrn_norm"]])
    ssd = ([(p768, MIX_W, 4), (pxbc, CONV_CH, 0), (psmall, LANES, 0)], [1],
           [lp["ssm_conv_w"], lp["ssm_conv_b"], lp["ssm_dt_bias"], lp["ssm_a_log"], lp["ssm_d"], lp["ssm_norm"]])
    gdn = ([(pqkv, QKV_W, 0), (p768, MIX_W, 5), (psmall, LANES, 0)], [0],
           [lp["gdn_conv_w"], lp["gdn_dt_bias"], lp["gdn_a_log"], lp["gdn_norm"]])
    return hgrn, ssd, gdn


def _layer_fwd(x, md, lw, lp):
    sv = {"x": x}
    (sv["h1"],) = _tile_fwd("lnmod1", _lnmod_fn, [(x, D_MODEL, 0)], [lp["norm_mix"], md["sc1"], md["sh1"]], [(D_MODEL, BF)])
    for nm in ("768", "xbc", "qkv", "gate", "small"):
        sv["p" + nm] = _mm(sv["h1"], lw["win_" + nm], "nn", F32, "proj_" + nm)
    hgrn, ssd, gdn = _mixer_inputs(sv, lp)
    sv["y_h"], sv["st_h"] = _scan_fwd("hgrn_fwd", _hgrn_chunk, *hgrn, MIX_W, STATE6)
    sv["y_s"], sv["st_s"] = _scan_fwd("ssd_fwd", _ssd_chunk, *ssd, MIX_W, STATE6)
    sv["y_g"], sv["st_g"] = _scan_fwd("gdn_fwd", _gdn_chunk, *gdn, MIX_W, STATE6)
    (sv["merged"],) = _tile_fwd("merge", _merge_fn, _merge_tiles(sv), [lw["w_branch"], lp["b_merge"]], [(D_MODEL, BF)])
    sv["out"] = _mm(sv["merged"], lw["w_out"], "nn", F32, "out_proj")
    (sv["x_mid"],) = _tile_fwd("resid1", _resid_fn, [(x, D_MODEL, 0), (sv["out"], D_MODEL, 0)], [md["g1"]], [(D_MODEL, F32)])
    (sv["h2"],) = _tile_fwd("lnmod2", _lnmod_fn, [(sv["x_mid"], D_MODEL, 0)], [lp["norm_ffn"], md["sc2"], md["sh2"]],
                            [(D_MODEL, BF)])
    sv["gu"] = _mm(sv["h2"], lw["w_ffn_in"], "nn", F32, "ffn_in")
    (sv["act"],) = _tile_fwd("swiglu", _swiglu_fn, [(sv["gu"], 2 * FFN_H, 0)], [], [(FFN_H, BF)])
    sv["o2"] = _mm(sv["act"], lw["w_ffn_out"], "nn", F32, "ffn_out")
    (x_out,) = _tile_fwd("resid2", _resid_fn, [(sv["x_mid"], D_MODEL, 0), (sv["o2"], D_MODEL, 0)], [md["g2"]], [(D_MODEL, F32)])
    return x_out, sv


def _merge_tiles(sv):
    return [(sv["y_h"], MIX_W, 0), (sv["y_s"], MIX_W, 0), (sv["y_g"], MIX_W, 0), (sv["pgate"], WGATE, 0)]


def _layer_bwd(dx_out, sv, md, lw, lp):
    g = {}
    x, x_mid = sv["x"], sv["x_mid"]
    d_xmid, d_o2, g["g2"] = _tile_bwd("resid2_b", _resid_fn, [(x_mid, D_MODEL, 0), (sv["o2"], D_MODEL, 0)], [md["g2"]], [dx_out],
                                      [F32, BF])
    d_act = _mm(d_o2, lw["w_ffn_out"], "nt", F32, "ffn_out_dx")
    g["w_ffn_out"] = _mm(sv["act"], d_o2, "tn", BF, "ffn_out_dw")
    (d_gu,) = _tile_bwd("swiglu_b", _swiglu_fn, [(sv["gu"], 2 * FFN_H, 0)], [], [d_act], [BF])
    d_h2 = _mm(d_gu, lw["w_ffn_in"], "nt", F32, "ffn_in_dx")
    g["w_ffn_in"] = _mm(sv["h2"], d_gu, "tn", BF, "ffn_in_dw")
    d_xmid, g["norm_ffn"], g["sc2"], g["sh2"] = _tile_bwd(
        "lnmod2_b", _lnmod_fn, [(x_mid, D_MODEL, 0)], [lp["norm_ffn"], md["sc2"], md["sh2"]], [d_h2], [F32], add_to=(0, d_xmid))
    d_x, d_out, g["g1"] = _tile_bwd("resid1_b", _resid_fn, [(x, D_MODEL, 0), (sv["out"], D_MODEL, 0)], [md["g1"]], [d_xmid],
                                    [F32, BF])
    d_merged = _mm(d_out, lw["w_out"], "nt", F32, "out_proj_dx")
    g["w_out"] = _mm(sv["merged"], d_out, "tn", BF, "out_proj_dw")
    d_yh, d_ys, d_yg, d_gate, g["w_branch"], g["b_merge"] = _tile_bwd(
        "merge_b", _merge_fn, _merge_tiles(sv), [lw["w_branch"], lp["b_merge"]], [d_merged], [F32, F32, F32, BF])
    hgrn, ssd, gdn = _mixer_inputs(sv, lp)
    d_q, d_f, d_v, d_g, g["lb"], g["hgrn_norm"] = _scan_bwd("hgrn_bwd", _hgrn_chunk, *hgrn, sv["st_h"], d_yh, [BF] * 4)
    (d_sz, d_xbc, d_small, g["ssm_conv_w"], g["ssm_conv_b"], g["ssm_dt_bias"], g["ssm_a_log"], g["ssm_d"],
     g["ssm_norm"]) = _scan_bwd("ssd_bwd", _ssd_chunk, *ssd, sv["st_s"], d_ys, [BF, BF, F32])
    d_qkv, d_gz, d_small, g["gdn_conv_w"], g["gdn_dt_bias"], g["gdn_a_log"], g["gdn_norm"] = _scan_bwd(
        "gdn_bwd", _gdn_chunk, *gdn, sv["st_g"], d_yg, [BF, BF, BF], extra=(2, d_small))
    d_proj = jnp.concatenate([d_q, d_f, d_v, d_g, d_sz, d_gz, d_xbc, d_qkv, d_gate, d_small,
                              jnp.zeros((x.shape[0], SMALL_W - LANES), BF)], axis=1)
    d_h1 = _mm(d_proj, lw["win_all"], "nt", F32, "proj_dx")
    g["w_in"] = _ungroup_w_in(_mm(sv["h1"], d_proj, "tn", BF, "proj_dw"))
    d_x, g["norm_mix"], g["sc1"], g["sh1"] = _tile_bwd(
        "lnmod1_b", _lnmod_fn, [(x, D_MODEL, 0)], [lp["norm_mix"], md["sc1"], md["sh1"]], [d_h1], [F32], add_to=(0, d_x))
    return d_x, g


SMALL_REPL = ("norm_mix", "norm_ffn", "b_merge", "hgrn_lb_logits", "hgrn_norm", "ssm_conv_w", "ssm_conv_b", "ssm_dt_bias",
              "ssm_a_log", "ssm_d", "ssm_norm", "gdn_conv_w", "gdn_dt_bias", "gdn_a_log", "gdn_norm", "norm_final")
WEIGHTS = ("w_ada", "b_ada", "norm_mix", "norm_ffn", "w_in", "b_merge", "hgrn_lb_logits", "hgrn_norm", "ssm_conv_w",
           "ssm_conv_b", "ssm_dt_bias", "ssm_a_log", "ssm_d", "ssm_norm", "gdn_conv_w", "gdn_dt_bias", "gdn_a_log",
           "gdn_norm", "w_branch", "w_out", "w_ffn_in", "w_ffn_out", "norm_final")
SMALL_ROWS = 120


def _pad_rows(flat, n_rows, n_cols):
    return jnp.concatenate([flat, jnp.zeros((n_rows * n_cols - flat.shape[0],), flat.dtype)]).reshape(n_rows, n_cols)


def _device_step(x, tgt, mod, lb, wfull, sp, chip=None, core=None):
    mds, lps, svs = [], [], []
    h = x
    for l in range(DEPTH):
        md = {n: mod[l, i * D_MODEL:(i + 1) * D_MODEL][None, :] for i, n in enumerate(("sh1", "sc1", "g1", "sh2", "sc2", "g2"))}
        lp = {n: sp[n][l][None, :] for n in ("norm_mix", "norm_ffn", "b_merge", "hgrn_norm", "ssm_conv_b", "ssm_norm", "gdn_norm")}
        lp["lb"] = lb[l][None, :]
        lp["ssm_conv_w"], lp["gdn_conv_w"] = sp["ssm_conv_w"][l], sp["gdn_conv_w"][l]
        for n in ("ssm_dt_bias", "ssm_a_log", "ssm_d"):
            lp[n] = _lane_pad(sp[n][l], DT_OFF)
        for n in ("gdn_dt_bias", "gdn_a_log"):
            lp[n] = _lane_pad(sp[n][l], GA_OFF)
        h, sv = _layer_fwd(h, md, wfull[l], lp)
        mds.append(md), lps.append(lp), svs.append(sv)
    loss, dh, d_nf = _final_loss(h, tgt, sp["norm_final"][None, :])
    grads = [None] * DEPTH
    if core is None:
        for l in reversed(range(DEPTH)):
            dh, grads[l] = _layer_bwd(dh, svs[l], mds[l], wfull[l], lps[l])
        return loss, dh, d_nf, grads
    names = [n for n, _, _, _ in BIG]
    landed, flying = [None] * DEPTH, None
    for l in reversed(range(DEPTH)):
        md = mds[l]
        if flying is not None:
            md = dict(md, g2=md["g2"] + sum(tok[0, 0] for _, _, _, tok in flying.values()))
        dh, grads[l] = _layer_bwd(dh, svs[l], md, wfull[l], lps[l])
        if flying is not None:
            landed[l + 1] = {n: _own_slab(_exchange_wait(f"exchange_wait_{n}_{l + 1}", *flying[n][:3], dh), sums[n], chip)
                             for n in names}
        sums = _pair_stage({n: grads[l][n].astype(BF)[None] for n in names}, core)
        if l > 0:
            flying = {n: _exchange_start(f"exchange_start_{n}_{l}", sums[n]) for n in names}
        else:
            landed[0] = {n: _own_slab(_chip_exchange("chip_exchange_" + n, sums[n]), sums[n], chip) for n in names}
    parts = {n: jnp.concatenate([landed[l][n] for l in range(DEPTH)], axis=1) for n in names}
    return loss, dh, d_nf, grads, _finish_reduce(parts, DEPTH, core)


def kernel(x, c, w_ada, b_ada, norm_mix, norm_ffn, w_in, b_merge, hgrn_lb_logits, hgrn_norm, ssm_conv_w, ssm_conv_b, ssm_dt_bias, ssm_a_log, ssm_d, ssm_norm, gdn_conv_w, gdn_dt_bias, gdn_a_log, gdn_norm, w_branch, w_out, w_ffn_in, w_ffn_out, norm_final, loss_target, m_w_ada, m_b_ada, m_norm_mix, m_norm_ffn, m_w_in, m_b_merge, m_hgrn_lb_logits, m_hgrn_norm, m_ssm_conv_w, m_ssm_conv_b, m_ssm_dt_bias, m_ssm_a_log, m_ssm_d, m_ssm_norm, m_gdn_conv_w, m_gdn_dt_bias, m_gdn_a_log, m_gdn_norm, m_w_branch, m_w_out, m_w_ffn_in, m_w_ffn_out, m_norm_final, v_w_ada, v_b_ada, v_norm_mix, v_norm_ffn, v_w_in, v_b_merge, v_hgrn_lb_logits, v_hgrn_norm, v_ssm_conv_w, v_ssm_conv_b, v_ssm_dt_bias, v_ssm_a_log, v_ssm_d, v_ssm_norm, v_gdn_conv_w, v_gdn_dt_bias, v_gdn_a_log, v_gdn_norm, v_w_branch, v_w_out, v_w_ffn_in, v_w_ffn_out, v_norm_final):
    w = dict(w_ada=w_ada, b_ada=b_ada, norm_mix=norm_mix, norm_ffn=norm_ffn, w_in=w_in, b_merge=b_merge,
             hgrn_lb_logits=hgrn_lb_logits, hgrn_norm=hgrn_norm, ssm_conv_w=ssm_conv_w, ssm_conv_b=ssm_conv_b,
             ssm_dt_bias=ssm_dt_bias, ssm_a_log=ssm_a_log, ssm_d=ssm_d, ssm_norm=ssm_norm, gdn_conv_w=gdn_conv_w,
             gdn_dt_bias=gdn_dt_bias, gdn_a_log=gdn_a_log, gdn_norm=gdn_norm, w_branch=w_branch, w_out=w_out,
             w_ffn_in=w_ffn_in, w_ffn_out=w_ffn_out, norm_final=norm_final)
    m = dict(w_ada=m_w_ada, b_ada=m_b_ada, norm_mix=m_norm_mix, norm_ffn=m_norm_ffn, w_in=m_w_in, b_merge=m_b_merge,
             hgrn_lb_logits=m_hgrn_lb_logits, hgrn_norm=m_hgrn_norm, ssm_conv_w=m_ssm_conv_w, ssm_conv_b=m_ssm_conv_b,
             ssm_dt_bias=m_ssm_dt_bias, ssm_a_log=m_ssm_a_log, ssm_d=m_ssm_d, ssm_norm=m_ssm_norm, gdn_conv_w=m_gdn_conv_w,
             gdn_dt_bias=m_gdn_dt_bias, gdn_a_log=m_gdn_a_log, gdn_norm=m_gdn_norm, w_branch=m_w_branch, w_out=m_w_out,
             w_ffn_in=m_w_ffn_in, w_ffn_out=m_w_ffn_out, norm_final=m_norm_final)
    v = dict(w_ada=v_w_ada, b_ada=v_b_ada, norm_mix=v_norm_mix, norm_ffn=v_norm_ffn, w_in=v_w_in, b_merge=v_b_merge,
             hgrn_lb_logits=v_hgrn_lb_logits, hgrn_norm=v_hgrn_norm, ssm_conv_w=v_ssm_conv_w, ssm_conv_b=v_ssm_conv_b,
             ssm_dt_bias=v_ssm_dt_bias, ssm_a_log=v_ssm_a_log, ssm_d=v_ssm_d, ssm_norm=v_ssm_norm, gdn_conv_w=v_gdn_conv_w,
             gdn_dt_bias=v_gdn_dt_bias, gdn_a_log=v_gdn_a_log, gdn_norm=v_gdn_norm, w_branch=v_w_branch, w_out=v_w_out,
             w_ffn_in=v_w_ffn_in, w_ffn_out=v_w_ffn_out, norm_final=v_norm_final)
    xi, yi, ci = _place()
    chip, me = 2 * xi + yi, 4 * xi + 2 * yi + ci
    seq = x.shape[1]

    conv_flat = jnp.concatenate([ssm_conv_w.reshape(-1), gdn_conv_w.reshape(-1)])
    n_conv = conv_flat.shape[0]
    first = _all_gather_small("gather_c_conv", _pad_rows(jnp.concatenate([c[0], conv_flat]), 16, D_MODEL))
    c_all = first[:, 0, :]
    conv_all = first[0::2].reshape(N_CHIP, -1)[:, D_MODEL:D_MODEL + n_conv]
    n_ssm = ssm_conv_w.size
    sp = dict(w)
    sp["ssm_conv_w"] = jnp.concatenate([conv_all[j, :n_ssm].reshape(ssm_conv_w.shape) for j in range(N_CHIP)], axis=2)
    sp["gdn_conv_w"] = jnp.concatenate([conv_all[j, n_ssm:].reshape(gdn_conv_w.shape) for j in range(N_CHIP)], axis=2)

    ada_cols = w_ada.shape[2]
    mod_part = _ada_fwd(c_all, w_ada, lax.dynamic_slice_in_dim(b_ada, chip * ada_cols, ada_cols, axis=1))
    mod_all = _all_gather_small("gather_mod", mod_part.reshape(DEPTH * 8, ada_cols))[0::2].reshape(N_CHIP, DEPTH, 8, ada_cols)
    mod = lax.dynamic_index_in_dim(mod_all, me, axis=2, keepdims=False).transpose(1, 0, 2).reshape(DEPTH, N_CHIP * ada_cols)
    lb = _lb_fwd(hgrn_lb_logits)

    gathered = _gather_weights(w, chip)
    wfull = []
    for l in range(DEPTH):
        full = {n: gathered[n][l] for n, _, _, _ in BIG}
        for nm, part in zip(("768", "xbc", "qkv", "gate", "small"), _regroup_w_in(full["w_in"])):
            full["win_" + nm] = part
        full["win_all"] = jnp.concatenate([full["win_" + nm] for nm in ("768", "xbc", "qkv", "gate", "small")], axis=1)
        wfull.append(full)

    loss8, d_x, d_nf, lg, grad = _device_step(x[0], loss_target[0], mod, lb, wfull, sp, chip, ci)

    dmod = jnp.stack([jnp.concatenate([lg[l][n] for n in ("sh1", "sc1", "g1", "sh2", "sc2", "g2")], axis=1)[0] for l in range(DEPTH)])
    d_lb = jnp.stack([lg[l]["lb"][0] for l in range(DEPTH)])
    contrib = {
        "norm_mix": jnp.stack([lg[l]["norm_mix"][0] for l in range(DEPTH)]),
        "norm_ffn": jnp.stack([lg[l]["norm_ffn"][0] for l in range(DEPTH)]),
        "b_merge": jnp.stack([lg[l]["b_merge"][0] for l in range(DEPTH)]),
        "hgrn_lb_logits": _lb_bwd(hgrn_lb_logits, d_lb),
        "hgrn_norm": jnp.stack([lg[l]["hgrn_norm"][0] for l in range(DEPTH)]),
        "ssm_conv_w": jnp.stack([lg[l]["ssm_conv_w"] for l in range(DEPTH)]),
        "ssm_conv_b": jnp.stack([lg[l]["ssm_conv_b"][0] for l in range(DEPTH)]),
        "ssm_dt_bias": jnp.stack([lg[l]["ssm_dt_bias"][0, DT_OFF:DT_OFF + 12] for l in range(DEPTH)]),
        "ssm_a_log": jnp.stack([lg[l]["ssm_a_log"][0, DT_OFF:DT_OFF + 12] for l in range(DEPTH)]),
        "ssm_d": jnp.stack([lg[l]["ssm_d"][0, DT_OFF:DT_OFF + 12] for l in range(DEPTH)]),
        "ssm_norm": jnp.stack([lg[l]["ssm_norm"][0] for l in range(DEPTH)]),
        "gdn_conv_w": jnp.stack([lg[l]["gdn_conv_w"] for l in range(DEPTH)]),
        "gdn_dt_bias": jnp.stack([lg[l]["gdn_dt_bias"][0, GA_OFF:GA_OFF + 6] for l in range(DEPTH)]),
        "gdn_a_log": jnp.stack([lg[l]["gdn_a_log"][0, GA_OFF:GA_OFF + 6] for l in range(DEPTH)]),
        "gdn_norm": jnp.stack([lg[l]["gdn_norm"][0] for l in range(DEPTH)]),
        "norm_final": d_nf[0],
    }
    flat = jnp.concatenate([dmod.reshape(-1)] + [contrib[n].reshape(-1) for n in SMALL_REPL] + [loss8[0, 0:1]])
    small_all = _all_gather_small("gather_small_grads", _pad_rows(flat, SMALL_ROWS, D_MODEL))
    total = _sum_leading("small_grad_sum", small_all, F32).reshape(-1)
    n_mod = dmod.size
    grad["b_ada"] = total[:n_mod].reshape(b_ada.shape)
    off = n_mod
    full_small = {}
    for n in SMALL_REPL:
        full_small[n] = total[off:off + contrib[n].size].reshape(contrib[n].shape)
        off += contrib[n].size
    loss = total[off]
    for n in SMALL_REPL:
        if n in ("ssm_conv_w", "gdn_conv_w"):
            cols = w[n].shape[2]
            grad[n] = lax.dynamic_slice_in_dim(full_small[n], chip * cols, cols, axis=2)
        else:
            grad[n] = full_small[n]
    dmod_cols = lax.dynamic_slice_in_dim(small_all[:, :n_mod // D_MODEL, :].reshape(8, DEPTH, -1), chip * ada_cols, ada_cols, axis=2)
    grad["w_ada"] = _ada_bwd(c_all, dmod_cols.transpose(1, 0, 2))

    delta, new_m, new_v = {}, {}, {}
    big_names = ("w_ada",) + tuple(n for n, _, _, _ in BIG)
    for n in big_names:
        delta[n], new_m[n], new_v[n] = _adamw("adamw_" + n, w[n], grad[n], m[n], v[n])
    small_names = [n for n in WEIGHTS if n not in big_names]
    packs = [_pad_rows(jnp.concatenate([d[n].reshape(-1) for n in small_names]), 584, LANES) for d in (w, grad, m, v)]
    outs = _ew("adamw_small", _adamw_fn, packs, [F32, F32, F32])
    off = 0
    for n in small_names:
        for dst, o in zip((delta, new_m, new_v), outs):
            dst[n] = o.reshape(-1)[off:off + w[n].size].reshape(w[n].shape)
        off += w[n].size
    return (loss, d_x[None], *[grad[n] for n in WEIGHTS], *[delta[n] for n in WEIGHTS], *[new_m[n] for n in WEIGHTS],
            *[new_v[n] for n in WEIGHTS])
```

```python
import functools

import jax
import jax.numpy as jnp
from jax import lax
from jax.experimental import pallas as pl
from jax.experimental.pallas import tpu as pltpu

F32 = jnp.float32
BF = jnp.bfloat16
HI = lax.Precision.HIGHEST

D_MODEL = 1024
DEPTH = 4
CHUNK = 64
MIX_W = 768
HEAD = 128
N_HEAD6 = 6
SSM_P = 64
SSM_N = 128
CONV_CH = 1280
QKV_W = 2304
FFN_H = 2816
IN_WIDTH = 11288
NORM_EPS = 1e-6
F_MIN = 1e-30
HALO = 8
HEAD_GROUP = 6
HGRN_SUB = 8
SMALL_W = 512
LANES = 128
DT_OFF, GB_OFF, GA_OFF = 0, 12, 18

ADAM_LR, ADAM_B1, ADAM_B2, ADAM_EPS, ADAM_WD, ADAM_STEP = 0.001, 0.9, 0.999, 1e-08, 0.01, 10

VMEM_LIMIT = 56 * 1024 * 1024
TOKEN_TILE = 256


def _pc(body, **kw):
    return pl.pallas_call(body, **kw)


def _cparams(sem):
    return pltpu.CompilerParams(dimension_semantics=sem, vmem_limit_bytes=VMEM_LIMIT)


def _bdot(a, b):
    return jnp.dot(a.astype(BF), b.astype(BF), preferred_element_type=F32)


def _bdot_nt(a, b):
    return lax.dot_general(a.astype(BF), b.astype(BF), (((1,), (1,)), ((), ())), preferred_element_type=F32)


def _bdot_tn(a, b):
    return lax.dot_general(a.astype(BF), b.astype(BF), (((0,), (0,)), ((), ())), preferred_element_type=F32)


def _silu(x):
    return x * jax.nn.sigmoid(x)


def _tri_mask(n, strict=False):
    t = lax.broadcasted_iota(jnp.int32, (n, n), 0)
    s = lax.broadcasted_iota(jnp.int32, (n, n), 1)
    return (s < t) if strict else (s <= t)


def _masked_exp(diff, mask):
    return jnp.where(mask, jnp.exp(jnp.where(mask, diff, 0.0)), 0.0)


def _split_bf16(x, n):
    parts, rest = [], x
    for _ in range(n):
        p = rest.astype(BF)
        parts.append(p)
        rest = rest - p.astype(F32)
    return parts


def _tri_sum(x, reverse):
    n, w = x.shape
    t = lax.broadcasted_iota(jnp.int32, (n, n), 0)
    s = lax.broadcasted_iota(jnp.int32, (n, n), 1)
    tri = jnp.where((s >= t) if reverse else (s <= t), 1.0, 0.0).astype(BF)
    y = jnp.dot(tri, jnp.concatenate(_split_bf16(x, 3), axis=1), preferred_element_type=F32)
    return y[:, :w] + y[:, w:2 * w] + y[:, 2 * w:]


@jax.custom_vjp
def _cumsum_rows(x):
    return _tri_sum(x, False)


_cumsum_rows.defvjp(lambda x: (_tri_sum(x, False), None), lambda _, g: (_tri_sum(g, True),))


def _dot_split(a, b, transpose_a=False):
    dims = (((0,), (0,)) if transpose_a else ((1,), (0,)), ((), ()))
    a_hi, a_lo = _split_bf16(a, 2)
    b_hi, b_lo = _split_bf16(b, 2)
    w = b.shape[1]
    y = lax.dot_general(a_hi, jnp.concatenate([b_hi, b_lo], axis=1), dims, preferred_element_type=F32)
    return y[:, :w] + y[:, w:] + lax.dot_general(a_lo, b_hi, dims, preferred_element_type=F32)


def _rms(x, w):
    return x * lax.rsqrt(jnp.mean(x * x, axis=-1, keepdims=True) + NORM_EPS) * w


def _causal_conv(halo, x, w):
    ext = jnp.concatenate([halo, x], axis=0)
    n = x.shape[0]
    acc = w[0:1, :] * ext[HALO - 3:HALO - 3 + n, :]
    for i in range(1, 4):
        acc = acc + w[i:i + 1, :] * ext[HALO - 3 + i:HALO - 3 + i + n, :]
    return acc


def _unit_lower_inverses(mats):
    n = mats[0].shape[0]
    t = lax.broadcasted_iota(jnp.int32, (n, n), 0)
    s_ = lax.broadcasted_iota(jnp.int32, (n, n), 1)
    xs = [jnp.where(t == s_, 1.0, 0.0).astype(F32) for _ in mats]
    for s in range(n - 1):
        r0 = 8 * ((s + 1) // 8)
        for i, a in enumerate(mats):
            x = xs[i]
            low = x[r0:] - a[r0:, s:s + 1] * x[s:s + 1, :]
            xs[i] = low if r0 == 0 else jnp.concatenate([x[:r0], low], axis=0)
    return xs


@jax.custom_vjp
def _unit_lower_solves(mats, rhss):
    return [_dot_split(inv, r) for inv, r in zip(_unit_lower_inverses(mats), rhss)]


def _uls_fwd(mats, rhss):
    invs = _unit_lower_inverses(mats)
    xs = [_dot_split(inv, r) for inv, r in zip(invs, rhss)]
    return xs, (invs, xs)


def _uls_bwd(res, gs):
    invs, xs = res
    ys = [_dot_split(inv, g, transpose_a=True) for inv, g in zip(invs, gs)]
    das = [jnp.where(_tri_mask(CHUNK, strict=True), -_bdot_nt(y, x), 0.0) for y, x in zip(ys, xs)]
    return das, ys


_unit_lower_solves.defvjp(_uls_fwd, _uls_bwd)


def _hgrn_chunk(tiles, halos, state, consts):
    q_raw, f_raw, v_all, g_raw = tiles
    lb, norm_w = consts
    q_all = _silu(q_raw)
    f = lb + (1.0 - lb) * jax.nn.sigmoid(f_raw)
    logf = jnp.log(jnp.maximum(f, F_MIN))
    k_all = (1.0 - lb) * jax.nn.sigmoid(-f_raw)
    b_all = _cumsum_rows(logf)
    sub = HGRN_SUB
    row = lax.broadcasted_iota(jnp.int32, (sub, 1), 0)
    src_row = lax.broadcasted_iota(jnp.int32, (CHUNK, 1), 0)
    src_lane = lax.broadcasted_iota(jnp.int32, (1, CHUNK), 1)
    heads = range(N_HEAD6)
    n_sub = CHUNK // sub
    cols = [slice(h * HEAD, (h + 1) * HEAD) for h in heads]
    qs, ks, vs, bs = ([a[:, sl] for sl in cols] for a in (q_all, k_all, v_all, b_all))
    o_inter = [_bdot_nt(qs[h] * jnp.exp(bs[h]), state[h]) for h in heads]
    blocks = [[None] * n_sub for _ in heads]
    for i in range(n_sub):
        r0 = i * sub
        for h in heads:
            if i > 0:
                ref = bs[h][r0 - 1:r0, :]
                blocks[h][i] = _bdot_nt(qs[h][r0:r0 + sub] * jnp.exp(bs[h][r0:r0 + sub] - ref),
                                        ks[h] * _masked_exp(ref - bs[h], src_row < r0))
            else:
                blocks[h][i] = jnp.zeros((sub, CHUNK), F32)
    for h in heads:
        for i in range(n_sub):
            r0 = i * sub
            qi, ki, bi = qs[h][r0:r0 + sub], ks[h][r0:r0 + sub], bs[h][r0:r0 + sub]
            for s in range(sub):
                e = _masked_exp(bi - bi[s:s + 1, :], row >= s)
                col = jnp.sum(qi * ki[s:s + 1, :] * e, axis=1, keepdims=True)
                blocks[h][i] = jnp.where(src_lane == r0 + s, col, blocks[h][i])
    os_ = [_bdot(jnp.concatenate(blocks[h], axis=0), vs[h]) + o_inter[h] for h in heads]
    ends = [bs[h][CHUNK - 1:CHUNK, :] for h in heads]
    new_states = [state[h] * jnp.exp(ends[h]) + _bdot_tn(vs[h], ks[h] * jnp.exp(ends[h] - bs[h])) for h in heads]
    outs = [_rms(os_[h], norm_w) * _silu(g_raw[:, cols[h]]) for h in heads]
    return (jnp.concatenate(outs, axis=1),), jnp.stack(new_states)


def _ssd_chunk(tiles, halos, state, consts):
    z, xbc_raw, small = tiles
    (halo,) = halos
    conv_w, conv_b, dt_bias, a_log, d_skip, norm_w = consts
    xbc = _silu(_causal_conv(halo, xbc_raw, conv_w) + conv_b)
    xs, bm, cm = xbc[:, :MIX_W], xbc[:, MIX_W:MIX_W + 2 * SSM_N], xbc[:, MIX_W + 2 * SSM_N:]
    dt = jax.nn.softplus(small + dt_bias)
    cum = _cumsum_rows(-jnp.exp(a_log) * dt)
    cum_t2 = jnp.concatenate([cum, cum], axis=0).T
    lane = lax.broadcasted_iota(jnp.int32, (1, LANES), 1)
    first = lane < SSM_P
    hm0 = jnp.where(first, 1.0, 0.0).astype(F32)
    hm1 = 1.0 - hm0
    src = jnp.where(first, lane, lane - SSM_P)
    tri2 = src <= lax.broadcasted_iota(jnp.int32, (CHUNK, 1), 0)
    pick = lambda a, b: jnp.where(first, a, b)
    bgs = [bm[:, g * SSM_N:(g + 1) * SSM_N] for g in range(2)]
    cgs = [cm[:, g * SSM_N:(g + 1) * SSM_N] for g in range(2)]
    gmats = [_bdot_nt(cgs[g], jnp.concatenate([bgs[g], bgs[g]], axis=0)) for g in range(2)]
    pairs = range(6)
    xps = [xs[:, p * LANES:(p + 1) * LANES] for p in pairs]
    c0s = [cum[:, 2 * p:2 * p + 1] for p in pairs]
    c1s = [cum[:, 2 * p + 1:2 * p + 2] for p in pairs]
    e0s = [cum[CHUNK - 1:CHUNK, 2 * p:2 * p + 1] for p in pairs]
    e1s = [cum[CHUNK - 1:CHUNK, 2 * p + 1:2 * p + 2] for p in pairs]
    segs = [_masked_exp(pick(c0s[p], c1s[p]) - pick(cum_t2[2 * p:2 * p + 1, :], cum_t2[2 * p + 1:2 * p + 2, :]), tri2)
            for p in pairs]
    vms = []
    for p in pairs:
        v = xps[p] * pick(dt[:, 2 * p:2 * p + 1], dt[:, 2 * p + 1:2 * p + 2])
        vms.append(jnp.concatenate([v * hm0, v * hm1], axis=0))
    y_intra = [_bdot(gmats[p // 3] * segs[p], vms[p]) for p in pairs]
    y_inter = [_bdot(jnp.concatenate([cgs[p // 3] * jnp.exp(c0s[p]), cgs[p // 3] * jnp.exp(c1s[p])], axis=1),
                     jnp.concatenate([state[p] * hm0, state[p] * hm1], axis=0)) for p in pairs]
    new_states = [_bdot_tn(jnp.concatenate([bgs[p // 3] * jnp.exp(e0s[p] - c0s[p]),
                                            bgs[p // 3] * jnp.exp(e1s[p] - c1s[p])], axis=0), vms[p])
                  + state[p] * pick(jnp.exp(e0s[p]), jnp.exp(e1s[p])) for p in pairs]
    ys = [y_intra[p] + y_inter[p] + pick(d_skip[:, 2 * p:2 * p + 1], d_skip[:, 2 * p + 1:2 * p + 2]) * xps[p] for p in pairs]
    y = jnp.concatenate(ys, axis=1) * _silu(z)
    gw = MIX_W // 2
    y = jnp.concatenate([_rms(y[:, g * gw:(g + 1) * gw], norm_w[:, g * gw:(g + 1) * gw]) for g in range(2)], axis=1)
    return (y,), jnp.stack(new_states)


def _gdn_chunk(tiles, halos, state, consts):
    qkv_raw, z, small = tiles
    (halo,) = halos
    conv_w, dt_bias, a_log, norm_w = consts
    qkv = _silu(_causal_conv(halo, qkv_raw, conv_w))
    beta_all = jax.nn.sigmoid(small)
    cum = _cumsum_rows(-jnp.exp(a_log) * jax.nn.softplus(small + dt_bias))
    cum_t = cum.T
    tri, tri_strict = _tri_mask(CHUNK), _tri_mask(CHUNK, strict=True)

    def group(hs):
        n = range(len(hs))
        qs, ks, betas, cs, ces, decays, rhss = [], [], [], [], [], [], []
        for h in hs:
            q = qkv[:, h * HEAD:(h + 1) * HEAD]
            k = qkv[:, MIX_W + h * HEAD:MIX_W + (h + 1) * HEAD]
            v = qkv[:, 2 * MIX_W + h * HEAD:2 * MIX_W + (h + 1) * HEAD]
            q = q * lax.rsqrt(jnp.sum(q * q, axis=-1, keepdims=True) + NORM_EPS) * (HEAD ** -0.5)
            k = k * lax.rsqrt(jnp.sum(k * k, axis=-1, keepdims=True) + NORM_EPS)
            beta = beta_all[:, GB_OFF + h:GB_OFF + h + 1]
            c, c_t = cum[:, GA_OFF + h:GA_OFF + h + 1], cum_t[GA_OFF + h:GA_OFF + h + 1, :]
            qs.append(q), ks.append(k), betas.append(beta), cs.append(c)
            ces.append(cum[CHUNK - 1:CHUNK, GA_OFF + h:GA_OFF + h + 1])
            decays.append(_masked_exp(c - c_t, tri))
            rhss.append(jnp.concatenate([v * beta, k * (beta * jnp.exp(c))], axis=1))
        sts = [state[h] for h in hs]
        qk_kks = [_bdot_nt(jnp.concatenate([qs[i], ks[i]], axis=0), ks[i]) for i in n]
        sols = _unit_lower_solves([jnp.where(tri_strict, betas[i] * qk_kks[i][CHUNK:] * decays[i], 0.0) for i in n], rhss)
        on_states = [_bdot(jnp.concatenate([sols[i][:, HEAD:], qs[i] * jnp.exp(cs[i])], axis=0), sts[i]) for i in n]
        us = [sols[i][:, :HEAD] - on_states[i][:CHUNK] for i in n]
        os_ = [on_states[i][CHUNK:] + _bdot(qk_kks[i][:CHUNK] * decays[i], us[i]) for i in n]
        new = [jnp.exp(ces[i]) * sts[i] + _bdot_tn(ks[i] * jnp.exp(ces[i] - cs[i]), us[i]) for i in n]
        outs = [_rms(os_[i], norm_w) * _silu(z[:, h * HEAD:(h + 1) * HEAD]) for i, h in enumerate(hs)]
        return outs, new

    outs, new_states = [], []
    for h0 in range(0, N_HEAD6, HEAD_GROUP):
        o, s = group(list(range(h0, h0 + HEAD_GROUP)))
        outs += o
        new_states += s
    return (jnp.concatenate(outs, axis=1),), jnp.stack(new_states)


def _scan_fwd(name, fn, tiled, halo_idx, consts, out_width, state_shape):
    seq = tiled[0][0].shape[0]
    nc = seq // CHUNK
    n_t, n_h, n_c = len(tiled), len(halo_idx), len(consts)

    def body(*refs):
        t_refs, h_refs, c_refs = refs[:n_t], refs[n_t:n_t + n_h], refs[n_t + n_h:n_t + n_h + n_c]
        y_ref, save_ref, st_ref = refs[n_t + n_h + n_c:]
        i = pl.program_id(0)

        @pl.when(i == 0)
        def _():
            st_ref[...] = jnp.zeros_like(st_ref)

        flag = jnp.where(i > 0, 1.0, 0.0).astype(F32)
        st = st_ref[...]
        (y,), new = fn([r[...] for r in t_refs], [r[...] * flag for r in h_refs], st, [r[...] for r in c_refs])
        save_ref[0] = st
        y_ref[...] = y.astype(y_ref.dtype)
        st_ref[...] = new

    in_specs = [pl.BlockSpec((CHUNK, w), functools.partial(lambda i, cb: (i, cb), cb=cb)) for _, w, cb in tiled]
    in_specs += [pl.BlockSpec((HALO, tiled[j][1]),
                              functools.partial(lambda i, cb: (jnp.maximum(i * (CHUNK // HALO) - 1, 0), cb), cb=tiled[j][2]))
                 for j in halo_idx]
    in_specs += [pl.BlockSpec(c.shape, functools.partial(lambda i, nd: (0,) * nd, nd=c.ndim)) for c in consts]
    zeros = (0,) * len(state_shape)
    return _pc(
        body, name=name, grid=(nc,), in_specs=in_specs,
        out_specs=(pl.BlockSpec((CHUNK, out_width), lambda i: (i, 0)),
                   pl.BlockSpec((1,) + state_shape, lambda i: (i,) + zeros)),
        out_shape=(jax.ShapeDtypeStruct((seq, out_width), BF), jax.ShapeDtypeStruct((nc,) + state_shape, F32)),
        scratch_shapes=[pltpu.VMEM(state_shape, F32)],
        compiler_params=_cparams(("arbitrary",)),
    )(*[t[0] for t in tiled], *[tiled[j][0] for j in halo_idx], *consts)


def _scan_bwd(name, fn, tiled, halo_idx, consts, saved, dy, dtile_dtypes, extra=None):
    seq = tiled[0][0].shape[0]
    nc = seq // CHUNK
    n_t, n_h, n_c = len(tiled), len(halo_idx), len(consts)
    state_shape = saved.shape[1:]
    n_x = 0 if extra is None else 1

    def body(*refs):
        t_refs, h_refs, c_refs = refs[:n_t], refs[n_t:n_t + n_h], refs[n_t + n_h:n_t + n_h + n_c]
        pos = n_t + n_h + n_c
        save_ref, dy_ref = refs[pos], refs[pos + 1]
        x_refs = refs[pos + 2:pos + 2 + n_x]
        pos += 2 + n_x
        dt_refs, dc_refs = refs[pos:pos + n_t], refs[pos + n_t:pos + n_t + n_c]
        dst_ref = refs[pos + n_t + n_c]
        carry_refs = refs[pos + n_t + n_c + 1:]
        i = pl.program_id(0)

        @pl.when(i == 0)
        def _():
            dst_ref[...] = jnp.zeros_like(dst_ref)
            for r in carry_refs:
                r[...] = jnp.zeros_like(r)
            for r in dc_refs:
                r[...] = jnp.zeros_like(r)

        flag = jnp.where(i < nc - 1, 1.0, 0.0).astype(F32)
        tiles = [r[...] for r in t_refs]
        halos = [r[...] * flag for r in h_refs]
        cvals = [r[...] for r in c_refs]
        _, vjp = jax.vjp(fn, tiles, halos, save_ref[0], cvals)
        d_tiles, d_halos, d_state, d_consts = vjp(((dy_ref[...].astype(F32),), dst_ref[...]))
        dst_ref[...] = d_state
        for r, g in zip(dc_refs, d_consts):
            r[...] += g
        for j, (r, g) in enumerate(zip(dt_refs, d_tiles)):
            if extra is not None and extra[0] == j:
                g = g + x_refs[0][...].astype(F32)
            r[...] = g.astype(r.dtype)
            if j in halo_idx:
                cr = carry_refs[halo_idx.index(j)]
                r[CHUNK - HALO:CHUNK, :] = (g[CHUNK - HALO:CHUNK, :] + cr[...]).astype(r.dtype)
                cr[...] = d_halos[halo_idx.index(j)] * flag

    rev = lambda i: nc - 1 - i
    in_specs = [pl.BlockSpec((CHUNK, w), functools.partial(lambda i, cb: (rev(i), cb), cb=cb)) for _, w, cb in tiled]
    in_specs += [pl.BlockSpec((HALO, tiled[j][1]),
                              functools.partial(lambda i, cb: (jnp.maximum(rev(i) * (CHUNK // HALO) - 1, 0), cb), cb=tiled[j][2]))
                 for j in halo_idx]
    in_specs += [pl.BlockSpec(c.shape, functools.partial(lambda i, nd: (0,) * nd, nd=c.ndim)) for c in consts]
    zeros = (0,) * len(state_shape)
    in_specs += [pl.BlockSpec((1,) + state_shape, lambda i: (rev(i),) + zeros),
                 pl.BlockSpec((CHUNK, dy.shape[1]), lambda i: (rev(i), 0))]
    args = [t[0] for t in tiled] + [tiled[j][0] for j in halo_idx] + list(consts) + [saved, dy]
    if extra is not None:
        in_specs.append(pl.BlockSpec((CHUNK, extra[1].shape[1]), lambda i: (rev(i), 0)))
        args.append(extra[1])
    out_specs = [pl.BlockSpec((CHUNK, w), lambda i: (rev(i), 0)) for _, w, _ in tiled]
    out_specs += [pl.BlockSpec(c.shape, functools.partial(lambda i, nd: (0,) * nd, nd=c.ndim)) for c in consts]
    out_shape = [jax.ShapeDtypeStruct((seq, w), dtd) for (_, w, _), dtd in zip(tiled, dtile_dtypes)]
    out_shape += [jax.ShapeDtypeStruct(c.shape, F32) for c in consts]
    scratch = [pltpu.VMEM(state_shape, F32)] + [pltpu.VMEM((HALO, tiled[j][1]), F32) for j in halo_idx]
    return _pc(body, name=name, grid=(nc,), in_specs=in_specs, out_specs=tuple(out_specs), out_shape=tuple(out_shape),
               scratch_shapes=scratch, compiler_params=_cparams(("arbitrary",)))(*args)


def _tile_fwd(name, fn, tiled, consts, outs, tm=TOKEN_TILE):
    seq = tiled[0][0].shape[0]
    n_t, n_c = len(tiled), len(consts)

    def body(*refs):
        res = fn(*[r[...] for r in refs[:n_t + n_c]])
        for r, y in zip(refs[n_t + n_c:], res):
            r[...] = y.astype(r.dtype)

    in_specs = [pl.BlockSpec((tm, w), functools.partial(lambda i, cb: (i, cb), cb=cb)) for _, w, cb in tiled]
    in_specs += [pl.BlockSpec(c.shape, functools.partial(lambda i, nd: (0,) * nd, nd=c.ndim)) for c in consts]
    return _pc(body, name=name, grid=(seq // tm,), in_specs=in_specs,
               out_specs=tuple(pl.BlockSpec((tm, w), lambda i: (i, 0)) for w, _ in outs),
               out_shape=tuple(jax.ShapeDtypeStruct((seq, w), dtp) for w, dtp in outs),
               compiler_params=_cparams(("arbitrary",)))(*[t[0] for t in tiled], *consts)


def _tile_bwd(name, fn, tiled, consts, douts, dtile_dtypes, add_to=None, tm=TOKEN_TILE):
    seq = tiled[0][0].shape[0]
    n_t, n_c, n_o = len(tiled), len(consts), len(douts)
    n_x = 0 if add_to is None else 1
    keep = [j for j, dtp in enumerate(dtile_dtypes) if dtp is not None]

    def body(*refs):
        vals = [r[...].astype(F32) for r in refs[:n_t + n_c]]
        pos = n_t + n_c
        g_refs, x_refs = refs[pos:pos + n_o], refs[pos + n_o:pos + n_o + n_x]
        pos += n_o + n_x
        dt_refs, dc_refs = refs[pos:pos + len(keep)], refs[pos + len(keep):]
        i = pl.program_id(0)

        @pl.when(i == 0)
        def _():
            for r in dc_refs:
                r[...] = jnp.zeros_like(r)

        _, vjp = jax.vjp(fn, *vals)
        cts = vjp(tuple(g[...].astype(F32) for g in g_refs))
        for r, j in zip(dt_refs, keep):
            g = cts[j]
            if add_to is not None and add_to[0] == j:
                g = g + x_refs[0][...].astype(F32)
            r[...] = g.astype(r.dtype)
        for r, g in zip(dc_refs, cts[n_t:]):
            r[...] += g

    in_specs = [pl.BlockSpec((tm, w), functools.partial(lambda i, cb: (i, cb), cb=cb)) for _, w, cb in tiled]
    in_specs += [pl.BlockSpec(c.shape, functools.partial(lambda i, nd: (0,) * nd, nd=c.ndim)) for c in consts]
    in_specs += [pl.BlockSpec((tm, g.shape[1]), lambda i: (i, 0)) for g in douts]
    args = [t[0] for t in tiled] + list(consts) + list(douts)
    if add_to is not None:
        in_specs.append(pl.BlockSpec((tm, add_to[1].shape[1]), lambda i: (i, 0)))
        args.append(add_to[1])
    out_specs = [pl.BlockSpec((tm, tiled[j][1]), lambda i: (i, 0)) for j in keep]
    out_specs += [pl.BlockSpec(c.shape, functools.partial(lambda i, nd: (0,) * nd, nd=c.ndim)) for c in consts]
    out_shape = [jax.ShapeDtypeStruct((seq, tiled[j][1]), dtile_dtypes[j]) for j in keep]
    out_shape += [jax.ShapeDtypeStruct(c.shape, F32) for c in consts]
    return _pc(body, name=name, grid=(seq // tm,), in_specs=in_specs, out_specs=tuple(out_specs),
               out_shape=tuple(out_shape), compiler_params=_cparams(("arbitrary",)))(*args)


def _lnmod_fn(x, nw, sc, sh):
    return (_rms(x, nw) * (1.0 + sc) + sh,)


def _resid_fn(x, o, g):
    return (x + (1.0 + g) * o,)


def _swiglu_fn(gu):
    return (_silu(gu[:, :FFN_H]) * gu[:, FFN_H:],)


def _merge_fn(yh, ys, yg, logits, wb, b_merge):
    gates = jax.nn.sigmoid(logits + b_merge)
    acc = None
    for n, y in enumerate((yh, ys, yg)):
        t = gates[:, n * D_MODEL:(n + 1) * D_MODEL] * _bdot(y, wb[n])
        acc = t if acc is None else acc + t
    return (acc,)


MM_VMEM_BUDGET = 40 * 1024 * 1024
MM_TILE_CAP = 1024
MM_K_CAP = 4096


def _divisor(n, cap, unit=LANES):
    best = None
    for d in range(unit, min(n, cap) + 1, unit):
        if n % d == 0:
            best = d
    return n if best is None else best


def _mm_tiles(m, n, k, out_bytes):
    tk = k if k <= MM_K_CAP else _divisor(k, 3072)
    tm, tn = _divisor(m, MM_TILE_CAP), _divisor(n, MM_TILE_CAP + MM_TILE_CAP // 2)

    def need(tm_, tn_):
        acc = tm_ * tn_ * 4 if tk < k else 0
        return 2 * 2 * tk * (tm_ + tn_) + acc + 2 * tm_ * tn_ * out_bytes

    while need(tm, tn) > MM_VMEM_BUDGET:
        if tn >= tm and _divisor(n, tn - LANES) < tn:
            tn = _divisor(n, tn - LANES)
        elif _divisor(m, tm - LANES) < tm:
            tm = _divisor(m, tm - LANES)
        else:
            break
    return tm, tn, tk


def _mm(a, b, mode, out_dtype, name):
    if mode == "nn":
        (m, k), n = a.shape, b.shape[1]
    elif mode == "nt":
        (m, k), n = a.shape, b.shape[0]
    else:
        (k, m), n = a.shape, b.shape[1]
    tm, tn, tk = _mm_tiles(m, n, k, jnp.dtype(out_dtype).itemsize)
    nk = k // tk
    dims = {"nn": ((1,), (0,)), "nt": ((1,), (1,)), "tn": ((0,), (0,))}[mode]

    def body_one(a_ref, b_ref, o_ref):
        o_ref[...] = lax.dot_general(a_ref[...], b_ref[...], (dims, ((), ())), preferred_element_type=F32).astype(o_ref.dtype)

    def body_acc(a_ref, b_ref, o_ref, acc_ref):
        kk = pl.program_id(2)

        @pl.when(kk == 0)
        def _():
            acc_ref[...] = jnp.zeros_like(acc_ref)

        acc_ref[...] += lax.dot_general(a_ref[...], b_ref[...], (dims, ((), ())), preferred_element_type=F32)

        @pl.when(kk == nk - 1)
        def _():
            o_ref[...] = acc_ref[...].astype(o_ref.dtype)

    a_spec = pl.BlockSpec((tk, tm), lambda i, j, kk: (kk, i)) if mode == "tn" else pl.BlockSpec((tm, tk), lambda i, j, kk: (i, kk))
    b_spec = pl.BlockSpec((tn, tk), lambda i, j, kk: (j, kk)) if mode == "nt" else pl.BlockSpec((tk, tn), lambda i, j, kk: (kk, j))
    return _pc(body_one if nk == 1 else body_acc, name=name, grid=(m // tm, n // tn, nk), in_specs=[a_spec, b_spec],
               out_specs=pl.BlockSpec((tm, tn), lambda i, j, kk: (i, j)),
               out_shape=jax.ShapeDtypeStruct((m, n), out_dtype),
               scratch_shapes=[] if nk == 1 else [pltpu.VMEM((tm, tn), F32)],
               compiler_params=_cparams(("parallel", "parallel", "arbitrary")))(a.astype(BF), b.astype(BF))


def _final_loss(x, tgt, norm_final, tm=TOKEN_TILE):
    seq = x.shape[0]

    def fn(xv, nf, tv):
        err = jnp.square(_rms(xv, nf) - tv)
        return 0.5 * jnp.sum(jnp.mean(err, axis=-1))

    def body(x_ref, t_ref, nf_ref, loss_ref, dx_ref, dnf_ref):
        i = pl.program_id(0)

        @pl.when(i == 0)
        def _():
            loss_ref[...] = jnp.zeros_like(loss_ref)
            dnf_ref[...] = jnp.zeros_like(dnf_ref)

        val, vjp = jax.vjp(functools.partial(fn, tv=t_ref[...]), x_ref[...], nf_ref[...])
        dx, dnf = vjp(jnp.ones((), F32))
        dx_ref[...] = dx
        dnf_ref[...] += dnf
        loss_ref[...] += jnp.broadcast_to(val, loss_ref.shape)

    return _pc(body, name="final_loss", grid=(seq // tm,),
               in_specs=[pl.BlockSpec((tm, D_MODEL), lambda i: (i, 0)), pl.BlockSpec((tm, D_MODEL), lambda i: (i, 0)),
                         pl.BlockSpec((1, D_MODEL), lambda i: (0, 0))],
               out_specs=(pl.BlockSpec((8, LANES), lambda i: (0, 0)), pl.BlockSpec((tm, D_MODEL), lambda i: (i, 0)),
                          pl.BlockSpec((1, D_MODEL), lambda i: (0, 0))),
               out_shape=(jax.ShapeDtypeStruct((8, LANES), F32), jax.ShapeDtypeStruct((seq, D_MODEL), F32),
                          jax.ShapeDtypeStruct((1, D_MODEL), F32)),
               compiler_params=_cparams(("arbitrary",)))(x, tgt, norm_final)


def _ada_fwd(c_all, w_ada, b_ada_cols):
    n_l, _, cols = w_ada.shape

    def body(c_ref, w_ref, b_ref, o_ref):
        o_ref[0] = jnp.dot(_silu(c_ref[...]), w_ref[0], preferred_element_type=F32, precision=HI) + b_ref[0]

    return _pc(body, name="ada_fwd", grid=(n_l,),
               in_specs=[pl.BlockSpec((8, D_MODEL), lambda l: (0, 0)), pl.BlockSpec((1, D_MODEL, cols), lambda l: (l, 0, 0)),
                         pl.BlockSpec((1, 1, cols), lambda l: (l, 0, 0))],
               out_specs=pl.BlockSpec((1, 8, cols), lambda l: (l, 0, 0)),
               out_shape=jax.ShapeDtypeStruct((n_l, 8, cols), F32),
               compiler_params=_cparams(("arbitrary",)))(c_all, w_ada, b_ada_cols.reshape(n_l, 1, cols))


def _ada_bwd(c_all, dmod_cols):
    n_l, _, cols = dmod_cols.shape

    def body(c_ref, g_ref, o_ref):
        o_ref[0] = lax.dot_general(_silu(c_ref[...]), g_ref[0], (((0,), (0,)), ((), ())), preferred_element_type=F32,
                                   precision=HI)

    return _pc(body, name="ada_bwd", grid=(n_l,),
               in_specs=[pl.BlockSpec((8, D_MODEL), lambda l: (0, 0)), pl.BlockSpec((1, 8, cols), lambda l: (l, 0, 0))],
               out_specs=pl.BlockSpec((1, D_MODEL, cols), lambda l: (l, 0, 0)),
               out_shape=jax.ShapeDtypeStruct((n_l, D_MODEL, cols), F32),
               compiler_params=_cparams(("arbitrary",)))(c_all, dmod_cols)


def _lb_fn(logits):
    e = jnp.exp(logits - jnp.max(logits, axis=0, keepdims=True))
    p = e / jnp.sum(e, axis=0, keepdims=True)
    r = lax.broadcasted_iota(jnp.int32, (DEPTH, 1), 0)
    lb = jnp.zeros_like(p)
    for j in range(1, DEPTH):
        lb = lb + jnp.where(r >= j, p[j:j + 1, :], 0.0)
    return lb


def _lb_fwd(logits):
    def body(l_ref, o_ref):
        o_ref[...] = _lb_fn(l_ref[...])

    return _pc(body, name="lb_fwd", out_shape=jax.ShapeDtypeStruct(logits.shape, F32))(logits)


def _lb_bwd(logits, dlb):
    def body(l_ref, g_ref, o_ref):
        _, vjp = jax.vjp(_lb_fn, l_ref[...])
        o_ref[...] = vjp(g_ref[...])[0]

    return _pc(body, name="lb_bwd", out_shape=jax.ShapeDtypeStruct(logits.shape, F32))(logits, dlb)


def _rows_for(n_rows, n_cols):
    r = 8
    while r * 2 <= n_rows and n_rows % (r * 2) == 0 and r * 2 * n_cols <= 256 * 1024:
        r *= 2
    return r if n_rows % r == 0 else n_rows


def _ew(name, fn, ins, out_dtypes):
    n_rows, n_cols = ins[0].shape
    tr = _rows_for(n_rows, n_cols)
    n_in = len(ins)

    def body(*refs):
        res = fn(*[r[...] for r in refs[:n_in]])
        for r, y in zip(refs[n_in:], res):
            r[...] = y.astype(r.dtype)

    spec = pl.BlockSpec((tr, n_cols), lambda i: (i, 0))
    return _pc(body, name=name, grid=(n_rows // tr,), in_specs=[spec] * n_in, out_specs=tuple([spec] * len(out_dtypes)),
               out_shape=tuple(jax.ShapeDtypeStruct((n_rows, n_cols), d) for d in out_dtypes),
               compiler_params=_cparams(("arbitrary",)))(*ins)


def _adamw_fn(w, g, m, v):
    m = ADAM_B1 * m + (1.0 - ADAM_B1) * g
    v = ADAM_B2 * v + (1.0 - ADAM_B2) * jnp.square(g)
    m_hat = m / (1.0 - ADAM_B1 ** ADAM_STEP)
    v_hat = v / (1.0 - ADAM_B2 ** ADAM_STEP)
    return -ADAM_LR * (m_hat / (jnp.sqrt(v_hat) + ADAM_EPS) + ADAM_WD * w), m, v


def _adamw(name, w, g, m, v):
    shape = w.shape
    two = (-1, shape[-1])
    d, nm, nv = _ew(name, _adamw_fn, [a.reshape(two) for a in (w, g, m, v)], [F32, F32, F32])
    return d.reshape(shape), nm.reshape(shape), nv.reshape(shape)


def _sum_leading(name, a, out_dtype):
    n, n_rows, n_cols = a.shape
    tr = _rows_for(n_rows, n_cols)

    def body(a_ref, o_ref):
        acc = a_ref[0].astype(F32)
        for j in range(1, n):
            acc = acc + a_ref[j].astype(F32)
        o_ref[...] = acc.astype(o_ref.dtype)

    return _pc(body, name=name, grid=(n_rows // tr,), in_specs=[pl.BlockSpec((n, tr, n_cols), lambda i: (0, i, 0))],
               out_specs=pl.BlockSpec((tr, n_cols), lambda i: (i, 0)),
               out_shape=jax.ShapeDtypeStruct((n_rows, n_cols), out_dtype),
               compiler_params=_cparams(("arbitrary",)))(a)


MESH = pl.DeviceIdType.MESH
ANY = pl.BlockSpec(memory_space=pl.ANY)


def _place():
    return lax.axis_index("x"), lax.axis_index("y"), lax.axis_index("c")


def _all_gather_small(name, a):
    m_per, n = a.shape

    def body(x_ref, out_ref, send_sems, recv_sems, local_sem):
        x, y, c = _place()
        me, sibling = (x, y, c), (x, y, 1 - c)
        chips = [(1 - x, y), (x, 1 - y), (1 - x, 1 - y)]

        def rows(px, py, pc):
            return out_ref.at[pl.ds((4 * px + 2 * py + pc) * m_per, m_per), :]

        def copy(k, block, to, src=None):
            return pltpu.make_async_remote_copy(src_ref=rows(*block) if src is None else src, dst_ref=rows(*block),
                                                send_sem=send_sems.at[k], recv_sem=recv_sems.at[k], device_id=to,
                                                device_id_type=MESH)

        mine = pltpu.make_async_copy(x_ref, rows(*me), local_sem)
        mine.start()
        first = [copy(0, me, sibling, src=x_ref)]
        first += [copy(1 + j, me, (*chip, c), src=x_ref) for j, chip in enumerate(chips)]
        for cp in first:
            cp.start()
        passed = [copy(4 + j, (*chip, c), sibling) for j, chip in enumerate(chips)]
        for j, chip in enumerate(chips):
            copy(1 + j, (*chip, c), me).wait_recv()
            passed[j].start()
        copy(0, sibling, me).wait_recv()
        for j, chip in enumerate(chips):
            copy(4 + j, (*chip, 1 - c), me).wait_recv()
        for cp in first + passed:
            cp.wait_send()
        mine.wait()

    out = _pc(body, name=name, out_shape=jax.ShapeDtypeStruct((8 * m_per, n), a.dtype),
              in_specs=[pl.BlockSpec(memory_space=pltpu.VMEM)], out_specs=pl.BlockSpec(memory_space=pltpu.VMEM),
              scratch_shapes=[pltpu.SemaphoreType.DMA((7,)), pltpu.SemaphoreType.DMA((7,)), pltpu.SemaphoreType.DMA],
              compiler_params=pltpu.CompilerParams(vmem_limit_bytes=VMEM_LIMIT))(a)
    return out.reshape(8, m_per, n)


def _chip_gather(name, pack):
    n_l, n_r, n_c = pack.shape
    half = n_r // 2

    def body(p_ref, o_ref, send_sems, recv_sems):
        x, y, c = _place()
        sibling = (x, y, 1 - c)
        chips = [(1 - x, y), (x, 1 - y), (1 - x, 1 - y)]

        def slab(px, py, pc):
            return o_ref.at[2 * px + py, :, pl.ds(pc * half, half), :]

        def copy(k, src, dst, to):
            return pltpu.make_async_remote_copy(src_ref=src, dst_ref=dst, send_sem=send_sems.at[k], recv_sem=recv_sems.at[k],
                                                device_id=to, device_id_type=MESH)

        first = [copy(j, p_ref.at[:, pl.ds(c * half, half), :], slab(x, y, c), (*chip, c)) for j, chip in enumerate(chips)]
        for cp in first:
            cp.start()
        passed = [copy(3 + j, slab(*chip, c), slab(*chip, c), sibling) for j, chip in enumerate(chips)]
        for j, chip in enumerate(chips):
            copy(j, slab(*chip, c), slab(*chip, c), (*chip, c)).wait_recv()
            passed[j].start()
        for j, chip in enumerate(chips):
            copy(3 + j, slab(*chip, 1 - c), slab(*chip, 1 - c), sibling).wait_recv()
        for cp in first + passed:
            cp.wait_send()

    return _pc(body, name=name, out_shape=jax.ShapeDtypeStruct((4, n_l, n_r, n_c), pack.dtype), in_specs=[ANY], out_specs=ANY,
               scratch_shapes=[pltpu.SemaphoreType.DMA((6,)), pltpu.SemaphoreType.DMA((6,))])(pack)


def _pair_swap(name, give):
    def body(g_ref, o_ref, send_sem, recv_sem):
        x, y, c = _place()
        cp = pltpu.make_async_remote_copy(src_ref=g_ref, dst_ref=o_ref, send_sem=send_sem, recv_sem=recv_sem,
                                          device_id=(x, y, 1 - c), device_id_type=MESH)
        cp.start()
        cp.wait()

    return _pc(body, name=name, out_shape=jax.ShapeDtypeStruct(give.shape, give.dtype), in_specs=[ANY], out_specs=ANY,
               scratch_shapes=[pltpu.SemaphoreType.DMA, pltpu.SemaphoreType.DMA])(give)


def _chip_exchange(name, parts):
    def body(p_ref, o_ref, send_sems, recv_sems):
        x, y, c = _place()
        me = 2 * x + y
        chips = [(1 - x, y), (x, 1 - y), (1 - x, 1 - y)]

        def copy(k, src, dst, to):
            return pltpu.make_async_remote_copy(src_ref=src, dst_ref=dst, send_sem=send_sems.at[k], recv_sem=recv_sems.at[k],
                                                device_id=to, device_id_type=MESH)

        sends = [copy(j, p_ref.at[2 * px + py], o_ref.at[me], (px, py, c)) for j, (px, py) in enumerate(chips)]
        for cp in sends:
            cp.start()
        for j, (px, py) in enumerate(chips):
            copy(j, p_ref.at[2 * px + py], o_ref.at[2 * px + py], (px, py, c)).wait_recv()
        for cp in sends:
            cp.wait_send()

    return _pc(body, name=name, out_shape=jax.ShapeDtypeStruct(parts.shape, parts.dtype), in_specs=[ANY], out_specs=ANY,
               scratch_shapes=[pltpu.SemaphoreType.DMA((3,)), pltpu.SemaphoreType.DMA((3,))])(parts)


N_CHIP = 4
BIG = (("w_in", (1024, 2822), 1, (1024, 2822)), ("w_branch", (3, 768, 256), 2, (2304, 256)),
       ("w_out", (256, 1024), 0, (256, 1024)), ("w_ffn_in", (1024, 1408), 1, (1024, 1408)),
       ("w_ffn_out", (704, 1024), 0, (704, 1024)))
G768 = ((0, 3072), (3072, 3840), (7436, 8204))
GXBC, GQKV, GGATE = (3840, 5120), (5132, 7436), (8216, 11288)
GSMALL = ((5120, 5132), (8204, 8210), (8210, 8216))
W768, WXBC, WGATE = 4608, CONV_CH, 3 * D_MODEL
IN_PAD = W768 + WXBC + QKV_W + WGATE + SMALL_W


def _join_shards(slabs, axis, shard_shape):
    n_l = slabs.shape[1]
    parts = [slabs[j].reshape((n_l,) + shard_shape) for j in range(N_CHIP)]
    return jnp.concatenate(parts, axis=axis + 1)


def _split_shards(full, axis, rows_cols):
    n_l = full.shape[0]
    size = full.shape[axis + 1] // N_CHIP
    return jnp.stack([lax.slice_in_dim(full, j * size, (j + 1) * size, axis=axis + 1).reshape((n_l,) + rows_cols)
                      for j in range(N_CHIP)])


def _gather_weights(w, chip, big=BIG):
    out = {}
    for n, shape, ax, rc in big:
        n_l = w[n].shape[0]
        mine = w[n].astype(BF).reshape((n_l,) + rc)
        slabs = lax.dynamic_update_slice(_chip_gather("gather_" + n, mine), mine[None], (chip, 0, 0, 0))
        out[n] = _join_shards(slabs, ax, shape)
    return out


def _pair_stage(full_grads, core, big=BIG):
    out = {}
    for n, _, ax, (rows, cols) in big:
        n_l = full_grads[n].shape[0]
        slabs = _split_shards(full_grads[n], ax, (rows, cols))
        half = rows // 2
        keep = lax.dynamic_slice_in_dim(slabs, core * half, half, axis=2).reshape(-1, cols)
        give = lax.dynamic_slice_in_dim(slabs, (1 - core) * half, half, axis=2).reshape(-1, cols)
        got = _pair_swap("pair_swap_" + n, give)
        (pair_sum,) = _ew("pair_sum_" + n, lambda a, b: (a.astype(F32) + b.astype(F32),), [keep, got], [BF])
        out[n] = pair_sum.reshape(N_CHIP, n_l * half, cols)
    return out


def _own_slab(landed, pair_sum, chip):
    return lax.dynamic_update_slice(landed, lax.dynamic_slice_in_dim(pair_sum, chip, 1, axis=0), (chip, 0, 0))


def _finish_reduce(parts, n_l, core, big=BIG):
    out = {}
    for n, shape, _, (rows, cols) in big:
        half = rows // 2
        mine = _sum_leading("chip_sum_" + n, parts[n], F32).reshape(n_l, half, cols)
        theirs = _pair_swap("pair_share_" + n, mine)
        full = jnp.concatenate([jnp.where(core == 0, mine, theirs), jnp.where(core == 0, theirs, mine)], axis=1)
        out[n] = full.reshape((n_l,) + shape)
    return out


def _reduce_grads(full_grads, chip, core, big=BIG):
    pair_sums = _pair_stage(full_grads, core, big)
    parts = {n: _own_slab(_chip_exchange("chip_exchange_" + n, pair_sums[n]), pair_sums[n], chip) for n, _, _, _ in big}
    return _finish_reduce(parts, full_grads[big[0][0]].shape[0], core, big)


HBM_SPEC = pl.BlockSpec(memory_space=pltpu.HBM)
SEM_SPEC = pl.BlockSpec(memory_space=pltpu.SEMAPHORE)
DATAFLOW = pltpu.SideEffectType.DATAFLOW_SIDE_EFFECTING


def _exchange_copies(p_ref, land_ref, sems, waiting, spread):
    x, y, c = _place()
    me = 2 * x + y
    out = []
    for j, (px, py) in enumerate([(1 - x, y), (x, 1 - y), (1 - x, 1 - y)]):
        out.append(pltpu.make_async_remote_copy(src_ref=p_ref if spread else p_ref.at[2 * px + py],
                                                dst_ref=land_ref.at[2 * px + py if waiting else me],
                                                send_sem=sems[j], recv_sem=sems[3 + j], device_id=(px, py, c),
                                                device_id_type=MESH))
    return out


def _exchange_start(name, parts, spread=False):
    land_shape = ((N_CHIP,) + parts.shape) if spread else parts.shape

    def body(p_ref, land_ref, s0, s1, s2, r0, r1, r2, p_thru, land_thru, token):
        for cp in _exchange_copies(p_ref, land_ref, (s0, s1, s2, r0, r1, r2), False, spread):
            cp.start()
        token[...] = jnp.zeros_like(token)

    res = _pc(body, name=name,
              out_shape=(pltpu.SemaphoreType.DMA(()),) * 6 + (pltpu.HBM(parts.shape, parts.dtype), pltpu.HBM(land_shape, parts.dtype),
                                                            jax.ShapeDtypeStruct((8, LANES), F32)),
              in_specs=(HBM_SPEC, HBM_SPEC), out_specs=(SEM_SPEC,) * 6 + (HBM_SPEC, HBM_SPEC, pl.BlockSpec(memory_space=pltpu.VMEM)),
              input_output_aliases={0: 6, 1: 7}, compiler_params=pltpu.CompilerParams(has_side_effects=DATAFLOW))(
        pltpu.with_memory_space_constraint(parts, pltpu.HBM),
        pltpu.with_memory_space_constraint(lax.empty(land_shape, parts.dtype), pltpu.HBM))
    return res[:6], res[6], res[7], res[8]


def _exchange_wait(name, sems, p_thru, land_thru, after, spread=False):
    def body(p_ref, land_ref, s0, s1, s2, r0, r1, r2, after_ref, p_dead, got_ref):
        for cp in _exchange_copies(p_ref, land_ref, (s0, s1, s2, r0, r1, r2), True, spread):
            cp.wait_send()
            cp.wait_recv()

    return _pc(body, name=name, out_shape=(pltpu.HBM(p_thru.shape, p_thru.dtype), pltpu.HBM(land_thru.shape, land_thru.dtype)),
               in_specs=(HBM_SPEC, HBM_SPEC) + (SEM_SPEC,) * 6 + (ANY,), out_specs=(HBM_SPEC, HBM_SPEC),
               input_output_aliases={0: 0, 1: 1}, compiler_params=pltpu.CompilerParams(has_side_effects=DATAFLOW))(
        p_thru, land_thru, *sems, after)[1]


def _regroup_w_in(w):
    cat = lambda spans: jnp.concatenate([w[:, a:b] for a, b in spans], axis=1)
    small = jnp.concatenate([cat(GSMALL), jnp.zeros((w.shape[0], SMALL_W - 24), w.dtype)], axis=1)
    return cat(G768), cat((GXBC,)), cat((GQKV,)), cat((GGATE,)), small


def _ungroup_w_in(d):
    o_xbc, o_qkv, o_gate, o_small = W768, W768 + WXBC, W768 + WXBC + QKV_W, W768 + WXBC + QKV_W + WGATE
    spans = ((0, 3072), (3072, 3840), (o_xbc, o_xbc + WXBC), (o_small, o_small + 12), (o_qkv, o_qkv + QKV_W),
             (3840, 4608), (o_small + 12, o_small + 18), (o_small + 18, o_small + 24), (o_gate, o_gate + WGATE))
    return jnp.concatenate([d[:, a:b] for a, b in spans], axis=1)


def _lane_pad(v, off):
    return jnp.pad(v, (off, LANES - off - v.shape[0]))[None, :]


STATE6 = (N_HEAD6, HEAD, HEAD)


def _mixer_inputs(sv, lp):
    p768, pxbc, pqkv, psmall = sv["p768"], sv["pxbc"], sv["pqkv"], sv["psmall"]
    hgrn = ([(p768, MIX_W, j) for j in range(4)], [], [lp["lb"], lp["hgrn_norm"]])
    ssd = ([(p768, MIX_W, 4), (pxbc, CONV_CH, 0), (psmall, LANES, 0)], [1],
           [lp["ssm_conv_w"], lp["ssm_conv_b"], lp["ssm_dt_bias"], lp["ssm_a_log"], lp["ssm_d"], lp["ssm_norm"]])
    gdn = ([(pqkv, QKV_W, 0), (p768, MIX_W, 5), (psmall, LANES, 0)], [0],
           [lp["gdn_conv_w"], lp["gdn_dt_bias"], lp["gdn_a_log"], lp["gdn_norm"]])
    return hgrn, ssd, gdn


def _layer_fwd(x, md, lw, lp):
    sv = {"x": x}
    (sv["h1"],) = _tile_fwd("lnmod1", _lnmod_fn, [(x, D_MODEL, 0)], [lp["norm_mix"], md["sc1"], md["sh1"]], [(D_MODEL, BF)])
    for nm in ("768", "xbc", "qkv", "gate", "small"):
        sv["p" + nm] = _mm(sv["h1"], lw["win_" + nm], "nn", F32, "proj_" + nm)
    hgrn, ssd, gdn = _mixer_inputs(sv, lp)
    sv["y_h"], sv["st_h"] = _scan_fwd("hgrn_fwd", _hgrn_chunk, *hgrn, MIX_W, STATE6)
    sv["y_s"], sv["st_s"] = _scan_fwd("ssd_fwd", _ssd_chunk, *ssd, MIX_W, STATE6)
    sv["y_g"], sv["st_g"] = _scan_fwd("gdn_fwd", _gdn_chunk, *gdn, MIX_W, STATE6)
    (sv["merged"],) = _tile_fwd("merge", _merge_fn, _merge_tiles(sv), [lw["w_branch"], lp["b_merge"]], [(D_MODEL, BF)])
    sv["out"] = _mm(sv["merged"], lw["w_out"], "nn", F32, "out_proj")
    (sv["x_mid"],) = _tile_fwd("resid1", _resid_fn, [(x, D_MODEL, 0), (sv["out"], D_MODEL, 0)], [md["g1"]], [(D_MODEL, F32)])
    (sv["h2"],) = _tile_fwd("lnmod2", _lnmod_fn, [(sv["x_mid"], D_MODEL, 0)], [lp["norm_ffn"], md["sc2"], md["sh2"]],
                            [(D_MODEL, BF)])
    sv["gu"] = _mm(sv["h2"], lw["w_ffn_in"], "nn", F32, "ffn_in")
    (sv["act"],) = _tile_fwd("swiglu", _swiglu_fn, [(sv["gu"], 2 * FFN_H, 0)], [], [(FFN_H, BF)])
    sv["o2"] = _mm(sv["act"], lw["w_ffn_out"], "nn", F32, "ffn_out")
    (x_out,) = _tile_fwd("resid2", _resid_fn, [(sv["x_mid"], D_MODEL, 0), (sv["o2"], D_MODEL, 0)], [md["g2"]], [(D_MODEL, F32)])
    return x_out, sv


def _merge_tiles(sv):
    return [(sv["y_h"], MIX_W, 0), (sv["y_s"], MIX_W, 0), (sv["y_g"], MIX_W, 0), (sv["pgate"], WGATE, 0)]


def _layer_bwd(dx_out, sv, md, lw, lp):
    g = {}
    x, x_mid = sv["x"], sv["x_mid"]
    d_xmid, d_o2, g["g2"] = _tile_bwd("resid2_b", _resid_fn, [(x_mid, D_MODEL, 0), (sv["o2"], D_MODEL, 0)], [md["g2"]], [dx_out],
                                      [F32, BF])
    d_act = _mm(d_o2, lw["w_ffn_out"], "nt", F32, "ffn_out_dx")
    g["w_ffn_out"] = _mm(sv["act"], d_o2, "tn", BF, "ffn_out_dw")
    (d_gu,) = _tile_bwd("swiglu_b", _swiglu_fn, [(sv["gu"], 2 * FFN_H, 0)], [], [d_act], [BF])
    d_h2 = _mm(d_gu, lw["w_ffn_in"], "nt", F32, "ffn_in_dx")
    g["w_ffn_in"] = _mm(sv["h2"], d_gu, "tn", BF, "ffn_in_dw")
    d_xmid, g["norm_ffn"], g["sc2"], g["sh2"] = _tile_bwd(
        "lnmod2_b", _lnmod_fn, [(x_mid, D_MODEL, 0)], [lp["norm_ffn"], md["sc2"], md["sh2"]], [d_h2], [F32], add_to=(0, d_xmid))
    d_x, d_out, g["g1"] = _tile_bwd("resid1_b", _resid_fn, [(x, D_MODEL, 0), (sv["out"], D_MODEL, 0)], [md["g1"]], [d_xmid],
                                    [F32, BF])
    d_merged = _mm(d_out, lw["w_out"], "nt", F32, "out_proj_dx")
    g["w_out"] = _mm(sv["merged"], d_out, "tn", BF, "out_proj_dw")
    d_yh, d_ys, d_yg, d_gate, g["w_branch"], g["b_merge"] = _tile_bwd(
        "merge_b", _merge_fn, _merge_tiles(sv), [lw["w_branch"], lp["b_merge"]], [d_merged], [F32, F32, F32, BF])
    hgrn, ssd, gdn = _mixer_inputs(sv, lp)
    d_q, d_f, d_v, d_g, g["lb"], g["hgrn_norm"] = _scan_bwd("hgrn_bwd", _hgrn_chunk, *hgrn, sv["st_h"], d_yh, [BF] * 4)
    (d_sz, d_xbc, d_small, g["ssm_conv_w"], g["ssm_conv_b"], g["ssm_dt_bias"], g["ssm_a_log"], g["ssm_d"],
     g["ssm_norm"]) = _scan_bwd("ssd_bwd", _ssd_chunk, *ssd, sv["st_s"], d_ys, [BF, BF, F32])
    d_qkv, d_gz, d_small, g["gdn_conv_w"], g["gdn_dt_bias"], g["gdn_a_log"], g["gdn_norm"] = _scan_bwd(
        "gdn_bwd", _gdn_chunk, *gdn, sv["st_g"], d_yg, [BF, BF, BF], extra=(2, d_small))
    d_proj = jnp.concatenate([d_q, d_f, d_v, d_g, d_sz, d_gz, d_xbc, d_qkv, d_gate, d_small,
                              jnp.zeros((x.shape[0], SMALL_W - LANES), BF)], axis=1)
    d_h1 = _mm(d_proj, lw["win_all"], "nt", F32, "proj_dx")
    g["w_in"] = _ungroup_w_in(_mm(sv["h1"], d_proj, "tn", BF, "proj_dw"))
    d_x, g["norm_mix"], g["sc1"], g["sh1"] = _tile_bwd(
        "lnmod1_b", _lnmod_fn, [(x, D_MODEL, 0)], [lp["norm_mix"], md["sc1"], md["sh1"]], [d_h1], [F32], add_to=(0, d_x))
    return d_x, g


SMALL_REPL = ("norm_mix", "norm_ffn", "b_merge", "hgrn_lb_logits", "hgrn_norm", "ssm_conv_w", "ssm_conv_b", "ssm_dt_bias",
              "ssm_a_log", "ssm_d", "ssm_norm", "gdn_conv_w", "gdn_dt_bias", "gdn_a_log", "gdn_norm", "norm_final")
WEIGHTS = ("w_ada", "b_ada", "norm_mix", "norm_ffn", "w_in", "b_merge", "hgrn_lb_logits", "hgrn_norm", "ssm_conv_w",
           "ssm_conv_b", "ssm_dt_bias", "ssm_a_log", "ssm_d", "ssm_norm", "gdn_conv_w", "gdn_dt_bias", "gdn_a_log",
           "gdn_norm", "w_branch", "w_out", "w_ffn_in", "w_ffn_out", "norm_final")
SMALL_ROWS = 120


def _pad_rows(flat, n_rows, n_cols):
    return jnp.concatenate([flat, jnp.zeros((n_rows * n_cols - flat.shape[0],), flat.dtype)]).reshape(n_rows, n_cols)


def _device_step(x, tgt, mod, lb, wfull, sp, chip=None, core=None, order_after=None):
    mds, lps, svs = [], [], []
    h = x
    weights_of = wfull if callable(wfull) else (lambda layer, after: wfull[layer])
    wfull = []
    for l in range(DEPTH):
        wfull.append(weights_of(l, h))
        md = {n: mod[l, i * D_MODEL:(i + 1) * D_MODEL][None, :] for i, n in enumerate(("sh1", "sc1", "g1", "sh2", "sc2", "g2"))}
        if l == 0 and order_after is not None:
            md["sc1"] = md["sc1"] + order_after
        lp = {n: sp[n][l][None, :] for n in ("norm_mix", "norm_ffn", "b_merge", "hgrn_norm", "ssm_conv_b", "ssm_norm", "gdn_norm")}
        lp["lb"] = lb[l][None, :]
        lp["ssm_conv_w"], lp["gdn_conv_w"] = sp["ssm_conv_w"][l], sp["gdn_conv_w"][l]
        for n in ("ssm_dt_bias", "ssm_a_log", "ssm_d"):
            lp[n] = _lane_pad(sp[n][l], DT_OFF)
        for n in ("gdn_dt_bias", "gdn_a_log"):
            lp[n] = _lane_pad(sp[n][l], GA_OFF)
        h, sv = _layer_fwd(h, md, wfull[l], lp)
        mds.append(md), lps.append(lp), svs.append(sv)
    loss, dh, d_nf = _final_loss(h, tgt, sp["norm_final"][None, :])
    grads = [None] * DEPTH
    if core is None:
        for l in reversed(range(DEPTH)):
            dh, grads[l] = _layer_bwd(dh, svs[l], mds[l], wfull[l], lps[l])
        return loss, dh, d_nf, grads
    names = [n for n, _, _, _ in BIG]
    landed, flying = [None] * DEPTH, None
    for l in reversed(range(DEPTH)):
        md = mds[l]
        if flying is not None:
            md = dict(md, g2=md["g2"] + sum(tok[0, 0] for _, _, _, tok in flying.values()))
        dh, grads[l] = _layer_bwd(dh, svs[l], md, wfull[l], lps[l])
        if flying is not None:
            landed[l + 1] = {n: _own_slab(_exchange_wait(f"exchange_wait_{n}_{l + 1}", *flying[n][:3], dh), sums[n], chip)
                             for n in names}
        sums = _pair_stage({n: grads[l][n].astype(BF)[None] for n in names}, core)
        if l > 0:
            flying = {n: _exchange_start(f"exchange_start_{n}_{l}", sums[n]) for n in names}
        else:
            landed[0] = {n: _own_slab(_chip_exchange("chip_exchange_" + n, sums[n]), sums[n], chip) for n in names}
    parts = {n: jnp.concatenate([landed[l][n] for l in range(DEPTH)], axis=1) for n in names}
    return loss, dh, d_nf, grads, _finish_reduce(parts, DEPTH, core)


def kernel(x, c, w_ada, b_ada, norm_mix, norm_ffn, w_in, b_merge, hgrn_lb_logits, hgrn_norm, ssm_conv_w, ssm_conv_b, ssm_dt_bias, ssm_a_log, ssm_d, ssm_norm, gdn_conv_w, gdn_dt_bias, gdn_a_log, gdn_norm, w_branch, w_out, w_ffn_in, w_ffn_out, norm_final, loss_target, m_w_ada, m_b_ada, m_norm_mix, m_norm_ffn, m_w_in, m_b_merge, m_hgrn_lb_logits, m_hgrn_norm, m_ssm_conv_w, m_ssm_conv_b, m_ssm_dt_bias, m_ssm_a_log, m_ssm_d, m_ssm_norm, m_gdn_conv_w, m_gdn_dt_bias, m_gdn_a_log, m_gdn_norm, m_w_branch, m_w_out, m_w_ffn_in, m_w_ffn_out, m_norm_final, v_w_ada, v_b_ada, v_norm_mix, v_norm_ffn, v_w_in, v_b_merge, v_hgrn_lb_logits, v_hgrn_norm, v_ssm_conv_w, v_ssm_conv_b, v_ssm_dt_bias, v_ssm_a_log, v_ssm_d, v_ssm_norm, v_gdn_conv_w, v_gdn_dt_bias, v_gdn_a_log, v_gdn_norm, v_w_branch, v_w_out, v_w_ffn_in, v_w_ffn_out, v_norm_final):
    w = dict(w_ada=w_ada, b_ada=b_ada, norm_mix=norm_mix, norm_ffn=norm_ffn, w_in=w_in, b_merge=b_merge,
             hgrn_lb_logits=hgrn_lb_logits, hgrn_norm=hgrn_norm, ssm_conv_w=ssm_conv_w, ssm_conv_b=ssm_conv_b,
             ssm_dt_bias=ssm_dt_bias, ssm_a_log=ssm_a_log, ssm_d=ssm_d, ssm_norm=ssm_norm, gdn_conv_w=gdn_conv_w,
             gdn_dt_bias=gdn_dt_bias, gdn_a_log=gdn_a_log, gdn_norm=gdn_norm, w_branch=w_branch, w_out=w_out,
             w_ffn_in=w_ffn_in, w_ffn_out=w_ffn_out, norm_final=norm_final)
    m = dict(w_ada=m_w_ada, b_ada=m_b_ada, norm_mix=m_norm_mix, norm_ffn=m_norm_ffn, w_in=m_w_in, b_merge=m_b_merge,
             hgrn_lb_logits=m_hgrn_lb_logits, hgrn_norm=m_hgrn_norm, ssm_conv_w=m_ssm_conv_w, ssm_conv_b=m_ssm_conv_b,
             ssm_dt_bias=m_ssm_dt_bias, ssm_a_log=m_ssm_a_log, ssm_d=m_ssm_d, ssm_norm=m_ssm_norm, gdn_conv_w=m_gdn_conv_w,
             gdn_dt_bias=m_gdn_dt_bias, gdn_a_log=m_gdn_a_log, gdn_norm=m_gdn_norm, w_branch=m_w_branch, w_out=m_w_out,
             w_ffn_in=m_w_ffn_in, w_ffn_out=m_w_ffn_out, norm_final=m_norm_final)
    v = dict(w_ada=v_w_ada, b_ada=v_b_ada, norm_mix=v_norm_mix, norm_ffn=v_norm_ffn, w_in=v_w_in, b_merge=v_b_merge,
             hgrn_lb_logits=v_hgrn_lb_logits, hgrn_norm=v_hgrn_norm, ssm_conv_w=v_ssm_conv_w, ssm_conv_b=v_ssm_conv_b,
             ssm_dt_bias=v_ssm_dt_bias, ssm_a_log=v_ssm_a_log, ssm_d=v_ssm_d, ssm_norm=v_ssm_norm, gdn_conv_w=v_gdn_conv_w,
             gdn_dt_bias=v_gdn_dt_bias, gdn_a_log=v_gdn_a_log, gdn_norm=v_gdn_norm, w_branch=v_w_branch, w_out=v_w_out,
             w_ffn_in=v_w_ffn_in, w_ffn_out=v_w_ffn_out, norm_final=v_norm_final)
    xi, yi, ci = _place()
    chip, me = 2 * xi + yi, 4 * xi + 2 * yi + ci
    seq = x.shape[1]

    conv_flat = jnp.concatenate([ssm_conv_w.reshape(-1), gdn_conv_w.reshape(-1)])
    n_conv = conv_flat.shape[0]
    first = _all_gather_small("gather_c_conv", _pad_rows(jnp.concatenate([c[0], conv_flat]), 16, D_MODEL))
    c_all = first[:, 0, :]
    conv_all = first[0::2].reshape(N_CHIP, -1)[:, D_MODEL:D_MODEL + n_conv]
    n_ssm = ssm_conv_w.size
    sp = dict(w)
    sp["ssm_conv_w"] = jnp.concatenate([conv_all[j, :n_ssm].reshape(ssm_conv_w.shape) for j in range(N_CHIP)], axis=2)
    sp["gdn_conv_w"] = jnp.concatenate([conv_all[j, n_ssm:].reshape(gdn_conv_w.shape) for j in range(N_CHIP)], axis=2)

    ada_cols = w_ada.shape[2]
    mod_part = _ada_fwd(c_all, w_ada, lax.dynamic_slice_in_dim(b_ada, chip * ada_cols, ada_cols, axis=1))
    mod_all = _all_gather_small("gather_mod", mod_part.reshape(DEPTH * 8, ada_cols))[0::2].reshape(N_CHIP, DEPTH, 8, ada_cols)
    mod = lax.dynamic_index_in_dim(mod_all, me, axis=2, keepdims=False).transpose(1, 0, 2).reshape(DEPTH, N_CHIP * ada_cols)
    lb = _lb_fwd(hgrn_lb_logits)

    names = [n for n, _, _, _ in BIG]

    def layer_weights(full):
        full = dict(full)
        for nm, part in zip(("768", "xbc", "qkv", "gate", "small"), _regroup_w_in(full["w_in"])):
            full["win_" + nm] = part
        full["win_all"] = jnp.concatenate([full["win_" + nm] for nm in ("768", "xbc", "qkv", "gate", "small")], axis=1)
        return full

    first = _gather_weights({n: w[n][0:1] for n in names}, chip)
    first = layer_weights({n: first[n][0] for n in names})
    started = {}
    for n, _, _, (rows, cols) in BIG:
        mine = w[n][1:].astype(BF).reshape((DEPTH - 1, rows, cols))
        started[n] = _exchange_start("gather_start_" + n, lax.dynamic_slice_in_dim(mine, ci * (rows // 2), rows // 2, axis=1),
                                     spread=True)
    rest = []

    def weights_of(layer, after):
        if layer == 0:
            return first
        if not rest:
            full = {}
            for n, shape, ax, _ in BIG:
                sems, my_half, landing, _ = started[n]
                landed = _exchange_wait("gather_wait_" + n, sems, my_half, landing, after, spread=True)
                landed = lax.dynamic_update_slice(landed, my_half[None], (chip, 0, 0, 0))
                theirs = _pair_swap("gather_share_" + n, landed)
                slabs = jnp.concatenate([jnp.where(ci == 0, landed, theirs), jnp.where(ci == 0, theirs, landed)], axis=2)
                full[n] = _join_shards(slabs, ax, shape)
            rest.extend(layer_weights({n: full[n][i] for n in names}) for i in range(DEPTH - 1))
        return rest[layer - 1]

    order = sum(tok[0, 0] for _, _, _, tok in started.values())
    loss8, d_x, d_nf, lg, grad = _device_step(x[0], loss_target[0], mod, lb, weights_of, sp, chip, ci, order)

    dmod = jnp.stack([jnp.concatenate([lg[l][n] for n in ("sh1", "sc1", "g1", "sh2", "sc2", "g2")], axis=1)[0] for l in range(DEPTH)])
    d_lb = jnp.stack([lg[l]["lb"][0] for l in range(DEPTH)])
    contrib = {
        "norm_mix": jnp.stack([lg[l]["norm_mix"][0] for l in range(DEPTH)]),
        "norm_ffn": jnp.stack([lg[l]["norm_ffn"][0] for l in range(DEPTH)]),
        "b_merge": jnp.stack([lg[l]["b_merge"][0] for l in range(DEPTH)]),
        "hgrn_lb_logits": _lb_bwd(hgrn_lb_logits, d_lb),
        "hgrn_norm": jnp.stack([lg[l]["hgrn_norm"][0] for l in range(DEPTH)]),
        "ssm_conv_w": jnp.stack([lg[l]["ssm_conv_w"] for l in range(DEPTH)]),
        "ssm_conv_b": jnp.stack([lg[l]["ssm_conv_b"][0] for l in range(DEPTH)]),
        "ssm_dt_bias": jnp.stack([lg[l]["ssm_dt_bias"][0, DT_OFF:DT_OFF + 12] for l in range(DEPTH)]),
        "ssm_a_log": jnp.stack([lg[l]["ssm_a_log"][0, DT_OFF:DT_OFF + 12] for l in range(DEPTH)]),
        "ssm_d": jnp.stack([lg[l]["ssm_d"][0, DT_OFF:DT_OFF + 12] for l in range(DEPTH)]),
        "ssm_norm": jnp.stack([lg[l]["ssm_norm"][0] for l in range(DEPTH)]),
        "gdn_conv_w": jnp.stack([lg[l]["gdn_conv_w"] for l in range(DEPTH)]),
        "gdn_dt_bias": jnp.stack([lg[l]["gdn_dt_bias"][0, GA_OFF:GA_OFF + 6] for l in range(DEPTH)]),
        "gdn_a_log": jnp.stack([lg[l]["gdn_a_log"][0, GA_OFF:GA_OFF + 6] for l in range(DEPTH)]),
        "gdn_norm": jnp.stack([lg[l]["gdn_norm"][0] for l in range(DEPTH)]),
        "norm_final": d_nf[0],
    }
    flat = jnp.concatenate([dmod.reshape(-1)] + [contrib[n].reshape(-1) for n in SMALL_REPL] + [loss8[0, 0:1]])
    small_all = _all_gather_small("gather_small_grads", _pad_rows(flat, SMALL_ROWS, D_MODEL))
    total = _sum_leading("small_grad_sum", small_all, F32).reshape(-1)
    n_mod = dmod.size
    grad["b_ada"] = total[:n_mod].reshape(b_ada.shape)
    off = n_mod
    full_small = {}
    for n in SMALL_REPL:
        full_small[n] = total[off:off + contrib[n].size].reshape(contrib[n].shape)
        off += contrib[n].size
    loss = total[off]
    for n in SMALL_REPL:
        if n in ("ssm_conv_w", "gdn_conv_w"):
            cols = w[n].shape[2]
            grad[n] = lax.dynamic_slice_in_dim(full_small[n], chip * cols, cols, axis=2)
        else:
            grad[n] = full_small[n]
    dmod_cols = lax.dynamic_slice_in_dim(small_all[:, :n_mod // D_MODEL, :].reshape(8, DEPTH, -1), chip * ada_cols, ada_cols, axis=2)
    grad["w_ada"] = _ada_bwd(c_all, dmod_cols.transpose(1, 0, 2))

    delta, new_m, new_v = {}, {}, {}
    big_names = ("w_ada",) + tuple(n for n, _, _, _ in BIG)
    for n in big_names:
        delta[n], new_m[n], new_v[n] = _adamw("adamw_" + n, w[n], grad[n], m[n], v[n])
    small_names = [n for n in WEIGHTS if n not in big_names]
    packs = [_pad_rows(jnp.concatenate([d[n].reshape(-1) for n in small_names]), 584, LANES) for d in (w, grad, m, v)]
    outs = _ew("adamw_small", _adamw_fn, packs, [F32, F32, F32])
    off = 0
    for n in small_names:
        for dst, o in zip((delta, new_m, new_v), outs):
            dst[n] = o.reshape(-1)[off:off + w[n].size].reshape(w[n].shape)
        off += w[n].size
    return (loss, d_x[None], *[grad[n] for n in WEIGHTS], *[delta[n] for n in WEIGHTS], *[new_m[n] for n in WEIGHTS],
            *[new_v[n] for n in WEIGHTS])
```

```python
import functools

import jax
import jax.numpy as jnp
from jax import lax
from jax.experimental import pallas as pl
from jax.experimental.pallas import tpu as pltpu

F32 = jnp.float32
BF = jnp.bfloat16
HI = lax.Precision.HIGHEST

D_MODEL = 1024
DEPTH = 4
CHUNK = 64
MIX_W = 768
HEAD = 128
N_HEAD6 = 6
SSM_P = 64
SSM_N = 128
CONV_CH = 1280
QKV_W = 2304
FFN_H = 2816
IN_WIDTH = 11288
NORM_EPS = 1e-6
F_MIN = 1e-30
HALO = 8
HEAD_GROUP = 6
HGRN_SUB = 8
SMALL_W = 512
LANES = 128
DT_OFF, GB_OFF, GA_OFF = 0, 12, 18

ADAM_LR, ADAM_B1, ADAM_B2, ADAM_EPS, ADAM_WD, ADAM_STEP = 0.001, 0.9, 0.999, 1e-08, 0.01, 10

VMEM_LIMIT = 56 * 1024 * 1024
TOKEN_TILE = 256


def _pc(body, **kw):
    return pl.pallas_call(body, **kw)


def _cparams(sem):
    return pltpu.CompilerParams(dimension_semantics=sem, vmem_limit_bytes=VMEM_LIMIT)


def _bdot(a, b):
    return jnp.dot(a.astype(BF), b.astype(BF), preferred_element_type=F32)


def _bdot_nt(a, b):
    return lax.dot_general(a.astype(BF), b.astype(BF), (((1,), (1,)), ((), ())), preferred_element_type=F32)


def _bdot_tn(a, b):
    return lax.dot_general(a.astype(BF), b.astype(BF), (((0,), (0,)), ((), ())), preferred_element_type=F32)


def _silu(x):
    return x * jax.nn.sigmoid(x)


def _tri_mask(n, strict=False):
    t = lax.broadcasted_iota(jnp.int32, (n, n), 0)
    s = lax.broadcasted_iota(jnp.int32, (n, n), 1)
    return (s < t) if strict else (s <= t)


def _masked_exp(diff, mask):
    return jnp.where(mask, jnp.exp(jnp.where(mask, diff, 0.0)), 0.0)


def _split_bf16(x, n):
    parts, rest = [], x
    for _ in range(n):
        p = rest.astype(BF)
        parts.append(p)
        rest = rest - p.astype(F32)
    return parts


def _tri_sum(x, reverse):
    n, w = x.shape
    t = lax.broadcasted_iota(jnp.int32, (n, n), 0)
    s = lax.broadcasted_iota(jnp.int32, (n, n), 1)
    tri = jnp.where((s >= t) if reverse else (s <= t), 1.0, 0.0).astype(BF)
    y = jnp.dot(tri, jnp.concatenate(_split_bf16(x, 3), axis=1), preferred_element_type=F32)
    return y[:, :w] + y[:, w:2 * w] + y[:, 2 * w:]


@jax.custom_vjp
def _cumsum_rows(x):
    return _tri_sum(x, False)


_cumsum_rows.defvjp(lambda x: (_tri_sum(x, False), None), lambda _, g: (_tri_sum(g, True),))


def _dot_split(a, b, transpose_a=False):
    dims = (((0,), (0,)) if transpose_a else ((1,), (0,)), ((), ()))
    a_hi, a_lo = _split_bf16(a, 2)
    b_hi, b_lo = _split_bf16(b, 2)
    w = b.shape[1]
    y = lax.dot_general(a_hi, jnp.concatenate([b_hi, b_lo], axis=1), dims, preferred_element_type=F32)
    return y[:, :w] + y[:, w:] + lax.dot_general(a_lo, b_hi, dims, preferred_element_type=F32)


def _rms(x, w):
    return x * lax.rsqrt(jnp.mean(x * x, axis=-1, keepdims=True) + NORM_EPS) * w


def _causal_conv(halo, x, w):
    ext = jnp.concatenate([halo, x], axis=0)
    n = x.shape[0]
    acc = w[0:1, :] * ext[HALO - 3:HALO - 3 + n, :]
    for i in range(1, 4):
        acc = acc + w[i:i + 1, :] * ext[HALO - 3 + i:HALO - 3 + i + n, :]
    return acc


def _unit_lower_inverses(mats):
    n = mats[0].shape[0]
    t = lax.broadcasted_iota(jnp.int32, (n, n), 0)
    s_ = lax.broadcasted_iota(jnp.int32, (n, n), 1)
    xs = [jnp.where(t == s_, 1.0, 0.0).astype(F32) for _ in mats]
    for s in range(n - 1):
        r0 = 8 * ((s + 1) // 8)
        for i, a in enumerate(mats):
            x = xs[i]
            low = x[r0:] - a[r0:, s:s + 1] * x[s:s + 1, :]
            xs[i] = low if r0 == 0 else jnp.concatenate([x[:r0], low], axis=0)
    return xs


@jax.custom_vjp
def _unit_lower_solves(mats, rhss):
    return [_dot_split(inv, r) for inv, r in zip(_unit_lower_inverses(mats), rhss)]


def _uls_fwd(mats, rhss):
    invs = _unit_lower_inverses(mats)
    xs = [_dot_split(inv, r) for inv, r in zip(invs, rhss)]
    return xs, (invs, xs)


def _uls_bwd(res, gs):
    invs, xs = res
    ys = [_dot_split(inv, g, transpose_a=True) for inv, g in zip(invs, gs)]
    das = [jnp.where(_tri_mask(CHUNK, strict=True), -_bdot_nt(y, x), 0.0) for y, x in zip(ys, xs)]
    return das, ys


_unit_lower_solves.defvjp(_uls_fwd, _uls_bwd)


def _hgrn_chunk(tiles, halos, state, consts):
    q_raw, f_raw, v_all, g_raw = tiles
    lb, norm_w = consts
    q_all = _silu(q_raw)
    f = lb + (1.0 - lb) * jax.nn.sigmoid(f_raw)
    logf = jnp.log(jnp.maximum(f, F_MIN))
    k_all = (1.0 - lb) * jax.nn.sigmoid(-f_raw)
    b_all = _cumsum_rows(logf)
    sub = HGRN_SUB
    row = lax.broadcasted_iota(jnp.int32, (sub, 1), 0)
    src_row = lax.broadcasted_iota(jnp.int32, (CHUNK, 1), 0)
    src_lane = lax.broadcasted_iota(jnp.int32, (1, CHUNK), 1)
    heads = range(N_HEAD6)
    n_sub = CHUNK // sub
    cols = [slice(h * HEAD, (h + 1) * HEAD) for h in heads]
    qs, ks, vs, bs = ([a[:, sl] for sl in cols] for a in (q_all, k_all, v_all, b_all))
    o_inter = [_bdot_nt(qs[h] * jnp.exp(bs[h]), state[h]) for h in heads]
    blocks = [[None] * n_sub for _ in heads]
    for i in range(n_sub):
        r0 = i * sub
        for h in heads:
            if i > 0:
                ref = bs[h][r0 - 1:r0, :]
                blocks[h][i] = _bdot_nt(qs[h][r0:r0 + sub] * jnp.exp(bs[h][r0:r0 + sub] - ref),
                                        ks[h] * _masked_exp(ref - bs[h], src_row < r0))
            else:
                blocks[h][i] = jnp.zeros((sub, CHUNK), F32)
    for h in heads:
        for i in range(n_sub):
            r0 = i * sub
            qi, ki, bi = qs[h][r0:r0 + sub], ks[h][r0:r0 + sub], bs[h][r0:r0 + sub]
            for s in range(sub):
                e = _masked_exp(bi - bi[s:s + 1, :], row >= s)
                col = jnp.sum(qi * ki[s:s + 1, :] * e, axis=1, keepdims=True)
                blocks[h][i] = jnp.where(src_lane == r0 + s, col, blocks[h][i])
    os_ = [_bdot(jnp.concatenate(blocks[h], axis=0), vs[h]) + o_inter[h] for h in heads]
    ends = [bs[h][CHUNK - 1:CHUNK, :] for h in heads]
    new_states = [state[h] * jnp.exp(ends[h]) + _bdot_tn(vs[h], ks[h] * jnp.exp(ends[h] - bs[h])) for h in heads]
    outs = [_rms(os_[h], norm_w) * _silu(g_raw[:, cols[h]]) for h in heads]
    return (jnp.concatenate(outs, axis=1),), jnp.stack(new_states)


def _ssd_chunk(tiles, halos, state, consts):
    z, xbc_raw, small = tiles
    (halo,) = halos
    conv_w, conv_b, dt_bias, a_log, d_skip, norm_w = consts
    xbc = _silu(_causal_conv(halo, xbc_raw, conv_w) + conv_b)
    xs, bm, cm = xbc[:, :MIX_W], xbc[:, MIX_W:MIX_W + 2 * SSM_N], xbc[:, MIX_W + 2 * SSM_N:]
    dt = jax.nn.softplus(small + dt_bias)
    cum = _cumsum_rows(-jnp.exp(a_log) * dt)
    cum_t2 = jnp.concatenate([cum, cum], axis=0).T
    lane = lax.broadcasted_iota(jnp.int32, (1, LANES), 1)
    first = lane < SSM_P
    hm0 = jnp.where(first, 1.0, 0.0).astype(F32)
    hm1 = 1.0 - hm0
    src = jnp.where(first, lane, lane - SSM_P)
    tri2 = src <= lax.broadcasted_iota(jnp.int32, (CHUNK, 1), 0)
    pick = lambda a, b: jnp.where(first, a, b)
    bgs = [bm[:, g * SSM_N:(g + 1) * SSM_N] for g in range(2)]
    cgs = [cm[:, g * SSM_N:(g + 1) * SSM_N] for g in range(2)]
    gmats = [_bdot_nt(cgs[g], jnp.concatenate([bgs[g], bgs[g]], axis=0)) for g in range(2)]
    pairs = range(6)
    xps = [xs[:, p * LANES:(p + 1) * LANES] for p in pairs]
    c0s = [cum[:, 2 * p:2 * p + 1] for p in pairs]
    c1s = [cum[:, 2 * p + 1:2 * p + 2] for p in pairs]
    e0s = [cum[CHUNK - 1:CHUNK, 2 * p:2 * p + 1] for p in pairs]
    e1s = [cum[CHUNK - 1:CHUNK, 2 * p + 1:2 * p + 2] for p in pairs]
    segs = [_masked_exp(pick(c0s[p], c1s[p]) - pick(cum_t2[2 * p:2 * p + 1, :], cum_t2[2 * p + 1:2 * p + 2, :]), tri2)
            for p in pairs]
    vms = []
    for p in pairs:
        v = xps[p] * pick(dt[:, 2 * p:2 * p + 1], dt[:, 2 * p + 1:2 * p + 2])
        vms.append(jnp.concatenate([v * hm0, v * hm1], axis=0))
    y_intra = [_bdot(gmats[p // 3] * segs[p], vms[p]) for p in pairs]
    y_inter = [_bdot(jnp.concatenate([cgs[p // 3] * jnp.exp(c0s[p]), cgs[p // 3] * jnp.exp(c1s[p])], axis=1),
                     jnp.concatenate([state[p] * hm0, state[p] * hm1], axis=0)) for p in pairs]
    new_states = [_bdot_tn(jnp.concatenate([bgs[p // 3] * jnp.exp(e0s[p] - c0s[p]),
                                            bgs[p // 3] * jnp.exp(e1s[p] - c1s[p])], axis=0), vms[p])
                  + state[p] * pick(jnp.exp(e0s[p]), jnp.exp(e1s[p])) for p in pairs]
    ys = [y_intra[p] + y_inter[p] + pick(d_skip[:, 2 * p:2 * p + 1], d_skip[:, 2 * p + 1:2 * p + 2]) * xps[p] for p in pairs]
    y = jnp.concatenate(ys, axis=1) * _silu(z)
    gw = MIX_W // 2
    y = jnp.concatenate([_rms(y[:, g * gw:(g + 1) * gw], norm_w[:, g * gw:(g + 1) * gw]) for g in range(2)], axis=1)
    return (y,), jnp.stack(new_states)


def _gdn_chunk(tiles, halos, state, consts):
    qkv_raw, z, small = tiles
    (halo,) = halos
    conv_w, dt_bias, a_log, norm_w = consts
    qkv = _silu(_causal_conv(halo, qkv_raw, conv_w))
    beta_all = jax.nn.sigmoid(small)
    cum = _cumsum_rows(-jnp.exp(a_log) * jax.nn.softplus(small + dt_bias))
    cum_t = cum.T
    tri, tri_strict = _tri_mask(CHUNK), _tri_mask(CHUNK, strict=True)

    def group(hs):
        n = range(len(hs))
        qs, ks, betas, cs, ces, decays, rhss = [], [], [], [], [], [], []
        for h in hs:
            q = qkv[:, h * HEAD:(h + 1) * HEAD]
            k = qkv[:, MIX_W + h * HEAD:MIX_W + (h + 1) * HEAD]
            v = qkv[:, 2 * MIX_W + h * HEAD:2 * MIX_W + (h + 1) * HEAD]
            q = q * lax.rsqrt(jnp.sum(q * q, axis=-1, keepdims=True) + NORM_EPS) * (HEAD ** -0.5)
            k = k * lax.rsqrt(jnp.sum(k * k, axis=-1, keepdims=True) + NORM_EPS)
            beta = beta_all[:, GB_OFF + h:GB_OFF + h + 1]
            c, c_t = cum[:, GA_OFF + h:GA_OFF + h + 1], cum_t[GA_OFF + h:GA_OFF + h + 1, :]
            qs.append(q), ks.append(k), betas.append(beta), cs.append(c)
            ces.append(cum[CHUNK - 1:CHUNK, GA_OFF + h:GA_OFF + h + 1])
            decays.append(_masked_exp(c - c_t, tri))
            rhss.append(jnp.concatenate([v * beta, k * (beta * jnp.exp(c))], axis=1))
        sts = [state[h] for h in hs]
        qk_kks = [_bdot_nt(jnp.concatenate([qs[i], ks[i]], axis=0), ks[i]) for i in n]
        sols = _unit_lower_solves([jnp.where(tri_strict, betas[i] * qk_kks[i][CHUNK:] * decays[i], 0.0) for i in n], rhss)
        on_states = [_bdot(jnp.concatenate([sols[i][:, HEAD:], qs[i] * jnp.exp(cs[i])], axis=0), sts[i]) for i in n]
        us = [sols[i][:, :HEAD] - on_states[i][:CHUNK] for i in n]
        os_ = [on_states[i][CHUNK:] + _bdot(qk_kks[i][:CHUNK] * decays[i], us[i]) for i in n]
        new = [jnp.exp(ces[i]) * sts[i] + _bdot_tn(ks[i] * jnp.exp(ces[i] - cs[i]), us[i]) for i in n]
        outs = [_rms(os_[i], norm_w) * _silu(z[:, h * HEAD:(h + 1) * HEAD]) for i, h in enumerate(hs)]
        return outs, new

    outs, new_states = [], []
    for h0 in range(0, N_HEAD6, HEAD_GROUP):
        o, s = group(list(range(h0, h0 + HEAD_GROUP)))
        outs += o
        new_states += s
    return (jnp.concatenate(outs, axis=1),), jnp.stack(new_states)


def _scan_fwd(name, fn, tiled, halo_idx, consts, out_width, state_shape):
    seq = tiled[0][0].shape[0]
    nc = seq // CHUNK
    n_t, n_h, n_c = len(tiled), len(halo_idx), len(consts)

    def body(*refs):
        t_refs, h_refs, c_refs = refs[:n_t], refs[n_t:n_t + n_h], refs[n_t + n_h:n_t + n_h + n_c]
        y_ref, save_ref, st_ref = refs[n_t + n_h + n_c:]
        i = pl.program_id(0)

        @pl.when(i == 0)
        def _():
            st_ref[...] = jnp.zeros_like(st_ref)

        flag = jnp.where(i > 0, 1.0, 0.0).astype(F32)
        st = st_ref[...]
        (y,), new = fn([r[...] for r in t_refs], [r[...] * flag for r in h_refs], st, [r[...] for r in c_refs])
        save_ref[0] = st
        y_ref[...] = y.astype(y_ref.dtype)
        st_ref[...] = new

    in_specs = [pl.BlockSpec((CHUNK, w), functools.partial(lambda i, cb: (i, cb), cb=cb)) for _, w, cb in tiled]
    in_specs += [pl.BlockSpec((HALO, tiled[j][1]),
                              functools.partial(lambda i, cb: (jnp.maximum(i * (CHUNK // HALO) - 1, 0), cb), cb=tiled[j][2]))
                 for j in halo_idx]
    in_specs += [pl.BlockSpec(c.shape, functools.partial(lambda i, nd: (0,) * nd, nd=c.ndim)) for c in consts]
    zeros = (0,) * len(state_shape)
    return _pc(
        body, name=name, grid=(nc,), in_specs=in_specs,
        out_specs=(pl.BlockSpec((CHUNK, out_width), lambda i: (i, 0)),
                   pl.BlockSpec((1,) + state_shape, lambda i: (i,) + zeros)),
        out_shape=(jax.ShapeDtypeStruct((seq, out_width), BF), jax.ShapeDtypeStruct((nc,) + state_shape, F32)),
        scratch_shapes=[pltpu.VMEM(state_shape, F32)],
        compiler_params=_cparams(("arbitrary",)),
    )(*[t[0] for t in tiled], *[tiled[j][0] for j in halo_idx], *consts)


def _scan_bwd(name, fn, tiled, halo_idx, consts, saved, dy, dtile_dtypes, extra=None):
    seq = tiled[0][0].shape[0]
    nc = seq // CHUNK
    n_t, n_h, n_c = len(tiled), len(halo_idx), len(consts)
    state_shape = saved.shape[1:]
    n_x = 0 if extra is None else 1

    def body(*refs):
        t_refs, h_refs, c_refs = refs[:n_t], refs[n_t:n_t + n_h], refs[n_t + n_h:n_t + n_h + n_c]
        pos = n_t + n_h + n_c
        save_ref, dy_ref = refs[pos], refs[pos + 1]
        x_refs = refs[pos + 2:pos + 2 + n_x]
        pos += 2 + n_x
        dt_refs, dc_refs = refs[pos:pos + n_t], refs[pos + n_t:pos + n_t + n_c]
        dst_ref = refs[pos + n_t + n_c]
        carry_refs = refs[pos + n_t + n_c + 1:]
        i = pl.program_id(0)

        @pl.when(i == 0)
        def _():
            dst_ref[...] = jnp.zeros_like(dst_ref)
            for r in carry_refs:
                r[...] = jnp.zeros_like(r)
            for r in dc_refs:
                r[...] = jnp.zeros_like(r)

        flag = jnp.where(i < nc - 1, 1.0, 0.0).astype(F32)
        tiles = [r[...] for r in t_refs]
        halos = [r[...] * flag for r in h_refs]
        cvals = [r[...] for r in c_refs]
        _, vjp = jax.vjp(fn, tiles, halos, save_ref[0], cvals)
        d_tiles, d_halos, d_state, d_consts = vjp(((dy_ref[...].astype(F32),), dst_ref[...]))
        dst_ref[...] = d_state
        for r, g in zip(dc_refs, d_consts):
            r[...] += g
        for j, (r, g) in enumerate(zip(dt_refs, d_tiles)):
            if extra is not None and extra[0] == j:
                g = g + x_refs[0][...].astype(F32)
            r[...] = g.astype(r.dtype)
            if j in halo_idx:
                cr = carry_refs[halo_idx.index(j)]
                r[CHUNK - HALO:CHUNK, :] = (g[CHUNK - HALO:CHUNK, :] + cr[...]).astype(r.dtype)
                cr[...] = d_halos[halo_idx.index(j)] * flag

    rev = lambda i: nc - 1 - i
    in_specs = [pl.BlockSpec((CHUNK, w), functools.partial(lambda i, cb: (rev(i), cb), cb=cb)) for _, w, cb in tiled]
    in_specs += [pl.BlockSpec((HALO, tiled[j][1]),
                              functools.partial(lambda i, cb: (jnp.maximum(rev(i) * (CHUNK // HALO) - 1, 0), cb), cb=tiled[j][2]))
                 for j in halo_idx]
    in_specs += [pl.BlockSpec(c.shape, functools.partial(lambda i, nd: (0,) * nd, nd=c.ndim)) for c in consts]
    zeros = (0,) * len(state_shape)
    in_specs += [pl.BlockSpec((1,) + state_shape, lambda i: (rev(i),) + zeros),
                 pl.BlockSpec((CHUNK, dy.shape[1]), lambda i: (rev(i), 0))]
    args = [t[0] for t in tiled] + [tiled[j][0] for j in halo_idx] + list(consts) + [saved, dy]
    if extra is not None:
        in_specs.append(pl.BlockSpec((CHUNK, extra[1].shape[1]), lambda i: (rev(i), 0)))
        args.append(extra[1])
    out_specs = [pl.BlockSpec((CHUNK, w), lambda i: (rev(i), 0)) for _, w, _ in tiled]
    out_specs += [pl.BlockSpec(c.shape, functools.partial(lambda i, nd: (0,) * nd, nd=c.ndim)) for c in consts]
    out_shape = [jax.ShapeDtypeStruct((seq, w), dtd) for (_, w, _), dtd in zip(tiled, dtile_dtypes)]
    out_shape += [jax.ShapeDtypeStruct(c.shape, F32) for c in consts]
    scratch = [pltpu.VMEM(state_shape, F32)] + [pltpu.VMEM((HALO, tiled[j][1]), F32) for j in halo_idx]
    return _pc(body, name=name, grid=(nc,), in_specs=in_specs, out_specs=tuple(out_specs), out_shape=tuple(out_shape),
               scratch_shapes=scratch, compiler_params=_cparams(("arbitrary",)))(*args)


def _tile_fwd(name, fn, tiled, consts, outs, tm=TOKEN_TILE):
    seq = tiled[0][0].shape[0]
    n_t, n_c = len(tiled), len(consts)

    def body(*refs):
        res = fn(*[r[...] for r in refs[:n_t + n_c]])
        for r, y in zip(refs[n_t + n_c:], res):
            r[...] = y.astype(r.dtype)

    in_specs = [pl.BlockSpec((tm, w), functools.partial(lambda i, cb: (i, cb), cb=cb)) for _, w, cb in tiled]
    in_specs += [pl.BlockSpec(c.shape, functools.partial(lambda i, nd: (0,) * nd, nd=c.ndim)) for c in consts]
    return _pc(body, name=name, grid=(seq // tm,), in_specs=in_specs,
               out_specs=tuple(pl.BlockSpec((tm, w), lambda i: (i, 0)) for w, _ in outs),
               out_shape=tuple(jax.ShapeDtypeStruct((seq, w), dtp) for w, dtp in outs),
               compiler_params=_cparams(("arbitrary",)))(*[t[0] for t in tiled], *consts)


def _tile_bwd(name, fn, tiled, consts, douts, dtile_dtypes, add_to=None, tm=TOKEN_TILE):
    seq = tiled[0][0].shape[0]
    n_t, n_c, n_o = len(tiled), len(consts), len(douts)
    n_x = 0 if add_to is None else 1
    keep = [j for j, dtp in enumerate(dtile_dtypes) if dtp is not None]

    def body(*refs):
        vals = [r[...].astype(F32) for r in refs[:n_t + n_c]]
        pos = n_t + n_c
        g_refs, x_refs = refs[pos:pos + n_o], refs[pos + n_o:pos + n_o + n_x]
        pos += n_o + n_x
        dt_refs, dc_refs = refs[pos:pos + len(keep)], refs[pos + len(keep):]
        i = pl.program_id(0)

        @pl.when(i == 0)
        def _():
            for r in dc_refs:
                r[...] = jnp.zeros_like(r)

        _, vjp = jax.vjp(fn, *vals)
        cts = vjp(tuple(g[...].astype(F32) for g in g_refs))
        for r, j in zip(dt_refs, keep):
            g = cts[j]
            if add_to is not None and add_to[0] == j:
                g = g + x_refs[0][...].astype(F32)
            r[...] = g.astype(r.dtype)
        for r, g in zip(dc_refs, cts[n_t:]):
            r[...] += g

    in_specs = [pl.BlockSpec((tm, w), functools.partial(lambda i, cb: (i, cb), cb=cb)) for _, w, cb in tiled]
    in_specs += [pl.BlockSpec(c.shape, functools.partial(lambda i, nd: (0,) * nd, nd=c.ndim)) for c in consts]
    in_specs += [pl.BlockSpec((tm, g.shape[1]), lambda i: (i, 0)) for g in douts]
    args = [t[0] for t in tiled] + list(consts) + list(douts)
    if add_to is not None:
        in_specs.append(pl.BlockSpec((tm, add_to[1].shape[1]), lambda i: (i, 0)))
        args.append(add_to[1])
    out_specs = [pl.BlockSpec((tm, tiled[j][1]), lambda i: (i, 0)) for j in keep]
    out_specs += [pl.BlockSpec(c.shape, functools.partial(lambda i, nd: (0,) * nd, nd=c.ndim)) for c in consts]
    out_shape = [jax.ShapeDtypeStruct((seq, tiled[j][1]), dtile_dtypes[j]) for j in keep]
    out_shape += [jax.ShapeDtypeStruct(c.shape, F32) for c in consts]
    return _pc(body, name=name, grid=(seq // tm,), in_specs=in_specs, out_specs=tuple(out_specs),
               out_shape=tuple(out_shape), compiler_params=_cparams(("arbitrary",)))(*args)


def _lnmod_fn(x, nw, sc, sh):
    return (_rms(x, nw) * (1.0 + sc) + sh,)


def _resid_fn(x, o, g):
    return (x + (1.0 + g) * o,)


def _swiglu_fn(gu):
    return (_silu(gu[:, :FFN_H]) * gu[:, FFN_H:],)


def _merge_fn(yh, ys, yg, logits, wb, b_merge):
    gates = jax.nn.sigmoid(logits + b_merge)
    acc = None
    for n, y in enumerate((yh, ys, yg)):
        t = gates[:, n * D_MODEL:(n + 1) * D_MODEL] * _bdot(y, wb[n])
        acc = t if acc is None else acc + t
    return (acc,)


MM_VMEM_BUDGET = 40 * 1024 * 1024
MM_TILE_CAP = 1024
MM_K_CAP = 4096


def _divisor(n, cap, unit=LANES):
    best = None
    for d in range(unit, min(n, cap) + 1, unit):
        if n % d == 0:
            best = d
    return n if best is None else best


def _mm_tiles(m, n, k, out_bytes):
    tk = k if k <= MM_K_CAP else _divisor(k, 3072)
    tm, tn = _divisor(m, MM_TILE_CAP), _divisor(n, MM_TILE_CAP + MM_TILE_CAP // 2)

    def need(tm_, tn_):
        acc = tm_ * tn_ * 4 if tk < k else 0
        return 2 * 2 * tk * (tm_ + tn_) + acc + 2 * tm_ * tn_ * out_bytes

    while need(tm, tn) > MM_VMEM_BUDGET:
        if tn >= tm and _divisor(n, tn - LANES) < tn:
            tn = _divisor(n, tn - LANES)
        elif _divisor(m, tm - LANES) < tm:
            tm = _divisor(m, tm - LANES)
        else:
            break
    return tm, tn, tk


def _mm(a, b, mode, out_dtype, name):
    if mode == "nn":
        (m, k), n = a.shape, b.shape[1]
    elif mode == "nt":
        (m, k), n = a.shape, b.shape[0]
    else:
        (k, m), n = a.shape, b.shape[1]
    tm, tn, tk = _mm_tiles(m, n, k, jnp.dtype(out_dtype).itemsize)
    nk = k // tk
    dims = {"nn": ((1,), (0,)), "nt": ((1,), (1,)), "tn": ((0,), (0,))}[mode]

    def body_one(a_ref, b_ref, o_ref):
        o_ref[...] = lax.dot_general(a_ref[...], b_ref[...], (dims, ((), ())), preferred_element_type=F32).astype(o_ref.dtype)

    def body_acc(a_ref, b_ref, o_ref, acc_ref):
        kk = pl.program_id(2)

        @pl.when(kk == 0)
        def _():
            acc_ref[...] = jnp.zeros_like(acc_ref)

        acc_ref[...] += lax.dot_general(a_ref[...], b_ref[...], (dims, ((), ())), preferred_element_type=F32)

        @pl.when(kk == nk - 1)
        def _():
            o_ref[...] = acc_ref[...].astype(o_ref.dtype)

    a_spec = pl.BlockSpec((tk, tm), lambda i, j, kk: (kk, i)) if mode == "tn" else pl.BlockSpec((tm, tk), lambda i, j, kk: (i, kk))
    b_spec = pl.BlockSpec((tn, tk), lambda i, j, kk: (j, kk)) if mode == "nt" else pl.BlockSpec((tk, tn), lambda i, j, kk: (kk, j))
    return _pc(body_one if nk == 1 else body_acc, name=name, grid=(m // tm, n // tn, nk), in_specs=[a_spec, b_spec],
               out_specs=pl.BlockSpec((tm, tn), lambda i, j, kk: (i, j)),
               out_shape=jax.ShapeDtypeStruct((m, n), out_dtype),
               scratch_shapes=[] if nk == 1 else [pltpu.VMEM((tm, tn), F32)],
               compiler_params=_cparams(("parallel", "parallel", "arbitrary")))(a.astype(BF), b.astype(BF))


def _final_loss(x, tgt, norm_final, tm=TOKEN_TILE):
    seq = x.shape[0]

    def fn(xv, nf, tv):
        err = jnp.square(_rms(xv, nf) - tv)
        return 0.5 * jnp.sum(jnp.mean(err, axis=-1))

    def body(x_ref, t_ref, nf_ref, loss_ref, dx_ref, dnf_ref):
        i = pl.program_id(0)

        @pl.when(i == 0)
        def _():
            loss_ref[...] = jnp.zeros_like(loss_ref)
            dnf_ref[...] = jnp.zeros_like(dnf_ref)

        val, vjp = jax.vjp(functools.partial(fn, tv=t_ref[...]), x_ref[...], nf_ref[...])
        dx, dnf = vjp(jnp.ones((), F32))
        dx_ref[...] = dx
        dnf_ref[...] += dnf
        loss_ref[...] += jnp.broadcast_to(val, loss_ref.shape)

    return _pc(body, name="final_loss", grid=(seq // tm,),
               in_specs=[pl.BlockSpec((tm, D_MODEL), lambda i: (i, 0)), pl.BlockSpec((tm, D_MODEL), lambda i: (i, 0)),
                         pl.BlockSpec((1, D_MODEL), lambda i: (0, 0))],
               out_specs=(pl.BlockSpec((8, LANES), lambda i: (0, 0)), pl.BlockSpec((tm, D_MODEL), lambda i: (i, 0)),
                          pl.BlockSpec((1, D_MODEL), lambda i: (0, 0))),
               out_shape=(jax.ShapeDtypeStruct((8, LANES), F32), jax.ShapeDtypeStruct((seq, D_MODEL), F32),
                          jax.ShapeDtypeStruct((1, D_MODEL), F32)),
               compiler_params=_cparams(("arbitrary",)))(x, tgt, norm_final)


def _ada_fwd(c_all, w_ada, b_ada_cols):
    n_l, _, cols = w_ada.shape

    def body(c_ref, w_ref, b_ref, o_ref):
        o_ref[0] = jnp.dot(_silu(c_ref[...]), w_ref[0], preferred_element_type=F32, precision=HI) + b_ref[0]

    return _pc(body, name="ada_fwd", grid=(n_l,),
               in_specs=[pl.BlockSpec((8, D_MODEL), lambda l: (0, 0)), pl.BlockSpec((1, D_MODEL, cols), lambda l: (l, 0, 0)),
                         pl.BlockSpec((1, 1, cols), lambda l: (l, 0, 0))],
               out_specs=pl.BlockSpec((1, 8, cols), lambda l: (l, 0, 0)),
               out_shape=jax.ShapeDtypeStruct((n_l, 8, cols), F32),
               compiler_params=_cparams(("arbitrary",)))(c_all, w_ada, b_ada_cols.reshape(n_l, 1, cols))


def _ada_bwd(c_all, dmod_cols):
    n_l, _, cols = dmod_cols.shape

    def body(c_ref, g_ref, o_ref):
        o_ref[0] = lax.dot_general(_silu(c_ref[...]), g_ref[0], (((0,), (0,)), ((), ())), preferred_element_type=F32,
                                   precision=HI)

    return _pc(body, name="ada_bwd", grid=(n_l,),
               in_specs=[pl.BlockSpec((8, D_MODEL), lambda l: (0, 0)), pl.BlockSpec((1, 8, cols), lambda l: (l, 0, 0))],
               out_specs=pl.BlockSpec((1, D_MODEL, cols), lambda l: (l, 0, 0)),
               out_shape=jax.ShapeDtypeStruct((n_l, D_MODEL, cols), F32),
               compiler_params=_cparams(("arbitrary",)))(c_all, dmod_cols)


def _lb_fn(logits):
    e = jnp.exp(logits - jnp.max(logits, axis=0, keepdims=True))
    p = e / jnp.sum(e, axis=0, keepdims=True)
    r = lax.broadcasted_iota(jnp.int32, (DEPTH, 1), 0)
    lb = jnp.zeros_like(p)
    for j in range(1, DEPTH):
        lb = lb + jnp.where(r >= j, p[j:j + 1, :], 0.0)
    return lb


def _lb_fwd(logits):
    def body(l_ref, o_ref):
        o_ref[...] = _lb_fn(l_ref[...])

    return _pc(body, name="lb_fwd", out_shape=jax.ShapeDtypeStruct(logits.shape, F32))(logits)


def _lb_bwd(logits, dlb):
    def body(l_ref, g_ref, o_ref):
        _, vjp = jax.vjp(_lb_fn, l_ref[...])
        o_ref[...] = vjp(g_ref[...])[0]

    return _pc(body, name="lb_bwd", out_shape=jax.ShapeDtypeStruct(logits.shape, F32))(logits, dlb)


def _rows_for(n_rows, n_cols):
    r = 8
    while r * 2 <= n_rows and n_rows % (r * 2) == 0 and r * 2 * n_cols <= 256 * 1024:
        r *= 2
    return r if n_rows % r == 0 else n_rows


def _ew(name, fn, ins, out_dtypes):
    n_rows, n_cols = ins[0].shape
    tr = _rows_for(n_rows, n_cols)
    n_in = len(ins)

    def body(*refs):
        res = fn(*[r[...] for r in refs[:n_in]])
        for r, y in zip(refs[n_in:], res):
            r[...] = y.astype(r.dtype)

    spec = pl.BlockSpec((tr, n_cols), lambda i: (i, 0))
    return _pc(body, name=name, grid=(n_rows // tr,), in_specs=[spec] * n_in, out_specs=tuple([spec] * len(out_dtypes)),
               out_shape=tuple(jax.ShapeDtypeStruct((n_rows, n_cols), d) for d in out_dtypes),
               compiler_params=_cparams(("arbitrary",)))(*ins)


def _adamw_fn(w, g, m, v):
    m = ADAM_B1 * m + (1.0 - ADAM_B1) * g
    v = ADAM_B2 * v + (1.0 - ADAM_B2) * jnp.square(g)
    m_hat = m / (1.0 - ADAM_B1 ** ADAM_STEP)
    v_hat = v / (1.0 - ADAM_B2 ** ADAM_STEP)
    return -ADAM_LR * (m_hat / (jnp.sqrt(v_hat) + ADAM_EPS) + ADAM_WD * w), m, v


def _adamw(name, w, g, m, v):
    shape = w.shape
    two = (-1, shape[-1])
    d, nm, nv = _ew(name, _adamw_fn, [a.reshape(two) for a in (w, g, m, v)], [F32, F32, F32])
    return d.reshape(shape), nm.reshape(shape), nv.reshape(shape)


def _sum_leading(name, a, out_dtype):
    n, n_rows, n_cols = a.shape
    tr = _rows_for(n_rows, n_cols)

    def body(a_ref, o_ref):
        acc = a_ref[0].astype(F32)
        for j in range(1, n):
            acc = acc + a_ref[j].astype(F32)
        o_ref[...] = acc.astype(o_ref.dtype)

    return _pc(body, name=name, grid=(n_rows // tr,), in_specs=[pl.BlockSpec((n, tr, n_cols), lambda i: (0, i, 0))],
               out_specs=pl.BlockSpec((tr, n_cols), lambda i: (i, 0)),
               out_shape=jax.ShapeDtypeStruct((n_rows, n_cols), out_dtype),
               compiler_params=_cparams(("arbitrary",)))(a)


MESH = pl.DeviceIdType.MESH
ANY = pl.BlockSpec(memory_space=pl.ANY)


def _place():
    return lax.axis_index("x"), lax.axis_index("y"), lax.axis_index("c")


def _all_gather_small(name, a):
    m_per, n = a.shape

    def body(x_ref, out_ref, send_sems, recv_sems, local_sem):
        x, y, c = _place()
        me, sibling = (x, y, c), (x, y, 1 - c)
        chips = [(1 - x, y), (x, 1 - y), (1 - x, 1 - y)]

        def rows(px, py, pc):
            return out_ref.at[pl.ds((4 * px + 2 * py + pc) * m_per, m_per), :]

        def copy(k, block, to, src=None):
            return pltpu.make_async_remote_copy(src_ref=rows(*block) if src is None else src, dst_ref=rows(*block),
                                                send_sem=send_sems.at[k], recv_sem=recv_sems.at[k], device_id=to,
                                                device_id_type=MESH)

        mine = pltpu.make_async_copy(x_ref, rows(*me), local_sem)
        mine.start()
        first = [copy(0, me, sibling, src=x_ref)]
        first += [copy(1 + j, me, (*chip, c), src=x_ref) for j, chip in enumerate(chips)]
        for cp in first:
            cp.start()
        passed = [copy(4 + j, (*chip, c), sibling) for j, chip in enumerate(chips)]
        for j, chip in enumerate(chips):
            copy(1 + j, (*chip, c), me).wait_recv()
            passed[j].start()
        copy(0, sibling, me).wait_recv()
        for j, chip in enumerate(chips):
            copy(4 + j, (*chip, 1 - c), me).wait_recv()
        for cp in first + passed:
            cp.wait_send()
        mine.wait()

    out = _pc(body, name=name, out_shape=jax.ShapeDtypeStruct((8 * m_per, n), a.dtype),
              in_specs=[pl.BlockSpec(memory_space=pltpu.VMEM)], out_specs=pl.BlockSpec(memory_space=pltpu.VMEM),
              scratch_shapes=[pltpu.SemaphoreType.DMA((7,)), pltpu.SemaphoreType.DMA((7,)), pltpu.SemaphoreType.DMA],
              compiler_params=pltpu.CompilerParams(vmem_limit_bytes=VMEM_LIMIT))(a)
    return out.reshape(8, m_per, n)


def _chip_gather(name, pack):
    n_l, n_r, n_c = pack.shape
    half = n_r // 2

    def body(p_ref, o_ref, send_sems, recv_sems):
        x, y, c = _place()
        sibling = (x, y, 1 - c)
        chips = [(1 - x, y), (x, 1 - y), (1 - x, 1 - y)]

        def slab(px, py, pc):
            return o_ref.at[2 * px + py, :, pl.ds(pc * half, half), :]

        def copy(k, src, dst, to):
            return pltpu.make_async_remote_copy(src_ref=src, dst_ref=dst, send_sem=send_sems.at[k], recv_sem=recv_sems.at[k],
                                                device_id=to, device_id_type=MESH)

        first = [copy(j, p_ref.at[:, pl.ds(c * half, half), :], slab(x, y, c), (*chip, c)) for j, chip in enumerate(chips)]
        for cp in first:
            cp.start()
        passed = [copy(3 + j, slab(*chip, c), slab(*chip, c), sibling) for j, chip in enumerate(chips)]
        for j, chip in enumerate(chips):
            copy(j, slab(*chip, c), slab(*chip, c), (*chip, c)).wait_recv()
            passed[j].start()
        for j, chip in enumerate(chips):
            copy(3 + j, slab(*chip, 1 - c), slab(*chip, 1 - c), sibling).wait_recv()
        for cp in first + passed:
            cp.wait_send()

    return _pc(body, name=name, out_shape=jax.ShapeDtypeStruct((4, n_l, n_r, n_c), pack.dtype), in_specs=[ANY], out_specs=ANY,
               scratch_shapes=[pltpu.SemaphoreType.DMA((6,)), pltpu.SemaphoreType.DMA((6,))])(pack)


def _pair_swap(name, give):
    def body(g_ref, o_ref, send_sem, recv_sem):
        x, y, c = _place()
        cp = pltpu.make_async_remote_copy(src_ref=g_ref, dst_ref=o_ref, send_sem=send_sem, recv_sem=recv_sem,
                                          device_id=(x, y, 1 - c), device_id_type=MESH)
        cp.start()
        cp.wait()

    return _pc(body, name=name, out_shape=jax.ShapeDtypeStruct(give.shape, give.dtype), in_specs=[ANY], out_specs=ANY,
               scratch_shapes=[pltpu.SemaphoreType.DMA, pltpu.SemaphoreType.DMA])(give)


def _chip_exchange(name, parts):
    def body(p_ref, o_ref, send_sems, recv_sems):
        x, y, c = _place()
        me = 2 * x + y
        chips = [(1 - x, y), (x, 1 - y), (1 - x, 1 - y)]

        def copy(k, src, dst, to):
            return pltpu.make_async_remote_copy(src_ref=src, dst_ref=dst, send_sem=send_sems.at[k], recv_sem=recv_sems.at[k],
                                                device_id=to, device_id_type=MESH)

        sends = [copy(j, p_ref.at[2 * px + py], o_ref.at[me], (px, py, c)) for j, (px, py) in enumerate(chips)]
        for cp in sends:
            cp.start()
        for j, (px, py) in enumerate(chips):
            copy(j, p_ref.at[2 * px + py], o_ref.at[2 * px + py], (px, py, c)).wait_recv()
        for cp in sends:
            cp.wait_send()

    return _pc(body, name=name, out_shape=jax.ShapeDtypeStruct(parts.shape, parts.dtype), in_specs=[ANY], out_specs=ANY,
               scratch_shapes=[pltpu.SemaphoreType.DMA((3,)), pltpu.SemaphoreType.DMA((3,))])(parts)


N_CHIP = 4
BIG = (("w_in", (1024, 2822), 1, (1024, 2822)), ("w_branch", (3, 768, 256), 2, (2304, 256)),
       ("w_out", (256, 1024), 0, (256, 1024)), ("w_ffn_in", (1024, 1408), 1, (1024, 1408)),
       ("w_ffn_out", (704, 1024), 0, (704, 1024)))
G768 = ((0, 3072), (3072, 3840), (7436, 8204))
GXBC, GQKV, GGATE = (3840, 5120), (5132, 7436), (8216, 11288)
GSMALL = ((5120, 5132), (8204, 8210), (8210, 8216))
W768, WXBC, WGATE = 4608, CONV_CH, 3 * D_MODEL
IN_PAD = W768 + WXBC + QKV_W + WGATE + SMALL_W


def _join_shards(slabs, axis, shard_shape):
    n_l = slabs.shape[1]
    parts = [slabs[j].reshape((n_l,) + shard_shape) for j in range(N_CHIP)]
    return jnp.concatenate(parts, axis=axis + 1)


def _split_shards(full, axis, rows_cols):
    n_l = full.shape[0]
    size = full.shape[axis + 1] // N_CHIP
    return jnp.stack([lax.slice_in_dim(full, j * size, (j + 1) * size, axis=axis + 1).reshape((n_l,) + rows_cols)
                      for j in range(N_CHIP)])


def _gather_weights(w, chip, big=BIG):
    out = {}
    for n, shape, ax, rc in big:
        n_l = w[n].shape[0]
        mine = w[n].astype(BF).reshape((n_l,) + rc)
        slabs = lax.dynamic_update_slice(_chip_gather("gather_" + n, mine), mine[None], (chip, 0, 0, 0))
        out[n] = _join_shards(slabs, ax, shape)
    return out


def _pair_stage(full_grads, core, big=BIG):
    out = {}
    for n, _, ax, (rows, cols) in big:
        n_l = full_grads[n].shape[0]
        slabs = _split_shards(full_grads[n], ax, (rows, cols))
        half = rows // 2
        keep = lax.dynamic_slice_in_dim(slabs, core * half, half, axis=2).reshape(-1, cols)
        give = lax.dynamic_slice_in_dim(slabs, (1 - core) * half, half, axis=2).reshape(-1, cols)
        got = _pair_swap("pair_swap_" + n, give)
        (pair_sum,) = _ew("pair_sum_" + n, lambda a, b: (a.astype(F32) + b.astype(F32),), [keep, got], [BF])
        out[n] = pair_sum.reshape(N_CHIP, n_l * half, cols)
    return out


def _own_slab(landed, pair_sum, chip):
    return lax.dynamic_update_slice(landed, lax.dynamic_slice_in_dim(pair_sum, chip, 1, axis=0), (chip, 0, 0))


def _finish_reduce(parts, n_l, core, big=BIG):
    out = {}
    for n, shape, _, (rows, cols) in big:
        half = rows // 2
        mine = _sum_leading("chip_sum_" + n, parts[n], F32).reshape(n_l, half, cols)
        theirs = _pair_swap("pair_share_" + n, mine)
        full = jnp.concatenate([jnp.where(core == 0, mine, theirs), jnp.where(core == 0, theirs, mine)], axis=1)
        out[n] = full.reshape((n_l,) + shape)
    return out


def _reduce_grads(full_grads, chip, core, big=BIG):
    pair_sums = _pair_stage(full_grads, core, big)
    parts = {n: _own_slab(_chip_exchange("chip_exchange_" + n, pair_sums[n]), pair_sums[n], chip) for n, _, _, _ in big}
    return _finish_reduce(parts, full_grads[big[0][0]].shape[0], core, big)


HBM_SPEC = pl.BlockSpec(memory_space=pltpu.HBM)
SEM_SPEC = pl.BlockSpec(memory_space=pltpu.SEMAPHORE)
DATAFLOW = pltpu.SideEffectType.DATAFLOW_SIDE_EFFECTING


def _exchange_copies(p_ref, land_ref, sems, waiting, spread):
    x, y, c = _place()
    me = 2 * x + y
    out = []
    for j, (px, py) in enumerate([(1 - x, y), (x, 1 - y), (1 - x, 1 - y)]):
        out.append(pltpu.make_async_remote_copy(src_ref=p_ref if spread else p_ref.at[2 * px + py],
                                                dst_ref=land_ref.at[2 * px + py if waiting else me],
                                                send_sem=sems[j], recv_sem=sems[3 + j], device_id=(px, py, c),
                                                device_id_type=MESH))
    return out


def _exchange_start(name, parts, after, spread=False):
    land_shape = ((N_CHIP,) + parts.shape) if spread else parts.shape

    def body(p_ref, land_ref, after_ref, s0, s1, s2, r0, r1, r2, p_thru, land_thru, token):
        for cp in _exchange_copies(p_ref, land_ref, (s0, s1, s2, r0, r1, r2), False, spread):
            cp.start()
        token[...] = jnp.zeros_like(token)

    res = _pc(body, name=name,
              out_shape=(pltpu.SemaphoreType.DMA(()),) * 6 + (pltpu.HBM(parts.shape, parts.dtype), pltpu.HBM(land_shape, parts.dtype),
                                                            jax.ShapeDtypeStruct((8, LANES), F32)),
              in_specs=(HBM_SPEC, HBM_SPEC, ANY),
              out_specs=(SEM_SPEC,) * 6 + (HBM_SPEC, HBM_SPEC, pl.BlockSpec(memory_space=pltpu.VMEM)),
              input_output_aliases={0: 6, 1: 7}, compiler_params=pltpu.CompilerParams(has_side_effects=DATAFLOW))(
        pltpu.with_memory_space_constraint(parts, pltpu.HBM),
        pltpu.with_memory_space_constraint(lax.empty(land_shape, parts.dtype), pltpu.HBM), after)
    return res[:6], res[6], res[7], res[8]


def _exchange_wait(name, sems, p_thru, land_thru, after, spread=False):
    def body(p_ref, land_ref, s0, s1, s2, r0, r1, r2, after_ref, p_dead, got_ref):
        for cp in _exchange_copies(p_ref, land_ref, (s0, s1, s2, r0, r1, r2), True, spread):
            cp.wait_send()
            cp.wait_recv()

    return _pc(body, name=name, out_shape=(pltpu.HBM(p_thru.shape, p_thru.dtype), pltpu.HBM(land_thru.shape, land_thru.dtype)),
               in_specs=(HBM_SPEC, HBM_SPEC) + (SEM_SPEC,) * 6 + (ANY,), out_specs=(HBM_SPEC, HBM_SPEC),
               input_output_aliases={0: 0, 1: 1}, compiler_params=pltpu.CompilerParams(has_side_effects=DATAFLOW))(
        p_thru, land_thru, *sems, after)[1]


def _regroup_w_in(w):
    cat = lambda spans: jnp.concatenate([w[:, a:b] for a, b in spans], axis=1)
    small = jnp.concatenate([cat(GSMALL), jnp.zeros((w.shape[0], SMALL_W - 24), w.dtype)], axis=1)
    return cat(G768), cat((GXBC,)), cat((GQKV,)), cat((GGATE,)), small


def _ungroup_w_in(d):
    o_xbc, o_qkv, o_gate, o_small = W768, W768 + WXBC, W768 + WXBC + QKV_W, W768 + WXBC + QKV_W + WGATE
    spans = ((0, 3072), (3072, 3840), (o_xbc, o_xbc + WXBC), (o_small, o_small + 12), (o_qkv, o_qkv + QKV_W),
             (3840, 4608), (o_small + 12, o_small + 18), (o_small + 18, o_small + 24), (o_gate, o_gate + WGATE))
    return jnp.concatenate([d[:, a:b] for a, b in spans], axis=1)


def _lane_pad(v, off):
    return jnp.pad(v, (off, LANES - off - v.shape[0]))[None, :]


STATE6 = (N_HEAD6, HEAD, HEAD)


def _mixer_inputs(sv, lp):
    p768, pxbc, pqkv, psmall = sv["p768"], sv["pxbc"], sv["pqkv"], sv["psmall"]
    hgrn = ([(p768, MIX_W, j) for j in range(4)], [], [lp["lb"], lp["hgrn_norm"]])
    ssd = ([(p768, MIX_W, 4), (pxbc, CONV_CH, 0), (psmall, LANES, 0)], [1],
           [lp["ssm_conv_w"], lp["ssm_conv_b"], lp["ssm_dt_bias"], lp["ssm_a_log"], lp["ssm_d"], lp["ssm_norm"]])
    gdn = ([(pqkv, QKV_W, 0), (p768, MIX_W, 5), (psmall, LANES, 0)], [0],
           [lp["gdn_conv_w"], lp["gdn_dt_bias"], lp["gdn_a_log"], lp["gdn_norm"]])
    return hgrn, ssd, gdn


def _layer_fwd(x, md, lw, lp):
    sv = {"x": x}
    (sv["h1"],) = _tile_fwd("lnmod1", _lnmod_fn, [(x, D_MODEL, 0)], [lp["norm_mix"], md["sc1"], md["sh1"]], [(D_MODEL, BF)])
    for nm in ("768", "xbc", "qkv", "gate", "small"):
        sv["p" + nm] = _mm(sv["h1"], lw["win_" + nm], "nn", F32, "proj_" + nm)
    hgrn, ssd, gdn = _mixer_inputs(sv, lp)
    sv["y_h"], sv["st_h"] = _scan_fwd("hgrn_fwd", _hgrn_chunk, *hgrn, MIX_W, STATE6)
    sv["y_s"], sv["st_s"] = _scan_fwd("ssd_fwd", _ssd_chunk, *ssd, MIX_W, STATE6)
    sv["y_g"], sv["st_g"] = _scan_fwd("gdn_fwd", _gdn_chunk, *gdn, MIX_W, STATE6)
    (sv["merged"],) = _tile_fwd("merge", _merge_fn, _merge_tiles(sv), [lw["w_branch"], lp["b_merge"]], [(D_MODEL, BF)])
    sv["out"] = _mm(sv["merged"], lw["w_out"], "nn", F32, "out_proj")
    (sv["x_mid"],) = _tile_fwd("resid1", _resid_fn, [(x, D_MODEL, 0), (sv["out"], D_MODEL, 0)], [md["g1"]], [(D_MODEL, F32)])
    (sv["h2"],) = _tile_fwd("lnmod2", _lnmod_fn, [(sv["x_mid"], D_MODEL, 0)], [lp["norm_ffn"], md["sc2"], md["sh2"]],
                            [(D_MODEL, BF)])
    sv["gu"] = _mm(sv["h2"], lw["w_ffn_in"], "nn", F32, "ffn_in")
    (sv["act"],) = _tile_fwd("swiglu", _swiglu_fn, [(sv["gu"], 2 * FFN_H, 0)], [], [(FFN_H, BF)])
    sv["o2"] = _mm(sv["act"], lw["w_ffn_out"], "nn", F32, "ffn_out")
    (x_out,) = _tile_fwd("resid2", _resid_fn, [(sv["x_mid"], D_MODEL, 0), (sv["o2"], D_MODEL, 0)], [md["g2"]], [(D_MODEL, F32)])
    return x_out, sv


def _merge_tiles(sv):
    return [(sv["y_h"], MIX_W, 0), (sv["y_s"], MIX_W, 0), (sv["y_g"], MIX_W, 0), (sv["pgate"], WGATE, 0)]


def _layer_bwd(dx_out, sv, md, lw, lp):
    g = {}
    x, x_mid = sv["x"], sv["x_mid"]
    d_xmid, d_o2, g["g2"] = _tile_bwd("resid2_b", _resid_fn, [(x_mid, D_MODEL, 0), (sv["o2"], D_MODEL, 0)], [md["g2"]], [dx_out],
                                      [F32, BF])
    d_act = _mm(d_o2, lw["w_ffn_out"], "nt", F32, "ffn_out_dx")
    g["w_ffn_out"] = _mm(sv["act"], d_o2, "tn", BF, "ffn_out_dw")
    (d_gu,) = _tile_bwd("swiglu_b", _swiglu_fn, [(sv["gu"], 2 * FFN_H, 0)], [], [d_act], [BF])
    d_h2 = _mm(d_gu, lw["w_ffn_in"], "nt", F32, "ffn_in_dx")
    g["w_ffn_in"] = _mm(sv["h2"], d_gu, "tn", BF, "ffn_in_dw")
    d_xmid, g["norm_ffn"], g["sc2"], g["sh2"] = _tile_bwd(
        "lnmod2_b", _lnmod_fn, [(x_mid, D_MODEL, 0)], [lp["norm_ffn"], md["sc2"], md["sh2"]], [d_h2], [F32], add_to=(0, d_xmid))
    d_x, d_out, g["g1"] = _tile_bwd("resid1_b", _resid_fn, [(x, D_MODEL, 0), (sv["out"], D_MODEL, 0)], [md["g1"]], [d_xmid],
                                    [F32, BF])
    d_merged = _mm(d_out, lw["w_out"], "nt", F32, "out_proj_dx")
    g["w_out"] = _mm(sv["merged"], d_out, "tn", BF, "out_proj_dw")
    d_yh, d_ys, d_yg, d_gate, g["w_branch"], g["b_merge"] = _tile_bwd(
        "merge_b", _merge_fn, _merge_tiles(sv), [lw["w_branch"], lp["b_merge"]], [d_merged], [F32, F32, F32, BF])
    hgrn, ssd, gdn = _mixer_inputs(sv, lp)
    d_q, d_f, d_v, d_g, g["lb"], g["hgrn_norm"] = _scan_bwd("hgrn_bwd", _hgrn_chunk, *hgrn, sv["st_h"], d_yh, [BF] * 4)
    (d_sz, d_xbc, d_small, g["ssm_conv_w"], g["ssm_conv_b"], g["ssm_dt_bias"], g["ssm_a_log"], g["ssm_d"],
     g["ssm_norm"]) = _scan_bwd("ssd_bwd", _ssd_chunk, *ssd, sv["st_s"], d_ys, [BF, BF, F32])
    d_qkv, d_gz, d_small, g["gdn_conv_w"], g["gdn_dt_bias"], g["gdn_a_log"], g["gdn_norm"] = _scan_bwd(
        "gdn_bwd", _gdn_chunk, *gdn, sv["st_g"], d_yg, [BF, BF, BF], extra=(2, d_small))
    d_proj = jnp.concatenate([d_q, d_f, d_v, d_g, d_sz, d_gz, d_xbc, d_qkv, d_gate, d_small,
                              jnp.zeros((x.shape[0], SMALL_W - LANES), BF)], axis=1)
    d_h1 = _mm(d_proj, lw["win_all"], "nt", F32, "proj_dx")
    g["w_in"] = _ungroup_w_in(_mm(sv["h1"], d_proj, "tn", BF, "proj_dw"))
    d_x, g["norm_mix"], g["sc1"], g["sh1"] = _tile_bwd(
        "lnmod1_b", _lnmod_fn, [(x, D_MODEL, 0)], [lp["norm_mix"], md["sc1"], md["sh1"]], [d_h1], [F32], add_to=(0, d_x))
    return d_x, g


SMALL_REPL = ("norm_mix", "norm_ffn", "b_merge", "hgrn_lb_logits", "hgrn_norm", "ssm_conv_w", "ssm_conv_b", "ssm_dt_bias",
              "ssm_a_log", "ssm_d", "ssm_norm", "gdn_conv_w", "gdn_dt_bias", "gdn_a_log", "gdn_norm", "norm_final")
WEIGHTS = ("w_ada", "b_ada", "norm_mix", "norm_ffn", "w_in", "b_merge", "hgrn_lb_logits", "hgrn_norm", "ssm_conv_w",
           "ssm_conv_b", "ssm_dt_bias", "ssm_a_log", "ssm_d", "ssm_norm", "gdn_conv_w", "gdn_dt_bias", "gdn_a_log",
           "gdn_norm", "w_branch", "w_out", "w_ffn_in", "w_ffn_out", "norm_final")
SMALL_ROWS = 120


def _pad_rows(flat, n_rows, n_cols):
    return jnp.concatenate([flat, jnp.zeros((n_rows * n_cols - flat.shape[0],), flat.dtype)]).reshape(n_rows, n_cols)


def _device_step(x, tgt, mod, lb, wfull, sp, chip=None, core=None, order_after=None):
    mds, lps, svs = [], [], []
    h = x
    weights_of = wfull if callable(wfull) else (lambda layer, after: wfull[layer])
    wfull = []
    for l in range(DEPTH):
        wfull.append(weights_of(l, h))
        md = {n: mod[l, i * D_MODEL:(i + 1) * D_MODEL][None, :] for i, n in enumerate(("sh1", "sc1", "g1", "sh2", "sc2", "g2"))}
        if l == 0 and order_after is not None:
            md["sc1"] = md["sc1"] + order_after
        lp = {n: sp[n][l][None, :] for n in ("norm_mix", "norm_ffn", "b_merge", "hgrn_norm", "ssm_conv_b", "ssm_norm", "gdn_norm")}
        lp["lb"] = lb[l][None, :]
        lp["ssm_conv_w"], lp["gdn_conv_w"] = sp["ssm_conv_w"][l], sp["gdn_conv_w"][l]
        for n in ("ssm_dt_bias", "ssm_a_log", "ssm_d"):
            lp[n] = _lane_pad(sp[n][l], DT_OFF)
        for n in ("gdn_dt_bias", "gdn_a_log"):
            lp[n] = _lane_pad(sp[n][l], GA_OFF)
        h, sv = _layer_fwd(h, md, wfull[l], lp)
        mds.append(md), lps.append(lp), svs.append(sv)
    loss, dh, d_nf = _final_loss(h, tgt, sp["norm_final"][None, :])
    grads = [None] * DEPTH
    if core is None:
        for l in reversed(range(DEPTH)):
            dh, grads[l] = _layer_bwd(dh, svs[l], mds[l], wfull[l], lps[l])
        return loss, dh, d_nf, grads
    names = [n for n, _, _, _ in BIG]
    landed, flying = [None] * DEPTH, None
    for l in reversed(range(DEPTH)):
        md = mds[l]
        if flying is not None:
            md = dict(md, g2=md["g2"] + sum(tok[0, 0] for _, _, _, tok in flying.values()))
        dh, grads[l] = _layer_bwd(dh, svs[l], md, wfull[l], lps[l])
        if flying is not None:
            landed[l + 1] = {n: _own_slab(_exchange_wait(f"exchange_wait_{n}_{l + 1}", *flying[n][:3], dh), sums[n], chip)
                             for n in names}
        sums = _pair_stage({n: grads[l][n].astype(BF)[None] for n in names}, core)
        if l > 0:
            flying = {n: _exchange_start(f"exchange_start_{n}_{l}", sums[n], dh) for n in names}
        else:
            landed[0] = {n: _own_slab(_chip_exchange("chip_exchange_" + n, sums[n]), sums[n], chip) for n in names}
    parts = {n: jnp.concatenate([landed[l][n] for l in range(DEPTH)], axis=1) for n in names}
    return loss, dh, d_nf, grads, _finish_reduce(parts, DEPTH, core)


def kernel(x, c, w_ada, b_ada, norm_mix, norm_ffn, w_in, b_merge, hgrn_lb_logits, hgrn_norm, ssm_conv_w, ssm_conv_b, ssm_dt_bias, ssm_a_log, ssm_d, ssm_norm, gdn_conv_w, gdn_dt_bias, gdn_a_log, gdn_norm, w_branch, w_out, w_ffn_in, w_ffn_out, norm_final, loss_target, m_w_ada, m_b_ada, m_norm_mix, m_norm_ffn, m_w_in, m_b_merge, m_hgrn_lb_logits, m_hgrn_norm, m_ssm_conv_w, m_ssm_conv_b, m_ssm_dt_bias, m_ssm_a_log, m_ssm_d, m_ssm_norm, m_gdn_conv_w, m_gdn_dt_bias, m_gdn_a_log, m_gdn_norm, m_w_branch, m_w_out, m_w_ffn_in, m_w_ffn_out, m_norm_final, v_w_ada, v_b_ada, v_norm_mix, v_norm_ffn, v_w_in, v_b_merge, v_hgrn_lb_logits, v_hgrn_norm, v_ssm_conv_w, v_ssm_conv_b, v_ssm_dt_bias, v_ssm_a_log, v_ssm_d, v_ssm_norm, v_gdn_conv_w, v_gdn_dt_bias, v_gdn_a_log, v_gdn_norm, v_w_branch, v_w_out, v_w_ffn_in, v_w_ffn_out, v_norm_final):
    w = dict(w_ada=w_ada, b_ada=b_ada, norm_mix=norm_mix, norm_ffn=norm_ffn, w_in=w_in, b_merge=b_merge,
             hgrn_lb_logits=hgrn_lb_logits, hgrn_norm=hgrn_norm, ssm_conv_w=ssm_conv_w, ssm_conv_b=ssm_conv_b,
             ssm_dt_bias=ssm_dt_bias, ssm_a_log=ssm_a_log, ssm_d=ssm_d, ssm_norm=ssm_norm, gdn_conv_w=gdn_conv_w,
             gdn_dt_bias=gdn_dt_bias, gdn_a_log=gdn_a_log, gdn_norm=gdn_norm, w_branch=w_branch, w_out=w_out,
             w_ffn_in=w_ffn_in, w_ffn_out=w_ffn_out, norm_final=norm_final)
    m = dict(w_ada=m_w_ada, b_ada=m_b_ada, norm_mix=m_norm_mix, norm_ffn=m_norm_ffn, w_in=m_w_in, b_merge=m_b_merge,
             hgrn_lb_logits=m_hgrn_lb_logits, hgrn_norm=m_hgrn_norm, ssm_conv_w=m_ssm_conv_w, ssm_conv_b=m_ssm_conv_b,
             ssm_dt_bias=m_ssm_dt_bias, ssm_a_log=m_ssm_a_log, ssm_d=m_ssm_d, ssm_norm=m_ssm_norm, gdn_conv_w=m_gdn_conv_w,
             gdn_dt_bias=m_gdn_dt_bias, gdn_a_log=m_gdn_a_log, gdn_norm=m_gdn_norm, w_branch=m_w_branch, w_out=m_w_out,
             w_ffn_in=m_w_ffn_in, w_ffn_out=m_w_ffn_out, norm_final=m_norm_final)
    v = dict(w_ada=v_w_ada, b_ada=v_b_ada, norm_mix=v_norm_mix, norm_ffn=v_norm_ffn, w_in=v_w_in, b_merge=v_b_merge,
             hgrn_lb_logits=v_hgrn_lb_logits, hgrn_norm=v_hgrn_norm, ssm_conv_w=v_ssm_conv_w, ssm_conv_b=v_ssm_conv_b,
             ssm_dt_bias=v_ssm_dt_bias, ssm_a_log=v_ssm_a_log, ssm_d=v_ssm_d, ssm_norm=v_ssm_norm, gdn_conv_w=v_gdn_conv_w,
             gdn_dt_bias=v_gdn_dt_bias, gdn_a_log=v_gdn_a_log, gdn_norm=v_gdn_norm, w_branch=v_w_branch, w_out=v_w_out,
             w_ffn_in=v_w_ffn_in, w_ffn_out=v_w_ffn_out, norm_final=v_norm_final)
    xi, yi, ci = _place()
    chip, me = 2 * xi + yi, 4 * xi + 2 * yi + ci
    seq = x.shape[1]

    conv_flat = jnp.concatenate([ssm_conv_w.reshape(-1), gdn_conv_w.reshape(-1)])
    n_conv = conv_flat.shape[0]
    first = _all_gather_small("gather_c_conv", _pad_rows(jnp.concatenate([c[0], conv_flat]), 16, D_MODEL))
    c_all = first[:, 0, :]
    conv_all = first[0::2].reshape(N_CHIP, -1)[:, D_MODEL:D_MODEL + n_conv]
    n_ssm = ssm_conv_w.size
    sp = dict(w)
    sp["ssm_conv_w"] = jnp.concatenate([conv_all[j, :n_ssm].reshape(ssm_conv_w.shape) for j in range(N_CHIP)], axis=2)
    sp["gdn_conv_w"] = jnp.concatenate([conv_all[j, n_ssm:].reshape(gdn_conv_w.shape) for j in range(N_CHIP)], axis=2)

    ada_cols = w_ada.shape[2]
    mod_part = _ada_fwd(c_all, w_ada, lax.dynamic_slice_in_dim(b_ada, chip * ada_cols, ada_cols, axis=1))
    mod_all = _all_gather_small("gather_mod", mod_part.reshape(DEPTH * 8, ada_cols))[0::2].reshape(N_CHIP, DEPTH, 8, ada_cols)
    mod = lax.dynamic_index_in_dim(mod_all, me, axis=2, keepdims=False).transpose(1, 0, 2).reshape(DEPTH, N_CHIP * ada_cols)
    lb = _lb_fwd(hgrn_lb_logits)

    names = [n for n, _, _, _ in BIG]

    def layer_weights(full):
        full = dict(full)
        for nm, part in zip(("768", "xbc", "qkv", "gate", "small"), _regroup_w_in(full["w_in"])):
            full["win_" + nm] = part
        full["win_all"] = jnp.concatenate([full["win_" + nm] for nm in ("768", "xbc", "qkv", "gate", "small")], axis=1)
        return full

    first = _gather_weights({n: w[n][0:1] for n in names}, chip)
    first = layer_weights({n: first[n][0] for n in names})
    started = {}
    settled = mod[0:1, 0:LANES] + first["w_out"][0:1, 0:LANES].astype(F32)
    for n, _, _, (rows, cols) in BIG:
        mine = w[n][1:].astype(BF).reshape((DEPTH - 1, rows, cols))
        started[n] = _exchange_start("gather_start_" + n, lax.dynamic_slice_in_dim(mine, ci * (rows // 2), rows // 2, axis=1),
                                     settled, spread=True)
    rest = []

    def weights_of(layer, after):
        if layer == 0:
            return first
        if not rest:
            full = {}
            for n, shape, ax, _ in BIG:
                sems, my_half, landing, _ = started[n]
                landed = _exchange_wait("gather_wait_" + n, sems, my_half, landing, after, spread=True)
                landed = lax.dynamic_update_slice(landed, my_half[None], (chip, 0, 0, 0))
                theirs = _pair_swap("gather_share_" + n, landed)
                slabs = jnp.concatenate([jnp.where(ci == 0, landed, theirs), jnp.where(ci == 0, theirs, landed)], axis=2)
                full[n] = _join_shards(slabs, ax, shape)
            rest.extend(layer_weights({n: full[n][i] for n in names}) for i in range(DEPTH - 1))
        return rest[layer - 1]

    order = sum(tok[0, 0] for _, _, _, tok in started.values())
    loss8, d_x, d_nf, lg, grad = _device_step(x[0], loss_target[0], mod, lb, weights_of, sp, chip, ci, order)

    dmod = jnp.stack([jnp.concatenate([lg[l][n] for n in ("sh1", "sc1", "g1", "sh2", "sc2", "g2")], axis=1)[0] for l in range(DEPTH)])
    d_lb = jnp.stack([lg[l]["lb"][0] for l in range(DEPTH)])
    contrib = {
        "norm_mix": jnp.stack([lg[l]["norm_mix"][0] for l in range(DEPTH)]),
        "norm_ffn": jnp.stack([lg[l]["norm_ffn"][0] for l in range(DEPTH)]),
        "b_merge": jnp.stack([lg[l]["b_merge"][0] for l in range(DEPTH)]),
        "hgrn_lb_logits": _lb_bwd(hgrn_lb_logits, d_lb),
        "hgrn_norm": jnp.stack([lg[l]["hgrn_norm"][0] for l in range(DEPTH)]),
        "ssm_conv_w": jnp.stack([lg[l]["ssm_conv_w"] for l in range(DEPTH)]),
        "ssm_conv_b": jnp.stack([lg[l]["ssm_conv_b"][0] for l in range(DEPTH)]),
        "ssm_dt_bias": jnp.stack([lg[l]["ssm_dt_bias"][0, DT_OFF:DT_OFF + 12] for l in range(DEPTH)]),
        "ssm_a_log": jnp.stack([lg[l]["ssm_a_log"][0, DT_OFF:DT_OFF + 12] for l in range(DEPTH)]),
        "ssm_d": jnp.stack([lg[l]["ssm_d"][0, DT_OFF:DT_OFF + 12] for l in range(DEPTH)]),
        "ssm_norm": jnp.stack([lg[l]["ssm_norm"][0] for l in range(DEPTH)]),
        "gdn_conv_w": jnp.stack([lg[l]["gdn_conv_w"] for l in range(DEPTH)]),
        "gdn_dt_bias": jnp.stack([lg[l]["gdn_dt_bias"][0, GA_OFF:GA_OFF + 6] for l in range(DEPTH)]),
        "gdn_a_log": jnp.stack([lg[l]["gdn_a_log"][0, GA_OFF:GA_OFF + 6] for l in range(DEPTH)]),
        "gdn_norm": jnp.stack([lg[l]["gdn_norm"][0] for l in range(DEPTH)]),
        "norm_final": d_nf[0],
    }
    flat = jnp.concatenate([dmod.reshape(-1)] + [contrib[n].reshape(-1) for n in SMALL_REPL] + [loss8[0, 0:1]])
    small_all = _all_gather_small("gather_small_grads", _pad_rows(flat, SMALL_ROWS, D_MODEL))
    total = _sum_leading("small_grad_sum", small_all, F32).reshape(-1)
    n_mod = dmod.size
    grad["b_ada"] = total[:n_mod].reshape(b_ada.shape)
    off = n_mod
    full_small = {}
    for n in SMALL_REPL:
        full_small[n] = total[off:off + contrib[n].size].reshape(contrib[n].shape)
        off += contrib[n].size
    loss = total[off]
    for n in SMALL_REPL:
        if n in ("ssm_conv_w", "gdn_conv_w"):
            cols = w[n].shape[2]
            grad[n] = lax.dynamic_slice_in_dim(full_small[n], chip * cols, cols, axis=2)
        else:
            grad[n] = full_small[n]
    dmod_cols = lax.dynamic_slice_in_dim(small_all[:, :n_mod // D_MODEL, :].reshape(8, DEPTH, -1), chip * ada_cols, ada_cols, axis=2)
    grad["w_ada"] = _ada_bwd(c_all, dmod_cols.transpose(1, 0, 2))

    delta, new_m, new_v = {}, {}, {}
    big_names = ("w_ada",) + tuple(n for n, _, _, _ in BIG)
    for n in big_names:
        delta[n], new_m[n], new_v[n] = _adamw("adamw_" + n, w[n], grad[n], m[n], v[n])
    small_names = [n for n in WEIGHTS if n not in big_names]
    packs = [_pad_rows(jnp.concatenate([d[n].reshape(-1) for n in small_names]), 584, LANES) for d in (w, grad, m, v)]
    outs = _ew("adamw_small", _adamw_fn, packs, [F32, F32, F32])
    off = 0
    for n in small_names:
        for dst, o in zip((delta, new_m, new_v), outs):
            dst[n] = o.reshape(-1)[off:off + w[n].size].reshape(w[n].shape)
        off += w[n].size
    return (loss, d_x[None], *[grad[n] for n in WEIGHTS], *[delta[n] for n in WEIGHTS], *[new_m[n] for n in WEIGHTS],
            *[new_v[n] for n in WEIGHTS])
```

```python
import functools

import jax
import jax.numpy as jnp
from jax import lax
from jax.experimental import pallas as pl
from jax.experimental.pallas import tpu as pltpu

F32 = jnp.float32
BF = jnp.bfloat16
HI = lax.Precision.HIGHEST

D_MODEL = 1024
DEPTH = 4
CHUNK = 64
MIX_W = 768
HEAD = 128
N_HEAD6 = 6
SSM_P = 64
SSM_N = 128
CONV_CH = 1280
QKV_W = 2304
FFN_H = 2816
IN_WIDTH = 11288
NORM_EPS = 1e-6
F_MIN = 1e-30
HALO = 8
HEAD_GROUP = 6
HGRN_SUB = 8
SMALL_W = 512
LANES = 128
DT_OFF, GB_OFF, GA_OFF = 0, 12, 18

ADAM_LR, ADAM_B1, ADAM_B2, ADAM_EPS, ADAM_WD, ADAM_STEP = 0.001, 0.9, 0.999, 1e-08, 0.01, 10

VMEM_LIMIT = 56 * 1024 * 1024
TOKEN_TILE = 256


def _pc(body, **kw):
    return pl.pallas_call(body, **kw)


def _cparams(sem):
    return pltpu.CompilerParams(dimension_semantics=sem, vmem_limit_bytes=VMEM_LIMIT)


def _bdot(a, b):
    return jnp.dot(a.astype(BF), b.astype(BF), preferred_element_type=F32)


def _bdot_nt(a, b):
    return lax.dot_general(a.astype(BF), b.astype(BF), (((1,), (1,)), ((), ())), preferred_element_type=F32)


def _bdot_tn(a, b):
    return lax.dot_general(a.astype(BF), b.astype(BF), (((0,), (0,)), ((), ())), preferred_element_type=F32)


def _silu(x):
    return x * jax.nn.sigmoid(x)


def _tri_mask(n, strict=False):
    t = lax.broadcasted_iota(jnp.int32, (n, n), 0)
    s = lax.broadcasted_iota(jnp.int32, (n, n), 1)
    return (s < t) if strict else (s <= t)


def _masked_exp(diff, mask):
    return jnp.where(mask, jnp.exp(jnp.where(mask, diff, 0.0)), 0.0)


def _split_bf16(x, n):
    parts, rest = [], x
    for _ in range(n):
        p = rest.astype(BF)
        parts.append(p)
        rest = rest - p.astype(F32)
    return parts


def _tri_sum(x, reverse):
    n, w = x.shape
    t = lax.broadcasted_iota(jnp.int32, (n, n), 0)
    s = lax.broadcasted_iota(jnp.int32, (n, n), 1)
    tri = jnp.where((s >= t) if reverse else (s <= t), 1.0, 0.0).astype(BF)
    y = jnp.dot(tri, jnp.concatenate(_split_bf16(x, 3), axis=1), preferred_element_type=F32)
    return y[:, :w] + y[:, w:2 * w] + y[:, 2 * w:]


@jax.custom_vjp
def _cumsum_rows(x):
    return _tri_sum(x, False)


_cumsum_rows.defvjp(lambda x: (_tri_sum(x, False), None), lambda _, g: (_tri_sum(g, True),))


def _dot_split(a, b, transpose_a=False):
    dims = (((0,), (0,)) if transpose_a else ((1,), (0,)), ((), ()))
    a_hi, a_lo = _split_bf16(a, 2)
    b_hi, b_lo = _split_bf16(b, 2)
    w = b.shape[1]
    y = lax.dot_general(a_hi, jnp.concatenate([b_hi, b_lo], axis=1), dims, preferred_element_type=F32)
    return y[:, :w] + y[:, w:] + lax.dot_general(a_lo, b_hi, dims, preferred_element_type=F32)


def _rms(x, w):
    return x * lax.rsqrt(jnp.mean(x * x, axis=-1, keepdims=True) + NORM_EPS) * w


def _causal_conv(halo, x, w):
    ext = jnp.concatenate([halo, x], axis=0)
    n = x.shape[0]
    acc = w[0:1, :] * ext[HALO - 3:HALO - 3 + n, :]
    for i in range(1, 4):
        acc = acc + w[i:i + 1, :] * ext[HALO - 3 + i:HALO - 3 + i + n, :]
    return acc


def _unit_lower_inverses(mats):
    n = mats[0].shape[0]
    t = lax.broadcasted_iota(jnp.int32, (n, n), 0)
    s_ = lax.broadcasted_iota(jnp.int32, (n, n), 1)
    xs = [jnp.where(t == s_, 1.0, 0.0).astype(F32) for _ in mats]
    for s in range(n - 1):
        r0 = 8 * ((s + 1) // 8)
        for i, a in enumerate(mats):
            x = xs[i]
            low = x[r0:] - a[r0:, s:s + 1] * x[s:s + 1, :]
            xs[i] = low if r0 == 0 else jnp.concatenate([x[:r0], low], axis=0)
    return xs


@jax.custom_vjp
def _unit_lower_solves(mats, rhss):
    return [_dot_split(inv, r) for inv, r in zip(_unit_lower_inverses(mats), rhss)]


def _uls_fwd(mats, rhss):
    invs = _unit_lower_inverses(mats)
    xs = [_dot_split(inv, r) for inv, r in zip(invs, rhss)]
    return xs, (invs, xs)


def _uls_bwd(res, gs):
    invs, xs = res
    ys = [_dot_split(inv, g, transpose_a=True) for inv, g in zip(invs, gs)]
    das = [jnp.where(_tri_mask(CHUNK, strict=True), -_bdot_nt(y, x), 0.0) for y, x in zip(ys, xs)]
    return das, ys


_unit_lower_solves.defvjp(_uls_fwd, _uls_bwd)


def _hgrn_chunk(tiles, halos, state, consts):
    q_raw, f_raw, v_all, g_raw = tiles
    lb, norm_w = consts
    q_all = _silu(q_raw)
    f = lb + (1.0 - lb) * jax.nn.sigmoid(f_raw)
    logf = jnp.log(jnp.maximum(f, F_MIN))
    k_all = (1.0 - lb) * jax.nn.sigmoid(-f_raw)
    b_all = _cumsum_rows(logf)
    sub = HGRN_SUB
    row = lax.broadcasted_iota(jnp.int32, (sub, 1), 0)
    src_row = lax.broadcasted_iota(jnp.int32, (CHUNK, 1), 0)
    src_lane = lax.broadcasted_iota(jnp.int32, (1, CHUNK), 1)
    heads = range(N_HEAD6)
    n_sub = CHUNK // sub
    cols = [slice(h * HEAD, (h + 1) * HEAD) for h in heads]
    qs, ks, vs, bs = ([a[:, sl] for sl in cols] for a in (q_all, k_all, v_all, b_all))
    o_inter = [_bdot_nt(qs[h] * jnp.exp(bs[h]), state[h]) for h in heads]
    blocks = [[None] * n_sub for _ in heads]
    for i in range(n_sub):
        r0 = i * sub
        for h in heads:
            if i > 0:
                ref = bs[h][r0 - 1:r0, :]
                blocks[h][i] = _bdot_nt(qs[h][r0:r0 + sub] * jnp.exp(bs[h][r0:r0 + sub] - ref),
                                        ks[h] * _masked_exp(ref - bs[h], src_row < r0))
            else:
                blocks[h][i] = jnp.zeros((sub, CHUNK), F32)
    for h in heads:
        for i in range(n_sub):
            r0 = i * sub
            qi, ki, bi = qs[h][r0:r0 + sub], ks[h][r0:r0 + sub], bs[h][r0:r0 + sub]
            for s in range(sub):
                e = _masked_exp(bi - bi[s:s + 1, :], row >= s)
                col = jnp.sum(qi * ki[s:s + 1, :] * e, axis=1, keepdims=True)
                blocks[h][i] = jnp.where(src_lane == r0 + s, col, blocks[h][i])
    os_ = [_bdot(jnp.concatenate(blocks[h], axis=0), vs[h]) + o_inter[h] for h in heads]
    ends = [bs[h][CHUNK - 1:CHUNK, :] for h in heads]
    new_states = [state[h] * jnp.exp(ends[h]) + _bdot_tn(vs[h], ks[h] * jnp.exp(ends[h] - bs[h])) for h in heads]
    outs = [_rms(os_[h], norm_w) * _silu(g_raw[:, cols[h]]) for h in heads]
    return (jnp.concatenate(outs, axis=1),), jnp.stack(new_states)


def _ssd_chunk(tiles, halos, state, consts):
    z, xbc_raw, small = tiles
    (halo,) = halos
    conv_w, conv_b, dt_bias, a_log, d_skip, norm_w = consts
    xbc = _silu(_causal_conv(halo, xbc_raw, conv_w) + conv_b)
    xs, bm, cm = xbc[:, :MIX_W], xbc[:, MIX_W:MIX_W + 2 * SSM_N], xbc[:, MIX_W + 2 * SSM_N:]
    dt = jax.nn.softplus(small + dt_bias)
    cum = _cumsum_rows(-jnp.exp(a_log) * dt)
    cum_t2 = jnp.concatenate([cum, cum], axis=0).T
    lane = lax.broadcasted_iota(jnp.int32, (1, LANES), 1)
    first = lane < SSM_P
    hm0 = jnp.where(first, 1.0, 0.0).astype(F32)
    hm1 = 1.0 - hm0
    src = jnp.where(first, lane, lane - SSM_P)
    tri2 = src <= lax.broadcasted_iota(jnp.int32, (CHUNK, 1), 0)
    pick = lambda a, b: jnp.where(first, a, b)
    bgs = [bm[:, g * SSM_N:(g + 1) * SSM_N] for g in range(2)]
    cgs = [cm[:, g * SSM_N:(g + 1) * SSM_N] for g in range(2)]
    gmats = [_bdot_nt(cgs[g], jnp.concatenate([bgs[g], bgs[g]], axis=0)) for g in range(2)]
    pairs = range(6)
    xps = [xs[:, p * LANES:(p + 1) * LANES] for p in pairs]
    c0s = [cum[:, 2 * p:2 * p + 1] for p in pairs]
    c1s = [cum[:, 2 * p + 1:2 * p + 2] for p in pairs]
    e0s = [cum[CHUNK - 1:CHUNK, 2 * p:2 * p + 1] for p in pairs]
    e1s = [cum[CHUNK - 1:CHUNK, 2 * p + 1:2 * p + 2] for p in pairs]
    segs = [_masked_exp(pick(c0s[p], c1s[p]) - pick(cum_t2[2 * p:2 * p + 1, :], cum_t2[2 * p + 1:2 * p + 2, :]), tri2)
            for p in pairs]
    vms = []
    for p in pairs:
        v = xps[p] * pick(dt[:, 2 * p:2 * p + 1], dt[:, 2 * p + 1:2 * p + 2])
        vms.append(jnp.concatenate([v * hm0, v * hm1], axis=0))
    y_intra = [_bdot(gmats[p // 3] * segs[p], vms[p]) for p in pairs]
    y_inter = [_bdot(jnp.concatenate([cgs[p // 3] * jnp.exp(c0s[p]), cgs[p // 3] * jnp.exp(c1s[p])], axis=1),
                     jnp.concatenate([state[p] * hm0, state[p] * hm1], axis=0)) for p in pairs]
    new_states = [_bdot_tn(jnp.concatenate([bgs[p // 3] * jnp.exp(e0s[p] - c0s[p]),
                                            bgs[p // 3] * jnp.exp(e1s[p] - c1s[p])], axis=0), vms[p])
                  + state[p] * pick(jnp.exp(e0s[p]), jnp.exp(e1s[p])) for p in pairs]
    ys = [y_intra[p] + y_inter[p] + pick(d_skip[:, 2 * p:2 * p + 1], d_skip[:, 2 * p + 1:2 * p + 2]) * xps[p] for p in pairs]
    y = jnp.concatenate(ys, axis=1) * _silu(z)
    gw = MIX_W // 2
    y = jnp.concatenate([_rms(y[:, g * gw:(g + 1) * gw], norm_w[:, g * gw:(g + 1) * gw]) for g in range(2)], axis=1)
    return (y,), jnp.stack(new_states)


def _gdn_chunk(tiles, halos, state, consts):
    qkv_raw, z, small = tiles
    (halo,) = halos
    conv_w, dt_bias, a_log, norm_w = consts
    qkv = _silu(_causal_conv(halo, qkv_raw, conv_w))
    beta_all = jax.nn.sigmoid(small)
    cum = _cumsum_rows(-jnp.exp(a_log) * jax.nn.softplus(small + dt_bias))
    cum_t = cum.T
    tri, tri_strict = _tri_mask(CHUNK), _tri_mask(CHUNK, strict=True)

    def group(hs):
        n = range(len(hs))
        qs, ks, betas, cs, ces, decays, rhss = [], [], [], [], [], [], []
        for h in hs:
            q = qkv[:, h * HEAD:(h + 1) * HEAD]
            k = qkv[:, MIX_W + h * HEAD:MIX_W + (h + 1) * HEAD]
            v = qkv[:, 2 * MIX_W + h * HEAD:2 * MIX_W + (h + 1) * HEAD]
            q = q * lax.rsqrt(jnp.sum(q * q, axis=-1, keepdims=True) + NORM_EPS) * (HEAD ** -0.5)
            k = k * lax.rsqrt(jnp.sum(k * k, axis=-1, keepdims=True) + NORM_EPS)
            beta = beta_all[:, GB_OFF + h:GB_OFF + h + 1]
            c, c_t = cum[:, GA_OFF + h:GA_OFF + h + 1], cum_t[GA_OFF + h:GA_OFF + h + 1, :]
            qs.append(q), ks.append(k), betas.append(beta), cs.append(c)
            ces.append(cum[CHUNK - 1:CHUNK, GA_OFF + h:GA_OFF + h + 1])
            decays.append(_masked_exp(c - c_t, tri))
            rhss.append(jnp.concatenate([v * beta, k * (beta * jnp.exp(c))], axis=1))
        sts = [state[h] for h in hs]
        qk_kks = [_bdot_nt(jnp.concatenate([qs[i], ks[i]], axis=0), ks[i]) for i in n]
        sols = _unit_lower_solves([jnp.where(tri_strict, betas[i] * qk_kks[i][CHUNK:] * decays[i], 0.0) for i in n], rhss)
        on_states = [_bdot(jnp.concatenate([sols[i][:, HEAD:], qs[i] * jnp.exp(cs[i])], axis=0), sts[i]) for i in n]
        us = [sols[i][:, :HEAD] - on_states[i][:CHUNK] for i in n]
        os_ = [on_states[i][CHUNK:] + _bdot(qk_kks[i][:CHUNK] * decays[i], us[i]) for i in n]
        new = [jnp.exp(ces[i]) * sts[i] + _bdot_tn(ks[i] * jnp.exp(ces[i] - cs[i]), us[i]) for i in n]
        outs = [_rms(os_[i], norm_w) * _silu(z[:, h * HEAD:(h + 1) * HEAD]) for i, h in enumerate(hs)]
        return outs, new

    outs, new_states = [], []
    for h0 in range(0, N_HEAD6, HEAD_GROUP):
        o, s = group(list(range(h0, h0 + HEAD_GROUP)))
        outs += o
        new_states += s
    return (jnp.concatenate(outs, axis=1),), jnp.stack(new_states)


def _scan_fwd(name, fn, tiled, halo_idx, consts, out_width, state_shape):
    seq = tiled[0][0].shape[0]
    nc = seq // CHUNK
    n_t, n_h, n_c = len(tiled), len(halo_idx), len(consts)

    def body(*refs):
        t_refs, h_refs, c_refs = refs[:n_t], refs[n_t:n_t + n_h], refs[n_t + n_h:n_t + n_h + n_c]
        y_ref, save_ref, st_ref = refs[n_t + n_h + n_c:]
        i = pl.program_id(0)

        @pl.when(i == 0)
        def _():
            st_ref[...] = jnp.zeros_like(st_ref)

        flag = jnp.where(i > 0, 1.0, 0.0).astype(F32)
        st = st_ref[...]
        (y,), new = fn([r[...] for r in t_refs], [r[...] * flag for r in h_refs], st, [r[...] for r in c_refs])
        save_ref[0] = st
        y_ref[...] = y.astype(y_ref.dtype)
        st_ref[...] = new

    in_specs = [pl.BlockSpec((CHUNK, w), functools.partial(lambda i, cb: (i, cb), cb=cb)) for _, w, cb in tiled]
    in_specs += [pl.BlockSpec((HALO, tiled[j][1]),
                              functools.partial(lambda i, cb: (jnp.maximum(i * (CHUNK // HALO) - 1, 0), cb), cb=tiled[j][2]))
                 for j in halo_idx]
    in_specs += [pl.BlockSpec(c.shape, functools.partial(lambda i, nd: (0,) * nd, nd=c.ndim)) for c in consts]
    zeros = (0,) * len(state_shape)
    return _pc(
        body, name=name, grid=(nc,), in_specs=in_specs,
        out_specs=(pl.BlockSpec((CHUNK, out_width), lambda i: (i, 0)),
                   pl.BlockSpec((1,) + state_shape, lambda i: (i,) + zeros)),
        out_shape=(jax.ShapeDtypeStruct((seq, out_width), BF), jax.ShapeDtypeStruct((nc,) + state_shape, F32)),
        scratch_shapes=[pltpu.VMEM(state_shape, F32)],
        compiler_params=_cparams(("arbitrary",)),
    )(*[t[0] for t in tiled], *[tiled[j][0] for j in halo_idx], *consts)


def _scan_bwd(name, fn, tiled, halo_idx, consts, saved, dy, dtile_dtypes, extra=None):
    seq = tiled[0][0].shape[0]
    nc = seq // CHUNK
    n_t, n_h, n_c = len(tiled), len(halo_idx), len(consts)
    state_shape = saved.shape[1:]
    n_x = 0 if extra is None else 1

    def body(*refs):
        t_refs, h_refs, c_refs = refs[:n_t], refs[n_t:n_t + n_h], refs[n_t + n_h:n_t + n_h + n_c]
        pos = n_t + n_h + n_c
        save_ref, dy_ref = refs[pos], refs[pos + 1]
        x_refs = refs[pos + 2:pos + 2 + n_x]
        pos += 2 + n_x
        dt_refs, dc_refs = refs[pos:pos + n_t], refs[pos + n_t:pos + n_t + n_c]
        dst_ref = refs[pos + n_t + n_c]
        carry_refs = refs[pos + n_t + n_c + 1:]
        i = pl.program_id(0)

        @pl.when(i == 0)
        def _():
            dst_ref[...] = jnp.zeros_like(dst_ref)
            for r in carry_refs:
                r[...] = jnp.zeros_like(r)
            for r in dc_refs:
                r[...] = jnp.zeros_like(r)

        flag = jnp.where(i < nc - 1, 1.0, 0.0).astype(F32)
        tiles = [r[...] for r in t_refs]
        halos = [r[...] * flag for r in h_refs]
        cvals = [r[...] for r in c_refs]
        _, vjp = jax.vjp(fn, tiles, halos, save_ref[0], cvals)
        d_tiles, d_halos, d_state, d_consts = vjp(((dy_ref[...].astype(F32),), dst_ref[...]))
        dst_ref[...] = d_state
        for r, g in zip(dc_refs, d_consts):
            r[...] += g
        for j, (r, g) in enumerate(zip(dt_refs, d_tiles)):
            if extra is not None and extra[0] == j:
                g = g + x_refs[0][...].astype(F32)
            r[...] = g.astype(r.dtype)
            if j in halo_idx:
                cr = carry_refs[halo_idx.index(j)]
                r[CHUNK - HALO:CHUNK, :] = (g[CHUNK - HALO:CHUNK, :] + cr[...]).astype(r.dtype)
                cr[...] = d_halos[halo_idx.index(j)] * flag

    rev = lambda i: nc - 1 - i
    in_specs = [pl.BlockSpec((CHUNK, w), functools.partial(lambda i, cb: (rev(i), cb), cb=cb)) for _, w, cb in tiled]
    in_specs += [pl.BlockSpec((HALO, tiled[j][1]),
                              functools.partial(lambda i, cb: (jnp.maximum(rev(i) * (CHUNK // HALO) - 1, 0), cb), cb=tiled[j][2]))
                 for j in halo_idx]
    in_specs += [pl.BlockSpec(c.shape, functools.partial(lambda i, nd: (0,) * nd, nd=c.ndim)) for c in consts]
    zeros = (0,) * len(state_shape)
    in_specs += [pl.BlockSpec((1,) + state_shape, lambda i: (rev(i),) + zeros),
                 pl.BlockSpec((CHUNK, dy.shape[1]), lambda i: (rev(i), 0))]
    args = [t[0] for t in tiled] + [tiled[j][0] for j in halo_idx] + list(consts) + [saved, dy]
    if extra is not None:
        in_specs.append(pl.BlockSpec((CHUNK, extra[1].shape[1]), lambda i: (rev(i), 0)))
        args.append(extra[1])
    out_specs = [pl.BlockSpec((CHUNK, w), lambda i: (rev(i), 0)) for _, w, _ in tiled]
    out_specs += [pl.BlockSpec(c.shape, functools.partial(lambda i, nd: (0,) * nd, nd=c.ndim)) for c in consts]
    out_shape = [jax.ShapeDtypeStruct((seq, w), dtd) for (_, w, _), dtd in zip(tiled, dtile_dtypes)]
    out_shape += [jax.ShapeDtypeStruct(c.shape, F32) for c in consts]
    scratch = [pltpu.VMEM(state_shape, F32)] + [pltpu.VMEM((HALO, tiled[j][1]), F32) for j in halo_idx]
    return _pc(body, name=name, grid=(nc,), in_specs=in_specs, out_specs=tuple(out_specs), out_shape=tuple(out_shape),
               scratch_shapes=scratch, compiler_params=_cparams(("arbitrary",)))(*args)


def _tile_fwd(name, fn, tiled, consts, outs, tm=TOKEN_TILE):
    seq = tiled[0][0].shape[0]
    n_t, n_c = len(tiled), len(consts)

    def body(*refs):
        res = fn(*[r[...] for r in refs[:n_t + n_c]])
        for r, y in zip(refs[n_t + n_c:], res):
            r[...] = y.astype(r.dtype)

    in_specs = [pl.BlockSpec((tm, w), functools.partial(lambda i, cb: (i, cb), cb=cb)) for _, w, cb in tiled]
    in_specs += [pl.BlockSpec(c.shape, functools.partial(lambda i, nd: (0,) * nd, nd=c.ndim)) for c in consts]
    return _pc(body, name=name, grid=(seq // tm,), in_specs=in_specs,
               out_specs=tuple(pl.BlockSpec((tm, w), lambda i: (i, 0)) for w, _ in outs),
               out_shape=tuple(jax.ShapeDtypeStruct((seq, w), dtp) for w, dtp in outs),
               compiler_params=_cparams(("arbitrary",)))(*[t[0] for t in tiled], *consts)


def _tile_bwd(name, fn, tiled, consts, douts, dtile_dtypes, add_to=None, tm=TOKEN_TILE):
    seq = tiled[0][0].shape[0]
    n_t, n_c, n_o = len(tiled), len(consts), len(douts)
    n_x = 0 if add_to is None else 1
    keep = [j for j, dtp in enumerate(dtile_dtypes) if dtp is not None]

    def body(*refs):
        vals = [r[...].astype(F32) for r in refs[:n_t + n_c]]
        pos = n_t + n_c
        g_refs, x_refs = refs[pos:pos + n_o], refs[pos + n_o:pos + n_o + n_x]
        pos += n_o + n_x
        dt_refs, dc_refs = refs[pos:pos + len(keep)], refs[pos + len(keep):]
        i = pl.program_id(0)

        @pl.when(i == 0)
        def _():
            for r in dc_refs:
                r[...] = jnp.zeros_like(r)

        _, vjp = jax.vjp(fn, *vals)
        cts = vjp(tuple(g[...].astype(F32) for g in g_refs))
        for r, j in zip(dt_refs, keep):
            g = cts[j]
            if add_to is not None and add_to[0] == j:
                g = g + x_refs[0][...].astype(F32)
            r[...] = g.astype(r.dtype)
        for r, g in zip(dc_refs, cts[n_t:]):
            r[...] += g

    in_specs = [pl.BlockSpec((tm, w), functools.partial(lambda i, cb: (i, cb), cb=cb)) for _, w, cb in tiled]
    in_specs += [pl.BlockSpec(c.shape, functools.partial(lambda i, nd: (0,) * nd, nd=c.ndim)) for c in consts]
    in_specs += [pl.BlockSpec((tm, g.shape[1]), lambda i: (i, 0)) for g in douts]
    args = [t[0] for t in tiled] + list(consts) + list(douts)
    if add_to is not None:
        in_specs.append(pl.BlockSpec((tm, add_to[1].shape[1]), lambda i: (i, 0)))
        args.append(add_to[1])
    out_specs = [pl.BlockSpec((tm, tiled[j][1]), lambda i: (i, 0)) for j in keep]
    out_specs += [pl.BlockSpec(c.shape, functools.partial(lambda i, nd: (0,) * nd, nd=c.ndim)) for c in consts]
    out_shape = [jax.ShapeDtypeStruct((seq, tiled[j][1]), dtile_dtypes[j]) for j in keep]
    out_shape += [jax.ShapeDtypeStruct(c.shape, F32) for c in consts]
    return _pc(body, name=name, grid=(seq // tm,), in_specs=in_specs, out_specs=tuple(out_specs),
               out_shape=tuple(out_shape), compiler_params=_cparams(("arbitrary",)))(*args)


def _lnmod_fn(x, nw, sc, sh):
    return (_rms(x, nw) * (1.0 + sc) + sh,)


def _gated_fn(o, g):
    return ((1.0 + g) * o,)


def _resid_fn(x, o, g):
    return (x + (1.0 + g) * o,)


def _swiglu_fn(gu):
    return (_silu(gu[:, :FFN_H]) * gu[:, FFN_H:],)


def _merge_fn(yh, ys, yg, logits, wb, b_merge):
    gates = jax.nn.sigmoid(logits + b_merge)
    acc = None
    for n, y in enumerate((yh, ys, yg)):
        t = gates[:, n * D_MODEL:(n + 1) * D_MODEL] * _bdot(y, wb[n])
        acc = t if acc is None else acc + t
    return (acc,)


MM_VMEM_BUDGET = 40 * 1024 * 1024
MM_TILE_CAP = 1024
MM_K_CAP = 4096


def _divisor(n, cap, unit=LANES):
    best = None
    for d in range(unit, min(n, cap) + 1, unit):
        if n % d == 0:
            best = d
    return n if best is None else best


def _mm_tiles(m, n, k, out_bytes):
    tk = k if k <= MM_K_CAP else _divisor(k, 3072)
    tm, tn = _divisor(m, MM_TILE_CAP), _divisor(n, MM_TILE_CAP + MM_TILE_CAP // 2)

    def need(tm_, tn_):
        acc = tm_ * tn_ * 4 if tk < k else 0
        return 2 * 2 * tk * (tm_ + tn_) + acc + 2 * tm_ * tn_ * out_bytes

    while need(tm, tn) > MM_VMEM_BUDGET:
        if tn >= tm and _divisor(n, tn - LANES) < tn:
            tn = _divisor(n, tn - LANES)
        elif _divisor(m, tm - LANES) < tm:
            tm = _divisor(m, tm - LANES)
        else:
            break
    return tm, tn, tk


def _mm(a, b, mode, out_dtype, name):
    if mode == "nn":
        (m, k), n = a.shape, b.shape[1]
    elif mode == "nt":
        (m, k), n = a.shape, b.shape[0]
    else:
        (k, m), n = a.shape, b.shape[1]
    tm, tn, tk = _mm_tiles(m, n, k, jnp.dtype(out_dtype).itemsize)
    nk = k // tk
    dims = {"nn": ((1,), (0,)), "nt": ((1,), (1,)), "tn": ((0,), (0,))}[mode]

    def body_one(a_ref, b_ref, o_ref):
        o_ref[...] = lax.dot_general(a_ref[...], b_ref[...], (dims, ((), ())), preferred_element_type=F32).astype(o_ref.dtype)

    def body_acc(a_ref, b_ref, o_ref, acc_ref):
        kk = pl.program_id(2)

        @pl.when(kk == 0)
        def _():
            acc_ref[...] = jnp.zeros_like(acc_ref)

        acc_ref[...] += lax.dot_general(a_ref[...], b_ref[...], (dims, ((), ())), preferred_element_type=F32)

        @pl.when(kk == nk - 1)
        def _():
            o_ref[...] = acc_ref[...].astype(o_ref.dtype)

    a_spec = pl.BlockSpec((tk, tm), lambda i, j, kk: (kk, i)) if mode == "tn" else pl.BlockSpec((tm, tk), lambda i, j, kk: (i, kk))
    b_spec = pl.BlockSpec((tn, tk), lambda i, j, kk: (j, kk)) if mode == "nt" else pl.BlockSpec((tk, tn), lambda i, j, kk: (kk, j))
    return _pc(body_one if nk == 1 else body_acc, name=name, grid=(m // tm, n // tn, nk), in_specs=[a_spec, b_spec],
               out_specs=pl.BlockSpec((tm, tn), lambda i, j, kk: (i, j)),
               out_shape=jax.ShapeDtypeStruct((m, n), out_dtype),
               scratch_shapes=[] if nk == 1 else [pltpu.VMEM((tm, tn), F32)],
               compiler_params=_cparams(("parallel", "parallel", "arbitrary")))(a.astype(BF), b.astype(BF))


def _final_loss(x, tgt, norm_final, tm=TOKEN_TILE):
    seq = x.shape[0]

    def fn(xv, nf, tv):
        err = jnp.square(_rms(xv, nf) - tv)
        return 0.5 * jnp.sum(jnp.mean(err, axis=-1))

    def body(x_ref, t_ref, nf_ref, loss_ref, dx_ref, dnf_ref):
        i = pl.program_id(0)

        @pl.when(i == 0)
        def _():
            loss_ref[...] = jnp.zeros_like(loss_ref)
            dnf_ref[...] = jnp.zeros_like(dnf_ref)

        val, vjp = jax.vjp(functools.partial(fn, tv=t_ref[...]), x_ref[...], nf_ref[...])
        dx, dnf = vjp(jnp.ones((), F32))
        dx_ref[...] = dx
        dnf_ref[...] += dnf
        loss_ref[...] += jnp.broadcast_to(val, loss_ref.shape)

    return _pc(body, name="final_loss", grid=(seq // tm,),
               in_specs=[pl.BlockSpec((tm, D_MODEL), lambda i: (i, 0)), pl.BlockSpec((tm, D_MODEL), lambda i: (i, 0)),
                         pl.BlockSpec((1, D_MODEL), lambda i: (0, 0))],
               out_specs=(pl.BlockSpec((8, LANES), lambda i: (0, 0)), pl.BlockSpec((tm, D_MODEL), lambda i: (i, 0)),
                          pl.BlockSpec((1, D_MODEL), lambda i: (0, 0))),
               out_shape=(jax.ShapeDtypeStruct((8, LANES), F32), jax.ShapeDtypeStruct((seq, D_MODEL), F32),
                          jax.ShapeDtypeStruct((1, D_MODEL), F32)),
               compiler_params=_cparams(("arbitrary",)))(x, tgt, norm_final)


def _ada_fwd(c_all, w_ada, b_ada_cols):
    n_l, _, cols = w_ada.shape

    def body(c_ref, w_ref, b_ref, o_ref):
        o_ref[0] = jnp.dot(_silu(c_ref[...]), w_ref[0], preferred_element_type=F32, precision=HI) + b_ref[0]

    return _pc(body, name="ada_fwd", grid=(n_l,),
               in_specs=[pl.BlockSpec((8, D_MODEL), lambda l: (0, 0)), pl.BlockSpec((1, D_MODEL, cols), lambda l: (l, 0, 0)),
                         pl.BlockSpec((1, 1, cols), lambda l: (l, 0, 0))],
               out_specs=pl.BlockSpec((1, 8, cols), lambda l: (l, 0, 0)),
               out_shape=jax.ShapeDtypeStruct((n_l, 8, cols), F32),
               compiler_params=_cparams(("arbitrary",)))(c_all, w_ada, b_ada_cols.reshape(n_l, 1, cols))


def _ada_bwd(c_all, dmod_cols):
    n_l, _, cols = dmod_cols.shape

    def body(c_ref, g_ref, o_ref):
        o_ref[0] = lax.dot_general(_silu(c_ref[...]), g_ref[0], (((0,), (0,)), ((), ())), preferred_element_type=F32,
                                   precision=HI)

    return _pc(body, name="ada_bwd", grid=(n_l,),
               in_specs=[pl.BlockSpec((8, D_MODEL), lambda l: (0, 0)), pl.BlockSpec((1, 8, cols), lambda l: (l, 0, 0))],
               out_specs=pl.BlockSpec((1, D_MODEL, cols), lambda l: (l, 0, 0)),
               out_shape=jax.ShapeDtypeStruct((n_l, D_MODEL, cols), F32),
               compiler_params=_cparams(("arbitrary",)))(c_all, dmod_cols)


def _lb_fn(logits):
    e = jnp.exp(logits - jnp.max(logits, axis=0, keepdims=True))
    p = e / jnp.sum(e, axis=0, keepdims=True)
    r = lax.broadcasted_iota(jnp.int32, (DEPTH, 1), 0)
    lb = jnp.zeros_like(p)
    for j in range(1, DEPTH):
        lb = lb + jnp.where(r >= j, p[j:j + 1, :], 0.0)
    return lb


def _lb_fwd(logits):
    def body(l_ref, o_ref):
        o_ref[...] = _lb_fn(l_ref[...])

    return _pc(body, name="lb_fwd", out_shape=jax.ShapeDtypeStruct(logits.shape, F32))(logits)


def _lb_bwd(logits, dlb):
    def body(l_ref, g_ref, o_ref):
        _, vjp = jax.vjp(_lb_fn, l_ref[...])
        o_ref[...] = vjp(g_ref[...])[0]

    return _pc(body, name="lb_bwd", out_shape=jax.ShapeDtypeStruct(logits.shape, F32))(logits, dlb)


def _rows_for(n_rows, n_cols):
    r = 8
    while r * 2 <= n_rows and n_rows % (r * 2) == 0 and r * 2 * n_cols <= 256 * 1024:
        r *= 2
    return r if n_rows % r == 0 else n_rows


def _ew(name, fn, ins, out_dtypes):
    n_rows, n_cols = ins[0].shape
    tr = _rows_for(n_rows, n_cols)
    n_in = len(ins)

    def body(*refs):
        res = fn(*[r[...] for r in refs[:n_in]])
        for r, y in zip(refs[n_in:], res):
            r[...] = y.astype(r.dtype)

    spec = pl.BlockSpec((tr, n_cols), lambda i: (i, 0))
    return _pc(body, name=name, grid=(n_rows // tr,), in_specs=[spec] * n_in, out_specs=tuple([spec] * len(out_dtypes)),
               out_shape=tuple(jax.ShapeDtypeStruct((n_rows, n_cols), d) for d in out_dtypes),
               compiler_params=_cparams(("arbitrary",)))(*ins)


def _adamw_fn(w, g, m, v):
    m = ADAM_B1 * m + (1.0 - ADAM_B1) * g
    v = ADAM_B2 * v + (1.0 - ADAM_B2) * jnp.square(g)
    m_hat = m / (1.0 - ADAM_B1 ** ADAM_STEP)
    v_hat = v / (1.0 - ADAM_B2 ** ADAM_STEP)
    return -ADAM_LR * (m_hat / (jnp.sqrt(v_hat) + ADAM_EPS) + ADAM_WD * w), m, v


def _adamw(name, w, g, m, v):
    shape = w.shape
    two = (-1, shape[-1])
    d, nm, nv = _ew(name, _adamw_fn, [a.reshape(two) for a in (w, g, m, v)], [F32, F32, F32])
    return d.reshape(shape), nm.reshape(shape), nv.reshape(shape)


def _sum_leading(name, a, out_dtype):
    n, n_rows, n_cols = a.shape
    tr = _rows_for(n_rows, n_cols)

    def body(a_ref, o_ref):
        acc = a_ref[0].astype(F32)
        for j in range(1, n):
            acc = acc + a_ref[j].astype(F32)
        o_ref[...] = acc.astype(o_ref.dtype)

    return _pc(body, name=name, grid=(n_rows // tr,), in_specs=[pl.BlockSpec((n, tr, n_cols), lambda i: (0, i, 0))],
               out_specs=pl.BlockSpec((tr, n_cols), lambda i: (i, 0)),
               out_shape=jax.ShapeDtypeStruct((n_rows, n_cols), out_dtype),
               compiler_params=_cparams(("arbitrary",)))(a)


MESH = pl.DeviceIdType.MESH
ANY = pl.BlockSpec(memory_space=pl.ANY)


def _place():
    return lax.axis_index("x"), lax.axis_index("y"), lax.axis_index("c")


def _all_gather_small(name, a):
    m_per, n = a.shape

    def body(x_ref, out_ref, send_sems, recv_sems, local_sem):
        x, y, c = _place()
        me, sibling = (x, y, c), (x, y, 1 - c)
        chips = [(1 - x, y), (x, 1 - y), (1 - x, 1 - y)]

        def rows(px, py, pc):
            return out_ref.at[pl.ds((4 * px + 2 * py + pc) * m_per, m_per), :]

        def copy(k, block, to, src=None):
            return pltpu.make_async_remote_copy(src_ref=rows(*block) if src is None else src, dst_ref=rows(*block),
                                                send_sem=send_sems.at[k], recv_sem=recv_sems.at[k], device_id=to,
                                                device_id_type=MESH)

        mine = pltpu.make_async_copy(x_ref, rows(*me), local_sem)
        mine.start()
        first = [copy(0, me, sibling, src=x_ref)]
        first += [copy(1 + j, me, (*chip, c), src=x_ref) for j, chip in enumerate(chips)]
        for cp in first:
            cp.start()
        passed = [copy(4 + j, (*chip, c), sibling) for j, chip in enumerate(chips)]
        for j, chip in enumerate(chips):
            copy(1 + j, (*chip, c), me).wait_recv()
            passed[j].start()
        copy(0, sibling, me).wait_recv()
        for j, chip in enumerate(chips):
            copy(4 + j, (*chip, 1 - c), me).wait_recv()
        for cp in first + passed:
            cp.wait_send()
        mine.wait()

    out = _pc(body, name=name, out_shape=jax.ShapeDtypeStruct((8 * m_per, n), a.dtype),
              in_specs=[pl.BlockSpec(memory_space=pltpu.VMEM)], out_specs=pl.BlockSpec(memory_space=pltpu.VMEM),
              scratch_shapes=[pltpu.SemaphoreType.DMA((7,)), pltpu.SemaphoreType.DMA((7,)), pltpu.SemaphoreType.DMA],
              compiler_params=pltpu.CompilerParams(vmem_limit_bytes=VMEM_LIMIT))(a)
    return out.reshape(8, m_per, n)


def _chip_gather(name, pack):
    n_l, n_r, n_c = pack.shape
    half = n_r // 2

    def body(p_ref, o_ref, send_sems, recv_sems):
        x, y, c = _place()
        sibling = (x, y, 1 - c)
        chips = [(1 - x, y), (x, 1 - y), (1 - x, 1 - y)]

        def slab(px, py, pc):
            return o_ref.at[2 * px + py, :, pl.ds(pc * half, half), :]

        def copy(k, src, dst, to):
            return pltpu.make_async_remote_copy(src_ref=src, dst_ref=dst, send_sem=send_sems.at[k], recv_sem=recv_sems.at[k],
                                                device_id=to, device_id_type=MESH)

        first = [copy(j, p_ref.at[:, pl.ds(c * half, half), :], slab(x, y, c), (*chip, c)) for j, chip in enumerate(chips)]
        for cp in first:
            cp.start()
        passed = [copy(3 + j, slab(*chip, c), slab(*chip, c), sibling) for j, chip in enumerate(chips)]
        for j, chip in enumerate(chips):
            copy(j, slab(*chip, c), slab(*chip, c), (*chip, c)).wait_recv()
            passed[j].start()
        for j, chip in enumerate(chips):
            copy(3 + j, slab(*chip, 1 - c), slab(*chip, 1 - c), sibling).wait_recv()
        for cp in first + passed:
            cp.wait_send()

    return _pc(body, name=name, out_shape=jax.ShapeDtypeStruct((4, n_l, n_r, n_c), pack.dtype), in_specs=[ANY], out_specs=ANY,
               scratch_shapes=[pltpu.SemaphoreType.DMA((6,)), pltpu.SemaphoreType.DMA((6,))])(pack)


def _pair_swap(name, give):
    def body(g_ref, o_ref, send_sem, recv_sem):
        x, y, c = _place()
        cp = pltpu.make_async_remote_copy(src_ref=g_ref, dst_ref=o_ref, send_sem=send_sem, recv_sem=recv_sem,
                                          device_id=(x, y, 1 - c), device_id_type=MESH)
        cp.start()
        cp.wait()

    return _pc(body, name=name, out_shape=jax.ShapeDtypeStruct(give.shape, give.dtype), in_specs=[ANY], out_specs=ANY,
               scratch_shapes=[pltpu.SemaphoreType.DMA, pltpu.SemaphoreType.DMA])(give)


def _chip_exchange(name, parts):
    def body(p_ref, o_ref, send_sems, recv_sems):
        x, y, c = _place()
        me = 2 * x + y
        chips = [(1 - x, y), (x, 1 - y), (1 - x, 1 - y)]

        def copy(k, src, dst, to):
            return pltpu.make_async_remote_copy(src_ref=src, dst_ref=dst, send_sem=send_sems.at[k], recv_sem=recv_sems.at[k],
                                                device_id=to, device_id_type=MESH)

        sends = [copy(j, p_ref.at[2 * px + py], o_ref.at[me], (px, py, c)) for j, (px, py) in enumerate(chips)]
        for cp in sends:
            cp.start()
        for j, (px, py) in enumerate(chips):
            copy(j, p_ref.at[2 * px + py], o_ref.at[2 * px + py], (px, py, c)).wait_recv()
        for cp in sends:
            cp.wait_send()

    return _pc(body, name=name, out_shape=jax.ShapeDtypeStruct(parts.shape, parts.dtype), in_specs=[ANY], out_specs=ANY,
               scratch_shapes=[pltpu.SemaphoreType.DMA((3,)), pltpu.SemaphoreType.DMA((3,))])(parts)


N_CHIP = 4
BIG = (("w_in", (1024, 2822), 1, (1024, 2822)), ("w_branch", (3, 768, 256), 2, (2304, 256)),
       ("w_out", (256, 1024), 0, (256, 1024)), ("w_ffn_in", (1024, 1408), 1, (1024, 1408)),
       ("w_ffn_out", (704, 1024), 0, (704, 1024)))
G768 = ((0, 3072), (3072, 3840), (7436, 8204))
GXBC, GQKV, GGATE = (3840, 5120), (5132, 7436), (8216, 11288)
GSMALL = ((5120, 5132), (8204, 8210), (8210, 8216))
W768, WXBC, WGATE = 4608, CONV_CH, 3 * D_MODEL
IN_PAD = W768 + WXBC + QKV_W + WGATE + SMALL_W


def _join_shards(slabs, axis, shard_shape):
    n_l = slabs.shape[1]
    parts = [slabs[j].reshape((n_l,) + shard_shape) for j in range(N_CHIP)]
    return jnp.concatenate(parts, axis=axis + 1)


def _split_shards(full, axis, rows_cols):
    n_l = full.shape[0]
    size = full.shape[axis + 1] // N_CHIP
    return jnp.stack([lax.slice_in_dim(full, j * size, (j + 1) * size, axis=axis + 1).reshape((n_l,) + rows_cols)
                      for j in range(N_CHIP)])


def _gather_weights(w, chip, big=BIG):
    out = {}
    for n, shape, ax, rc in big:
        n_l = w[n].shape[0]
        mine = w[n].astype(BF).reshape((n_l,) + rc)
        slabs = lax.dynamic_update_slice(_chip_gather("gather_" + n, mine), mine[None], (chip, 0, 0, 0))
        out[n] = _join_shards(slabs, ax, shape)
    return out


def _pair_stage(full_grads, core, big=BIG):
    out = {}
    for n, _, ax, (rows, cols) in big:
        n_l = full_grads[n].shape[0]
        slabs = _split_shards(full_grads[n], ax, (rows, cols))
        half = rows // 2
        keep = lax.dynamic_slice_in_dim(slabs, core * half, half, axis=2).reshape(-1, cols)
        give = lax.dynamic_slice_in_dim(slabs, (1 - core) * half, half, axis=2).reshape(-1, cols)
        got = _pair_swap("pair_swap_" + n, give)
        (pair_sum,) = _ew("pair_sum_" + n, lambda a, b: (a.astype(F32) + b.astype(F32),), [keep, got], [BF])
        out[n] = pair_sum.reshape(N_CHIP, n_l * half, cols)
    return out


def _own_slab(landed, pair_sum, chip):
    return lax.dynamic_update_slice(landed, lax.dynamic_slice_in_dim(pair_sum, chip, 1, axis=0), (chip, 0, 0))


def _finish_reduce(parts, n_l, core, big=BIG):
    out = {}
    for n, shape, _, (rows, cols) in big:
        half = rows // 2
        mine = _sum_leading("chip_sum_" + n, parts[n], F32).reshape(n_l, half, cols)
        theirs = _pair_swap("pair_share_" + n, mine)
        full = jnp.concatenate([jnp.where(core == 0, mine, theirs), jnp.where(core == 0, theirs, mine)], axis=1)
        out[n] = full.reshape((n_l,) + shape)
    return out


def _reduce_grads(full_grads, chip, core, big=BIG):
    pair_sums = _pair_stage(full_grads, core, big)
    parts = {n: _own_slab(_chip_exchange("chip_exchange_" + n, pair_sums[n]), pair_sums[n], chip) for n, _, _, _ in big}
    return _finish_reduce(parts, full_grads[big[0][0]].shape[0], core, big)


HBM_SPEC = pl.BlockSpec(memory_space=pltpu.HBM)
SEM_SPEC = pl.BlockSpec(memory_space=pltpu.SEMAPHORE)
DATAFLOW = pltpu.SideEffectType.DATAFLOW_SIDE_EFFECTING


def _exchange_copies(p_ref, land_ref, sems, waiting, spread):
    x, y, c = _place()
    me = 2 * x + y
    out = []
    for j, (px, py) in enumerate([(1 - x, y), (x, 1 - y), (1 - x, 1 - y)]):
        out.append(pltpu.make_async_remote_copy(src_ref=p_ref if spread else p_ref.at[2 * px + py],
                                                dst_ref=land_ref.at[2 * px + py if waiting else me],
                                                send_sem=sems[j], recv_sem=sems[3 + j], device_id=(px, py, c),
                                                device_id_type=MESH))
    return out


def _exchange_start(name, parts, after, spread=False):
    land_shape = ((N_CHIP,) + parts.shape) if spread else parts.shape

    def body(p_ref, land_ref, after_ref, s0, s1, s2, r0, r1, r2, p_thru, land_thru, token):
        for cp in _exchange_copies(p_ref, land_ref, (s0, s1, s2, r0, r1, r2), False, spread):
            cp.start()
        token[...] = jnp.zeros_like(token)

    res = _pc(body, name=name,
              out_shape=(pltpu.SemaphoreType.DMA(()),) * 6 + (pltpu.HBM(parts.shape, parts.dtype), pltpu.HBM(land_shape, parts.dtype),
                                                            jax.ShapeDtypeStruct((8, LANES), F32)),
              in_specs=(HBM_SPEC, HBM_SPEC, ANY),
              out_specs=(SEM_SPEC,) * 6 + (HBM_SPEC, HBM_SPEC, pl.BlockSpec(memory_space=pltpu.VMEM)),
              input_output_aliases={0: 6, 1: 7}, compiler_params=pltpu.CompilerParams(has_side_effects=DATAFLOW))(
        pltpu.with_memory_space_constraint(parts, pltpu.HBM),
        pltpu.with_memory_space_constraint(lax.empty(land_shape, parts.dtype), pltpu.HBM), after)
    return res[:6], res[6], res[7], res[8]


def _exchange_wait(name, sems, p_thru, land_thru, after, spread=False):
    def body(p_ref, land_ref, s0, s1, s2, r0, r1, r2, after_ref, p_dead, got_ref):
        for cp in _exchange_copies(p_ref, land_ref, (s0, s1, s2, r0, r1, r2), True, spread):
            cp.wait_send()
            cp.wait_recv()

    return _pc(body, name=name, out_shape=(pltpu.HBM(p_thru.shape, p_thru.dtype), pltpu.HBM(land_thru.shape, land_thru.dtype)),
               in_specs=(HBM_SPEC, HBM_SPEC) + (SEM_SPEC,) * 6 + (ANY,), out_specs=(HBM_SPEC, HBM_SPEC),
               input_output_aliases={0: 0, 1: 1}, compiler_params=pltpu.CompilerParams(has_side_effects=DATAFLOW))(
        p_thru, land_thru, *sems, after)[1]


def _regroup_w_in(w):
    cat = lambda spans: jnp.concatenate([w[:, a:b] for a, b in spans], axis=1)
    small = jnp.concatenate([cat(GSMALL), jnp.zeros((w.shape[0], SMALL_W - 24), w.dtype)], axis=1)
    return cat(G768), cat((GXBC,)), cat((GQKV,)), cat((GGATE,)), small


def _ungroup_w_in(d):
    o_xbc, o_qkv, o_gate, o_small = W768, W768 + WXBC, W768 + WXBC + QKV_W, W768 + WXBC + QKV_W + WGATE
    spans = ((0, 3072), (3072, 3840), (o_xbc, o_xbc + WXBC), (o_small, o_small + 12), (o_qkv, o_qkv + QKV_W),
             (3840, 4608), (o_small + 12, o_small + 18), (o_small + 18, o_small + 24), (o_gate, o_gate + WGATE))
    return jnp.concatenate([d[:, a:b] for a, b in spans], axis=1)


def _lane_pad(v, off):
    return jnp.pad(v, (off, LANES - off - v.shape[0]))[None, :]


STATE6 = (N_HEAD6, HEAD, HEAD)


def _mixer_inputs(sv, lp):
    p768, pxbc, pqkv, psmall = sv["p768"], sv["pxbc"], sv["pqkv"], sv["psmall"]
    hgrn = ([(p768, MIX_W, j) for j in range(4)], [], [lp["lb"], lp["hgrn_norm"]])
    ssd = ([(p768, MIX_W, 4), (pxbc, CONV_CH, 0), (psmall, LANES, 0)], [1],
           [lp["ssm_conv_w"], lp["ssm_conv_b"], lp["ssm_dt_bias"], lp["ssm_a_log"], lp["ssm_d"], lp["ssm_norm"]])
    gdn = ([(pqkv, QKV_W, 0), (p768, MIX_W, 5), (psmall, LANES, 0)], [0],
           [lp["gdn_conv_w"], lp["gdn_dt_bias"], lp["gdn_a_log"], lp["gdn_norm"]])
    return hgrn, ssd, gdn


def _layer_fwd(x, md, lw, lp):
    sv = {"x": x}
    (sv["h1"],) = _tile_fwd("lnmod1", _lnmod_fn, [(x, D_MODEL, 0)], [lp["norm_mix"], md["sc1"], md["sh1"]], [(D_MODEL, BF)])
    for nm in ("768", "xbc", "qkv", "gate", "small"):
        sv["p" + nm] = _mm(sv["h1"], lw["win_" + nm], "nn", F32, "proj_" + nm)
    hgrn, ssd, gdn = _mixer_inputs(sv, lp)
    sv["y_h"], sv["st_h"] = _scan_fwd("hgrn_fwd", _hgrn_chunk, *hgrn, MIX_W, STATE6)
    sv["y_s"], sv["st_s"] = _scan_fwd("ssd_fwd", _ssd_chunk, *ssd, MIX_W, STATE6)
    sv["y_g"], sv["st_g"] = _scan_fwd("gdn_fwd", _gdn_chunk, *gdn, MIX_W, STATE6)
    (sv["merged"],) = _tile_fwd("merge", _merge_fn, _merge_tiles(sv), [lw["w_branch"], lp["b_merge"]], [(D_MODEL, BF)])
    sv["out"] = _mm(sv["merged"], lw["w_out"], "nn", F32, "out_proj")
    (sv["x_mid"],) = _tile_fwd("resid1", _resid_fn, [(x, D_MODEL, 0), (sv["out"], D_MODEL, 0)], [md["g1"]], [(D_MODEL, F32)])
    (sv["h2"],) = _tile_fwd("lnmod2", _lnmod_fn, [(sv["x_mid"], D_MODEL, 0)], [lp["norm_ffn"], md["sc2"], md["sh2"]],
                            [(D_MODEL, BF)])
    sv["gu"] = _mm(sv["h2"], lw["w_ffn_in"], "nn", F32, "ffn_in")
    (sv["act"],) = _tile_fwd("swiglu", _swiglu_fn, [(sv["gu"], 2 * FFN_H, 0)], [], [(FFN_H, BF)])
    sv["o2"] = _mm(sv["act"], lw["w_ffn_out"], "nn", F32, "ffn_out")
    (x_out,) = _tile_fwd("resid2", _resid_fn, [(sv["x_mid"], D_MODEL, 0), (sv["o2"], D_MODEL, 0)], [md["g2"]], [(D_MODEL, F32)])
    return x_out, sv


def _merge_tiles(sv):
    return [(sv["y_h"], MIX_W, 0), (sv["y_s"], MIX_W, 0), (sv["y_g"], MIX_W, 0), (sv["pgate"], WGATE, 0)]


def _layer_bwd(dx_out, sv, md, lw, lp):
    g = {}
    x, x_mid = sv["x"], sv["x_mid"]
    d_o2, g["g2"] = _tile_bwd("resid2_b", _gated_fn, [(sv["o2"], D_MODEL, 0)], [md["g2"]], [dx_out], [BF])
    d_xmid = dx_out
    d_act =_mm(d_o2, lw["w_ffn_out"], "nt", F32, "ffn_out_dx")
    g["w_ffn_out"] = _mm(sv["act"], d_o2, "tn", BF, "ffn_out_dw")
    (d_gu,) = _tile_bwd("swiglu_b", _swiglu_fn, [(sv["gu"], 2 * FFN_H, 0)], [], [d_act], [BF])
    d_h2 = _mm(d_gu, lw["w_ffn_in"], "nt", F32, "ffn_in_dx")
    g["w_ffn_in"] = _mm(sv["h2"], d_gu, "tn", BF, "ffn_in_dw")
    d_xmid, g["norm_ffn"], g["sc2"], g["sh2"] = _tile_bwd(
        "lnmod2_b", _lnmod_fn, [(x_mid, D_MODEL, 0)], [lp["norm_ffn"], md["sc2"], md["sh2"]], [d_h2], [F32], add_to=(0, d_xmid))
    d_out, g["g1"] = _tile_bwd("resid1_b", _gated_fn, [(sv["out"], D_MODEL, 0)], [md["g1"]], [d_xmid], [BF])
    d_x = d_xmid
    d_merged = _mm(d_out, lw["w_out"], "nt", F32, "out_proj_dx")
    g["w_out"] = _mm(sv["merged"], d_out, "tn", BF, "out_proj_dw")
    d_yh, d_ys, d_yg, d_gate, g["w_branch"], g["b_merge"] = _tile_bwd(
        "merge_b", _merge_fn, _merge_tiles(sv), [lw["w_branch"], lp["b_merge"]], [d_merged], [F32, F32, F32, BF])
    hgrn, ssd, gdn = _mixer_inputs(sv, lp)
    d_q, d_f, d_v, d_g, g["lb"], g["hgrn_norm"] = _scan_bwd("hgrn_bwd", _hgrn_chunk, *hgrn, sv["st_h"], d_yh, [BF] * 4)
    (d_sz, d_xbc, d_small, g["ssm_conv_w"], g["ssm_conv_b"], g["ssm_dt_bias"], g["ssm_a_log"], g["ssm_d"],
     g["ssm_norm"]) = _scan_bwd("ssd_bwd", _ssd_chunk, *ssd, sv["st_s"], d_ys, [BF, BF, F32])
    d_qkv, d_gz, d_small, g["gdn_conv_w"], g["gdn_dt_bias"], g["gdn_a_log"], g["gdn_norm"] = _scan_bwd(
        "gdn_bwd", _gdn_chunk, *gdn, sv["st_g"], d_yg, [BF, BF, BF], extra=(2, d_small))
    d_proj = jnp.concatenate([d_q, d_f, d_v, d_g, d_sz, d_gz, d_xbc, d_qkv, d_gate, d_small,
                              jnp.zeros((x.shape[0], SMALL_W - LANES), BF)], axis=1)
    d_h1 = _mm(d_proj, lw["win_all"], "nt", F32, "proj_dx")
    g["w_in"] = _ungroup_w_in(_mm(sv["h1"], d_proj, "tn", BF, "proj_dw"))
    d_x, g["norm_mix"], g["sc1"], g["sh1"] = _tile_bwd(
        "lnmod1_b", _lnmod_fn, [(x, D_MODEL, 0)], [lp["norm_mix"], md["sc1"], md["sh1"]], [d_h1], [F32], add_to=(0, d_x))
    return d_x, g


SMALL_REPL = ("norm_mix", "norm_ffn", "b_merge", "hgrn_lb_logits", "hgrn_norm", "ssm_conv_w", "ssm_conv_b", "ssm_dt_bias",
              "ssm_a_log", "ssm_d", "ssm_norm", "gdn_conv_w", "gdn_dt_bias", "gdn_a_log", "gdn_norm", "norm_final")
WEIGHTS = ("w_ada", "b_ada", "norm_mix", "norm_ffn", "w_in", "b_merge", "hgrn_lb_logits", "hgrn_norm", "ssm_conv_w",
           "ssm_conv_b", "ssm_dt_bias", "ssm_a_log", "ssm_d", "ssm_norm", "gdn_conv_w", "gdn_dt_bias", "gdn_a_log",
           "gdn_norm", "w_branch", "w_out", "w_ffn_in", "w_ffn_out", "norm_final")
SMALL_ROWS = 120


def _pad_rows(flat, n_rows, n_cols):
    return jnp.concatenate([flat, jnp.zeros((n_rows * n_cols - flat.shape[0],), flat.dtype)]).reshape(n_rows, n_cols)


def _device_step(x, tgt, mod, lb, wfull, sp, chip=None, core=None, order_after=None):
    mds, lps, svs = [], [], []
    h = x
    weights_of = wfull if callable(wfull) else (lambda layer, after: wfull[layer])
    wfull = []
    for l in range(DEPTH):
        wfull.append(weights_of(l, h))
        md = {n: mod[l, i * D_MODEL:(i + 1) * D_MODEL][None, :] for i, n in enumerate(("sh1", "sc1", "g1", "sh2", "sc2", "g2"))}
        if l == 0 and order_after is not None:
            md["sc1"] = md["sc1"] + order_after
        lp = {n: sp[n][l][None, :] for n in ("norm_mix", "norm_ffn", "b_merge", "hgrn_norm", "ssm_conv_b", "ssm_norm", "gdn_norm")}
        lp["lb"] = lb[l][None, :]
        lp["ssm_conv_w"], lp["gdn_conv_w"] = sp["ssm_conv_w"][l], sp["gdn_conv_w"][l]
        for n in ("ssm_dt_bias", "ssm_a_log", "ssm_d"):
            lp[n] = _lane_pad(sp[n][l], DT_OFF)
        for n in ("gdn_dt_bias", "gdn_a_log"):
            lp[n] = _lane_pad(sp[n][l], GA_OFF)
        h, sv = _layer_fwd(h, md, wfull[l], lp)
        mds.append(md), lps.append(lp), svs.append(sv)
    loss, dh, d_nf = _final_loss(h, tgt, sp["norm_final"][None, :])
    grads = [None] * DEPTH
    if core is None:
        for l in reversed(range(DEPTH)):
            dh, grads[l] = _layer_bwd(dh, svs[l], mds[l], wfull[l], lps[l])
        return loss, dh, d_nf, grads
    names = [n for n, _, _, _ in BIG]
    landed, flying = [None] * DEPTH, None
    for l in reversed(range(DEPTH)):
        md = mds[l]
        if flying is not None:
            md = dict(md, g2=md["g2"] + sum(tok[0, 0] for _, _, _, tok in flying.values()))
        dh, grads[l] = _layer_bwd(dh, svs[l], md, wfull[l], lps[l])
        if flying is not None:
            landed[l + 1] = {n: _own_slab(_exchange_wait(f"exchange_wait_{n}_{l + 1}", *flying[n][:3], dh), sums[n], chip)
                             for n in names}
        sums = _pair_stage({n: grads[l][n].astype(BF)[None] for n in names}, core)
        if l > 0:
            flying = {n: _exchange_start(f"exchange_start_{n}_{l}", sums[n], dh) for n in names}
        else:
            landed[0] = {n: _own_slab(_chip_exchange("chip_exchange_" + n, sums[n]), sums[n], chip) for n in names}
    parts = {n: jnp.concatenate([landed[l][n] for l in range(DEPTH)], axis=1) for n in names}
    return loss, dh, d_nf, grads, _finish_reduce(parts, DEPTH, core)


def kernel(x, c, w_ada, b_ada, norm_mix, norm_ffn, w_in, b_merge, hgrn_lb_logits, hgrn_norm, ssm_conv_w, ssm_conv_b, ssm_dt_bias, ssm_a_log, ssm_d, ssm_norm, gdn_conv_w, gdn_dt_bias, gdn_a_log, gdn_norm, w_branch, w_out, w_ffn_in, w_ffn_out, norm_final, loss_target, m_w_ada, m_b_ada, m_norm_mix, m_norm_ffn, m_w_in, m_b_merge, m_hgrn_lb_logits, m_hgrn_norm, m_ssm_conv_w, m_ssm_conv_b, m_ssm_dt_bias, m_ssm_a_log, m_ssm_d, m_ssm_norm, m_gdn_conv_w, m_gdn_dt_bias, m_gdn_a_log, m_gdn_norm, m_w_branch, m_w_out, m_w_ffn_in, m_w_ffn_out, m_norm_final, v_w_ada, v_b_ada, v_norm_mix, v_norm_ffn, v_w_in, v_b_merge, v_hgrn_lb_logits, v_hgrn_norm, v_ssm_conv_w, v_ssm_conv_b, v_ssm_dt_bias, v_ssm_a_log, v_ssm_d, v_ssm_norm, v_gdn_conv_w, v_gdn_dt_bias, v_gdn_a_log, v_gdn_norm, v_w_branch, v_w_out, v_w_ffn_in, v_w_ffn_out, v_norm_final):
    w = dict(w_ada=w_ada, b_ada=b_ada, norm_mix=norm_mix, norm_ffn=norm_ffn, w_in=w_in, b_merge=b_merge,
             hgrn_lb_logits=hgrn_lb_logits, hgrn_norm=hgrn_norm, ssm_conv_w=ssm_conv_w, ssm_conv_b=ssm_conv_b,
             ssm_dt_bias=ssm_dt_bias, ssm_a_log=ssm_a_log, ssm_d=ssm_d, ssm_norm=ssm_norm, gdn_conv_w=gdn_conv_w,
             gdn_dt_bias=gdn_dt_bias, gdn_a_log=gdn_a_log, gdn_norm=gdn_norm, w_branch=w_branch, w_out=w_out,
             w_ffn_in=w_ffn_in, w_ffn_out=w_ffn_out, norm_final=norm_final)
    m = dict(w_ada=m_w_ada, b_ada=m_b_ada, norm_mix=m_norm_mix, norm_ffn=m_norm_ffn, w_in=m_w_in, b_merge=m_b_merge,
             hgrn_lb_logits=m_hgrn_lb_logits, hgrn_norm=m_hgrn_norm, ssm_conv_w=m_ssm_conv_w, ssm_conv_b=m_ssm_conv_b,
             ssm_dt_bias=m_ssm_dt_bias, ssm_a_log=m_ssm_a_log, ssm_d=m_ssm_d, ssm_norm=m_ssm_norm, gdn_conv_w=m_gdn_conv_w,
             gdn_dt_bias=m_gdn_dt_bias, gdn_a_log=m_gdn_a_log, gdn_norm=m_gdn_norm, w_branch=m_w_branch, w_out=m_w_out,
             w_ffn_in=m_w_ffn_in, w_ffn_out=m_w_ffn_out, norm_final=m_norm_final)
    v = dict(w_ada=v_w_ada, b_ada=v_b_ada, norm_mix=v_norm_mix, norm_ffn=v_norm_ffn, w_in=v_w_in, b_merge=v_b_merge,
             hgrn_lb_logits=v_hgrn_lb_logits, hgrn_norm=v_hgrn_norm, ssm_conv_w=v_ssm_conv_w, ssm_conv_b=v_ssm_conv_b,
             ssm_dt_bias=v_ssm_dt_bias, ssm_a_log=v_ssm_a_log, ssm_d=v_ssm_d, ssm_norm=v_ssm_norm, gdn_conv_w=v_gdn_conv_w,
             gdn_dt_bias=v_gdn_dt_bias, gdn_a_log=v_gdn_a_log, gdn_norm=v_gdn_norm, w_branch=v_w_branch, w_out=v_w_out,
             w_ffn_in=v_w_ffn_in, w_ffn_out=v_w_ffn_out, norm_final=v_norm_final)
    xi, yi, ci = _place()
    chip, me = 2 * xi + yi, 4 * xi + 2 * yi + ci
    seq = x.shape[1]

    conv_flat = jnp.concatenate([ssm_conv_w.reshape(-1), gdn_conv_w.reshape(-1)])
    n_conv = conv_flat.shape[0]
    first = _all_gather_small("gather_c_conv", _pad_rows(jnp.concatenate([c[0], conv_flat]), 16, D_MODEL))
    c_all = first[:, 0, :]
    conv_all = first[0::2].reshape(N_CHIP, -1)[:, D_MODEL:D_MODEL + n_conv]
    n_ssm = ssm_conv_w.size
    sp = dict(w)
    sp["ssm_conv_w"] = jnp.concatenate([conv_all[j, :n_ssm].reshape(ssm_conv_w.shape) for j in range(N_CHIP)], axis=2)
    sp["gdn_conv_w"] = jnp.concatenate([conv_all[j, n_ssm:].reshape(gdn_conv_w.shape) for j in range(N_CHIP)], axis=2)

    ada_cols = w_ada.shape[2]
    mod_part = _ada_fwd(c_all, w_ada, lax.dynamic_slice_in_dim(b_ada, chip * ada_cols, ada_cols, axis=1))
    mod_all = _all_gather_small("gather_mod", mod_part.reshape(DEPTH * 8, ada_cols))[0::2].reshape(N_CHIP, DEPTH, 8, ada_cols)
    mod = lax.dynamic_index_in_dim(mod_all, me, axis=2, keepdims=False).transpose(1, 0, 2).reshape(DEPTH, N_CHIP * ada_cols)
    lb = _lb_fwd(hgrn_lb_logits)

    names = [n for n, _, _, _ in BIG]

    def layer_weights(full):
        full = dict(full)
        for nm, part in zip(("768", "xbc", "qkv", "gate", "small"), _regroup_w_in(full["w_in"])):
            full["win_" + nm] = part
        full["win_all"] = jnp.concatenate([full["win_" + nm] for nm in ("768", "xbc", "qkv", "gate", "small")], axis=1)
        return full

    first = _gather_weights({n: w[n][0:1] for n in names}, chip)
    first = layer_weights({n: first[n][0] for n in names})
    started = {}
    settled = mod[0:1, 0:LANES]
    for n in names:
        settled = settled + first[n].reshape(-1, first[n].shape[-1])[0:1, 0:LANES].astype(F32)
    for n, _, _, (rows, cols) in BIG:
        mine = w[n][1:].astype(BF).reshape((DEPTH - 1, rows, cols))
        started[n] = _exchange_start("gather_start_" + n, lax.dynamic_slice_in_dim(mine, ci * (rows // 2), rows // 2, axis=1),
                                     settled, spread=True)
    rest = []

    def weights_of(layer, after):
        if layer == 0:
            return first
        if not rest:
            full = {}
            for n, shape, ax, _ in BIG:
                sems, my_half, landing, _ = started[n]
                landed = _exchange_wait("gather_wait_" + n, sems, my_half, landing, after, spread=True)
                landed = lax.dynamic_update_slice(landed, my_half[None], (chip, 0, 0, 0))
                theirs = _pair_swap("gather_share_" + n, landed)
                slabs = jnp.concatenate([jnp.where(ci == 0, landed, theirs), jnp.where(ci == 0, theirs, landed)], axis=2)
                full[n] = _join_shards(slabs, ax, shape)
            rest.extend(layer_weights({n: full[n][i] for n in names}) for i in range(DEPTH - 1))
        return rest[layer - 1]

    order = sum(tok[0, 0] for _, _, _, tok in started.values())
    loss8, d_x, d_nf, lg, grad = _device_step(x[0], loss_target[0], mod, lb, weights_of, sp, chip, ci, order)

    dmod = jnp.stack([jnp.concatenate([lg[l][n] for n in ("sh1", "sc1", "g1", "sh2", "sc2", "g2")], axis=1)[0] for l in range(DEPTH)])
    d_lb = jnp.stack([lg[l]["lb"][0] for l in range(DEPTH)])
    contrib = {
        "norm_mix": jnp.stack([lg[l]["norm_mix"][0] for l in range(DEPTH)]),
        "norm_ffn": jnp.stack([lg[l]["norm_ffn"][0] for l in range(DEPTH)]),
        "b_merge": jnp.stack([lg[l]["b_merge"][0] for l in range(DEPTH)]),
        "hgrn_lb_logits": _lb_bwd(hgrn_lb_logits, d_lb),
        "hgrn_norm": jnp.stack([lg[l]["hgrn_norm"][0] for l in range(DEPTH)]),
        "ssm_conv_w": jnp.stack([lg[l]["ssm_conv_w"] for l in range(DEPTH)]),
        "ssm_conv_b": jnp.stack([lg[l]["ssm_conv_b"][0] for l in range(DEPTH)]),
        "ssm_dt_bias": jnp.stack([lg[l]["ssm_dt_bias"][0, DT_OFF:DT_OFF + 12] for l in range(DEPTH)]),
        "ssm_a_log": jnp.stack([lg[l]["ssm_a_log"][0, DT_OFF:DT_OFF + 12] for l in range(DEPTH)]),
        "ssm_d": jnp.stack([lg[l]["ssm_d"][0, DT_OFF:DT_OFF + 12] for l in range(DEPTH)]),
        "ssm_norm": jnp.stack([lg[l]["ssm_norm"][0] for l in range(DEPTH)]),
        "gdn_conv_w": jnp.stack([lg[l]["gdn_conv_w"] for l in range(DEPTH)]),
        "gdn_dt_bias": jnp.stack([lg[l]["gdn_dt_bias"][0, GA_OFF:GA_OFF + 6] for l in range(DEPTH)]),
        "gdn_a_log": jnp.stack([lg[l]["gdn_a_log"][0, GA_OFF:GA_OFF + 6] for l in range(DEPTH)]),
        "gdn_norm": jnp.stack([lg[l]["gdn_norm"][0] for l in range(DEPTH)]),
        "norm_final": d_nf[0],
    }
    flat = jnp.concatenate([dmod.reshape(-1)] + [contrib[n].reshape(-1) for n in SMALL_REPL] + [loss8[0, 0:1]])
    small_all = _all_gather_small("gather_small_grads", _pad_rows(flat, SMALL_ROWS, D_MODEL))
    total = _sum_leading("small_grad_sum", small_all, F32).reshape(-1)
    n_mod = dmod.size
    grad["b_ada"] = total[:n_mod].reshape(b_ada.shape)
    off = n_mod
    full_small = {}
    for n in SMALL_REPL:
        full_small[n] = total[off:off + contrib[n].size].reshape(contrib[n].shape)
        off += contrib[n].size
    loss = total[off]
    for n in SMALL_REPL:
        if n in ("ssm_conv_w", "gdn_conv_w"):
            cols = w[n].shape[2]
            grad[n] = lax.dynamic_slice_in_dim(full_small[n], chip * cols, cols, axis=2)
        else:
            grad[n] = full_small[n]
    dmod_cols = lax.dynamic_slice_in_dim(small_all[:, :n_mod // D_MODEL, :].reshape(8, DEPTH, -1), chip * ada_cols, ada_cols, axis=2)
    grad["w_ada"] = _ada_bwd(c_all, dmod_cols.transpose(1, 0, 2))

    delta, new_m, new_v = {}, {}, {}
    big_names = ("w_ada",) + tuple(n for n, _, _, _ in BIG)
    for n in big_names:
        delta[n], new_m[n], new_v[n] = _adamw("adamw_" + n, w[n], grad[n], m[n], v[n])
    small_names = [n for n in WEIGHTS if n not in big_names]
    packs = [_pad_rows(jnp.concatenate([d[n].reshape(-1) for n in small_names]), 584, LANES) for d in (w, grad, m, v)]
    outs = _ew("adamw_small", _adamw_fn, packs, [F32, F32, F32])
    off = 0
    for n in small_names:
        for dst, o in zip((delta, new_m, new_v), outs):
            dst[n] = o.reshape(-1)[off:off + w[n].size].reshape(w[n].shape)
        off += w[n].size
    return (loss, d_x[None], *[grad[n] for n in WEIGHTS], *[delta[n] for n in WEIGHTS], *[new_m[n] for n in WEIGHTS],
            *[new_v[n] for n in WEIGHTS])
```

```python
import functools

import jax
import jax.numpy as jnp
from jax import lax
from jax.experimental import pallas as pl
from jax.experimental.pallas import tpu as pltpu

F32 = jnp.float32
BF = jnp.bfloat16
HI = lax.Precision.HIGHEST

D_MODEL = 1024
DEPTH = 4
CHUNK = 64
MIX_W = 768
HEAD = 128
N_HEAD6 = 6
SSM_P = 64
SSM_N = 128
CONV_CH = 1280
QKV_W = 2304
FFN_H = 2816
IN_WIDTH = 11288
NORM_EPS = 1e-6
F_MIN = 1e-30
HALO = 8
HEAD_GROUP = 6
HGRN_SUB = 8
SMALL_W = 512
LANES = 128
DT_OFF, GB_OFF, GA_OFF = 0, 12, 18

ADAM_LR, ADAM_B1, ADAM_B2, ADAM_EPS, ADAM_WD, ADAM_STEP = 0.001, 0.9, 0.999, 1e-08, 0.01, 10

VMEM_LIMIT = 56 * 1024 * 1024
TOKEN_TILE = 256


def _pc(body, **kw):
    return pl.pallas_call(body, **kw)


def _cparams(sem):
    return pltpu.CompilerParams(dimension_semantics=sem, vmem_limit_bytes=VMEM_LIMIT)


def _bdot(a, b):
    return jnp.dot(a.astype(BF), b.astype(BF), preferred_element_type=F32)


def _bdot_nt(a, b):
    return lax.dot_general(a.astype(BF), b.astype(BF), (((1,), (1,)), ((), ())), preferred_element_type=F32)


def _bdot_tn(a, b):
    return lax.dot_general(a.astype(BF), b.astype(BF), (((0,), (0,)), ((), ())), preferred_element_type=F32)


def _silu(x):
    return x * jax.nn.sigmoid(x)


def _tri_mask(n, strict=False):
    t = lax.broadcasted_iota(jnp.int32, (n, n), 0)
    s = lax.broadcasted_iota(jnp.int32, (n, n), 1)
    return (s < t) if strict else (s <= t)


def _masked_exp(diff, mask):
    return jnp.where(mask, jnp.exp(jnp.where(mask, diff, 0.0)), 0.0)


def _split_bf16(x, n):
    parts, rest = [], x
    for _ in range(n):
        p = rest.astype(BF)
        parts.append(p)
        rest = rest - p.astype(F32)
    return parts


def _tri_sum(x, reverse):
    n, w = x.shape
    t = lax.broadcasted_iota(jnp.int32, (n, n), 0)
    s = lax.broadcasted_iota(jnp.int32, (n, n), 1)
    tri = jnp.where((s >= t) if reverse else (s <= t), 1.0, 0.0).astype(BF)
    y = jnp.dot(tri, jnp.concatenate(_split_bf16(x, 3), axis=1), preferred_element_type=F32)
    return y[:, :w] + y[:, w:2 * w] + y[:, 2 * w:]


@jax.custom_vjp
def _cumsum_rows(x):
    return _tri_sum(x, False)


_cumsum_rows.defvjp(lambda x: (_tri_sum(x, False), None), lambda _, g: (_tri_sum(g, True),))


def _dot_split(a, b, transpose_a=False):
    dims = (((0,), (0,)) if transpose_a else ((1,), (0,)), ((), ()))
    a_hi, a_lo = _split_bf16(a, 2)
    b_hi, b_lo = _split_bf16(b, 2)
    w = b.shape[1]
    y = lax.dot_general(a_hi, jnp.concatenate([b_hi, b_lo], axis=1), dims, preferred_element_type=F32)
    return y[:, :w] + y[:, w:] + lax.dot_general(a_lo, b_hi, dims, preferred_element_type=F32)


def _rms(x, w):
    return x * lax.rsqrt(jnp.mean(x * x, axis=-1, keepdims=True) + NORM_EPS) * w


def _causal_conv(halo, x, w):
    ext = jnp.concatenate([halo, x], axis=0)
    n = x.shape[0]
    acc = w[0:1, :] * ext[HALO - 3:HALO - 3 + n, :]
    for i in range(1, 4):
        acc = acc + w[i:i + 1, :] * ext[HALO - 3 + i:HALO - 3 + i + n, :]
    return acc


def _unit_lower_inverses(mats):
    n = mats[0].shape[0]
    t = lax.broadcasted_iota(jnp.int32, (n, n), 0)
    s_ = lax.broadcasted_iota(jnp.int32, (n, n), 1)
    xs = [jnp.where(t == s_, 1.0, 0.0).astype(F32) for _ in mats]
    for s in range(n - 1):
        r0 = 8 * ((s + 1) // 8)
        for i, a in enumerate(mats):
            x = xs[i]
            low = x[r0:] - a[r0:, s:s + 1] * x[s:s + 1, :]
            xs[i] = low if r0 == 0 else jnp.concatenate([x[:r0], low], axis=0)
    return xs


@jax.custom_vjp
def _unit_lower_solves(mats, rhss):
    return [_dot_split(inv, r) for inv, r in zip(_unit_lower_inverses(mats), rhss)]


def _uls_fwd(mats, rhss):
    invs = _unit_lower_inverses(mats)
    xs = [_dot_split(inv, r) for inv, r in zip(invs, rhss)]
    return xs, (invs, xs)


def _uls_bwd(res, gs):
    invs, xs = res
    ys = [_dot_split(inv, g, transpose_a=True) for inv, g in zip(invs, gs)]
    das = [jnp.where(_tri_mask(CHUNK, strict=True), -_bdot_nt(y, x), 0.0) for y, x in zip(ys, xs)]
    return das, ys


_unit_lower_solves.defvjp(_uls_fwd, _uls_bwd)


def _hgrn_chunk(tiles, halos, state, consts):
    q_raw, f_raw, v_all, g_raw = tiles
    lb, norm_w = consts
    q_all = _silu(q_raw)
    f = lb + (1.0 - lb) * jax.nn.sigmoid(f_raw)
    logf = jnp.log(jnp.maximum(f, F_MIN))
    k_all = (1.0 - lb) * jax.nn.sigmoid(-f_raw)
    b_all = _cumsum_rows(logf)
    sub = HGRN_SUB
    row = lax.broadcasted_iota(jnp.int32, (sub, 1), 0)
    src_row = lax.broadcasted_iota(jnp.int32, (CHUNK, 1), 0)
    src_lane = lax.broadcasted_iota(jnp.int32, (1, CHUNK), 1)
    heads = range(N_HEAD6)
    n_sub = CHUNK // sub
    cols = [slice(h * HEAD, (h + 1) * HEAD) for h in heads]
    qs, ks, vs, bs = ([a[:, sl] for sl in cols] for a in (q_all, k_all, v_all, b_all))
    o_inter = [_bdot_nt(qs[h] * jnp.exp(bs[h]), state[h]) for h in heads]
    blocks = [[None] * n_sub for _ in heads]
    for i in range(n_sub):
        r0 = i * sub
        for h in heads:
            if i > 0:
                ref = bs[h][r0 - 1:r0, :]
                blocks[h][i] = _bdot_nt(qs[h][r0:r0 + sub] * jnp.exp(bs[h][r0:r0 + sub] - ref),
                                        ks[h] * _masked_exp(ref - bs[h], src_row < r0))
            else:
                blocks[h][i] = jnp.zeros((sub, CHUNK), F32)
    for h in heads:
        for i in range(n_sub):
            r0 = i * sub
            qi, ki, bi = qs[h][r0:r0 + sub], ks[h][r0:r0 + sub], bs[h][r0:r0 + sub]
            for s in range(sub):
                e = _masked_exp(bi - bi[s:s + 1, :], row >= s)
                col = jnp.sum(qi * ki[s:s + 1, :] * e, axis=1, keepdims=True)
                blocks[h][i] = jnp.where(src_lane == r0 + s, col, blocks[h][i])
    os_ = [_bdot(jnp.concatenate(blocks[h], axis=0), vs[h]) + o_inter[h] for h in heads]
    ends = [bs[h][CHUNK - 1:CHUNK, :] for h in heads]
    new_states = [state[h] * jnp.exp(ends[h]) + _bdot_tn(vs[h], ks[h] * jnp.exp(ends[h] - bs[h])) for h in heads]
    outs = [_rms(os_[h], norm_w) * _silu(g_raw[:, cols[h]]) for h in heads]
    return (jnp.concatenate(outs, axis=1),), jnp.stack(new_states)


def _ssd_chunk(tiles, halos, state, consts):
    z, xbc_raw, small = tiles
    (halo,) = halos
    conv_w, conv_b, dt_bias, a_log, d_skip, norm_w = consts
    xbc = _silu(_causal_conv(halo, xbc_raw, conv_w) + conv_b)
    xs, bm, cm = xbc[:, :MIX_W], xbc[:, MIX_W:MIX_W + 2 * SSM_N], xbc[:, MIX_W + 2 * SSM_N:]
    dt = jax.nn.softplus(small + dt_bias)
    cum = _cumsum_rows(-jnp.exp(a_log) * dt)
    cum_t2 = jnp.concatenate([cum, cum], axis=0).T
    lane = lax.broadcasted_iota(jnp.int32, (1, LANES), 1)
    first = lane < SSM_P
    hm0 = jnp.where(first, 1.0, 0.0).astype(F32)
    hm1 = 1.0 - hm0
    src = jnp.where(first, lane, lane - SSM_P)
    tri2 = src <= lax.broadcasted_iota(jnp.int32, (CHUNK, 1), 0)
    pick = lambda a, b: jnp.where(first, a, b)
    bgs = [bm[:, g * SSM_N:(g + 1) * SSM_N] for g in range(2)]
    cgs = [cm[:, g * SSM_N:(g + 1) * SSM_N] for g in range(2)]
    gmats = [_bdot_nt(cgs[g], jnp.concatenate([bgs[g], bgs[g]], axis=0)) for g in range(2)]
    pairs = range(6)
    xps = [xs[:, p * LANES:(p + 1) * LANES] for p in pairs]
    c0s = [cum[:, 2 * p:2 * p + 1] for p in pairs]
    c1s = [cum[:, 2 * p + 1:2 * p + 2] for p in pairs]
    e0s = [cum[CHUNK - 1:CHUNK, 2 * p:2 * p + 1] for p in pairs]
    e1s = [cum[CHUNK - 1:CHUNK, 2 * p + 1:2 * p + 2] for p in pairs]
    segs = [_masked_exp(pick(c0s[p], c1s[p]) - pick(cum_t2[2 * p:2 * p + 1, :], cum_t2[2 * p + 1:2 * p + 2, :]), tri2)
            for p in pairs]
    vms = []
    for p in pairs:
        v = xps[p] * pick(dt[:, 2 * p:2 * p + 1], dt[:, 2 * p + 1:2 * p + 2])
        vms.append(jnp.concatenate([v * hm0, v * hm1], axis=0))
    y_intra = [_bdot(gmats[p // 3] * segs[p], vms[p]) for p in pairs]
    y_inter = [_bdot(jnp.concatenate([cgs[p // 3] * jnp.exp(c0s[p]), cgs[p // 3] * jnp.exp(c1s[p])], axis=1),
                     jnp.concatenate([state[p] * hm0, state[p] * hm1], axis=0)) for p in pairs]
    new_states = [_bdot_tn(jnp.concatenate([bgs[p // 3] * jnp.exp(e0s[p] - c0s[p]),
                                            bgs[p // 3] * jnp.exp(e1s[p] - c1s[p])], axis=0), vms[p])
                  + state[p] * pick(jnp.exp(e0s[p]), jnp.exp(e1s[p])) for p in pairs]
    ys = [y_intra[p] + y_inter[p] + pick(d_skip[:, 2 * p:2 * p + 1], d_skip[:, 2 * p + 1:2 * p + 2]) * xps[p] for p in pairs]
    y = jnp.concatenate(ys, axis=1) * _silu(z)
    gw = MIX_W // 2
    y = jnp.concatenate([_rms(y[:, g * gw:(g + 1) * gw], norm_w[:, g * gw:(g + 1) * gw]) for g in range(2)], axis=1)
    return (y,), jnp.stack(new_states)


def _gdn_chunk(tiles, halos, state, consts):
    qkv_raw, z, small = tiles
    (halo,) = halos
    conv_w, dt_bias, a_log, norm_w = consts
    qkv = _silu(_causal_conv(halo, qkv_raw, conv_w))
    beta_all = jax.nn.sigmoid(small)
    cum = _cumsum_rows(-jnp.exp(a_log) * jax.nn.softplus(small + dt_bias))
    cum_t = cum.T
    tri, tri_strict = _tri_mask(CHUNK), _tri_mask(CHUNK, strict=True)

    def group(hs):
        n = range(len(hs))
        qs, ks, betas, cs, ces, decays, rhss = [], [], [], [], [], [], []
        for h in hs:
            q = qkv[:, h * HEAD:(h + 1) * HEAD]
            k = qkv[:, MIX_W + h * HEAD:MIX_W + (h + 1) * HEAD]
            v = qkv[:, 2 * MIX_W + h * HEAD:2 * MIX_W + (h + 1) * HEAD]
            q = q * lax.rsqrt(jnp.sum(q * q, axis=-1, keepdims=True) + NORM_EPS) * (HEAD ** -0.5)
            k = k * lax.rsqrt(jnp.sum(k * k, axis=-1, keepdims=True) + NORM_EPS)
            beta = beta_all[:, GB_OFF + h:GB_OFF + h + 1]
            c, c_t = cum[:, GA_OFF + h:GA_OFF + h + 1], cum_t[GA_OFF + h:GA_OFF + h + 1, :]
            qs.append(q), ks.append(k), betas.append(beta), cs.append(c)
            ces.append(cum[CHUNK - 1:CHUNK, GA_OFF + h:GA_OFF + h + 1])
            decays.append(_masked_exp(c - c_t, tri))
            rhss.append(jnp.concatenate([v * beta, k * (beta * jnp.exp(c))], axis=1))
        sts = [state[h] for h in hs]
        qk_kks = [_bdot_nt(jnp.concatenate([qs[i], ks[i]], axis=0), ks[i]) for i in n]
        sols = _unit_lower_solves([jnp.where(tri_strict, betas[i] * qk_kks[i][CHUNK:] * decays[i], 0.0) for i in n], rhss)
        on_states = [_bdot(jnp.concatenate([sols[i][:, HEAD:], qs[i] * jnp.exp(cs[i])], axis=0), sts[i]) for i in n]
        us = [sols[i][:, :HEAD] - on_states[i][:CHUNK] for i in n]
        os_ = [on_states[i][CHUNK:] + _bdot(qk_kks[i][:CHUNK] * decays[i], us[i]) for i in n]
        new = [jnp.exp(ces[i]) * sts[i] + _bdot_tn(ks[i] * jnp.exp(ces[i] - cs[i]), us[i]) for i in n]
        outs = [_rms(os_[i], norm_w) * _silu(z[:, h * HEAD:(h + 1) * HEAD]) for i, h in enumerate(hs)]
        return outs, new

    outs, new_states = [], []
    for h0 in range(0, N_HEAD6, HEAD_GROUP):
        o, s = group(list(range(h0, h0 + HEAD_GROUP)))
        outs += o
        new_states += s
    return (jnp.concatenate(outs, axis=1),), jnp.stack(new_states)


def _scan_fwd(name, fn, tiled, halo_idx, consts, out_width, state_shape):
    seq = tiled[0][0].shape[0]
    nc = seq // CHUNK
    n_t, n_h, n_c = len(tiled), len(halo_idx), len(consts)

    def body(*refs):
        t_refs, h_refs, c_refs = refs[:n_t], refs[n_t:n_t + n_h], refs[n_t + n_h:n_t + n_h + n_c]
        y_ref, save_ref, st_ref = refs[n_t + n_h + n_c:]
        i = pl.program_id(0)

        @pl.when(i == 0)
        def _():
            st_ref[...] = jnp.zeros_like(st_ref)

        flag = jnp.where(i > 0, 1.0, 0.0).astype(F32)
        st = st_ref[...]
        (y,), new = fn([r[...] for r in t_refs], [r[...] * flag for r in h_refs], st, [r[...] for r in c_refs])
        save_ref[0] = st
        y_ref[...] = y.astype(y_ref.dtype)
        st_ref[...] = new

    in_specs = [pl.BlockSpec((CHUNK, w), functools.partial(lambda i, cb: (i, cb), cb=cb)) for _, w, cb in tiled]
    in_specs += [pl.BlockSpec((HALO, tiled[j][1]),
                              functools.partial(lambda i, cb: (jnp.maximum(i * (CHUNK // HALO) - 1, 0), cb), cb=tiled[j][2]))
                 for j in halo_idx]
    in_specs += [pl.BlockSpec(c.shape, functools.partial(lambda i, nd: (0,) * nd, nd=c.ndim)) for c in consts]
    zeros = (0,) * len(state_shape)
    return _pc(
        body, name=name, grid=(nc,), in_specs=in_specs,
        out_specs=(pl.BlockSpec((CHUNK, out_width), lambda i: (i, 0)),
                   pl.BlockSpec((1,) + state_shape, lambda i: (i,) + zeros)),
        out_shape=(jax.ShapeDtypeStruct((seq, out_width), BF), jax.ShapeDtypeStruct((nc,) + state_shape, F32)),
        scratch_shapes=[pltpu.VMEM(state_shape, F32)],
        compiler_params=_cparams(("arbitrary",)),
    )(*[t[0] for t in tiled], *[tiled[j][0] for j in halo_idx], *consts)


def _scan_bwd(name, fn, tiled, halo_idx, consts, saved, dy, dtile_dtypes, extra=None):
    seq = tiled[0][0].shape[0]
    nc = seq // CHUNK
    n_t, n_h, n_c = len(tiled), len(halo_idx), len(consts)
    state_shape = saved.shape[1:]
    n_x = 0 if extra is None else 1

    def body(*refs):
        t_refs, h_refs, c_refs = refs[:n_t], refs[n_t:n_t + n_h], refs[n_t + n_h:n_t + n_h + n_c]
        pos = n_t + n_h + n_c
        save_ref, dy_ref = refs[pos], refs[pos + 1]
        x_refs = refs[pos + 2:pos + 2 + n_x]
        pos += 2 + n_x
        dt_refs, dc_refs = refs[pos:pos + n_t], refs[pos + n_t:pos + n_t + n_c]
        dst_ref = refs[pos + n_t + n_c]
        carry_refs = refs[pos + n_t + n_c + 1:]
        i = pl.program_id(0)

        @pl.when(i == 0)
        def _():
            dst_ref[...] = jnp.zeros_like(dst_ref)
            for r in carry_refs:
                r[...] = jnp.zeros_like(r)
            for r in dc_refs:
                r[...] = jnp.zeros_like(r)

        flag = jnp.where(i < nc - 1, 1.0, 0.0).astype(F32)
        tiles = [r[...] for r in t_refs]
        halos = [r[...] * flag for r in h_refs]
        cvals = [r[...] for r in c_refs]
        _, vjp = jax.vjp(fn, tiles, halos, save_ref[0], cvals)
        d_tiles, d_halos, d_state, d_consts = vjp(((dy_ref[...].astype(F32),), dst_ref[...]))
        dst_ref[...] = d_state
        for r, g in zip(dc_refs, d_consts):
            r[...] += g
        for j, (r, g) in enumerate(zip(dt_refs, d_tiles)):
            if extra is not None and extra[0] == j:
                g = g + x_refs[0][...].astype(F32)
            r[...] = g.astype(r.dtype)
            if j in halo_idx:
                cr = carry_refs[halo_idx.index(j)]
                r[CHUNK - HALO:CHUNK, :] = (g[CHUNK - HALO:CHUNK, :] + cr[...]).astype(r.dtype)
                cr[...] = d_halos[halo_idx.index(j)] * flag

    rev = lambda i: nc - 1 - i
    in_specs = [pl.BlockSpec((CHUNK, w), functools.partial(lambda i, cb: (rev(i), cb), cb=cb)) for _, w, cb in tiled]
    in_specs += [pl.BlockSpec((HALO, tiled[j][1]),
                              functools.partial(lambda i, cb: (jnp.maximum(rev(i) * (CHUNK // HALO) - 1, 0), cb), cb=tiled[j][2]))
                 for j in halo_idx]
    in_specs += [pl.BlockSpec(c.shape, functools.partial(lambda i, nd: (0,) * nd, nd=c.ndim)) for c in consts]
    zeros = (0,) * len(state_shape)
    in_specs += [pl.BlockSpec((1,) + state_shape, lambda i: (rev(i),) + zeros),
                 pl.BlockSpec((CHUNK, dy.shape[1]), lambda i: (rev(i), 0))]
    args = [t[0] for t in tiled] + [tiled[j][0] for j in halo_idx] + list(consts) + [saved, dy]
    if extra is not None:
        in_specs.append(pl.BlockSpec((CHUNK, extra[1].shape[1]), lambda i: (rev(i), 0)))
        args.append(extra[1])
    out_specs = [pl.BlockSpec((CHUNK, w), lambda i: (rev(i), 0)) for _, w, _ in tiled]
    out_specs += [pl.BlockSpec(c.shape, functools.partial(lambda i, nd: (0,) * nd, nd=c.ndim)) for c in consts]
    out_shape = [jax.ShapeDtypeStruct((seq, w), dtd) for (_, w, _), dtd in zip(tiled, dtile_dtypes)]
    out_shape += [jax.ShapeDtypeStruct(c.shape, F32) for c in consts]
    scratch = [pltpu.VMEM(state_shape, F32)] + [pltpu.VMEM((HALO, tiled[j][1]), F32) for j in halo_idx]
    return _pc(body, name=name, grid=(nc,), in_specs=in_specs, out_specs=tuple(out_specs), out_shape=tuple(out_shape),
               scratch_shapes=scratch, compiler_params=_cparams(("arbitrary",)))(*args)


def _tile_fwd(name, fn, tiled, consts, outs, tm=TOKEN_TILE):
    seq = tiled[0][0].shape[0]
    n_t, n_c = len(tiled), len(consts)

    def body(*refs):
        res = fn(*[r[...] for r in refs[:n_t + n_c]])
        for r, y in zip(refs[n_t + n_c:], res):
            r[...] = y.astype(r.dtype)

    in_specs = [pl.BlockSpec((tm, w), functools.partial(lambda i, cb: (i, cb), cb=cb)) for _, w, cb in tiled]
    in_specs += [pl.BlockSpec(c.shape, functools.partial(lambda i, nd: (0,) * nd, nd=c.ndim)) for c in consts]
    return _pc(body, name=name, grid=(seq // tm,), in_specs=in_specs,
               out_specs=tuple(pl.BlockSpec((tm, w), lambda i: (i, 0)) for w, _ in outs),
               out_shape=tuple(jax.ShapeDtypeStruct((seq, w), dtp) for w, dtp in outs),
               compiler_params=_cparams(("arbitrary",)))(*[t[0] for t in tiled], *consts)


def _tile_bwd(name, fn, tiled, consts, douts, dtile_dtypes, add_to=None, tm=TOKEN_TILE):
    seq = tiled[0][0].shape[0]
    n_t, n_c, n_o = len(tiled), len(consts), len(douts)
    n_x = 0 if add_to is None else 1
    keep = [j for j, dtp in enumerate(dtile_dtypes) if dtp is not None]

    def body(*refs):
        vals = [r[...].astype(F32) for r in refs[:n_t + n_c]]
        pos = n_t + n_c
        g_refs, x_refs = refs[pos:pos + n_o], refs[pos + n_o:pos + n_o + n_x]
        pos += n_o + n_x
        dt_refs, dc_refs = refs[pos:pos + len(keep)], refs[pos + len(keep):]
        i = pl.program_id(0)

        @pl.when(i == 0)
        def _():
            for r in dc_refs:
                r[...] = jnp.zeros_like(r)

        _, vjp = jax.vjp(fn, *vals)
        cts = vjp(tuple(g[...].astype(F32) for g in g_refs))
        for r, j in zip(dt_refs, keep):
            g = cts[j]
            if add_to is not None and add_to[0] == j:
                g = g + x_refs[0][...].astype(F32)
            r[...] = g.astype(r.dtype)
        for r, g in zip(dc_refs, cts[n_t:]):
            r[...] += g

    in_specs = [pl.BlockSpec((tm, w), functools.partial(lambda i, cb: (i, cb), cb=cb)) for _, w, cb in tiled]
    in_specs += [pl.BlockSpec(c.shape, functools.partial(lambda i, nd: (0,) * nd, nd=c.ndim)) for c in consts]
    in_specs += [pl.BlockSpec((tm, g.shape[1]), lambda i: (i, 0)) for g in douts]
    args = [t[0] for t in tiled] + list(consts) + list(douts)
    if add_to is not None:
        in_specs.append(pl.BlockSpec((tm, add_to[1].shape[1]), lambda i: (i, 0)))
        args.append(add_to[1])
    out_specs = [pl.BlockSpec((tm, tiled[j][1]), lambda i: (i, 0)) for j in keep]
    out_specs += [pl.BlockSpec(c.shape, functools.partial(lambda i, nd: (0,) * nd, nd=c.ndim)) for c in consts]
    out_shape = [jax.ShapeDtypeStruct((seq, tiled[j][1]), dtile_dtypes[j]) for j in keep]
    out_shape += [jax.ShapeDtypeStruct(c.shape, F32) for c in consts]
    return _pc(body, name=name, grid=(seq // tm,), in_specs=in_specs, out_specs=tuple(out_specs),
               out_shape=tuple(out_shape), compiler_params=_cparams(("arbitrary",)))(*args)


def _lnmod_fn(x, nw, sc, sh):
    return (_rms(x, nw) * (1.0 + sc) + sh,)


def _gated_fn(o, g):
    return ((1.0 + g) * o,)


def _resid_fn(x, o, g):
    return (x + (1.0 + g) * o,)


def _swiglu_fn(gu):
    return (_silu(gu[:, :FFN_H]) * gu[:, FFN_H:],)


def _merge_fn(yh, ys, yg, logits, wb, b_merge):
    gates = jax.nn.sigmoid(logits + b_merge)
    acc = None
    for n, y in enumerate((yh, ys, yg)):
        t = gates[:, n * D_MODEL:(n + 1) * D_MODEL] * _bdot(y, wb[n])
        acc = t if acc is None else acc + t
    return (acc,)


MM_VMEM_BUDGET = 40 * 1024 * 1024
MM_TILE_CAP = 1024
MM_K_CAP = 4096


def _divisor(n, cap, unit=LANES):
    best = None
    for d in range(unit, min(n, cap) + 1, unit):
        if n % d == 0:
            best = d
    return n if best is None else best


def _mm_tiles(m, n, k, out_bytes):
    tk = k if k <= MM_K_CAP else _divisor(k, 3072)
    tm, tn = _divisor(m, MM_TILE_CAP), _divisor(n, MM_TILE_CAP + MM_TILE_CAP // 2)

    def need(tm_, tn_):
        acc = tm_ * tn_ * 4 if tk < k else 0
        return 2 * 2 * tk * (tm_ + tn_) + acc + 2 * tm_ * tn_ * out_bytes

    while need(tm, tn) > MM_VMEM_BUDGET:
        if tn >= tm and _divisor(n, tn - LANES) < tn:
            tn = _divisor(n, tn - LANES)
        elif _divisor(m, tm - LANES) < tm:
            tm = _divisor(m, tm - LANES)
        else:
            break
    return tm, tn, tk


def _mm(a, b, mode, out_dtype, name):
    if mode == "nn":
        (m, k), n = a.shape, b.shape[1]
    elif mode == "nt":
        (m, k), n = a.shape, b.shape[0]
    else:
        (k, m), n = a.shape, b.shape[1]
    tm, tn, tk = _mm_tiles(m, n, k, jnp.dtype(out_dtype).itemsize)
    nk = k // tk
    dims = {"nn": ((1,), (0,)), "nt": ((1,), (1,)), "tn": ((0,), (0,))}[mode]

    def body_one(a_ref, b_ref, o_ref):
        o_ref[...] = lax.dot_general(a_ref[...], b_ref[...], (dims, ((), ())), preferred_element_type=F32).astype(o_ref.dtype)

    def body_acc(a_ref, b_ref, o_ref, acc_ref):
        kk = pl.program_id(2)

        @pl.when(kk == 0)
        def _():
            acc_ref[...] = jnp.zeros_like(acc_ref)

        acc_ref[...] += lax.dot_general(a_ref[...], b_ref[...], (dims, ((), ())), preferred_element_type=F32)

        @pl.when(kk == nk - 1)
        def _():
            o_ref[...] = acc_ref[...].astype(o_ref.dtype)

    a_spec = pl.BlockSpec((tk, tm), lambda i, j, kk: (kk, i)) if mode == "tn" else pl.BlockSpec((tm, tk), lambda i, j, kk: (i, kk))
    b_spec = pl.BlockSpec((tn, tk), lambda i, j, kk: (j, kk)) if mode == "nt" else pl.BlockSpec((tk, tn), lambda i, j, kk: (kk, j))
    return _pc(body_one if nk == 1 else body_acc, name=name, grid=(m // tm, n // tn, nk), in_specs=[a_spec, b_spec],
               out_specs=pl.BlockSpec((tm, tn), lambda i, j, kk: (i, j)),
               out_shape=jax.ShapeDtypeStruct((m, n), out_dtype),
               scratch_shapes=[] if nk == 1 else [pltpu.VMEM((tm, tn), F32)],
               compiler_params=_cparams(("parallel", "parallel", "arbitrary")))(a.astype(BF), b.astype(BF))


def _final_loss(x, tgt, norm_final, tm=TOKEN_TILE):
    seq = x.shape[0]

    def fn(xv, nf, tv):
        err = jnp.square(_rms(xv, nf) - tv)
        return 0.5 * jnp.sum(jnp.mean(err, axis=-1))

    def body(x_ref, t_ref, nf_ref, loss_ref, dx_ref, dnf_ref):
        i = pl.program_id(0)

        @pl.when(i == 0)
        def _():
            loss_ref[...] = jnp.zeros_like(loss_ref)
            dnf_ref[...] = jnp.zeros_like(dnf_ref)

        val, vjp = jax.vjp(functools.partial(fn, tv=t_ref[...]), x_ref[...], nf_ref[...])
        dx, dnf = vjp(jnp.ones((), F32))
        dx_ref[...] = dx
        dnf_ref[...] += dnf
        loss_ref[...] += jnp.broadcast_to(val, loss_ref.shape)

    return _pc(body, name="final_loss", grid=(seq // tm,),
               in_specs=[pl.BlockSpec((tm, D_MODEL), lambda i: (i, 0)), pl.BlockSpec((tm, D_MODEL), lambda i: (i, 0)),
                         pl.BlockSpec((1, D_MODEL), lambda i: (0, 0))],
               out_specs=(pl.BlockSpec((8, LANES), lambda i: (0, 0)), pl.BlockSpec((tm, D_MODEL), lambda i: (i, 0)),
                          pl.BlockSpec((1, D_MODEL), lambda i: (0, 0))),
               out_shape=(jax.ShapeDtypeStruct((8, LANES), F32), jax.ShapeDtypeStruct((seq, D_MODEL), F32),
                          jax.ShapeDtypeStruct((1, D_MODEL), F32)),
               compiler_params=_cparams(("arbitrary",)))(x, tgt, norm_final)


def _ada_fwd(c_all, w_ada, b_ada_cols):
    n_l, _, cols = w_ada.shape

    def body(c_ref, w_ref, b_ref, o_ref):
        o_ref[0] = jnp.dot(_silu(c_ref[...]), w_ref[0], preferred_element_type=F32, precision=HI) + b_ref[0]

    return _pc(body, name="ada_fwd", grid=(n_l,),
               in_specs=[pl.BlockSpec((8, D_MODEL), lambda l: (0, 0)), pl.BlockSpec((1, D_MODEL, cols), lambda l: (l, 0, 0)),
                         pl.BlockSpec((1, 1, cols), lambda l: (l, 0, 0))],
               out_specs=pl.BlockSpec((1, 8, cols), lambda l: (l, 0, 0)),
               out_shape=jax.ShapeDtypeStruct((n_l, 8, cols), F32),
               compiler_params=_cparams(("arbitrary",)))(c_all, w_ada, b_ada_cols.reshape(n_l, 1, cols))


def _ada_bwd(c_all, dmod_cols):
    n_l, _, cols = dmod_cols.shape

    def body(c_ref, g_ref, o_ref):
        o_ref[0] = lax.dot_general(_silu(c_ref[...]), g_ref[0], (((0,), (0,)), ((), ())), preferred_element_type=F32,
                                   precision=HI)

    return _pc(body, name="ada_bwd", grid=(n_l,),
               in_specs=[pl.BlockSpec((8, D_MODEL), lambda l: (0, 0)), pl.BlockSpec((1, 8, cols), lambda l: (l, 0, 0))],
               out_specs=pl.BlockSpec((1, D_MODEL, cols), lambda l: (l, 0, 0)),
               out_shape=jax.ShapeDtypeStruct((n_l, D_MODEL, cols), F32),
               compiler_params=_cparams(("arbitrary",)))(c_all, dmod_cols)


def _lb_fn(logits):
    e = jnp.exp(logits - jnp.max(logits, axis=0, keepdims=True))
    p = e / jnp.sum(e, axis=0, keepdims=True)
    r = lax.broadcasted_iota(jnp.int32, (DEPTH, 1), 0)
    lb = jnp.zeros_like(p)
    for j in range(1, DEPTH):
        lb = lb + jnp.where(r >= j, p[j:j + 1, :], 0.0)
    return lb


def _lb_fwd(logits):
    def body(l_ref, o_ref):
        o_ref[...] = _lb_fn(l_ref[...])

    return _pc(body, name="lb_fwd", out_shape=jax.ShapeDtypeStruct(logits.shape, F32))(logits)


def _lb_bwd(logits, dlb):
    def body(l_ref, g_ref, o_ref):
        _, vjp = jax.vjp(_lb_fn, l_ref[...])
        o_ref[...] = vjp(g_ref[...])[0]

    return _pc(body, name="lb_bwd", out_shape=jax.ShapeDtypeStruct(logits.shape, F32))(logits, dlb)


def _rows_for(n_rows, n_cols):
    r = 8
    while r * 2 <= n_rows and n_rows % (r * 2) == 0 and r * 2 * n_cols <= 256 * 1024:
        r *= 2
    return r if n_rows % r == 0 else n_rows


def _ew(name, fn, ins, out_dtypes):
    n_rows, n_cols = ins[0].shape
    tr = _rows_for(n_rows, n_cols)
    n_in = len(ins)

    def body(*refs):
        res = fn(*[r[...] for r in refs[:n_in]])
        for r, y in zip(refs[n_in:], res):
            r[...] = y.astype(r.dtype)

    spec = pl.BlockSpec((tr, n_cols), lambda i: (i, 0))
    return _pc(body, name=name, grid=(n_rows // tr,), in_specs=[spec] * n_in, out_specs=tuple([spec] * len(out_dtypes)),
               out_shape=tuple(jax.ShapeDtypeStruct((n_rows, n_cols), d) for d in out_dtypes),
               compiler_params=_cparams(("arbitrary",)))(*ins)


def _adamw_fn(w, g, m, v):
    m = ADAM_B1 * m + (1.0 - ADAM_B1) * g
    v = ADAM_B2 * v + (1.0 - ADAM_B2) * jnp.square(g)
    m_hat = m / (1.0 - ADAM_B1 ** ADAM_STEP)
    v_hat = v / (1.0 - ADAM_B2 ** ADAM_STEP)
    return -ADAM_LR * (m_hat / (jnp.sqrt(v_hat) + ADAM_EPS) + ADAM_WD * w), m, v


def _adamw(name, w, g, m, v):
    shape = w.shape
    two = (-1, shape[-1])
    d, nm, nv = _ew(name, _adamw_fn, [a.reshape(two) for a in (w, g, m, v)], [F32, F32, F32])
    return d.reshape(shape), nm.reshape(shape), nv.reshape(shape)


def _sum_leading(name, a, out_dtype):
    n, n_rows, n_cols = a.shape
    tr = _rows_for(n_rows, n_cols)

    def body(a_ref, o_ref):
        acc = a_ref[0].astype(F32)
        for j in range(1, n):
            acc = acc + a_ref[j].astype(F32)
        o_ref[...] = acc.astype(o_ref.dtype)

    return _pc(body, name=name, grid=(n_rows // tr,), in_specs=[pl.BlockSpec((n, tr, n_cols), lambda i: (0, i, 0))],
               out_specs=pl.BlockSpec((tr, n_cols), lambda i: (i, 0)),
               out_shape=jax.ShapeDtypeStruct((n_rows, n_cols), out_dtype),
               compiler_params=_cparams(("arbitrary",)))(a)


MESH = pl.DeviceIdType.MESH
ANY = pl.BlockSpec(memory_space=pl.ANY)


def _place():
    return lax.axis_index("x"), lax.axis_index("y"), lax.axis_index("c")


def _all_gather_small(name, a):
    m_per, n = a.shape

    def body(x_ref, out_ref, send_sems, recv_sems, local_sem):
        x, y, c = _place()
        me, sibling = (x, y, c), (x, y, 1 - c)
        chips = [(1 - x, y), (x, 1 - y), (1 - x, 1 - y)]

        def rows(px, py, pc):
            return out_ref.at[pl.ds((4 * px + 2 * py + pc) * m_per, m_per), :]

        def copy(k, block, to, src=None):
            return pltpu.make_async_remote_copy(src_ref=rows(*block) if src is None else src, dst_ref=rows(*block),
                                                send_sem=send_sems.at[k], recv_sem=recv_sems.at[k], device_id=to,
                                                device_id_type=MESH)

        mine = pltpu.make_async_copy(x_ref, rows(*me), local_sem)
        mine.start()
        first = [copy(0, me, sibling, src=x_ref)]
        first += [copy(1 + j, me, (*chip, c), src=x_ref) for j, chip in enumerate(chips)]
        for cp in first:
            cp.start()
        passed = [copy(4 + j, (*chip, c), sibling) for j, chip in enumerate(chips)]
        for j, chip in enumerate(chips):
            copy(1 + j, (*chip, c), me).wait_recv()
            passed[j].start()
        copy(0, sibling, me).wait_recv()
        for j, chip in enumerate(chips):
            copy(4 + j, (*chip, 1 - c), me).wait_recv()
        for cp in first + passed:
            cp.wait_send()
        mine.wait()

    out = _pc(body, name=name, out_shape=jax.ShapeDtypeStruct((8 * m_per, n), a.dtype),
              in_specs=[pl.BlockSpec(memory_space=pltpu.VMEM)], out_specs=pl.BlockSpec(memory_space=pltpu.VMEM),
              scratch_shapes=[pltpu.SemaphoreType.DMA((7,)), pltpu.SemaphoreType.DMA((7,)), pltpu.SemaphoreType.DMA],
              compiler_params=pltpu.CompilerParams(vmem_limit_bytes=VMEM_LIMIT))(a)
    return out.reshape(8, m_per, n)


def _chip_gather(name, pack):
    n_l, n_r, n_c = pack.shape
    half = n_r // 2

    def body(p_ref, o_ref, send_sems, recv_sems):
        x, y, c = _place()
        sibling = (x, y, 1 - c)
        chips = [(1 - x, y), (x, 1 - y), (1 - x, 1 - y)]

        def slab(px, py, pc):
            return o_ref.at[2 * px + py, :, pl.ds(pc * half, half), :]

        def copy(k, src, dst, to):
            return pltpu.make_async_remote_copy(src_ref=src, dst_ref=dst, send_sem=send_sems.at[k], recv_sem=recv_sems.at[k],
                                                device_id=to, device_id_type=MESH)

        first = [copy(j, p_ref.at[:, pl.ds(c * half, half), :], slab(x, y, c), (*chip, c)) for j, chip in enumerate(chips)]
        for cp in first:
            cp.start()
        passed = [copy(3 + j, slab(*chip, c), slab(*chip, c), sibling) for j, chip in enumerate(chips)]
        for j, chip in enumerate(chips):
            copy(j, slab(*chip, c), slab(*chip, c), (*chip, c)).wait_recv()
            passed[j].start()
        for j, chip in enumerate(chips):
            copy(3 + j, slab(*chip, 1 - c), slab(*chip, 1 - c), sibling).wait_recv()
        for cp in first + passed:
            cp.wait_send()

    return _pc(body, name=name, out_shape=jax.ShapeDtypeStruct((4, n_l, n_r, n_c), pack.dtype), in_specs=[ANY], out_specs=ANY,
               scratch_shapes=[pltpu.SemaphoreType.DMA((6,)), pltpu.SemaphoreType.DMA((6,))])(pack)


def _pair_swap(name, give):
    def body(g_ref, o_ref, send_sem, recv_sem):
        x, y, c = _place()
        cp = pltpu.make_async_remote_copy(src_ref=g_ref, dst_ref=o_ref, send_sem=send_sem, recv_sem=recv_sem,
                                          device_id=(x, y, 1 - c), device_id_type=MESH)
        cp.start()
        cp.wait()

    return _pc(body, name=name, out_shape=jax.ShapeDtypeStruct(give.shape, give.dtype), in_specs=[ANY], out_specs=ANY,
               scratch_shapes=[pltpu.SemaphoreType.DMA, pltpu.SemaphoreType.DMA])(give)


def _pair_join(name, mine, axis, core):
    half = mine.shape[axis]
    shape = mine.shape[:axis] + (2 * half,) + mine.shape[axis + 1:]

    def body(m_ref, o_ref, send_sem, recv_sem):
        x, y, c = _place()

        def rows(which):
            idx = [slice(None)] * len(shape)
            idx[axis] = pl.ds(which * half, half)
            return o_ref.at[tuple(idx)]

        cp = pltpu.make_async_remote_copy(src_ref=m_ref, dst_ref=rows(c), send_sem=send_sem, recv_sem=recv_sem,
                                          device_id=(x, y, 1 - c), device_id_type=MESH)
        cp.start()
        pltpu.make_async_remote_copy(src_ref=m_ref, dst_ref=rows(1 - c), send_sem=send_sem, recv_sem=recv_sem,
                                     device_id=(x, y, 1 - c), device_id_type=MESH).wait_recv()
        cp.wait_send()

    out = _pc(body, name=name, out_shape=jax.ShapeDtypeStruct(shape, mine.dtype), in_specs=[ANY], out_specs=ANY,
              scratch_shapes=[pltpu.SemaphoreType.DMA, pltpu.SemaphoreType.DMA])(mine)
    return lax.dynamic_update_slice_in_dim(out, mine, core * half, axis)


def _chip_exchange(name, parts):
    def body(p_ref, o_ref, send_sems, recv_sems):
        x, y, c = _place()
        me = 2 * x + y
        chips = [(1 - x, y), (x, 1 - y), (1 - x, 1 - y)]

        def copy(k, src, dst, to):
            return pltpu.make_async_remote_copy(src_ref=src, dst_ref=dst, send_sem=send_sems.at[k], recv_sem=recv_sems.at[k],
                                                device_id=to, device_id_type=MESH)

        sends = [copy(j, p_ref.at[2 * px + py], o_ref.at[me], (px, py, c)) for j, (px, py) in enumerate(chips)]
        for cp in sends:
            cp.start()
        for j, (px, py) in enumerate(chips):
            copy(j, p_ref.at[2 * px + py], o_ref.at[2 * px + py], (px, py, c)).wait_recv()
        for cp in sends:
            cp.wait_send()

    return _pc(body, name=name, out_shape=jax.ShapeDtypeStruct(parts.shape, parts.dtype), in_specs=[ANY], out_specs=ANY,
               scratch_shapes=[pltpu.SemaphoreType.DMA((3,)), pltpu.SemaphoreType.DMA((3,))])(parts)


N_CHIP = 4
BIG = (("w_in", (1024, 2822), 1, (1024, 2822)), ("w_branch", (3, 768, 256), 2, (2304, 256)),
       ("w_out", (256, 1024), 0, (256, 1024)), ("w_ffn_in", (1024, 1408), 1, (1024, 1408)),
       ("w_ffn_out", (704, 1024), 0, (704, 1024)))
G768 = ((0, 3072), (3072, 3840), (7436, 8204))
GXBC, GQKV, GGATE = (3840, 5120), (5132, 7436), (8216, 11288)
GSMALL = ((5120, 5132), (8204, 8210), (8210, 8216))
W768, WXBC, WGATE = 4608, CONV_CH, 3 * D_MODEL
IN_PAD = W768 + WXBC + QKV_W + WGATE + SMALL_W


def _join_shards(slabs, axis, shard_shape):
    n_l = slabs.shape[1]
    parts = [slabs[j].reshape((n_l,) + shard_shape) for j in range(N_CHIP)]
    return jnp.concatenate(parts, axis=axis + 1)


def _split_shards(full, axis, rows_cols):
    n_l = full.shape[0]
    size = full.shape[axis + 1] // N_CHIP
    return jnp.stack([lax.slice_in_dim(full, j * size, (j + 1) * size, axis=axis + 1).reshape((n_l,) + rows_cols)
                      for j in range(N_CHIP)])


def _gather_weights(w, chip, big=BIG):
    out = {}
    for n, shape, ax, rc in big:
        n_l = w[n].shape[0]
        mine = w[n].astype(BF).reshape((n_l,) + rc)
        slabs = lax.dynamic_update_slice(_chip_gather("gather_" + n, mine), mine[None], (chip, 0, 0, 0))
        out[n] = _join_shards(slabs, ax, shape)
    return out


def _pair_stage(full_grads, core, big=BIG):
    out = {}
    for n, _, ax, (rows, cols) in big:
        n_l = full_grads[n].shape[0]
        slabs = _split_shards(full_grads[n], ax, (rows, cols))
        half = rows // 2
        keep = lax.dynamic_slice_in_dim(slabs, core * half, half, axis=2).reshape(-1, cols)
        give = lax.dynamic_slice_in_dim(slabs, (1 - core) * half, half, axis=2).reshape(-1, cols)
        got = _pair_swap("pair_swap_" + n, give)
        (pair_sum,) = _ew("pair_sum_" + n, lambda a, b: (a.astype(F32) + b.astype(F32),), [keep, got], [BF])
        out[n] = pair_sum.reshape(N_CHIP, n_l * half, cols)
    return out


def _own_slab(landed, pair_sum, chip):
    return lax.dynamic_update_slice(landed, lax.dynamic_slice_in_dim(pair_sum, chip, 1, axis=0), (chip, 0, 0))


def _finish_reduce(parts, n_l, core, big=BIG):
    out = {}
    for n, shape, _, (rows, cols) in big:
        half = rows // 2
        mine = _sum_leading("chip_sum_" + n, parts[n], F32).reshape(n_l, half, cols)
        full = _pair_join("pair_share_" + n, mine, 1, core)
        out[n] = full.reshape((n_l,) + shape)
    return out


def _reduce_grads(full_grads, chip, core, big=BIG):
    pair_sums = _pair_stage(full_grads, core, big)
    parts = {n: _own_slab(_chip_exchange("chip_exchange_" + n, pair_sums[n]), pair_sums[n], chip) for n, _, _, _ in big}
    return _finish_reduce(parts, full_grads[big[0][0]].shape[0], core, big)


HBM_SPEC = pl.BlockSpec(memory_space=pltpu.HBM)
SEM_SPEC = pl.BlockSpec(memory_space=pltpu.SEMAPHORE)
DATAFLOW = pltpu.SideEffectType.DATAFLOW_SIDE_EFFECTING


def _exchange_copies(p_ref, land_ref, sems, waiting, spread):
    x, y, c = _place()
    me = 2 * x + y
    out = []
    for j, (px, py) in enumerate([(1 - x, y), (x, 1 - y), (1 - x, 1 - y)]):
        out.append(pltpu.make_async_remote_copy(src_ref=p_ref if spread else p_ref.at[2 * px + py],
                                                dst_ref=land_ref.at[2 * px + py if waiting else me],
                                                send_sem=sems[j], recv_sem=sems[3 + j], device_id=(px, py, c),
                                                device_id_type=MESH))
    return out


def _exchange_start(name, parts, after, spread=False):
    land_shape = ((N_CHIP,) + parts.shape) if spread else parts.shape

    def body(p_ref, land_ref, after_ref, s0, s1, s2, r0, r1, r2, p_thru, land_thru, token):
        for cp in _exchange_copies(p_ref, land_ref, (s0, s1, s2, r0, r1, r2), False, spread):
            cp.start()
        token[...] = jnp.zeros_like(token)

    res = _pc(body, name=name,
              out_shape=(pltpu.SemaphoreType.DMA(()),) * 6 + (pltpu.HBM(parts.shape, parts.dtype), pltpu.HBM(land_shape, parts.dtype),
                                                            jax.ShapeDtypeStruct((8, LANES), F32)),
              in_specs=(HBM_SPEC, HBM_SPEC, ANY),
              out_specs=(SEM_SPEC,) * 6 + (HBM_SPEC, HBM_SPEC, pl.BlockSpec(memory_space=pltpu.VMEM)),
              input_output_aliases={0: 6, 1: 7}, compiler_params=pltpu.CompilerParams(has_side_effects=DATAFLOW))(
        pltpu.with_memory_space_constraint(parts, pltpu.HBM),
        pltpu.with_memory_space_constraint(lax.empty(land_shape, parts.dtype), pltpu.HBM), after)
    return res[:6], res[6], res[7], res[8]


def _exchange_wait(name, sems, p_thru, land_thru, after, spread=False):
    def body(p_ref, land_ref, s0, s1, s2, r0, r1, r2, after_ref, p_dead, got_ref):
        for cp in _exchange_copies(p_ref, land_ref, (s0, s1, s2, r0, r1, r2), True, spread):
            cp.wait_send()
            cp.wait_recv()

    return _pc(body, name=name, out_shape=(pltpu.HBM(p_thru.shape, p_thru.dtype), pltpu.HBM(land_thru.shape, land_thru.dtype)),
               in_specs=(HBM_SPEC, HBM_SPEC) + (SEM_SPEC,) * 6 + (ANY,), out_specs=(HBM_SPEC, HBM_SPEC),
               input_output_aliases={0: 0, 1: 1}, compiler_params=pltpu.CompilerParams(has_side_effects=DATAFLOW))(
        p_thru, land_thru, *sems, after)[1]


def _regroup_w_in(w):
    cat = lambda spans: jnp.concatenate([w[:, a:b] for a, b in spans], axis=1)
    small = jnp.concatenate([cat(GSMALL), jnp.zeros((w.shape[0], SMALL_W - 24), w.dtype)], axis=1)
    return cat(G768), cat((GXBC,)), cat((GQKV,)), cat((GGATE,)), small


def _ungroup_w_in(d):
    o_xbc, o_qkv, o_gate, o_small = W768, W768 + WXBC, W768 + WXBC + QKV_W, W768 + WXBC + QKV_W + WGATE
    spans = ((0, 3072), (3072, 3840), (o_xbc, o_xbc + WXBC), (o_small, o_small + 12), (o_qkv, o_qkv + QKV_W),
             (3840, 4608), (o_small + 12, o_small + 18), (o_small + 18, o_small + 24), (o_gate, o_gate + WGATE))
    return jnp.concatenate([d[:, a:b] for a, b in spans], axis=1)


def _lane_pad(v, off):
    return jnp.pad(v, (off, LANES - off - v.shape[0]))[None, :]


STATE6 = (N_HEAD6, HEAD, HEAD)


def _mixer_inputs(sv, lp):
    p768, pxbc, pqkv, psmall = sv["p768"], sv["pxbc"], sv["pqkv"], sv["psmall"]
    hgrn = ([(p768, MIX_W, j) for j in range(4)], [], [lp["lb"], lp["hgrn_norm"]])
    ssd = ([(p768, MIX_W, 4), (pxbc, CONV_CH, 0), (psmall, LANES, 0)], [1],
           [lp["ssm_conv_w"], lp["ssm_conv_b"], lp["ssm_dt_bias"], lp["ssm_a_log"], lp["ssm_d"], lp["ssm_norm"]])
    gdn = ([(pqkv, QKV_W, 0), (p768, MIX_W, 5), (psmall, LANES, 0)], [0],
           [lp["gdn_conv_w"], lp["gdn_dt_bias"], lp["gdn_a_log"], lp["gdn_norm"]])
    return hgrn, ssd, gdn


def _layer_fwd(x, md, lw, lp):
    sv = {"x": x}
    (sv["h1"],) = _tile_fwd("lnmod1", _lnmod_fn, [(x, D_MODEL, 0)], [lp["norm_mix"], md["sc1"], md["sh1"]], [(D_MODEL, BF)])
    for nm in ("768", "xbc", "qkv", "gate", "small"):
        sv["p" + nm] = _mm(sv["h1"], lw["win_" + nm], "nn", F32, "proj_" + nm)
    hgrn, ssd, gdn = _mixer_inputs(sv, lp)
    sv["y_h"], sv["st_h"] = _scan_fwd("hgrn_fwd", _hgrn_chunk, *hgrn, MIX_W, STATE6)
    sv["y_s"], sv["st_s"] = _scan_fwd("ssd_fwd", _ssd_chunk, *ssd, MIX_W, STATE6)
    sv["y_g"], sv["st_g"] = _scan_fwd("gdn_fwd", _gdn_chunk, *gdn, MIX_W, STATE6)
    (sv["merged"],) = _tile_fwd("merge", _merge_fn, _merge_tiles(sv), [lw["w_branch"], lp["b_merge"]], [(D_MODEL, BF)])
    sv["out"] = _mm(sv["merged"], lw["w_out"], "nn", F32, "out_proj")
    (sv["x_mid"],) = _tile_fwd("resid1", _resid_fn, [(x, D_MODEL, 0), (sv["out"], D_MODEL, 0)], [md["g1"]], [(D_MODEL, F32)])
    (sv["h2"],) = _tile_fwd("lnmod2", _lnmod_fn, [(sv["x_mid"], D_MODEL, 0)], [lp["norm_ffn"], md["sc2"], md["sh2"]],
                            [(D_MODEL, BF)])
    sv["gu"] = _mm(sv["h2"], lw["w_ffn_in"], "nn", F32, "ffn_in")
    (sv["act"],) = _tile_fwd("swiglu", _swiglu_fn, [(sv["gu"], 2 * FFN_H, 0)], [], [(FFN_H, BF)])
    sv["o2"] = _mm(sv["act"], lw["w_ffn_out"], "nn", F32, "ffn_out")
    (x_out,) = _tile_fwd("resid2", _resid_fn, [(sv["x_mid"], D_MODEL, 0), (sv["o2"], D_MODEL, 0)], [md["g2"]], [(D_MODEL, F32)])
    return x_out, sv


def _merge_tiles(sv):
    return [(sv["y_h"], MIX_W, 0), (sv["y_s"], MIX_W, 0), (sv["y_g"], MIX_W, 0), (sv["pgate"], WGATE, 0)]


def _layer_bwd(dx_out, sv, md, lw, lp):
    g = {}
    x, x_mid = sv["x"], sv["x_mid"]
    d_o2, g["g2"] = _tile_bwd("resid2_b", _gated_fn, [(sv["o2"], D_MODEL, 0)], [md["g2"]], [dx_out], [BF])
    d_xmid = dx_out
    d_act =_mm(d_o2, lw["w_ffn_out"], "nt", F32, "ffn_out_dx")
    g["w_ffn_out"] = _mm(sv["act"], d_o2, "tn", BF, "ffn_out_dw")
    (d_gu,) = _tile_bwd("swiglu_b", _swiglu_fn, [(sv["gu"], 2 * FFN_H, 0)], [], [d_act], [BF])
    d_h2 = _mm(d_gu, lw["w_ffn_in"], "nt", F32, "ffn_in_dx")
    g["w_ffn_in"] = _mm(sv["h2"], d_gu, "tn", BF, "ffn_in_dw")
    d_xmid, g["norm_ffn"], g["sc2"], g["sh2"] = _tile_bwd(
        "lnmod2_b", _lnmod_fn, [(x_mid, D_MODEL, 0)], [lp["norm_ffn"], md["sc2"], md["sh2"]], [d_h2], [F32], add_to=(0, d_xmid))
    d_out, g["g1"] = _tile_bwd("resid1_b", _gated_fn, [(sv["out"], D_MODEL, 0)], [md["g1"]], [d_xmid], [BF])
    d_x = d_xmid
    d_merged = _mm(d_out, lw["w_out"], "nt", F32, "out_proj_dx")
    g["w_out"] = _mm(sv["merged"], d_out, "tn", BF, "out_proj_dw")
    d_yh, d_ys, d_yg, d_gate, g["w_branch"], g["b_merge"] = _tile_bwd(
        "merge_b", _merge_fn, _merge_tiles(sv), [lw["w_branch"], lp["b_merge"]], [d_merged], [F32, F32, F32, BF])
    hgrn, ssd, gdn = _mixer_inputs(sv, lp)
    d_q, d_f, d_v, d_g, g["lb"], g["hgrn_norm"] = _scan_bwd("hgrn_bwd", _hgrn_chunk, *hgrn, sv["st_h"], d_yh, [BF] * 4)
    (d_sz, d_xbc, d_small, g["ssm_conv_w"], g["ssm_conv_b"], g["ssm_dt_bias"], g["ssm_a_log"], g["ssm_d"],
     g["ssm_norm"]) = _scan_bwd("ssd_bwd", _ssd_chunk, *ssd, sv["st_s"], d_ys, [BF, BF, F32])
    d_qkv, d_gz, d_small, g["gdn_conv_w"], g["gdn_dt_bias"], g["gdn_a_log"], g["gdn_norm"] = _scan_bwd(
        "gdn_bwd", _gdn_chunk, *gdn, sv["st_g"], d_yg, [BF, BF, BF], extra=(2, d_small))
    d_proj = jnp.concatenate([d_q, d_f, d_v, d_g, d_sz, d_gz, d_xbc, d_qkv, d_gate, d_small,
                              jnp.zeros((x.shape[0], SMALL_W - LANES), BF)], axis=1)
    d_h1 = _mm(d_proj, lw["win_all"], "nt", F32, "proj_dx")
    g["w_in"] = _ungroup_w_in(_mm(sv["h1"], d_proj, "tn", BF, "proj_dw"))
    d_x, g["norm_mix"], g["sc1"], g["sh1"] = _tile_bwd(
        "lnmod1_b", _lnmod_fn, [(x, D_MODEL, 0)], [lp["norm_mix"], md["sc1"], md["sh1"]], [d_h1], [F32], add_to=(0, d_x))
    return d_x, g


SMALL_REPL = ("norm_mix", "norm_ffn", "b_merge", "hgrn_lb_logits", "hgrn_norm", "ssm_conv_w", "ssm_conv_b", "ssm_dt_bias",
              "ssm_a_log", "ssm_d", "ssm_norm", "gdn_conv_w", "gdn_dt_bias", "gdn_a_log", "gdn_norm", "norm_final")
WEIGHTS = ("w_ada", "b_ada", "norm_mix", "norm_ffn", "w_in", "b_merge", "hgrn_lb_logits", "hgrn_norm", "ssm_conv_w",
           "ssm_conv_b", "ssm_dt_bias", "ssm_a_log", "ssm_d", "ssm_norm", "gdn_conv_w", "gdn_dt_bias", "gdn_a_log",
           "gdn_norm", "w_branch", "w_out", "w_ffn_in", "w_ffn_out", "norm_final")
SMALL_ROWS = 120


def _pad_rows(flat, n_rows, n_cols):
    return jnp.concatenate([flat, jnp.zeros((n_rows * n_cols - flat.shape[0],), flat.dtype)]).reshape(n_rows, n_cols)


def _device_step(x, tgt, mod, lb, wfull, sp, chip=None, core=None, order_after=None):
    mds, lps, svs = [], [], []
    h = x
    weights_of = wfull if callable(wfull) else (lambda layer, after: wfull[layer])
    wfull = []
    for l in range(DEPTH):
        wfull.append(weights_of(l, h))
        md = {n: mod[l, i * D_MODEL:(i + 1) * D_MODEL][None, :] for i, n in enumerate(("sh1", "sc1", "g1", "sh2", "sc2", "g2"))}
        if l == 0 and order_after is not None:
            md["sc1"] = md["sc1"] + order_after
        lp = {n: sp[n][l][None, :] for n in ("norm_mix", "norm_ffn", "b_merge", "hgrn_norm", "ssm_conv_b", "ssm_norm", "gdn_norm")}
        lp["lb"] = lb[l][None, :]
        lp["ssm_conv_w"], lp["gdn_conv_w"] = sp["ssm_conv_w"][l], sp["gdn_conv_w"][l]
        for n in ("ssm_dt_bias", "ssm_a_log", "ssm_d"):
            lp[n] = _lane_pad(sp[n][l], DT_OFF)
        for n in ("gdn_dt_bias", "gdn_a_log"):
            lp[n] = _lane_pad(sp[n][l], GA_OFF)
        h, sv = _layer_fwd(h, md, wfull[l], lp)
        mds.append(md), lps.append(lp), svs.append(sv)
    loss, dh, d_nf = _final_loss(h, tgt, sp["norm_final"][None, :])
    grads = [None] * DEPTH
    if core is None:
        for l in reversed(range(DEPTH)):
            dh, grads[l] = _layer_bwd(dh, svs[l], mds[l], wfull[l], lps[l])
        return loss, dh, d_nf, grads
    names = [n for n, _, _, _ in BIG]
    landed, flying = [None] * DEPTH, None
    for l in reversed(range(DEPTH)):
        md = mds[l]
        if flying is not None:
            md = dict(md, g2=md["g2"] + sum(tok[0, 0] for _, _, _, tok in flying.values()))
        dh, grads[l] = _layer_bwd(dh, svs[l], md, wfull[l], lps[l])
        if flying is not None:
            landed[l + 1] = {n: _own_slab(_exchange_wait(f"exchange_wait_{n}_{l + 1}", *flying[n][:3], dh), sums[n], chip)
                             for n in names}
        sums = _pair_stage({n: grads[l][n].astype(BF)[None] for n in names}, core)
        if l > 0:
            flying = {n: _exchange_start(f"exchange_start_{n}_{l}", sums[n], dh) for n in names}
        else:
            landed[0] = {n: _own_slab(_chip_exchange("chip_exchange_" + n, sums[n]), sums[n], chip) for n in names}
    parts = {n: jnp.concatenate([landed[l][n] for l in range(DEPTH)], axis=1) for n in names}
    return loss, dh, d_nf, grads, _finish_reduce(parts, DEPTH, core)


def kernel(x, c, w_ada, b_ada, norm_mix, norm_ffn, w_in, b_merge, hgrn_lb_logits, hgrn_norm, ssm_conv_w, ssm_conv_b, ssm_dt_bias, ssm_a_log, ssm_d, ssm_norm, gdn_conv_w, gdn_dt_bias, gdn_a_log, gdn_norm, w_branch, w_out, w_ffn_in, w_ffn_out, norm_final, loss_target, m_w_ada, m_b_ada, m_norm_mix, m_norm_ffn, m_w_in, m_b_merge, m_hgrn_lb_logits, m_hgrn_norm, m_ssm_conv_w, m_ssm_conv_b, m_ssm_dt_bias, m_ssm_a_log, m_ssm_d, m_ssm_norm, m_gdn_conv_w, m_gdn_dt_bias, m_gdn_a_log, m_gdn_norm, m_w_branch, m_w_out, m_w_ffn_in, m_w_ffn_out, m_norm_final, v_w_ada, v_b_ada, v_norm_mix, v_norm_ffn, v_w_in, v_b_merge, v_hgrn_lb_logits, v_hgrn_norm, v_ssm_conv_w, v_ssm_conv_b, v_ssm_dt_bias, v_ssm_a_log, v_ssm_d, v_ssm_norm, v_gdn_conv_w, v_gdn_dt_bias, v_gdn_a_log, v_gdn_norm, v_w_branch, v_w_out, v_w_ffn_in, v_w_ffn_out, v_norm_final):
    w = dict(w_ada=w_ada, b_ada=b_ada, norm_mix=norm_mix, norm_ffn=norm_ffn, w_in=w_in, b_merge=b_merge,
             hgrn_lb_logits=hgrn_lb_logits, hgrn_norm=hgrn_norm, ssm_conv_w=ssm_conv_w, ssm_conv_b=ssm_conv_b,
             ssm_dt_bias=ssm_dt_bias, ssm_a_log=ssm_a_log, ssm_d=ssm_d, ssm_norm=ssm_norm, gdn_conv_w=gdn_conv_w,
             gdn_dt_bias=gdn_dt_bias, gdn_a_log=gdn_a_log, gdn_norm=gdn_norm, w_branch=w_branch, w_out=w_out,
             w_ffn_in=w_ffn_in, w_ffn_out=w_ffn_out, norm_final=norm_final)
    m = dict(w_ada=m_w_ada, b_ada=m_b_ada, norm_mix=m_norm_mix, norm_ffn=m_norm_ffn, w_in=m_w_in, b_merge=m_b_merge,
             hgrn_lb_logits=m_hgrn_lb_logits, hgrn_norm=m_hgrn_norm, ssm_conv_w=m_ssm_conv_w, ssm_conv_b=m_ssm_conv_b,
             ssm_dt_bias=m_ssm_dt_bias, ssm_a_log=m_ssm_a_log, ssm_d=m_ssm_d, ssm_norm=m_ssm_norm, gdn_conv_w=m_gdn_conv_w,
             gdn_dt_bias=m_gdn_dt_bias, gdn_a_log=m_gdn_a_log, gdn_norm=m_gdn_norm, w_branch=m_w_branch, w_out=m_w_out,
             w_ffn_in=m_w_ffn_in, w_ffn_out=m_w_ffn_out, norm_final=m_norm_final)
    v = dict(w_ada=v_w_ada, b_ada=v_b_ada, norm_mix=v_norm_mix, norm_ffn=v_norm_ffn, w_in=v_w_in, b_merge=v_b_merge,
             hgrn_lb_logits=v_hgrn_lb_logits, hgrn_norm=v_hgrn_norm, ssm_conv_w=v_ssm_conv_w, ssm_conv_b=v_ssm_conv_b,
             ssm_dt_bias=v_ssm_dt_bias, ssm_a_log=v_ssm_a_log, ssm_d=v_ssm_d, ssm_norm=v_ssm_norm, gdn_conv_w=v_gdn_conv_w,
             gdn_dt_bias=v_gdn_dt_bias, gdn_a_log=v_gdn_a_log, gdn_norm=v_gdn_norm, w_branch=v_w_branch, w_out=v_w_out,
             w_ffn_in=v_w_ffn_in, w_ffn_out=v_w_ffn_out, norm_final=v_norm_final)
    xi, yi, ci = _place()
    chip, me = 2 * xi + yi, 4 * xi + 2 * yi + ci
    seq = x.shape[1]

    conv_flat = jnp.concatenate([ssm_conv_w.reshape(-1), gdn_conv_w.reshape(-1)])
    n_conv = conv_flat.shape[0]
    first = _all_gather_small("gather_c_conv", _pad_rows(jnp.concatenate([c[0], conv_flat]), 16, D_MODEL))
    c_all = first[:, 0, :]
    conv_all = first[0::2].reshape(N_CHIP, -1)[:, D_MODEL:D_MODEL + n_conv]
    n_ssm = ssm_conv_w.size
    sp = dict(w)
    sp["ssm_conv_w"] = jnp.concatenate([conv_all[j, :n_ssm].reshape(ssm_conv_w.shape) for j in range(N_CHIP)], axis=2)
    sp["gdn_conv_w"] = jnp.concatenate([conv_all[j, n_ssm:].reshape(gdn_conv_w.shape) for j in range(N_CHIP)], axis=2)

    ada_cols = w_ada.shape[2]
    mod_part = _ada_fwd(c_all, w_ada, lax.dynamic_slice_in_dim(b_ada, chip * ada_cols, ada_cols, axis=1))
    mod_all = _all_gather_small("gather_mod", mod_part.reshape(DEPTH * 8, ada_cols))[0::2].reshape(N_CHIP, DEPTH, 8, ada_cols)
    mod = lax.dynamic_index_in_dim(mod_all, me, axis=2, keepdims=False).transpose(1, 0, 2).reshape(DEPTH, N_CHIP * ada_cols)
    lb = _lb_fwd(hgrn_lb_logits)

    names = [n for n, _, _, _ in BIG]

    def layer_weights(full):
        full = dict(full)
        for nm, part in zip(("768", "xbc", "qkv", "gate", "small"), _regroup_w_in(full["w_in"])):
            full["win_" + nm] = part
        full["win_all"] = jnp.concatenate([full["win_" + nm] for nm in ("768", "xbc", "qkv", "gate", "small")], axis=1)
        return full

    first = _gather_weights({n: w[n][0:1] for n in names}, chip)
    first = layer_weights({n: first[n][0] for n in names})
    started = {}
    settled = mod[0:1, 0:LANES]
    for n in names:
        settled = settled + first[n].reshape(-1, first[n].shape[-1])[0:1, 0:LANES].astype(F32)
    for n, _, _, (rows, cols) in BIG:
        mine = w[n][1:].astype(BF).reshape((DEPTH - 1, rows, cols))
        started[n] = _exchange_start("gather_start_" + n, lax.dynamic_slice_in_dim(mine, ci * (rows // 2), rows // 2, axis=1),
                                     settled, spread=True)
    rest = []

    def weights_of(layer, after):
        if layer == 0:
            return first
        if not rest:
            full = {}
            for n, shape, ax, _ in BIG:
                sems, my_half, landing, _ = started[n]
                landed = _exchange_wait("gather_wait_" + n, sems, my_half, landing, after, spread=True)
                landed = lax.dynamic_update_slice(landed, my_half[None], (chip, 0, 0, 0))
                slabs = _pair_join("gather_share_" + n, landed, 2, ci)
                full[n] = _join_shards(slabs, ax, shape)
            rest.extend(layer_weights({n: full[n][i] for n in names}) for i in range(DEPTH - 1))
        return rest[layer - 1]

    order = sum(tok[0, 0] for _, _, _, tok in started.values())
    loss8, d_x, d_nf, lg, grad = _device_step(x[0], loss_target[0], mod, lb, weights_of, sp, chip, ci, order)

    dmod = jnp.stack([jnp.concatenate([lg[l][n] for n in ("sh1", "sc1", "g1", "sh2", "sc2", "g2")], axis=1)[0] for l in range(DEPTH)])
    d_lb = jnp.stack([lg[l]["lb"][0] for l in range(DEPTH)])
    contrib = {
        "norm_mix": jnp.stack([lg[l]["norm_mix"][0] for l in range(DEPTH)]),
        "norm_ffn": jnp.stack([lg[l]["norm_ffn"][0] for l in range(DEPTH)]),
        "b_merge": jnp.stack([lg[l]["b_merge"][0] for l in range(DEPTH)]),
        "hgrn_lb_logits": _lb_bwd(hgrn_lb_logits, d_lb),
        "hgrn_norm": jnp.stack([lg[l]["hgrn_norm"][0] for l in range(DEPTH)]),
        "ssm_conv_w": jnp.stack([lg[l]["ssm_conv_w"] for l in range(DEPTH)]),
        "ssm_conv_b": jnp.stack([lg[l]["ssm_conv_b"][0] for l in range(DEPTH)]),
        "ssm_dt_bias": jnp.stack([lg[l]["ssm_dt_bias"][0, DT_OFF:DT_OFF + 12] for l in range(DEPTH)]),
        "ssm_a_log": jnp.stack([lg[l]["ssm_a_log"][0, DT_OFF:DT_OFF + 12] for l in range(DEPTH)]),
        "ssm_d": jnp.stack([lg[l]["ssm_d"][0, DT_OFF:DT_OFF + 12] for l in range(DEPTH)]),
        "ssm_norm": jnp.stack([lg[l]["ssm_norm"][0] for l in range(DEPTH)]),
        "gdn_conv_w": jnp.stack([lg[l]["gdn_conv_w"] for l in range(DEPTH)]),
        "gdn_dt_bias": jnp.stack([lg[l]["gdn_dt_bias"][0, GA_OFF:GA_OFF + 6] for l in range(DEPTH)]),
        "gdn_a_log": jnp.stack([lg[l]["gdn_a_log"][0, GA_OFF:GA_OFF + 6] for l in range(DEPTH)]),
        "gdn_norm": jnp.stack([lg[l]["gdn_norm"][0] for l in range(DEPTH)]),
        "norm_final": d_nf[0],
    }
    flat = jnp.concatenate([dmod.reshape(-1)] + [contrib[n].reshape(-1) for n in SMALL_REPL] + [loss8[0, 0:1]])
    small_all = _all_gather_small("gather_small_grads", _pad_rows(flat, SMALL_ROWS, D_MODEL))
    total = _sum_leading("small_grad_sum", small_all, F32).reshape(-1)
    n_mod = dmod.size
    grad["b_ada"] = total[:n_mod].reshape(b_ada.shape)
    off = n_mod
    full_small = {}
    for n in SMALL_REPL:
        full_small[n] = total[off:off + contrib[n].size].reshape(contrib[n].shape)
        off += contrib[n].size
    loss = total[off]
    for n in SMALL_REPL:
        if n in ("ssm_conv_w", "gdn_conv_w"):
            cols = w[n].shape[2]
            grad[n] = lax.dynamic_slice_in_dim(full_small[n], chip * cols, cols, axis=2)
        else:
            grad[n] = full_small[n]
    dmod_cols = lax.dynamic_slice_in_dim(small_all[:, :n_mod // D_MODEL, :].reshape(8, DEPTH, -1), chip * ada_cols, ada_cols, axis=2)
    grad["w_ada"] = _ada_bwd(c_all, dmod_cols.transpose(1, 0, 2))

    delta, new_m, new_v = {}, {}, {}
    big_names = ("w_ada",) + tuple(n for n, _, _, _ in BIG)
    for n in big_names:
        delta[n], new_m[n], new_v[n] = _adamw("adamw_" + n, w[n], grad[n], m[n], v[n])
    small_names = [n for n in WEIGHTS if n not in big_names]
    packs = [_pad_rows(jnp.concatenate([d[n].reshape(-1) for n in small_names]), 584, LANES) for d in (w, grad, m, v)]
    outs = _ew("adamw_small", _adamw_fn, packs, [F32, F32, F32])
    off = 0
    for n in small_names:
        for dst, o in zip((delta, new_m, new_v), outs):
            dst[n] = o.reshape(-1)[off:off + w[n].size].reshape(w[n].shape)
        off += w[n].size
    return (loss, d_x[None], *[grad[n] for n in WEIGHTS], *[delta[n] for n in WEIGHTS], *[new_m[n] for n in WEIGHTS],
            *[new_v[n] for n in WEIGHTS])
```

```python
import functools

import jax
import jax.numpy as jnp
from jax import lax
from jax.experimental import pallas as pl
from jax.experimental.pallas import tpu as pltpu

F32 = jnp.float32
BF = jnp.bfloat16
HI = lax.Precision.HIGHEST

D_MODEL = 1024
DEPTH = 4
CHUNK = 64
MIX_W = 768
HEAD = 128
N_HEAD6 = 6
SSM_P = 64
SSM_N = 128
CONV_CH = 1280
QKV_W = 2304
FFN_H = 2816
IN_WIDTH = 11288
NORM_EPS = 1e-6
F_MIN = 1e-30
HALO = 8
HEAD_GROUP = 6
HGRN_SUB = 8
SMALL_W = 512
LANES = 128
DT_OFF, GB_OFF, GA_OFF = 0, 12, 18

ADAM_LR, ADAM_B1, ADAM_B2, ADAM_EPS, ADAM_WD, ADAM_STEP = 0.001, 0.9, 0.999, 1e-08, 0.01, 10

VMEM_LIMIT = 56 * 1024 * 1024
TOKEN_TILE = 256


def _pc(body, **kw):
    return pl.pallas_call(body, **kw)


def _cparams(sem):
    return pltpu.CompilerParams(dimension_semantics=sem, vmem_limit_bytes=VMEM_LIMIT)


def _bdot(a, b):
    return jnp.dot(a.astype(BF), b.astype(BF), preferred_element_type=F32)


def _bdot_nt(a, b):
    return lax.dot_general(a.astype(BF), b.astype(BF), (((1,), (1,)), ((), ())), preferred_element_type=F32)


def _bdot_tn(a, b):
    return lax.dot_general(a.astype(BF), b.astype(BF), (((0,), (0,)), ((), ())), preferred_element_type=F32)


def _silu(x):
    return x * jax.nn.sigmoid(x)


def _tri_mask(n, strict=False):
    t = lax.broadcasted_iota(jnp.int32, (n, n), 0)
    s = lax.broadcasted_iota(jnp.int32, (n, n), 1)
    return (s < t) if strict else (s <= t)


def _masked_exp(diff, mask):
    return jnp.where(mask, jnp.exp(jnp.where(mask, diff, 0.0)), 0.0)


def _split_bf16(x, n):
    parts, rest = [], x
    for _ in range(n):
        p = rest.astype(BF)
        parts.append(p)
        rest = rest - p.astype(F32)
    return parts


def _tri_sum(x, reverse):
    n, w = x.shape
    t = lax.broadcasted_iota(jnp.int32, (n, n), 0)
    s = lax.broadcasted_iota(jnp.int32, (n, n), 1)
    tri = jnp.where((s >= t) if reverse else (s <= t), 1.0, 0.0).astype(BF)
    y = jnp.dot(tri, jnp.concatenate(_split_bf16(x, 3), axis=1), preferred_element_type=F32)
    return y[:, :w] + y[:, w:2 * w] + y[:, 2 * w:]


@jax.custom_vjp
def _cumsum_rows(x):
    return _tri_sum(x, False)


_cumsum_rows.defvjp(lambda x: (_tri_sum(x, False), None), lambda _, g: (_tri_sum(g, True),))


def _dot_split(a, b, transpose_a=False):
    dims = (((0,), (0,)) if transpose_a else ((1,), (0,)), ((), ()))
    a_hi, a_lo = _split_bf16(a, 2)
    b_hi, b_lo = _split_bf16(b, 2)
    w = b.shape[1]
    y = lax.dot_general(a_hi, jnp.concatenate([b_hi, b_lo], axis=1), dims, preferred_element_type=F32)
    return y[:, :w] + y[:, w:] + lax.dot_general(a_lo, b_hi, dims, preferred_element_type=F32)


def _rms(x, w):
    return x * lax.rsqrt(jnp.mean(x * x, axis=-1, keepdims=True) + NORM_EPS) * w


def _causal_conv(halo, x, w):
    ext = jnp.concatenate([halo, x], axis=0)
    n = x.shape[0]
    acc = w[0:1, :] * ext[HALO - 3:HALO - 3 + n, :]
    for i in range(1, 4):
        acc = acc + w[i:i + 1, :] * ext[HALO - 3 + i:HALO - 3 + i + n, :]
    return acc


def _unit_lower_inverses(mats):
    n = mats[0].shape[0]
    t = lax.broadcasted_iota(jnp.int32, (n, LANES), 0)
    s_ = lax.broadcasted_iota(jnp.int32, (n, LANES), 1)
    xs = [jnp.where(t == s_, 1.0, 0.0).astype(F32) for _ in mats]
    for s in range(n - 1):
        r0 = 8 * ((s + 1) // 8)
        for i, a in enumerate(mats):
            x = xs[i]
            low = x[r0:] - a[r0:, s:s + 1] * x[s:s + 1, :]
            xs[i] = low if r0 == 0 else jnp.concatenate([x[:r0], low], axis=0)
    return xs


@jax.custom_vjp
def _solves_with_inverses(invs, mats, rhss):
    return [_dot_split(inv, r) for inv, r in zip(invs, rhss)]


def _swi_fwd(invs, mats, rhss):
    xs = [_dot_split(inv, r) for inv, r in zip(invs, rhss)]
    return xs, (invs, xs)


def _swi_bwd(res, gs):
    invs, xs = res
    ys = [_dot_split(inv, g, transpose_a=True) for inv, g in zip(invs, gs)]
    das = [jnp.where(_tri_mask(CHUNK, strict=True), -_bdot_nt(y, x), 0.0) for y, x in zip(ys, xs)]
    return [jnp.zeros_like(inv) for inv in invs], das, ys


_solves_with_inverses.defvjp(_swi_fwd, _swi_bwd)


def _hgrn_chunk(tiles, halos, state, consts):
    q_raw, f_raw, v_all, g_raw = tiles
    lb, norm_w = consts
    q_all = _silu(q_raw)
    f = lb + (1.0 - lb) * jax.nn.sigmoid(f_raw)
    logf = jnp.log(jnp.maximum(f, F_MIN))
    k_all = (1.0 - lb) * jax.nn.sigmoid(-f_raw)
    b_all = _cumsum_rows(logf)
    sub = HGRN_SUB
    row = lax.broadcasted_iota(jnp.int32, (sub, 1), 0)
    src_row = lax.broadcasted_iota(jnp.int32, (CHUNK, 1), 0)
    src_lane = lax.broadcasted_iota(jnp.int32, (1, CHUNK), 1)
    heads = range(N_HEAD6)
    n_sub = CHUNK // sub
    cols = [slice(h * HEAD, (h + 1) * HEAD) for h in heads]
    qs, ks, vs, bs = ([a[:, sl] for sl in cols] for a in (q_all, k_all, v_all, b_all))
    o_inter = [_bdot_nt(qs[h] * jnp.exp(bs[h]), state[h]) for h in heads]
    blocks = [[None] * n_sub for _ in heads]
    for i in range(n_sub):
        r0 = i * sub
        for h in heads:
            if i > 0:
                ref = bs[h][r0 - 1:r0, :]
                blocks[h][i] = _bdot_nt(qs[h][r0:r0 + sub] * jnp.exp(bs[h][r0:r0 + sub] - ref),
                                        ks[h] * _masked_exp(ref - bs[h], src_row < r0))
            else:
                blocks[h][i] = jnp.zeros((sub, CHUNK), F32)
    for h in heads:
        for i in range(n_sub):
            r0 = i * sub
            qi, ki, bi = qs[h][r0:r0 + sub], ks[h][r0:r0 + sub], bs[h][r0:r0 + sub]
            for s in range(sub):
                e = _masked_exp(bi - bi[s:s + 1, :], row >= s)
                col = jnp.sum(qi * ki[s:s + 1, :] * e, axis=1, keepdims=True)
                blocks[h][i] = jnp.where(src_lane == r0 + s, col, blocks[h][i])
    os_ = [_bdot(jnp.concatenate(blocks[h], axis=0), vs[h]) + o_inter[h] for h in heads]
    ends = [bs[h][CHUNK - 1:CHUNK, :] for h in heads]
    new_states = [state[h] * jnp.exp(ends[h]) + _bdot_tn(vs[h], ks[h] * jnp.exp(ends[h] - bs[h])) for h in heads]
    outs = [_rms(os_[h], norm_w) * _silu(g_raw[:, cols[h]]) for h in heads]
    return (jnp.concatenate(outs, axis=1),), jnp.stack(new_states)


def _ssd_chunk(tiles, halos, state, consts):
    z, xbc_raw, small = tiles
    (halo,) = halos
    conv_w, conv_b, dt_bias, a_log, d_skip, norm_w = consts
    xbc = _silu(_causal_conv(halo, xbc_raw, conv_w) + conv_b)
    xs, bm, cm = xbc[:, :MIX_W], xbc[:, MIX_W:MIX_W + 2 * SSM_N], xbc[:, MIX_W + 2 * SSM_N:]
    dt = jax.nn.softplus(small + dt_bias)
    cum = _cumsum_rows(-jnp.exp(a_log) * dt)
    cum_t2 = jnp.concatenate([cum, cum], axis=0).T
    lane = lax.broadcasted_iota(jnp.int32, (1, LANES), 1)
    first = lane < SSM_P
    hm0 = jnp.where(first, 1.0, 0.0).astype(F32)
    hm1 = 1.0 - hm0
    src = jnp.where(first, lane, lane - SSM_P)
    tri2 = src <= lax.broadcasted_iota(jnp.int32, (CHUNK, 1), 0)
    pick = lambda a, b: jnp.where(first, a, b)
    bgs = [bm[:, g * SSM_N:(g + 1) * SSM_N] for g in range(2)]
    cgs = [cm[:, g * SSM_N:(g + 1) * SSM_N] for g in range(2)]
    gmats = [_bdot_nt(cgs[g], jnp.concatenate([bgs[g], bgs[g]], axis=0)) for g in range(2)]
    pairs = range(6)
    xps = [xs[:, p * LANES:(p + 1) * LANES] for p in pairs]
    c0s = [cum[:, 2 * p:2 * p + 1] for p in pairs]
    c1s = [cum[:, 2 * p + 1:2 * p + 2] for p in pairs]
    e0s = [cum[CHUNK - 1:CHUNK, 2 * p:2 * p + 1] for p in pairs]
    e1s = [cum[CHUNK - 1:CHUNK, 2 * p + 1:2 * p + 2] for p in pairs]
    segs = [_masked_exp(pick(c0s[p], c1s[p]) - pick(cum_t2[2 * p:2 * p + 1, :], cum_t2[2 * p + 1:2 * p + 2, :]), tri2)
            for p in pairs]
    vms = []
    for p in pairs:
        v = xps[p] * pick(dt[:, 2 * p:2 * p + 1], dt[:, 2 * p + 1:2 * p + 2])
        vms.append(jnp.concatenate([v * hm0, v * hm1], axis=0))
    y_intra = [_bdot(gmats[p // 3] * segs[p], vms[p]) for p in pairs]
    y_inter = [_bdot(jnp.concatenate([cgs[p // 3] * jnp.exp(c0s[p]), cgs[p // 3] * jnp.exp(c1s[p])], axis=1),
                     jnp.concatenate([state[p] * hm0, state[p] * hm1], axis=0)) for p in pairs]
    new_states = [_bdot_tn(jnp.concatenate([bgs[p // 3] * jnp.exp(e0s[p] - c0s[p]),
                                            bgs[p // 3] * jnp.exp(e1s[p] - c1s[p])], axis=0), vms[p])
                  + state[p] * pick(jnp.exp(e0s[p]), jnp.exp(e1s[p])) for p in pairs]
    ys = [y_intra[p] + y_inter[p] + pick(d_skip[:, 2 * p:2 * p + 1], d_skip[:, 2 * p + 1:2 * p + 2]) * xps[p] for p in pairs]
    y = jnp.concatenate(ys, axis=1) * _silu(z)
    gw = MIX_W // 2
    y = jnp.concatenate([_rms(y[:, g * gw:(g + 1) * gw], norm_w[:, g * gw:(g + 1) * gw]) for g in range(2)], axis=1)
    return (y,), jnp.stack(new_states)


def _gdn_chunk(tiles, halos, state, consts):
    qkv_raw, z, small = tiles[:3]
    given = tiles[3] if len(tiles) > 3 else None
    (halo,) = halos
    conv_w, dt_bias, a_log, norm_w = consts
    qkv = _silu(_causal_conv(halo, qkv_raw, conv_w))
    beta_all = jax.nn.sigmoid(small)
    cum = _cumsum_rows(-jnp.exp(a_log) * jax.nn.softplus(small + dt_bias))
    cum_t = cum.T
    tri, tri_strict = _tri_mask(CHUNK), _tri_mask(CHUNK, strict=True)

    def group(hs):
        n = range(len(hs))
        qs, ks, betas, cs, ces, decays, rhss = [], [], [], [], [], [], []
        for h in hs:
            q = qkv[:, h * HEAD:(h + 1) * HEAD]
            k = qkv[:, MIX_W + h * HEAD:MIX_W + (h + 1) * HEAD]
            v = qkv[:, 2 * MIX_W + h * HEAD:2 * MIX_W + (h + 1) * HEAD]
            q = q * lax.rsqrt(jnp.sum(q * q, axis=-1, keepdims=True) + NORM_EPS) * (HEAD ** -0.5)
            k = k * lax.rsqrt(jnp.sum(k * k, axis=-1, keepdims=True) + NORM_EPS)
            beta = beta_all[:, GB_OFF + h:GB_OFF + h + 1]
            c, c_t = cum[:, GA_OFF + h:GA_OFF + h + 1], cum_t[GA_OFF + h:GA_OFF + h + 1, :]
            qs.append(q), ks.append(k), betas.append(beta), cs.append(c)
            ces.append(cum[CHUNK - 1:CHUNK, GA_OFF + h:GA_OFF + h + 1])
            decays.append(_masked_exp(c - c_t, tri))
            rhss.append(jnp.concatenate([v * beta, k * (beta * jnp.exp(c))], axis=1))
        sts = [state[h] for h in hs]
        qk_kks = [_bdot_nt(jnp.concatenate([qs[i], ks[i]], axis=0), ks[i]) for i in n]
        mats = [jnp.where(tri_strict, betas[i] * qk_kks[i][CHUNK:] * decays[i], 0.0) for i in n]
        wide = _unit_lower_inverses(mats) if given is None else [given[:, h * LANES:(h + 1) * LANES] for h in hs]
        sols = _solves_with_inverses([inv[:, :CHUNK] for inv in wide], mats, rhss)
        on_states = [_bdot(jnp.concatenate([sols[i][:, HEAD:], qs[i] * jnp.exp(cs[i])], axis=0), sts[i]) for i in n]
        us = [sols[i][:, :HEAD] - on_states[i][:CHUNK] for i in n]
        os_ = [on_states[i][CHUNK:] + _bdot(qk_kks[i][:CHUNK] * decays[i], us[i]) for i in n]
        new = [jnp.exp(ces[i]) * sts[i] + _bdot_tn(ks[i] * jnp.exp(ces[i] - cs[i]), us[i]) for i in n]
        outs = [_rms(os_[i], norm_w) * _silu(z[:, h * HEAD:(h + 1) * HEAD]) for i, h in enumerate(hs)]
        return outs, new, wide

    outs, new_states, inverses = [], [], []
    for h0 in range(0, N_HEAD6, HEAD_GROUP):
        o, s, w = group(list(range(h0, h0 + HEAD_GROUP)))
        outs += o
        new_states += s
        inverses += w
    y = jnp.concatenate(outs, axis=1)
    return ((y,) if given is not None else (y, jnp.concatenate(inverses, axis=1))), jnp.stack(new_states)


def _scan_fwd(name, fn, tiled, halo_idx, consts, out_width, state_shape, aux_width=None):
    seq = tiled[0][0].shape[0]
    nc = seq // CHUNK
    n_t, n_h, n_c = len(tiled), len(halo_idx), len(consts)

    def body(*refs):
        t_refs, h_refs, c_refs = refs[:n_t], refs[n_t:n_t + n_h], refs[n_t + n_h:n_t + n_h + n_c]
        y_ref, save_ref = refs[n_t + n_h + n_c:n_t + n_h + n_c + 2]
        st_ref = refs[-1]
        i = pl.program_id(0)

        @pl.when(i == 0)
        def _():
            st_ref[...] = jnp.zeros_like(st_ref)

        flag = jnp.where(i > 0, 1.0, 0.0).astype(F32)
        st = st_ref[...]
        outs, new = fn([r[...] for r in t_refs], [r[...] * flag for r in h_refs], st, [r[...] for r in c_refs])
        save_ref[0] = st
        y_ref[...] = outs[0].astype(y_ref.dtype)
        if aux_width is not None:
            refs[-2][...] = outs[1]
        st_ref[...] = new

    in_specs = [pl.BlockSpec((CHUNK, w), functools.partial(lambda i, cb: (i, cb), cb=cb)) for _, w, cb in tiled]
    in_specs += [pl.BlockSpec((HALO, tiled[j][1]),
                              functools.partial(lambda i, cb: (jnp.maximum(i * (CHUNK // HALO) - 1, 0), cb), cb=tiled[j][2]))
                 for j in halo_idx]
    in_specs += [pl.BlockSpec(c.shape, functools.partial(lambda i, nd: (0,) * nd, nd=c.ndim)) for c in consts]
    zeros = (0,) * len(state_shape)
    out_specs = [pl.BlockSpec((CHUNK, out_width), lambda i: (i, 0)), pl.BlockSpec((1,) + state_shape, lambda i: (i,) + zeros)]
    out_shape = [jax.ShapeDtypeStruct((seq, out_width), BF), jax.ShapeDtypeStruct((nc,) + state_shape, F32)]
    if aux_width is not None:
        out_specs.append(pl.BlockSpec((CHUNK, aux_width), lambda i: (i, 0)))
        out_shape.append(jax.ShapeDtypeStruct((seq, aux_width), F32))
    return _pc(
        body, name=name, grid=(nc,), in_specs=in_specs, out_specs=tuple(out_specs), out_shape=tuple(out_shape),
        scratch_shapes=[pltpu.VMEM(state_shape, F32)],
        compiler_params=_cparams(("arbitrary",)),
    )(*[t[0] for t in tiled], *[tiled[j][0] for j in halo_idx], *consts)


def _scan_bwd(name, fn, tiled, halo_idx, consts, saved, dy, dtile_dtypes, extra=None):
    seq = tiled[0][0].shape[0]
    nc = seq // CHUNK
    n_t, n_h, n_c = len(tiled), len(halo_idx), len(consts)
    state_shape = saved.shape[1:]
    n_x = 0 if extra is None else 1
    keep = [j for j, dtd in enumerate(dtile_dtypes) if dtd is not None]
    n_k = len(keep)

    def body(*refs):
        t_refs, h_refs, c_refs = refs[:n_t], refs[n_t:n_t + n_h], refs[n_t + n_h:n_t + n_h + n_c]
        pos = n_t + n_h + n_c
        save_ref, dy_ref = refs[pos], refs[pos + 1]
        x_refs = refs[pos + 2:pos + 2 + n_x]
        pos += 2 + n_x
        dt_refs, dc_refs = refs[pos:pos + n_k], refs[pos + n_k:pos + n_k + n_c]
        dst_ref = refs[pos + n_k + n_c]
        carry_refs = refs[pos + n_k + n_c + 1:]
        i = pl.program_id(0)

        @pl.when(i == 0)
        def _():
            dst_ref[...] = jnp.zeros_like(dst_ref)
            for r in carry_refs:
                r[...] = jnp.zeros_like(r)
            for r in dc_refs:
                r[...] = jnp.zeros_like(r)

        flag = jnp.where(i < nc - 1, 1.0, 0.0).astype(F32)
        tiles = [r[...] for r in t_refs]
        halos = [r[...] * flag for r in h_refs]
        cvals = [r[...] for r in c_refs]
        _, vjp = jax.vjp(fn, tiles, halos, save_ref[0], cvals)
        d_tiles, d_halos, d_state, d_consts = vjp(((dy_ref[...].astype(F32),), dst_ref[...]))
        dst_ref[...] = d_state
        for r, g in zip(dc_refs, d_consts):
            r[...] += g
        for r, j in zip(dt_refs, keep):
            g = d_tiles[j]
            if extra is not None and extra[0] == j:
                g = g + x_refs[0][...].astype(F32)
            r[...] = g.astype(r.dtype)
            if j in halo_idx:
                cr = carry_refs[halo_idx.index(j)]
                r[CHUNK - HALO:CHUNK, :] = (g[CHUNK - HALO:CHUNK, :] + cr[...]).astype(r.dtype)
                cr[...] = d_halos[halo_idx.index(j)] * flag

    rev = lambda i: nc - 1 - i
    in_specs = [pl.BlockSpec((CHUNK, w), functools.partial(lambda i, cb: (rev(i), cb), cb=cb)) for _, w, cb in tiled]
    in_specs += [pl.BlockSpec((HALO, tiled[j][1]),
                              functools.partial(lambda i, cb: (jnp.maximum(rev(i) * (CHUNK // HALO) - 1, 0), cb), cb=tiled[j][2]))
                 for j in halo_idx]
    in_specs += [pl.BlockSpec(c.shape, functools.partial(lambda i, nd: (0,) * nd, nd=c.ndim)) for c in consts]
    zeros = (0,) * len(state_shape)
    in_specs += [pl.BlockSpec((1,) + state_shape, lambda i: (rev(i),) + zeros),
                 pl.BlockSpec((CHUNK, dy.shape[1]), lambda i: (rev(i), 0))]
    args = [t[0] for t in tiled] + [tiled[j][0] for j in halo_idx] + list(consts) + [saved, dy]
    if extra is not None:
        in_specs.append(pl.BlockSpec((CHUNK, extra[1].shape[1]), lambda i: (rev(i), 0)))
        args.append(extra[1])
    out_specs = [pl.BlockSpec((CHUNK, tiled[j][1]), lambda i: (rev(i), 0)) for j in keep]
    out_specs += [pl.BlockSpec(c.shape, functools.partial(lambda i, nd: (0,) * nd, nd=c.ndim)) for c in consts]
    out_shape = [jax.ShapeDtypeStruct((seq, tiled[j][1]), dtile_dtypes[j]) for j in keep]
    out_shape += [jax.ShapeDtypeStruct(c.shape, F32) for c in consts]
    scratch = [pltpu.VMEM(state_shape, F32)] + [pltpu.VMEM((HALO, tiled[j][1]), F32) for j in halo_idx]
    return _pc(body, name=name, grid=(nc,), in_specs=in_specs, out_specs=tuple(out_specs), out_shape=tuple(out_shape),
               scratch_shapes=scratch, compiler_params=_cparams(("arbitrary",)))(*args)


def _tile_fwd(name, fn, tiled, consts, outs, tm=TOKEN_TILE):
    seq = tiled[0][0].shape[0]
    n_t, n_c = len(tiled), len(consts)

    def body(*refs):
        res = fn(*[r[...] for r in refs[:n_t + n_c]])
        for r, y in zip(refs[n_t + n_c:], res):
            r[...] = y.astype(r.dtype)

    in_specs = [pl.BlockSpec((tm, w), functools.partial(lambda i, cb: (i, cb), cb=cb)) for _, w, cb in tiled]
    in_specs += [pl.BlockSpec(c.shape, functools.partial(lambda i, nd: (0,) * nd, nd=c.ndim)) for c in consts]
    return _pc(body, name=name, grid=(seq // tm,), in_specs=in_specs,
               out_specs=tuple(pl.BlockSpec((tm, w), lambda i: (i, 0)) for w, _ in outs),
               out_shape=tuple(jax.ShapeDtypeStruct((seq, w), dtp) for w, dtp in outs),
               compiler_params=_cparams(("arbitrary",)))(*[t[0] for t in tiled], *consts)


def _tile_bwd(name, fn, tiled, consts, douts, dtile_dtypes, add_to=None, tm=TOKEN_TILE):
    seq = tiled[0][0].shape[0]
    n_t, n_c, n_o = len(tiled), len(consts), len(douts)
    n_x = 0 if add_to is None else 1
    keep = [j for j, dtp in enumerate(dtile_dtypes) if dtp is not None]

    def body(*refs):
        vals = [r[...].astype(F32) for r in refs[:n_t + n_c]]
        pos = n_t + n_c
        g_refs, x_refs = refs[pos:pos + n_o], refs[pos + n_o:pos + n_o + n_x]
        pos += n_o + n_x
        dt_refs, dc_refs = refs[pos:pos + len(keep)], refs[pos + len(keep):]
        i = pl.program_id(0)

        @pl.when(i == 0)
        def _():
            for r in dc_refs:
                r[...] = jnp.zeros_like(r)

        _, vjp = jax.vjp(fn, *vals)
        cts = vjp(tuple(g[...].astype(F32) for g in g_refs))
        for r, j in zip(dt_refs, keep):
            g = cts[j]
            if add_to is not None and add_to[0] == j:
                g = g + x_refs[0][...].astype(F32)
            r[...] = g.astype(r.dtype)
        for r, g in zip(dc_refs, cts[n_t:]):
            r[...] += g

    in_specs = [pl.BlockSpec((tm, w), functools.partial(lambda i, cb: (i, cb), cb=cb)) for _, w, cb in tiled]
    in_specs += [pl.BlockSpec(c.shape, functools.partial(lambda i, nd: (0,) * nd, nd=c.ndim)) for c in consts]
    in_specs += [pl.BlockSpec((tm, g.shape[1]), lambda i: (i, 0)) for g in douts]
    args = [t[0] for t in tiled] + list(consts) + list(douts)
    if add_to is not None:
        in_specs.append(pl.BlockSpec((tm, add_to[1].shape[1]), lambda i: (i, 0)))
        args.append(add_to[1])
    out_specs = [pl.BlockSpec((tm, tiled[j][1]), lambda i: (i, 0)) for j in keep]
    out_specs += [pl.BlockSpec(c.shape, functools.partial(lambda i, nd: (0,) * nd, nd=c.ndim)) for c in consts]
    out_shape = [jax.ShapeDtypeStruct((seq, tiled[j][1]), dtile_dtypes[j]) for j in keep]
    out_shape += [jax.ShapeDtypeStruct(c.shape, F32) for c in consts]
    return _pc(body, name=name, grid=(seq // tm,), in_specs=in_specs, out_specs=tuple(out_specs),
               out_shape=tuple(out_shape), compiler_params=_cparams(("arbitrary",)))(*args)


def _lnmod_fn(x, nw, sc, sh):
    return (_rms(x, nw) * (1.0 + sc) + sh,)


def _gated_fn(o, g):
    return ((1.0 + g) * o,)


def _resid_fn(x, o, g):
    return (x + (1.0 + g) * o,)


def _swiglu_fn(gu):
    return (_silu(gu[:, :FFN_H]) * gu[:, FFN_H:],)


def _merge_fn(yh, ys, yg, logits, wb, b_merge):
    gates = jax.nn.sigmoid(logits + b_merge)
    acc = None
    for n, y in enumerate((yh, ys, yg)):
        t = gates[:, n * D_MODEL:(n + 1) * D_MODEL] * _bdot(y, wb[n])
        acc = t if acc is None else acc + t
    return (acc,)


MM_VMEM_BUDGET = 40 * 1024 * 1024
MM_TILE_CAP = 1024
MM_K_CAP = 4096


def _divisor(n, cap, unit=LANES):
    best = None
    for d in range(unit, min(n, cap) + 1, unit):
        if n % d == 0:
            best = d
    return n if best is None else best


def _mm_tiles(m, n, k, out_bytes):
    tk = k if k <= MM_K_CAP else _divisor(k, 3072)
    tm, tn = _divisor(m, MM_TILE_CAP), _divisor(n, MM_TILE_CAP + MM_TILE_CAP // 2)

    def need(tm_, tn_):
        acc = tm_ * tn_ * 4 if tk < k else 0
        return 2 * 2 * tk * (tm_ + tn_) + acc + 2 * tm_ * tn_ * out_bytes

    while need(tm, tn) > MM_VMEM_BUDGET:
        if tn >= tm and _divisor(n, tn - LANES) < tn:
            tn = _divisor(n, tn - LANES)
        elif _divisor(m, tm - LANES) < tm:
            tm = _divisor(m, tm - LANES)
        else:
            break
    return tm, tn, tk


def _mm(a, b, mode, out_dtype, name):
    if mode == "nn":
        (m, k), n = a.shape, b.shape[1]
    elif mode == "nt":
        (m, k), n = a.shape, b.shape[0]
    else:
        (k, m), n = a.shape, b.shape[1]
    tm, tn, tk = _mm_tiles(m, n, k, jnp.dtype(out_dtype).itemsize)
    nk = k // tk
    dims = {"nn": ((1,), (0,)), "nt": ((1,), (1,)), "tn": ((0,), (0,))}[mode]

    def body_one(a_ref, b_ref, o_ref):
        o_ref[...] = lax.dot_general(a_ref[...], b_ref[...], (dims, ((), ())), preferred_element_type=F32).astype(o_ref.dtype)

    def body_acc(a_ref, b_ref, o_ref, acc_ref):
        kk = pl.program_id(2)

        @pl.when(kk == 0)
        def _():
            acc_ref[...] = jnp.zeros_like(acc_ref)

        acc_ref[...] += lax.dot_general(a_ref[...], b_ref[...], (dims, ((), ())), preferred_element_type=F32)

        @pl.when(kk == nk - 1)
        def _():
            o_ref[...] = acc_ref[...].astype(o_ref.dtype)

    a_spec = pl.BlockSpec((tk, tm), lambda i, j, kk: (kk, i)) if mode == "tn" else pl.BlockSpec((tm, tk), lambda i, j, kk: (i, kk))
    b_spec = pl.BlockSpec((tn, tk), lambda i, j, kk: (j, kk)) if mode == "nt" else pl.BlockSpec((tk, tn), lambda i, j, kk: (kk, j))
    return _pc(body_one if nk == 1 else body_acc, name=name, grid=(m // tm, n // tn, nk), in_specs=[a_spec, b_spec],
               out_specs=pl.BlockSpec((tm, tn), lambda i, j, kk: (i, j)),
               out_shape=jax.ShapeDtypeStruct((m, n), out_dtype),
               scratch_shapes=[] if nk == 1 else [pltpu.VMEM((tm, tn), F32)],
               compiler_params=_cparams(("parallel", "parallel", "arbitrary")))(a.astype(BF), b.astype(BF))


def _final_loss(x, tgt, norm_final, tm=TOKEN_TILE):
    seq = x.shape[0]

    def fn(xv, nf, tv):
        err = jnp.square(_rms(xv, nf) - tv)
        return 0.5 * jnp.sum(jnp.mean(err, axis=-1))

    def body(x_ref, t_ref, nf_ref, loss_ref, dx_ref, dnf_ref):
        i = pl.program_id(0)

        @pl.when(i == 0)
        def _():
            loss_ref[...] = jnp.zeros_like(loss_ref)
            dnf_ref[...] = jnp.zeros_like(dnf_ref)

        val, vjp = jax.vjp(functools.partial(fn, tv=t_ref[...]), x_ref[...], nf_ref[...])
        dx, dnf = vjp(jnp.ones((), F32))
        dx_ref[...] = dx
        dnf_ref[...] += dnf
        loss_ref[...] += jnp.broadcast_to(val, loss_ref.shape)

    return _pc(body, name="final_loss", grid=(seq // tm,),
               in_specs=[pl.BlockSpec((tm, D_MODEL), lambda i: (i, 0)), pl.BlockSpec((tm, D_MODEL), lambda i: (i, 0)),
                         pl.BlockSpec((1, D_MODEL), lambda i: (0, 0))],
               out_specs=(pl.BlockSpec((8, LANES), lambda i: (0, 0)), pl.BlockSpec((tm, D_MODEL), lambda i: (i, 0)),
                          pl.BlockSpec((1, D_MODEL), lambda i: (0, 0))),
               out_shape=(jax.ShapeDtypeStruct((8, LANES), F32), jax.ShapeDtypeStruct((seq, D_MODEL), F32),
                          jax.ShapeDtypeStruct((1, D_MODEL), F32)),
               compiler_params=_cparams(("arbitrary",)))(x, tgt, norm_final)


def _ada_fwd(c_all, w_ada, b_ada_cols):
    n_l, _, cols = w_ada.shape

    def body(c_ref, w_ref, b_ref, o_ref):
        o_ref[0] = jnp.dot(_silu(c_ref[...]), w_ref[0], preferred_element_type=F32, precision=HI) + b_ref[0]

    return _pc(body, name="ada_fwd", grid=(n_l,),
               in_specs=[pl.BlockSpec((8, D_MODEL), lambda l: (0, 0)), pl.BlockSpec((1, D_MODEL, cols), lambda l: (l, 0, 0)),
                         pl.BlockSpec((1, 1, cols), lambda l: (l, 0, 0))],
               out_specs=pl.BlockSpec((1, 8, cols), lambda l: (l, 0, 0)),
               out_shape=jax.ShapeDtypeStruct((n_l, 8, cols), F32),
               compiler_params=_cparams(("arbitrary",)))(c_all, w_ada, b_ada_cols.reshape(n_l, 1, cols))


def _ada_bwd(c_all, dmod_cols):
    n_l, _, cols = dmod_cols.shape

    def body(c_ref, g_ref, o_ref):
        o_ref[0] = lax.dot_general(_silu(c_ref[...]), g_ref[0], (((0,), (0,)), ((), ())), preferred_element_type=F32,
                                   precision=HI)

    return _pc(body, name="ada_bwd", grid=(n_l,),
               in_specs=[pl.BlockSpec((8, D_MODEL), lambda l: (0, 0)), pl.BlockSpec((1, 8, cols), lambda l: (l, 0, 0))],
               out_specs=pl.BlockSpec((1, D_MODEL, cols), lambda l: (l, 0, 0)),
               out_shape=jax.ShapeDtypeStruct((n_l, D_MODEL, cols), F32),
               compiler_params=_cparams(("arbitrary",)))(c_all, dmod_cols)


def _lb_fn(logits):
    e = jnp.exp(logits - jnp.max(logits, axis=0, keepdims=True))
    p = e / jnp.sum(e, axis=0, keepdims=True)
    r = lax.broadcasted_iota(jnp.int32, (DEPTH, 1), 0)
    lb = jnp.zeros_like(p)
    for j in range(1, DEPTH):
        lb = lb + jnp.where(r >= j, p[j:j + 1, :], 0.0)
    return lb


def _lb_fwd(logits):
    def body(l_ref, o_ref):
        o_ref[...] = _lb_fn(l_ref[...])

    return _pc(body, name="lb_fwd", out_shape=jax.ShapeDtypeStruct(logits.shape, F32))(logits)


def _lb_bwd(logits, dlb):
    def body(l_ref, g_ref, o_ref):
        _, vjp = jax.vjp(_lb_fn, l_ref[...])
        o_ref[...] = vjp(g_ref[...])[0]

    return _pc(body, name="lb_bwd", out_shape=jax.ShapeDtypeStruct(logits.shape, F32))(logits, dlb)


def _rows_for(n_rows, n_cols):
    r = 8
    while r * 2 <= n_rows and n_rows % (r * 2) == 0 and r * 2 * n_cols <= 256 * 1024:
        r *= 2
    return r if n_rows % r == 0 else n_rows


def _ew(name, fn, ins, out_dtypes):
    n_rows, n_cols = ins[0].shape
    tr = _rows_for(n_rows, n_cols)
    n_in = len(ins)

    def body(*refs):
        res = fn(*[r[...] for r in refs[:n_in]])
        for r, y in zip(refs[n_in:], res):
            r[...] = y.astype(r.dtype)

    spec = pl.BlockSpec((tr, n_cols), lambda i: (i, 0))
    return _pc(body, name=name, grid=(n_rows // tr,), in_specs=[spec] * n_in, out_specs=tuple([spec] * len(out_dtypes)),
               out_shape=tuple(jax.ShapeDtypeStruct((n_rows, n_cols), d) for d in out_dtypes),
               compiler_params=_cparams(("arbitrary",)))(*ins)


def _adamw_fn(w, g, m, v):
    m = ADAM_B1 * m + (1.0 - ADAM_B1) * g
    v = ADAM_B2 * v + (1.0 - ADAM_B2) * jnp.square(g)
    m_hat = m / (1.0 - ADAM_B1 ** ADAM_STEP)
    v_hat = v / (1.0 - ADAM_B2 ** ADAM_STEP)
    return -ADAM_LR * (m_hat / (jnp.sqrt(v_hat) + ADAM_EPS) + ADAM_WD * w), m, v


def _adamw(name, w, g, m, v):
    shape = w.shape
    two = (-1, shape[-1])
    d, nm, nv = _ew(name, _adamw_fn, [a.reshape(two) for a in (w, g, m, v)], [F32, F32, F32])
    return d.reshape(shape), nm.reshape(shape), nv.reshape(shape)


def _sum_leading(name, a, out_dtype):
    n, n_rows, n_cols = a.shape
    tr = _rows_for(n_rows, n_cols)

    def body(a_ref, o_ref):
        acc = a_ref[0].astype(F32)
        for j in range(1, n):
            acc = acc + a_ref[j].astype(F32)
        o_ref[...] = acc.astype(o_ref.dtype)

    return _pc(body, name=name, grid=(n_rows // tr,), in_specs=[pl.BlockSpec((n, tr, n_cols), lambda i: (0, i, 0))],
               out_specs=pl.BlockSpec((tr, n_cols), lambda i: (i, 0)),
               out_shape=jax.ShapeDtypeStruct((n_rows, n_cols), out_dtype),
               compiler_params=_cparams(("arbitrary",)))(a)


MESH = pl.DeviceIdType.MESH
ANY = pl.BlockSpec(memory_space=pl.ANY)


def _place():
    return lax.axis_index("x"), lax.axis_index("y"), lax.axis_index("c")


def _all_gather_small(name, a):
    m_per, n = a.shape

    def body(x_ref, out_ref, send_sems, recv_sems, local_sem):
        x, y, c = _place()
        me, sibling = (x, y, c), (x, y, 1 - c)
        chips = [(1 - x, y), (x, 1 - y), (1 - x, 1 - y)]

        def rows(px, py, pc):
            return out_ref.at[pl.ds((4 * px + 2 * py + pc) * m_per, m_per), :]

        def copy(k, block, to, src=None):
            return pltpu.make_async_remote_copy(src_ref=rows(*block) if src is None else src, dst_ref=rows(*block),
                                                send_sem=send_sems.at[k], recv_sem=recv_sems.at[k], device_id=to,
                                                device_id_type=MESH)

        mine = pltpu.make_async_copy(x_ref, rows(*me), local_sem)
        mine.start()
        first = [copy(0, me, sibling, src=x_ref)]
        first += [copy(1 + j, me, (*chip, c), src=x_ref) for j, chip in enumerate(chips)]
        for cp in first:
            cp.start()
        passed = [copy(4 + j, (*chip, c), sibling) for j, chip in enumerate(chips)]
        for j, chip in enumerate(chips):
            copy(1 + j, (*chip, c), me).wait_recv()
            passed[j].start()
        copy(0, sibling, me).wait_recv()
        for j, chip in enumerate(chips):
            copy(4 + j, (*chip, 1 - c), me).wait_recv()
        for cp in first + passed:
            cp.wait_send()
        mine.wait()

    out = _pc(body, name=name, out_shape=jax.ShapeDtypeStruct((8 * m_per, n), a.dtype),
              in_specs=[pl.BlockSpec(memory_space=pltpu.VMEM)], out_specs=pl.BlockSpec(memory_space=pltpu.VMEM),
              scratch_shapes=[pltpu.SemaphoreType.DMA((7,)), pltpu.SemaphoreType.DMA((7,)), pltpu.SemaphoreType.DMA],
              compiler_params=pltpu.CompilerParams(vmem_limit_bytes=VMEM_LIMIT))(a)
    return out.reshape(8, m_per, n)


def _chip_gather(name, pack):
    n_l, n_r, n_c = pack.shape
    half = n_r // 2

    def body(p_ref, o_ref, send_sems, recv_sems):
        x, y, c = _place()
        sibling = (x, y, 1 - c)
        chips = [(1 - x, y), (x, 1 - y), (1 - x, 1 - y)]

        def slab(px, py, pc):
            return o_ref.at[2 * px + py, :, pl.ds(pc * half, half), :]

        def copy(k, src, dst, to):
            return pltpu.make_async_remote_copy(src_ref=src, dst_ref=dst, send_sem=send_sems.at[k], recv_sem=recv_sems.at[k],
                                                device_id=to, device_id_type=MESH)

        first = [copy(j, p_ref.at[:, pl.ds(c * half, half), :], slab(x, y, c), (*chip, c)) for j, chip in enumerate(chips)]
        for cp in first:
            cp.start()
        passed = [copy(3 + j, slab(*chip, c), slab(*chip, c), sibling) for j, chip in enumerate(chips)]
        for j, chip in enumerate(chips):
            copy(j, slab(*chip, c), slab(*chip, c), (*chip, c)).wait_recv()
            passed[j].start()
        for j, chip in enumerate(chips):
            copy(3 + j, slab(*chip, 1 - c), slab(*chip, 1 - c), sibling).wait_recv()
        for cp in first + passed:
            cp.wait_send()

    return _pc(body, name=name, out_shape=jax.ShapeDtypeStruct((4, n_l, n_r, n_c), pack.dtype), in_specs=[ANY], out_specs=ANY,
               scratch_shapes=[pltpu.SemaphoreType.DMA((6,)), pltpu.SemaphoreType.DMA((6,))])(pack)


def _pair_swap(name, give):
    def body(g_ref, o_ref, send_sem, recv_sem):
        x, y, c = _place()
        cp = pltpu.make_async_remote_copy(src_ref=g_ref, dst_ref=o_ref, send_sem=send_sem, recv_sem=recv_sem,
                                          device_id=(x, y, 1 - c), device_id_type=MESH)
        cp.start()
        cp.wait()

    return _pc(body, name=name, out_shape=jax.ShapeDtypeStruct(give.shape, give.dtype), in_specs=[ANY], out_specs=ANY,
               scratch_shapes=[pltpu.SemaphoreType.DMA, pltpu.SemaphoreType.DMA])(give)


def _chip_exchange(name, parts):
    def body(p_ref, o_ref, send_sems, recv_sems):
        x, y, c = _place()
        me = 2 * x + y
        chips = [(1 - x, y), (x, 1 - y), (1 - x, 1 - y)]

        def copy(k, src, dst, to):
            return pltpu.make_async_remote_copy(src_ref=src, dst_ref=dst, send_sem=send_sems.at[k], recv_sem=recv_sems.at[k],
                                                device_id=to, device_id_type=MESH)

        sends = [copy(j, p_ref.at[2 * px + py], o_ref.at[me], (px, py, c)) for j, (px, py) in enumerate(chips)]
        for cp in sends:
            cp.start()
        for j, (px, py) in enumerate(chips):
            copy(j, p_ref.at[2 * px + py], o_ref.at[2 * px + py], (px, py, c)).wait_recv()
        for cp in sends:
            cp.wait_send()

    return _pc(body, name=name, out_shape=jax.ShapeDtypeStruct(parts.shape, parts.dtype), in_specs=[ANY], out_specs=ANY,
               scratch_shapes=[pltpu.SemaphoreType.DMA((3,)), pltpu.SemaphoreType.DMA((3,))])(parts)


N_CHIP = 4
BIG = (("w_in", (1024, 2822), 1, (1024, 2822)), ("w_branch", (3, 768, 256), 2, (2304, 256)),
       ("w_out", (256, 1024), 0, (256, 1024)), ("w_ffn_in", (1024, 1408), 1, (1024, 1408)),
       ("w_ffn_out", (704, 1024), 0, (704, 1024)))
G768 = ((0, 3072), (3072, 3840), (7436, 8204))
GXBC, GQKV, GGATE = (3840, 5120), (5132, 7436), (8216, 11288)
GSMALL = ((5120, 5132), (8204, 8210), (8210, 8216))
W768, WXBC, WGATE = 4608, CONV_CH, 3 * D_MODEL
IN_PAD = W768 + WXBC + QKV_W + WGATE + SMALL_W


def _join_shards(slabs, axis, shard_shape):
    n_l = slabs.shape[1]
    parts = [slabs[j].reshape((n_l,) + shard_shape) for j in range(N_CHIP)]
    return jnp.concatenate(parts, axis=axis + 1)


def _split_shards(full, axis, rows_cols):
    n_l = full.shape[0]
    size = full.shape[axis + 1] // N_CHIP
    return jnp.stack([lax.slice_in_dim(full, j * size, (j + 1) * size, axis=axis + 1).reshape((n_l,) + rows_cols)
                      for j in range(N_CHIP)])


def _gather_weights(w, chip, big=BIG):
    out = {}
    for n, shape, ax, rc in big:
        n_l = w[n].shape[0]
        mine = w[n].astype(BF).reshape((n_l,) + rc)
        slabs = lax.dynamic_update_slice(_chip_gather("gather_" + n, mine), mine[None], (chip, 0, 0, 0))
        out[n] = _join_shards(slabs, ax, shape)
    return out


def _pair_stage(full_grads, core, big=BIG):
    out = {}
    for n, _, ax, (rows, cols) in big:
        n_l = full_grads[n].shape[0]
        slabs = _split_shards(full_grads[n], ax, (rows, cols))
        half = rows // 2
        keep = lax.dynamic_slice_in_dim(slabs, core * half, half, axis=2).reshape(-1, cols)
        give = lax.dynamic_slice_in_dim(slabs, (1 - core) * half, half, axis=2).reshape(-1, cols)
        got = _pair_swap("pair_swap_" + n, give)
        (pair_sum,) = _ew("pair_sum_" + n, lambda a, b: (a.astype(F32) + b.astype(F32),), [keep, got], [BF])
        out[n] = pair_sum.reshape(N_CHIP, n_l * half, cols)
    return out


def _own_slab(landed, pair_sum, chip):
    return lax.dynamic_update_slice(landed, lax.dynamic_slice_in_dim(pair_sum, chip, 1, axis=0), (chip, 0, 0))


def _finish_reduce(parts, n_l, core, big=BIG):
    out = {}
    for n, shape, _, (rows, cols) in big:
        half = rows // 2
        mine = _sum_leading("chip_sum_" + n, parts[n], F32).reshape(n_l, half, cols)
        theirs = _pair_swap("pair_share_" + n, mine)
        full = jnp.concatenate([jnp.where(core == 0, mine, theirs), jnp.where(core == 0, theirs, mine)], axis=1)
        out[n] = full.reshape((n_l,) + shape)
    return out


def _reduce_grads(full_grads, chip, core, big=BIG):
    pair_sums = _pair_stage(full_grads, core, big)
    parts = {n: _own_slab(_chip_exchange("chip_exchange_" + n, pair_sums[n]), pair_sums[n], chip) for n, _, _, _ in big}
    return _finish_reduce(parts, full_grads[big[0][0]].shape[0], core, big)


HBM_SPEC = pl.BlockSpec(memory_space=pltpu.HBM)
SEM_SPEC = pl.BlockSpec(memory_space=pltpu.SEMAPHORE)
DATAFLOW = pltpu.SideEffectType.DATAFLOW_SIDE_EFFECTING


def _exchange_copies(p_ref, land_ref, sems, waiting, spread):
    x, y, c = _place()
    me = 2 * x + y
    out = []
    for j, (px, py) in enumerate([(1 - x, y), (x, 1 - y), (1 - x, 1 - y)]):
        out.append(pltpu.make_async_remote_copy(src_ref=p_ref if spread else p_ref.at[2 * px + py],
                                                dst_ref=land_ref.at[2 * px + py if waiting else me],
                                                send_sem=sems[j], recv_sem=sems[3 + j], device_id=(px, py, c),
                                                device_id_type=MESH))
    return out


def _exchange_start(name, parts, after, spread=False):
    land_shape = ((N_CHIP,) + parts.shape) if spread else parts.shape

    def body(p_ref, land_ref, after_ref, s0, s1, s2, r0, r1, r2, p_thru, land_thru, token):
        for cp in _exchange_copies(p_ref, land_ref, (s0, s1, s2, r0, r1, r2), False, spread):
            cp.start()
        token[...] = jnp.zeros_like(token)

    res = _pc(body, name=name,
              out_shape=(pltpu.SemaphoreType.DMA(()),) * 6 + (pltpu.HBM(parts.shape, parts.dtype), pltpu.HBM(land_shape, parts.dtype),
                                                            jax.ShapeDtypeStruct((8, LANES), F32)),
              in_specs=(HBM_SPEC, HBM_SPEC, ANY),
              out_specs=(SEM_SPEC,) * 6 + (HBM_SPEC, HBM_SPEC, pl.BlockSpec(memory_space=pltpu.VMEM)),
              input_output_aliases={0: 6, 1: 7}, compiler_params=pltpu.CompilerParams(has_side_effects=DATAFLOW))(
        pltpu.with_memory_space_constraint(parts, pltpu.HBM),
        pltpu.with_memory_space_constraint(lax.empty(land_shape, parts.dtype), pltpu.HBM), after)
    return res[:6], res[6], res[7], res[8]


def _exchange_wait(name, sems, p_thru, land_thru, after, spread=False):
    def body(p_ref, land_ref, s0, s1, s2, r0, r1, r2, after_ref, p_dead, got_ref):
        for cp in _exchange_copies(p_ref, land_ref, (s0, s1, s2, r0, r1, r2), True, spread):
            cp.wait_send()
            cp.wait_recv()

    return _pc(body, name=name, out_shape=(pltpu.HBM(p_thru.shape, p_thru.dtype), pltpu.HBM(land_thru.shape, land_thru.dtype)),
               in_specs=(HBM_SPEC, HBM_SPEC) + (SEM_SPEC,) * 6 + (ANY,), out_specs=(HBM_SPEC, HBM_SPEC),
               input_output_aliases={0: 0, 1: 1}, compiler_params=pltpu.CompilerParams(has_side_effects=DATAFLOW))(
        p_thru, land_thru, *sems, after)[1]


def _regroup_w_in(w):
    cat = lambda spans: jnp.concatenate([w[:, a:b] for a, b in spans], axis=1)
    small = jnp.concatenate([cat(GSMALL), jnp.zeros((w.shape[0], SMALL_W - 24), w.dtype)], axis=1)
    return cat(G768), cat((GXBC,)), cat((GQKV,)), cat((GGATE,)), small


def _ungroup_w_in(d):
    o_xbc, o_qkv, o_gate, o_small = W768, W768 + WXBC, W768 + WXBC + QKV_W, W768 + WXBC + QKV_W + WGATE
    spans = ((0, 3072), (3072, 3840), (o_xbc, o_xbc + WXBC), (o_small, o_small + 12), (o_qkv, o_qkv + QKV_W),
             (3840, 4608), (o_small + 12, o_small + 18), (o_small + 18, o_small + 24), (o_gate, o_gate + WGATE))
    return jnp.concatenate([d[:, a:b] for a, b in spans], axis=1)


def _lane_pad(v, off):
    return jnp.pad(v, (off, LANES - off - v.shape[0]))[None, :]


STATE6 = (N_HEAD6, HEAD, HEAD)


def _mixer_inputs(sv, lp):
    p768, pxbc, pqkv, psmall = sv["p768"], sv["pxbc"], sv["pqkv"], sv["psmall"]
    hgrn = ([(p768, MIX_W, j) for j in range(4)], [], [lp["lb"], lp["hgrn_norm"]])
    ssd = ([(p768, MIX_W, 4), (pxbc, CONV_CH, 0), (psmall, LANES, 0)], [1],
           [lp["ssm_conv_w"], lp["ssm_conv_b"], lp["ssm_dt_bias"], lp["ssm_a_log"], lp["ssm_d"], lp["ssm_norm"]])
    gdn = ([(pqkv, QKV_W, 0), (p768, MIX_W, 5), (psmall, LANES, 0)], [0],
           [lp["gdn_conv_w"], lp["gdn_dt_bias"], lp["gdn_a_log"], lp["gdn_norm"]])
    return hgrn, ssd, gdn


def _layer_fwd(x, md, lw, lp):
    sv = {"x": x}
    (sv["h1"],) = _tile_fwd("lnmod1", _lnmod_fn, [(x, D_MODEL, 0)], [lp["norm_mix"], md["sc1"], md["sh1"]], [(D_MODEL, BF)])
    for nm in ("768", "xbc", "qkv", "gate", "small"):
        sv["p" + nm] = _mm(sv["h1"], lw["win_" + nm], "nn", F32, "proj_" + nm)
    hgrn, ssd, gdn = _mixer_inputs(sv, lp)
    sv["y_h"], sv["st_h"] = _scan_fwd("hgrn_fwd", _hgrn_chunk, *hgrn, MIX_W, STATE6)
    sv["y_s"], sv["st_s"] = _scan_fwd("ssd_fwd", _ssd_chunk, *ssd, MIX_W, STATE6)
    sv["y_g"], sv["st_g"], sv["inv_g"] = _scan_fwd("gdn_fwd", _gdn_chunk, *gdn, MIX_W, STATE6, aux_width=N_HEAD6 * LANES)
    (sv["merged"],) = _tile_fwd("merge", _merge_fn, _merge_tiles(sv), [lw["w_branch"], lp["b_merge"]], [(D_MODEL, BF)])
    sv["out"] = _mm(sv["merged"], lw["w_out"], "nn", F32, "out_proj")
    (sv["x_mid"],) = _tile_fwd("resid1", _resid_fn, [(x, D_MODEL, 0), (sv["out"], D_MODEL, 0)], [md["g1"]], [(D_MODEL, F32)])
    (sv["h2"],) = _tile_fwd("lnmod2", _lnmod_fn, [(sv["x_mid"], D_MODEL, 0)], [lp["norm_ffn"], md["sc2"], md["sh2"]],
                            [(D_MODEL, BF)])
    sv["gu"] = _mm(sv["h2"], lw["w_ffn_in"], "nn", F32, "ffn_in")
    (sv["act"],) = _tile_fwd("swiglu", _swiglu_fn, [(sv["gu"], 2 * FFN_H, 0)], [], [(FFN_H, BF)])
    sv["o2"] = _mm(sv["act"], lw["w_ffn_out"], "nn", F32, "ffn_out")
    (x_out,) = _tile_fwd("resid2", _resid_fn, [(sv["x_mid"], D_MODEL, 0), (sv["o2"], D_MODEL, 0)], [md["g2"]], [(D_MODEL, F32)])
    return x_out, sv


def _merge_tiles(sv):
    return [(sv["y_h"], MIX_W, 0), (sv["y_s"], MIX_W, 0), (sv["y_g"], MIX_W, 0), (sv["pgate"], WGATE, 0)]


def _layer_bwd(dx_out, sv, md, lw, lp):
    g = {}
    x, x_mid = sv["x"], sv["x_mid"]
    d_o2, g["g2"] = _tile_bwd("resid2_b", _gated_fn, [(sv["o2"], D_MODEL, 0)], [md["g2"]], [dx_out], [BF])
    d_xmid = dx_out
    d_act =_mm(d_o2, lw["w_ffn_out"], "nt", F32, "ffn_out_dx")
    g["w_ffn_out"] = _mm(sv["act"], d_o2, "tn", BF, "ffn_out_dw")
    (d_gu,) = _tile_bwd("swiglu_b", _swiglu_fn, [(sv["gu"], 2 * FFN_H, 0)], [], [d_act], [BF])
    d_h2 = _mm(d_gu, lw["w_ffn_in"], "nt", F32, "ffn_in_dx")
    g["w_ffn_in"] = _mm(sv["h2"], d_gu, "tn", BF, "ffn_in_dw")
    d_xmid, g["norm_ffn"], g["sc2"], g["sh2"] = _tile_bwd(
        "lnmod2_b", _lnmod_fn, [(x_mid, D_MODEL, 0)], [lp["norm_ffn"], md["sc2"], md["sh2"]], [d_h2], [F32], add_to=(0, d_xmid))
    d_out, g["g1"] = _tile_bwd("resid1_b", _gated_fn, [(sv["out"], D_MODEL, 0)], [md["g1"]], [d_xmid], [BF])
    d_x = d_xmid
    d_merged = _mm(d_out, lw["w_out"], "nt", F32, "out_proj_dx")
    g["w_out"] = _mm(sv["merged"], d_out, "tn", BF, "out_proj_dw")
    d_yh, d_ys, d_yg, d_gate, g["w_branch"], g["b_merge"] = _tile_bwd(
        "merge_b", _merge_fn, _merge_tiles(sv), [lw["w_branch"], lp["b_merge"]], [d_merged], [F32, F32, F32, BF])
    hgrn, ssd, gdn = _mixer_inputs(sv, lp)
    d_q, d_f, d_v, d_g, g["lb"], g["hgrn_norm"] = _scan_bwd("hgrn_bwd", _hgrn_chunk, *hgrn, sv["st_h"], d_yh, [BF] * 4)
    (d_sz, d_xbc, d_small, g["ssm_conv_w"], g["ssm_conv_b"], g["ssm_dt_bias"], g["ssm_a_log"], g["ssm_d"],
     g["ssm_norm"]) = _scan_bwd("ssd_bwd", _ssd_chunk, *ssd, sv["st_s"], d_ys, [BF, BF, F32])
    d_qkv, d_gz, d_small, g["gdn_conv_w"], g["gdn_dt_bias"], g["gdn_a_log"], g["gdn_norm"] = _scan_bwd(
        "gdn_bwd", _gdn_chunk, gdn[0] + [(sv["inv_g"], N_HEAD6 * LANES, 0)], gdn[1], gdn[2], sv["st_g"], d_yg,
        [BF, BF, BF, None], extra=(2, d_small))
    d_proj = jnp.concatenate([d_q, d_f, d_v, d_g, d_sz, d_gz, d_xbc, d_qkv, d_gate, d_small,
                              jnp.zeros((x.shape[0], SMALL_W - LANES), BF)], axis=1)
    d_h1 = _mm(d_proj, lw["win_all"], "nt", F32, "proj_dx")
    g["w_in"] = _ungroup_w_in(_mm(sv["h1"], d_proj, "tn", BF, "proj_dw"))
    d_x, g["norm_mix"], g["sc1"], g["sh1"] = _tile_bwd(
        "lnmod1_b", _lnmod_fn, [(x, D_MODEL, 0)], [lp["norm_mix"], md["sc1"], md["sh1"]], [d_h1], [F32], add_to=(0, d_x))
    return d_x, g


SMALL_REPL = ("norm_mix", "norm_ffn", "b_merge", "hgrn_lb_logits", "hgrn_norm", "ssm_conv_w", "ssm_conv_b", "ssm_dt_bias",
              "ssm_a_log", "ssm_d", "ssm_norm", "gdn_conv_w", "gdn_dt_bias", "gdn_a_log", "gdn_norm", "norm_final")
WEIGHTS = ("w_ada", "b_ada", "norm_mix", "norm_ffn", "w_in", "b_merge", "hgrn_lb_logits", "hgrn_norm", "ssm_conv_w",
           "ssm_conv_b", "ssm_dt_bias", "ssm_a_log", "ssm_d", "ssm_norm", "gdn_conv_w", "gdn_dt_bias", "gdn_a_log",
           "gdn_norm", "w_branch", "w_out", "w_ffn_in", "w_ffn_out", "norm_final")
SMALL_ROWS = 120


def _pad_rows(flat, n_rows, n_cols):
    return jnp.concatenate([flat, jnp.zeros((n_rows * n_cols - flat.shape[0],), flat.dtype)]).reshape(n_rows, n_cols)


def _device_step(x, tgt, mod, lb, wfull, sp, chip=None, core=None, order_after=None):
    mds, lps, svs = [], [], []
    h = x
    weights_of = wfull if callable(wfull) else (lambda layer, after: wfull[layer])
    wfull = []
    for l in range(DEPTH):
        wfull.append(weights_of(l, h))
        md = {n: mod[l, i * D_MODEL:(i + 1) * D_MODEL][None, :] for i, n in enumerate(("sh1", "sc1", "g1", "sh2", "sc2", "g2"))}
        if l == 0 and order_after is not None:
            md["sc1"] = md["sc1"] + order_after
        lp = {n: sp[n][l][None, :] for n in ("norm_mix", "norm_ffn", "b_merge", "hgrn_norm", "ssm_conv_b", "ssm_norm", "gdn_norm")}
        lp["lb"] = lb[l][None, :]
        lp["ssm_conv_w"], lp["gdn_conv_w"] = sp["ssm_conv_w"][l], sp["gdn_conv_w"][l]
        for n in ("ssm_dt_bias", "ssm_a_log", "ssm_d"):
            lp[n] = _lane_pad(sp[n][l], DT_OFF)
        for n in ("gdn_dt_bias", "gdn_a_log"):
            lp[n] = _lane_pad(sp[n][l], GA_OFF)
        h, sv = _layer_fwd(h, md, wfull[l], lp)
        mds.append(md), lps.append(lp), svs.append(sv)
    loss, dh, d_nf = _final_loss(h, tgt, sp["norm_final"][None, :])
    grads = [None] * DEPTH
    if core is None:
        for l in reversed(range(DEPTH)):
            dh, grads[l] = _layer_bwd(dh, svs[l], mds[l], wfull[l], lps[l])
        return loss, dh, d_nf, grads
    names = [n for n, _, _, _ in BIG]
    landed, flying = [None] * DEPTH, None
    for l in reversed(range(DEPTH)):
        md = mds[l]
        if flying is not None:
            md = dict(md, g2=md["g2"] + sum(tok[0, 0] for _, _, _, tok in flying.values()))
        dh, grads[l] = _layer_bwd(dh, svs[l], md, wfull[l], lps[l])
        if flying is not None:
            landed[l + 1] = {n: _own_slab(_exchange_wait(f"exchange_wait_{n}_{l + 1}", *flying[n][:3], dh), sums[n], chip)
                             for n in names}
        sums = _pair_stage({n: grads[l][n].astype(BF)[None] for n in names}, core)
        if l > 0:
            flying = {n: _exchange_start(f"exchange_start_{n}_{l}", sums[n], dh) for n in names}
        else:
            landed[0] = {n: _own_slab(_chip_exchange("chip_exchange_" + n, sums[n]), sums[n], chip) for n in names}
    parts = {n: jnp.concatenate([landed[l][n] for l in range(DEPTH)], axis=1) for n in names}
    return loss, dh, d_nf, grads, _finish_reduce(parts, DEPTH, core)


def kernel(x, c, w_ada, b_ada, norm_mix, norm_ffn, w_in, b_merge, hgrn_lb_logits, hgrn_norm, ssm_conv_w, ssm_conv_b, ssm_dt_bias, ssm_a_log, ssm_d, ssm_norm, gdn_conv_w, gdn_dt_bias, gdn_a_log, gdn_norm, w_branch, w_out, w_ffn_in, w_ffn_out, norm_final, loss_target, m_w_ada, m_b_ada, m_norm_mix, m_norm_ffn, m_w_in, m_b_merge, m_hgrn_lb_logits, m_hgrn_norm, m_ssm_conv_w, m_ssm_conv_b, m_ssm_dt_bias, m_ssm_a_log, m_ssm_d, m_ssm_norm, m_gdn_conv_w, m_gdn_dt_bias, m_gdn_a_log, m_gdn_norm, m_w_branch, m_w_out, m_w_ffn_in, m_w_ffn_out, m_norm_final, v_w_ada, v_b_ada, v_norm_mix, v_norm_ffn, v_w_in, v_b_merge, v_hgrn_lb_logits, v_hgrn_norm, v_ssm_conv_w, v_ssm_conv_b, v_ssm_dt_bias, v_ssm_a_log, v_ssm_d, v_ssm_norm, v_gdn_conv_w, v_gdn_dt_bias, v_gdn_a_log, v_gdn_norm, v_w_branch, v_w_out, v_w_ffn_in, v_w_ffn_out, v_norm_final):
    w = dict(w_ada=w_ada, b_ada=b_ada, norm_mix=norm_mix, norm_ffn=norm_ffn, w_in=w_in, b_merge=b_merge,
             hgrn_lb_logits=hgrn_lb_logits, hgrn_norm=hgrn_norm, ssm_conv_w=ssm_conv_w, ssm_conv_b=ssm_conv_b,
             ssm_dt_bias=ssm_dt_bias, ssm_a_log=ssm_a_log, ssm_d=ssm_d, ssm_norm=ssm_norm, gdn_conv_w=gdn_conv_w,
             gdn_dt_bias=gdn_dt_bias, gdn_a_log=gdn_a_log, gdn_norm=gdn_norm, w_branch=w_branch, w_out=w_out,
             w_ffn_in=w_ffn_in, w_ffn_out=w_ffn_out, norm_final=norm_final)
    m = dict(w_ada=m_w_ada, b_ada=m_b_ada, norm_mix=m_norm_mix, norm_ffn=m_norm_ffn, w_in=m_w_in, b_merge=m_b_merge,
             hgrn_lb_logits=m_hgrn_lb_logits, hgrn_norm=m_hgrn_norm, ssm_conv_w=m_ssm_conv_w, ssm_conv_b=m_ssm_conv_b,
             ssm_dt_bias=m_ssm_dt_bias, ssm_a_log=m_ssm_a_log, ssm_d=m_ssm_d, ssm_norm=m_ssm_norm, gdn_conv_w=m_gdn_conv_w,
             gdn_dt_bias=m_gdn_dt_bias, gdn_a_log=m_gdn_a_log, gdn_norm=m_gdn_norm, w_branch=m_w_branch, w_out=m_w_out,
             w_ffn_in=m_w_ffn_in, w_ffn_out=m_w_ffn_out, norm_final=m_norm_final)
    v = dict(w_ada=v_w_ada, b_ada=v_b_ada, norm_mix=v_norm_mix, norm_ffn=v_norm_ffn, w_in=v_w_in, b_merge=v_b_merge,
             hgrn_lb_logits=v_hgrn_lb_logits, hgrn_norm=v_hgrn_norm, ssm_conv_w=v_ssm_conv_w, ssm_conv_b=v_ssm_conv_b,
             ssm_dt_bias=v_ssm_dt_bias, ssm_a_log=v_ssm_a_log, ssm_d=v_ssm_d, ssm_norm=v_ssm_norm, gdn_conv_w=v_gdn_conv_w,
             gdn_dt_bias=v_gdn_dt_bias, gdn_a_log=v_gdn_a_log, gdn_norm=v_gdn_norm, w_branch=v_w_branch, w_out=v_w_out,
             w_ffn_in=v_w_ffn_in, w_ffn_out=v_w_ffn_out, norm_final=v_norm_final)
    xi, yi, ci = _place()
    chip, me = 2 * xi + yi, 4 * xi + 2 * yi + ci
    seq = x.shape[1]

    conv_flat = jnp.concatenate([ssm_conv_w.reshape(-1), gdn_conv_w.reshape(-1)])
    n_conv = conv_flat.shape[0]
    first = _all_gather_small("gather_c_conv", _pad_rows(jnp.concatenate([c[0], conv_flat]), 16, D_MODEL))
    c_all = first[:, 0, :]
    conv_all = first[0::2].reshape(N_CHIP, -1)[:, D_MODEL:D_MODEL + n_conv]
    n_ssm = ssm_conv_w.size
    sp = dict(w)
    sp["ssm_conv_w"] = jnp.concatenate([conv_all[j, :n_ssm].reshape(ssm_conv_w.shape) for j in range(N_CHIP)], axis=2)
    sp["gdn_conv_w"] = jnp.concatenate([conv_all[j, n_ssm:].reshape(gdn_conv_w.shape) for j in range(N_CHIP)], axis=2)

    ada_cols = w_ada.shape[2]
    mod_part = _ada_fwd(c_all, w_ada, lax.dynamic_slice_in_dim(b_ada, chip * ada_cols, ada_cols, axis=1))
    mod_all = _all_gather_small("gather_mod", mod_part.reshape(DEPTH * 8, ada_cols))[0::2].reshape(N_CHIP, DEPTH, 8, ada_cols)
    mod = lax.dynamic_index_in_dim(mod_all, me, axis=2, keepdims=False).transpose(1, 0, 2).reshape(DEPTH, N_CHIP * ada_cols)
    lb = _lb_fwd(hgrn_lb_logits)

    names = [n for n, _, _, _ in BIG]

    def layer_weights(full):
        full = dict(full)
        for nm, part in zip(("768", "xbc", "qkv", "gate", "small"), _regroup_w_in(full["w_in"])):
            full["win_" + nm] = part
        full["win_all"] = jnp.concatenate([full["win_" + nm] for nm in ("768", "xbc", "qkv", "gate", "small")], axis=1)
        return full

    first = _gather_weights({n: w[n][0:1] for n in names}, chip)
    first = layer_weights({n: first[n][0] for n in names})
    started = {}
    settled = mod[0:1, 0:LANES]
    for n in names:
        settled = settled + first[n].reshape(-1, first[n].shape[-1])[0:1, 0:LANES].astype(F32)
    for n, _, _, (rows, cols) in BIG:
        mine = w[n][1:].astype(BF).reshape((DEPTH - 1, rows, cols))
        started[n] = _exchange_start("gather_start_" + n, lax.dynamic_slice_in_dim(mine, ci * (rows // 2), rows // 2, axis=1),
                                     settled, spread=True)
    rest = []

    def weights_of(layer, after):
        if layer == 0:
            return first
        if not rest:
            full = {}
            for n, shape, ax, _ in BIG:
                sems, my_half, landing, _ = started[n]
                landed = _exchange_wait("gather_wait_" + n, sems, my_half, landing, after, spread=True)
                landed = lax.dynamic_update_slice(landed, my_half[None], (chip, 0, 0, 0))
                theirs = _pair_swap("gather_share_" + n, landed)
                slabs = jnp.concatenate([jnp.where(ci == 0, landed, theirs), jnp.where(ci == 0, theirs, landed)], axis=2)
                full[n] = _join_shards(slabs, ax, shape)
            rest.extend(layer_weights({n: full[n][i] for n in names}) for i in range(DEPTH - 1))
        return rest[layer - 1]

    order = sum(tok[0, 0] for _, _, _, tok in started.values())
    loss8, d_x, d_nf, lg, grad = _device_step(x[0], loss_target[0], mod, lb, weights_of, sp, chip, ci, order)

    dmod = jnp.stack([jnp.concatenate([lg[l][n] for n in ("sh1", "sc1", "g1", "sh2", "sc2", "g2")], axis=1)[0] for l in range(DEPTH)])
    d_lb = jnp.stack([lg[l]["lb"][0] for l in range(DEPTH)])
    contrib = {
        "norm_mix": jnp.stack([lg[l]["norm_mix"][0] for l in range(DEPTH)]),
        "norm_ffn": jnp.stack([lg[l]["norm_ffn"][0] for l in range(DEPTH)]),
        "b_merge": jnp.stack([lg[l]["b_merge"][0] for l in range(DEPTH)]),
        "hgrn_lb_logits": _lb_bwd(hgrn_lb_logits, d_lb),
        "hgrn_norm": jnp.stack([lg[l]["hgrn_norm"][0] for l in range(DEPTH)]),
        "ssm_conv_w": jnp.stack([lg[l]["ssm_conv_w"] for l in range(DEPTH)]),
        "ssm_conv_b": jnp.stack([lg[l]["ssm_conv_b"][0] for l in range(DEPTH)]),
        "ssm_dt_bias": jnp.stack([lg[l]["ssm_dt_bias"][0, DT_OFF:DT_OFF + 12] for l in range(DEPTH)]),
        "ssm_a_log": jnp.stack([lg[l]["ssm_a_log"][0, DT_OFF:DT_OFF + 12] for l in range(DEPTH)]),
        "ssm_d": jnp.stack([lg[l]["ssm_d"][0, DT_OFF:DT_OFF + 12] for l in range(DEPTH)]),
        "ssm_norm": jnp.stack([lg[l]["ssm_norm"][0] for l in range(DEPTH)]),
        "gdn_conv_w": jnp.stack([lg[l]["gdn_conv_w"] for l in range(DEPTH)]),
        "gdn_dt_bias": jnp.stack([lg[l]["gdn_dt_bias"][0, GA_OFF:GA_OFF + 6] for l in range(DEPTH)]),
        "gdn_a_log": jnp.stack([lg[l]["gdn_a_log"][0, GA_OFF:GA_OFF + 6] for l in range(DEPTH)]),
        "gdn_norm": jnp.stack([lg[l]["gdn_norm"][0] for l in range(DEPTH)]),
        "norm_final": d_nf[0],
    }
    flat = jnp.concatenate([dmod.reshape(-1)] + [contrib[n].reshape(-1) for n in SMALL_REPL] + [loss8[0, 0:1]])
    small_all = _all_gather_small("gather_small_grads", _pad_rows(flat, SMALL_ROWS, D_MODEL))
    total = _sum_leading("small_grad_sum", small_all, F32).reshape(-1)
    n_mod = dmod.size
    grad["b_ada"] = total[:n_mod].reshape(b_ada.shape)
    off = n_mod
    full_small = {}
    for n in SMALL_REPL:
        full_small[n] = total[off:off + contrib[n].size].reshape(contrib[n].shape)
        off += contrib[n].size
    loss = total[off]
    for n in SMALL_REPL:
        if n in ("ssm_conv_w", "gdn_conv_w"):
            cols = w[n].shape[2]
            grad[n] = lax.dynamic_slice_in_dim(full_small[n], chip * cols, cols, axis=2)
        else:
            grad[n] = full_small[n]
    dmod_cols = lax.dynamic_slice_in_dim(small_all[:, :n_mod // D_MODEL, :].reshape(8, DEPTH, -1), chip * ada_cols, ada_cols, axis=2)
    grad["w_ada"] = _ada_bwd(c_all, dmod_cols.transpose(1, 0, 2))

    delta, new_m, new_v = {}, {}, {}
    big_names = ("w_ada",) + tuple(n for n, _, _, _ in BIG)
    for n in big_names:
        delta[n], new_m[n], new_v[n] = _adamw("adamw_" + n, w[n], grad[n], m[n], v[n])
    small_names = [n for n in WEIGHTS if n not in big_names]
    packs = [_pad_rows(jnp.concatenate([d[n].reshape(-1) for n in small_names]), 584, LANES) for d in (w, grad, m, v)]
    outs = _ew("adamw_small", _adamw_fn, packs, [F32, F32, F32])
    off = 0
    for n in small_names:
        for dst, o in zip((delta, new_m, new_v), outs):
            dst[n] = o.reshape(-1)[off:off + w[n].size].reshape(w[n].shape)
        off += w[n].size
    return (loss, d_x[None], *[grad[n] for n in WEIGHTS], *[delta[n] for n in WEIGHTS], *[new_m[n] for n in WEIGHTS],
            *[new_v[n] for n in WEIGHTS])
```

```python
import functools

import jax
import jax.numpy as jnp
from jax import lax
from jax.experimental import pallas as pl
from jax.experimental.pallas import tpu as pltpu

F32 = jnp.float32
BF = jnp.bfloat16
HI = lax.Precision.HIGHEST

D_MODEL = 1024
DEPTH = 4
CHUNK = 64
MIX_W = 768
HEAD = 128
N_HEAD6 = 6
SSM_P = 64
SSM_N = 128
CONV_CH = 1280
QKV_W = 2304
FFN_H = 2816
IN_WIDTH = 11288
NORM_EPS = 1e-6
F_MIN = 1e-30
HALO = 8
HEAD_GROUP = 6
HGRN_SUB = 8
SMALL_W = 512
LANES = 128
DT_OFF, GB_OFF, GA_OFF = 0, 12, 18

ADAM_LR, ADAM_B1, ADAM_B2, ADAM_EPS, ADAM_WD, ADAM_STEP = 0.001, 0.9, 0.999, 1e-08, 0.01, 10

VMEM_LIMIT = 56 * 1024 * 1024
TOKEN_TILE = 256


def _pc(body, **kw):
    return pl.pallas_call(body, **kw)


def _cparams(sem):
    return pltpu.CompilerParams(dimension_semantics=sem, vmem_limit_bytes=VMEM_LIMIT)


def _bdot(a, b):
    return jnp.dot(a.astype(BF), b.astype(BF), preferred_element_type=F32)


def _bdot_nt(a, b):
    return lax.dot_general(a.astype(BF), b.astype(BF), (((1,), (1,)), ((), ())), preferred_element_type=F32)


def _bdot_tn(a, b):
    return lax.dot_general(a.astype(BF), b.astype(BF), (((0,), (0,)), ((), ())), preferred_element_type=F32)


def _silu(x):
    return x * jax.nn.sigmoid(x)


def _tri_mask(n, strict=False):
    t = lax.broadcasted_iota(jnp.int32, (n, n), 0)
    s = lax.broadcasted_iota(jnp.int32, (n, n), 1)
    return (s < t) if strict else (s <= t)


def _masked_exp(diff, mask):
    return jnp.where(mask, jnp.exp(jnp.where(mask, diff, 0.0)), 0.0)


def _split_bf16(x, n):
    parts, rest = [], x
    for _ in range(n):
        p = rest.astype(BF)
        parts.append(p)
        rest = rest - p.astype(F32)
    return parts


def _tri_sum(x, reverse):
    n, w = x.shape
    t = lax.broadcasted_iota(jnp.int32, (n, n), 0)
    s = lax.broadcasted_iota(jnp.int32, (n, n), 1)
    tri = jnp.where((s >= t) if reverse else (s <= t), 1.0, 0.0).astype(BF)
    y = jnp.dot(tri, jnp.concatenate(_split_bf16(x, 3), axis=1), preferred_element_type=F32)
    return y[:, :w] + y[:, w:2 * w] + y[:, 2 * w:]


@jax.custom_vjp
def _cumsum_rows(x):
    return _tri_sum(x, False)


_cumsum_rows.defvjp(lambda x: (_tri_sum(x, False), None), lambda _, g: (_tri_sum(g, True),))


def _dot_split(a, b, transpose_a=False):
    dims = (((0,), (0,)) if transpose_a else ((1,), (0,)), ((), ()))
    a_hi, a_lo = _split_bf16(a, 2)
    b_hi, b_lo = _split_bf16(b, 2)
    w = b.shape[1]
    y = lax.dot_general(a_hi, jnp.concatenate([b_hi, b_lo], axis=1), dims, preferred_element_type=F32)
    return y[:, :w] + y[:, w:] + lax.dot_general(a_lo, b_hi, dims, preferred_element_type=F32)


def _rms(x, w):
    return x * lax.rsqrt(jnp.mean(x * x, axis=-1, keepdims=True) + NORM_EPS) * w


def _causal_conv(halo, x, w):
    ext = jnp.concatenate([halo, x], axis=0)
    n = x.shape[0]
    acc = w[0:1, :] * ext[HALO - 3:HALO - 3 + n, :]
    for i in range(1, 4):
        acc = acc + w[i:i + 1, :] * ext[HALO - 3 + i:HALO - 3 + i + n, :]
    return acc


def _unit_lower_inverses(mats):
    n = mats[0].shape[0]
    t = lax.broadcasted_iota(jnp.int32, (n, LANES), 0)
    s_ = lax.broadcasted_iota(jnp.int32, (n, LANES), 1)
    xs = [jnp.where(t == s_, 1.0, 0.0).astype(F32) for _ in mats]
    for s in range(n - 1):
        r0 = 8 * ((s + 1) // 8)
        for i, a in enumerate(mats):
            x = xs[i]
            low = x[r0:] - a[r0:, s:s + 1] * x[s:s + 1, :]
            xs[i] = low if r0 == 0 else jnp.concatenate([x[:r0], low], axis=0)
    return xs


@jax.custom_vjp
def _solves_with_inverses(invs, mats, rhss):
    return [_dot_split(inv, r) for inv, r in zip(invs, rhss)]


def _swi_fwd(invs, mats, rhss):
    xs = [_dot_split(inv, r) for inv, r in zip(invs, rhss)]
    return xs, (invs, xs)


def _swi_bwd(res, gs):
    invs, xs = res
    ys = [_dot_split(inv, g, transpose_a=True) for inv, g in zip(invs, gs)]
    das = [jnp.where(_tri_mask(CHUNK, strict=True), -_bdot_nt(y, x), 0.0) for y, x in zip(ys, xs)]
    return [jnp.zeros_like(inv) for inv in invs], das, ys


_solves_with_inverses.defvjp(_swi_fwd, _swi_bwd)


def _hgrn_chunk(tiles, halos, state, consts):
    q_raw, f_raw, v_all, g_raw = tiles
    lb, norm_w = consts
    q_all = _silu(q_raw)
    f = lb + (1.0 - lb) * jax.nn.sigmoid(f_raw)
    logf = jnp.log(jnp.maximum(f, F_MIN))
    k_all = (1.0 - lb) * jax.nn.sigmoid(-f_raw)
    b_all = _cumsum_rows(logf)
    sub = HGRN_SUB
    row = lax.broadcasted_iota(jnp.int32, (sub, 1), 0)
    src_row = lax.broadcasted_iota(jnp.int32, (CHUNK, 1), 0)
    src_lane = lax.broadcasted_iota(jnp.int32, (1, CHUNK), 1)
    heads = range(N_HEAD6)
    n_sub = CHUNK // sub
    cols = [slice(h * HEAD, (h + 1) * HEAD) for h in heads]
    qs, ks, vs, bs = ([a[:, sl] for sl in cols] for a in (q_all, k_all, v_all, b_all))
    o_inter = [_bdot_nt(qs[h] * jnp.exp(bs[h]), state[h]) for h in heads]
    blocks = [[None] * n_sub for _ in heads]
    for i in range(n_sub):
        r0 = i * sub
        for h in heads:
            if i > 0:
                ref = bs[h][r0 - 1:r0, :]
                blocks[h][i] = _bdot_nt(qs[h][r0:r0 + sub] * jnp.exp(bs[h][r0:r0 + sub] - ref),
                                        ks[h] * _masked_exp(ref - bs[h], src_row < r0))
            else:
                blocks[h][i] = jnp.zeros((sub, CHUNK), F32)
    for h in heads:
        for i in range(n_sub):
            r0 = i * sub
            qi, ki, bi = qs[h][r0:r0 + sub], ks[h][r0:r0 + sub], bs[h][r0:r0 + sub]
            for s in range(sub):
                e = _masked_exp(bi - bi[s:s + 1, :], row >= s)
                col = jnp.sum(qi * ki[s:s + 1, :] * e, axis=1, keepdims=True)
                blocks[h][i] = jnp.where(src_lane == r0 + s, col, blocks[h][i])
    os_ = [_bdot(jnp.concatenate(blocks[h], axis=0), vs[h]) + o_inter[h] for h in heads]
    ends = [bs[h][CHUNK - 1:CHUNK, :] for h in heads]
    new_states = [state[h] * jnp.exp(ends[h]) + _bdot_tn(vs[h], ks[h] * jnp.exp(ends[h] - bs[h])) for h in heads]
    outs = [_rms(os_[h], norm_w) * _silu(g_raw[:, cols[h]]) for h in heads]
    return (jnp.concatenate(outs, axis=1),), jnp.stack(new_states)


def _ssd_chunk(tiles, halos, state, consts):
    z, xbc_raw, small = tiles
    (halo,) = halos
    conv_w, conv_b, dt_bias, a_log, d_skip, norm_w = consts
    xbc = _silu(_causal_conv(halo, xbc_raw, conv_w) + conv_b)
    xs, bm, cm = xbc[:, :MIX_W], xbc[:, MIX_W:MIX_W + 2 * SSM_N], xbc[:, MIX_W + 2 * SSM_N:]
    dt = jax.nn.softplus(small + dt_bias)
    cum = _cumsum_rows(-jnp.exp(a_log) * dt)
    cum_t2 = jnp.concatenate([cum, cum], axis=0).T
    lane = lax.broadcasted_iota(jnp.int32, (1, LANES), 1)
    first = lane < SSM_P
    hm0 = jnp.where(first, 1.0, 0.0).astype(F32)
    hm1 = 1.0 - hm0
    src = jnp.where(first, lane, lane - SSM_P)
    tri2 = src <= lax.broadcasted_iota(jnp.int32, (CHUNK, 1), 0)
    pick = lambda a, b: jnp.where(first, a, b)
    bgs = [bm[:, g * SSM_N:(g + 1) * SSM_N] for g in range(2)]
    cgs = [cm[:, g * SSM_N:(g + 1) * SSM_N] for g in range(2)]
    gmats = [_bdot_nt(cgs[g], jnp.concatenate([bgs[g], bgs[g]], axis=0)) for g in range(2)]
    pairs = range(6)
    xps = [xs[:, p * LANES:(p + 1) * LANES] for p in pairs]
    c0s = [cum[:, 2 * p:2 * p + 1] for p in pairs]
    c1s = [cum[:, 2 * p + 1:2 * p + 2] for p in pairs]
    e0s = [cum[CHUNK - 1:CHUNK, 2 * p:2 * p + 1] for p in pairs]
    e1s = [cum[CHUNK - 1:CHUNK, 2 * p + 1:2 * p + 2] for p in pairs]
    segs = [_masked_exp(pick(c0s[p], c1s[p]) - pick(cum_t2[2 * p:2 * p + 1, :], cum_t2[2 * p + 1:2 * p + 2, :]), tri2)
            for p in pairs]
    vms = []
    for p in pairs:
        v = xps[p] * pick(dt[:, 2 * p:2 * p + 1], dt[:, 2 * p + 1:2 * p + 2])
        vms.append(jnp.concatenate([v * hm0, v * hm1], axis=0))
    y_intra = [_bdot(gmats[p // 3] * segs[p], vms[p]) for p in pairs]
    y_inter = [_bdot(jnp.concatenate([cgs[p // 3] * jnp.exp(c0s[p]), cgs[p // 3] * jnp.exp(c1s[p])], axis=1),
                     jnp.concatenate([state[p] * hm0, state[p] * hm1], axis=0)) for p in pairs]
    new_states = [_bdot_tn(jnp.concatenate([bgs[p // 3] * jnp.exp(e0s[p] - c0s[p]),
                                            bgs[p // 3] * jnp.exp(e1s[p] - c1s[p])], axis=0), vms[p])
                  + state[p] * pick(jnp.exp(e0s[p]), jnp.exp(e1s[p])) for p in pairs]
    ys = [y_intra[p] + y_inter[p] + pick(d_skip[:, 2 * p:2 * p + 1], d_skip[:, 2 * p + 1:2 * p + 2]) * xps[p] for p in pairs]
    y = jnp.concatenate(ys, axis=1) * _silu(z)
    gw = MIX_W // 2
    y = jnp.concatenate([_rms(y[:, g * gw:(g + 1) * gw], norm_w[:, g * gw:(g + 1) * gw]) for g in range(2)], axis=1)
    return (y,), jnp.stack(new_states)


def _gdn_chunk(tiles, halos, state, consts):
    qkv_raw, z, small = tiles[:3]
    given = tiles[3] if len(tiles) > 3 else None
    (halo,) = halos
    conv_w, dt_bias, a_log, norm_w = consts
    qkv = _silu(_causal_conv(halo, qkv_raw, conv_w))
    beta_all = jax.nn.sigmoid(small)
    cum = _cumsum_rows(-jnp.exp(a_log) * jax.nn.softplus(small + dt_bias))
    cum_t = cum.T
    tri, tri_strict = _tri_mask(CHUNK), _tri_mask(CHUNK, strict=True)

    def group(hs):
        n = range(len(hs))
        qs, ks, betas, cs, ces, decays, rhss = [], [], [], [], [], [], []
        for h in hs:
            q = qkv[:, h * HEAD:(h + 1) * HEAD]
            k = qkv[:, MIX_W + h * HEAD:MIX_W + (h + 1) * HEAD]
            v = qkv[:, 2 * MIX_W + h * HEAD:2 * MIX_W + (h + 1) * HEAD]
            q = q * lax.rsqrt(jnp.sum(q * q, axis=-1, keepdims=True) + NORM_EPS) * (HEAD ** -0.5)
            k = k * lax.rsqrt(jnp.sum(k * k, axis=-1, keepdims=True) + NORM_EPS)
            beta = beta_all[:, GB_OFF + h:GB_OFF + h + 1]
            c, c_t = cum[:, GA_OFF + h:GA_OFF + h + 1], cum_t[GA_OFF + h:GA_OFF + h + 1, :]
            qs.append(q), ks.append(k), betas.append(beta), cs.append(c)
            ces.append(cum[CHUNK - 1:CHUNK, GA_OFF + h:GA_OFF + h + 1])
            decays.append(_masked_exp(c - c_t, tri))
            rhss.append(jnp.concatenate([v * beta, k * (beta * jnp.exp(c))], axis=1))
        sts = [state[h] for h in hs]
        qk_kks = [_bdot_nt(jnp.concatenate([qs[i], ks[i]], axis=0), ks[i]) for i in n]
        mats = [jnp.where(tri_strict, betas[i] * qk_kks[i][CHUNK:] * decays[i], 0.0) for i in n]
        wide = _unit_lower_inverses(mats) if given is None else [given[:, h * LANES:(h + 1) * LANES] for h in hs]
        sols = _solves_with_inverses([inv[:, :CHUNK] for inv in wide], mats, rhss)
        on_states = [_bdot(jnp.concatenate([sols[i][:, HEAD:], qs[i] * jnp.exp(cs[i])], axis=0), sts[i]) for i in n]
        us = [sols[i][:, :HEAD] - on_states[i][:CHUNK] for i in n]
        os_ = [on_states[i][CHUNK:] + _bdot(qk_kks[i][:CHUNK] * decays[i], us[i]) for i in n]
        new = [jnp.exp(ces[i]) * sts[i] + _bdot_tn(ks[i] * jnp.exp(ces[i] - cs[i]), us[i]) for i in n]
        outs = [_rms(os_[i], norm_w) * _silu(z[:, h * HEAD:(h + 1) * HEAD]) for i, h in enumerate(hs)]
        return outs, new, wide

    outs, new_states, inverses = [], [], []
    for h0 in range(0, N_HEAD6, HEAD_GROUP):
        o, s, w = group(list(range(h0, h0 + HEAD_GROUP)))
        outs += o
        new_states += s
        inverses += w
    y = jnp.concatenate(outs, axis=1)
    return ((y,) if given is not None else (y, jnp.concatenate(inverses, axis=1))), jnp.stack(new_states)


def _scan_fwd(name, fn, tiled, halo_idx, consts, out_width, state_shape, aux_width=None):
    seq = tiled[0][0].shape[0]
    nc = seq // CHUNK
    n_t, n_h, n_c = len(tiled), len(halo_idx), len(consts)

    def body(*refs):
        t_refs, h_refs, c_refs = refs[:n_t], refs[n_t:n_t + n_h], refs[n_t + n_h:n_t + n_h + n_c]
        y_ref, save_ref = refs[n_t + n_h + n_c:n_t + n_h + n_c + 2]
        st_ref = refs[-1]
        i = pl.program_id(0)

        @pl.when(i == 0)
        def _():
            st_ref[...] = jnp.zeros_like(st_ref)

        flag = jnp.where(i > 0, 1.0, 0.0).astype(F32)
        st = st_ref[...]
        outs, new = fn([r[...] for r in t_refs], [r[...] * flag for r in h_refs], st, [r[...] for r in c_refs])
        save_ref[0] = st
        y_ref[...] = outs[0].astype(y_ref.dtype)
        if aux_width is not None:
            refs[-2][...] = outs[1]
        st_ref[...] = new

    in_specs = [pl.BlockSpec((CHUNK, w), functools.partial(lambda i, cb: (i, cb), cb=cb)) for _, w, cb in tiled]
    in_specs += [pl.BlockSpec((HALO, tiled[j][1]),
                              functools.partial(lambda i, cb: (jnp.maximum(i * (CHUNK // HALO) - 1, 0), cb), cb=tiled[j][2]))
                 for j in halo_idx]
    in_specs += [pl.BlockSpec(c.shape, functools.partial(lambda i, nd: (0,) * nd, nd=c.ndim)) for c in consts]
    zeros = (0,) * len(state_shape)
    out_specs = [pl.BlockSpec((CHUNK, out_width), lambda i: (i, 0)), pl.BlockSpec((1,) + state_shape, lambda i: (i,) + zeros)]
    out_shape = [jax.ShapeDtypeStruct((seq, out_width), BF), jax.ShapeDtypeStruct((nc,) + state_shape, F32)]
    if aux_width is not None:
        out_specs.append(pl.BlockSpec((CHUNK, aux_width), lambda i: (i, 0)))
        out_shape.append(jax.ShapeDtypeStruct((seq, aux_width), F32))
    return _pc(
        body, name=name, grid=(nc,), in_specs=in_specs, out_specs=tuple(out_specs), out_shape=tuple(out_shape),
        scratch_shapes=[pltpu.VMEM(state_shape, F32)],
        compiler_params=_cparams(("arbitrary",)),
    )(*[t[0] for t in tiled], *[tiled[j][0] for j in halo_idx], *consts)


def _scan_bwd(name, fn, tiled, halo_idx, consts, saved, dy, dtile_dtypes, extra=None):
    seq = tiled[0][0].shape[0]
    nc = seq // CHUNK
    n_t, n_h, n_c = len(tiled), len(halo_idx), len(consts)
    state_shape = saved.shape[1:]
    n_x = 0 if extra is None else 1
    keep = [j for j, dtd in enumerate(dtile_dtypes) if dtd is not None]
    n_k = len(keep)

    def body(*refs):
        t_refs, h_refs, c_refs = refs[:n_t], refs[n_t:n_t + n_h], refs[n_t + n_h:n_t + n_h + n_c]
        pos = n_t + n_h + n_c
        save_ref, dy_ref = refs[pos], refs[pos + 1]
        x_refs = refs[pos + 2:pos + 2 + n_x]
        pos += 2 + n_x
        dt_refs, dc_refs = refs[pos:pos + n_k], refs[pos + n_k:pos + n_k + n_c]
        dst_ref = refs[pos + n_k + n_c]
        carry_refs = refs[pos + n_k + n_c + 1:]
        i = pl.program_id(0)

        @pl.when(i == 0)
        def _():
            dst_ref[...] = jnp.zeros_like(dst_ref)
            for r in carry_refs:
                r[...] = jnp.zeros_like(r)
            for r in dc_refs:
                r[...] = jnp.zeros_like(r)

        flag = jnp.where(i < nc - 1, 1.0, 0.0).astype(F32)
        tiles = [r[...] for r in t_refs]
        halos = [r[...] * flag for r in h_refs]
        cvals = [r[...] for r in c_refs]
        _, vjp = jax.vjp(fn, tiles, halos, save_ref[0], cvals)
        d_tiles, d_halos, d_state, d_consts = vjp(((dy_ref[...].astype(F32),), dst_ref[...]))
        dst_ref[...] = d_state
        for r, g in zip(dc_refs, d_consts):
            r[...] += g
        for r, j in zip(dt_refs, keep):
            g = d_tiles[j]
            if extra is not None and extra[0] == j:
                g = g + x_refs[0][...].astype(F32)
            r[...] = g.astype(r.dtype)
            if j in halo_idx:
                cr = carry_refs[halo_idx.index(j)]
                r[CHUNK - HALO:CHUNK, :] = (g[CHUNK - HALO:CHUNK, :] + cr[...]).astype(r.dtype)
                cr[...] = d_halos[halo_idx.index(j)] * flag

    rev = lambda i: nc - 1 - i
    in_specs = [pl.BlockSpec((CHUNK, w), functools.partial(lambda i, cb: (rev(i), cb), cb=cb)) for _, w, cb in tiled]
    in_specs += [pl.BlockSpec((HALO, tiled[j][1]),
                              functools.partial(lambda i, cb: (jnp.maximum(rev(i) * (CHUNK // HALO) - 1, 0), cb), cb=tiled[j][2]))
                 for j in halo_idx]
    in_specs += [pl.BlockSpec(c.shape, functools.partial(lambda i, nd: (0,) * nd, nd=c.ndim)) for c in consts]
    zeros = (0,) * len(state_shape)
    in_specs += [pl.BlockSpec((1,) + state_shape, lambda i: (rev(i),) + zeros),
                 pl.BlockSpec((CHUNK, dy.shape[1]), lambda i: (rev(i), 0))]
    args = [t[0] for t in tiled] + [tiled[j][0] for j in halo_idx] + list(consts) + [saved, dy]
    if extra is not None:
        in_specs.append(pl.BlockSpec((CHUNK, extra[1].shape[1]), lambda i: (rev(i), 0)))
        args.append(extra[1])
    out_specs = [pl.BlockSpec((CHUNK, tiled[j][1]), lambda i: (rev(i), 0)) for j in keep]
    out_specs += [pl.BlockSpec(c.shape, functools.partial(lambda i, nd: (0,) * nd, nd=c.ndim)) for c in consts]
    out_shape = [jax.ShapeDtypeStruct((seq, tiled[j][1]), dtile_dtypes[j]) for j in keep]
    out_shape += [jax.ShapeDtypeStruct(c.shape, F32) for c in consts]
    scratch = [pltpu.VMEM(state_shape, F32)] + [pltpu.VMEM((HALO, tiled[j][1]), F32) for j in halo_idx]
    return _pc(body, name=name, grid=(nc,), in_specs=in_specs, out_specs=tuple(out_specs), out_shape=tuple(out_shape),
               scratch_shapes=scratch, compiler_params=_cparams(("arbitrary",)))(*args)


def _tile_fwd(name, fn, tiled, consts, outs, tm=TOKEN_TILE):
    seq = tiled[0][0].shape[0]
    n_t, n_c = len(tiled), len(consts)

    def body(*refs):
        res = fn(*[r[...] for r in refs[:n_t + n_c]])
        for r, y in zip(refs[n_t + n_c:], res):
            r[...] = y.astype(r.dtype)

    in_specs = [pl.BlockSpec((tm, w), functools.partial(lambda i, cb: (i, cb), cb=cb)) for _, w, cb in tiled]
    in_specs += [pl.BlockSpec(c.shape, functools.partial(lambda i, nd: (0,) * nd, nd=c.ndim)) for c in consts]
    return _pc(body, name=name, grid=(seq // tm,), in_specs=in_specs,
               out_specs=tuple(pl.BlockSpec((tm, w), lambda i: (i, 0)) for w, _ in outs),
               out_shape=tuple(jax.ShapeDtypeStruct((seq, w), dtp) for w, dtp in outs),
               compiler_params=_cparams(("arbitrary",)))(*[t[0] for t in tiled], *consts)


def _tile_bwd(name, fn, tiled, consts, douts, dtile_dtypes, add_to=None, tm=TOKEN_TILE):
    seq = tiled[0][0].shape[0]
    n_t, n_c, n_o = len(tiled), len(consts), len(douts)
    n_x = 0 if add_to is None else 1
    keep = [j for j, dtp in enumerate(dtile_dtypes) if dtp is not None]

    def body(*refs):
        vals = [r[...].astype(F32) for r in refs[:n_t + n_c]]
        pos = n_t + n_c
        g_refs, x_refs = refs[pos:pos + n_o], refs[pos + n_o:pos + n_o + n_x]
        pos += n_o + n_x
        dt_refs, dc_refs = refs[pos:pos + len(keep)], refs[pos + len(keep):]
        i = pl.program_id(0)

        @pl.when(i == 0)
        def _():
            for r in dc_refs:
                r[...] = jnp.zeros_like(r)

        _, vjp = jax.vjp(fn, *vals)
        cts = vjp(tuple(g[...].astype(F32) for g in g_refs))
        for r, j in zip(dt_refs, keep):
            g = cts[j]
            if add_to is not None and add_to[0] == j:
                g = g + x_refs[0][...].astype(F32)
            r[...] = g.astype(r.dtype)
        for r, g in zip(dc_refs, cts[n_t:]):
            r[...] += g

    in_specs = [pl.BlockSpec((tm, w), functools.partial(lambda i, cb: (i, cb), cb=cb)) for _, w, cb in tiled]
    in_specs += [pl.BlockSpec(c.shape, functools.partial(lambda i, nd: (0,) * nd, nd=c.ndim)) for c in consts]
    in_specs += [pl.BlockSpec((tm, g.shape[1]), lambda i: (i, 0)) for g in douts]
    args = [t[0] for t in tiled] + list(consts) + list(douts)
    if add_to is not None:
        in_specs.append(pl.BlockSpec((tm, add_to[1].shape[1]), lambda i: (i, 0)))
        args.append(add_to[1])
    out_specs = [pl.BlockSpec((tm, tiled[j][1]), lambda i: (i, 0)) for j in keep]
    out_specs += [pl.BlockSpec(c.shape, functools.partial(lambda i, nd: (0,) * nd, nd=c.ndim)) for c in consts]
    out_shape = [jax.ShapeDtypeStruct((seq, tiled[j][1]), dtile_dtypes[j]) for j in keep]
    out_shape += [jax.ShapeDtypeStruct(c.shape, F32) for c in consts]
    return _pc(body, name=name, grid=(seq // tm,), in_specs=in_specs, out_specs=tuple(out_specs),
               out_shape=tuple(out_shape), compiler_params=_cparams(("arbitrary",)))(*args)


def _lnmod_fn(x, nw, sc, sh):
    return (_rms(x, nw) * (1.0 + sc) + sh,)


def _gated_fn(o, g):
    return ((1.0 + g) * o,)


def _resid_fn(x, o, g):
    return (x + (1.0 + g) * o,)


def _swiglu_fn(gu):
    return (_silu(gu[:, :FFN_H]) * gu[:, FFN_H:],)


def _merge_fn(yh, ys, yg, logits, wb, b_merge):
    gates = jax.nn.sigmoid(logits + b_merge)
    acc = None
    for n, y in enumerate((yh, ys, yg)):
        t = gates[:, n * D_MODEL:(n + 1) * D_MODEL] * _bdot(y, wb[n])
        acc = t if acc is None else acc + t
    return (acc,)


MM_VMEM_BUDGET = 40 * 1024 * 1024
MM_TILE_CAP = 1024
MM_K_CAP = 4096


def _divisor(n, cap, unit=LANES):
    best = None
    for d in range(unit, min(n, cap) + 1, unit):
        if n % d == 0:
            best = d
    return n if best is None else best


def _mm_tiles(m, n, k, out_bytes):
    tk = k if k <= MM_K_CAP else _divisor(k, 3072)
    tm, tn = _divisor(m, MM_TILE_CAP), _divisor(n, MM_TILE_CAP + MM_TILE_CAP // 2)

    def need(tm_, tn_):
        acc = tm_ * tn_ * 4 if tk < k else 0
        return 2 * 2 * tk * (tm_ + tn_) + acc + 2 * tm_ * tn_ * out_bytes

    while need(tm, tn) > MM_VMEM_BUDGET:
        if tn >= tm and _divisor(n, tn - LANES) < tn:
            tn = _divisor(n, tn - LANES)
        elif _divisor(m, tm - LANES) < tm:
            tm = _divisor(m, tm - LANES)
        else:
            break
    return tm, tn, tk


def _mm(a, b, mode, out_dtype, name):
    if mode == "nn":
        (m, k), n = a.shape, b.shape[1]
    elif mode == "nt":
        (m, k), n = a.shape, b.shape[0]
    else:
        (k, m), n = a.shape, b.shape[1]
    tm, tn, tk = _mm_tiles(m, n, k, jnp.dtype(out_dtype).itemsize)
    nk = k // tk
    dims = {"nn": ((1,), (0,)), "nt": ((1,), (1,)), "tn": ((0,), (0,))}[mode]

    def body_one(a_ref, b_ref, o_ref):
        o_ref[...] = lax.dot_general(a_ref[...], b_ref[...], (dims, ((), ())), preferred_element_type=F32).astype(o_ref.dtype)

    def body_acc(a_ref, b_ref, o_ref, acc_ref):
        kk = pl.program_id(2)

        @pl.when(kk == 0)
        def _():
            acc_ref[...] = jnp.zeros_like(acc_ref)

        acc_ref[...] += lax.dot_general(a_ref[...], b_ref[...], (dims, ((), ())), preferred_element_type=F32)

        @pl.when(kk == nk - 1)
        def _():
            o_ref[...] = acc_ref[...].astype(o_ref.dtype)

    a_spec = pl.BlockSpec((tk, tm), lambda i, j, kk: (kk, i)) if mode == "tn" else pl.BlockSpec((tm, tk), lambda i, j, kk: (i, kk))
    b_spec = pl.BlockSpec((tn, tk), lambda i, j, kk: (j, kk)) if mode == "nt" else pl.BlockSpec((tk, tn), lambda i, j, kk: (kk, j))
    return _pc(body_one if nk == 1 else body_acc, name=name, grid=(m // tm, n // tn, nk), in_specs=[a_spec, b_spec],
               out_specs=pl.BlockSpec((tm, tn), lambda i, j, kk: (i, j)),
               out_shape=jax.ShapeDtypeStruct((m, n), out_dtype),
               scratch_shapes=[] if nk == 1 else [pltpu.VMEM((tm, tn), F32)],
               compiler_params=_cparams(("parallel", "parallel", "arbitrary")))(a.astype(BF), b.astype(BF))


def _final_loss(x, tgt, norm_final, tm=TOKEN_TILE):
    seq = x.shape[0]

    def fn(xv, nf, tv):
        err = jnp.square(_rms(xv, nf) - tv)
        return 0.5 * jnp.sum(jnp.mean(err, axis=-1))

    def body(x_ref, t_ref, nf_ref, loss_ref, dx_ref, dnf_ref):
        i = pl.program_id(0)

        @pl.when(i == 0)
        def _():
            loss_ref[...] = jnp.zeros_like(loss_ref)
            dnf_ref[...] = jnp.zeros_like(dnf_ref)

        val, vjp = jax.vjp(functools.partial(fn, tv=t_ref[...]), x_ref[...], nf_ref[...])
        dx, dnf = vjp(jnp.ones((), F32))
        dx_ref[...] = dx
        dnf_ref[...] += dnf
        loss_ref[...] += jnp.broadcast_to(val, loss_ref.shape)

    return _pc(body, name="final_loss", grid=(seq // tm,),
               in_specs=[pl.BlockSpec((tm, D_MODEL), lambda i: (i, 0)), pl.BlockSpec((tm, D_MODEL), lambda i: (i, 0)),
                         pl.BlockSpec((1, D_MODEL), lambda i: (0, 0))],
               out_specs=(pl.BlockSpec((8, LANES), lambda i: (0, 0)), pl.BlockSpec((tm, D_MODEL), lambda i: (i, 0)),
                          pl.BlockSpec((1, D_MODEL), lambda i: (0, 0))),
               out_shape=(jax.ShapeDtypeStruct((8, LANES), F32), jax.ShapeDtypeStruct((seq, D_MODEL), F32),
                          jax.ShapeDtypeStruct((1, D_MODEL), F32)),
               compiler_params=_cparams(("arbitrary",)))(x, tgt, norm_final)


def _ada_fwd(c_all, w_ada, b_ada_cols):
    n_l, _, cols = w_ada.shape

    def body(c_ref, w_ref, b_ref, o_ref):
        o_ref[0] = jnp.dot(_silu(c_ref[...]), w_ref[0], preferred_element_type=F32, precision=HI) + b_ref[0]

    return _pc(body, name="ada_fwd", grid=(n_l,),
               in_specs=[pl.BlockSpec((8, D_MODEL), lambda l: (0, 0)), pl.BlockSpec((1, D_MODEL, cols), lambda l: (l, 0, 0)),
                         pl.BlockSpec((1, 1, cols), lambda l: (l, 0, 0))],
               out_specs=pl.BlockSpec((1, 8, cols), lambda l: (l, 0, 0)),
               out_shape=jax.ShapeDtypeStruct((n_l, 8, cols), F32),
               compiler_params=_cparams(("arbitrary",)))(c_all, w_ada, b_ada_cols.reshape(n_l, 1, cols))


def _ada_bwd(c_all, dmod_cols):
    n_l, _, cols = dmod_cols.shape

    def body(c_ref, g_ref, o_ref):
        o_ref[0] = lax.dot_general(_silu(c_ref[...]), g_ref[0], (((0,), (0,)), ((), ())), preferred_element_type=F32,
                                   precision=HI)

    return _pc(body, name="ada_bwd", grid=(n_l,),
               in_specs=[pl.BlockSpec((8, D_MODEL), lambda l: (0, 0)), pl.BlockSpec((1, 8, cols), lambda l: (l, 0, 0))],
               out_specs=pl.BlockSpec((1, D_MODEL, cols), lambda l: (l, 0, 0)),
               out_shape=jax.ShapeDtypeStruct((n_l, D_MODEL, cols), F32),
               compiler_params=_cparams(("arbitrary",)))(c_all, dmod_cols)


def _lb_fn(logits):
    e = jnp.exp(logits - jnp.max(logits, axis=0, keepdims=True))
    p = e / jnp.sum(e, axis=0, keepdims=True)
    r = lax.broadcasted_iota(jnp.int32, (DEPTH, 1), 0)
    lb = jnp.zeros_like(p)
    for j in range(1, DEPTH):
        lb = lb + jnp.where(r >= j, p[j:j + 1, :], 0.0)
    return lb


def _lb_fwd(logits):
    def body(l_ref, o_ref):
        o_ref[...] = _lb_fn(l_ref[...])

    return _pc(body, name="lb_fwd", out_shape=jax.ShapeDtypeStruct(logits.shape, F32))(logits)


def _lb_bwd(logits, dlb):
    def body(l_ref, g_ref, o_ref):
        _, vjp = jax.vjp(_lb_fn, l_ref[...])
        o_ref[...] = vjp(g_ref[...])[0]

    return _pc(body, name="lb_bwd", out_shape=jax.ShapeDtypeStruct(logits.shape, F32))(logits, dlb)


def _rows_for(n_rows, n_cols):
    r = 8
    while r * 2 <= n_rows and n_rows % (r * 2) == 0 and r * 2 * n_cols <= 256 * 1024:
        r *= 2
    return r if n_rows % r == 0 else n_rows


def _ew(name, fn, ins, out_dtypes):
    n_rows, n_cols = ins[0].shape
    tr = _rows_for(n_rows, n_cols)
    n_in = len(ins)

    def body(*refs):
        res = fn(*[r[...] for r in refs[:n_in]])
        for r, y in zip(refs[n_in:], res):
            r[...] = y.astype(r.dtype)

    spec = pl.BlockSpec((tr, n_cols), lambda i: (i, 0))
    return _pc(body, name=name, grid=(n_rows // tr,), in_specs=[spec] * n_in, out_specs=tuple([spec] * len(out_dtypes)),
               out_shape=tuple(jax.ShapeDtypeStruct((n_rows, n_cols), d) for d in out_dtypes),
               compiler_params=_cparams(("arbitrary",)))(*ins)


def _adamw_fn(w, g, m, v):
    m = ADAM_B1 * m + (1.0 - ADAM_B1) * g
    v = ADAM_B2 * v + (1.0 - ADAM_B2) * jnp.square(g)
    m_hat = m / (1.0 - ADAM_B1 ** ADAM_STEP)
    v_hat = v / (1.0 - ADAM_B2 ** ADAM_STEP)
    return -ADAM_LR * (m_hat / (jnp.sqrt(v_hat) + ADAM_EPS) + ADAM_WD * w), m, v


def _adamw(name, w, g, m, v):
    shape = w.shape
    two = (-1, shape[-1])
    d, nm, nv = _ew(name, _adamw_fn, [a.reshape(two) for a in (w, g, m, v)], [F32, F32, F32])
    return d.reshape(shape), nm.reshape(shape), nv.reshape(shape)


def _sum_leading(name, a, out_dtype):
    n, n_rows, n_cols = a.shape
    tr = _rows_for(n_rows, n_cols)

    def body(a_ref, o_ref):
        acc = a_ref[0].astype(F32)
        for j in range(1, n):
            acc = acc + a_ref[j].astype(F32)
        o_ref[...] = acc.astype(o_ref.dtype)

    return _pc(body, name=name, grid=(n_rows // tr,), in_specs=[pl.BlockSpec((n, tr, n_cols), lambda i: (0, i, 0))],
               out_specs=pl.BlockSpec((tr, n_cols), lambda i: (i, 0)),
               out_shape=jax.ShapeDtypeStruct((n_rows, n_cols), out_dtype),
               compiler_params=_cparams(("arbitrary",)))(a)


MESH = pl.DeviceIdType.MESH
ANY = pl.BlockSpec(memory_space=pl.ANY)


def _place():
    return lax.axis_index("x"), lax.axis_index("y"), lax.axis_index("c")


def _all_gather_small(name, a):
    m_per, n = a.shape

    def body(x_ref, out_ref, send_sems, recv_sems, local_sem):
        x, y, c = _place()
        me, sibling = (x, y, c), (x, y, 1 - c)
        chips = [(1 - x, y), (x, 1 - y), (1 - x, 1 - y)]

        def rows(px, py, pc):
            return out_ref.at[pl.ds((4 * px + 2 * py + pc) * m_per, m_per), :]

        def copy(k, block, to, src=None):
            return pltpu.make_async_remote_copy(src_ref=rows(*block) if src is None else src, dst_ref=rows(*block),
                                                send_sem=send_sems.at[k], recv_sem=recv_sems.at[k], device_id=to,
                                                device_id_type=MESH)

        mine = pltpu.make_async_copy(x_ref, rows(*me), local_sem)
        mine.start()
        first = [copy(0, me, sibling, src=x_ref)]
        first += [copy(1 + j, me, (*chip, c), src=x_ref) for j, chip in enumerate(chips)]
        for cp in first:
            cp.start()
        passed = [copy(4 + j, (*chip, c), sibling) for j, chip in enumerate(chips)]
        for j, chip in enumerate(chips):
            copy(1 + j, (*chip, c), me).wait_recv()
            passed[j].start()
        copy(0, sibling, me).wait_recv()
        for j, chip in enumerate(chips):
            copy(4 + j, (*chip, 1 - c), me).wait_recv()
        for cp in first + passed:
            cp.wait_send()
        mine.wait()

    out = _pc(body, name=name, out_shape=jax.ShapeDtypeStruct((8 * m_per, n), a.dtype),
              in_specs=[pl.BlockSpec(memory_space=pltpu.VMEM)], out_specs=pl.BlockSpec(memory_space=pltpu.VMEM),
              scratch_shapes=[pltpu.SemaphoreType.DMA((7,)), pltpu.SemaphoreType.DMA((7,)), pltpu.SemaphoreType.DMA],
              compiler_params=pltpu.CompilerParams(vmem_limit_bytes=VMEM_LIMIT))(a)
    return out.reshape(8, m_per, n)


def _chip_gather(name, pack):
    n_l, n_r, n_c = pack.shape
    half = n_r // 2

    def body(p_ref, o_ref, send_sems, recv_sems):
        x, y, c = _place()
        sibling = (x, y, 1 - c)
        chips = [(1 - x, y), (x, 1 - y), (1 - x, 1 - y)]

        def slab(px, py, pc):
            return o_ref.at[2 * px + py, :, pl.ds(pc * half, half), :]

        def copy(k, src, dst, to):
            return pltpu.make_async_remote_copy(src_ref=src, dst_ref=dst, send_sem=send_sems.at[k], recv_sem=recv_sems.at[k],
                                                device_id=to, device_id_type=MESH)

        first = [copy(j, p_ref.at[:, pl.ds(c * half, half), :], slab(x, y, c), (*chip, c)) for j, chip in enumerate(chips)]
        for cp in first:
            cp.start()
        passed = [copy(3 + j, slab(*chip, c), slab(*chip, c), sibling) for j, chip in enumerate(chips)]
        for j, chip in enumerate(chips):
            copy(j, slab(*chip, c), slab(*chip, c), (*chip, c)).wait_recv()
            passed[j].start()
        for j, chip in enumerate(chips):
            copy(3 + j, slab(*chip, 1 - c), slab(*chip, 1 - c), sibling).wait_recv()
        for cp in first + passed:
            cp.wait_send()

    return _pc(body, name=name, out_shape=jax.ShapeDtypeStruct((4, n_l, n_r, n_c), pack.dtype), in_specs=[ANY], out_specs=ANY,
               scratch_shapes=[pltpu.SemaphoreType.DMA((6,)), pltpu.SemaphoreType.DMA((6,))])(pack)


def _pair_swap(name, give):
    def body(g_ref, o_ref, send_sem, recv_sem):
        x, y, c = _place()
        cp = pltpu.make_async_remote_copy(src_ref=g_ref, dst_ref=o_ref, send_sem=send_sem, recv_sem=recv_sem,
                                          device_id=(x, y, 1 - c), device_id_type=MESH)
        cp.start()
        cp.wait()

    return _pc(body, name=name, out_shape=jax.ShapeDtypeStruct(give.shape, give.dtype), in_specs=[ANY], out_specs=ANY,
               scratch_shapes=[pltpu.SemaphoreType.DMA, pltpu.SemaphoreType.DMA])(give)


def _chip_exchange(name, parts):
    def body(p_ref, o_ref, send_sems, recv_sems):
        x, y, c = _place()
        me = 2 * x + y
        chips = [(1 - x, y), (x, 1 - y), (1 - x, 1 - y)]

        def copy(k, src, dst, to):
            return pltpu.make_async_remote_copy(src_ref=src, dst_ref=dst, send_sem=send_sems.at[k], recv_sem=recv_sems.at[k],
                                                device_id=to, device_id_type=MESH)

        sends = [copy(j, p_ref.at[2 * px + py], o_ref.at[me], (px, py, c)) for j, (px, py) in enumerate(chips)]
        for cp in sends:
            cp.start()
        for j, (px, py) in enumerate(chips):
            copy(j, p_ref.at[2 * px + py], o_ref.at[2 * px + py], (px, py, c)).wait_recv()
        for cp in sends:
            cp.wait_send()

    return _pc(body, name=name, out_shape=jax.ShapeDtypeStruct(parts.shape, parts.dtype), in_specs=[ANY], out_specs=ANY,
               scratch_shapes=[pltpu.SemaphoreType.DMA((3,)), pltpu.SemaphoreType.DMA((3,))])(parts)


N_CHIP = 4
BIG = (("w_in", (1024, 2822), 1, (1024, 2822)), ("w_branch", (3, 768, 256), 2, (2304, 256)),
       ("w_out", (256, 1024), 0, (256, 1024)), ("w_ffn_in", (1024, 1408), 1, (1024, 1408)),
       ("w_ffn_out", (704, 1024), 0, (704, 1024)))
G768 = ((0, 3072), (3072, 3840), (7436, 8204))
GXBC, GQKV, GGATE = (3840, 5120), (5132, 7436), (8216, 11288)
GSMALL = ((5120, 5132), (8204, 8210), (8210, 8216))
W768, WXBC, WGATE = 4608, CONV_CH, 3 * D_MODEL
IN_PAD = W768 + WXBC + QKV_W + WGATE + SMALL_W


def _join_shards(slabs, axis, shard_shape):
    n_l = slabs.shape[1]
    parts = [slabs[j].reshape((n_l,) + shard_shape) for j in range(N_CHIP)]
    return jnp.concatenate(parts, axis=axis + 1)


def _split_shards(full, axis, rows_cols):
    n_l = full.shape[0]
    size = full.shape[axis + 1] // N_CHIP
    return jnp.stack([lax.slice_in_dim(full, j * size, (j + 1) * size, axis=axis + 1).reshape((n_l,) + rows_cols)
                      for j in range(N_CHIP)])


def _gather_weights(w, chip, big=BIG):
    out = {}
    for n, shape, ax, rc in big:
        n_l = w[n].shape[0]
        mine = w[n].astype(BF).reshape((n_l,) + rc)
        slabs = lax.dynamic_update_slice(_chip_gather("gather_" + n, mine), mine[None], (chip, 0, 0, 0))
        out[n] = _join_shards(slabs, ax, shape)
    return out


def _pair_stage(full_grads, core, big=BIG):
    out = {}
    for n, _, ax, (rows, cols) in big:
        n_l = full_grads[n].shape[0]
        slabs = _split_shards(full_grads[n], ax, (rows, cols))
        half = rows // 2
        keep = lax.dynamic_slice_in_dim(slabs, core * half, half, axis=2).reshape(-1, cols)
        give = lax.dynamic_slice_in_dim(slabs, (1 - core) * half, half, axis=2).reshape(-1, cols)
        got = _pair_swap("pair_swap_" + n, give)
        (pair_sum,) = _ew("pair_sum_" + n, lambda a, b: (a.astype(F32) + b.astype(F32),), [keep, got], [BF])
        out[n] = pair_sum.reshape(N_CHIP, n_l * half, cols)
    return out


def _own_slab(landed, pair_sum, chip):
    return lax.dynamic_update_slice(landed, lax.dynamic_slice_in_dim(pair_sum, chip, 1, axis=0), (chip, 0, 0))


def _finish_reduce(parts, n_l, core, big=BIG):
    out = {}
    for n, shape, _, (rows, cols) in big:
        half = rows // 2
        mine = _sum_leading("chip_sum_" + n, parts[n], F32).reshape(n_l, half, cols)
        theirs = _pair_swap("pair_share_" + n, mine)
        full = jnp.concatenate([jnp.where(core == 0, mine, theirs), jnp.where(core == 0, theirs, mine)], axis=1)
        out[n] = full.reshape((n_l,) + shape)
    return out


def _reduce_grads(full_grads, chip, core, big=BIG):
    pair_sums = _pair_stage(full_grads, core, big)
    parts = {n: _own_slab(_chip_exchange("chip_exchange_" + n, pair_sums[n]), pair_sums[n], chip) for n, _, _, _ in big}
    return _finish_reduce(parts, full_grads[big[0][0]].shape[0], core, big)


HBM_SPEC = pl.BlockSpec(memory_space=pltpu.HBM)
SEM_SPEC = pl.BlockSpec(memory_space=pltpu.SEMAPHORE)
DATAFLOW = pltpu.SideEffectType.DATAFLOW_SIDE_EFFECTING


def _exchange_copies(p_ref, land_ref, sems, waiting, spread):
    x, y, c = _place()
    me = 2 * x + y
    out = []
    for j, (px, py) in enumerate([(1 - x, y), (x, 1 - y), (1 - x, 1 - y)]):
        out.append(pltpu.make_async_remote_copy(src_ref=p_ref if spread else p_ref.at[2 * px + py],
                                                dst_ref=land_ref.at[2 * px + py if waiting else me],
                                                send_sem=sems[j], recv_sem=sems[3 + j], device_id=(px, py, c),
                                                device_id_type=MESH))
    return out


def _exchange_start(name, parts, after, spread=False):
    land_shape = ((N_CHIP,) + parts.shape) if spread else parts.shape

    def body(p_ref, land_ref, after_ref, s0, s1, s2, r0, r1, r2, p_thru, land_thru, token):
        for cp in _exchange_copies(p_ref, land_ref, (s0, s1, s2, r0, r1, r2), False, spread):
            cp.start()
        token[...] = jnp.zeros_like(token)

    res = _pc(body, name=name,
              out_shape=(pltpu.SemaphoreType.DMA(()),) * 6 + (pltpu.HBM(parts.shape, parts.dtype), pltpu.HBM(land_shape, parts.dtype),
                                                            jax.ShapeDtypeStruct((8, LANES), F32)),
              in_specs=(HBM_SPEC, HBM_SPEC, ANY),
              out_specs=(SEM_SPEC,) * 6 + (HBM_SPEC, HBM_SPEC, pl.BlockSpec(memory_space=pltpu.VMEM)),
              input_output_aliases={0: 6, 1: 7}, compiler_params=pltpu.CompilerParams(has_side_effects=DATAFLOW))(
        pltpu.with_memory_space_constraint(parts, pltpu.HBM),
        pltpu.with_memory_space_constraint(lax.empty(land_shape, parts.dtype), pltpu.HBM), after)
    return res[:6], res[6], res[7], res[8]


def _exchange_wait(name, sems, p_thru, land_thru, after, spread=False):
    def body(p_ref, land_ref, s0, s1, s2, r0, r1, r2, after_ref, p_dead, got_ref):
        for cp in _exchange_copies(p_ref, land_ref, (s0, s1, s2, r0, r1, r2), True, spread):
            cp.wait_send()
            cp.wait_recv()

    return _pc(body, name=name, out_shape=(pltpu.HBM(p_thru.shape, p_thru.dtype), pltpu.HBM(land_thru.shape, land_thru.dtype)),
               in_specs=(HBM_SPEC, HBM_SPEC) + (SEM_SPEC,) * 6 + (ANY,), out_specs=(HBM_SPEC, HBM_SPEC),
               input_output_aliases={0: 0, 1: 1}, compiler_params=pltpu.CompilerParams(has_side_effects=DATAFLOW))(
        p_thru, land_thru, *sems, after)[1]


def _regroup_w_in(w):
    cat = lambda spans: jnp.concatenate([w[:, a:b] for a, b in spans], axis=1)
    small = jnp.concatenate([cat(GSMALL), jnp.zeros((w.shape[0], SMALL_W - 24), w.dtype)], axis=1)
    return cat(G768), cat((GXBC,)), cat((GQKV,)), cat((GGATE,)), small


def _ungroup_w_in(d):
    o_xbc, o_qkv, o_gate, o_small = W768, W768 + WXBC, W768 + WXBC + QKV_W, W768 + WXBC + QKV_W + WGATE
    spans = ((0, 3072), (3072, 3840), (o_xbc, o_xbc + WXBC), (o_small, o_small + 12), (o_qkv, o_qkv + QKV_W),
             (3840, 4608), (o_small + 12, o_small + 18), (o_small + 18, o_small + 24), (o_gate, o_gate + WGATE))
    return jnp.concatenate([d[:, a:b] for a, b in spans], axis=1)


def _lane_pad(v, off):
    return jnp.pad(v, (off, LANES - off - v.shape[0]))[None, :]


STATE6 = (N_HEAD6, HEAD, HEAD)


def _mixer_inputs(sv, lp):
    p768, pxbc, pqkv, psmall = sv["p768"], sv["pxbc"], sv["pqkv"], sv["psmall"]
    hgrn = ([(p768, MIX_W, j) for j in range(4)], [], [lp["lb"], lp["hgrn_norm"]])
    ssd = ([(p768, MIX_W, 4), (pxbc, CONV_CH, 0), (psmall, LANES, 0)], [1],
           [lp["ssm_conv_w"], lp["ssm_conv_b"], lp["ssm_dt_bias"], lp["ssm_a_log"], lp["ssm_d"], lp["ssm_norm"]])
    gdn = ([(pqkv, QKV_W, 0), (p768, MIX_W, 5), (psmall, LANES, 0)], [0],
           [lp["gdn_conv_w"], lp["gdn_dt_bias"], lp["gdn_a_log"], lp["gdn_norm"]])
    return hgrn, ssd, gdn


def _layer_fwd(x, md, lw, lp):
    sv = {"x": x}
    (sv["h1"],) = _tile_fwd("lnmod1", _lnmod_fn, [(x, D_MODEL, 0)], [lp["norm_mix"], md["sc1"], md["sh1"]], [(D_MODEL, BF)])
    for nm in ("768", "xbc", "qkv", "gate", "small"):
        sv["p" + nm] = _mm(sv["h1"], lw["win_" + nm], "nn", F32, "proj_" + nm)
    hgrn, ssd, gdn = _mixer_inputs(sv, lp)
    sv["y_h"], sv["st_h"] = _scan_fwd("hgrn_fwd", _hgrn_chunk, *hgrn, MIX_W, STATE6)
    sv["y_s"], sv["st_s"] = _scan_fwd("ssd_fwd", _ssd_chunk, *ssd, MIX_W, STATE6)
    sv["y_g"], sv["st_g"], sv["inv_g"] = _scan_fwd("gdn_fwd", _gdn_chunk, *gdn, MIX_W, STATE6, aux_width=N_HEAD6 * LANES)
    (sv["merged"],) = _tile_fwd("merge", _merge_fn, _merge_tiles(sv), [lw["w_branch"], lp["b_merge"]], [(D_MODEL, BF)])
    sv["out"] = _mm(sv["merged"], lw["w_out"], "nn", F32, "out_proj")
    (sv["x_mid"],) = _tile_fwd("resid1", _resid_fn, [(x, D_MODEL, 0), (sv["out"], D_MODEL, 0)], [md["g1"]], [(D_MODEL, F32)])
    (sv["h2"],) = _tile_fwd("lnmod2", _lnmod_fn, [(sv["x_mid"], D_MODEL, 0)], [lp["norm_ffn"], md["sc2"], md["sh2"]],
                            [(D_MODEL, BF)])
    sv["gu"] = _mm(sv["h2"], lw["w_ffn_in"], "nn", F32, "ffn_in")
    (sv["act"],) = _tile_fwd("swiglu", _swiglu_fn, [(sv["gu"], 2 * FFN_H, 0)], [], [(FFN_H, BF)])
    sv["o2"] = _mm(sv["act"], lw["w_ffn_out"], "nn", F32, "ffn_out")
    (x_out,) = _tile_fwd("resid2", _resid_fn, [(sv["x_mid"], D_MODEL, 0), (sv["o2"], D_MODEL, 0)], [md["g2"]], [(D_MODEL, F32)])
    return x_out, sv


def _merge_tiles(sv):
    return [(sv["y_h"], MIX_W, 0), (sv["y_s"], MIX_W, 0), (sv["y_g"], MIX_W, 0), (sv["pgate"], WGATE, 0)]


def _layer_bwd(dx_out, sv, md, lw, lp):
    g = {}
    x, x_mid = sv["x"], sv["x_mid"]
    d_o2, g["g2"] = _tile_bwd("resid2_b", _gated_fn, [(sv["o2"], D_MODEL, 0)], [md["g2"]], [dx_out], [BF])
    d_xmid = dx_out
    d_act =_mm(d_o2, lw["w_ffn_out"], "nt", F32, "ffn_out_dx")
    g["w_ffn_out"] = _mm(sv["act"].T, d_o2, "nn", BF, "ffn_out_dw")
    (d_gu,) = _tile_bwd("swiglu_b", _swiglu_fn, [(sv["gu"], 2 * FFN_H, 0)], [], [d_act], [BF])
    d_h2 = _mm(d_gu, lw["w_ffn_in"], "nt", F32, "ffn_in_dx")
    g["w_ffn_in"] = _mm(sv["h2"].T, d_gu, "nn", BF, "ffn_in_dw")
    d_xmid, g["norm_ffn"], g["sc2"], g["sh2"] = _tile_bwd(
        "lnmod2_b", _lnmod_fn, [(x_mid, D_MODEL, 0)], [lp["norm_ffn"], md["sc2"], md["sh2"]], [d_h2], [F32], add_to=(0, d_xmid))
    d_out, g["g1"] = _tile_bwd("resid1_b", _gated_fn, [(sv["out"], D_MODEL, 0)], [md["g1"]], [d_xmid], [BF])
    d_x = d_xmid
    d_merged = _mm(d_out, lw["w_out"], "nt", F32, "out_proj_dx")
    g["w_out"] = _mm(sv["merged"].T, d_out, "nn", BF, "out_proj_dw")
    d_yh, d_ys, d_yg, d_gate, g["w_branch"], g["b_merge"] = _tile_bwd(
        "merge_b", _merge_fn, _merge_tiles(sv), [lw["w_branch"], lp["b_merge"]], [d_merged], [F32, F32, F32, BF])
    hgrn, ssd, gdn = _mixer_inputs(sv, lp)
    d_q, d_f, d_v, d_g, g["lb"], g["hgrn_norm"] = _scan_bwd("hgrn_bwd", _hgrn_chunk, *hgrn, sv["st_h"], d_yh, [BF] * 4)
    (d_sz, d_xbc, d_small, g["ssm_conv_w"], g["ssm_conv_b"], g["ssm_dt_bias"], g["ssm_a_log"], g["ssm_d"],
     g["ssm_norm"]) = _scan_bwd("ssd_bwd", _ssd_chunk, *ssd, sv["st_s"], d_ys, [BF, BF, F32])
    d_qkv, d_gz, d_small, g["gdn_conv_w"], g["gdn_dt_bias"], g["gdn_a_log"], g["gdn_norm"] = _scan_bwd(
        "gdn_bwd", _gdn_chunk, gdn[0] + [(sv["inv_g"], N_HEAD6 * LANES, 0)], gdn[1], gdn[2], sv["st_g"], d_yg,
        [BF, BF, BF, None], extra=(2, d_small))
    d_proj = jnp.concatenate([d_q, d_f, d_v, d_g, d_sz, d_gz, d_xbc, d_qkv, d_gate, d_small,
                              jnp.zeros((x.shape[0], SMALL_W - LANES), BF)], axis=1)
    d_h1 = _mm(d_proj, lw["win_all"], "nt", F32, "proj_dx")
    g["w_in"] = _ungroup_w_in(_mm(sv["h1"].T, d_proj, "nn", BF, "proj_dw"))
    d_x, g["norm_mix"], g["sc1"], g["sh1"] = _tile_bwd(
        "lnmod1_b", _lnmod_fn, [(x, D_MODEL, 0)], [lp["norm_mix"], md["sc1"], md["sh1"]], [d_h1], [F32], add_to=(0, d_x))
    return d_x, g


SMALL_REPL = ("norm_mix", "norm_ffn", "b_merge", "hgrn_lb_logits", "hgrn_norm", "ssm_conv_w", "ssm_conv_b", "ssm_dt_bias",
              "ssm_a_log", "ssm_d", "ssm_norm", "gdn_conv_w", "gdn_dt_bias", "gdn_a_log", "gdn_norm", "norm_final")
WEIGHTS = ("w_ada", "b_ada", "norm_mix", "norm_ffn", "w_in", "b_merge", "hgrn_lb_logits", "hgrn_norm", "ssm_conv_w",
           "ssm_conv_b", "ssm_dt_bias", "ssm_a_log", "ssm_d", "ssm_norm", "gdn_conv_w", "gdn_dt_bias", "gdn_a_log",
           "gdn_norm", "w_branch", "w_out", "w_ffn_in", "w_ffn_out", "norm_final")
SMALL_ROWS = 120


def _pad_rows(flat, n_rows, n_cols):
    return jnp.concatenate([flat, jnp.zeros((n_rows * n_cols - flat.shape[0],), flat.dtype)]).reshape(n_rows, n_cols)


def _device_step(x, tgt, mod, lb, wfull, sp, chip=None, core=None, order_after=None):
    mds, lps, svs = [], [], []
    h = x
    weights_of = wfull if callable(wfull) else (lambda layer, after: wfull[layer])
    wfull = []
    for l in range(DEPTH):
        wfull.append(weights_of(l, h))
        md = {n: mod[l, i * D_MODEL:(i + 1) * D_MODEL][None, :] for i, n in enumerate(("sh1", "sc1", "g1", "sh2", "sc2", "g2"))}
        if l == 0 and order_after is not None:
            md["sc1"] = md["sc1"] + order_after
        lp = {n: sp[n][l][None, :] for n in ("norm_mix", "norm_ffn", "b_merge", "hgrn_norm", "ssm_conv_b", "ssm_norm", "gdn_norm")}
        lp["lb"] = lb[l][None, :]
        lp["ssm_conv_w"], lp["gdn_conv_w"] = sp["ssm_conv_w"][l], sp["gdn_conv_w"][l]
        for n in ("ssm_dt_bias", "ssm_a_log", "ssm_d"):
            lp[n] = _lane_pad(sp[n][l], DT_OFF)
        for n in ("gdn_dt_bias", "gdn_a_log"):
            lp[n] = _lane_pad(sp[n][l], GA_OFF)
        h, sv = _layer_fwd(h, md, wfull[l], lp)
        mds.append(md), lps.append(lp), svs.append(sv)
    loss, dh, d_nf = _final_loss(h, tgt, sp["norm_final"][None, :])
    grads = [None] * DEPTH
    if core is None:
        for l in reversed(range(DEPTH)):
            dh, grads[l] = _layer_bwd(dh, svs[l], mds[l], wfull[l], lps[l])
        return loss, dh, d_nf, grads
    names = [n for n, _, _, _ in BIG]
    landed, flying = [None] * DEPTH, None
    for l in reversed(range(DEPTH)):
        md = mds[l]
        if flying is not None:
            md = dict(md, g2=md["g2"] + sum(tok[0, 0] for _, _, _, tok in flying.values()))
        dh, grads[l] = _layer_bwd(dh, svs[l], md, wfull[l], lps[l])
        if flying is not None:
            landed[l + 1] = {n: _own_slab(_exchange_wait(f"exchange_wait_{n}_{l + 1}", *flying[n][:3], dh), sums[n], chip)
                             for n in names}
        sums = _pair_stage({n: grads[l][n].astype(BF)[None] for n in names}, core)
        if l > 0:
            flying = {n: _exchange_start(f"exchange_start_{n}_{l}", sums[n], dh) for n in names}
        else:
            landed[0] = {n: _own_slab(_chip_exchange("chip_exchange_" + n, sums[n]), sums[n], chip) for n in names}
    parts = {n: jnp.concatenate([landed[l][n] for l in range(DEPTH)], axis=1) for n in names}
    return loss, dh, d_nf, grads, _finish_reduce(parts, DEPTH, core)


def kernel(x, c, w_ada, b_ada, norm_mix, norm_ffn, w_in, b_merge, hgrn_lb_logits, hgrn_norm, ssm_conv_w, ssm_conv_b, ssm_dt_bias, ssm_a_log, ssm_d, ssm_norm, gdn_conv_w, gdn_dt_bias, gdn_a_log, gdn_norm, w_branch, w_out, w_ffn_in, w_ffn_out, norm_final, loss_target, m_w_ada, m_b_ada, m_norm_mix, m_norm_ffn, m_w_in, m_b_merge, m_hgrn_lb_logits, m_hgrn_norm, m_ssm_conv_w, m_ssm_conv_b, m_ssm_dt_bias, m_ssm_a_log, m_ssm_d, m_ssm_norm, m_gdn_conv_w, m_gdn_dt_bias, m_gdn_a_log, m_gdn_norm, m_w_branch, m_w_out, m_w_ffn_in, m_w_ffn_out, m_norm_final, v_w_ada, v_b_ada, v_norm_mix, v_norm_ffn, v_w_in, v_b_merge, v_hgrn_lb_logits, v_hgrn_norm, v_ssm_conv_w, v_ssm_conv_b, v_ssm_dt_bias, v_ssm_a_log, v_ssm_d, v_ssm_norm, v_gdn_conv_w, v_gdn_dt_bias, v_gdn_a_log, v_gdn_norm, v_w_branch, v_w_out, v_w_ffn_in, v_w_ffn_out, v_norm_final):
    w = dict(w_ada=w_ada, b_ada=b_ada, norm_mix=norm_mix, norm_ffn=norm_ffn, w_in=w_in, b_merge=b_merge,
             hgrn_lb_logits=hgrn_lb_logits, hgrn_norm=hgrn_norm, ssm_conv_w=ssm_conv_w, ssm_conv_b=ssm_conv_b,
             ssm_dt_bias=ssm_dt_bias, ssm_a_log=ssm_a_log, ssm_d=ssm_d, ssm_norm=ssm_norm, gdn_conv_w=gdn_conv_w,
             gdn_dt_bias=gdn_dt_bias, gdn_a_log=gdn_a_log, gdn_norm=gdn_norm, w_branch=w_branch, w_out=w_out,
             w_ffn_in=w_ffn_in, w_ffn_out=w_ffn_out, norm_final=norm_final)
    m = dict(w_ada=m_w_ada, b_ada=m_b_ada, norm_mix=m_norm_mix, norm_ffn=m_norm_ffn, w_in=m_w_in, b_merge=m_b_merge,
             hgrn_lb_logits=m_hgrn_lb_logits, hgrn_norm=m_hgrn_norm, ssm_conv_w=m_ssm_conv_w, ssm_conv_b=m_ssm_conv_b,
             ssm_dt_bias=m_ssm_dt_bias, ssm_a_log=m_ssm_a_log, ssm_d=m_ssm_d, ssm_norm=m_ssm_norm, gdn_conv_w=m_gdn_conv_w,
             gdn_dt_bias=m_gdn_dt_bias, gdn_a_log=m_gdn_a_log, gdn_norm=m_gdn_norm, w_branch=m_w_branch, w_out=m_w_out,
             w_ffn_in=m_w_ffn_in, w_ffn_out=m_w_ffn_out, norm_final=m_norm_final)
    v = dict(w_ada=v_w_ada, b_ada=v_b_ada, norm_mix=v_norm_mix, norm_ffn=v_norm_ffn, w_in=v_w_in, b_merge=v_b_merge,
             hgrn_lb_logits=v_hgrn_lb_logits, hgrn_norm=v_hgrn_norm, ssm_conv_w=v_ssm_conv_w, ssm_conv_b=v_ssm_conv_b,
             ssm_dt_bias=v_ssm_dt_bias, ssm_a_log=v_ssm_a_log, ssm_d=v_ssm_d, ssm_norm=v_ssm_norm, gdn_conv_w=v_gdn_conv_w,
             gdn_dt_bias=v_gdn_dt_bias, gdn_a_log=v_gdn_a_log, gdn_norm=v_gdn_norm, w_branch=v_w_branch, w_out=v_w_out,
             w_ffn_in=v_w_ffn_in, w_ffn_out=v_w_ffn_out, norm_final=v_norm_final)
    xi, yi, ci = _place()
    chip, me = 2 * xi + yi, 4 * xi + 2 * yi + ci
    seq = x.shape[1]

    conv_flat = jnp.concatenate([ssm_conv_w.reshape(-1), gdn_conv_w.reshape(-1)])
    n_conv = conv_flat.shape[0]
    first = _all_gather_small("gather_c_conv", _pad_rows(jnp.concatenate([c[0], conv_flat]), 16, D_MODEL))
    c_all = first[:, 0, :]
    conv_all = first[0::2].reshape(N_CHIP, -1)[:, D_MODEL:D_MODEL + n_conv]
    n_ssm = ssm_conv_w.size
    sp = dict(w)
    sp["ssm_conv_w"] = jnp.concatenate([conv_all[j, :n_ssm].reshape(ssm_conv_w.shape) for j in range(N_CHIP)], axis=2)
    sp["gdn_conv_w"] = jnp.concatenate([conv_all[j, n_ssm:].reshape(gdn_conv_w.shape) for j in range(N_CHIP)], axis=2)

    ada_cols = w_ada.shape[2]
    mod_part = _ada_fwd(c_all, w_ada, lax.dynamic_slice_in_dim(b_ada, chip * ada_cols, ada_cols, axis=1))
    mod_all = _all_gather_small("gather_mod", mod_part.reshape(DEPTH * 8, ada_cols))[0::2].reshape(N_CHIP, DEPTH, 8, ada_cols)
    mod = lax.dynamic_index_in_dim(mod_all, me, axis=2, keepdims=False).transpose(1, 0, 2).reshape(DEPTH, N_CHIP * ada_cols)
    lb = _lb_fwd(hgrn_lb_logits)

    names = [n for n, _, _, _ in BIG]

    def layer_weights(full):
        full = dict(full)
        for nm, part in zip(("768", "xbc", "qkv", "gate", "small"), _regroup_w_in(full["w_in"])):
            full["win_" + nm] = part
        full["win_all"] = jnp.concatenate([full["win_" + nm] for nm in ("768", "xbc", "qkv", "gate", "small")], axis=1)
        return full

    first = _gather_weights({n: w[n][0:1] for n in names}, chip)
    first = layer_weights({n: first[n][0] for n in names})
    started = {}
    settled = mod[0:1, 0:LANES]
    for n in names:
        settled = settled + first[n].reshape(-1, first[n].shape[-1])[0:1, 0:LANES].astype(F32)
    for n, _, _, (rows, cols) in BIG:
        mine = w[n][1:].astype(BF).reshape((DEPTH - 1, rows, cols))
        started[n] = _exchange_start("gather_start_" + n, lax.dynamic_slice_in_dim(mine, ci * (rows // 2), rows // 2, axis=1),
                                     settled, spread=True)
    rest = []

    def weights_of(layer, after):
        if layer == 0:
            return first
        if not rest:
            full = {}
            for n, shape, ax, _ in BIG:
                sems, my_half, landing, _ = started[n]
                landed = _exchange_wait("gather_wait_" + n, sems, my_half, landing, after, spread=True)
                landed = lax.dynamic_update_slice(landed, my_half[None], (chip, 0, 0, 0))
                theirs = _pair_swap("gather_share_" + n, landed)
                slabs = jnp.concatenate([jnp.where(ci == 0, landed, theirs), jnp.where(ci == 0, theirs, landed)], axis=2)
                full[n] = _join_shards(slabs, ax, shape)
            rest.extend(layer_weights({n: full[n][i] for n in names}) for i in range(DEPTH - 1))
        return rest[layer - 1]

    order = sum(tok[0, 0] for _, _, _, tok in started.values())
    loss8, d_x, d_nf, lg, grad = _device_step(x[0], loss_target[0], mod, lb, weights_of, sp, chip, ci, order)

    dmod = jnp.stack([jnp.concatenate([lg[l][n] for n in ("sh1", "sc1", "g1", "sh2", "sc2", "g2")], axis=1)[0] for l in range(DEPTH)])
    d_lb = jnp.stack([lg[l]["lb"][0] for l in range(DEPTH)])
    contrib = {
        "norm_mix": jnp.stack([lg[l]["norm_mix"][0] for l in range(DEPTH)]),
        "norm_ffn": jnp.stack([lg[l]["norm_ffn"][0] for l in range(DEPTH)]),
        "b_merge": jnp.stack([lg[l]["b_merge"][0] for l in range(DEPTH)]),
        "hgrn_lb_logits": _lb_bwd(hgrn_lb_logits, d_lb),
        "hgrn_norm": jnp.stack([lg[l]["hgrn_norm"][0] for l in range(DEPTH)]),
        "ssm_conv_w": jnp.stack([lg[l]["ssm_conv_w"] for l in range(DEPTH)]),
        "ssm_conv_b": jnp.stack([lg[l]["ssm_conv_b"][0] for l in range(DEPTH)]),
        "ssm_dt_bias": jnp.stack([lg[l]["ssm_dt_bias"][0, DT_OFF:DT_OFF + 12] for l in range(DEPTH)]),
        "ssm_a_log": jnp.stack([lg[l]["ssm_a_log"][0, DT_OFF:DT_OFF + 12] for l in range(DEPTH)]),
        "ssm_d": jnp.stack([lg[l]["ssm_d"][0, DT_OFF:DT_OFF + 12] for l in range(DEPTH)]),
        "ssm_norm": jnp.stack([lg[l]["ssm_norm"][0] for l in range(DEPTH)]),
        "gdn_conv_w": jnp.stack([lg[l]["gdn_conv_w"] for l in range(DEPTH)]),
        "gdn_dt_bias": jnp.stack([lg[l]["gdn_dt_bias"][0, GA_OFF:GA_OFF + 6] for l in range(DEPTH)]),
        "gdn_a_log": jnp.stack([lg[l]["gdn_a_log"][0, GA_OFF:GA_OFF + 6] for l in range(DEPTH)]),
        "gdn_norm": jnp.stack([lg[l]["gdn_norm"][0] for l in range(DEPTH)]),
        "norm_final": d_nf[0],
    }
    flat = jnp.concatenate([dmod.reshape(-1)] + [contrib[n].reshape(-1) for n in SMALL_REPL] + [loss8[0, 0:1]])
    small_all = _all_gather_small("gather_small_grads", _pad_rows(flat, SMALL_ROWS, D_MODEL))
    total = _sum_leading("small_grad_sum", small_all, F32).reshape(-1)
    n_mod = dmod.size
    grad["b_ada"] = total[:n_mod].reshape(b_ada.shape)
    off = n_mod
    full_small = {}
    for n in SMALL_REPL:
        full_small[n] = total[off:off + contrib[n].size].reshape(contrib[n].shape)
        off += contrib[n].size
    loss = total[off]
    for n in SMALL_REPL:
        if n in ("ssm_conv_w", "gdn_conv_w"):
            cols = w[n].shape[2]
            grad[n] = lax.dynamic_slice_in_dim(full_small[n], chip * cols, cols, axis=2)
        else:
            grad[n] = full_small[n]
    dmod_cols = lax.dynamic_slice_in_dim(small_all[:, :n_mod // D_MODEL, :].reshape(8, DEPTH, -1), chip * ada_cols, ada_cols, axis=2)
    grad["w_ada"] = _ada_bwd(c_all, dmod_cols.transpose(1, 0, 2))

    delta, new_m, new_v = {}, {}, {}
    big_names = ("w_ada",) + tuple(n for n, _, _, _ in BIG)
    for n in big_names:
        delta[n], new_m[n], new_v[n] = _adamw("adamw_" + n, w[n], grad[n], m[n], v[n])
    small_names = [n for n in WEIGHTS if n not in big_names]
    packs = [_pad_rows(jnp.concatenate([d[n].reshape(-1) for n in small_names]), 584, LANES) for d in (w, grad, m, v)]
    outs = _ew("adamw_small", _adamw_fn, packs, [F32, F32, F32])
    off = 0
    for n in small_names:
        for dst, o in zip((delta, new_m, new_v), outs):
            dst[n] = o.reshape(-1)[off:off + w[n].size].reshape(w[n].shape)
        off += w[n].size
    return (loss, d_x[None], *[grad[n] for n in WEIGHTS], *[delta[n] for n in WEIGHTS], *[new_m[n] for n in WEIGHTS],
            *[new_v[n] for n in WEIGHTS])
```

```python
import functools

import jax
import jax.numpy as jnp
from jax import lax
from jax.experimental import pallas as pl
from jax.experimental.pallas import tpu as pltpu

F32 = jnp.float32
BF = jnp.bfloat16
HI = lax.Precision.HIGHEST

D_MODEL = 1024
DEPTH = 4
CHUNK = 64
MIX_W = 768
HEAD = 128
N_HEAD6 = 6
SSM_P = 64
SSM_N = 128
CONV_CH = 1280
QKV_W = 2304
FFN_H = 2816
IN_WIDTH = 11288
NORM_EPS = 1e-6
F_MIN = 1e-30
HALO = 8
HEAD_GROUP = 6
HGRN_SUB = 8
SMALL_W = 512
LANES = 128
DT_OFF, GB_OFF, GA_OFF = 0, 12, 18

ADAM_LR, ADAM_B1, ADAM_B2, ADAM_EPS, ADAM_WD, ADAM_STEP = 0.001, 0.9, 0.999, 1e-08, 0.01, 10

VMEM_LIMIT = 56 * 1024 * 1024
TOKEN_TILE = 512
WIDE_TOKEN_TILE = 256


def _pc(body, **kw):
    return pl.pallas_call(body, **kw)


def _cparams(sem):
    return pltpu.CompilerParams(dimension_semantics=sem, vmem_limit_bytes=VMEM_LIMIT)


def _bdot(a, b):
    return jnp.dot(a.astype(BF), b.astype(BF), preferred_element_type=F32)


def _bdot_nt(a, b):
    return lax.dot_general(a.astype(BF), b.astype(BF), (((1,), (1,)), ((), ())), preferred_element_type=F32)


def _bdot_tn(a, b):
    return lax.dot_general(a.astype(BF), b.astype(BF), (((0,), (0,)), ((), ())), preferred_element_type=F32)


def _silu(x):
    return x * jax.nn.sigmoid(x)


def _tri_mask(n, strict=False):
    t = lax.broadcasted_iota(jnp.int32, (n, n), 0)
    s = lax.broadcasted_iota(jnp.int32, (n, n), 1)
    return (s < t) if strict else (s <= t)


def _masked_exp(diff, mask):
    return jnp.where(mask, jnp.exp(jnp.where(mask, diff, 0.0)), 0.0)


def _split_bf16(x, n):
    parts, rest = [], x
    for _ in range(n):
        p = rest.astype(BF)
        parts.append(p)
        rest = rest - p.astype(F32)
    return parts


def _tri_sum(x, reverse):
    n, w = x.shape
    t = lax.broadcasted_iota(jnp.int32, (n, n), 0)
    s = lax.broadcasted_iota(jnp.int32, (n, n), 1)
    tri = jnp.where((s >= t) if reverse else (s <= t), 1.0, 0.0).astype(BF)
    y = jnp.dot(tri, jnp.concatenate(_split_bf16(x, 3), axis=1), preferred_element_type=F32)
    return y[:, :w] + y[:, w:2 * w] + y[:, 2 * w:]


@jax.custom_vjp
def _cumsum_rows(x):
    return _tri_sum(x, False)


_cumsum_rows.defvjp(lambda x: (_tri_sum(x, False), None), lambda _, g: (_tri_sum(g, True),))


def _dot_split(a, b, transpose_a=False):
    dims = (((0,), (0,)) if transpose_a else ((1,), (0,)), ((), ()))
    a_hi, a_lo = _split_bf16(a, 2)
    b_hi, b_lo = _split_bf16(b, 2)
    w = b.shape[1]
    y = lax.dot_general(a_hi, jnp.concatenate([b_hi, b_lo], axis=1), dims, preferred_element_type=F32)
    return y[:, :w] + y[:, w:] + lax.dot_general(a_lo, b_hi, dims, preferred_element_type=F32)


def _rms(x, w):
    return x * lax.rsqrt(jnp.mean(x * x, axis=-1, keepdims=True) + NORM_EPS) * w


def _causal_conv(halo, x, w):
    ext = jnp.concatenate([halo, x], axis=0)
    n = x.shape[0]
    acc = w[0:1, :] * ext[HALO - 3:HALO - 3 + n, :]
    for i in range(1, 4):
        acc = acc + w[i:i + 1, :] * ext[HALO - 3 + i:HALO - 3 + i + n, :]
    return acc


def _unit_lower_inverses(mats):
    n = mats[0].shape[0]
    t = lax.broadcasted_iota(jnp.int32, (n, LANES), 0)
    s_ = lax.broadcasted_iota(jnp.int32, (n, LANES), 1)
    xs = [jnp.where(t == s_, 1.0, 0.0).astype(F32) for _ in mats]
    for s in range(n - 1):
        r0 = 8 * ((s + 1) // 8)
        for i, a in enumerate(mats):
            x = xs[i]
            low = x[r0:] - a[r0:, s:s + 1] * x[s:s + 1, :]
            xs[i] = low if r0 == 0 else jnp.concatenate([x[:r0], low], axis=0)
    return xs


@jax.custom_vjp
def _solves_with_inverses(invs, mats, rhss):
    return [_dot_split(inv, r) for inv, r in zip(invs, rhss)]


def _swi_fwd(invs, mats, rhss):
    xs = [_dot_split(inv, r) for inv, r in zip(invs, rhss)]
    return xs, (invs, xs)


def _swi_bwd(res, gs):
    invs, xs = res
    ys = [_dot_split(inv, g, transpose_a=True) for inv, g in zip(invs, gs)]
    das = [jnp.where(_tri_mask(CHUNK, strict=True), -_bdot_nt(y, x), 0.0) for y, x in zip(ys, xs)]
    return [jnp.zeros_like(inv) for inv in invs], das, ys


_solves_with_inverses.defvjp(_swi_fwd, _swi_bwd)


def _hgrn_chunk(tiles, halos, state, consts):
    q_raw, f_raw, v_all, g_raw = tiles
    lb, norm_w = consts
    q_all = _silu(q_raw)
    f = lb + (1.0 - lb) * jax.nn.sigmoid(f_raw)
    logf = jnp.log(jnp.maximum(f, F_MIN))
    k_all = (1.0 - lb) * jax.nn.sigmoid(-f_raw)
    b_all = _cumsum_rows(logf)
    sub = HGRN_SUB
    row = lax.broadcasted_iota(jnp.int32, (sub, 1), 0)
    src_row = lax.broadcasted_iota(jnp.int32, (CHUNK, 1), 0)
    src_lane = lax.broadcasted_iota(jnp.int32, (1, CHUNK), 1)
    heads = range(N_HEAD6)
    n_sub = CHUNK // sub
    cols = [slice(h * HEAD, (h + 1) * HEAD) for h in heads]
    qs, ks, vs, bs = ([a[:, sl] for sl in cols] for a in (q_all, k_all, v_all, b_all))
    o_inter = [_bdot_nt(qs[h] * jnp.exp(bs[h]), state[h]) for h in heads]
    blocks = [[None] * n_sub for _ in heads]
    for i in range(n_sub):
        r0 = i * sub
        for h in heads:
            if i > 0:
                ref = bs[h][r0 - 1:r0, :]
                blocks[h][i] = _bdot_nt(qs[h][r0:r0 + sub] * jnp.exp(bs[h][r0:r0 + sub] - ref),
                                        ks[h] * _masked_exp(ref - bs[h], src_row < r0))
            else:
                blocks[h][i] = jnp.zeros((sub, CHUNK), F32)
    for h in heads:
        for i in range(n_sub):
            r0 = i * sub
            qi, ki, bi = qs[h][r0:r0 + sub], ks[h][r0:r0 + sub], bs[h][r0:r0 + sub]
            for s in range(sub):
                e = _masked_exp(bi - bi[s:s + 1, :], row >= s)
                col = jnp.sum(qi * ki[s:s + 1, :] * e, axis=1, keepdims=True)
                blocks[h][i] = jnp.where(src_lane == r0 + s, col, blocks[h][i])
    os_ = [_bdot(jnp.concatenate(blocks[h], axis=0), vs[h]) + o_inter[h] for h in heads]
    ends = [bs[h][CHUNK - 1:CHUNK, :] for h in heads]
    new_states = [state[h] * jnp.exp(ends[h]) + _bdot_tn(vs[h], ks[h] * jnp.exp(ends[h] - bs[h])) for h in heads]
    outs = [_rms(os_[h], norm_w) * _silu(g_raw[:, cols[h]]) for h in heads]
    return (jnp.concatenate(outs, axis=1),), jnp.stack(new_states)


def _ssd_chunk(tiles, halos, state, consts):
    z, xbc_raw, small = tiles
    (halo,) = halos
    conv_w, conv_b, dt_bias, a_log, d_skip, norm_w = consts
    xbc = _silu(_causal_conv(halo, xbc_raw, conv_w) + conv_b)
    xs, bm, cm = xbc[:, :MIX_W], xbc[:, MIX_W:MIX_W + 2 * SSM_N], xbc[:, MIX_W + 2 * SSM_N:]
    dt = jax.nn.softplus(small + dt_bias)
    cum = _cumsum_rows(-jnp.exp(a_log) * dt)
    cum_t2 = jnp.concatenate([cum, cum], axis=0).T
    lane = lax.broadcasted_iota(jnp.int32, (1, LANES), 1)
    first = lane < SSM_P
    hm0 = jnp.where(first, 1.0, 0.0).astype(F32)
    hm1 = 1.0 - hm0
    src = jnp.where(first, lane, lane - SSM_P)
    tri2 = src <= lax.broadcasted_iota(jnp.int32, (CHUNK, 1), 0)
    pick = lambda a, b: jnp.where(first, a, b)
    bgs = [bm[:, g * SSM_N:(g + 1) * SSM_N] for g in range(2)]
    cgs = [cm[:, g * SSM_N:(g + 1) * SSM_N] for g in range(2)]
    gmats = [_bdot_nt(cgs[g], jnp.concatenate([bgs[g], bgs[g]], axis=0)) for g in range(2)]
    pairs = range(6)
    xps = [xs[:, p * LANES:(p + 1) * LANES] for p in pairs]
    c0s = [cum[:, 2 * p:2 * p + 1] for p in pairs]
    c1s = [cum[:, 2 * p + 1:2 * p + 2] for p in pairs]
    e0s = [cum[CHUNK - 1:CHUNK, 2 * p:2 * p + 1] for p in pairs]
    e1s = [cum[CHUNK - 1:CHUNK, 2 * p + 1:2 * p + 2] for p in pairs]
    segs = [_masked_exp(pick(c0s[p], c1s[p]) - pick(cum_t2[2 * p:2 * p + 1, :], cum_t2[2 * p + 1:2 * p + 2, :]), tri2)
            for p in pairs]
    vms = []
    for p in pairs:
        v = xps[p] * pick(dt[:, 2 * p:2 * p + 1], dt[:, 2 * p + 1:2 * p + 2])
        vms.append(jnp.concatenate([v * hm0, v * hm1], axis=0))
    y_intra = [_bdot(gmats[p // 3] * segs[p], vms[p]) for p in pairs]
    y_inter = [_bdot(jnp.concatenate([cgs[p // 3] * jnp.exp(c0s[p]), cgs[p // 3] * jnp.exp(c1s[p])], axis=1),
                     jnp.concatenate([state[p] * hm0, state[p] * hm1], axis=0)) for p in pairs]
    new_states = [_bdot_tn(jnp.concatenate([bgs[p // 3] * jnp.exp(e0s[p] - c0s[p]),
                                            bgs[p // 3] * jnp.exp(e1s[p] - c1s[p])], axis=0), vms[p])
                  + state[p] * pick(jnp.exp(e0s[p]), jnp.exp(e1s[p])) for p in pairs]
    ys = [y_intra[p] + y_inter[p] + pick(d_skip[:, 2 * p:2 * p + 1], d_skip[:, 2 * p + 1:2 * p + 2]) * xps[p] for p in pairs]
    y = jnp.concatenate(ys, axis=1) * _silu(z)
    gw = MIX_W // 2
    y = jnp.concatenate([_rms(y[:, g * gw:(g + 1) * gw], norm_w[:, g * gw:(g + 1) * gw]) for g in range(2)], axis=1)
    return (y,), jnp.stack(new_states)


def _gdn_chunk(tiles, halos, state, consts):
    qkv_raw, z, small = tiles[:3]
    given = tiles[3] if len(tiles) > 3 else None
    (halo,) = halos
    conv_w, dt_bias, a_log, norm_w = consts
    qkv = _silu(_causal_conv(halo, qkv_raw, conv_w))
    beta_all = jax.nn.sigmoid(small)
    cum = _cumsum_rows(-jnp.exp(a_log) * jax.nn.softplus(small + dt_bias))
    cum_t = cum.T
    tri, tri_strict = _tri_mask(CHUNK), _tri_mask(CHUNK, strict=True)

    def group(hs):
        n = range(len(hs))
        qs, ks, betas, cs, ces, decays, rhss = [], [], [], [], [], [], []
        for h in hs:
            q = qkv[:, h * HEAD:(h + 1) * HEAD]
            k = qkv[:, MIX_W + h * HEAD:MIX_W + (h + 1) * HEAD]
            v = qkv[:, 2 * MIX_W + h * HEAD:2 * MIX_W + (h + 1) * HEAD]
            q = q * lax.rsqrt(jnp.sum(q * q, axis=-1, keepdims=True) + NORM_EPS) * (HEAD ** -0.5)
            k = k * lax.rsqrt(jnp.sum(k * k, axis=-1, keepdims=True) + NORM_EPS)
            beta = beta_all[:, GB_OFF + h:GB_OFF + h + 1]
            c, c_t = cum[:, GA_OFF + h:GA_OFF + h + 1], cum_t[GA_OFF + h:GA_OFF + h + 1, :]
            qs.append(q), ks.append(k), betas.append(beta), cs.append(c)
            ces.append(cum[CHUNK - 1:CHUNK, GA_OFF + h:GA_OFF + h + 1])
            decays.append(_masked_exp(c - c_t, tri))
            rhss.append(jnp.concatenate([v * beta, k * (beta * jnp.exp(c))], axis=1))
        sts = [state[h] for h in hs]
        qk_kks = [_bdot_nt(jnp.concatenate([qs[i], ks[i]], axis=0), ks[i]) for i in n]
        mats = [jnp.where(tri_strict, betas[i] * qk_kks[i][CHUNK:] * decays[i], 0.0) for i in n]
        wide = _unit_lower_inverses(mats) if given is None else [given[:, h * LANES:(h + 1) * LANES] for h in hs]
        sols = _solves_with_inverses([inv[:, :CHUNK] for inv in wide], mats, rhss)
        on_states = [_bdot(jnp.concatenate([sols[i][:, HEAD:], qs[i] * jnp.exp(cs[i])], axis=0), sts[i]) for i in n]
        us = [sols[i][:, :HEAD] - on_states[i][:CHUNK] for i in n]
        os_ = [on_states[i][CHUNK:] + _bdot(qk_kks[i][:CHUNK] * decays[i], us[i]) for i in n]
        new = [jnp.exp(ces[i]) * sts[i] + _bdot_tn(ks[i] * jnp.exp(ces[i] - cs[i]), us[i]) for i in n]
        outs = [_rms(os_[i], norm_w) * _silu(z[:, h * HEAD:(h + 1) * HEAD]) for i, h in enumerate(hs)]
        return outs, new, wide

    outs, new_states, inverses = [], [], []
    for h0 in range(0, N_HEAD6, HEAD_GROUP):
        o, s, w = group(list(range(h0, h0 + HEAD_GROUP)))
        outs += o
        new_states += s
        inverses += w
    y = jnp.concatenate(outs, axis=1)
    return ((y,) if given is not None else (y, jnp.concatenate(inverses, axis=1))), jnp.stack(new_states)


def _scan_fwd(name, fn, tiled, halo_idx, consts, out_width, state_shape, aux_width=None):
    seq = tiled[0][0].shape[0]
    nc = seq // CHUNK
    n_t, n_h, n_c = len(tiled), len(halo_idx), len(consts)

    def body(*refs):
        t_refs, h_refs, c_refs = refs[:n_t], refs[n_t:n_t + n_h], refs[n_t + n_h:n_t + n_h + n_c]
        y_ref, save_ref = refs[n_t + n_h + n_c:n_t + n_h + n_c + 2]
        st_ref = refs[-1]
        i = pl.program_id(0)

        @pl.when(i == 0)
        def _():
            st_ref[...] = jnp.zeros_like(st_ref)

        flag = jnp.where(i > 0, 1.0, 0.0).astype(F32)
        st = st_ref[...]
        outs, new = fn([r[...] for r in t_refs], [r[...] * flag for r in h_refs], st, [r[...] for r in c_refs])
        save_ref[0] = st
        y_ref[...] = outs[0].astype(y_ref.dtype)
        if aux_width is not None:
            refs[-2][...] = outs[1]
        st_ref[...] = new

    in_specs = [pl.BlockSpec((CHUNK, w), functools.partial(lambda i, cb: (i, cb), cb=cb)) for _, w, cb in tiled]
    in_specs += [pl.BlockSpec((HALO, tiled[j][1]),
                              functools.partial(lambda i, cb: (jnp.maximum(i * (CHUNK // HALO) - 1, 0), cb), cb=tiled[j][2]))
                 for j in halo_idx]
    in_specs += [pl.BlockSpec(c.shape, functools.partial(lambda i, nd: (0,) * nd, nd=c.ndim)) for c in consts]
    zeros = (0,) * len(state_shape)
    out_specs = [pl.BlockSpec((CHUNK, out_width), lambda i: (i, 0)), pl.BlockSpec((1,) + state_shape, lambda i: (i,) + zeros)]
    out_shape = [jax.ShapeDtypeStruct((seq, out_width), BF), jax.ShapeDtypeStruct((nc,) + state_shape, F32)]
    if aux_width is not None:
        out_specs.append(pl.BlockSpec((CHUNK, aux_width), lambda i: (i, 0)))
        out_shape.append(jax.ShapeDtypeStruct((seq, aux_width), F32))
    return _pc(
        body, name=name, grid=(nc,), in_specs=in_specs, out_specs=tuple(out_specs), out_shape=tuple(out_shape),
        scratch_shapes=[pltpu.VMEM(state_shape, F32)],
        compiler_params=_cparams(("arbitrary",)),
    )(*[t[0] for t in tiled], *[tiled[j][0] for j in halo_idx], *consts)


def _scan_bwd(name, fn, tiled, halo_idx, consts, saved, dy, dtile_dtypes, extra=None):
    seq = tiled[0][0].shape[0]
    nc = seq // CHUNK
    n_t, n_h, n_c = len(tiled), len(halo_idx), len(consts)
    state_shape = saved.shape[1:]
    n_x = 0 if extra is None else 1
    keep = [j for j, dtd in enumerate(dtile_dtypes) if dtd is not None]
    n_k = len(keep)

    def body(*refs):
        t_refs, h_refs, c_refs = refs[:n_t], refs[n_t:n_t + n_h], refs[n_t + n_h:n_t + n_h + n_c]
        pos = n_t + n_h + n_c
        save_ref, dy_ref = refs[pos], refs[pos + 1]
        x_refs = refs[pos + 2:pos + 2 + n_x]
        pos += 2 + n_x
        dt_refs, dc_refs = refs[pos:pos + n_k], refs[pos + n_k:pos + n_k + n_c]
        dst_ref = refs[pos + n_k + n_c]
        carry_refs = refs[pos + n_k + n_c + 1:]
        i = pl.program_id(0)

        @pl.when(i == 0)
        def _():
            dst_ref[...] = jnp.zeros_like(dst_ref)
            for r in carry_refs:
                r[...] = jnp.zeros_like(r)
            for r in dc_refs:
                r[...] = jnp.zeros_like(r)

        flag = jnp.where(i < nc - 1, 1.0, 0.0).astype(F32)
        tiles = [r[...] for r in t_refs]
        halos = [r[...] * flag for r in h_refs]
        cvals = [r[...] for r in c_refs]
        _, vjp = jax.vjp(fn, tiles, halos, save_ref[0], cvals)
        d_tiles, d_halos, d_state, d_consts = vjp(((dy_ref[...].astype(F32),), dst_ref[...]))
        dst_ref[...] = d_state
        for r, g in zip(dc_refs, d_consts):
            r[...] += g
        for r, j in zip(dt_refs, keep):
            g = d_tiles[j]
            if extra is not None and extra[0] == j:
                g = g + x_refs[0][...].astype(F32)
            r[...] = g.astype(r.dtype)
            if j in halo_idx:
                cr = carry_refs[halo_idx.index(j)]
                r[CHUNK - HALO:CHUNK, :] = (g[CHUNK - HALO:CHUNK, :] + cr[...]).astype(r.dtype)
                cr[...] = d_halos[halo_idx.index(j)] * flag

    rev = lambda i: nc - 1 - i
    in_specs = [pl.BlockSpec((CHUNK, w), functools.partial(lambda i, cb: (rev(i), cb), cb=cb)) for _, w, cb in tiled]
    in_specs += [pl.BlockSpec((HALO, tiled[j][1]),
                              functools.partial(lambda i, cb: (jnp.maximum(rev(i) * (CHUNK // HALO) - 1, 0), cb), cb=tiled[j][2]))
                 for j in halo_idx]
    in_specs += [pl.BlockSpec(c.shape, functools.partial(lambda i, nd: (0,) * nd, nd=c.ndim)) for c in consts]
    zeros = (0,) * len(state_shape)
    in_specs += [pl.BlockSpec((1,) + state_shape, lambda i: (rev(i),) + zeros),
                 pl.BlockSpec((CHUNK, dy.shape[1]), lambda i: (rev(i), 0))]
    args = [t[0] for t in tiled] + [tiled[j][0] for j in halo_idx] + list(consts) + [saved, dy]
    if extra is not None:
        in_specs.append(pl.BlockSpec((CHUNK, extra[1].shape[1]), lambda i: (rev(i), 0)))
        args.append(extra[1])
    out_specs = [pl.BlockSpec((CHUNK, tiled[j][1]), lambda i: (rev(i), 0)) for j in keep]
    out_specs += [pl.BlockSpec(c.shape, functools.partial(lambda i, nd: (0,) * nd, nd=c.ndim)) for c in consts]
    out_shape = [jax.ShapeDtypeStruct((seq, tiled[j][1]), dtile_dtypes[j]) for j in keep]
    out_shape += [jax.ShapeDtypeStruct(c.shape, F32) for c in consts]
    scratch = [pltpu.VMEM(state_shape, F32)] + [pltpu.VMEM((HALO, tiled[j][1]), F32) for j in halo_idx]
    return _pc(body, name=name, grid=(nc,), in_specs=in_specs, out_specs=tuple(out_specs), out_shape=tuple(out_shape),
               scratch_shapes=scratch, compiler_params=_cparams(("arbitrary",)))(*args)


def _tile_fwd(name, fn, tiled, consts, outs, tm=TOKEN_TILE):
    seq = tiled[0][0].shape[0]
    n_t, n_c = len(tiled), len(consts)

    def body(*refs):
        res = fn(*[r[...] for r in refs[:n_t + n_c]])
        for r, y in zip(refs[n_t + n_c:], res):
            r[...] = y.astype(r.dtype)

    in_specs = [pl.BlockSpec((tm, w), functools.partial(lambda i, cb: (i, cb), cb=cb)) for _, w, cb in tiled]
    in_specs += [pl.BlockSpec(c.shape, functools.partial(lambda i, nd: (0,) * nd, nd=c.ndim)) for c in consts]
    return _pc(body, name=name, grid=(seq // tm,), in_specs=in_specs,
               out_specs=tuple(pl.BlockSpec((tm, w), lambda i: (i, 0)) for w, _ in outs),
               out_shape=tuple(jax.ShapeDtypeStruct((seq, w), dtp) for w, dtp in outs),
               compiler_params=_cparams(("arbitrary",)))(*[t[0] for t in tiled], *consts)


def _tile_bwd(name, fn, tiled, consts, douts, dtile_dtypes, add_to=None, tm=TOKEN_TILE):
    seq = tiled[0][0].shape[0]
    n_t, n_c, n_o = len(tiled), len(consts), len(douts)
    n_x = 0 if add_to is None else 1
    keep = [j for j, dtp in enumerate(dtile_dtypes) if dtp is not None]

    def body(*refs):
        vals = [r[...].astype(F32) for r in refs[:n_t + n_c]]
        pos = n_t + n_c
        g_refs, x_refs = refs[pos:pos + n_o], refs[pos + n_o:pos + n_o + n_x]
        pos += n_o + n_x
        dt_refs, dc_refs = refs[pos:pos + len(keep)], refs[pos + len(keep):]
        i = pl.program_id(0)

        @pl.when(i == 0)
        def _():
            for r in dc_refs:
                r[...] = jnp.zeros_like(r)

        _, vjp = jax.vjp(fn, *vals)
        cts = vjp(tuple(g[...].astype(F32) for g in g_refs))
        for r, j in zip(dt_refs, keep):
            g = cts[j]
            if add_to is not None and add_to[0] == j:
                g = g + x_refs[0][...].astype(F32)
            r[...] = g.astype(r.dtype)
        for r, g in zip(dc_refs, cts[n_t:]):
            r[...] += g

    in_specs = [pl.BlockSpec((tm, w), functools.partial(lambda i, cb: (i, cb), cb=cb)) for _, w, cb in tiled]
    in_specs += [pl.BlockSpec(c.shape, functools.partial(lambda i, nd: (0,) * nd, nd=c.ndim)) for c in consts]
    in_specs += [pl.BlockSpec((tm, g.shape[1]), lambda i: (i, 0)) for g in douts]
    args = [t[0] for t in tiled] + list(consts) + list(douts)
    if add_to is not None:
        in_specs.append(pl.BlockSpec((tm, add_to[1].shape[1]), lambda i: (i, 0)))
        args.append(add_to[1])
    out_specs = [pl.BlockSpec((tm, tiled[j][1]), lambda i: (i, 0)) for j in keep]
    out_specs += [pl.BlockSpec(c.shape, functools.partial(lambda i, nd: (0,) * nd, nd=c.ndim)) for c in consts]
    out_shape = [jax.ShapeDtypeStruct((seq, tiled[j][1]), dtile_dtypes[j]) for j in keep]
    out_shape += [jax.ShapeDtypeStruct(c.shape, F32) for c in consts]
    return _pc(body, name=name, grid=(seq // tm,), in_specs=in_specs, out_specs=tuple(out_specs),
               out_shape=tuple(out_shape), compiler_params=_cparams(("arbitrary",)))(*args)


def _lnmod_fn(x, nw, sc, sh):
    return (_rms(x, nw) * (1.0 + sc) + sh,)


def _gated_fn(o, g):
    return ((1.0 + g) * o,)


def _resid_fn(x, o, g):
    return (x + (1.0 + g) * o,)


def _swiglu_fn(gu):
    return (_silu(gu[:, :FFN_H]) * gu[:, FFN_H:],)


def _merge_fn(yh, ys, yg, logits, wb, b_merge):
    gates = jax.nn.sigmoid(logits + b_merge)
    acc = None
    for n, y in enumerate((yh, ys, yg)):
        t = gates[:, n * D_MODEL:(n + 1) * D_MODEL] * _bdot(y, wb[n])
        acc = t if acc is None else acc + t
    return (acc,)


MM_VMEM_BUDGET = 40 * 1024 * 1024
MM_TILE_CAP = 1024
MM_K_CAP = 4096


def _divisor(n, cap, unit=LANES):
    best = None
    for d in range(unit, min(n, cap) + 1, unit):
        if n % d == 0:
            best = d
    return n if best is None else best


def _mm_tiles(m, n, k, out_bytes):
    tk = k if k <= MM_K_CAP else _divisor(k, 3072)
    tm, tn = _divisor(m, MM_TILE_CAP), _divisor(n, MM_TILE_CAP + MM_TILE_CAP // 2)

    def need(tm_, tn_):
        acc = tm_ * tn_ * 4 if tk < k else 0
        return 2 * 2 * tk * (tm_ + tn_) + acc + 2 * tm_ * tn_ * out_bytes

    while need(tm, tn) > MM_VMEM_BUDGET:
        if tn >= tm and _divisor(n, tn - LANES) < tn:
            tn = _divisor(n, tn - LANES)
        elif _divisor(m, tm - LANES) < tm:
            tm = _divisor(m, tm - LANES)
        else:
            break
    return tm, tn, tk


def _mm(a, b, mode, out_dtype, name):
    if mode == "nn":
        (m, k), n = a.shape, b.shape[1]
    elif mode == "nt":
        (m, k), n = a.shape, b.shape[0]
    else:
        (k, m), n = a.shape, b.shape[1]
    tm, tn, tk = _mm_tiles(m, n, k, jnp.dtype(out_dtype).itemsize)
    nk = k // tk
    dims = {"nn": ((1,), (0,)), "nt": ((1,), (1,)), "tn": ((0,), (0,))}[mode]

    def body_one(a_ref, b_ref, o_ref):
        o_ref[...] = lax.dot_general(a_ref[...], b_ref[...], (dims, ((), ())), preferred_element_type=F32).astype(o_ref.dtype)

    def body_acc(a_ref, b_ref, o_ref, acc_ref):
        kk = pl.program_id(2)

        @pl.when(kk == 0)
        def _():
            acc_ref[...] = jnp.zeros_like(acc_ref)

        acc_ref[...] += lax.dot_general(a_ref[...], b_ref[...], (dims, ((), ())), preferred_element_type=F32)

        @pl.when(kk == nk - 1)
        def _():
            o_ref[...] = acc_ref[...].astype(o_ref.dtype)

    a_spec = pl.BlockSpec((tk, tm), lambda i, j, kk: (kk, i)) if mode == "tn" else pl.BlockSpec((tm, tk), lambda i, j, kk: (i, kk))
    b_spec = pl.BlockSpec((tn, tk), lambda i, j, kk: (j, kk)) if mode == "nt" else pl.BlockSpec((tk, tn), lambda i, j, kk: (kk, j))
    return _pc(body_one if nk == 1 else body_acc, name=name, grid=(m // tm, n // tn, nk), in_specs=[a_spec, b_spec],
               out_specs=pl.BlockSpec((tm, tn), lambda i, j, kk: (i, j)),
               out_shape=jax.ShapeDtypeStruct((m, n), out_dtype),
               scratch_shapes=[] if nk == 1 else [pltpu.VMEM((tm, tn), F32)],
               compiler_params=_cparams(("parallel", "parallel", "arbitrary")))(a.astype(BF), b.astype(BF))


def _final_loss(x, tgt, norm_final, tm=TOKEN_TILE):
    seq = x.shape[0]

    def fn(xv, nf, tv):
        err = jnp.square(_rms(xv, nf) - tv)
        return 0.5 * jnp.sum(jnp.mean(err, axis=-1))

    def body(x_ref, t_ref, nf_ref, loss_ref, dx_ref, dnf_ref):
        i = pl.program_id(0)

        @pl.when(i == 0)
        def _():
            loss_ref[...] = jnp.zeros_like(loss_ref)
            dnf_ref[...] = jnp.zeros_like(dnf_ref)

        val, vjp = jax.vjp(functools.partial(fn, tv=t_ref[...]), x_ref[...], nf_ref[...])
        dx, dnf = vjp(jnp.ones((), F32))
        dx_ref[...] = dx
        dnf_ref[...] += dnf
        loss_ref[...] += jnp.broadcast_to(val, loss_ref.shape)

    return _pc(body, name="final_loss", grid=(seq // tm,),
               in_specs=[pl.BlockSpec((tm, D_MODEL), lambda i: (i, 0)), pl.BlockSpec((tm, D_MODEL), lambda i: (i, 0)),
                         pl.BlockSpec((1, D_MODEL), lambda i: (0, 0))],
               out_specs=(pl.BlockSpec((8, LANES), lambda i: (0, 0)), pl.BlockSpec((tm, D_MODEL), lambda i: (i, 0)),
                          pl.BlockSpec((1, D_MODEL), lambda i: (0, 0))),
               out_shape=(jax.ShapeDtypeStruct((8, LANES), F32), jax.ShapeDtypeStruct((seq, D_MODEL), F32),
                          jax.ShapeDtypeStruct((1, D_MODEL), F32)),
               compiler_params=_cparams(("arbitrary",)))(x, tgt, norm_final)


def _ada_fwd(c_all, w_ada, b_ada_cols):
    n_l, _, cols = w_ada.shape

    def body(c_ref, w_ref, b_ref, o_ref):
        o_ref[0] = jnp.dot(_silu(c_ref[...]), w_ref[0], preferred_element_type=F32, precision=HI) + b_ref[0]

    return _pc(body, name="ada_fwd", grid=(n_l,),
               in_specs=[pl.BlockSpec((8, D_MODEL), lambda l: (0, 0)), pl.BlockSpec((1, D_MODEL, cols), lambda l: (l, 0, 0)),
                         pl.BlockSpec((1, 1, cols), lambda l: (l, 0, 0))],
               out_specs=pl.BlockSpec((1, 8, cols), lambda l: (l, 0, 0)),
               out_shape=jax.ShapeDtypeStruct((n_l, 8, cols), F32),
               compiler_params=_cparams(("arbitrary",)))(c_all, w_ada, b_ada_cols.reshape(n_l, 1, cols))


def _ada_bwd(c_all, dmod_cols):
    n_l, _, cols = dmod_cols.shape

    def body(c_ref, g_ref, o_ref):
        o_ref[0] = lax.dot_general(_silu(c_ref[...]), g_ref[0], (((0,), (0,)), ((), ())), preferred_element_type=F32,
                                   precision=HI)

    return _pc(body, name="ada_bwd", grid=(n_l,),
               in_specs=[pl.BlockSpec((8, D_MODEL), lambda l: (0, 0)), pl.BlockSpec((1, 8, cols), lambda l: (l, 0, 0))],
               out_specs=pl.BlockSpec((1, D_MODEL, cols), lambda l: (l, 0, 0)),
               out_shape=jax.ShapeDtypeStruct((n_l, D_MODEL, cols), F32),
               compiler_params=_cparams(("arbitrary",)))(c_all, dmod_cols)


def _lb_fn(logits):
    e = jnp.exp(logits - jnp.max(logits, axis=0, keepdims=True))
    p = e / jnp.sum(e, axis=0, keepdims=True)
    r = lax.broadcasted_iota(jnp.int32, (DEPTH, 1), 0)
    lb = jnp.zeros_like(p)
    for j in range(1, DEPTH):
        lb = lb + jnp.where(r >= j, p[j:j + 1, :], 0.0)
    return lb


def _lb_fwd(logits):
    def body(l_ref, o_ref):
        o_ref[...] = _lb_fn(l_ref[...])

    return _pc(body, name="lb_fwd", out_shape=jax.ShapeDtypeStruct(logits.shape, F32))(logits)


def _lb_bwd(logits, dlb):
    def body(l_ref, g_ref, o_ref):
        _, vjp = jax.vjp(_lb_fn, l_ref[...])
        o_ref[...] = vjp(g_ref[...])[0]

    return _pc(body, name="lb_bwd", out_shape=jax.ShapeDtypeStruct(logits.shape, F32))(logits, dlb)


def _rows_for(n_rows, n_cols):
    r = 8
    while r * 2 <= n_rows and n_rows % (r * 2) == 0 and r * 2 * n_cols <= 256 * 1024:
        r *= 2
    return r if n_rows % r == 0 else n_rows


def _ew(name, fn, ins, out_dtypes):
    n_rows, n_cols = ins[0].shape
    tr = _rows_for(n_rows, n_cols)
    n_in = len(ins)

    def body(*refs):
        res = fn(*[r[...] for r in refs[:n_in]])
        for r, y in zip(refs[n_in:], res):
            r[...] = y.astype(r.dtype)

    spec = pl.BlockSpec((tr, n_cols), lambda i: (i, 0))
    return _pc(body, name=name, grid=(n_rows // tr,), in_specs=[spec] * n_in, out_specs=tuple([spec] * len(out_dtypes)),
               out_shape=tuple(jax.ShapeDtypeStruct((n_rows, n_cols), d) for d in out_dtypes),
               compiler_params=_cparams(("arbitrary",)))(*ins)


def _adamw_fn(w, g, m, v):
    m = ADAM_B1 * m + (1.0 - ADAM_B1) * g
    v = ADAM_B2 * v + (1.0 - ADAM_B2) * jnp.square(g)
    m_hat = m / (1.0 - ADAM_B1 ** ADAM_STEP)
    v_hat = v / (1.0 - ADAM_B2 ** ADAM_STEP)
    return -ADAM_LR * (m_hat / (jnp.sqrt(v_hat) + ADAM_EPS) + ADAM_WD * w), m, v


def _adamw(name, w, g, m, v):
    shape = w.shape
    two = (-1, shape[-1])
    d, nm, nv = _ew(name, _adamw_fn, [a.reshape(two) for a in (w, g, m, v)], [F32, F32, F32])
    return d.reshape(shape), nm.reshape(shape), nv.reshape(shape)


def _sum_leading(name, a, out_dtype):
    n, n_rows, n_cols = a.shape
    tr = _rows_for(n_rows, n_cols)

    def body(a_ref, o_ref):
        acc = a_ref[0].astype(F32)
        for j in range(1, n):
            acc = acc + a_ref[j].astype(F32)
        o_ref[...] = acc.astype(o_ref.dtype)

    return _pc(body, name=name, grid=(n_rows // tr,), in_specs=[pl.BlockSpec((n, tr, n_cols), lambda i: (0, i, 0))],
               out_specs=pl.BlockSpec((tr, n_cols), lambda i: (i, 0)),
               out_shape=jax.ShapeDtypeStruct((n_rows, n_cols), out_dtype),
               compiler_params=_cparams(("arbitrary",)))(a)


MESH = pl.DeviceIdType.MESH
ANY = pl.BlockSpec(memory_space=pl.ANY)


def _place():
    return lax.axis_index("x"), lax.axis_index("y"), lax.axis_index("c")


def _all_gather_small(name, a):
    m_per, n = a.shape

    def body(x_ref, out_ref, send_sems, recv_sems, local_sem):
        x, y, c = _place()
        me, sibling = (x, y, c), (x, y, 1 - c)
        chips = [(1 - x, y), (x, 1 - y), (1 - x, 1 - y)]

        def rows(px, py, pc):
            return out_ref.at[pl.ds((4 * px + 2 * py + pc) * m_per, m_per), :]

        def copy(k, block, to, src=None):
            return pltpu.make_async_remote_copy(src_ref=rows(*block) if src is None else src, dst_ref=rows(*block),
                                                send_sem=send_sems.at[k], recv_sem=recv_sems.at[k], device_id=to,
                                                device_id_type=MESH)

        mine = pltpu.make_async_copy(x_ref, rows(*me), local_sem)
        mine.start()
        first = [copy(0, me, sibling, src=x_ref)]
        first += [copy(1 + j, me, (*chip, c), src=x_ref) for j, chip in enumerate(chips)]
        for cp in first:
            cp.start()
        passed = [copy(4 + j, (*chip, c), sibling) for j, chip in enumerate(chips)]
        for j, chip in enumerate(chips):
            copy(1 + j, (*chip, c), me).wait_recv()
            passed[j].start()
        copy(0, sibling, me).wait_recv()
        for j, chip in enumerate(chips):
            copy(4 + j, (*chip, 1 - c), me).wait_recv()
        for cp in first + passed:
            cp.wait_send()
        mine.wait()

    out = _pc(body, name=name, out_shape=jax.ShapeDtypeStruct((8 * m_per, n), a.dtype),
              in_specs=[pl.BlockSpec(memory_space=pltpu.VMEM)], out_specs=pl.BlockSpec(memory_space=pltpu.VMEM),
              scratch_shapes=[pltpu.SemaphoreType.DMA((7,)), pltpu.SemaphoreType.DMA((7,)), pltpu.SemaphoreType.DMA],
              compiler_params=pltpu.CompilerParams(vmem_limit_bytes=VMEM_LIMIT))(a)
    return out.reshape(8, m_per, n)


def _chip_gather(name, pack):
    n_l, n_r, n_c = pack.shape
    half = n_r // 2

    def body(p_ref, o_ref, send_sems, recv_sems):
        x, y, c = _place()
        sibling = (x, y, 1 - c)
        chips = [(1 - x, y), (x, 1 - y), (1 - x, 1 - y)]

        def slab(px, py, pc):
            return o_ref.at[2 * px + py, :, pl.ds(pc * half, half), :]

        def copy(k, src, dst, to):
            return pltpu.make_async_remote_copy(src_ref=src, dst_ref=dst, send_sem=send_sems.at[k], recv_sem=recv_sems.at[k],
                                                device_id=to, device_id_type=MESH)

        first = [copy(j, p_ref.at[:, pl.ds(c * half, half), :], slab(x, y, c), (*chip, c)) for j, chip in enumerate(chips)]
        for cp in first:
            cp.start()
        passed = [copy(3 + j, slab(*chip, c), slab(*chip, c), sibling) for j, chip in enumerate(chips)]
        for j, chip in enumerate(chips):
            copy(j, slab(*chip, c), slab(*chip, c), (*chip, c)).wait_recv()
            passed[j].start()
        for j, chip in enumerate(chips):
            copy(3 + j, slab(*chip, 1 - c), slab(*chip, 1 - c), sibling).wait_recv()
        for cp in first + passed:
            cp.wait_send()

    return _pc(body, name=name, out_shape=jax.ShapeDtypeStruct((4, n_l, n_r, n_c), pack.dtype), in_specs=[ANY], out_specs=ANY,
               scratch_shapes=[pltpu.SemaphoreType.DMA((6,)), pltpu.SemaphoreType.DMA((6,))])(pack)


def _pair_swap(name, give):
    def body(g_ref, o_ref, send_sem, recv_sem):
        x, y, c = _place()
        cp = pltpu.make_async_remote_copy(src_ref=g_ref, dst_ref=o_ref, send_sem=send_sem, recv_sem=recv_sem,
                                          device_id=(x, y, 1 - c), device_id_type=MESH)
        cp.start()
        cp.wait()

    return _pc(body, name=name, out_shape=jax.ShapeDtypeStruct(give.shape, give.dtype), in_specs=[ANY], out_specs=ANY,
               scratch_shapes=[pltpu.SemaphoreType.DMA, pltpu.SemaphoreType.DMA])(give)


def _chip_exchange(name, parts):
    def body(p_ref, o_ref, send_sems, recv_sems):
        x, y, c = _place()
        me = 2 * x + y
        chips = [(1 - x, y), (x, 1 - y), (1 - x, 1 - y)]

        def copy(k, src, dst, to):
            return pltpu.make_async_remote_copy(src_ref=src, dst_ref=dst, send_sem=send_sems.at[k], recv_sem=recv_sems.at[k],
                                                device_id=to, device_id_type=MESH)

        sends = [copy(j, p_ref.at[2 * px + py], o_ref.at[me], (px, py, c)) for j, (px, py) in enumerate(chips)]
        for cp in sends:
            cp.start()
        for j, (px, py) in enumerate(chips):
            copy(j, p_ref.at[2 * px + py], o_ref.at[2 * px + py], (px, py, c)).wait_recv()
        for cp in sends:
            cp.wait_send()

    return _pc(body, name=name, out_shape=jax.ShapeDtypeStruct(parts.shape, parts.dtype), in_specs=[ANY], out_specs=ANY,
               scratch_shapes=[pltpu.SemaphoreType.DMA((3,)), pltpu.SemaphoreType.DMA((3,))])(parts)


N_CHIP = 4
BIG = (("w_in", (1024, 2822), 1, (1024, 2822)), ("w_branch", (3, 768, 256), 2, (2304, 256)),
       ("w_out", (256, 1024), 0, (256, 1024)), ("w_ffn_in", (1024, 1408), 1, (1024, 1408)),
       ("w_ffn_out", (704, 1024), 0, (704, 1024)))
G768 = ((0, 3072), (3072, 3840), (7436, 8204))
GXBC, GQKV, GGATE = (3840, 5120), (5132, 7436), (8216, 11288)
GSMALL = ((5120, 5132), (8204, 8210), (8210, 8216))
W768, WXBC, WGATE = 4608, CONV_CH, 3 * D_MODEL
IN_PAD = W768 + WXBC + QKV_W + WGATE + SMALL_W


def _join_shards(slabs, axis, shard_shape):
    n_l = slabs.shape[1]
    parts = [slabs[j].reshape((n_l,) + shard_shape) for j in range(N_CHIP)]
    return jnp.concatenate(parts, axis=axis + 1)


def _split_shards(full, axis, rows_cols):
    n_l = full.shape[0]
    size = full.shape[axis + 1] // N_CHIP
    return jnp.stack([lax.slice_in_dim(full, j * size, (j + 1) * size, axis=axis + 1).reshape((n_l,) + rows_cols)
                      for j in range(N_CHIP)])


def _gather_weights(w, chip, big=BIG):
    out = {}
    for n, shape, ax, rc in big:
        n_l = w[n].shape[0]
        mine = w[n].astype(BF).reshape((n_l,) + rc)
        slabs = lax.dynamic_update_slice(_chip_gather("gather_" + n, mine), mine[None], (chip, 0, 0, 0))
        out[n] = _join_shards(slabs, ax, shape)
    return out


def _pair_stage(full_grads, core, big=BIG):
    out = {}
    for n, _, ax, (rows, cols) in big:
        n_l = full_grads[n].shape[0]
        slabs = _split_shards(full_grads[n], ax, (rows, cols))
        half = rows // 2
        keep = lax.dynamic_slice_in_dim(slabs, core * half, half, axis=2).reshape(-1, cols)
        give = lax.dynamic_slice_in_dim(slabs, (1 - core) * half, half, axis=2).reshape(-1, cols)
        got = _pair_swap("pair_swap_" + n, give)
        (pair_sum,) = _ew("pair_sum_" + n, lambda a, b: (a.astype(F32) + b.astype(F32),), [keep, got], [BF])
        out[n] = pair_sum.reshape(N_CHIP, n_l * half, cols)
    return out


def _own_slab(landed, pair_sum, chip):
    return lax.dynamic_update_slice(landed, lax.dynamic_slice_in_dim(pair_sum, chip, 1, axis=0), (chip, 0, 0))


def _finish_reduce(parts, n_l, core, big=BIG):
    out = {}
    for n, shape, _, (rows, cols) in big:
        half = rows // 2
        mine = _sum_leading("chip_sum_" + n, parts[n], F32).reshape(n_l, half, cols)
        theirs = _pair_swap("pair_share_" + n, mine)
        full = jnp.concatenate([jnp.where(core == 0, mine, theirs), jnp.where(core == 0, theirs, mine)], axis=1)
        out[n] = full.reshape((n_l,) + shape)
    return out


def _reduce_grads(full_grads, chip, core, big=BIG):
    pair_sums = _pair_stage(full_grads, core, big)
    parts = {n: _own_slab(_chip_exchange("chip_exchange_" + n, pair_sums[n]), pair_sums[n], chip) for n, _, _, _ in big}
    return _finish_reduce(parts, full_grads[big[0][0]].shape[0], core, big)


HBM_SPEC = pl.BlockSpec(memory_space=pltpu.HBM)
SEM_SPEC = pl.BlockSpec(memory_space=pltpu.SEMAPHORE)
DATAFLOW = pltpu.SideEffectType.DATAFLOW_SIDE_EFFECTING


def _exchange_copies(p_ref, land_ref, sems, waiting, spread):
    x, y, c = _place()
    me = 2 * x + y
    out = []
    for j, (px, py) in enumerate([(1 - x, y), (x, 1 - y), (1 - x, 1 - y)]):
        out.append(pltpu.make_async_remote_copy(src_ref=p_ref if spread else p_ref.at[2 * px + py],
                                                dst_ref=land_ref.at[2 * px + py if waiting else me],
                                                send_sem=sems[j], recv_sem=sems[3 + j], device_id=(px, py, c),
                                                device_id_type=MESH))
    return out


def _exchange_start(name, parts, after, spread=False):
    land_shape = ((N_CHIP,) + parts.shape) if spread else parts.shape

    def body(p_ref, land_ref, after_ref, s0, s1, s2, r0, r1, r2, p_thru, land_thru, token):
        for cp in _exchange_copies(p_ref, land_ref, (s0, s1, s2, r0, r1, r2), False, spread):
            cp.start()
        token[...] = jnp.zeros_like(token)

    res = _pc(body, name=name,
              out_shape=(pltpu.SemaphoreType.DMA(()),) * 6 + (pltpu.HBM(parts.shape, parts.dtype), pltpu.HBM(land_shape, parts.dtype),
                                                            jax.ShapeDtypeStruct((8, LANES), F32)),
              in_specs=(HBM_SPEC, HBM_SPEC, ANY),
              out_specs=(SEM_SPEC,) * 6 + (HBM_SPEC, HBM_SPEC, pl.BlockSpec(memory_space=pltpu.VMEM)),
              input_output_aliases={0: 6, 1: 7}, compiler_params=pltpu.CompilerParams(has_side_effects=DATAFLOW))(
        pltpu.with_memory_space_constraint(parts, pltpu.HBM),
        pltpu.with_memory_space_constraint(lax.empty(land_shape, parts.dtype), pltpu.HBM), after)
    return res[:6], res[6], res[7], res[8]


def _exchange_wait(name, sems, p_thru, land_thru, after, spread=False):
    def body(p_ref, land_ref, s0, s1, s2, r0, r1, r2, after_ref, p_dead, got_ref):
        for cp in _exchange_copies(p_ref, land_ref, (s0, s1, s2, r0, r1, r2), True, spread):
            cp.wait_send()
            cp.wait_recv()

    return _pc(body, name=name, out_shape=(pltpu.HBM(p_thru.shape, p_thru.dtype), pltpu.HBM(land_thru.shape, land_thru.dtype)),
               in_specs=(HBM_SPEC, HBM_SPEC) + (SEM_SPEC,) * 6 + (ANY,), out_specs=(HBM_SPEC, HBM_SPEC),
               input_output_aliases={0: 0, 1: 1}, compiler_params=pltpu.CompilerParams(has_side_effects=DATAFLOW))(
        p_thru, land_thru, *sems, after)[1]


def _regroup_w_in(w):
    cat = lambda spans: jnp.concatenate([w[:, a:b] for a, b in spans], axis=1)
    small = jnp.concatenate([cat(GSMALL), jnp.zeros((w.shape[0], SMALL_W - 24), w.dtype)], axis=1)
    return cat(G768), cat((GXBC,)), cat((GQKV,)), cat((GGATE,)), small


def _ungroup_w_in(d):
    o_xbc, o_qkv, o_gate, o_small = W768, W768 + WXBC, W768 + WXBC + QKV_W, W768 + WXBC + QKV_W + WGATE
    spans = ((0, 3072), (3072, 3840), (o_xbc, o_xbc + WXBC), (o_small, o_small + 12), (o_qkv, o_qkv + QKV_W),
             (3840, 4608), (o_small + 12, o_small + 18), (o_small + 18, o_small + 24), (o_gate, o_gate + WGATE))
    return jnp.concatenate([d[:, a:b] for a, b in spans], axis=1)


def _lane_pad(v, off):
    return jnp.pad(v, (off, LANES - off - v.shape[0]))[None, :]


STATE6 = (N_HEAD6, HEAD, HEAD)


def _mixer_inputs(sv, lp):
    p768, pxbc, pqkv, psmall = sv["p768"], sv["pxbc"], sv["pqkv"], sv["psmall"]
    hgrn = ([(p768, MIX_W, j) for j in range(4)], [], [lp["lb"], lp["hgrn_norm"]])
    ssd = ([(p768, MIX_W, 4), (pxbc, CONV_CH, 0), (psmall, LANES, 0)], [1],
           [lp["ssm_conv_w"], lp["ssm_conv_b"], lp["ssm_dt_bias"], lp["ssm_a_log"], lp["ssm_d"], lp["ssm_norm"]])
    gdn = ([(pqkv, QKV_W, 0), (p768, MIX_W, 5), (psmall, LANES, 0)], [0],
           [lp["gdn_conv_w"], lp["gdn_dt_bias"], lp["gdn_a_log"], lp["gdn_norm"]])
    return hgrn, ssd, gdn


def _layer_fwd(x, md, lw, lp):
    sv = {"x": x}
    (sv["h1"],) = _tile_fwd("lnmod1", _lnmod_fn, [(x, D_MODEL, 0)], [lp["norm_mix"], md["sc1"], md["sh1"]], [(D_MODEL, BF)])
    for nm in ("768", "xbc", "qkv", "gate", "small"):
        sv["p" + nm] = _mm(sv["h1"], lw["win_" + nm], "nn", F32, "proj_" + nm)
    hgrn, ssd, gdn = _mixer_inputs(sv, lp)
    sv["y_h"], sv["st_h"] = _scan_fwd("hgrn_fwd", _hgrn_chunk, *hgrn, MIX_W, STATE6)
    sv["y_s"], sv["st_s"] = _scan_fwd("ssd_fwd", _ssd_chunk, *ssd, MIX_W, STATE6)
    sv["y_g"], sv["st_g"], sv["inv_g"] = _scan_fwd("gdn_fwd", _gdn_chunk, *gdn, MIX_W, STATE6, aux_width=N_HEAD6 * LANES)
    (sv["merged"],) = _tile_fwd("merge", _merge_fn, _merge_tiles(sv), [lw["w_branch"], lp["b_merge"]], [(D_MODEL, BF)],
                                tm=WIDE_TOKEN_TILE)
    sv["out"] = _mm(sv["merged"], lw["w_out"], "nn", F32, "out_proj")
    (sv["x_mid"],) = _tile_fwd("resid1", _resid_fn, [(x, D_MODEL, 0), (sv["out"], D_MODEL, 0)], [md["g1"]], [(D_MODEL, F32)])
    (sv["h2"],) = _tile_fwd("lnmod2", _lnmod_fn, [(sv["x_mid"], D_MODEL, 0)], [lp["norm_ffn"], md["sc2"], md["sh2"]],
                            [(D_MODEL, BF)])
    sv["gu"] = _mm(sv["h2"], lw["w_ffn_in"], "nn", F32, "ffn_in")
    (sv["act"],) = _tile_fwd("swiglu", _swiglu_fn, [(sv["gu"], 2 * FFN_H, 0)], [], [(FFN_H, BF)], tm=WIDE_TOKEN_TILE)
    sv["o2"] = _mm(sv["act"], lw["w_ffn_out"], "nn", F32, "ffn_out")
    (x_out,) = _tile_fwd("resid2", _resid_fn, [(sv["x_mid"], D_MODEL, 0), (sv["o2"], D_MODEL, 0)], [md["g2"]], [(D_MODEL, F32)])
    return x_out, sv


def _merge_tiles(sv):
    return [(sv["y_h"], MIX_W, 0), (sv["y_s"], MIX_W, 0), (sv["y_g"], MIX_W, 0), (sv["pgate"], WGATE, 0)]


def _layer_bwd(dx_out, sv, md, lw, lp):
    g = {}
    x, x_mid = sv["x"], sv["x_mid"]
    d_o2, g["g2"] = _tile_bwd("resid2_b", _gated_fn, [(sv["o2"], D_MODEL, 0)], [md["g2"]], [dx_out], [BF])
    d_xmid = dx_out
    d_act =_mm(d_o2, lw["w_ffn_out"], "nt", F32, "ffn_out_dx")
    g["w_ffn_out"] = _mm(sv["act"], d_o2, "tn", BF, "ffn_out_dw")
    (d_gu,) = _tile_bwd("swiglu_b", _swiglu_fn, [(sv["gu"], 2 * FFN_H, 0)], [], [d_act], [BF], tm=WIDE_TOKEN_TILE)
    d_h2 = _mm(d_gu, lw["w_ffn_in"], "nt", F32, "ffn_in_dx")
    g["w_ffn_in"] = _mm(sv["h2"], d_gu, "tn", BF, "ffn_in_dw")
    d_xmid, g["norm_ffn"], g["sc2"], g["sh2"] = _tile_bwd(
        "lnmod2_b", _lnmod_fn, [(x_mid, D_MODEL, 0)], [lp["norm_ffn"], md["sc2"], md["sh2"]], [d_h2], [F32], add_to=(0, d_xmid))
    d_out, g["g1"] = _tile_bwd("resid1_b", _gated_fn, [(sv["out"], D_MODEL, 0)], [md["g1"]], [d_xmid], [BF])
    d_x = d_xmid
    d_merged = _mm(d_out, lw["w_out"], "nt", F32, "out_proj_dx")
    g["w_out"] = _mm(sv["merged"], d_out, "tn", BF, "out_proj_dw")
    d_yh, d_ys, d_yg, d_gate, g["w_branch"], g["b_merge"] = _tile_bwd(
        "merge_b", _merge_fn, _merge_tiles(sv), [lw["w_branch"], lp["b_merge"]], [d_merged], [F32, F32, F32, BF],
        tm=WIDE_TOKEN_TILE)
    hgrn, ssd, gdn = _mixer_inputs(sv, lp)
    d_q, d_f, d_v, d_g, g["lb"], g["hgrn_norm"] = _scan_bwd("hgrn_bwd", _hgrn_chunk, *hgrn, sv["st_h"], d_yh, [BF] * 4)
    (d_sz, d_xbc, d_small, g["ssm_conv_w"], g["ssm_conv_b"], g["ssm_dt_bias"], g["ssm_a_log"], g["ssm_d"],
     g["ssm_norm"]) = _scan_bwd("ssd_bwd", _ssd_chunk, *ssd, sv["st_s"], d_ys, [BF, BF, F32])
    d_qkv, d_gz, d_small, g["gdn_conv_w"], g["gdn_dt_bias"], g["gdn_a_log"], g["gdn_norm"] = _scan_bwd(
        "gdn_bwd", _gdn_chunk, gdn[0] + [(sv["inv_g"], N_HEAD6 * LANES, 0)], gdn[1], gdn[2], sv["st_g"], d_yg,
        [BF, BF, BF, None], extra=(2, d_small))
    d_proj = jnp.concatenate([d_q, d_f, d_v, d_g, d_sz, d_gz, d_xbc, d_qkv, d_gate, d_small,
                              jnp.zeros((x.shape[0], SMALL_W - LANES), BF)], axis=1)
    d_h1 = _mm(d_proj, lw["win_all"], "nt", F32, "proj_dx")
    g["w_in"] = _ungroup_w_in(_mm(sv["h1"], d_proj, "tn", BF, "proj_dw"))
    d_x, g["norm_mix"], g["sc1"], g["sh1"] = _tile_bwd(
        "lnmod1_b", _lnmod_fn, [(x, D_MODEL, 0)], [lp["norm_mix"], md["sc1"], md["sh1"]], [d_h1], [F32], add_to=(0, d_x))
    return d_x, g


SMALL_REPL = ("norm_mix", "norm_ffn", "b_merge", "hgrn_lb_logits", "hgrn_norm", "ssm_conv_w", "ssm_conv_b", "ssm_dt_bias",
              "ssm_a_log", "ssm_d", "ssm_norm", "gdn_conv_w", "gdn_dt_bias", "gdn_a_log", "gdn_norm", "norm_final")
WEIGHTS = ("w_ada", "b_ada", "norm_mix", "norm_ffn", "w_in", "b_merge", "hgrn_lb_logits", "hgrn_norm", "ssm_conv_w",
           "ssm_conv_b", "ssm_dt_bias", "ssm_a_log", "ssm_d", "ssm_norm", "gdn_conv_w", "gdn_dt_bias", "gdn_a_log",
           "gdn_norm", "w_branch", "w_out", "w_ffn_in", "w_ffn_out", "norm_final")
SMALL_ROWS = 120


def _pad_rows(flat, n_rows, n_cols):
    return jnp.concatenate([flat, jnp.zeros((n_rows * n_cols - flat.shape[0],), flat.dtype)]).reshape(n_rows, n_cols)


def _device_step(x, tgt, mod, lb, wfull, sp, chip=None, core=None, order_after=None):
    mds, lps, svs = [], [], []
    h = x
    weights_of = wfull if callable(wfull) else (lambda layer, after: wfull[layer])
    wfull = []
    for l in range(DEPTH):
        wfull.append(weights_of(l, h))
        md = {n: mod[l, i * D_MODEL:(i + 1) * D_MODEL][None, :] for i, n in enumerate(("sh1", "sc1", "g1", "sh2", "sc2", "g2"))}
        if l == 0 and order_after is not None:
            md["sc1"] = md["sc1"] + order_after
        lp = {n: sp[n][l][None, :] for n in ("norm_mix", "norm_ffn", "b_merge", "hgrn_norm", "ssm_conv_b", "ssm_norm", "gdn_norm")}
        lp["lb"] = lb[l][None, :]
        lp["ssm_conv_w"], lp["gdn_conv_w"] = sp["ssm_conv_w"][l], sp["gdn_conv_w"][l]
        for n in ("ssm_dt_bias", "ssm_a_log", "ssm_d"):
            lp[n] = _lane_pad(sp[n][l], DT_OFF)
        for n in ("gdn_dt_bias", "gdn_a_log"):
            lp[n] = _lane_pad(sp[n][l], GA_OFF)
        h, sv = _layer_fwd(h, md, wfull[l], lp)
        mds.append(md), lps.append(lp), svs.append(sv)
    loss, dh, d_nf = _final_loss(h, tgt, sp["norm_final"][None, :])
    grads = [None] * DEPTH
    if core is None:
        for l in reversed(range(DEPTH)):
            dh, grads[l] = _layer_bwd(dh, svs[l], mds[l], wfull[l], lps[l])
        return loss, dh, d_nf, grads
    names = [n for n, _, _, _ in BIG]
    landed, flying = [None] * DEPTH, None
    for l in reversed(range(DEPTH)):
        md = mds[l]
        if flying is not None:
            md = dict(md, g2=md["g2"] + sum(tok[0, 0] for _, _, _, tok in flying.values()))
        dh, grads[l] = _layer_bwd(dh, svs[l], md, wfull[l], lps[l])
        if flying is not None:
            landed[l + 1] = {n: _own_slab(_exchange_wait(f"exchange_wait_{n}_{l + 1}", *flying[n][:3], dh), sums[n], chip)
                             for n in names}
        sums = _pair_stage({n: grads[l][n].astype(BF)[None] for n in names}, core)
        if l > 0:
            flying = {n: _exchange_start(f"exchange_start_{n}_{l}", sums[n], dh) for n in names}
        else:
            landed[0] = {n: _own_slab(_chip_exchange("chip_exchange_" + n, sums[n]), sums[n], chip) for n in names}
    parts = {n: jnp.concatenate([landed[l][n] for l in range(DEPTH)], axis=1) for n in names}
    return loss, dh, d_nf, grads, _finish_reduce(parts, DEPTH, core)


def kernel(x, c, w_ada, b_ada, norm_mix, norm_ffn, w_in, b_merge, hgrn_lb_logits, hgrn_norm, ssm_conv_w, ssm_conv_b, ssm_dt_bias, ssm_a_log, ssm_d, ssm_norm, gdn_conv_w, gdn_dt_bias, gdn_a_log, gdn_norm, w_branch, w_out, w_ffn_in, w_ffn_out, norm_final, loss_target, m_w_ada, m_b_ada, m_norm_mix, m_norm_ffn, m_w_in, m_b_merge, m_hgrn_lb_logits, m_hgrn_norm, m_ssm_conv_w, m_ssm_conv_b, m_ssm_dt_bias, m_ssm_a_log, m_ssm_d, m_ssm_norm, m_gdn_conv_w, m_gdn_dt_bias, m_gdn_a_log, m_gdn_norm, m_w_branch, m_w_out, m_w_ffn_in, m_w_ffn_out, m_norm_final, v_w_ada, v_b_ada, v_norm_mix, v_norm_ffn, v_w_in, v_b_merge, v_hgrn_lb_logits, v_hgrn_norm, v_ssm_conv_w, v_ssm_conv_b, v_ssm_dt_bias, v_ssm_a_log, v_ssm_d, v_ssm_norm, v_gdn_conv_w, v_gdn_dt_bias, v_gdn_a_log, v_gdn_norm, v_w_branch, v_w_out, v_w_ffn_in, v_w_ffn_out, v_norm_final):
    w = dict(w_ada=w_ada, b_ada=b_ada, norm_mix=norm_mix, norm_ffn=norm_ffn, w_in=w_in, b_merge=b_merge,
             hgrn_lb_logits=hgrn_lb_logits, hgrn_norm=hgrn_norm, ssm_conv_w=ssm_conv_w, ssm_conv_b=ssm_conv_b,
             ssm_dt_bias=ssm_dt_bias, ssm_a_log=ssm_a_log, ssm_d=ssm_d, ssm_norm=ssm_norm, gdn_conv_w=gdn_conv_w,
             gdn_dt_bias=gdn_dt_bias, gdn_a_log=gdn_a_log, gdn_norm=gdn_norm, w_branch=w_branch, w_out=w_out,
             w_ffn_in=w_ffn_in, w_ffn_out=w_ffn_out, norm_final=norm_final)
    m = dict(w_ada=m_w_ada, b_ada=m_b_ada, norm_mix=m_norm_mix, norm_ffn=m_norm_ffn, w_in=m_w_in, b_merge=m_b_merge,
             hgrn_lb_logits=m_hgrn_lb_logits, hgrn_norm=m_hgrn_norm, ssm_conv_w=m_ssm_conv_w, ssm_conv_b=m_ssm_conv_b,
             ssm_dt_bias=m_ssm_dt_bias, ssm_a_log=m_ssm_a_log, ssm_d=m_ssm_d, ssm_norm=m_ssm_norm, gdn_conv_w=m_gdn_conv_w,
             gdn_dt_bias=m_gdn_dt_bias, gdn_a_log=m_gdn_a_log, gdn_norm=m_gdn_norm, w_branch=m_w_branch, w_out=m_w_out,
             w_ffn_in=m_w_ffn_in, w_ffn_out=m_w_ffn_out, norm_final=m_norm_final)
    v = dict(w_ada=v_w_ada, b_ada=v_b_ada, norm_mix=v_norm_mix, norm_ffn=v_norm_ffn, w_in=v_w_in, b_merge=v_b_merge,
             hgrn_lb_logits=v_hgrn_lb_logits, hgrn_norm=v_hgrn_norm, ssm_conv_w=v_ssm_conv_w, ssm_conv_b=v_ssm_conv_b,
             ssm_dt_bias=v_ssm_dt_bias, ssm_a_log=v_ssm_a_log, ssm_d=v_ssm_d, ssm_norm=v_ssm_norm, gdn_conv_w=v_gdn_conv_w,
             gdn_dt_bias=v_gdn_dt_bias, gdn_a_log=v_gdn_a_log, gdn_norm=v_gdn_norm, w_branch=v_w_branch, w_out=v_w_out,
             w_ffn_in=v_w_ffn_in, w_ffn_out=v_w_ffn_out, norm_final=v_norm_final)
    xi, yi, ci = _place()
    chip, me = 2 * xi + yi, 4 * xi + 2 * yi + ci
    seq = x.shape[1]

    conv_flat = jnp.concatenate([ssm_conv_w.reshape(-1), gdn_conv_w.reshape(-1)])
    n_conv = conv_flat.shape[0]
    first = _all_gather_small("gather_c_conv", _pad_rows(jnp.concatenate([c[0], conv_flat]), 16, D_MODEL))
    c_all = first[:, 0, :]
    conv_all = first[0::2].reshape(N_CHIP, -1)[:, D_MODEL:D_MODEL + n_conv]
    n_ssm = ssm_conv_w.size
    sp = dict(w)
    sp["ssm_conv_w"] = jnp.concatenate([conv_all[j, :n_ssm].reshape(ssm_conv_w.shape) for j in range(N_CHIP)], axis=2)
    sp["gdn_conv_w"] = jnp.concatenate([conv_all[j, n_ssm:].reshape(gdn_conv_w.shape) for j in range(N_CHIP)], axis=2)

    ada_cols = w_ada.shape[2]
    mod_part = _ada_fwd(c_all, w_ada, lax.dynamic_slice_in_dim(b_ada, chip * ada_cols, ada_cols, axis=1))
    mod_all = _all_gather_small("gather_mod", mod_part.reshape(DEPTH * 8, ada_cols))[0::2].reshape(N_CHIP, DEPTH, 8, ada_cols)
    mod = lax.dynamic_index_in_dim(mod_all, me, axis=2, keepdims=False).transpose(1, 0, 2).reshape(DEPTH, N_CHIP * ada_cols)
    lb = _lb_fwd(hgrn_lb_logits)

    names = [n for n, _, _, _ in BIG]

    def layer_weights(full):
        full = dict(full)
        for nm, part in zip(("768", "xbc", "qkv", "gate", "small"), _regroup_w_in(full["w_in"])):
            full["win_" + nm] = part
        full["win_all"] = jnp.concatenate([full["win_" + nm] for nm in ("768", "xbc", "qkv", "gate", "small")], axis=1)
        return full

    first = _gather_weights({n: w[n][0:1] for n in names}, chip)
    first = layer_weights({n: first[n][0] for n in names})
    started = {}
    settled = mod[0:1, 0:LANES]
    for n in names:
        settled = settled + first[n].reshape(-1, first[n].shape[-1])[0:1, 0:LANES].astype(F32)
    for n, _, _, (rows, cols) in BIG:
        mine = w[n][1:].astype(BF).reshape((DEPTH - 1, rows, cols))
        started[n] = _exchange_start("gather_start_" + n, lax.dynamic_slice_in_dim(mine, ci * (rows // 2), rows // 2, axis=1),
                                     settled, spread=True)
    rest = []

    def weights_of(layer, after):
        if layer == 0:
            return first
        if not rest:
            full = {}
            for n, shape, ax, _ in BIG:
                sems, my_half, landing, _ = started[n]
                landed = _exchange_wait("gather_wait_" + n, sems, my_half, landing, after, spread=True)
                landed = lax.dynamic_update_slice(landed, my_half[None], (chip, 0, 0, 0))
                theirs = _pair_swap("gather_share_" + n, landed)
                slabs = jnp.concatenate([jnp.where(ci == 0, landed, theirs), jnp.where(ci == 0, theirs, landed)], axis=2)
                full[n] = _join_shards(slabs, ax, shape)
            rest.extend(layer_weights({n: full[n][i] for n in names}) for i in range(DEPTH - 1))
        return rest[layer - 1]

    order = sum(tok[0, 0] for _, _, _, tok in started.values())
    loss8, d_x, d_nf, lg, grad = _device_step(x[0], loss_target[0], mod, lb, weights_of, sp, chip, ci, order)

    dmod = jnp.stack([jnp.concatenate([lg[l][n] for n in ("sh1", "sc1", "g1", "sh2", "sc2", "g2")], axis=1)[0] for l in range(DEPTH)])
    d_lb = jnp.stack([lg[l]["lb"][0] for l in range(DEPTH)])
    contrib = {
        "norm_mix": jnp.stack([lg[l]["norm_mix"][0] for l in range(DEPTH)]),
        "norm_ffn": jnp.stack([lg[l]["norm_ffn"][0] for l in range(DEPTH)]),
        "b_merge": jnp.stack([lg[l]["b_merge"][0] for l in range(DEPTH)]),
        "hgrn_lb_logits": _lb_bwd(hgrn_lb_logits, d_lb),
        "hgrn_norm": jnp.stack([lg[l]["hgrn_norm"][0] for l in range(DEPTH)]),
        "ssm_conv_w": jnp.stack([lg[l]["ssm_conv_w"] for l in range(DEPTH)]),
        "ssm_conv_b": jnp.stack([lg[l]["ssm_conv_b"][0] for l in range(DEPTH)]),
        "ssm_dt_bias": jnp.stack([lg[l]["ssm_dt_bias"][0, DT_OFF:DT_OFF + 12] for l in range(DEPTH)]),
        "ssm_a_log": jnp.stack([lg[l]["ssm_a_log"][0, DT_OFF:DT_OFF + 12] for l in range(DEPTH)]),
        "ssm_d": jnp.stack([lg[l]["ssm_d"][0, DT_OFF:DT_OFF + 12] for l in range(DEPTH)]),
        "ssm_norm": jnp.stack([lg[l]["ssm_norm"][0] for l in range(DEPTH)]),
        "gdn_conv_w": jnp.stack([lg[l]["gdn_conv_w"] for l in range(DEPTH)]),
        "gdn_dt_bias": jnp.stack([lg[l]["gdn_dt_bias"][0, GA_OFF:GA_OFF + 6] for l in range(DEPTH)]),
        "gdn_a_log": jnp.stack([lg[l]["gdn_a_log"][0, GA_OFF:GA_OFF + 6] for l in range(DEPTH)]),
        "gdn_norm": jnp.stack([lg[l]["gdn_norm"][0] for l in range(DEPTH)]),
        "norm_final": d_nf[0],
    }
    flat = jnp.concatenate([dmod.reshape(-1)] + [contrib[n].reshape(-1) for n in SMALL_REPL] + [loss8[0, 0:1]])
    small_all = _all_gather_small("gather_small_grads", _pad_rows(flat, SMALL_ROWS, D_MODEL))
    total = _sum_leading("small_grad_sum", small_all, F32).reshape(-1)
    n_mod = dmod.size
    grad["b_ada"] = total[:n_mod].reshape(b_ada.shape)
    off = n_mod
    full_small = {}
    for n in SMALL_REPL:
        full_small[n] = total[off:off + contrib[n].size].reshape(contrib[n].shape)
        off += contrib[n].size
    loss = total[off]
    for n in SMALL_REPL:
        if n in ("ssm_conv_w", "gdn_conv_w"):
            cols = w[n].shape[2]
            grad[n] = lax.dynamic_slice_in_dim(full_small[n], chip * cols, cols, axis=2)
        else:
            grad[n] = full_small[n]
    dmod_cols = lax.dynamic_slice_in_dim(small_all[:, :n_mod // D_MODEL, :].reshape(8, DEPTH, -1), chip * ada_cols, ada_cols, axis=2)
    grad["w_ada"] = _ada_bwd(c_all, dmod_cols.transpose(1, 0, 2))

    delta, new_m, new_v = {}, {}, {}
    big_names = ("w_ada",) + tuple(n for n, _, _, _ in BIG)
    for n in big_names:
        delta[n], new_m[n], new_v[n] = _adamw("adamw_" + n, w[n], grad[n], m[n], v[n])
    small_names = [n for n in WEIGHTS if n not in big_names]
    packs = [_pad_rows(jnp.concatenate([d[n].reshape(-1) for n in small_names]), 584, LANES) for d in (w, grad, m, v)]
    outs = _ew("adamw_small", _adamw_fn, packs, [F32, F32, F32])
    off = 0
    for n in small_names:
        for dst, o in zip((delta, new_m, new_v), outs):
            dst[n] = o.reshape(-1)[off:off + w[n].size].reshape(w[n].shape)
        off += w[n].size
    return (loss, d_x[None], *[grad[n] for n in WEIGHTS], *[delta[n] for n in WEIGHTS], *[new_m[n] for n in WEIGHTS],
            *[new_v[n] for n in WEIGHTS])
```

```python
import functools

import jax
import jax.numpy as jnp
from jax import lax
from jax.experimental import pallas as pl
from jax.experimental.pallas import tpu as pltpu

F32 = jnp.float32
BF = jnp.bfloat16
HI = lax.Precision.HIGHEST

D_MODEL = 1024
DEPTH = 4
CHUNK = 64
MIX_W = 768
HEAD = 128
N_HEAD6 = 6
SSM_P = 64
SSM_N = 128
CONV_CH = 1280
QKV_W = 2304
FFN_H = 2816
IN_WIDTH = 11288
NORM_EPS = 1e-6
F_MIN = 1e-30
HALO = 8
HEAD_GROUP = 6
HGRN_SUB = 8
SMALL_W = 512
LANES = 128
DT_OFF, GB_OFF, GA_OFF = 0, 12, 18

ADAM_LR, ADAM_B1, ADAM_B2, ADAM_EPS, ADAM_WD, ADAM_STEP = 0.001, 0.9, 0.999, 1e-08, 0.01, 10

VMEM_LIMIT = 56 * 1024 * 1024
TOKEN_TILE = 512
WIDE_TOKEN_TILE = 256


def _pc(body, **kw):
    return pl.pallas_call(body, **kw)


def _cparams(sem):
    return pltpu.CompilerParams(dimension_semantics=sem, vmem_limit_bytes=VMEM_LIMIT)


def _bdot(a, b):
    return jnp.dot(a.astype(BF), b.astype(BF), preferred_element_type=F32)


def _bdot_nt(a, b):
    return lax.dot_general(a.astype(BF), b.astype(BF), (((1,), (1,)), ((), ())), preferred_element_type=F32)


def _bdot_tn(a, b):
    return lax.dot_general(a.astype(BF), b.astype(BF), (((0,), (0,)), ((), ())), preferred_element_type=F32)


def _silu(x):
    return x * jax.nn.sigmoid(x)


def _tri_mask(n, strict=False):
    t = lax.broadcasted_iota(jnp.int32, (n, n), 0)
    s = lax.broadcasted_iota(jnp.int32, (n, n), 1)
    return (s < t) if strict else (s <= t)


def _masked_exp(diff, mask):
    return jnp.where(mask, jnp.exp(jnp.where(mask, diff, 0.0)), 0.0)


def _split_bf16(x, n):
    parts, rest = [], x
    for _ in range(n):
        p = rest.astype(BF)
        parts.append(p)
        rest = rest - p.astype(F32)
    return parts


def _tri_sum(x, reverse):
    n, w = x.shape
    t = lax.broadcasted_iota(jnp.int32, (n, n), 0)
    s = lax.broadcasted_iota(jnp.int32, (n, n), 1)
    tri = jnp.where((s >= t) if reverse else (s <= t), 1.0, 0.0).astype(BF)
    y = jnp.dot(tri, jnp.concatenate(_split_bf16(x, 3), axis=1), preferred_element_type=F32)
    return y[:, :w] + y[:, w:2 * w] + y[:, 2 * w:]


@jax.custom_vjp
def _cumsum_rows(x):
    return _tri_sum(x, False)


_cumsum_rows.defvjp(lambda x: (_tri_sum(x, False), None), lambda _, g: (_tri_sum(g, True),))


def _dot_split(a, b, transpose_a=False):
    dims = (((0,), (0,)) if transpose_a else ((1,), (0,)), ((), ()))
    a_hi, a_lo = _split_bf16(a, 2)
    b_hi, b_lo = _split_bf16(b, 2)
    w = b.shape[1]
    y = lax.dot_general(a_hi, jnp.concatenate([b_hi, b_lo], axis=1), dims, preferred_element_type=F32)
    return y[:, :w] + y[:, w:] + lax.dot_general(a_lo, b_hi, dims, preferred_element_type=F32)


def _rms(x, w):
    return x * lax.rsqrt(jnp.mean(x * x, axis=-1, keepdims=True) + NORM_EPS) * w


def _causal_conv(halo, x, w):
    ext = jnp.concatenate([halo, x], axis=0)
    n = x.shape[0]
    acc = w[0:1, :] * ext[HALO - 3:HALO - 3 + n, :]
    for i in range(1, 4):
        acc = acc + w[i:i + 1, :] * ext[HALO - 3 + i:HALO - 3 + i + n, :]
    return acc


def _unit_lower_inverses(mats):
    n = mats[0].shape[0]
    t = lax.broadcasted_iota(jnp.int32, (n, LANES), 0)
    s_ = lax.broadcasted_iota(jnp.int32, (n, LANES), 1)
    xs = [jnp.where(t == s_, 1.0, 0.0).astype(F32) for _ in mats]
    for s in range(n - 1):
        r0 = 8 * ((s + 1) // 8)
        for i, a in enumerate(mats):
            x = xs[i]
            low = x[r0:] - a[r0:, s:s + 1] * x[s:s + 1, :]
            xs[i] = low if r0 == 0 else jnp.concatenate([x[:r0], low], axis=0)
    return xs


@jax.custom_vjp
def _solves_with_inverses(invs, mats, rhss):
    return [_dot_split(inv, r) for inv, r in zip(invs, rhss)]


def _swi_fwd(invs, mats, rhss):
    xs = [_dot_split(inv, r) for inv, r in zip(invs, rhss)]
    return xs, (invs, xs)


def _swi_bwd(res, gs):
    invs, xs = res
    ys = [_dot_split(inv, g, transpose_a=True) for inv, g in zip(invs, gs)]
    das = [jnp.where(_tri_mask(CHUNK, strict=True), -_bdot_nt(y, x), 0.0) for y, x in zip(ys, xs)]
    return [jnp.zeros_like(inv) for inv in invs], das, ys


_solves_with_inverses.defvjp(_swi_fwd, _swi_bwd)


def _hgrn_chunk(tiles, halos, state, consts):
    q_raw, f_raw, v_all, g_raw = tiles
    lb, norm_w = consts
    q_all = _silu(q_raw)
    f = lb + (1.0 - lb) * jax.nn.sigmoid(f_raw)
    logf = jnp.log(jnp.maximum(f, F_MIN))
    k_all = (1.0 - lb) * jax.nn.sigmoid(-f_raw)
    b_all = _cumsum_rows(logf)
    sub = HGRN_SUB
    row = lax.broadcasted_iota(jnp.int32, (sub, 1), 0)
    src_row = lax.broadcasted_iota(jnp.int32, (CHUNK, 1), 0)
    src_lane = lax.broadcasted_iota(jnp.int32, (1, CHUNK), 1)
    heads = range(N_HEAD6)
    n_sub = CHUNK // sub
    cols = [slice(h * HEAD, (h + 1) * HEAD) for h in heads]
    qs, ks, vs, bs = ([a[:, sl] for sl in cols] for a in (q_all, k_all, v_all, b_all))
    o_inter = [_bdot_nt(qs[h] * jnp.exp(bs[h]), state[h]) for h in heads]
    blocks = [[None] * n_sub for _ in heads]
    for i in range(n_sub):
        r0 = i * sub
        for h in heads:
            if i > 0:
                ref = bs[h][r0 - 1:r0, :]
                blocks[h][i] = _bdot_nt(qs[h][r0:r0 + sub] * jnp.exp(bs[h][r0:r0 + sub] - ref),
                                        ks[h] * _masked_exp(ref - bs[h], src_row < r0))
            else:
                blocks[h][i] = jnp.zeros((sub, CHUNK), F32)
    for h in heads:
        for i in range(n_sub):
            r0 = i * sub
            qi, ki, bi = qs[h][r0:r0 + sub], ks[h][r0:r0 + sub], bs[h][r0:r0 + sub]
            for s in range(sub):
                e = _masked_exp(bi - bi[s:s + 1, :], row >= s)
                col = jnp.sum(qi * ki[s:s + 1, :] * e, axis=1, keepdims=True)
                blocks[h][i] = jnp.where(src_lane == r0 + s, col, blocks[h][i])
    os_ = [_bdot(jnp.concatenate(blocks[h], axis=0), vs[h]) + o_inter[h] for h in heads]
    ends = [bs[h][CHUNK - 1:CHUNK, :] for h in heads]
    new_states = [state[h] * jnp.exp(ends[h]) + _bdot_tn(vs[h], ks[h] * jnp.exp(ends[h] - bs[h])) for h in heads]
    outs = [_rms(os_[h], norm_w) * _silu(g_raw[:, cols[h]]) for h in heads]
    return (jnp.concatenate(outs, axis=1),), jnp.stack(new_states)


def _ssd_chunk(tiles, halos, state, consts):
    z, xbc_raw, small = tiles
    (halo,) = halos
    conv_w, conv_b, dt_bias, a_log, d_skip, norm_w = consts
    xbc = _silu(_causal_conv(halo, xbc_raw, conv_w) + conv_b)
    xs, bm, cm = xbc[:, :MIX_W], xbc[:, MIX_W:MIX_W + 2 * SSM_N], xbc[:, MIX_W + 2 * SSM_N:]
    dt = jax.nn.softplus(small + dt_bias)
    cum = _cumsum_rows(-jnp.exp(a_log) * dt)
    cum_t2 = jnp.concatenate([cum, cum], axis=0).T
    lane = lax.broadcasted_iota(jnp.int32, (1, LANES), 1)
    first = lane < SSM_P
    hm0 = jnp.where(first, 1.0, 0.0).astype(F32)
    hm1 = 1.0 - hm0
    src = jnp.where(first, lane, lane - SSM_P)
    tri2 = src <= lax.broadcasted_iota(jnp.int32, (CHUNK, 1), 0)
    pick = lambda a, b: jnp.where(first, a, b)
    bgs = [bm[:, g * SSM_N:(g + 1) * SSM_N] for g in range(2)]
    cgs = [cm[:, g * SSM_N:(g + 1) * SSM_N] for g in range(2)]
    gmats = [_bdot_nt(cgs[g], jnp.concatenate([bgs[g], bgs[g]], axis=0)) for g in range(2)]
    pairs = range(6)
    xps = [xs[:, p * LANES:(p + 1) * LANES] for p in pairs]
    c0s = [cum[:, 2 * p:2 * p + 1] for p in pairs]
    c1s = [cum[:, 2 * p + 1:2 * p + 2] for p in pairs]
    e0s = [cum[CHUNK - 1:CHUNK, 2 * p:2 * p + 1] for p in pairs]
    e1s = [cum[CHUNK - 1:CHUNK, 2 * p + 1:2 * p + 2] for p in pairs]
    segs = [_masked_exp(pick(c0s[p], c1s[p]) - pick(cum_t2[2 * p:2 * p + 1, :], cum_t2[2 * p + 1:2 * p + 2, :]), tri2)
            for p in pairs]
    vms = []
    for p in pairs:
        v = xps[p] * pick(dt[:, 2 * p:2 * p + 1], dt[:, 2 * p + 1:2 * p + 2])
        vms.append(jnp.concatenate([v * hm0, v * hm1], axis=0))
    y_intra = [_bdot(gmats[p // 3] * segs[p], vms[p]) for p in pairs]
    y_inter = [_bdot(jnp.concatenate([cgs[p // 3] * jnp.exp(c0s[p]), cgs[p // 3] * jnp.exp(c1s[p])], axis=1),
                     jnp.concatenate([state[p] * hm0, state[p] * hm1], axis=0)) for p in pairs]
    new_states = [_bdot_tn(jnp.concatenate([bgs[p // 3] * jnp.exp(e0s[p] - c0s[p]),
                                            bgs[p // 3] * jnp.exp(e1s[p] - c1s[p])], axis=0), vms[p])
                  + state[p] * pick(jnp.exp(e0s[p]), jnp.exp(e1s[p])) for p in pairs]
    ys = [y_intra[p] + y_inter[p] + pick(d_skip[:, 2 * p:2 * p + 1], d_skip[:, 2 * p + 1:2 * p + 2]) * xps[p] for p in pairs]
    y = jnp.concatenate(ys, axis=1) * _silu(z)
    gw = MIX_W // 2
    y = jnp.concatenate([_rms(y[:, g * gw:(g + 1) * gw], norm_w[:, g * gw:(g + 1) * gw]) for g in range(2)], axis=1)
    return (y,), jnp.stack(new_states)


def _gdn_chunk(tiles, halos, state, consts):
    qkv_raw, z, small = tiles[:3]
    given = tiles[3] if len(tiles) > 3 else None
    (halo,) = halos
    conv_w, dt_bias, a_log, norm_w = consts
    qkv = _silu(_causal_conv(halo, qkv_raw, conv_w))
    beta_all = jax.nn.sigmoid(small)
    cum = _cumsum_rows(-jnp.exp(a_log) * jax.nn.softplus(small + dt_bias))
    cum_t = cum.T
    tri, tri_strict = _tri_mask(CHUNK), _tri_mask(CHUNK, strict=True)

    def group(hs):
        n = range(len(hs))
        qs, ks, betas, cs, ces, decays, rhss = [], [], [], [], [], [], []
        for h in hs:
            q = qkv[:, h * HEAD:(h + 1) * HEAD]
            k = qkv[:, MIX_W + h * HEAD:MIX_W + (h + 1) * HEAD]
            v = qkv[:, 2 * MIX_W + h * HEAD:2 * MIX_W + (h + 1) * HEAD]
            q = q * lax.rsqrt(jnp.sum(q * q, axis=-1, keepdims=True) + NORM_EPS) * (HEAD ** -0.5)
            k = k * lax.rsqrt(jnp.sum(k * k, axis=-1, keepdims=True) + NORM_EPS)
            beta = beta_all[:, GB_OFF + h:GB_OFF + h + 1]
            c, c_t = cum[:, GA_OFF + h:GA_OFF + h + 1], cum_t[GA_OFF + h:GA_OFF + h + 1, :]
            qs.append(q), ks.append(k), betas.append(beta), cs.append(c)
            ces.append(cum[CHUNK - 1:CHUNK, GA_OFF + h:GA_OFF + h + 1])
            decays.append(_masked_exp(c - c_t, tri))
            rhss.append(jnp.concatenate([v * beta, k * (beta * jnp.exp(c))], axis=1))
        sts = [state[h] for h in hs]
        qk_kks = [_bdot_nt(jnp.concatenate([qs[i], ks[i]], axis=0), ks[i]) for i in n]
        mats = [jnp.where(tri_strict, betas[i] * qk_kks[i][CHUNK:] * decays[i], 0.0) for i in n]
        wide = _unit_lower_inverses(mats) if given is None else [given[:, h * LANES:(h + 1) * LANES] for h in hs]
        sols = _solves_with_inverses([inv[:, :CHUNK] for inv in wide], mats, rhss)
        on_states = [_bdot(jnp.concatenate([sols[i][:, HEAD:], qs[i] * jnp.exp(cs[i])], axis=0), sts[i]) for i in n]
        us = [sols[i][:, :HEAD] - on_states[i][:CHUNK] for i in n]
        os_ = [on_states[i][CHUNK:] + _bdot(qk_kks[i][:CHUNK] * decays[i], us[i]) for i in n]
        new = [jnp.exp(ces[i]) * sts[i] + _bdot_tn(ks[i] * jnp.exp(ces[i] - cs[i]), us[i]) for i in n]
        outs = [_rms(os_[i], norm_w) * _silu(z[:, h * HEAD:(h + 1) * HEAD]) for i, h in enumerate(hs)]
        return outs, new, wide

    outs, new_states, inverses = [], [], []
    for h0 in range(0, N_HEAD6, HEAD_GROUP):
        o, s, w = group(list(range(h0, h0 + HEAD_GROUP)))
        outs += o
        new_states += s
        inverses += w
    y = jnp.concatenate(outs, axis=1)
    return ((y,) if given is not None else (y, jnp.concatenate(inverses, axis=1))), jnp.stack(new_states)


def _scan_fwd(name, fn, tiled, halo_idx, consts, out_width, state_shape, aux_width=None):
    seq = tiled[0][0].shape[0]
    nc = seq // CHUNK
    n_t, n_h, n_c = len(tiled), len(halo_idx), len(consts)

    def body(*refs):
        t_refs, h_refs, c_refs = refs[:n_t], refs[n_t:n_t + n_h], refs[n_t + n_h:n_t + n_h + n_c]
        y_ref, save_ref = refs[n_t + n_h + n_c:n_t + n_h + n_c + 2]
        st_ref = refs[-1]
        i = pl.program_id(0)

        @pl.when(i == 0)
        def _():
            st_ref[...] = jnp.zeros_like(st_ref)

        flag = jnp.where(i > 0, 1.0, 0.0).astype(F32)
        st = st_ref[...]
        outs, new = fn([r[...] for r in t_refs], [r[...] * flag for r in h_refs], st, [r[...] for r in c_refs])
        save_ref[0] = st
        y_ref[...] = outs[0].astype(y_ref.dtype)
        if aux_width is not None:
            refs[-2][...] = outs[1]
        st_ref[...] = new

    in_specs = [pl.BlockSpec((CHUNK, w), functools.partial(lambda i, cb: (i, cb), cb=cb)) for _, w, cb in tiled]
    in_specs += [pl.BlockSpec((HALO, tiled[j][1]),
                              functools.partial(lambda i, cb: (jnp.maximum(i * (CHUNK // HALO) - 1, 0), cb), cb=tiled[j][2]))
                 for j in halo_idx]
    in_specs += [pl.BlockSpec(c.shape, functools.partial(lambda i, nd: (0,) * nd, nd=c.ndim)) for c in consts]
    zeros = (0,) * len(state_shape)
    out_specs = [pl.BlockSpec((CHUNK, out_width), lambda i: (i, 0)), pl.BlockSpec((1,) + state_shape, lambda i: (i,) + zeros)]
    out_shape = [jax.ShapeDtypeStruct((seq, out_width), BF), jax.ShapeDtypeStruct((nc,) + state_shape, F32)]
    if aux_width is not None:
        out_specs.append(pl.BlockSpec((CHUNK, aux_width), lambda i: (i, 0)))
        out_shape.append(jax.ShapeDtypeStruct((seq, aux_width), F32))
    return _pc(
        body, name=name, grid=(nc,), in_specs=in_specs, out_specs=tuple(out_specs), out_shape=tuple(out_shape),
        scratch_shapes=[pltpu.VMEM(state_shape, F32)],
        compiler_params=_cparams(("arbitrary",)),
    )(*[t[0] for t in tiled], *[tiled[j][0] for j in halo_idx], *consts)


def _scan_bwd(name, fn, tiled, halo_idx, consts, saved, dy, dtile_dtypes, extra=None):
    seq = tiled[0][0].shape[0]
    nc = seq // CHUNK
    n_t, n_h, n_c = len(tiled), len(halo_idx), len(consts)
    state_shape = saved.shape[1:]
    n_x = 0 if extra is None else 1
    keep = [j for j, dtd in enumerate(dtile_dtypes) if dtd is not None]
    n_k = len(keep)

    def body(*refs):
        t_refs, h_refs, c_refs = refs[:n_t], refs[n_t:n_t + n_h], refs[n_t + n_h:n_t + n_h + n_c]
        pos = n_t + n_h + n_c
        save_ref, dy_ref = refs[pos], refs[pos + 1]
        x_refs = refs[pos + 2:pos + 2 + n_x]
        pos += 2 + n_x
        dt_refs, dc_refs = refs[pos:pos + n_k], refs[pos + n_k:pos + n_k + n_c]
        dst_ref = refs[pos + n_k + n_c]
        carry_refs = refs[pos + n_k + n_c + 1:]
        i = pl.program_id(0)

        @pl.when(i == 0)
        def _():
            dst_ref[...] = jnp.zeros_like(dst_ref)
            for r in carry_refs:
                r[...] = jnp.zeros_like(r)
            for r in dc_refs:
                r[...] = jnp.zeros_like(r)

        flag = jnp.where(i < nc - 1, 1.0, 0.0).astype(F32)
        tiles = [r[...] for r in t_refs]
        halos = [r[...] * flag for r in h_refs]
        cvals = [r[...] for r in c_refs]
        _, vjp = jax.vjp(fn, tiles, halos, save_ref[0], cvals)
        d_tiles, d_halos, d_state, d_consts = vjp(((dy_ref[...].astype(F32),), dst_ref[...]))
        dst_ref[...] = d_state
        for r, g in zip(dc_refs, d_consts):
            r[...] += g
        for r, j in zip(dt_refs, keep):
            g = d_tiles[j]
            if extra is not None and extra[0] == j:
                g = g + x_refs[0][...].astype(F32)
            r[...] = g.astype(r.dtype)
            if j in halo_idx:
                cr = carry_refs[halo_idx.index(j)]
                r[CHUNK - HALO:CHUNK, :] = (g[CHUNK - HALO:CHUNK, :] + cr[...]).astype(r.dtype)
                cr[...] = d_halos[halo_idx.index(j)] * flag

    rev = lambda i: nc - 1 - i
    in_specs = [pl.BlockSpec((CHUNK, w), functools.partial(lambda i, cb: (rev(i), cb), cb=cb)) for _, w, cb in tiled]
    in_specs += [pl.BlockSpec((HALO, tiled[j][1]),
                              functools.partial(lambda i, cb: (jnp.maximum(rev(i) * (CHUNK // HALO) - 1, 0), cb), cb=tiled[j][2]))
                 for j in halo_idx]
    in_specs += [pl.BlockSpec(c.shape, functools.partial(lambda i, nd: (0,) * nd, nd=c.ndim)) for c in consts]
    zeros = (0,) * len(state_shape)
    in_specs += [pl.BlockSpec((1,) + state_shape, lambda i: (rev(i),) + zeros),
                 pl.BlockSpec((CHUNK, dy.shape[1]), lambda i: (rev(i), 0))]
    args = [t[0] for t in tiled] + [tiled[j][0] for j in halo_idx] + list(consts) + [saved, dy]
    if extra is not None:
        in_specs.append(pl.BlockSpec((CHUNK, extra[1].shape[1]), lambda i: (rev(i), 0)))
        args.append(extra[1])
    out_specs = [pl.BlockSpec((CHUNK, tiled[j][1]), lambda i: (rev(i), 0)) for j in keep]
    out_specs += [pl.BlockSpec(c.shape, functools.partial(lambda i, nd: (0,) * nd, nd=c.ndim)) for c in consts]
    out_shape = [jax.ShapeDtypeStruct((seq, tiled[j][1]), dtile_dtypes[j]) for j in keep]
    out_shape += [jax.ShapeDtypeStruct(c.shape, F32) for c in consts]
    scratch = [pltpu.VMEM(state_shape, F32)] + [pltpu.VMEM((HALO, tiled[j][1]), F32) for j in halo_idx]
    return _pc(body, name=name, grid=(nc,), in_specs=in_specs, out_specs=tuple(out_specs), out_shape=tuple(out_shape),
               scratch_shapes=scratch, compiler_params=_cparams(("arbitrary",)))(*args)


def _tile_fwd(name, fn, tiled, consts, outs, tm=TOKEN_TILE):
    seq = tiled[0][0].shape[0]
    n_t, n_c = len(tiled), len(consts)

    def body(*refs):
        res = fn(*[r[...] for r in refs[:n_t + n_c]])
        for r, y in zip(refs[n_t + n_c:], res):
            r[...] = y.astype(r.dtype)

    in_specs = [pl.BlockSpec((tm, w), functools.partial(lambda i, cb: (i, cb), cb=cb)) for _, w, cb in tiled]
    in_specs += [pl.BlockSpec(c.shape, functools.partial(lambda i, nd: (0,) * nd, nd=c.ndim)) for c in consts]
    return _pc(body, name=name, grid=(seq // tm,), in_specs=in_specs,
               out_specs=tuple(pl.BlockSpec((tm, w), lambda i: (i, 0)) for w, _ in outs),
               out_shape=tuple(jax.ShapeDtypeStruct((seq, w), dtp) for w, dtp in outs),
               compiler_params=_cparams(("arbitrary",)))(*[t[0] for t in tiled], *consts)


def _tile_bwd(name, fn, tiled, consts, douts, dtile_dtypes, add_to=None, tm=TOKEN_TILE):
    seq = tiled[0][0].shape[0]
    n_t, n_c, n_o = len(tiled), len(consts), len(douts)
    n_x = 0 if add_to is None else 1
    keep = [j for j, dtp in enumerate(dtile_dtypes) if dtp is not None]

    def body(*refs):
        vals = [r[...].astype(F32) for r in refs[:n_t + n_c]]
        pos = n_t + n_c
        g_refs, x_refs = refs[pos:pos + n_o], refs[pos + n_o:pos + n_o + n_x]
        pos += n_o + n_x
        dt_refs, dc_refs = refs[pos:pos + len(keep)], refs[pos + len(keep):]
        i = pl.program_id(0)

        @pl.when(i == 0)
        def _():
            for r in dc_refs:
                r[...] = jnp.zeros_like(r)

        _, vjp = jax.vjp(fn, *vals)
        cts = vjp(tuple(g[...].astype(F32) for g in g_refs))
        for r, j in zip(dt_refs, keep):
            g = cts[j]
            if add_to is not None and add_to[0] == j:
                g = g + x_refs[0][...].astype(F32)
            r[...] = g.astype(r.dtype)
        for r, g in zip(dc_refs, cts[n_t:]):
            r[...] += g

    in_specs = [pl.BlockSpec((tm, w), functools.partial(lambda i, cb: (i, cb), cb=cb)) for _, w, cb in tiled]
    in_specs += [pl.BlockSpec(c.shape, functools.partial(lambda i, nd: (0,) * nd, nd=c.ndim)) for c in consts]
    in_specs += [pl.BlockSpec((tm, g.shape[1]), lambda i: (i, 0)) for g in douts]
    args = [t[0] for t in tiled] + list(consts) + list(douts)
    if add_to is not None:
        in_specs.append(pl.BlockSpec((tm, add_to[1].shape[1]), lambda i: (i, 0)))
        args.append(add_to[1])
    out_specs = [pl.BlockSpec((tm, tiled[j][1]), lambda i: (i, 0)) for j in keep]
    out_specs += [pl.BlockSpec(c.shape, functools.partial(lambda i, nd: (0,) * nd, nd=c.ndim)) for c in consts]
    out_shape = [jax.ShapeDtypeStruct((seq, tiled[j][1]), dtile_dtypes[j]) for j in keep]
    out_shape += [jax.ShapeDtypeStruct(c.shape, F32) for c in consts]
    return _pc(body, name=name, grid=(seq // tm,), in_specs=in_specs, out_specs=tuple(out_specs),
               out_shape=tuple(out_shape), compiler_params=_cparams(("arbitrary",)))(*args)


def _lnmod_fn(x, nw, sc, sh):
    return (_rms(x, nw) * (1.0 + sc) + sh,)


def _gated_fn(o, g):
    return ((1.0 + g) * o,)


def _resid_fn(x, o, g):
    return (x + (1.0 + g) * o,)


def _swiglu_fn(gu):
    return (_silu(gu[:, :FFN_H]) * gu[:, FFN_H:],)


def _merge_fn(yh, ys, yg, logits, wb, b_merge):
    gates = jax.nn.sigmoid(logits + b_merge)
    acc = None
    for n, y in enumerate((yh, ys, yg)):
        t = gates[:, n * D_MODEL:(n + 1) * D_MODEL] * _bdot(y, wb[n])
        acc = t if acc is None else acc + t
    return (acc,)


MM_VMEM_BUDGET = 40 * 1024 * 1024
MM_TILE_CAP = 1024
MM_K_CAP = 4096


def _divisor(n, cap, unit=LANES):
    best = None
    for d in range(unit, min(n, cap) + 1, unit):
        if n % d == 0:
            best = d
    return n if best is None else best


def _mm_tiles(m, n, k, out_bytes):
    tk = k if k <= MM_K_CAP else _divisor(k, 3072)
    tm, tn = _divisor(m, MM_TILE_CAP), _divisor(n, MM_TILE_CAP + MM_TILE_CAP // 2)

    def need(tm_, tn_):
        acc = tm_ * tn_ * 4 if tk < k else 0
        return 2 * 2 * tk * (tm_ + tn_) + acc + 2 * tm_ * tn_ * out_bytes

    while need(tm, tn) > MM_VMEM_BUDGET:
        if tn >= tm and _divisor(n, tn - LANES) < tn:
            tn = _divisor(n, tn - LANES)
        elif _divisor(m, tm - LANES) < tm:
            tm = _divisor(m, tm - LANES)
        else:
            break
    return tm, tn, tk


def _mm(a, b, mode, out_dtype, name):
    if mode == "nn":
        (m, k), n = a.shape, b.shape[1]
    elif mode == "nt":
        (m, k), n = a.shape, b.shape[0]
    else:
        (k, m), n = a.shape, b.shape[1]
    tm, tn, tk = _mm_tiles(m, n, k, jnp.dtype(out_dtype).itemsize)
    nk = k // tk
    dims = {"nn": ((1,), (0,)), "nt": ((1,), (1,)), "tn": ((0,), (0,))}[mode]

    def body_one(a_ref, b_ref, o_ref):
        o_ref[...] = lax.dot_general(a_ref[...], b_ref[...], (dims, ((), ())), preferred_element_type=F32).astype(o_ref.dtype)

    def body_acc(a_ref, b_ref, o_ref, acc_ref):
        kk = pl.program_id(2)

        @pl.when(kk == 0)
        def _():
            acc_ref[...] = jnp.zeros_like(acc_ref)

        acc_ref[...] += lax.dot_general(a_ref[...], b_ref[...], (dims, ((), ())), preferred_element_type=F32)

        @pl.when(kk == nk - 1)
        def _():
            o_ref[...] = acc_ref[...].astype(o_ref.dtype)

    a_spec = pl.BlockSpec((tk, tm), lambda i, j, kk: (kk, i)) if mode == "tn" else pl.BlockSpec((tm, tk), lambda i, j, kk: (i, kk))
    b_spec = pl.BlockSpec((tn, tk), lambda i, j, kk: (j, kk)) if mode == "nt" else pl.BlockSpec((tk, tn), lambda i, j, kk: (kk, j))
    return _pc(body_one if nk == 1 else body_acc, name=name, grid=(m // tm, n // tn, nk), in_specs=[a_spec, b_spec],
               out_specs=pl.BlockSpec((tm, tn), lambda i, j, kk: (i, j)),
               out_shape=jax.ShapeDtypeStruct((m, n), out_dtype),
               scratch_shapes=[] if nk == 1 else [pltpu.VMEM((tm, tn), F32)],
               compiler_params=_cparams(("parallel", "parallel", "arbitrary")))(a.astype(BF), b.astype(BF))


def _final_loss(x, tgt, norm_final, tm=TOKEN_TILE):
    seq = x.shape[0]

    def fn(xv, nf, tv):
        err = jnp.square(_rms(xv, nf) - tv)
        return 0.5 * jnp.sum(jnp.mean(err, axis=-1))

    def body(x_ref, t_ref, nf_ref, loss_ref, dx_ref, dnf_ref):
        i = pl.program_id(0)

        @pl.when(i == 0)
        def _():
            loss_ref[...] = jnp.zeros_like(loss_ref)
            dnf_ref[...] = jnp.zeros_like(dnf_ref)

        val, vjp = jax.vjp(functools.partial(fn, tv=t_ref[...]), x_ref[...], nf_ref[...])
        dx, dnf = vjp(jnp.ones((), F32))
        dx_ref[...] = dx
        dnf_ref[...] += dnf
        loss_ref[...] += jnp.broadcast_to(val, loss_ref.shape)

    return _pc(body, name="final_loss", grid=(seq // tm,),
               in_specs=[pl.BlockSpec((tm, D_MODEL), lambda i: (i, 0)), pl.BlockSpec((tm, D_MODEL), lambda i: (i, 0)),
                         pl.BlockSpec((1, D_MODEL), lambda i: (0, 0))],
               out_specs=(pl.BlockSpec((8, LANES), lambda i: (0, 0)), pl.BlockSpec((tm, D_MODEL), lambda i: (i, 0)),
                          pl.BlockSpec((1, D_MODEL), lambda i: (0, 0))),
               out_shape=(jax.ShapeDtypeStruct((8, LANES), F32), jax.ShapeDtypeStruct((seq, D_MODEL), F32),
                          jax.ShapeDtypeStruct((1, D_MODEL), F32)),
               compiler_params=_cparams(("arbitrary",)))(x, tgt, norm_final)


def _ada_fwd(c_all, w_ada, b_ada_cols):
    n_l, _, cols = w_ada.shape

    def body(c_ref, w_ref, b_ref, o_ref):
        o_ref[0] = jnp.dot(_silu(c_ref[...]), w_ref[0], preferred_element_type=F32, precision=HI) + b_ref[0]

    return _pc(body, name="ada_fwd", grid=(n_l,),
               in_specs=[pl.BlockSpec((8, D_MODEL), lambda l: (0, 0)), pl.BlockSpec((1, D_MODEL, cols), lambda l: (l, 0, 0)),
                         pl.BlockSpec((1, 1, cols), lambda l: (l, 0, 0))],
               out_specs=pl.BlockSpec((1, 8, cols), lambda l: (l, 0, 0)),
               out_shape=jax.ShapeDtypeStruct((n_l, 8, cols), F32),
               compiler_params=_cparams(("arbitrary",)))(c_all, w_ada, b_ada_cols.reshape(n_l, 1, cols))


def _ada_bwd(c_all, dmod_cols):
    n_l, _, cols = dmod_cols.shape

    def body(c_ref, g_ref, o_ref):
        o_ref[0] = lax.dot_general(_silu(c_ref[...]), g_ref[0], (((0,), (0,)), ((), ())), preferred_element_type=F32,
                                   precision=HI)

    return _pc(body, name="ada_bwd", grid=(n_l,),
               in_specs=[pl.BlockSpec((8, D_MODEL), lambda l: (0, 0)), pl.BlockSpec((1, 8, cols), lambda l: (l, 0, 0))],
               out_specs=pl.BlockSpec((1, D_MODEL, cols), lambda l: (l, 0, 0)),
               out_shape=jax.ShapeDtypeStruct((n_l, D_MODEL, cols), F32),
               compiler_params=_cparams(("arbitrary",)))(c_all, dmod_cols)


def _lb_fn(logits):
    e = jnp.exp(logits - jnp.max(logits, axis=0, keepdims=True))
    p = e / jnp.sum(e, axis=0, keepdims=True)
    r = lax.broadcasted_iota(jnp.int32, (DEPTH, 1), 0)
    lb = jnp.zeros_like(p)
    for j in range(1, DEPTH):
        lb = lb + jnp.where(r >= j, p[j:j + 1, :], 0.0)
    return lb


def _lb_fwd(logits):
    def body(l_ref, o_ref):
        o_ref[...] = _lb_fn(l_ref[...])

    return _pc(body, name="lb_fwd", out_shape=jax.ShapeDtypeStruct(logits.shape, F32))(logits)


def _lb_bwd(logits, dlb):
    def body(l_ref, g_ref, o_ref):
        _, vjp = jax.vjp(_lb_fn, l_ref[...])
        o_ref[...] = vjp(g_ref[...])[0]

    return _pc(body, name="lb_bwd", out_shape=jax.ShapeDtypeStruct(logits.shape, F32))(logits, dlb)


def _rows_for(n_rows, n_cols):
    r = 8
    while r * 2 <= n_rows and n_rows % (r * 2) == 0 and r * 2 * n_cols <= 512 * 1024:
        r *= 2
    return r if n_rows % r == 0 else n_rows


def _ew(name, fn, ins, out_dtypes):
    n_rows, n_cols = ins[0].shape
    tr = _rows_for(n_rows, n_cols)
    n_in = len(ins)

    def body(*refs):
        res = fn(*[r[...] for r in refs[:n_in]])
        for r, y in zip(refs[n_in:], res):
            r[...] = y.astype(r.dtype)

    spec = pl.BlockSpec((tr, n_cols), lambda i: (i, 0))
    return _pc(body, name=name, grid=(n_rows // tr,), in_specs=[spec] * n_in, out_specs=tuple([spec] * len(out_dtypes)),
               out_shape=tuple(jax.ShapeDtypeStruct((n_rows, n_cols), d) for d in out_dtypes),
               compiler_params=_cparams(("arbitrary",)))(*ins)


def _adamw_fn(w, g, m, v):
    m = ADAM_B1 * m + (1.0 - ADAM_B1) * g
    v = ADAM_B2 * v + (1.0 - ADAM_B2) * jnp.square(g)
    m_hat = m / (1.0 - ADAM_B1 ** ADAM_STEP)
    v_hat = v / (1.0 - ADAM_B2 ** ADAM_STEP)
    return -ADAM_LR * (m_hat / (jnp.sqrt(v_hat) + ADAM_EPS) + ADAM_WD * w), m, v


def _adamw(name, w, g, m, v):
    shape = w.shape
    two = (-1, shape[-1])
    d, nm, nv = _ew(name, _adamw_fn, [a.reshape(two) for a in (w, g, m, v)], [F32, F32, F32])
    return d.reshape(shape), nm.reshape(shape), nv.reshape(shape)


def _sum_leading(name, a, out_dtype):
    n, n_rows, n_cols = a.shape
    tr = _rows_for(n_rows, n_cols)

    def body(a_ref, o_ref):
        acc = a_ref[0].astype(F32)
        for j in range(1, n):
            acc = acc + a_ref[j].astype(F32)
        o_ref[...] = acc.astype(o_ref.dtype)

    return _pc(body, name=name, grid=(n_rows // tr,), in_specs=[pl.BlockSpec((n, tr, n_cols), lambda i: (0, i, 0))],
               out_specs=pl.BlockSpec((tr, n_cols), lambda i: (i, 0)),
               out_shape=jax.ShapeDtypeStruct((n_rows, n_cols), out_dtype),
               compiler_params=_cparams(("arbitrary",)))(a)


MESH = pl.DeviceIdType.MESH
ANY = pl.BlockSpec(memory_space=pl.ANY)


def _place():
    return lax.axis_index("x"), lax.axis_index("y"), lax.axis_index("c")


def _all_gather_small(name, a):
    m_per, n = a.shape

    def body(x_ref, out_ref, send_sems, recv_sems, local_sem):
        x, y, c = _place()
        me, sibling = (x, y, c), (x, y, 1 - c)
        chips = [(1 - x, y), (x, 1 - y), (1 - x, 1 - y)]

        def rows(px, py, pc):
            return out_ref.at[pl.ds((4 * px + 2 * py + pc) * m_per, m_per), :]

        def copy(k, block, to, src=None):
            return pltpu.make_async_remote_copy(src_ref=rows(*block) if src is None else src, dst_ref=rows(*block),
                                                send_sem=send_sems.at[k], recv_sem=recv_sems.at[k], device_id=to,
                                                device_id_type=MESH)

        mine = pltpu.make_async_copy(x_ref, rows(*me), local_sem)
        mine.start()
        first = [copy(0, me, sibling, src=x_ref)]
        first += [copy(1 + j, me, (*chip, c), src=x_ref) for j, chip in enumerate(chips)]
        for cp in first:
            cp.start()
        passed = [copy(4 + j, (*chip, c), sibling) for j, chip in enumerate(chips)]
        for j, chip in enumerate(chips):
            copy(1 + j, (*chip, c), me).wait_recv()
            passed[j].start()
        copy(0, sibling, me).wait_recv()
        for j, chip in enumerate(chips):
            copy(4 + j, (*chip, 1 - c), me).wait_recv()
        for cp in first + passed:
            cp.wait_send()
        mine.wait()

    out = _pc(body, name=name, out_shape=jax.ShapeDtypeStruct((8 * m_per, n), a.dtype),
              in_specs=[pl.BlockSpec(memory_space=pltpu.VMEM)], out_specs=pl.BlockSpec(memory_space=pltpu.VMEM),
              scratch_shapes=[pltpu.SemaphoreType.DMA((7,)), pltpu.SemaphoreType.DMA((7,)), pltpu.SemaphoreType.DMA],
              compiler_params=pltpu.CompilerParams(vmem_limit_bytes=VMEM_LIMIT))(a)
    return out.reshape(8, m_per, n)


def _chip_gather(name, pack):
    n_l, n_r, n_c = pack.shape
    half = n_r // 2

    def body(p_ref, o_ref, send_sems, recv_sems):
        x, y, c = _place()
        sibling = (x, y, 1 - c)
        chips = [(1 - x, y), (x, 1 - y), (1 - x, 1 - y)]

        def slab(px, py, pc):
            return o_ref.at[2 * px + py, :, pl.ds(pc * half, half), :]

        def copy(k, src, dst, to):
            return pltpu.make_async_remote_copy(src_ref=src, dst_ref=dst, send_sem=send_sems.at[k], recv_sem=recv_sems.at[k],
                                                device_id=to, device_id_type=MESH)

        first = [copy(j, p_ref.at[:, pl.ds(c * half, half), :], slab(x, y, c), (*chip, c)) for j, chip in enumerate(chips)]
        for cp in first:
            cp.start()
        passed = [copy(3 + j, slab(*chip, c), slab(*chip, c), sibling) for j, chip in enumerate(chips)]
        for j, chip in enumerate(chips):
            copy(j, slab(*chip, c), slab(*chip, c), (*chip, c)).wait_recv()
            passed[j].start()
        for j, chip in enumerate(chips):
            copy(3 + j, slab(*chip, 1 - c), slab(*chip, 1 - c), sibling).wait_recv()
        for cp in first + passed:
            cp.wait_send()

    return _pc(body, name=name, out_shape=jax.ShapeDtypeStruct((4, n_l, n_r, n_c), pack.dtype), in_specs=[ANY], out_specs=ANY,
               scratch_shapes=[pltpu.SemaphoreType.DMA((6,)), pltpu.SemaphoreType.DMA((6,))])(pack)


def _pair_swap(name, give):
    def body(g_ref, o_ref, send_sem, recv_sem):
        x, y, c = _place()
        cp = pltpu.make_async_remote_copy(src_ref=g_ref, dst_ref=o_ref, send_sem=send_sem, recv_sem=recv_sem,
                                          device_id=(x, y, 1 - c), device_id_type=MESH)
        cp.start()
        cp.wait()

    return _pc(body, name=name, out_shape=jax.ShapeDtypeStruct(give.shape, give.dtype), in_specs=[ANY], out_specs=ANY,
               scratch_shapes=[pltpu.SemaphoreType.DMA, pltpu.SemaphoreType.DMA])(give)


def _chip_exchange(name, parts):
    def body(p_ref, o_ref, send_sems, recv_sems):
        x, y, c = _place()
        me = 2 * x + y
        chips = [(1 - x, y), (x, 1 - y), (1 - x, 1 - y)]

        def copy(k, src, dst, to):
            return pltpu.make_async_remote_copy(src_ref=src, dst_ref=dst, send_sem=send_sems.at[k], recv_sem=recv_sems.at[k],
                                                device_id=to, device_id_type=MESH)

        sends = [copy(j, p_ref.at[2 * px + py], o_ref.at[me], (px, py, c)) for j, (px, py) in enumerate(chips)]
        for cp in sends:
            cp.start()
        for j, (px, py) in enumerate(chips):
            copy(j, p_ref.at[2 * px + py], o_ref.at[2 * px + py], (px, py, c)).wait_recv()
        for cp in sends:
            cp.wait_send()

    return _pc(body, name=name, out_shape=jax.ShapeDtypeStruct(parts.shape, parts.dtype), in_specs=[ANY], out_specs=ANY,
               scratch_shapes=[pltpu.SemaphoreType.DMA((3,)), pltpu.SemaphoreType.DMA((3,))])(parts)


N_CHIP = 4
BIG = (("w_in", (1024, 2822), 1, (1024, 2822)), ("w_branch", (3, 768, 256), 2, (2304, 256)),
       ("w_out", (256, 1024), 0, (256, 1024)), ("w_ffn_in", (1024, 1408), 1, (1024, 1408)),
       ("w_ffn_out", (704, 1024), 0, (704, 1024)))
G768 = ((0, 3072), (3072, 3840), (7436, 8204))
GXBC, GQKV, GGATE = (3840, 5120), (5132, 7436), (8216, 11288)
GSMALL = ((5120, 5132), (8204, 8210), (8210, 8216))
W768, WXBC, WGATE = 4608, CONV_CH, 3 * D_MODEL
IN_PAD = W768 + WXBC + QKV_W + WGATE + SMALL_W


def _join_shards(slabs, axis, shard_shape):
    n_l = slabs.shape[1]
    parts = [slabs[j].reshape((n_l,) + shard_shape) for j in range(N_CHIP)]
    return jnp.concatenate(parts, axis=axis + 1)


def _split_shards(full, axis, rows_cols):
    n_l = full.shape[0]
    size = full.shape[axis + 1] // N_CHIP
    return jnp.stack([lax.slice_in_dim(full, j * size, (j + 1) * size, axis=axis + 1).reshape((n_l,) + rows_cols)
                      for j in range(N_CHIP)])


def _gather_weights(w, chip, big=BIG):
    out = {}
    for n, shape, ax, rc in big:
        n_l = w[n].shape[0]
        mine = w[n].astype(BF).reshape((n_l,) + rc)
        slabs = lax.dynamic_update_slice(_chip_gather("gather_" + n, mine), mine[None], (chip, 0, 0, 0))
        out[n] = _join_shards(slabs, ax, shape)
    return out


def _pair_stage(full_grads, core, big=BIG):
    out = {}
    for n, _, ax, (rows, cols) in big:
        n_l = full_grads[n].shape[0]
        slabs = _split_shards(full_grads[n], ax, (rows, cols))
        half = rows // 2
        keep = lax.dynamic_slice_in_dim(slabs, core * half, half, axis=2).reshape(-1, cols)
        give = lax.dynamic_slice_in_dim(slabs, (1 - core) * half, half, axis=2).reshape(-1, cols)
        got = _pair_swap("pair_swap_" + n, give)
        (pair_sum,) = _ew("pair_sum_" + n, lambda a, b: (a.astype(F32) + b.astype(F32),), [keep, got], [BF])
        out[n] = pair_sum.reshape(N_CHIP, n_l * half, cols)
    return out


def _own_slab(landed, pair_sum, chip):
    return lax.dynamic_update_slice(landed, lax.dynamic_slice_in_dim(pair_sum, chip, 1, axis=0), (chip, 0, 0))


def _finish_reduce(parts, n_l, core, big=BIG):
    out = {}
    for n, shape, _, (rows, cols) in big:
        half = rows // 2
        mine = _sum_leading("chip_sum_" + n, parts[n], F32).reshape(n_l, half, cols)
        theirs = _pair_swap("pair_share_" + n, mine)
        full = jnp.concatenate([jnp.where(core == 0, mine, theirs), jnp.where(core == 0, theirs, mine)], axis=1)
        out[n] = full.reshape((n_l,) + shape)
    return out


def _reduce_grads(full_grads, chip, core, big=BIG):
    pair_sums = _pair_stage(full_grads, core, big)
    parts = {n: _own_slab(_chip_exchange("chip_exchange_" + n, pair_sums[n]), pair_sums[n], chip) for n, _, _, _ in big}
    return _finish_reduce(parts, full_grads[big[0][0]].shape[0], core, big)


HBM_SPEC = pl.BlockSpec(memory_space=pltpu.HBM)
SEM_SPEC = pl.BlockSpec(memory_space=pltpu.SEMAPHORE)
DATAFLOW = pltpu.SideEffectType.DATAFLOW_SIDE_EFFECTING


def _exchange_copies(p_ref, land_ref, sems, waiting, spread):
    x, y, c = _place()
    me = 2 * x + y
    out = []
    for j, (px, py) in enumerate([(1 - x, y), (x, 1 - y), (1 - x, 1 - y)]):
        out.append(pltpu.make_async_remote_copy(src_ref=p_ref if spread else p_ref.at[2 * px + py],
                                                dst_ref=land_ref.at[2 * px + py if waiting else me],
                                                send_sem=sems[j], recv_sem=sems[3 + j], device_id=(px, py, c),
                                                device_id_type=MESH))
    return out


def _exchange_start(name, parts, after, spread=False):
    land_shape = ((N_CHIP,) + parts.shape) if spread else parts.shape

    def body(p_ref, land_ref, after_ref, s0, s1, s2, r0, r1, r2, p_thru, land_thru, token):
        for cp in _exchange_copies(p_ref, land_ref, (s0, s1, s2, r0, r1, r2), False, spread):
            cp.start()
        token[...] = jnp.zeros_like(token)

    res = _pc(body, name=name,
              out_shape=(pltpu.SemaphoreType.DMA(()),) * 6 + (pltpu.HBM(parts.shape, parts.dtype), pltpu.HBM(land_shape, parts.dtype),
                                                            jax.ShapeDtypeStruct((8, LANES), F32)),
              in_specs=(HBM_SPEC, HBM_SPEC, ANY),
              out_specs=(SEM_SPEC,) * 6 + (HBM_SPEC, HBM_SPEC, pl.BlockSpec(memory_space=pltpu.VMEM)),
              input_output_aliases={0: 6, 1: 7}, compiler_params=pltpu.CompilerParams(has_side_effects=DATAFLOW))(
        pltpu.with_memory_space_constraint(parts, pltpu.HBM),
        pltpu.with_memory_space_constraint(lax.empty(land_shape, parts.dtype), pltpu.HBM), after)
    return res[:6], res[6], res[7], res[8]


def _exchange_wait(name, sems, p_thru, land_thru, after, spread=False):
    def body(p_ref, land_ref, s0, s1, s2, r0, r1, r2, after_ref, p_dead, got_ref):
        for cp in _exchange_copies(p_ref, land_ref, (s0, s1, s2, r0, r1, r2), True, spread):
            cp.wait_send()
            cp.wait_recv()

    return _pc(body, name=name, out_shape=(pltpu.HBM(p_thru.shape, p_thru.dtype), pltpu.HBM(land_thru.shape, land_thru.dtype)),
               in_specs=(HBM_SPEC, HBM_SPEC) + (SEM_SPEC,) * 6 + (ANY,), out_specs=(HBM_SPEC, HBM_SPEC),
               input_output_aliases={0: 0, 1: 1}, compiler_params=pltpu.CompilerParams(has_side_effects=DATAFLOW))(
        p_thru, land_thru, *sems, after)[1]


def _regroup_w_in(w):
    cat = lambda spans: jnp.concatenate([w[:, a:b] for a, b in spans], axis=1)
    small = jnp.concatenate([cat(GSMALL), jnp.zeros((w.shape[0], SMALL_W - 24), w.dtype)], axis=1)
    return cat(G768), cat((GXBC,)), cat((GQKV,)), cat((GGATE,)), small


def _ungroup_w_in(d):
    o_xbc, o_qkv, o_gate, o_small = W768, W768 + WXBC, W768 + WXBC + QKV_W, W768 + WXBC + QKV_W + WGATE
    spans = ((0, 3072), (3072, 3840), (o_xbc, o_xbc + WXBC), (o_small, o_small + 12), (o_qkv, o_qkv + QKV_W),
             (3840, 4608), (o_small + 12, o_small + 18), (o_small + 18, o_small + 24), (o_gate, o_gate + WGATE))
    return jnp.concatenate([d[:, a:b] for a, b in spans], axis=1)


def _lane_pad(v, off):
    return jnp.pad(v, (off, LANES - off - v.shape[0]))[None, :]


STATE6 = (N_HEAD6, HEAD, HEAD)


def _mixer_inputs(sv, lp):
    p768, pxbc, pqkv, psmall = sv["p768"], sv["pxbc"], sv["pqkv"], sv["psmall"]
    hgrn = ([(p768, MIX_W, j) for j in range(4)], [], [lp["lb"], lp["hgrn_norm"]])
    ssd = ([(p768, MIX_W, 4), (pxbc, CONV_CH, 0), (psmall, LANES, 0)], [1],
           [lp["ssm_conv_w"], lp["ssm_conv_b"], lp["ssm_dt_bias"], lp["ssm_a_log"], lp["ssm_d"], lp["ssm_norm"]])
    gdn = ([(pqkv, QKV_W, 0), (p768, MIX_W, 5), (psmall, LANES, 0)], [0],
           [lp["gdn_conv_w"], lp["gdn_dt_bias"], lp["gdn_a_log"], lp["gdn_norm"]])
    return hgrn, ssd, gdn


def _layer_fwd(x, md, lw, lp):
    sv = {"x": x}
    (sv["h1"],) = _tile_fwd("lnmod1", _lnmod_fn, [(x, D_MODEL, 0)], [lp["norm_mix"], md["sc1"], md["sh1"]], [(D_MODEL, BF)])
    for nm in ("768", "xbc", "qkv", "gate", "small"):
        sv["p" + nm] = _mm(sv["h1"], lw["win_" + nm], "nn", F32, "proj_" + nm)
    hgrn, ssd, gdn = _mixer_inputs(sv, lp)
    sv["y_h"], sv["st_h"] = _scan_fwd("hgrn_fwd", _hgrn_chunk, *hgrn, MIX_W, STATE6)
    sv["y_s"], sv["st_s"] = _scan_fwd("ssd_fwd", _ssd_chunk, *ssd, MIX_W, STATE6)
    sv["y_g"], sv["st_g"], sv["inv_g"] = _scan_fwd("gdn_fwd", _gdn_chunk, *gdn, MIX_W, STATE6, aux_width=N_HEAD6 * LANES)
    (sv["merged"],) = _tile_fwd("merge", _merge_fn, _merge_tiles(sv), [lw["w_branch"], lp["b_merge"]], [(D_MODEL, BF)],
                                tm=WIDE_TOKEN_TILE)
    sv["out"] = _mm(sv["merged"], lw["w_out"], "nn", F32, "out_proj")
    (sv["x_mid"],) = _tile_fwd("resid1", _resid_fn, [(x, D_MODEL, 0), (sv["out"], D_MODEL, 0)], [md["g1"]], [(D_MODEL, F32)])
    (sv["h2"],) = _tile_fwd("lnmod2", _lnmod_fn, [(sv["x_mid"], D_MODEL, 0)], [lp["norm_ffn"], md["sc2"], md["sh2"]],
                            [(D_MODEL, BF)])
    sv["gu"] = _mm(sv["h2"], lw["w_ffn_in"], "nn", F32, "ffn_in")
    (sv["act"],) = _tile_fwd("swiglu", _swiglu_fn, [(sv["gu"], 2 * FFN_H, 0)], [], [(FFN_H, BF)], tm=WIDE_TOKEN_TILE)
    sv["o2"] = _mm(sv["act"], lw["w_ffn_out"], "nn", F32, "ffn_out")
    (x_out,) = _tile_fwd("resid2", _resid_fn, [(sv["x_mid"], D_MODEL, 0), (sv["o2"], D_MODEL, 0)], [md["g2"]], [(D_MODEL, F32)])
    return x_out, sv


def _merge_tiles(sv):
    return [(sv["y_h"], MIX_W, 0), (sv["y_s"], MIX_W, 0), (sv["y_g"], MIX_W, 0), (sv["pgate"], WGATE, 0)]


def _layer_bwd(dx_out, sv, md, lw, lp):
    g = {}
    x, x_mid = sv["x"], sv["x_mid"]
    d_o2, g["g2"] = _tile_bwd("resid2_b", _gated_fn, [(sv["o2"], D_MODEL, 0)], [md["g2"]], [dx_out], [BF])
    d_xmid = dx_out
    d_act =_mm(d_o2, lw["w_ffn_out"], "nt", F32, "ffn_out_dx")
    g["w_ffn_out"] = _mm(sv["act"], d_o2, "tn", BF, "ffn_out_dw")
    (d_gu,) = _tile_bwd("swiglu_b", _swiglu_fn, [(sv["gu"], 2 * FFN_H, 0)], [], [d_act], [BF], tm=WIDE_TOKEN_TILE)
    d_h2 = _mm(d_gu, lw["w_ffn_in"], "nt", F32, "ffn_in_dx")
    g["w_ffn_in"] = _mm(sv["h2"], d_gu, "tn", BF, "ffn_in_dw")
    d_xmid, g["norm_ffn"], g["sc2"], g["sh2"] = _tile_bwd(
        "lnmod2_b", _lnmod_fn, [(x_mid, D_MODEL, 0)], [lp["norm_ffn"], md["sc2"], md["sh2"]], [d_h2], [F32], add_to=(0, d_xmid))
    d_out, g["g1"] = _tile_bwd("resid1_b", _gated_fn, [(sv["out"], D_MODEL, 0)], [md["g1"]], [d_xmid], [BF])
    d_x = d_xmid
    d_merged = _mm(d_out, lw["w_out"], "nt", F32, "out_proj_dx")
    g["w_out"] = _mm(sv["merged"], d_out, "tn", BF, "out_proj_dw")
    d_yh, d_ys, d_yg, d_gate, g["w_branch"], g["b_merge"] = _tile_bwd(
        "merge_b", _merge_fn, _merge_tiles(sv), [lw["w_branch"], lp["b_merge"]], [d_merged], [F32, F32, F32, BF],
        tm=WIDE_TOKEN_TILE)
    hgrn, ssd, gdn = _mixer_inputs(sv, lp)
    d_q, d_f, d_v, d_g, g["lb"], g["hgrn_norm"] = _scan_bwd("hgrn_bwd", _hgrn_chunk, *hgrn, sv["st_h"], d_yh, [BF] * 4)
    (d_sz, d_xbc, d_small, g["ssm_conv_w"], g["ssm_conv_b"], g["ssm_dt_bias"], g["ssm_a_log"], g["ssm_d"],
     g["ssm_norm"]) = _scan_bwd("ssd_bwd", _ssd_chunk, *ssd, sv["st_s"], d_ys, [BF, BF, F32])
    d_qkv, d_gz, d_small, g["gdn_conv_w"], g["gdn_dt_bias"], g["gdn_a_log"], g["gdn_norm"] = _scan_bwd(
        "gdn_bwd", _gdn_chunk, gdn[0] + [(sv["inv_g"], N_HEAD6 * LANES, 0)], gdn[1], gdn[2], sv["st_g"], d_yg,
        [BF, BF, BF, None], extra=(2, d_small))
    d_proj = jnp.concatenate([d_q, d_f, d_v, d_g, d_sz, d_gz, d_xbc, d_qkv, d_gate, d_small,
                              jnp.zeros((x.shape[0], SMALL_W - LANES), BF)], axis=1)
    d_h1 = _mm(d_proj, lw["win_all"], "nt", F32, "proj_dx")
    g["w_in"] = _ungroup_w_in(_mm(sv["h1"], d_proj, "tn", BF, "proj_dw"))
    d_x, g["norm_mix"], g["sc1"], g["sh1"] = _tile_bwd(
        "lnmod1_b", _lnmod_fn, [(x, D_MODEL, 0)], [lp["norm_mix"], md["sc1"], md["sh1"]], [d_h1], [F32], add_to=(0, d_x))
    return d_x, g


SMALL_REPL = ("norm_mix", "norm_ffn", "b_merge", "hgrn_lb_logits", "hgrn_norm", "ssm_conv_w", "ssm_conv_b", "ssm_dt_bias",
              "ssm_a_log", "ssm_d", "ssm_norm", "gdn_conv_w", "gdn_dt_bias", "gdn_a_log", "gdn_norm", "norm_final")
WEIGHTS = ("w_ada", "b_ada", "norm_mix", "norm_ffn", "w_in", "b_merge", "hgrn_lb_logits", "hgrn_norm", "ssm_conv_w",
           "ssm_conv_b", "ssm_dt_bias", "ssm_a_log", "ssm_d", "ssm_norm", "gdn_conv_w", "gdn_dt_bias", "gdn_a_log",
           "gdn_norm", "w_branch", "w_out", "w_ffn_in", "w_ffn_out", "norm_final")
SMALL_ROWS = 120


def _pad_rows(flat, n_rows, n_cols):
    return jnp.concatenate([flat, jnp.zeros((n_rows * n_cols - flat.shape[0],), flat.dtype)]).reshape(n_rows, n_cols)


def _device_step(x, tgt, mod, lb, wfull, sp, chip=None, core=None, order_after=None):
    mds, lps, svs = [], [], []
    h = x
    weights_of = wfull if callable(wfull) else (lambda layer, after: wfull[layer])
    wfull = []
    for l in range(DEPTH):
        wfull.append(weights_of(l, h))
        md = {n: mod[l, i * D_MODEL:(i + 1) * D_MODEL][None, :] for i, n in enumerate(("sh1", "sc1", "g1", "sh2", "sc2", "g2"))}
        if l == 0 and order_after is not None:
            md["sc1"] = md["sc1"] + order_after
        lp = {n: sp[n][l][None, :] for n in ("norm_mix", "norm_ffn", "b_merge", "hgrn_norm", "ssm_conv_b", "ssm_norm", "gdn_norm")}
        lp["lb"] = lb[l][None, :]
        lp["ssm_conv_w"], lp["gdn_conv_w"] = sp["ssm_conv_w"][l], sp["gdn_conv_w"][l]
        for n in ("ssm_dt_bias", "ssm_a_log", "ssm_d"):
            lp[n] = _lane_pad(sp[n][l], DT_OFF)
        for n in ("gdn_dt_bias", "gdn_a_log"):
            lp[n] = _lane_pad(sp[n][l], GA_OFF)
        h, sv = _layer_fwd(h, md, wfull[l], lp)
        mds.append(md), lps.append(lp), svs.append(sv)
    loss, dh, d_nf = _final_loss(h, tgt, sp["norm_final"][None, :])
    grads = [None] * DEPTH
    if core is None:
        for l in reversed(range(DEPTH)):
            dh, grads[l] = _layer_bwd(dh, svs[l], mds[l], wfull[l], lps[l])
        return loss, dh, d_nf, grads
    names = [n for n, _, _, _ in BIG]
    landed, flying = [None] * DEPTH, None
    for l in reversed(range(DEPTH)):
        md = mds[l]
        if flying is not None:
            md = dict(md, g2=md["g2"] + sum(tok[0, 0] for _, _, _, tok in flying.values()))
        dh, grads[l] = _layer_bwd(dh, svs[l], md, wfull[l], lps[l])
        if flying is not None:
            landed[l + 1] = {n: _own_slab(_exchange_wait(f"exchange_wait_{n}_{l + 1}", *flying[n][:3], dh), sums[n], chip)
                             for n in names}
        sums = _pair_stage({n: grads[l][n].astype(BF)[None] for n in names}, core)
        if l > 0:
            flying = {n: _exchange_start(f"exchange_start_{n}_{l}", sums[n], dh) for n in names}
        else:
            landed[0] = {n: _own_slab(_chip_exchange("chip_exchange_" + n, sums[n]), sums[n], chip) for n in names}
    parts = {n: jnp.concatenate([landed[l][n] for l in range(DEPTH)], axis=1) for n in names}
    return loss, dh, d_nf, grads, _finish_reduce(parts, DEPTH, core)


def kernel(x, c, w_ada, b_ada, norm_mix, norm_ffn, w_in, b_merge, hgrn_lb_logits, hgrn_norm, ssm_conv_w, ssm_conv_b, ssm_dt_bias, ssm_a_log, ssm_d, ssm_norm, gdn_conv_w, gdn_dt_bias, gdn_a_log, gdn_norm, w_branch, w_out, w_ffn_in, w_ffn_out, norm_final, loss_target, m_w_ada, m_b_ada, m_norm_mix, m_norm_ffn, m_w_in, m_b_merge, m_hgrn_lb_logits, m_hgrn_norm, m_ssm_conv_w, m_ssm_conv_b, m_ssm_dt_bias, m_ssm_a_log, m_ssm_d, m_ssm_norm, m_gdn_conv_w, m_gdn_dt_bias, m_gdn_a_log, m_gdn_norm, m_w_branch, m_w_out, m_w_ffn_in, m_w_ffn_out, m_norm_final, v_w_ada, v_b_ada, v_norm_mix, v_norm_ffn, v_w_in, v_b_merge, v_hgrn_lb_logits, v_hgrn_norm, v_ssm_conv_w, v_ssm_conv_b, v_ssm_dt_bias, v_ssm_a_log, v_ssm_d, v_ssm_norm, v_gdn_conv_w, v_gdn_dt_bias, v_gdn_a_log, v_gdn_norm, v_w_branch, v_w_out, v_w_ffn_in, v_w_ffn_out, v_norm_final):
    w = dict(w_ada=w_ada, b_ada=b_ada, norm_mix=norm_mix, norm_ffn=norm_ffn, w_in=w_in, b_merge=b_merge,
             hgrn_lb_logits=hgrn_lb_logits, hgrn_norm=hgrn_norm, ssm_conv_w=ssm_conv_w, ssm_conv_b=ssm_conv_b,
             ssm_dt_bias=ssm_dt_bias, ssm_a_log=ssm_a_log, ssm_d=ssm_d, ssm_norm=ssm_norm, gdn_conv_w=gdn_conv_w,
             gdn_dt_bias=gdn_dt_bias, gdn_a_log=gdn_a_log, gdn_norm=gdn_norm, w_branch=w_branch, w_out=w_out,
             w_ffn_in=w_ffn_in, w_ffn_out=w_ffn_out, norm_final=norm_final)
    m = dict(w_ada=m_w_ada, b_ada=m_b_ada, norm_mix=m_norm_mix, norm_ffn=m_norm_ffn, w_in=m_w_in, b_merge=m_b_merge,
             hgrn_lb_logits=m_hgrn_lb_logits, hgrn_norm=m_hgrn_norm, ssm_conv_w=m_ssm_conv_w, ssm_conv_b=m_ssm_conv_b,
             ssm_dt_bias=m_ssm_dt_bias, ssm_a_log=m_ssm_a_log, ssm_d=m_ssm_d, ssm_norm=m_ssm_norm, gdn_conv_w=m_gdn_conv_w,
             gdn_dt_bias=m_gdn_dt_bias, gdn_a_log=m_gdn_a_log, gdn_norm=m_gdn_norm, w_branch=m_w_branch, w_out=m_w_out,
             w_ffn_in=m_w_ffn_in, w_ffn_out=m_w_ffn_out, norm_final=m_norm_final)
    v = dict(w_ada=v_w_ada, b_ada=v_b_ada, norm_mix=v_norm_mix, norm_ffn=v_norm_ffn, w_in=v_w_in, b_merge=v_b_merge,
             hgrn_lb_logits=v_hgrn_lb_logits, hgrn_norm=v_hgrn_norm, ssm_conv_w=v_ssm_conv_w, ssm_conv_b=v_ssm_conv_b,
             ssm_dt_bias=v_ssm_dt_bias, ssm_a_log=v_ssm_a_log, ssm_d=v_ssm_d, ssm_norm=v_ssm_norm, gdn_conv_w=v_gdn_conv_w,
             gdn_dt_bias=v_gdn_dt_bias, gdn_a_log=v_gdn_a_log, gdn_norm=v_gdn_norm, w_branch=v_w_branch, w_out=v_w_out,
             w_ffn_in=v_w_ffn_in, w_ffn_out=v_w_ffn_out, norm_final=v_norm_final)
    xi, yi, ci = _place()
    chip, me = 2 * xi + yi, 4 * xi + 2 * yi + ci
    seq = x.shape[1]

    conv_flat = jnp.concatenate([ssm_conv_w.reshape(-1), gdn_conv_w.reshape(-1)])
    n_conv = conv_flat.shape[0]
    first = _all_gather_small("gather_c_conv", _pad_rows(jnp.concatenate([c[0], conv_flat]), 16, D_MODEL))
    c_all = first[:, 0, :]
    conv_all = first[0::2].reshape(N_CHIP, -1)[:, D_MODEL:D_MODEL + n_conv]
    n_ssm = ssm_conv_w.size
    sp = dict(w)
    sp["ssm_conv_w"] = jnp.concatenate([conv_all[j, :n_ssm].reshape(ssm_conv_w.shape) for j in range(N_CHIP)], axis=2)
    sp["gdn_conv_w"] = jnp.concatenate([conv_all[j, n_ssm:].reshape(gdn_conv_w.shape) for j in range(N_CHIP)], axis=2)

    ada_cols = w_ada.shape[2]
    mod_part = _ada_fwd(c_all, w_ada, lax.dynamic_slice_in_dim(b_ada, chip * ada_cols, ada_cols, axis=1))
    mod_all = _all_gather_small("gather_mod", mod_part.reshape(DEPTH * 8, ada_cols))[0::2].reshape(N_CHIP, DEPTH, 8, ada_cols)
    mod = lax.dynamic_index_in_dim(mod_all, me, axis=2, keepdims=False).transpose(1, 0, 2).reshape(DEPTH, N_CHIP * ada_cols)
    lb = _lb_fwd(hgrn_lb_logits)

    names = [n for n, _, _, _ in BIG]

    def layer_weights(full):
        full = dict(full)
        for nm, part in zip(("768", "xbc", "qkv", "gate", "small"), _regroup_w_in(full["w_in"])):
            full["win_" + nm] = part
        full["win_all"] = jnp.concatenate([full["win_" + nm] for nm in ("768", "xbc", "qkv", "gate", "small")], axis=1)
        return full

    first = _gather_weights({n: w[n][0:1] for n in names}, chip)
    first = layer_weights({n: first[n][0] for n in names})
    started = {}
    settled = mod[0:1, 0:LANES]
    for n in names:
        settled = settled + first[n].reshape(-1, first[n].shape[-1])[0:1, 0:LANES].astype(F32)
    for n, _, _, (rows, cols) in BIG:
        mine = w[n][1:].astype(BF).reshape((DEPTH - 1, rows, cols))
        started[n] = _exchange_start("gather_start_" + n, lax.dynamic_slice_in_dim(mine, ci * (rows // 2), rows // 2, axis=1),
                                     settled, spread=True)
    rest = []

    def weights_of(layer, after):
        if layer == 0:
            return first
        if not rest:
            full = {}
            for n, shape, ax, _ in BIG:
                sems, my_half, landing, _ = started[n]
                landed = _exchange_wait("gather_wait_" + n, sems, my_half, landing, after, spread=True)
                landed = lax.dynamic_update_slice(landed, my_half[None], (chip, 0, 0, 0))
                theirs = _pair_swap("gather_share_" + n, landed)
                slabs = jnp.concatenate([jnp.where(ci == 0, landed, theirs), jnp.where(ci == 0, theirs, landed)], axis=2)
                full[n] = _join_shards(slabs, ax, shape)
            rest.extend(layer_weights({n: full[n][i] for n in names}) for i in range(DEPTH - 1))
        return rest[layer - 1]

    order = sum(tok[0, 0] for _, _, _, tok in started.values())
    loss8, d_x, d_nf, lg, grad = _device_step(x[0], loss_target[0], mod, lb, weights_of, sp, chip, ci, order)

    dmod = jnp.stack([jnp.concatenate([lg[l][n] for n in ("sh1", "sc1", "g1", "sh2", "sc2", "g2")], axis=1)[0] for l in range(DEPTH)])
    d_lb = jnp.stack([lg[l]["lb"][0] for l in range(DEPTH)])
    contrib = {
        "norm_mix": jnp.stack([lg[l]["norm_mix"][0] for l in range(DEPTH)]),
        "norm_ffn": jnp.stack([lg[l]["norm_ffn"][0] for l in range(DEPTH)]),
        "b_merge": jnp.stack([lg[l]["b_merge"][0] for l in range(DEPTH)]),
        "hgrn_lb_logits": _lb_bwd(hgrn_lb_logits, d_lb),
        "hgrn_norm": jnp.stack([lg[l]["hgrn_norm"][0] for l in range(DEPTH)]),
        "ssm_conv_w": jnp.stack([lg[l]["ssm_conv_w"] for l in range(DEPTH)]),
        "ssm_conv_b": jnp.stack([lg[l]["ssm_conv_b"][0] for l in range(DEPTH)]),
        "ssm_dt_bias": jnp.stack([lg[l]["ssm_dt_bias"][0, DT_OFF:DT_OFF + 12] for l in range(DEPTH)]),
        "ssm_a_log": jnp.stack([lg[l]["ssm_a_log"][0, DT_OFF:DT_OFF + 12] for l in range(DEPTH)]),
        "ssm_d": jnp.stack([lg[l]["ssm_d"][0, DT_OFF:DT_OFF + 12] for l in range(DEPTH)]),
        "ssm_norm": jnp.stack([lg[l]["ssm_norm"][0] for l in range(DEPTH)]),
        "gdn_conv_w": jnp.stack([lg[l]["gdn_conv_w"] for l in range(DEPTH)]),
        "gdn_dt_bias": jnp.stack([lg[l]["gdn_dt_bias"][0, GA_OFF:GA_OFF + 6] for l in range(DEPTH)]),
        "gdn_a_log": jnp.stack([lg[l]["gdn_a_log"][0, GA_OFF:GA_OFF + 6] for l in range(DEPTH)]),
        "gdn_norm": jnp.stack([lg[l]["gdn_norm"][0] for l in range(DEPTH)]),
        "norm_final": d_nf[0],
    }
    flat = jnp.concatenate([dmod.reshape(-1)] + [contrib[n].reshape(-1) for n in SMALL_REPL] + [loss8[0, 0:1]])
    small_all = _all_gather_small("gather_small_grads", _pad_rows(flat, SMALL_ROWS, D_MODEL))
    total = _sum_leading("small_grad_sum", small_all, F32).reshape(-1)
    n_mod = dmod.size
    grad["b_ada"] = total[:n_mod].reshape(b_ada.shape)
    off = n_mod
    full_small = {}
    for n in SMALL_REPL:
        full_small[n] = total[off:off + contrib[n].size].reshape(contrib[n].shape)
        off += contrib[n].size
    loss = total[off]
    for n in SMALL_REPL:
        if n in ("ssm_conv_w", "gdn_conv_w"):
            cols = w[n].shape[2]
            grad[n] = lax.dynamic_slice_in_dim(full_small[n], chip * cols, cols, axis=2)
        else:
            grad[n] = full_small[n]
    dmod_cols = lax.dynamic_slice_in_dim(small_all[:, :n_mod // D_MODEL, :].reshape(8, DEPTH, -1), chip * ada_cols, ada_cols, axis=2)
    grad["w_ada"] = _ada_bwd(c_all, dmod_cols.transpose(1, 0, 2))

    delta, new_m, new_v = {}, {}, {}
    big_names = ("w_ada",) + tuple(n for n, _, _, _ in BIG)
    for n in big_names:
        delta[n], new_m[n], new_v[n] = _adamw("adamw_" + n, w[n], grad[n], m[n], v[n])
    small_names = [n for n in WEIGHTS if n not in big_names]
    packs = [_pad_rows(jnp.concatenate([d[n].reshape(-1) for n in small_names]), 584, LANES) for d in (w, grad, m, v)]
    outs = _ew("adamw_small", _adamw_fn, packs, [F32, F32, F32])
    off = 0
    for n in small_names:
        for dst, o in zip((delta, new_m, new_v), outs):
            dst[n] = o.reshape(-1)[off:off + w[n].size].reshape(w[n].shape)
        off += w[n].size
    return (loss, d_x[None], *[grad[n] for n in WEIGHTS], *[delta[n] for n in WEIGHTS], *[new_m[n] for n in WEIGHTS],
            *[new_v[n] for n in WEIGHTS])
```
